```python
import math
import jax, jax.numpy as jnp
from jax import lax
import numpy as np

D_MODEL = 1024
BATCH = 16
SEQ = 2048
DEPTH = 4

N_MIXERS = 4
BLOCK = 128
EPS = 1e-6
SB_HEADS = 16
SB_HEAD_DIM = D_MODEL // SB_HEADS
SB_WIDTH = SB_HEADS * SB_HEAD_DIM
GM_WIDTH = 2 * D_MODEL
GM_CHUNK = 128
GM_GROUPS = 16
CV_WIDTH = 2 * D_MODEL
CV_KERNEL = 31
FX_HEADS = 16
FX_HEAD_DIM = D_MODEL // FX_HEADS
FX_WIDTH = FX_HEADS * FX_HEAD_DIM

kernel_name = "hybrid_sb_gmlp_conv_fox_trunk"


def _layer_counts():
    return [len(range(kind, DEPTH, N_MIXERS)) for kind in range(N_MIXERS)]


def rmsnorm(x, g):
    xf = x.astype(jnp.float32)
    y = xf * lax.rsqrt(jnp.mean(xf * xf, axis=-1, keepdims=True) + EPS)
    return (y * g.astype(jnp.float32)).astype(x.dtype)


def layernorm(x, g, b):
    xf = x.astype(jnp.float32)
    mu = jnp.mean(xf, axis=-1, keepdims=True)
    xc = xf - mu
    var = jnp.mean(xc * xc, axis=-1, keepdims=True)
    y = xc * lax.rsqrt(var + EPS) * g.astype(jnp.float32) + b.astype(jnp.float32)
    return y.astype(x.dtype)


def stick_breaking_attention(q, k, v):
    B, S, H, Dh = q.shape
    scale = 1.0 / math.sqrt(Dh)
    outs = []
    for t0 in range(0, S, BLOCK):
        kend = t0 + BLOCK
        z = jnp.einsum('bthd,bshd->bhts', q[:, t0:kend], k[:, :kend]).astype(jnp.float32) * scale
        t_idx = t0 + jnp.arange(BLOCK)[:, None]
        s_idx = jnp.arange(kend)[None, :]
        mask = s_idx < t_idx
        log_beta = jax.nn.log_sigmoid(z)
        log_rem = jnp.where(mask, jax.nn.log_sigmoid(-z), 0.0)
        after = lax.cumsum(log_rem, axis=3, reverse=True) - log_rem
        w = jnp.where(mask, jnp.exp(log_beta + after), 0.0)
        outs.append(jnp.einsum('bhts,bshd->bthd', w.astype(v.dtype), v[:, :kend]))
    return jnp.concatenate(outs, axis=1)


def forgetting_attention(q, k, v, log_f):
    B, S, H, Dh = q.shape
    scale = 1.0 / math.sqrt(Dh)
    cum = jnp.cumsum(log_f, axis=1).transpose(0, 2, 1)
    outs = []
    for t0 in range(0, S, BLOCK):
        kend = t0 + BLOCK
        logits = jnp.einsum('bthd,bshd->bhts', q[:, t0:kend], k[:, :kend]).astype(jnp.float32) * scale
        logits = logits + cum[:, :, t0:kend, None] - cum[:, :, None, :kend]
        t_idx = t0 + jnp.arange(BLOCK)[:, None]
        s_idx = jnp.arange(kend)[None, :]
        p = jax.nn.softmax(jnp.where(s_idx <= t_idx, logits, -jnp.inf), axis=-1)
        outs.append(jnp.einsum('bhts,bshd->bthd', p.astype(v.dtype), v[:, :kend]))
    return jnp.concatenate(outs, axis=1)


def mixer_stick_breaking(h, w_in, w_out):
    B, S, _ = h.shape
    proj = jnp.einsum('bsd,de->bse', h, w_in)
    q, k, v, g = jnp.split(proj, 4, axis=-1)
    hs = (B, S, SB_HEADS, SB_HEAD_DIM)
    o = stick_breaking_attention(q.reshape(hs), k.reshape(hs), v.reshape(hs)).reshape(B, S, SB_WIDTH)
    return jnp.einsum('bse,ed->bsd', o * jax.nn.silu(g), w_out)


def mixer_chunked_gmlp(h, w_in, v_ln_g, v_ln_b, w_s, b_s, w_out):
    B, S, _ = h.shape
    proj = jnp.einsum('bsd,de->bse', h, w_in)
    uv, g = proj[..., :2 * GM_WIDTH], proj[..., 2 * GM_WIDTH:]
    uv = jax.nn.gelu(uv)
    u, v = jnp.split(uv, 2, axis=-1)
    v = layernorm(v, v_ln_g, v_ln_b)
    tril = jnp.tril(jnp.ones((GM_CHUNK, GM_CHUNK), w_s.dtype))
    vc = v.reshape(B, S // GM_CHUNK, GM_CHUNK, GM_GROUPS, GM_WIDTH // GM_GROUPS)
    s = jnp.einsum('gts,bnsgc->bntgc', w_s * tril, vc) + b_s.T[None, None, :, :, None]
    o = u * s.reshape(B, S, GM_WIDTH)
    return jnp.einsum('bse,ed->bsd', o * jax.nn.silu(g), w_out)


def mixer_conformer_conv(h, w_in, conv_w, conv_b, ln_g, ln_b, w_out):
    proj = jnp.einsum('bsd,de->bse', h, w_in)
    a, b, g = jnp.split(proj, 3, axis=-1)
    y = a * jax.nn.sigmoid(b)
    y = lax.conv_general_dilated(
        y, conv_w[:, None, :], window_strides=(1,), padding=[(CV_KERNEL - 1, 0)],
        dimension_numbers=('NWC', 'WIO', 'NWC'), feature_group_count=CV_WIDTH) + conv_b
    y = jax.nn.silu(layernorm(y, ln_g, ln_b))
    return jnp.einsum('bse,ed->bsd', y * jax.nn.silu(g), w_out)


def mixer_forgetting(h, w_in, b_f, w_out):
    B, S, _ = h.shape
    proj = jnp.einsum('bsd,de->bse', h, w_in)
    q = proj[..., :FX_WIDTH]
    k = proj[..., FX_WIDTH:2 * FX_WIDTH]
    v = proj[..., 2 * FX_WIDTH:3 * FX_WIDTH]
    g = proj[..., 3 * FX_WIDTH:4 * FX_WIDTH]
    f_logit = proj[..., 4 * FX_WIDTH:].astype(jnp.float32) + b_f.astype(jnp.float32)
    log_f = jax.nn.log_sigmoid(f_logit)
    hs = (B, S, FX_HEADS, FX_HEAD_DIM)
    o = forgetting_attention(q.reshape(hs), k.reshape(hs), v.reshape(hs), log_f).reshape(B, S, FX_WIDTH)
    return jnp.einsum('bse,ed->bsd', o * jax.nn.silu(g), w_out)


def _fwd_setup_inputs(seed: int = 0) -> dict:
    key = jax.random.key(seed)
    ks = iter(jax.random.split(key, 32))
    nA, nB, nC, nD = _layer_counts()
    f32 = jnp.float32

    def nrm(shape, scale):
        return jax.random.normal(next(ks), shape, f32) * scale

    def gain(n, width):
        return 1.0 + nrm((n, width), 0.02)

    d_in = D_MODEL ** -0.5
    return {
        "x": nrm((BATCH, SEQ, D_MODEL), 1.0),
        "a_norm": gain(nA, D_MODEL),
        "a_w_in": nrm((nA, D_MODEL, 4 * SB_WIDTH), d_in),
        "a_w_out": nrm((nA, SB_WIDTH, D_MODEL), SB_WIDTH ** -0.5),
        "b_norm": gain(nB, D_MODEL),
        "b_w_in": nrm((nB, D_MODEL, 3 * GM_WIDTH), d_in),
        "b_v_ln_g": gain(nB, GM_WIDTH),
        "b_v_ln_b": nrm((nB, GM_WIDTH), 0.02),
        "b_w_s": nrm((nB, GM_GROUPS, GM_CHUNK, GM_CHUNK), GM_CHUNK ** -0.5),
        "b_b_s": 1.0 + nrm((nB, GM_GROUPS, GM_CHUNK), 0.1),
        "b_w_out": nrm((nB, GM_WIDTH, D_MODEL), GM_WIDTH ** -0.5),
        "c_norm": gain(nC, D_MODEL),
        "c_w_in": nrm((nC, D_MODEL, 3 * CV_WIDTH), d_in),
        "c_conv_w": nrm((nC, CV_KERNEL, CV_WIDTH), CV_KERNEL ** -0.5),
        "c_conv_b": nrm((nC, CV_WIDTH), 0.02),
        "c_ln_g": gain(nC, CV_WIDTH),
        "c_ln_b": nrm((nC, CV_WIDTH), 0.02),
        "c_w_out": nrm((nC, CV_WIDTH, D_MODEL), CV_WIDTH ** -0.5),
        "d_norm": gain(nD, D_MODEL),
        "d_w_in": nrm((nD, D_MODEL, 4 * FX_WIDTH + FX_HEADS), d_in),
        "d_b_f": 3.0 + nrm((nD, FX_HEADS), 0.5),
        "d_w_out": nrm((nD, FX_WIDTH, D_MODEL), FX_WIDTH ** -0.5),
        "final_norm": 1.0 + nrm((D_MODEL,), 0.02),
    }


def _fwd_reference(x, a_norm, a_w_in, a_w_out, b_norm, b_w_in, b_v_ln_g, b_v_ln_b, b_w_s, b_b_s, b_w_out,
              c_norm, c_w_in, c_conv_w, c_conv_b, c_ln_g, c_ln_b, c_w_out,
              d_norm, d_w_in, d_b_f, d_w_out, final_norm):
    for i in range(DEPTH):
        kind, j = i % N_MIXERS, i // N_MIXERS
        if kind == 0:
            x = x + mixer_stick_breaking(rmsnorm(x, a_norm[j]), a_w_in[j], a_w_out[j])
        elif kind == 1:
            x = x + mixer_chunked_gmlp(rmsnorm(x, b_norm[j]), b_w_in[j], b_v_ln_g[j], b_v_ln_b[j],
                                       b_w_s[j], b_b_s[j], b_w_out[j])
        elif kind == 2:
            x = x + mixer_conformer_conv(rmsnorm(x, c_norm[j]), c_w_in[j], c_conv_w[j], c_conv_b[j],
                                         c_ln_g[j], c_ln_b[j], c_w_out[j])
        else:
            x = x + mixer_forgetting(rmsnorm(x, d_norm[j]), d_w_in[j], d_b_f[j], d_w_out[j])
    return rmsnorm(x, final_norm)


import jax as _jax
import jax.numpy as _jnp

TWIN_FORMAT = 'train_step'
FWD_PARAMS = ['x', 'a_norm', 'a_w_in', 'a_w_out', 'b_norm', 'b_w_in', 'b_v_ln_g', 'b_v_ln_b', 'b_w_s', 'b_b_s', 'b_w_out', 'c_norm', 'c_w_in', 'c_conv_w', 'c_conv_b', 'c_ln_g', 'c_ln_b', 'c_w_out', 'd_norm', 'd_w_in', 'd_b_f', 'd_w_out', 'final_norm']
TWIN_WEIGHTS = ['a_norm', 'a_w_in', 'a_w_out', 'b_norm', 'b_w_in', 'b_v_ln_g', 'b_v_ln_b', 'b_w_s', 'b_b_s', 'b_w_out', 'c_norm', 'c_w_in', 'c_conv_w', 'c_conv_b', 'c_ln_g', 'c_ln_b', 'c_w_out', 'd_norm', 'd_w_in', 'd_b_f', 'd_w_out', 'final_norm']
TWIN_DIFF_INPUT = 'x'
TWIN_INPUTS = ['x', 'a_norm', 'a_w_in', 'a_w_out', 'b_norm', 'b_w_in', 'b_v_ln_g', 'b_v_ln_b', 'b_w_s', 'b_b_s', 'b_w_out', 'c_norm', 'c_w_in', 'c_conv_w', 'c_conv_b', 'c_ln_g', 'c_ln_b', 'c_w_out', 'd_norm', 'd_w_in', 'd_b_f', 'd_w_out', 'final_norm', 'loss_target', 'm_a_norm', 'm_a_w_in', 'm_a_w_out', 'm_b_norm', 'm_b_w_in', 'm_b_v_ln_g', 'm_b_v_ln_b', 'm_b_w_s', 'm_b_b_s', 'm_b_w_out', 'm_c_norm', 'm_c_w_in', 'm_c_conv_w', 'm_c_conv_b', 'm_c_ln_g', 'm_c_ln_b', 'm_c_w_out', 'm_d_norm', 'm_d_w_in', 'm_d_b_f', 'm_d_w_out', 'm_final_norm', 'v_a_norm', 'v_a_w_in', 'v_a_w_out', 'v_b_norm', 'v_b_w_in', 'v_b_v_ln_g', 'v_b_v_ln_b', 'v_b_w_s', 'v_b_b_s', 'v_b_w_out', 'v_c_norm', 'v_c_w_in', 'v_c_conv_w', 'v_c_conv_b', 'v_c_ln_g', 'v_c_ln_b', 'v_c_w_out', 'v_d_norm', 'v_d_w_in', 'v_d_b_f', 'v_d_w_out', 'v_final_norm']
TWIN_OUTPUTS = ['loss', 'grad_x', 'grad_a_norm', 'grad_a_w_in', 'grad_a_w_out', 'grad_b_norm', 'grad_b_w_in', 'grad_b_v_ln_g', 'grad_b_v_ln_b', 'grad_b_w_s', 'grad_b_b_s', 'grad_b_w_out', 'grad_c_norm', 'grad_c_w_in', 'grad_c_conv_w', 'grad_c_conv_b', 'grad_c_ln_g', 'grad_c_ln_b', 'grad_c_w_out', 'grad_d_norm', 'grad_d_w_in', 'grad_d_b_f', 'grad_d_w_out', 'grad_final_norm', 'delta_a_norm', 'delta_a_w_in', 'delta_a_w_out', 'delta_b_norm', 'delta_b_w_in', 'delta_b_v_ln_g', 'delta_b_v_ln_b', 'delta_b_w_s', 'delta_b_b_s', 'delta_b_w_out', 'delta_c_norm', 'delta_c_w_in', 'delta_c_conv_w', 'delta_c_conv_b', 'delta_c_ln_g', 'delta_c_ln_b', 'delta_c_w_out', 'delta_d_norm', 'delta_d_w_in', 'delta_d_b_f', 'delta_d_w_out', 'delta_final_norm', 'new_m_a_norm', 'new_m_a_w_in', 'new_m_a_w_out', 'new_m_b_norm', 'new_m_b_w_in', 'new_m_b_v_ln_g', 'new_m_b_v_ln_b', 'new_m_b_w_s', 'new_m_b_b_s', 'new_m_b_w_out', 'new_m_c_norm', 'new_m_c_w_in', 'new_m_c_conv_w', 'new_m_c_conv_b', 'new_m_c_ln_g', 'new_m_c_ln_b', 'new_m_c_w_out', 'new_m_d_norm', 'new_m_d_w_in', 'new_m_d_b_f', 'new_m_d_w_out', 'new_m_final_norm', 'new_v_a_norm', 'new_v_a_w_in', 'new_v_a_w_out', 'new_v_b_norm', 'new_v_b_w_in', 'new_v_b_v_ln_g', 'new_v_b_v_ln_b', 'new_v_b_w_s', 'new_v_b_b_s', 'new_v_b_w_out', 'new_v_c_norm', 'new_v_c_w_in', 'new_v_c_conv_w', 'new_v_c_conv_b', 'new_v_c_ln_g', 'new_v_c_ln_b', 'new_v_c_w_out', 'new_v_d_norm', 'new_v_d_w_in', 'new_v_d_b_f', 'new_v_d_w_out', 'new_v_final_norm']
TWIN_LEAF_KINDS = {'loss': 'loss', 'grad_x': 'grad_x', 'grad_a_norm': 'grad_w', 'grad_a_w_in': 'grad_w', 'grad_a_w_out': 'grad_w', 'grad_b_norm': 'grad_w', 'grad_b_w_in': 'grad_w', 'grad_b_v_ln_g': 'grad_w', 'grad_b_v_ln_b': 'grad_w', 'grad_b_w_s': 'grad_w', 'grad_b_b_s': 'grad_w', 'grad_b_w_out': 'grad_w', 'grad_c_norm': 'grad_w', 'grad_c_w_in': 'grad_w', 'grad_c_conv_w': 'grad_w', 'grad_c_conv_b': 'grad_w', 'grad_c_ln_g': 'grad_w', 'grad_c_ln_b': 'grad_w', 'grad_c_w_out': 'grad_w', 'grad_d_norm': 'grad_w', 'grad_d_w_in': 'grad_w', 'grad_d_b_f': 'grad_w', 'grad_d_w_out': 'grad_w', 'grad_final_norm': 'grad_w', 'delta_a_norm': 'delta_w', 'delta_a_w_in': 'delta_w', 'delta_a_w_out': 'delta_w', 'delta_b_norm': 'delta_w', 'delta_b_w_in': 'delta_w', 'delta_b_v_ln_g': 'delta_w', 'delta_b_v_ln_b': 'delta_w', 'delta_b_w_s': 'delta_w', 'delta_b_b_s': 'delta_w', 'delta_b_w_out': 'delta_w', 'delta_c_norm': 'delta_w', 'delta_c_w_in': 'delta_w', 'delta_c_conv_w': 'delta_w', 'delta_c_conv_b': 'delta_w', 'delta_c_ln_g': 'delta_w', 'delta_c_ln_b': 'delta_w', 'delta_c_w_out': 'delta_w', 'delta_d_norm': 'delta_w', 'delta_d_w_in': 'delta_w', 'delta_d_b_f': 'delta_w', 'delta_d_w_out': 'delta_w', 'delta_final_norm': 'delta_w', 'new_m_a_norm': 'new_m', 'new_m_a_w_in': 'new_m', 'new_m_a_w_out': 'new_m', 'new_m_b_norm': 'new_m', 'new_m_b_w_in': 'new_m', 'new_m_b_v_ln_g': 'new_m', 'new_m_b_v_ln_b': 'new_m', 'new_m_b_w_s': 'new_m', 'new_m_b_b_s': 'new_m', 'new_m_b_w_out': 'new_m', 'new_m_c_norm': 'new_m', 'new_m_c_w_in': 'new_m', 'new_m_c_conv_w': 'new_m', 'new_m_c_conv_b': 'new_m', 'new_m_c_ln_g': 'new_m', 'new_m_c_ln_b': 'new_m', 'new_m_c_w_out': 'new_m', 'new_m_d_norm': 'new_m', 'new_m_d_w_in': 'new_m', 'new_m_d_b_f': 'new_m', 'new_m_d_w_out': 'new_m', 'new_m_final_norm': 'new_m', 'new_v_a_norm': 'new_v', 'new_v_a_w_in': 'new_v', 'new_v_a_w_out': 'new_v', 'new_v_b_norm': 'new_v', 'new_v_b_w_in': 'new_v', 'new_v_b_v_ln_g': 'new_v', 'new_v_b_v_ln_b': 'new_v', 'new_v_b_w_s': 'new_v', 'new_v_b_b_s': 'new_v', 'new_v_b_w_out': 'new_v', 'new_v_c_norm': 'new_v', 'new_v_c_w_in': 'new_v', 'new_v_c_conv_w': 'new_v', 'new_v_c_conv_b': 'new_v', 'new_v_c_ln_g': 'new_v', 'new_v_c_ln_b': 'new_v', 'new_v_c_w_out': 'new_v', 'new_v_d_norm': 'new_v', 'new_v_d_w_in': 'new_v', 'new_v_d_b_f': 'new_v', 'new_v_d_w_out': 'new_v', 'new_v_final_norm': 'new_v'}


def _forward(args):
    return _fwd_reference(*[args[k] for k in FWD_PARAMS])


def _output_shape():
    out = _jax.eval_shape(lambda: _forward(_fwd_setup_inputs(0)))
    return out.shape, out.dtype

N_MICROBATCH = 1
ADAM_LR = 0.001
ADAM_B1 = 0.9
ADAM_B2 = 0.999
ADAM_EPS = 1e-08
ADAM_WD = 0.01
ADAM_STEP = 10
PER_EXAMPLE_BATCH_AXIS = {'x': 0, 'loss_target': 0}
SHARED_INPUTS = []
_WEIGHT_DTYPES = {'a_norm': _jnp.float32, 'a_w_in': _jnp.float32, 'a_w_out': _jnp.float32, 'b_norm': _jnp.float32, 'b_w_in': _jnp.float32, 'b_v_ln_g': _jnp.float32, 'b_v_ln_b': _jnp.float32, 'b_w_s': _jnp.float32, 'b_b_s': _jnp.float32, 'b_w_out': _jnp.float32, 'c_norm': _jnp.float32, 'c_w_in': _jnp.float32, 'c_conv_w': _jnp.float32, 'c_conv_b': _jnp.float32, 'c_ln_g': _jnp.float32, 'c_ln_b': _jnp.float32, 'c_w_out': _jnp.float32, 'd_norm': _jnp.float32, 'd_w_in': _jnp.float32, 'd_b_f': _jnp.float32, 'd_w_out': _jnp.float32, 'final_norm': _jnp.float32}
MOMENT_SCALE = {'a_norm': 1.293587e-01, 'a_w_in': 6.053360e-02, 'a_w_out': 7.645215e-02, 'b_norm': 1.259810e-01, 'b_w_in': 5.071465e-02, 'b_v_ln_g': 3.175237e-02, 'b_v_ln_b': 3.196231e-02, 'b_w_s': 3.129604e-02, 'b_b_s': 4.462895e-02, 'b_w_out': 7.755809e-02, 'c_norm': 8.150715e-02, 'c_w_in': 3.289254e-02, 'c_conv_w': 3.833627e-02, 'c_conv_b': 8.440535e-02, 'c_ln_g': 4.781584e-02, 'c_ln_b': 3.833015e-02, 'c_w_out': 5.329738e-02, 'd_norm': 5.075633e-02, 'd_w_in': 2.454173e-02, 'd_b_f': 1.023826e-01, 'd_w_out': 2.658354e-02, 'final_norm': 3.196142e+01}


def _to_microbatches(a, axis):
    t = _jnp.moveaxis(a, axis, 0)
    t = t.reshape((N_MICROBATCH, t.shape[0] // N_MICROBATCH) + t.shape[1:])
    return _jnp.moveaxis(t, 1, axis + 1)


def setup_inputs(seed: int = 0) -> dict:
    inp = _fwd_setup_inputs(seed)
    key = _jax.random.fold_in(_jax.random.key(seed), 7919)
    shape, _ = _output_shape()
    out = dict(inp)
    out["loss_target"] = _jax.random.normal(_jax.random.fold_in(key, 0), shape, _jnp.float32)
    for i, name in enumerate(TWIN_WEIGHTS):
        w = inp[name].astype(_jnp.float32)
        if MOMENT_SCALE is None:
            s = _jnp.sqrt(_jnp.mean(_jnp.square(w)) + 1e-30)
        else:
            s = MOMENT_SCALE[name]
        km, kv = _jax.random.split(_jax.random.fold_in(key, i + 1))
        out[name] = w
        out["m_" + name] = s * _jax.random.normal(km, w.shape, _jnp.float32)
        out["v_" + name] = (s * s) * _jax.random.uniform(kv, w.shape, _jnp.float32, 0.5, 1.5)
    if N_MICROBATCH > 1:
        for name, axis in PER_EXAMPLE_BATCH_AXIS.items():
            out[name] = _to_microbatches(out[name], axis)
    return {'x': out['x'], 'a_norm': out['a_norm'], 'a_w_in': out['a_w_in'], 'a_w_out': out['a_w_out'], 'b_norm': out['b_norm'], 'b_w_in': out['b_w_in'], 'b_v_ln_g': out['b_v_ln_g'], 'b_v_ln_b': out['b_v_ln_b'], 'b_w_s': out['b_w_s'], 'b_b_s': out['b_b_s'], 'b_w_out': out['b_w_out'], 'c_norm': out['c_norm'], 'c_w_in': out['c_w_in'], 'c_conv_w': out['c_conv_w'], 'c_conv_b': out['c_conv_b'], 'c_ln_g': out['c_ln_g'], 'c_ln_b': out['c_ln_b'], 'c_w_out': out['c_w_out'], 'd_norm': out['d_norm'], 'd_w_in': out['d_w_in'], 'd_b_f': out['d_b_f'], 'd_w_out': out['d_w_out'], 'final_norm': out['final_norm'], 'loss_target': out['loss_target'], 'm_a_norm': out['m_a_norm'], 'm_a_w_in': out['m_a_w_in'], 'm_a_w_out': out['m_a_w_out'], 'm_b_norm': out['m_b_norm'], 'm_b_w_in': out['m_b_w_in'], 'm_b_v_ln_g': out['m_b_v_ln_g'], 'm_b_v_ln_b': out['m_b_v_ln_b'], 'm_b_w_s': out['m_b_w_s'], 'm_b_b_s': out['m_b_b_s'], 'm_b_w_out': out['m_b_w_out'], 'm_c_norm': out['m_c_norm'], 'm_c_w_in': out['m_c_w_in'], 'm_c_conv_w': out['m_c_conv_w'], 'm_c_conv_b': out['m_c_conv_b'], 'm_c_ln_g': out['m_c_ln_g'], 'm_c_ln_b': out['m_c_ln_b'], 'm_c_w_out': out['m_c_w_out'], 'm_d_norm': out['m_d_norm'], 'm_d_w_in': out['m_d_w_in'], 'm_d_b_f': out['m_d_b_f'], 'm_d_w_out': out['m_d_w_out'], 'm_final_norm': out['m_final_norm'], 'v_a_norm': out['v_a_norm'], 'v_a_w_in': out['v_a_w_in'], 'v_a_w_out': out['v_a_w_out'], 'v_b_norm': out['v_b_norm'], 'v_b_w_in': out['v_b_w_in'], 'v_b_v_ln_g': out['v_b_v_ln_g'], 'v_b_v_ln_b': out['v_b_v_ln_b'], 'v_b_w_s': out['v_b_w_s'], 'v_b_b_s': out['v_b_b_s'], 'v_b_w_out': out['v_b_w_out'], 'v_c_norm': out['v_c_norm'], 'v_c_w_in': out['v_c_w_in'], 'v_c_conv_w': out['v_c_conv_w'], 'v_c_conv_b': out['v_c_conv_b'], 'v_c_ln_g': out['v_c_ln_g'], 'v_c_ln_b': out['v_c_ln_b'], 'v_c_w_out': out['v_c_w_out'], 'v_d_norm': out['v_d_norm'], 'v_d_w_in': out['v_d_w_in'], 'v_d_b_f': out['v_d_b_f'], 'v_d_w_out': out['v_d_w_out'], 'v_final_norm': out['v_final_norm']}


def _loss(weights, diff, rest, loss_target):
    with _jax.named_scope("forward"):
        args = {**rest, TWIN_DIFF_INPUT: diff, **{k: w.astype(_WEIGHT_DTYPES[k]) for k, w in weights.items()}}
        y = _forward(args)
    with _jax.named_scope("loss_head"):
        err = _jnp.square(y.astype(_jnp.float32) - loss_target)
        return 0.5 * _jnp.sum(_jnp.mean(err, axis=-1)) if err.ndim else 0.5 * err


def _adamw(w, g, m, v):
    m = ADAM_B1 * m + (1.0 - ADAM_B1) * g
    v = ADAM_B2 * v + (1.0 - ADAM_B2) * _jnp.square(g)
    m_hat = m / (1.0 - ADAM_B1 ** ADAM_STEP)
    v_hat = v / (1.0 - ADAM_B2 ** ADAM_STEP)
    delta = -ADAM_LR * (m_hat / (_jnp.sqrt(v_hat) + ADAM_EPS) + ADAM_WD * w)
    return delta, m, v


def reference(x, a_norm, a_w_in, a_w_out, b_norm, b_w_in, b_v_ln_g, b_v_ln_b, b_w_s, b_b_s, b_w_out, c_norm, c_w_in, c_conv_w, c_conv_b, c_ln_g, c_ln_b, c_w_out, d_norm, d_w_in, d_b_f, d_w_out, final_norm, loss_target, m_a_norm, m_a_w_in, m_a_w_out, m_b_norm, m_b_w_in, m_b_v_ln_g, m_b_v_ln_b, m_b_w_s, m_b_b_s, m_b_w_out, m_c_norm, m_c_w_in, m_c_conv_w, m_c_conv_b, m_c_ln_g, m_c_ln_b, m_c_w_out, m_d_norm, m_d_w_in, m_d_b_f, m_d_w_out, m_final_norm, v_a_norm, v_a_w_in, v_a_w_out, v_b_norm, v_b_w_in, v_b_v_ln_g, v_b_v_ln_b, v_b_w_s, v_b_b_s, v_b_w_out, v_c_norm, v_c_w_in, v_c_conv_w, v_c_conv_b, v_c_ln_g, v_c_ln_b, v_c_w_out, v_d_norm, v_d_w_in, v_d_b_f, v_d_w_out, v_final_norm):
    given = dict(x=x, a_norm=a_norm, a_w_in=a_w_in, a_w_out=a_w_out, b_norm=b_norm, b_w_in=b_w_in, b_v_ln_g=b_v_ln_g, b_v_ln_b=b_v_ln_b, b_w_s=b_w_s, b_b_s=b_b_s, b_w_out=b_w_out, c_norm=c_norm, c_w_in=c_w_in, c_conv_w=c_conv_w, c_conv_b=c_conv_b, c_ln_g=c_ln_g, c_ln_b=c_ln_b, c_w_out=c_w_out, d_norm=d_norm, d_w_in=d_w_in, d_b_f=d_b_f, d_w_out=d_w_out, final_norm=final_norm, loss_target=loss_target, m_a_norm=m_a_norm, m_a_w_in=m_a_w_in, m_a_w_out=m_a_w_out, m_b_norm=m_b_norm, m_b_w_in=m_b_w_in, m_b_v_ln_g=m_b_v_ln_g, m_b_v_ln_b=m_b_v_ln_b, m_b_w_s=m_b_w_s, m_b_b_s=m_b_b_s, m_b_w_out=m_b_w_out, m_c_norm=m_c_norm, m_c_w_in=m_c_w_in, m_c_conv_w=m_c_conv_w, m_c_conv_b=m_c_conv_b, m_c_ln_g=m_c_ln_g, m_c_ln_b=m_c_ln_b, m_c_w_out=m_c_w_out, m_d_norm=m_d_norm, m_d_w_in=m_d_w_in, m_d_b_f=m_d_b_f, m_d_w_out=m_d_w_out, m_final_norm=m_final_norm, v_a_norm=v_a_norm, v_a_w_in=v_a_w_in, v_a_w_out=v_a_w_out, v_b_norm=v_b_norm, v_b_w_in=v_b_w_in, v_b_v_ln_g=v_b_v_ln_g, v_b_v_ln_b=v_b_v_ln_b, v_b_w_s=v_b_w_s, v_b_b_s=v_b_b_s, v_b_w_out=v_b_w_out, v_c_norm=v_c_norm, v_c_w_in=v_c_w_in, v_c_conv_w=v_c_conv_w, v_c_conv_b=v_c_conv_b, v_c_ln_g=v_c_ln_g, v_c_ln_b=v_c_ln_b, v_c_w_out=v_c_w_out, v_d_norm=v_d_norm, v_d_w_in=v_d_w_in, v_d_b_f=v_d_b_f, v_d_w_out=v_d_w_out, v_final_norm=v_final_norm)
    weights = {n: given[n] for n in TWIN_WEIGHTS}
    shared = {n: given[n] for n in SHARED_INPUTS}
    per_example = {n: given[n] for n in ['x']}
    grad_fn = _jax.value_and_grad(_loss, argnums=(0, 1))

    def one_microbatch(ex, loss_target):
        ex = dict(ex)
        diff = ex.pop(TWIN_DIFF_INPUT)
        return grad_fn(weights, diff, {**shared, **ex}, loss_target)

    if N_MICROBATCH == 1:
        loss, (grad_w, grad_x) = one_microbatch(per_example, given["loss_target"])
    else:
        def body(carry, xs):
            loss_sum, grad_sum = carry
            l_k, (gw_k, gx_k) = one_microbatch(xs[0], xs[1])
            with _jax.named_scope("update"):
                return (loss_sum + l_k, _jax.tree.map(_jnp.add, grad_sum, gw_k)), gx_k

        init = (_jnp.zeros((), _jnp.float32), _jax.tree.map(_jnp.zeros_like, weights))
        (loss, grad_w), grad_x = _jax.lax.scan(body, init, (per_example, given["loss_target"]))
    with _jax.named_scope("update"):
        delta_w, new_m, new_v = {}, {}, {}
        for n in TWIN_WEIGHTS:
            delta_w[n], new_m[n], new_v[n] = _adamw(weights[n], grad_w[n], given["m_" + n], given["v_" + n])
    return (loss, grad_x, *[grad_w[n] for n in TWIN_WEIGHTS], *[delta_w[n] for n in TWIN_WEIGHTS],
            *[new_m[n] for n in TWIN_WEIGHTS], *[new_v[n] for n in TWIN_WEIGHTS])
```

```python
import functools
import math

import jax
import jax.numpy as jnp
from jax import lax
from jax.experimental import pallas as pl
from jax.experimental.pallas import tpu as pltpu

F32, BF16 = jnp.float32, jnp.bfloat16
MESH = pl.DeviceIdType.MESH

D_MODEL = 1024
HEADS = 16
HEAD_DIM = 64
BLK = 128
PAIRS = HEADS // 2
GM_W = 2048
GM_G = 16
CV_W = 2048
CV_K = 31
HALO = 32
EPS = 1e-6
N_CHIPS = 4
PACK_C = 1024
ADAM_LR, ADAM_B1, ADAM_B2, ADAM_EPS, ADAM_WD, ADAM_STEP = 0.001, 0.9, 0.999, 1e-08, 0.01, 10

_NT = (((1,), (1,)), ((), ()))
_TN = (((0,), (0,)), ((), ()))
_NN = (((1,), (0,)), ((), ()))


def _dot(a, b, dims=_NN):
    return lax.dot_general(a, b, dims, preferred_element_type=F32)


def _split3(x):
    hi = x.astype(BF16)
    r = x - hi.astype(F32)
    mid = r.astype(BF16)
    lo = (r - mid.astype(F32)).astype(BF16)
    return hi, mid, lo


def _dot3_right(x, m):
    hi, mid, lo = _split3(x)
    return _dot(hi, m) + _dot(mid, m) + _dot(lo, m)


def _dot3_left(m, x):
    hi, mid, lo = _split3(x)
    return _dot(m, hi) + _dot(m, mid) + _dot(m, lo)


def _sigmoid(x):
    return 1.0 / (1.0 + jnp.exp(-x))


def _silu(x):
    return x * _sigmoid(x)


def _dsilu(x):
    s = _sigmoid(x)
    return s * (1.0 + x * (1.0 - s))


_GELU_C = math.sqrt(2.0 / math.pi)
_GELU_A = 0.044715


def _gelu(x):
    return 0.5 * x * (1.0 + jnp.tanh(_GELU_C * (x + _GELU_A * x * x * x)))


def _dgelu(x):
    t = jnp.tanh(_GELU_C * (x + _GELU_A * x * x * x))
    return 0.5 * (1.0 + t) + 0.5 * x * (1.0 - t * t) * _GELU_C * (1.0 + 3.0 * _GELU_A * x * x)


def _log_sigmoid(x):
    return jnp.minimum(x, 0.0) - jnp.log(1.0 + jnp.exp(-jnp.abs(x)))


def _rms_fwd(x, g):
    r = lax.rsqrt(jnp.mean(x * x, axis=-1, keepdims=True) + EPS)
    return x * r * g


def _rms_bwd(dy, x, g):
    r = lax.rsqrt(jnp.mean(x * x, axis=-1, keepdims=True) + EPS)
    xh = x * r
    dxh = dy * g
    dx = r * (dxh - xh * jnp.mean(dxh * xh, axis=-1, keepdims=True))
    return dx, dy * xh


def _ln_stats(x):
    mu = jnp.mean(x, axis=-1, keepdims=True)
    xc = x - mu
    r = lax.rsqrt(jnp.mean(xc * xc, axis=-1, keepdims=True) + EPS)
    return xc * r, r


def _ln_bwd(dy, xh, r, g):
    dxh = dy * g
    return r * (dxh - jnp.mean(dxh, axis=-1, keepdims=True) - xh * jnp.mean(dxh * xh, axis=-1, keepdims=True))


def _colsum(x):
    return jnp.sum(x, axis=0, keepdims=True)


def _tile(n, want):
    for t in range(min(n, want), 7, -1):
        if n % t == 0 and t % 8 == 0:
            return t
    return n


def _matmul(a, b, *, name, mode="nn", residual=None, out_dtype=F32):
    if mode == "nn":
        (m, k), (_, n) = a.shape, b.shape
    elif mode == "nt":
        (m, k), (n, _) = a.shape, b.shape
    else:
        (k, m), (_, n) = a.shape, b.shape
    tm, tn = _tile(m, 512), _tile(n, 512)
    tk = _tile(k, 1024 if mode != "tn" else 512)
    nk = k // tk
    if mode == "nn":
        a_spec = pl.BlockSpec((tm, tk), lambda i, j, kk: (i, kk))
        b_spec = pl.BlockSpec((tk, tn), lambda i, j, kk: (kk, j))
        dims = _NN
    elif mode == "nt":
        a_spec = pl.BlockSpec((tm, tk), lambda i, j, kk: (i, kk))
        b_spec = pl.BlockSpec((tn, tk), lambda i, j, kk: (j, kk))
        dims = _NT
    else:
        a_spec = pl.BlockSpec((tk, tm), lambda i, j, kk: (kk, i))
        b_spec = pl.BlockSpec((tk, tn), lambda i, j, kk: (kk, j))
        dims = _TN
    o_spec = pl.BlockSpec((tm, tn), lambda i, j, kk: (i, j))
    has_res = residual is not None

    def body(a_ref, b_ref, *rest):
        if has_res:
            r_ref, o_ref, acc_ref = rest
        else:
            o_ref, acc_ref = rest
        kk = pl.program_id(2)
        part = _dot(a_ref[...].astype(BF16), b_ref[...].astype(BF16), dims)

        @pl.when(kk == 0)
        def _():
            acc_ref[...] = part

        @pl.when(kk > 0)
        def _():
            acc_ref[...] += part

        @pl.when(kk == nk - 1)
        def _():
            out = acc_ref[...]
            if has_res:
                out = out + r_ref[...]
            o_ref[...] = out.astype(out_dtype)

    return pl.pallas_call(
        body, name=name, grid=(m // tm, n // tn, nk),
        in_specs=[a_spec, b_spec] + ([o_spec] if has_res else []),
        out_specs=o_spec, out_shape=jax.ShapeDtypeStruct((m, n), out_dtype),
        scratch_shapes=[pltpu.VMEM((tm, tn), F32)],
        compiler_params=pltpu.CompilerParams(dimension_semantics=("parallel", "parallel", "arbitrary")),
    )(a, b, *([residual] if has_res else []))


def _rows(fn, *, name, steps, ins, outs, accs=(), scratch=()):
    ni, no, na = len(ins), len(outs), len(accs)

    def body(*refs):
        in_refs, out_refs = refs[:ni], refs[ni:ni + no]
        acc_refs, scr = refs[ni + no:ni + no + na], refs[ni + no + na:]
        i = pl.program_id(0)

        @pl.when(i == 0)
        def _():
            for r in acc_refs:
                r[...] = jnp.zeros(r.shape, r.dtype)

        fn(i, in_refs, out_refs, acc_refs, scr)

    def full(shape):
        nd = len(shape)
        return pl.BlockSpec(tuple(shape), lambda i: (0,) * nd)

    res = pl.pallas_call(
        body, name=name, grid=(steps,),
        in_specs=[pl.BlockSpec(bs, im) for _, bs, im in ins],
        out_specs=[pl.BlockSpec(bs, im) for _, _, bs, im in outs] + [full(s) for s, _ in accs],
        out_shape=[jax.ShapeDtypeStruct(s, d) for s, d, _, _ in outs] + [jax.ShapeDtypeStruct(s, d) for s, d in accs],
        scratch_shapes=list(scratch),
        compiler_params=pltpu.CompilerParams(dimension_semantics=("arbitrary",)),
    )(*[a for a, _, _ in ins])
    return res


def _rb(arr, bm, cb=0, width=None):
    w = arr.shape[1] if width is None else width
    return (arr, (bm, w), lambda i: (i, cb))


def _const(arr):
    nd = arr.ndim
    return (arr, tuple(arr.shape), lambda i: (0,) * nd)


def _ro(t, w, dtype, bm):
    return ((t, w), dtype, (bm, w), lambda i: (i, 0))


def _rmsnorm(x, g, *, name, bm=512):
    t, d = x.shape
    bm = _tile(t, bm)

    def fn(i, ins, outs, accs, scr):
        outs[0][...] = _rms_fwd(ins[0][...], ins[1][...]).astype(BF16)

    return _rows(fn, name=name, steps=t // bm, ins=[_rb(x, bm), _const(g)], outs=[_ro(t, d, BF16, bm)])[0]


def _rmsnorm_bwd(dh, x, g, dres, *, name, bm=512):
    t, d = x.shape
    bm = _tile(t, bm)

    def fn(i, ins, outs, accs, scr):
        dx, dgrow = _rms_bwd(ins[0][...], ins[1][...], ins[2][...])
        outs[0][...] = ins[3][...] + dx
        accs[0][...] += _colsum(dgrow)

    return _rows(fn, name=name, steps=t // bm, ins=[_rb(dh, bm), _rb(x, bm), _const(g), _rb(dres, bm)],
                 outs=[_ro(t, d, F32, bm)], accs=[((1, d), F32)])


def _gate(o, p, gcb, *, name, bm=512):
    t, w = o.shape
    bm = _tile(t, bm)

    def fn(i, ins, outs, accs, scr):
        outs[0][...] = (ins[0][...] * _silu(ins[1][...])).astype(BF16)

    return _rows(fn, name=name, steps=t // bm, ins=[_rb(o, bm), _rb(p, bm, gcb, w)], outs=[_ro(t, w, BF16, bm)])[0]


def _gate_bwd(dy, o, p, gcb, *, name, bm=512):
    t, w = o.shape
    bm = _tile(t, bm)

    def fn(i, ins, outs, accs, scr):
        dy_, o_, g_ = ins[0][...], ins[1][...], ins[2][...]
        outs[0][...] = dy_ * _silu(g_)
        outs[1][...] = dy_ * o_ * _dsilu(g_)

    return _rows(fn, name=name, steps=t // bm, ins=[_rb(dy, bm), _rb(o, bm), _rb(p, bm, gcb, w)],
                 outs=[_ro(t, w, F32, bm), _ro(t, w, F32, bm)])


def _loss_head(x, g, tgt, *, name, bm=512):
    t, d = x.shape
    bm = _tile(t, bm)

    def fn(i, ins, outs, accs, scr):
        x_, g_, tg = ins[0][...], ins[1][...], ins[2][...]
        err = _rms_fwd(x_, g_) - tg
        part = 0.5 * jnp.sum(jnp.sum(err * err, axis=-1, keepdims=True), axis=0, keepdims=True) / d
        dx, dgrow = _rms_bwd(err / d, x_, g_)
        outs[0][...] = dx
        accs[0][...] += _colsum(dgrow)
        accs[1][...] += jnp.broadcast_to(part, (1, BLK))

    return _rows(fn, name=name, steps=t // bm, ins=[_rb(x, bm), _const(g), _rb(tgt, bm)],
                 outs=[_ro(t, d, F32, bm)], accs=[((1, d), F32), ((1, BLK), F32)])


def _gmlp_mix_weights(ws_ref, g):
    row = lax.broadcasted_iota(jnp.int32, (BLK, BLK), 0)
    col = lax.broadcasted_iota(jnp.int32, (BLK, BLK), 1)
    tril = col <= row
    return jnp.where(tril, ws_ref[g], 0.0), tril


def _gmlp_fwd(p, ln_g, ln_b, w_s, bs_t, *, name):
    t = p.shape[0]

    def fn(i, ins, outs, accs, scr):
        p_ref, lg, lb, ws_ref, bst = ins
        vn = _ln_stats(_gelu(p_ref[:, GM_W:2 * GM_W]))[0] * lg[...] + lb[...]
        for g in range(GM_G):
            cs = slice(g * BLK, (g + 1) * BLK)
            wt, _ = _gmlp_mix_weights(ws_ref, g)
            s = _dot(wt.astype(BF16), vn[:, cs].astype(BF16)) + bst[:, g:g + 1]
            u = _gelu(p_ref[:, cs])
            gate = p_ref[:, 2 * GM_W + g * BLK:2 * GM_W + (g + 1) * BLK]
            outs[0][:, cs] = (u * s * _silu(gate)).astype(BF16)

    return _rows(fn, name=name, steps=t // BLK, ins=[_rb(p, BLK), _const(ln_g), _const(ln_b), _const(w_s), _const(bs_t)],
                 outs=[_ro(t, GM_W, BF16, BLK)])[0]


def _gmlp_bwd(dy, p, ln_g, ln_b, w_s, bs_t, *, name):
    t = p.shape[0]

    def fn(i, ins, outs, accs, scr):
        dy_ref, p_ref, lg, lb, ws_ref, bst = ins
        dp_ref = outs[0]
        dlg, dlb, dws, dbst = accs
        dvn_ref = scr[0]
        v_pre = p_ref[:, GM_W:2 * GM_W]
        xh, r = _ln_stats(_gelu(v_pre))
        vn = xh * lg[...] + lb[...]
        for g in range(GM_G):
            cs = slice(g * BLK, (g + 1) * BLK)
            gs = slice(2 * GM_W + g * BLK, 2 * GM_W + (g + 1) * BLK)
            wt, tril = _gmlp_mix_weights(ws_ref, g)
            vg = vn[:, cs].astype(BF16)
            s = _dot(wt.astype(BF16), vg) + bst[:, g:g + 1]
            u_pre, gate, dyg = p_ref[:, cs], p_ref[:, gs], dy_ref[:, cs]
            u = _gelu(u_pre)
            dos = dyg * _silu(gate)
            dp_ref[:, gs] = dyg * u * s * _dsilu(gate)
            dp_ref[:, cs] = dos * s * _dgelu(u_pre)
            ds = (dos * u).astype(BF16)
            dws[g] += jnp.where(tril, _dot(ds, vg, _NT), 0.0)
            dbst[:, g:g + 1] += jnp.sum(dos * u, axis=1, keepdims=True)
            dvn_ref[:, cs] = _dot(wt.astype(BF16), ds, _TN)
        dvn = dvn_ref[...]
        dlg[...] += _colsum(dvn * xh)
        dlb[...] += _colsum(dvn)
        dp_ref[:, GM_W:2 * GM_W] = _ln_bwd(dvn, xh, r, lg[...]) * _dgelu(v_pre)

    return _rows(fn, name=name, steps=t // BLK,
                 ins=[_rb(dy, BLK), _rb(p, BLK), _const(ln_g), _const(ln_b), _const(w_s), _const(bs_t)],
                 outs=[_ro(t, 3 * GM_W, F32, BLK)],
                 accs=[((1, GM_W), F32), ((1, GM_W), F32), ((GM_G, BLK, BLK), F32), ((BLK, GM_G), F32)],
                 scratch=[pltpu.VMEM((BLK, GM_W), F32)])


CV_BM = 256


def _conv_halo_prev(p, cb, bm):
    per = bm // HALO
    return (p, (HALO, CV_W), lambda i: (jnp.maximum(i * per - 1, 0), cb))


def _conv_taps(ext_ref, cw, bm):
    y = jnp.zeros((bm, CV_W), F32)
    for k in range(CV_K):
        y = y + cw[k:k + 1, :] * ext_ref[pl.ds(HALO - (CV_K - 1) + k, bm), :]
    return y


def _conv_fill(i, ext_ref, a_prev, b_prev, a, b, bm, seq):
    keep = jnp.where((i % (seq // bm)) == 0, 0.0, 1.0)
    ext_ref[pl.ds(0, HALO), :] = keep * (a_prev * _sigmoid(b_prev))
    ext_ref[pl.ds(HALO, bm), :] = a * _sigmoid(b)


def _conv_fwd(p, cw, cb, ln_g, ln_b, seq, *, name, bm=CV_BM):
    t = p.shape[0]

    def fn(i, ins, outs, accs, scr):
        a, b, gate, ap, bp, cw_, cb_, lg, lb = [r[...] for r in ins]
        _conv_fill(i, scr[0], ap, bp, a, b, bm, seq)
        y1 = _conv_taps(scr[0], cw_, bm) + cb_
        y2 = _ln_stats(y1)[0] * lg + lb
        outs[0][...] = (_silu(y2) * _silu(gate)).astype(BF16)

    return _rows(fn, name=name, steps=t // bm,
                 ins=[_rb(p, bm, 0, CV_W), _rb(p, bm, 1, CV_W), _rb(p, bm, 2, CV_W),
                      _conv_halo_prev(p, 0, bm), _conv_halo_prev(p, 1, bm),
                      _const(cw), _const(cb), _const(ln_g), _const(ln_b)],
                 outs=[_ro(t, CV_W, BF16, bm)], scratch=[pltpu.VMEM((bm + HALO, CV_W), F32)])[0]


def _conv_bwd_post(dy, p, cw, cb, ln_g, ln_b, seq, *, name, bm=CV_BM):
    t = p.shape[0]

    def fn(i, ins, outs, accs, scr):
        dy_, a, b, gate, ap, bp, cw_, cb_, lg, lb = [r[...] for r in ins]
        dlg, dlb, dcb, dcw = accs
        ext = scr[0]
        _conv_fill(i, ext, ap, bp, a, b, bm, seq)
        xh, r = _ln_stats(_conv_taps(ext, cw_, bm) + cb_)
        y2 = xh * lg + lb
        outs[1][...] = dy_ * _silu(y2) * _dsilu(gate)
        dy2 = dy_ * _silu(gate) * _dsilu(y2)
        dlg[...] += _colsum(dy2 * xh)
        dlb[...] += _colsum(dy2)
        dy1 = _ln_bwd(dy2, xh, r, lg)
        outs[0][...] = dy1
        dcb[...] += _colsum(dy1)
        for k in range(CV_K):
            dcw[k:k + 1, :] += _colsum(dy1 * ext[pl.ds(HALO - (CV_K - 1) + k, bm), :])

    return _rows(fn, name=name, steps=t // bm,
                 ins=[_rb(dy, bm), _rb(p, bm, 0, CV_W), _rb(p, bm, 1, CV_W), _rb(p, bm, 2, CV_W),
                      _conv_halo_prev(p, 0, bm), _conv_halo_prev(p, 1, bm),
                      _const(cw), _const(cb), _const(ln_g), _const(ln_b)],
                 outs=[_ro(t, CV_W, F32, bm), _ro(t, CV_W, F32, bm)],
                 accs=[((1, CV_W), F32), ((1, CV_W), F32), ((1, CV_W), F32), ((CV_K, CV_W), F32)],
                 scratch=[pltpu.VMEM((bm + HALO, CV_W), F32)])


def _conv_bwd_pre(dy1, dgate, p, cw, seq, *, name, bm=CV_BM):
    t = p.shape[0]
    per = bm // HALO
    last_halo = t // HALO - 1

    def fn(i, ins, outs, accs, scr):
        d1, d1n, dg, a, b, cw_ = [r[...] for r in ins]
        ext = scr[0]
        keep = jnp.where((i % (seq // bm)) == (seq // bm - 1), 0.0, 1.0)
        ext[pl.ds(0, bm), :] = d1
        ext[pl.ds(bm, HALO), :] = keep * d1n
        dy0 = jnp.zeros((bm, CV_W), F32)
        for k in range(CV_K):
            dy0 = dy0 + cw_[k:k + 1, :] * ext[pl.ds(CV_K - 1 - k, bm), :]
        sb = _sigmoid(b)
        outs[0][:, 0:CV_W] = dy0 * sb
        outs[0][:, CV_W:2 * CV_W] = dy0 * a * sb * (1.0 - sb)
        outs[0][:, 2 * CV_W:3 * CV_W] = dg

    return _rows(fn, name=name, steps=t // bm,
                 ins=[_rb(dy1, bm), (dy1, (HALO, CV_W), lambda i: (jnp.minimum((i + 1) * per, last_halo), 0)),
                      _rb(dgate, bm), _rb(p, bm, 0, CV_W), _rb(p, bm, 1, CV_W), _const(cw)],
                 outs=[_ro(t, 3 * CV_W, F32, bm)], scratch=[pltpu.VMEM((bm + HALO, CV_W), F32)])[0]


def _iotas():
    row = lax.broadcasted_iota(jnp.int32, (BLK, BLK), 0)
    col = lax.broadcasted_iota(jnp.int32, (BLK, BLK), 1)
    return row, col


def _heads(x, head0):
    return jnp.where(head0, x, 0.0).astype(BF16), jnp.where(head0, 0.0, x).astype(BF16)


def _pair_spec(seq, off):
    return pl.BlockSpec((seq, BLK), lambda b, hp: (b, off + hp))


def _stat_spec(seq):
    return pl.BlockSpec((None, None, seq, BLK), lambda b, hp: (b, hp, 0, 0))


_ATT_PARAMS = dict(compiler_params=pltpu.CompilerParams(dimension_semantics=("parallel", "parallel")))
_SCALE = 1.0 / math.sqrt(HEAD_DIM)


def _sb_terms(qh, k, mask):
    z = _dot(qh, k, _NT) * _SCALE
    e = jnp.exp(-jnp.abs(z))
    t = jnp.log(1.0 + e)
    lsz = jnp.minimum(z, 0.0) - t
    lr = jnp.where(mask, jnp.minimum(-z, 0.0) - t, 0.0)
    return z, e, lsz, lr


def _sb_fwd(p, nb, seq, *, name):
    nq = seq // BLK

    def body(q_ref, k_ref, v_ref, o_ref, tot_ref):
        row, col = _iotas()
        head0 = col < HEAD_DIM
        upper = (row > col).astype(BF16)

        def qblock(qb, c):
            t0 = pl.multiple_of(qb * BLK, BLK)
            q0, q1 = _heads(q_ref[pl.ds(t0, BLK), :], head0)

            def kblock(it, carry):
                acc, r0, r1 = carry
                s0 = pl.multiple_of((qb - it) * BLK, BLK)
                k = k_ref[pl.ds(s0, BLK), :].astype(BF16)
                v0, v1 = _heads(v_ref[pl.ds(s0, BLK), :], head0)
                mask = (s0 + col) < (t0 + row)

                def one(qh, r):
                    _, _, lsz, lr = _sb_terms(qh, k, mask)
                    after = _dot3_right(lr, upper) + r
                    w = jnp.where(mask, jnp.exp(lsz + after), 0.0)
                    return w.astype(BF16), r + jnp.sum(lr, axis=1, keepdims=True)

                w0, r0 = one(q0, r0)
                w1, r1 = one(q1, r1)
                return acc + _dot(w0, v0) + _dot(w1, v1), r0, r1

            zero_col = jnp.zeros((BLK, 1), F32)
            acc, r0, r1 = lax.fori_loop(0, qb + 1, kblock, (jnp.zeros((BLK, BLK), F32), zero_col, zero_col))
            o_ref[pl.ds(t0, BLK), :] = acc
            tot_ref[pl.ds(t0, BLK), :] = jnp.where(head0, r0, r1)
            return c

        lax.fori_loop(0, nq, qblock, 0)

    return pl.pallas_call(
        body, name=name, grid=(nb, PAIRS),
        in_specs=[_pair_spec(seq, 0), _pair_spec(seq, PAIRS), _pair_spec(seq, 2 * PAIRS)],
        out_specs=[_pair_spec(seq, 0), _stat_spec(seq)],
        out_shape=[jax.ShapeDtypeStruct((nb * seq, D_MODEL), F32), jax.ShapeDtypeStruct((nb, PAIRS, seq, BLK), F32)],
        **_ATT_PARAMS,
    )(p, p, p)


def _sb_bwd(p, do, tot, nb, seq, *, name):
    nq = seq // BLK

    def body(q_ref, k_ref, v_ref, do_ref, tot_ref, dq_ref, dk_ref, dv_ref):
        row, col = _iotas()
        head0 = col < HEAD_DIM
        lower_incl = (row <= col).astype(BF16)
        lower_strict = (row < col).astype(BF16)
        dk_ref[...] = jnp.zeros(dk_ref.shape, F32)
        dv_ref[...] = jnp.zeros(dv_ref.shape, F32)

        def qblock(qb, c):
            t0 = pl.multiple_of(qb * BLK, BLK)
            q0, q1 = _heads(q_ref[pl.ds(t0, BLK), :], head0)
            do0, do1 = _heads(do_ref[pl.ds(t0, BLK), :], head0)
            tot = tot_ref[pl.ds(t0, BLK), :]
            tot0 = jnp.max(jnp.where(head0, tot, -jnp.inf), axis=1, keepdims=True)
            tot1 = jnp.max(jnp.where(head0, -jnp.inf, tot), axis=1, keepdims=True)

            def kblock(kb, carry):
                dq, pf0, pf1, ef0, ef1 = carry
                s0 = pl.multiple_of(kb * BLK, BLK)
                kf = k_ref[pl.ds(s0, BLK), :]
                k = kf.astype(BF16)
                k0, k1 = _heads(kf, head0)
                v = v_ref[pl.ds(s0, BLK), :].astype(BF16)
                mask = (s0 + col) < (t0 + row)

                def one(qh, doh, tt, pf, ef):
                    z, e, lsz, lr = _sb_terms(qh, k, mask)
                    after = tt - pf - _dot3_right(lr, lower_incl)
                    w = jnp.where(mask, jnp.exp(lsz + after), 0.0)
                    ew = _dot(doh, v, _NT) * w
                    dlr = ef + _dot3_right(ew, lower_strict)
                    s = 1.0 / (1.0 + e)
                    sig_pos = jnp.where(z >= 0.0, s, e * s)
                    sig_neg = jnp.where(z >= 0.0, e * s, s)
                    dz = jnp.where(mask, ew * sig_neg - dlr * sig_pos, 0.0) * _SCALE
                    return (dz.astype(BF16), w.astype(BF16), pf + jnp.sum(lr, axis=1, keepdims=True),
                            ef + jnp.sum(ew, axis=1, keepdims=True))

                dz0, w0, pf0, ef0 = one(q0, do0, tot0, pf0, ef0)
                dz1, w1, pf1, ef1 = one(q1, do1, tot1, pf1, ef1)
                dq = dq + _dot(dz0, k0) + _dot(dz1, k1)
                dk_ref[pl.ds(s0, BLK), :] += _dot(dz0, q0, _TN) + _dot(dz1, q1, _TN)
                dv_ref[pl.ds(s0, BLK), :] += _dot(w0, do0, _TN) + _dot(w1, do1, _TN)
                return dq, pf0, pf1, ef0, ef1

            zc = jnp.zeros((BLK, 1), F32)
            dq = lax.fori_loop(0, qb + 1, kblock, (jnp.zeros((BLK, BLK), F32), zc, zc, zc, zc))[0]
            dq_ref[pl.ds(t0, BLK), :] = dq
            return c

        lax.fori_loop(0, nq, qblock, 0)

    t = nb * seq
    return pl.pallas_call(
        body, name=name, grid=(nb, PAIRS),
        in_specs=[_pair_spec(seq, 0), _pair_spec(seq, PAIRS), _pair_spec(seq, 2 * PAIRS), _pair_spec(seq, 0), _stat_spec(seq)],
        out_specs=[_pair_spec(seq, 0)] * 3,
        out_shape=[jax.ShapeDtypeStruct((t, D_MODEL), F32)] * 3,
        **_ATT_PARAMS,
    )(p, p, p, do, tot)


def _fox_cum(f, bf, nb, seq, *, name):
    def body(f_ref, bf_ref, cc_ref, cr_ref):
        row, col = _iotas()
        lower = (col <= row).astype(BF16)
        carry = jnp.zeros((1, BLK), F32)
        for blk in range(seq // BLK):
            rs = slice(blk * BLK, (blk + 1) * BLK)
            lf = jnp.where(col < HEADS, _log_sigmoid(f_ref[rs, :] + bf_ref[...]), 0.0)
            cc = _dot3_left(lower, lf) + carry
            cc_ref[rs, :] = cc
            cr_ref[:, rs] = cc.T[0:HEADS, :]
            carry = carry + _colsum(lf)

    return pl.pallas_call(
        body, name=name, grid=(nb,),
        in_specs=[pl.BlockSpec((seq, BLK), lambda b: (b, 0)), pl.BlockSpec((1, BLK), lambda b: (0, 0))],
        out_specs=[pl.BlockSpec((seq, BLK), lambda b: (b, 0)), pl.BlockSpec((None, HEADS, seq), lambda b: (b, 0, 0))],
        out_shape=[jax.ShapeDtypeStruct((nb * seq, BLK), F32), jax.ShapeDtypeStruct((nb, HEADS, seq), F32)],
        compiler_params=pltpu.CompilerParams(dimension_semantics=("parallel",)),
    )(f, bf)


def _fox_cum_bwd(dcr, dcc, f, bf, nb, seq, *, name):
    def body(dcr_ref, dcc_ref, f_ref, bf_ref, df_ref, dbf_ref):
        row, col = _iotas()
        upper_incl = (col >= row).astype(BF16)

        @pl.when(pl.program_id(0) == 0)
        def _():
            dbf_ref[...] = jnp.zeros((1, BLK), F32)

        carry = jnp.zeros((1, BLK), F32)
        for blk in reversed(range(seq // BLK)):
            rs = slice(blk * BLK, (blk + 1) * BLK)
            dc = dcr_ref[:, rs].T + dcc_ref[rs, :]
            dlf = _dot3_left(upper_incl, dc) + carry
            carry = carry + _colsum(dc)
            fl = f_ref[rs, :] + bf_ref[...]
            df = jnp.where(col < HEADS, dlf * _sigmoid(-fl), 0.0)
            df_ref[rs, :] = df
            dbf_ref[...] += _colsum(df)

    return pl.pallas_call(
        body, name=name, grid=(nb,),
        in_specs=[pl.BlockSpec((None, BLK, seq), lambda b: (b, 0, 0)), pl.BlockSpec((seq, BLK), lambda b: (b, 0)),
                  pl.BlockSpec((seq, BLK), lambda b: (b, 0)), pl.BlockSpec((1, BLK), lambda b: (0, 0))],
        out_specs=[pl.BlockSpec((seq, BLK), lambda b: (b, 0)), pl.BlockSpec((1, BLK), lambda b: (0, 0))],
        out_shape=[jax.ShapeDtypeStruct((nb * seq, BLK), F32), jax.ShapeDtypeStruct((1, BLK), F32)],
        compiler_params=pltpu.CompilerParams(dimension_semantics=("arbitrary",)),
    )(dcr, dcc, f, bf)


def _fox_cum_cols(cc_ref, t0, col, hp):
    cc = cc_ref[pl.ds(t0, BLK), :]
    c0 = jnp.sum(jnp.where(col == 2 * hp, cc, 0.0), axis=1, keepdims=True)
    c1 = jnp.sum(jnp.where(col == 2 * hp + 1, cc, 0.0), axis=1, keepdims=True)
    return c0, c1


def _fox_fwd(p, cc, cr, nb, seq, *, name):
    nq = seq // BLK

    def body(q_ref, k_ref, v_ref, cc_ref, cr_ref, o_ref, lse_ref):
        hp = pl.program_id(1)
        row, col = _iotas()
        head0 = col < HEAD_DIM

        def qblock(qb, c):
            t0 = pl.multiple_of(qb * BLK, BLK)
            q0, q1 = _heads(q_ref[pl.ds(t0, BLK), :], head0)
            c0, c1 = _fox_cum_cols(cc_ref, t0, col, hp)

            def kblock(kb, carry):
                acc, m0, l0, m1, l1 = carry
                s0 = pl.multiple_of(kb * BLK, BLK)
                k = k_ref[pl.ds(s0, BLK), :].astype(BF16)
                v0, v1 = _heads(v_ref[pl.ds(s0, BLK), :], head0)
                mask = (s0 + col) <= (t0 + row)

                def one(qh, ct, h, m, l):
                    s = _dot(qh, k, _NT) * _SCALE + ct - cr_ref[h:h + 1, pl.ds(s0, BLK)]
                    s = jnp.where(mask, s, -jnp.inf)
                    m_new = jnp.maximum(m, jnp.max(s, axis=1, keepdims=True))
                    pr = jnp.exp(s - m_new)
                    alpha = jnp.exp(m - m_new)
                    return pr.astype(BF16), m_new, alpha * l + jnp.sum(pr, axis=1, keepdims=True), alpha

                p0, m0, l0, a0 = one(q0, c0, 0, m0, l0)
                p1, m1, l1, a1 = one(q1, c1, 1, m1, l1)
                acc = acc * jnp.where(head0, a0, a1) + _dot(p0, v0) + _dot(p1, v1)
                return acc, m0, l0, m1, l1

            ninf = jnp.full((BLK, 1), -jnp.inf, F32)
            zc = jnp.zeros((BLK, 1), F32)
            acc, m0, l0, m1, l1 = lax.fori_loop(0, qb + 1, kblock, (jnp.zeros((BLK, BLK), F32), ninf, zc, ninf, zc))
            o_ref[pl.ds(t0, BLK), :] = acc / jnp.where(head0, l0, l1)
            lse_ref[pl.ds(t0, BLK), :] = jnp.where(head0, m0 + jnp.log(l0), m1 + jnp.log(l1))
            return c

        lax.fori_loop(0, nq, qblock, 0)

    return pl.pallas_call(
        body, name=name, grid=(nb, PAIRS),
        in_specs=[_pair_spec(seq, 0), _pair_spec(seq, PAIRS), _pair_spec(seq, 2 * PAIRS),
                  pl.BlockSpec((seq, BLK), lambda b, hp: (b, 0)), pl.BlockSpec((None, None, 8, seq), lambda b, hp: (b, hp, 0, 0))],
        out_specs=[_pair_spec(seq, 0), _stat_spec(seq)],
        out_shape=[jax.ShapeDtypeStruct((nb * seq, D_MODEL), F32), jax.ShapeDtypeStruct((nb, PAIRS, seq, BLK), F32)],
        **_ATT_PARAMS,
    )(p, p, p, cc, cr)


def _fox_bwd(p, do, o, lse, cc, cr, nb, seq, *, name):
    nq = seq // BLK

    def body(q_ref, k_ref, v_ref, do_ref, o_ref, lse_ref, cc_ref, cr_ref, dq_ref, dk_ref, dv_ref, dcr_ref, dcc_ref):
        hp = pl.program_id(1)
        row, col = _iotas()
        head0 = col < HEAD_DIM
        dk_ref[...] = jnp.zeros(dk_ref.shape, F32)
        dv_ref[...] = jnp.zeros(dv_ref.shape, F32)
        dcr_ref[...] = jnp.zeros(dcr_ref.shape, F32)

        def qblock(qb, c):
            t0 = pl.multiple_of(qb * BLK, BLK)
            q0, q1 = _heads(q_ref[pl.ds(t0, BLK), :], head0)
            dof = do_ref[pl.ds(t0, BLK), :]
            do0, do1 = _heads(dof, head0)
            prod = dof * o_ref[pl.ds(t0, BLK), :]
            dl0 = jnp.sum(jnp.where(head0, prod, 0.0), axis=1, keepdims=True)
            dl1 = jnp.sum(jnp.where(head0, 0.0, prod), axis=1, keepdims=True)
            lse = lse_ref[pl.ds(t0, BLK), :]
            lse0 = jnp.max(jnp.where(head0, lse, -jnp.inf), axis=1, keepdims=True)
            lse1 = jnp.max(jnp.where(head0, -jnp.inf, lse), axis=1, keepdims=True)
            c0, c1 = _fox_cum_cols(cc_ref, t0, col, hp)

            def kblock(kb, carry):
                dq, rs0, rs1 = carry
                s0 = pl.multiple_of(kb * BLK, BLK)
                kf = k_ref[pl.ds(s0, BLK), :]
                k = kf.astype(BF16)
                k0, k1 = _heads(kf, head0)
                v = v_ref[pl.ds(s0, BLK), :].astype(BF16)
                mask = (s0 + col) <= (t0 + row)

                def one(qh, doh, ct, h, lse_h, dl):
                    s = _dot(qh, k, _NT) * _SCALE + ct - cr_ref[h:h + 1, pl.ds(s0, BLK)]
                    pr = jnp.where(mask, jnp.exp(s - lse_h), 0.0)
                    ds = pr * (_dot(doh, v, _NT) - dl)
                    dcr_ref[pl.ds(h, 1), pl.ds(s0, BLK)] -= _colsum(ds)
                    return (ds * _SCALE).astype(BF16), pr.astype(BF16), jnp.sum(ds, axis=1, keepdims=True)

                ds0, p0, r0 = one(q0, do0, c0, 0, lse0, dl0)
                ds1, p1, r1 = one(q1, do1, c1, 1, lse1, dl1)
                dk_ref[pl.ds(s0, BLK), :] += _dot(ds0, q0, _TN) + _dot(ds1, q1, _TN)
                dv_ref[pl.ds(s0, BLK), :] += _dot(p0, do0, _TN) + _dot(p1, do1, _TN)
                return dq + _dot(ds0, k0) + _dot(ds1, k1), rs0 + r0, rs1 + r1

            zc = jnp.zeros((BLK, 1), F32)
            dq, rs0, rs1 = lax.fori_loop(0, qb + 1, kblock, (jnp.zeros((BLK, BLK), F32), zc, zc))
            dq_ref[pl.ds(t0, BLK), :] = dq
            dcc_ref[pl.ds(t0, BLK), :] = jnp.where(head0, rs0, rs1)
            return c

        lax.fori_loop(0, nq, qblock, 0)

    t = nb * seq
    return pl.pallas_call(
        body, name=name, grid=(nb, PAIRS),
        in_specs=[_pair_spec(seq, 0), _pair_spec(seq, PAIRS), _pair_spec(seq, 2 * PAIRS), _pair_spec(seq, 0), _pair_spec(seq, 0),
                  _stat_spec(seq), pl.BlockSpec((seq, BLK), lambda b, hp: (b, 0)),
                  pl.BlockSpec((None, None, 8, seq), lambda b, hp: (b, hp, 0, 0))],
        out_specs=[_pair_spec(seq, 0)] * 3 + [pl.BlockSpec((None, None, 8, seq), lambda b, hp: (b, hp, 0, 0)), _stat_spec(seq)],
        out_shape=[jax.ShapeDtypeStruct((t, D_MODEL), F32)] * 3 + [jax.ShapeDtypeStruct((nb, PAIRS, 8, seq), F32),
                                                                     jax.ShapeDtypeStruct((nb, PAIRS, seq, BLK), F32)],
        **_ATT_PARAMS,
    )(p, p, p, do, o, lse, cc, cr)


def _local_step(x3, tgt3, w):
    nb, seq, d = x3.shape
    t = nb * seq
    x0, tgt = x3.reshape(t, d), tgt3.reshape(t, d)
    g = {}

    a_gain = w["a_norm"].reshape(1, d)
    h_a = _rmsnorm(x0, a_gain, name="a_norm_fwd")
    p_a = _matmul(h_a, w["a_w_in"], name="a_in_fwd")
    o_a, tot_a = _sb_fwd(p_a, nb, seq, name="a_attn_fwd")
    y_a = _gate(o_a, p_a, 3, name="a_gate_fwd")
    x1 = _matmul(y_a, w["a_w_out"], name="a_out_fwd", residual=x0)

    b_gain = w["b_norm"].reshape(1, d)
    b_lg, b_lb = w["b_v_ln_g"].reshape(1, GM_W), w["b_v_ln_b"].reshape(1, GM_W)
    b_ws, b_bst = w["b_w_s"].reshape(GM_G, BLK, BLK), w["b_b_s"].reshape(GM_G, BLK).T
    h_b = _rmsnorm(x1, b_gain, name="b_norm_fwd")
    p_b = _matmul(h_b, w["b_w_in"], name="b_in_fwd")
    y_b = _gmlp_fwd(p_b, b_lg, b_lb, b_ws, b_bst, name="b_mix_fwd")
    x2 = _matmul(y_b, w["b_w_out"], name="b_out_fwd", residual=x1)

    c_gain = w["c_norm"].reshape(1, d)
    c_cw, c_cb = w["c_conv_w"].reshape(CV_K, CV_W), w["c_conv_b"].reshape(1, CV_W)
    c_lg, c_lb = w["c_ln_g"].reshape(1, CV_W), w["c_ln_b"].reshape(1, CV_W)
    h_c = _rmsnorm(x2, c_gain, name="c_norm_fwd")
    p_c = _matmul(h_c, w["c_w_in"], name="c_in_fwd")
    y_c = _conv_fwd(p_c, c_cw, c_cb, c_lg, c_lb, seq, name="c_conv_fwd")
    x3_ = _matmul(y_c, w["c_w_out"], name="c_out_fwd", residual=x2)

    d_gain = w["d_norm"].reshape(1, d)
    d_win = w["d_w_in"].reshape(d, 4 * D_MODEL + HEADS)
    d_wmain = d_win[:, :4 * D_MODEL]
    d_wf = jnp.pad(d_win[:, 4 * D_MODEL:], ((0, 0), (0, BLK - HEADS)))
    d_bf = jnp.pad(w["d_b_f"].reshape(1, HEADS), ((0, 0), (0, BLK - HEADS)))
    h_d = _rmsnorm(x3_, d_gain, name="d_norm_fwd")
    p_d = _matmul(h_d, d_wmain, name="d_in_fwd")
    f_d = _matmul(h_d, d_wf, name="d_inf_fwd")
    cc, cr = _fox_cum(f_d, d_bf, nb, seq, name="d_cum_fwd")
    cr = jnp.pad(cr.reshape(nb, PAIRS, 2, seq), ((0, 0), (0, 0), (0, 6), (0, 0)))
    o_d, lse_d = _fox_fwd(p_d, cc, cr, nb, seq, name="d_attn_fwd")
    y_d = _gate(o_d, p_d, 3, name="d_gate_fwd")
    x4 = _matmul(y_d, w["d_w_out"], name="d_out_fwd", residual=x3_)

    f_gain = w["final_norm"].reshape(1, d)
    dx, g_fn, loss_row = _loss_head(x4, f_gain, tgt, name="loss_head")
    g["final_norm"] = g_fn

    g["d_w_out"] = _matmul(y_d, dx, name="d_out_dw", mode="tn")
    dy = _matmul(dx, w["d_w_out"], name="d_out_dy", mode="nt")
    do_d, dg_d = _gate_bwd(dy, o_d, p_d, 3, name="d_gate_bwd")
    dq, dk, dv, dcr, dcc = _fox_bwd(p_d, do_d, o_d, lse_d, cc, cr, nb, seq, name="d_attn_bwd")
    dcr = jnp.pad(dcr[:, :, :2, :].reshape(nb, HEADS, seq), ((0, 0), (0, BLK - HEADS), (0, 0)))
    dcc = dcc[:, :, :, ::HEAD_DIM].transpose(0, 2, 1, 3).reshape(t, HEADS)
    dcc = jnp.pad(dcc, ((0, 0), (0, BLK - HEADS)))
    df, dbf = _fox_cum_bwd(dcr, dcc, f_d, d_bf, nb, seq, name="d_cum_bwd")
    g["d_b_f"] = dbf[:, :HEADS]
    parts = [dq, dk, dv, dg_d]
    dws = [_matmul(h_d, pt, name=f"d_in_dw{n}", mode="tn") for n, pt in enumerate(parts)]
    dwf = _matmul(h_d, df, name="d_inf_dw", mode="tn")
    g["d_w_in"] = jnp.concatenate(dws + [dwf[:, :HEADS]], axis=1)
    dh = _matmul(df, d_wf, name="d_inf_dh", mode="nt")
    for n, pt in enumerate(parts):
        dh = _matmul(pt, d_wmain[:, n * D_MODEL:(n + 1) * D_MODEL], name=f"d_in_dh{n}", mode="nt", residual=dh)
    dx, g["d_norm"] = _rmsnorm_bwd(dh, x3_, d_gain, dx, name="d_norm_bwd")

    g["c_w_out"] = _matmul(y_c, dx, name="c_out_dw", mode="tn")
    dy = _matmul(dx, w["c_w_out"], name="c_out_dy", mode="nt")
    dy1, dgate, g["c_ln_g"], g["c_ln_b"], g["c_conv_b"], g["c_conv_w"] = _conv_bwd_post(
        dy, p_c, c_cw, c_cb, c_lg, c_lb, seq, name="c_conv_bwd_post")
    dp = _conv_bwd_pre(dy1, dgate, p_c, c_cw, seq, name="c_conv_bwd_pre")
    g["c_w_in"] = _matmul(h_c, dp, name="c_in_dw", mode="tn")
    dh = _matmul(dp, w["c_w_in"], name="c_in_dh", mode="nt")
    dx, g["c_norm"] = _rmsnorm_bwd(dh, x2, c_gain, dx, name="c_norm_bwd")

    g["b_w_out"] = _matmul(y_b, dx, name="b_out_dw", mode="tn")
    dy = _matmul(dx, w["b_w_out"], name="b_out_dy", mode="nt")
    dp, g["b_v_ln_g"], g["b_v_ln_b"], g["b_w_s"], dbst = _gmlp_bwd(dy, p_b, b_lg, b_lb, b_ws, b_bst, name="b_mix_bwd")
    g["b_b_s"] = dbst.T
    g["b_w_in"] = _matmul(h_b, dp, name="b_in_dw", mode="tn")
    dh = _matmul(dp, w["b_w_in"], name="b_in_dh", mode="nt")
    dx, g["b_norm"] = _rmsnorm_bwd(dh, x1, b_gain, dx, name="b_norm_bwd")

    g["a_w_out"] = _matmul(y_a, dx, name="a_out_dw", mode="tn")
    dy = _matmul(dx, w["a_w_out"], name="a_out_dy", mode="nt")
    do_a, dg_a = _gate_bwd(dy, o_a, p_a, 3, name="a_gate_bwd")
    dq, dk, dv = _sb_bwd(p_a, do_a, tot_a, nb, seq, name="a_attn_bwd")
    parts = [dq, dk, dv, dg_a]
    g["a_w_in"] = jnp.concatenate([_matmul(h_a, pt, name=f"a_in_dw{n}", mode="tn") for n, pt in enumerate(parts)], axis=1)
    dh = None
    for n, pt in enumerate(parts):
        dh = _matmul(pt, w["a_w_in"][:, n * D_MODEL:(n + 1) * D_MODEL], name=f"a_in_dh{n}", mode="nt", residual=dh)
    dx, g["a_norm"] = _rmsnorm_bwd(dh, x0, a_gain, dx, name="a_norm_bwd")

    return loss_row[0, 0], dx.reshape(nb, seq, d), g


_HBM = pl.BlockSpec(memory_space=pltpu.HBM)


def _place():
    return lax.axis_index("x"), lax.axis_index("y"), lax.axis_index("c")


def _other_chips(x, y):
    return [(1 - x, y), (x, 1 - y), (1 - x, 1 - y)]


def _allgather_chips(s, *, name):
    r, c_ = s.shape
    h = r // 2
    assert r % 32 == 0

    def body(s_ref, o_ref, send_sems, recv_sems, local_sem):
        x, y, c = _place()
        me = 2 * x + y
        chips = _other_chips(x, y)

        def half(j, hc):
            return o_ref.at[j, pl.ds(hc * h, h), :]

        def copy(kk, src, dst, to):
            return pltpu.make_async_remote_copy(src_ref=src, dst_ref=dst, send_sem=send_sems.at[kk], recv_sem=recv_sems.at[kk],
                                                device_id=to, device_id_type=MESH)

        mine = pltpu.make_async_copy(s_ref, o_ref.at[me], local_sem)
        mine.start()
        first = [copy(kk, s_ref.at[pl.ds(c * h, h), :], half(me, c), (cx, cy, c)) for kk, (cx, cy) in enumerate(chips)]
        for cp in first:
            cp.start()
        passed = []
        for kk, (cx, cy) in enumerate(chips):
            blk = half(2 * cx + cy, c)
            copy(kk, blk, blk, (cx, cy, c)).wait_recv()
            fwd = copy(3 + kk, blk, blk, (x, y, 1 - c))
            fwd.start()
            passed.append(fwd)
        for kk, (cx, cy) in enumerate(chips):
            blk = half(2 * cx + cy, 1 - c)
            copy(3 + kk, blk, blk, (x, y, 1 - c)).wait_recv()
        for cp in first + passed:
            cp.wait_send()
        mine.wait()

    return pl.pallas_call(
        body, name=name, in_specs=[_HBM], out_specs=_HBM,
        out_shape=jax.ShapeDtypeStruct((N_CHIPS, r, c_), s.dtype),
        scratch_shapes=[pltpu.SemaphoreType.DMA((6,)), pltpu.SemaphoreType.DMA((6,)), pltpu.SemaphoreType.DMA],
    )(s)


def _swap_halves(gp, *, name):
    n, r, c_ = gp.shape
    h = r // 2

    def body(g_ref, o_ref, send_sem, recv_sem):
        x, y, c = _place()
        cp = pltpu.make_async_remote_copy(src_ref=g_ref.at[:, pl.ds((1 - c) * h, h), :], dst_ref=o_ref, send_sem=send_sem,
                                          recv_sem=recv_sem, device_id=(x, y, 1 - c), device_id_type=MESH)
        cp.start()
        cp.wait()

    return pl.pallas_call(
        body, name=name, in_specs=[_HBM], out_specs=_HBM, out_shape=jax.ShapeDtypeStruct((n, h, c_), gp.dtype),
        scratch_shapes=[pltpu.SemaphoreType.DMA, pltpu.SemaphoreType.DMA],
    )(gp)


def _scatter_chips(hp, *, name):
    n, h, c_ = hp.shape

    def body(h_ref, o_ref, send_sems, recv_sems):
        x, y, c = _place()
        cps = [pltpu.make_async_remote_copy(src_ref=h_ref.at[2 * cx + cy], dst_ref=o_ref.at[kk], send_sem=send_sems.at[kk],
                                            recv_sem=recv_sems.at[kk], device_id=(cx, cy, c), device_id_type=MESH)
               for kk, (cx, cy) in enumerate(_other_chips(x, y))]
        for cp in cps:
            cp.start()
        for cp in cps:
            cp.wait()

    return pl.pallas_call(
        body, name=name, in_specs=[_HBM], out_specs=_HBM, out_shape=jax.ShapeDtypeStruct((3, h, c_), hp.dtype),
        scratch_shapes=[pltpu.SemaphoreType.DMA((3,)), pltpu.SemaphoreType.DMA((3,))],
    )(hp)


def _join_halves(f, *, name):
    h, c_ = f.shape

    def body(f_ref, o_ref, send_sem, recv_sem, local_sem):
        x, y, c = _place()
        mine = pltpu.make_async_copy(f_ref, o_ref.at[pl.ds(c * h, h), :], local_sem)
        mine.start()
        cp = pltpu.make_async_remote_copy(src_ref=f_ref, dst_ref=o_ref.at[pl.ds(c * h, h), :], send_sem=send_sem,
                                          recv_sem=recv_sem, device_id=(x, y, 1 - c), device_id_type=MESH)
        cp.start()
        pltpu.make_async_remote_copy(src_ref=f_ref, dst_ref=o_ref.at[pl.ds((1 - c) * h, h), :], send_sem=send_sem,
                                     recv_sem=recv_sem, device_id=(x, y, 1 - c), device_id_type=MESH).wait_recv()
        cp.wait_send()
        mine.wait()

    return pl.pallas_call(
        body, name=name, in_specs=[_HBM], out_specs=_HBM, out_shape=jax.ShapeDtypeStruct((2 * h, c_), f.dtype),
        scratch_shapes=[pltpu.SemaphoreType.DMA, pltpu.SemaphoreType.DMA, pltpu.SemaphoreType.DMA],
    )(f)


def _add_halves(gp, ra, *, name, bm=256):
    n, r, c_ = gp.shape
    h = r // 2
    bm = _tile(h, bm)
    per = h // bm
    c = lax.axis_index("c").astype(jnp.int32).reshape(1)

    def body(c_ref, g_ref, ra_ref, o_ref):
        o_ref[...] = g_ref[...] + ra_ref[...]

    return pl.pallas_call(
        body, name=name,
        grid_spec=pltpu.PrefetchScalarGridSpec(
            num_scalar_prefetch=1, grid=(n, per),
            in_specs=[pl.BlockSpec((None, bm, c_), lambda j, i, cr: (j, cr[0] * per + i, 0)),
                      pl.BlockSpec((None, bm, c_), lambda j, i, cr: (j, i, 0))],
            out_specs=pl.BlockSpec((None, bm, c_), lambda j, i, cr: (j, i, 0))),
        out_shape=jax.ShapeDtypeStruct((n, h, c_), F32),
        compiler_params=pltpu.CompilerParams(dimension_semantics=("parallel", "parallel")),
    )(c, gp, ra)


def _add_chips(hp, rb, *, name, bm=256):
    n, h, c_ = hp.shape
    bm = _tile(h, bm)
    me = (2 * lax.axis_index("x") + lax.axis_index("y")).astype(jnp.int32).reshape(1)

    def body(me_ref, h_ref, rb_ref, o_ref):
        o_ref[...] = ((h_ref[...] + rb_ref[0]) + rb_ref[1]) + rb_ref[2]

    return pl.pallas_call(
        body, name=name,
        grid_spec=pltpu.PrefetchScalarGridSpec(
            num_scalar_prefetch=1, grid=(h // bm,),
            in_specs=[pl.BlockSpec((None, bm, c_), lambda i, mr: (mr[0], i, 0)),
                      pl.BlockSpec((3, bm, c_), lambda i, mr: (0, i, 0))],
            out_specs=pl.BlockSpec((bm, c_), lambda i, mr: (i, 0))),
        out_shape=jax.ShapeDtypeStruct((h, c_), F32),
        compiler_params=pltpu.CompilerParams(dimension_semantics=("parallel",)),
    )(me, hp, rb)


def _reduce_scatter(gp, *, tag):
    ra = _swap_halves(gp, name=f"{tag}_swap_halves")
    hp = _add_halves(gp, ra, name=f"{tag}_add_halves")
    rb = _scatter_chips(hp, name=f"{tag}_scatter_chips")
    f = _add_chips(hp, rb, name=f"{tag}_add_chips")
    return _join_halves(f, name=f"{tag}_join_halves")


def _adamw(w, g, m, v, *, name):
    r, c_ = w.shape
    bm = r
    for cand in (512, 256, 128, 64, 32, 16, 8):
        if r % cand == 0:
            bm = cand
            break
    c1 = 1.0 - ADAM_B1 ** ADAM_STEP
    c2 = 1.0 - ADAM_B2 ** ADAM_STEP

    def body(w_ref, g_ref, m_ref, v_ref, d_ref, nm_ref, nv_ref):
        g_ = g_ref[...]
        m_ = ADAM_B1 * m_ref[...] + (1.0 - ADAM_B1) * g_
        v_ = ADAM_B2 * v_ref[...] + (1.0 - ADAM_B2) * (g_ * g_)
        d_ref[...] = -ADAM_LR * ((m_ / c1) / (jnp.sqrt(v_ / c2) + ADAM_EPS) + ADAM_WD * w_ref[...])
        nm_ref[...] = m_
        nv_ref[...] = v_

    spec = pl.BlockSpec((bm, c_), lambda i: (i, 0))
    return pl.pallas_call(
        body, name=name, grid=(r // bm,), in_specs=[spec] * 4, out_specs=[spec] * 3,
        out_shape=[jax.ShapeDtypeStruct((r, c_), F32)] * 3,
        compiler_params=pltpu.CompilerParams(dimension_semantics=("parallel",)),
    )(w, g, m, v)


_WEIGHTS = ["a_norm", "a_w_in", "a_w_out", "b_norm", "b_w_in", "b_v_ln_g", "b_v_ln_b", "b_w_s", "b_b_s", "b_w_out",
            "c_norm", "c_w_in", "c_conv_w", "c_conv_b", "c_ln_g", "c_ln_b", "c_w_out", "d_norm", "d_w_in", "d_b_f",
            "d_w_out", "final_norm"]
_SHARD_AXIS = {"a_norm": None, "a_w_in": 2, "a_w_out": 1, "b_norm": 1, "b_w_in": 2, "b_v_ln_g": 1, "b_v_ln_b": 1, "b_w_s": None,
               "b_b_s": None, "b_w_out": 1, "c_norm": 1, "c_w_in": 2, "c_conv_w": 2, "c_conv_b": 1, "c_ln_g": 1, "c_ln_b": 1,
               "c_w_out": 1, "d_norm": 1, "d_w_in": 2, "d_b_f": None, "d_w_out": 1, "final_norm": None}
_BIG = ["a_w_in", "a_w_out", "b_w_in", "b_w_out", "c_w_in", "c_w_out", "d_w_in", "d_w_out"]
_SMALL_SHARDED = [n for n in _WEIGHTS if _SHARD_AXIS[n] is not None and n not in _BIG]
_REPLICATED = [n for n in _WEIGHTS if _SHARD_AXIS[n] is None]
_ROW_ALIGN = 32
_ROW_ALIGN_SUMMED = 512


def _pack(pieces, dtype, align=_ROW_ALIGN):
    flat = jnp.concatenate([p.reshape(-1).astype(dtype) for p in pieces])
    unit = align * PACK_C
    total = -(-flat.shape[0] // unit) * unit
    return jnp.pad(flat, (0, total - flat.shape[0])).reshape(total // PACK_C, PACK_C)


def _unpack(flat, shapes):
    out, off = [], 0
    for s in shapes:
        n = math.prod(s)
        out.append(flat[off:off + n].reshape(s))
        off += n
    return out


def _full_shape(local_shape, axis):
    s = list(local_shape)
    if axis is not None:
        s[axis] *= N_CHIPS
    return tuple(s)


def _gather_weights(local):
    full = {n: local[n] for n in _REPLICATED}
    for names, dtype, tag in ((_BIG, BF16, "gather_big"), (_SMALL_SHARDED, F32, "gather_small")):
        got = _allgather_chips(_pack([local[n] for n in names], dtype), name=tag)
        got = got.reshape(N_CHIPS, -1)
        shards = [_unpack(got[j], [local[n].shape for n in names]) for j in range(N_CHIPS)]
        for i, n in enumerate(names):
            full[n] = jnp.concatenate([shards[j][i] for j in range(N_CHIPS)], axis=_SHARD_AXIS[n])
    return full


def _repl_piece_len(local):
    total = sum(math.prod(local[n].shape) for n in _REPLICATED)
    return -(-total // N_CHIPS)


def _reduce_grads(g, local):
    sharded = _BIG + _SMALL_SHARDED
    rep_flat = jnp.concatenate([g[n].reshape(-1) for n in _REPLICATED])
    piece = _repl_piece_len(local)
    rep_flat = jnp.pad(rep_flat, (0, N_CHIPS * piece - rep_flat.shape[0]))
    blocks = []
    for j in range(N_CHIPS):
        pieces = []
        for n in sharded:
            full = g[n].reshape(_full_shape(local[n].shape, _SHARD_AXIS[n]))
            width = local[n].shape[_SHARD_AXIS[n]]
            pieces.append(lax.slice_in_dim(full, j * width, (j + 1) * width, axis=_SHARD_AXIS[n]))
        pieces.append(rep_flat[j * piece:(j + 1) * piece])
        blocks.append(_pack(pieces, F32, _ROW_ALIGN_SUMMED))
    summed = _reduce_scatter(jnp.stack(blocks), tag="grads").reshape(-1)
    out = _unpack(summed, [local[n].shape for n in sharded] + [(piece,)])
    red = dict(zip(sharded, out[:-1]))
    rep = _allgather_chips(_pack([out[-1]], F32), name="gather_replicated_grads").reshape(N_CHIPS, -1)[:, :piece].reshape(-1)
    for n, val in zip(_REPLICATED, _unpack(rep, [local[n].shape for n in _REPLICATED])):
        red[n] = val
    return red


def _update(local, grads, m, v):
    delta, new_m, new_v = {}, {}, {}
    for n in _BIG:
        shp = local[n].shape
        two = (shp[-2], shp[-1])
        res = _adamw(local[n].reshape(two), grads[n].reshape(two), m[n].reshape(two), v[n].reshape(two), name=f"adamw_{n}")
        delta[n], new_m[n], new_v[n] = [r.reshape(shp) for r in res]
    small = [n for n in _WEIGHTS if n not in _BIG]
    shapes = [local[n].shape for n in small]
    packed = [_pack([src[n] for n in small], F32) for src in (local, grads, m, v)]
    res = _adamw(*packed, name="adamw_small")
    for dst, r in zip((delta, new_m, new_v), res):
        for n, val in zip(small, _unpack(r.reshape(-1), shapes)):
            dst[n] = val
    return delta, new_m, new_v


def kernel(x, a_norm, a_w_in, a_w_out, b_norm, b_w_in, b_v_ln_g, b_v_ln_b, b_w_s, b_b_s, b_w_out, c_norm, c_w_in, c_conv_w, c_conv_b, c_ln_g, c_ln_b, c_w_out, d_norm, d_w_in, d_b_f, d_w_out, final_norm, loss_target, m_a_norm, m_a_w_in, m_a_w_out, m_b_norm, m_b_w_in, m_b_v_ln_g, m_b_v_ln_b, m_b_w_s, m_b_b_s, m_b_w_out, m_c_norm, m_c_w_in, m_c_conv_w, m_c_conv_b, m_c_ln_g, m_c_ln_b, m_c_w_out, m_d_norm, m_d_w_in, m_d_b_f, m_d_w_out, m_final_norm, v_a_norm, v_a_w_in, v_a_w_out, v_b_norm, v_b_w_in, v_b_v_ln_g, v_b_v_ln_b, v_b_w_s, v_b_b_s, v_b_w_out, v_c_norm, v_c_w_in, v_c_conv_w, v_c_conv_b, v_c_ln_g, v_c_ln_b, v_c_w_out, v_d_norm, v_d_w_in, v_d_b_f, v_d_w_out, v_final_norm):
    local = dict(zip(_WEIGHTS, (a_norm, a_w_in, a_w_out, b_norm, b_w_in, b_v_ln_g, b_v_ln_b, b_w_s, b_b_s, b_w_out, c_norm, c_w_in,
                                c_conv_w, c_conv_b, c_ln_g, c_ln_b, c_w_out, d_norm, d_w_in, d_b_f, d_w_out, final_norm)))
    m = dict(zip(_WEIGHTS, (m_a_norm, m_a_w_in, m_a_w_out, m_b_norm, m_b_w_in, m_b_v_ln_g, m_b_v_ln_b, m_b_w_s, m_b_b_s, m_b_w_out,
                            m_c_norm, m_c_w_in, m_c_conv_w, m_c_conv_b, m_c_ln_g, m_c_ln_b, m_c_w_out, m_d_norm, m_d_w_in, m_d_b_f,
                            m_d_w_out, m_final_norm)))
    v = dict(zip(_WEIGHTS, (v_a_norm, v_a_w_in, v_a_w_out, v_b_norm, v_b_w_in, v_b_v_ln_g, v_b_v_ln_b, v_b_w_s, v_b_b_s, v_b_w_out,
                            v_c_norm, v_c_w_in, v_c_conv_w, v_c_conv_b, v_c_ln_g, v_c_ln_b, v_c_w_out, v_d_norm, v_d_w_in, v_d_b_f,
                            v_d_w_out, v_final_norm)))
    full = _gather_weights(local)
    w = {n: full[n][0] if n != "final_norm" else full[n] for n in _WEIGHTS}
    loss_part, grad_x, g = _local_step(x, loss_target, w)
    loss = lax.psum(loss_part, ("x", "y", "c"))
    grads = _reduce_grads(g, local)
    delta, new_m, new_v = _update(local, grads, m, v)
    return (loss, grad_x, *[grads[n] for n in _WEIGHTS], *[delta[n] for n in _WEIGHTS],
            *[new_m[n] for n in _WEIGHTS], *[new_v[n] for n in _WEIGHTS])
```

```python
import functools
import math

import jax
import jax.numpy as jnp
from jax import lax
from jax.experimental import pallas as pl
from jax.experimental.pallas import tpu as pltpu

F32, BF16 = jnp.float32, jnp.bfloat16
MESH = pl.DeviceIdType.MESH

D_MODEL = 1024
HEADS = 16
HEAD_DIM = 64
BLK = 128
PAIRS = HEADS // 2
GM_W = 2048
GM_G = 16
CV_W = 2048
CV_K = 31
HALO = 32
EPS = 1e-6
N_CHIPS = 4
PACK_C = 1024
ADAM_LR, ADAM_B1, ADAM_B2, ADAM_EPS, ADAM_WD, ADAM_STEP = 0.001, 0.9, 0.999, 1e-08, 0.01, 10

_NT = (((1,), (1,)), ((), ()))
_TN = (((0,), (0,)), ((), ()))
_NN = (((1,), (0,)), ((), ()))


def _dot(a, b, dims=_NN):
    return lax.dot_general(a, b, dims, preferred_element_type=F32)


def _split3(x):
    hi = x.astype(BF16)
    r = x - hi.astype(F32)
    mid = r.astype(BF16)
    lo = (r - mid.astype(F32)).astype(BF16)
    return hi, mid, lo


def _dot3_right(x, m):
    hi, mid, lo = _split3(x)
    return _dot(hi, m) + _dot(mid, m) + _dot(lo, m)


def _dot3_left(m, x):
    hi, mid, lo = _split3(x)
    return _dot(m, hi) + _dot(m, mid) + _dot(m, lo)


def _sigmoid(x):
    return 1.0 / (1.0 + jnp.exp(-x))


def _silu(x):
    return x * _sigmoid(x)


def _dsilu(x):
    s = _sigmoid(x)
    return s * (1.0 + x * (1.0 - s))


_GELU_C = math.sqrt(2.0 / math.pi)
_GELU_A = 0.044715


def _gelu(x):
    return 0.5 * x * (1.0 + jnp.tanh(_GELU_C * (x + _GELU_A * x * x * x)))


def _dgelu(x):
    t = jnp.tanh(_GELU_C * (x + _GELU_A * x * x * x))
    return 0.5 * (1.0 + t) + 0.5 * x * (1.0 - t * t) * _GELU_C * (1.0 + 3.0 * _GELU_A * x * x)


def _log_sigmoid(x):
    return jnp.minimum(x, 0.0) - jnp.log(1.0 + jnp.exp(-jnp.abs(x)))


def _rms_fwd(x, g):
    r = lax.rsqrt(jnp.mean(x * x, axis=-1, keepdims=True) + EPS)
    return x * r * g


def _rms_bwd(dy, x, g):
    r = lax.rsqrt(jnp.mean(x * x, axis=-1, keepdims=True) + EPS)
    xh = x * r
    dxh = dy * g
    dx = r * (dxh - xh * jnp.mean(dxh * xh, axis=-1, keepdims=True))
    return dx, dy * xh


def _ln_stats(x):
    mu = jnp.mean(x, axis=-1, keepdims=True)
    xc = x - mu
    r = lax.rsqrt(jnp.mean(xc * xc, axis=-1, keepdims=True) + EPS)
    return xc * r, r


def _ln_bwd(dy, xh, r, g):
    dxh = dy * g
    return r * (dxh - jnp.mean(dxh, axis=-1, keepdims=True) - xh * jnp.mean(dxh * xh, axis=-1, keepdims=True))


def _colsum(x):
    return jnp.sum(x, axis=0, keepdims=True)


def _tile(n, want):
    for t in range(min(n, want), 7, -1):
        if n % t == 0 and t % 8 == 0:
            return t
    return n


MM_TILE = 1024


def _matmul(a, b, *, name, mode="nn", residual=None):
    (m, k) = a.shape
    n = b.shape[1] if mode == "nn" else b.shape[0]
    tm, tn, tk = _tile(m, MM_TILE), _tile(n, MM_TILE), _tile(k, MM_TILE)
    nk = k // tk
    a_spec = pl.BlockSpec((tm, tk), lambda i, j, kk: (i, kk))
    if mode == "nn":
        b_spec = pl.BlockSpec((tk, tn), lambda i, j, kk: (kk, j))
        dims = _NN
    else:
        b_spec = pl.BlockSpec((tn, tk), lambda i, j, kk: (j, kk))
        dims = _NT
    o_spec = pl.BlockSpec((tm, tn), lambda i, j, kk: (i, j))
    has_res = residual is not None

    def body(a_ref, b_ref, *rest):
        o_ref = rest[-1]
        kk = pl.program_id(2)
        part = _dot(a_ref[...].astype(BF16), b_ref[...].astype(BF16), dims)
        if has_res:
            @pl.when(kk == 0)
            def _():
                o_ref[...] = part + rest[0][...]
        else:
            @pl.when(kk == 0)
            def _():
                o_ref[...] = part

        @pl.when(kk > 0)
        def _():
            o_ref[...] += part

    return pl.pallas_call(
        body, name=name, grid=(m // tm, n // tn, nk),
        in_specs=[a_spec, b_spec] + ([o_spec] if has_res else []),
        out_specs=o_spec, out_shape=jax.ShapeDtypeStruct((m, n), F32),
        compiler_params=pltpu.CompilerParams(dimension_semantics=("parallel", "parallel", "arbitrary")),
    )(a, b, *([residual] if has_res else []))


def _rows(fn, *, name, steps, ins, outs, accs=(), scratch=()):
    ni, no, na = len(ins), len(outs), len(accs)

    def body(*refs):
        in_refs, out_refs = refs[:ni], refs[ni:ni + no]
        acc_refs, scr = refs[ni + no:ni + no + na], refs[ni + no + na:]
        i = pl.program_id(0)

        @pl.when(i == 0)
        def _():
            for r in acc_refs:
                r[...] = jnp.zeros(r.shape, r.dtype)

        fn(i, in_refs, out_refs, acc_refs, scr)

    def full(shape):
        nd = len(shape)
        return pl.BlockSpec(tuple(shape), lambda i: (0,) * nd)

    res = pl.pallas_call(
        body, name=name, grid=(steps,),
        in_specs=[pl.BlockSpec(bs, im) for _, bs, im in ins],
        out_specs=[pl.BlockSpec(bs, im) for _, _, bs, im in outs] + [full(s) for s, _ in accs],
        out_shape=[jax.ShapeDtypeStruct(s, d) for s, d, _, _ in outs] + [jax.ShapeDtypeStruct(s, d) for s, d in accs],
        scratch_shapes=list(scratch),
        compiler_params=pltpu.CompilerParams(dimension_semantics=("arbitrary",)),
    )(*[a for a, _, _ in ins])
    return res


def _rb(arr, bm, cb=0, width=None):
    w = arr.shape[1] if width is None else width
    return (arr, (bm, w), lambda i: (i, cb))


def _const(arr):
    nd = arr.ndim
    return (arr, tuple(arr.shape), lambda i: (0,) * nd)


def _ro(t, w, dtype, bm):
    return ((t, w), dtype, (bm, w), lambda i: (i, 0))


def _rmsnorm(x, g, *, name, bm=512):
    t, d = x.shape
    bm = _tile(t, bm)

    def fn(i, ins, outs, accs, scr):
        outs[0][...] = _rms_fwd(ins[0][...], ins[1][...]).astype(BF16)

    return _rows(fn, name=name, steps=t // bm, ins=[_rb(x, bm), _const(g)], outs=[_ro(t, d, BF16, bm)])[0]


def _rmsnorm_bwd(dh, x, g, dres, *, name, bm=512):
    t, d = x.shape
    bm = _tile(t, bm)

    def fn(i, ins, outs, accs, scr):
        dx, dgrow = _rms_bwd(ins[0][...], ins[1][...], ins[2][...])
        outs[0][...] = ins[3][...] + dx
        accs[0][...] += _colsum(dgrow)

    return _rows(fn, name=name, steps=t // bm, ins=[_rb(dh, bm), _rb(x, bm), _const(g), _rb(dres, bm)],
                 outs=[_ro(t, d, F32, bm)], accs=[((1, d), F32)])


def _gate(o, p, gcb, *, name, bm=512):
    t, w = o.shape
    bm = _tile(t, bm)

    def fn(i, ins, outs, accs, scr):
        outs[0][...] = (ins[0][...] * _silu(ins[1][...])).astype(BF16)

    return _rows(fn, name=name, steps=t // bm, ins=[_rb(o, bm), _rb(p, bm, gcb, w)], outs=[_ro(t, w, BF16, bm)])[0]


def _gate_bwd(dy, o, p, gcb, *, name, bm=512):
    t, w = o.shape
    bm = _tile(t, bm)

    def fn(i, ins, outs, accs, scr):
        dy_, o_, g_ = ins[0][...], ins[1][...], ins[2][...]
        outs[0][...] = dy_ * _silu(g_)
        outs[1][...] = dy_ * o_ * _dsilu(g_)

    return _rows(fn, name=name, steps=t // bm, ins=[_rb(dy, bm), _rb(o, bm), _rb(p, bm, gcb, w)],
                 outs=[_ro(t, w, F32, bm), _ro(t, w, F32, bm)])


def _loss_head(x, g, tgt, *, name, bm=512):
    t, d = x.shape
    bm = _tile(t, bm)

    def fn(i, ins, outs, accs, scr):
        x_, g_, tg = ins[0][...], ins[1][...], ins[2][...]
        err = _rms_fwd(x_, g_) - tg
        part = 0.5 * jnp.sum(jnp.sum(err * err, axis=-1, keepdims=True), axis=0, keepdims=True) / d
        dx, dgrow = _rms_bwd(err / d, x_, g_)
        outs[0][...] = dx
        accs[0][...] += _colsum(dgrow)
        accs[1][...] += jnp.broadcast_to(part, (1, BLK))

    return _rows(fn, name=name, steps=t // bm, ins=[_rb(x, bm), _const(g), _rb(tgt, bm)],
                 outs=[_ro(t, d, F32, bm)], accs=[((1, d), F32), ((1, BLK), F32)])


def _gmlp_mix_weights(ws_ref, g):
    row = lax.broadcasted_iota(jnp.int32, (BLK, BLK), 0)
    col = lax.broadcasted_iota(jnp.int32, (BLK, BLK), 1)
    tril = col <= row
    return jnp.where(tril, ws_ref[g], 0.0), tril


def _gmlp_fwd(p, ln_g, ln_b, w_s, bs_t, *, name):
    t = p.shape[0]

    def fn(i, ins, outs, accs, scr):
        p_ref, lg, lb, ws_ref, bst = ins
        vn = _ln_stats(_gelu(p_ref[:, GM_W:2 * GM_W]))[0] * lg[...] + lb[...]
        for g in range(GM_G):
            cs = slice(g * BLK, (g + 1) * BLK)
            wt, _ = _gmlp_mix_weights(ws_ref, g)
            s = _dot(wt.astype(BF16), vn[:, cs].astype(BF16)) + bst[:, g:g + 1]
            u = _gelu(p_ref[:, cs])
            gate = p_ref[:, 2 * GM_W + g * BLK:2 * GM_W + (g + 1) * BLK]
            outs[0][:, cs] = (u * s * _silu(gate)).astype(BF16)

    return _rows(fn, name=name, steps=t // BLK, ins=[_rb(p, BLK), _const(ln_g), _const(ln_b), _const(w_s), _const(bs_t)],
                 outs=[_ro(t, GM_W, BF16, BLK)])[0]


def _gmlp_bwd(dy, p, ln_g, ln_b, w_s, bs_t, *, name):
    t = p.shape[0]

    def fn(i, ins, outs, accs, scr):
        dy_ref, p_ref, lg, lb, ws_ref, bst = ins
        dp_ref = outs[0]
        dlg, dlb, dws, dbst = accs
        dvn_ref = scr[0]
        v_pre = p_ref[:, GM_W:2 * GM_W]
        xh, r = _ln_stats(_gelu(v_pre))
        vn = xh * lg[...] + lb[...]
        for g in range(GM_G):
            cs = slice(g * BLK, (g + 1) * BLK)
            gs = slice(2 * GM_W + g * BLK, 2 * GM_W + (g + 1) * BLK)
            wt, tril = _gmlp_mix_weights(ws_ref, g)
            vg = vn[:, cs].astype(BF16)
            s = _dot(wt.astype(BF16), vg) + bst[:, g:g + 1]
            u_pre, gate, dyg = p_ref[:, cs], p_ref[:, gs], dy_ref[:, cs]
            u = _gelu(u_pre)
            dos = dyg * _silu(gate)
            dp_ref[:, gs] = dyg * u * s * _dsilu(gate)
            dp_ref[:, cs] = dos * s * _dgelu(u_pre)
            ds = (dos * u).astype(BF16)
            dws[g] += jnp.where(tril, _dot(ds, vg, _NT), 0.0)
            dbst[:, g:g + 1] += jnp.sum(dos * u, axis=1, keepdims=True)
            dvn_ref[:, cs] = _dot(wt.astype(BF16), ds, _TN)
        dvn = dvn_ref[...]
        dlg[...] += _colsum(dvn * xh)
        dlb[...] += _colsum(dvn)
        dp_ref[:, GM_W:2 * GM_W] = _ln_bwd(dvn, xh, r, lg[...]) * _dgelu(v_pre)

    return _rows(fn, name=name, steps=t // BLK,
                 ins=[_rb(dy, BLK), _rb(p, BLK), _const(ln_g), _const(ln_b), _const(w_s), _const(bs_t)],
                 outs=[_ro(t, 3 * GM_W, F32, BLK)],
                 accs=[((1, GM_W), F32), ((1, GM_W), F32), ((GM_G, BLK, BLK), F32), ((BLK, GM_G), F32)],
                 scratch=[pltpu.VMEM((BLK, GM_W), F32)])


CV_BM = 128
CV_RC = 8
SUBLANES = 8
CV_FWD_OFFS = [HALO - (CV_K - 1) + k for k in range(CV_K)]
CV_BWD_OFFS = [CV_K - 1 - k for k in range(CV_K)]


def _conv_halo_prev(p, cb, bm):
    per = bm // HALO
    return (p, (HALO, CV_W), lambda i: (jnp.maximum(i * per - 1, 0), cb))


def _conv_scratch(bm):
    return [pltpu.VMEM((bm + HALO, CV_W), F32), pltpu.VMEM((SUBLANES - 1, bm + HALO - SUBLANES, CV_W), F32),
            pltpu.VMEM((bm, CV_W), F32)]


def _conv_shift_copies(ext_ref, sh_ref):
    rows = sh_ref.shape[1]
    for b in range(1, SUBLANES):
        sh_ref[b - 1] = ext_ref[pl.ds(b, rows), :]


def _conv_window(ext_ref, sh_ref, off, r0, rows):
    b = off % SUBLANES
    src = ext_ref if b == 0 else sh_ref.at[b - 1]
    return src[pl.ds(r0 + (off - b), rows), :]


def _conv_taps(ext_ref, sh_ref, cw_ref, y_ref, offs):
    bm = y_ref.shape[0]

    def chunk(ci, c):
        r0 = pl.multiple_of(ci * CV_RC, CV_RC)
        acc = jnp.zeros((CV_RC, CV_W), F32)
        for k in range(CV_K):
            acc = acc + cw_ref[pl.ds(k * SUBLANES, CV_RC), :] * _conv_window(ext_ref, sh_ref, offs[k], r0, CV_RC)
        y_ref[pl.ds(r0, CV_RC), :] = acc
        return c

    lax.fori_loop(0, bm // CV_RC, chunk, 0)


def _conv_dweights(dy1_ref, ext_ref, sh_ref, dcw_ref):
    bm = dy1_ref.shape[0]
    for k in range(CV_K):
        def chunk(ci, acc, off=CV_FWD_OFFS[k]):
            r0 = pl.multiple_of(ci * CV_RC, CV_RC)
            return acc + dy1_ref[pl.ds(r0, CV_RC), :] * _conv_window(ext_ref, sh_ref, off, r0, CV_RC)

        dcw_ref[k:k + 1, :] += _colsum(lax.fori_loop(0, bm // CV_RC, chunk, jnp.zeros((CV_RC, CV_W), F32)))


def _conv_fill(i, ext_ref, a_prev, b_prev, a, b, bm, seq):
    keep = jnp.where((i % (seq // bm)) == 0, 0.0, 1.0)
    ext_ref[pl.ds(0, HALO), :] = keep * (a_prev * _sigmoid(b_prev))
    ext_ref[pl.ds(HALO, bm), :] = a * _sigmoid(b)


def _conv_fwd(p, cw, cb, ln_g, ln_b, seq, *, name, bm=CV_BM):
    t = p.shape[0]

    def fn(i, ins, outs, accs, scr):
        a, b, gate, ap, bp = [r[...] for r in ins[:5]]
        cw_ref, cb_, lg, lb = ins[5], ins[6][...], ins[7][...], ins[8][...]
        ext, sh, y = scr
        _conv_fill(i, ext, ap, bp, a, b, bm, seq)
        _conv_shift_copies(ext, sh)
        _conv_taps(ext, sh, cw_ref, y, CV_FWD_OFFS)
        y2 = _ln_stats(y[...] + cb_)[0] * lg + lb
        outs[0][...] = (_silu(y2) * _silu(gate)).astype(BF16)

    return _rows(fn, name=name, steps=t // bm,
                 ins=[_rb(p, bm, 0, CV_W), _rb(p, bm, 1, CV_W), _rb(p, bm, 2, CV_W),
                      _conv_halo_prev(p, 0, bm), _conv_halo_prev(p, 1, bm),
                      _const(cw), _const(cb), _const(ln_g), _const(ln_b)],
                 outs=[_ro(t, CV_W, BF16, bm)], scratch=_conv_scratch(bm))[0]


def _conv_bwd_post(dy, p, cw, cb, ln_g, ln_b, seq, *, name, bm=CV_BM):
    t = p.shape[0]

    def fn(i, ins, outs, accs, scr):
        dy_, a, b, gate, ap, bp = [r[...] for r in ins[:6]]
        cw_ref, cb_, lg, lb = ins[6], ins[7][...], ins[8][...], ins[9][...]
        dlg, dlb, dcb, dcw = accs
        ext, sh, y = scr
        _conv_fill(i, ext, ap, bp, a, b, bm, seq)
        _conv_shift_copies(ext, sh)
        _conv_taps(ext, sh, cw_ref, y, CV_FWD_OFFS)
        xh, r = _ln_stats(y[...] + cb_)
        y2 = xh * lg + lb
        outs[1][...] = dy_ * _silu(y2) * _dsilu(gate)
        dy2 = dy_ * _silu(gate) * _dsilu(y2)
        dlg[...] += _colsum(dy2 * xh)
        dlb[...] += _colsum(dy2)
        dy1 = _ln_bwd(dy2, xh, r, lg)
        outs[0][...] = dy1
        dcb[...] += _colsum(dy1)
        _conv_dweights(outs[0], ext, sh, dcw)

    return _rows(fn, name=name, steps=t // bm,
                 ins=[_rb(dy, bm), _rb(p, bm, 0, CV_W), _rb(p, bm, 1, CV_W), _rb(p, bm, 2, CV_W),
                      _conv_halo_prev(p, 0, bm), _conv_halo_prev(p, 1, bm),
                      _const(cw), _const(cb), _const(ln_g), _const(ln_b)],
                 outs=[_ro(t, CV_W, F32, bm), _ro(t, CV_W, F32, bm)],
                 accs=[((1, CV_W), F32), ((1, CV_W), F32), ((1, CV_W), F32), ((CV_K, CV_W), F32)],
                 scratch=_conv_scratch(bm))


def _conv_bwd_pre(dy1, dgate, p, cw, seq, *, name, bm=CV_BM):
    t = p.shape[0]
    per = bm // HALO
    last_halo = t // HALO - 1

    def fn(i, ins, outs, accs, scr):
        d1, d1n, dg, a, b = [r[...] for r in ins[:5]]
        ext, sh, y = scr
        keep = jnp.where((i % (seq // bm)) == (seq // bm - 1), 0.0, 1.0)
        ext[pl.ds(0, bm), :] = d1
        ext[pl.ds(bm, HALO), :] = keep * d1n
        _conv_shift_copies(ext, sh)
        _conv_taps(ext, sh, ins[5], y, CV_BWD_OFFS)
        dy0 = y[...]
        sb = _sigmoid(b)
        outs[0][:, 0:CV_W] = dy0 * sb
        outs[0][:, CV_W:2 * CV_W] = dy0 * a * sb * (1.0 - sb)
        outs[0][:, 2 * CV_W:3 * CV_W] = dg

    return _rows(fn, name=name, steps=t // bm,
                 ins=[_rb(dy1, bm), (dy1, (HALO, CV_W), lambda i: (jnp.minimum((i + 1) * per, last_halo), 0)),
                      _rb(dgate, bm), _rb(p, bm, 0, CV_W), _rb(p, bm, 1, CV_W), _const(cw)],
                 outs=[_ro(t, 3 * CV_W, F32, bm)], scratch=_conv_scratch(bm))[0]


def _iotas():
    row = lax.broadcasted_iota(jnp.int32, (BLK, BLK), 0)
    col = lax.broadcasted_iota(jnp.int32, (BLK, BLK), 1)
    return row, col


def _heads(x, head0):
    if head0.shape != x.shape:
        head0 = lax.broadcasted_iota(jnp.int32, x.shape, 1) < HEAD_DIM
    return jnp.where(head0, x, 0.0).astype(BF16), jnp.where(head0, 0.0, x).astype(BF16)


def _pair_spec(seq, off):
    return pl.BlockSpec((seq, BLK), lambda b, hp: (b, off + hp))


def _stat_spec(seq):
    return pl.BlockSpec((None, None, seq, BLK), lambda b, hp: (b, hp, 0, 0))


_ATT_PARAMS = dict(compiler_params=pltpu.CompilerParams(dimension_semantics=("parallel", "parallel")))
_SCALE = 1.0 / math.sqrt(HEAD_DIM)


KEY_BLOCK = 512


def _stack_heads(x, head0, scale=None):
    if scale is not None:
        x = x * scale
    return jnp.concatenate(_heads(x, head0), axis=0)


def _pair_cols(x, head0, fill):
    a = jnp.max(jnp.where(head0, x, fill), axis=1, keepdims=True)
    b = jnp.max(jnp.where(head0, fill, x), axis=1, keepdims=True)
    return jnp.concatenate([a, b], axis=0)


def _causal_mask(t0, s0, kw, inclusive):
    row = lax.broadcasted_iota(jnp.int32, (2 * BLK, kw), 0) & (BLK - 1)
    col = lax.broadcasted_iota(jnp.int32, (2 * BLK, kw), 1)
    return (s0 + col) <= (t0 + row) if inclusive else (s0 + col) < (t0 + row)


def _sub(x, j):
    return x[:, j * BLK:(j + 1) * BLK]


def _block_cumsum(x, tri, ksub):
    parts = _split3(x)
    cs = _dot(jnp.concatenate([_sub(pt, j) for pt in parts for j in range(ksub)], axis=0), tri)
    n = 2 * BLK
    return [cs[j * n:(j + 1) * n] + cs[(ksub + j) * n:(ksub + j + 1) * n] + cs[(2 * ksub + j) * n:(2 * ksub + j + 1) * n]
            for j in range(ksub)]


def _sb_terms(qs, k, mask):
    z = _dot(qs, k, _NT)
    t = jnp.log(1.0 + jnp.exp(-jnp.abs(z)))
    lsz = jnp.minimum(z, 0.0) - t
    lr = jnp.minimum(-z, 0.0) - t
    if mask is not None:
        lr = jnp.where(mask, lr, 0.0)
    return lsz, lr


def _sb_fwd(p, nb, seq, *, name):
    nq = seq // BLK
    kw = min(KEY_BLOCK, seq)
    ksub = kw // BLK

    def body(q_ref, k_ref, v_ref, o_ref, tot_ref):
        row, col = _iotas()
        head0 = col < HEAD_DIM
        upper = (row > col).astype(BF16)

        def qblock(qb, c):
            t0 = pl.multiple_of(qb * BLK, BLK)
            qs = _stack_heads(q_ref[pl.ds(t0, BLK), :], head0, _SCALE)
            diag = qb // ksub

            def kblock(kb, carry, masked):
                acc, run = carry
                s0 = pl.multiple_of(kb * kw, kw)
                k = k_ref[pl.ds(s0, kw), :].astype(BF16)
                v0, v1 = _heads(v_ref[pl.ds(s0, kw), :], head0)
                mask = _causal_mask(t0, s0, kw, False) if masked else None
                lsz, lr = _sb_terms(qs, k, mask)
                inblock = _block_cumsum(lr, upper, ksub)
                ws = [None] * ksub
                for j in reversed(range(ksub)):
                    w = jnp.exp(_sub(lsz, j) + inblock[j] + run)
                    if masked:
                        w = jnp.where(_sub(mask, j), w, 0.0)
                    ws[j] = w.astype(BF16)
                    run = run + jnp.sum(_sub(lr, j), axis=1, keepdims=True)
                w = jnp.concatenate(ws, axis=1)
                return acc + _dot(w[:BLK], v0) + _dot(w[BLK:], v1), run

            carry = kblock(diag, (jnp.zeros((BLK, BLK), F32), jnp.zeros((2 * BLK, 1), F32)), True)
            acc, run = lax.fori_loop(0, diag, lambda it, cr: kblock(diag - 1 - it, cr, False), carry)
            o_ref[pl.ds(t0, BLK), :] = acc
            tot_ref[pl.ds(t0, BLK), :] = jnp.where(head0, run[:BLK], run[BLK:])
            return c

        lax.fori_loop(0, nq, qblock, 0)

    return pl.pallas_call(
        body, name=name, grid=(nb, PAIRS),
        in_specs=[_pair_spec(seq, 0), _pair_spec(seq, PAIRS), _pair_spec(seq, 2 * PAIRS)],
        out_specs=[_pair_spec(seq, 0), _stat_spec(seq)],
        out_shape=[jax.ShapeDtypeStruct((nb * seq, D_MODEL), F32), jax.ShapeDtypeStruct((nb, PAIRS, seq, BLK), F32)],
        **_ATT_PARAMS,
    )(p, p, p)


def _sb_bwd(p, do, tot, nb, seq, *, name):
    nq = seq // BLK
    kw = min(KEY_BLOCK, seq)
    ksub = kw // BLK

    def body(q_ref, k_ref, v_ref, do_ref, tot_ref, dq_ref, dk_ref, dv_ref):
        row, col = _iotas()
        head0 = col < HEAD_DIM
        lower_incl = (row <= col).astype(BF16)
        lower_strict = (row < col).astype(BF16)
        dk_ref[...] = jnp.zeros(dk_ref.shape, F32)
        dv_ref[...] = jnp.zeros(dv_ref.shape, F32)

        def qblock(qb, c):
            t0 = pl.multiple_of(qb * BLK, BLK)
            qs = _stack_heads(q_ref[pl.ds(t0, BLK), :], head0, _SCALE)
            dos = _stack_heads(do_ref[pl.ds(t0, BLK), :], head0)
            tt = _pair_cols(tot_ref[pl.ds(t0, BLK), :], head0, -jnp.inf)
            diag = qb // ksub

            def kblock(kb, carry, masked):
                dq, pf, ef = carry
                s0 = pl.multiple_of(kb * kw, kw)
                kf = k_ref[pl.ds(s0, kw), :]
                k = kf.astype(BF16)
                k0, k1 = _heads(kf, head0)
                v = v_ref[pl.ds(s0, kw), :].astype(BF16)
                mask = _causal_mask(t0, s0, kw, False) if masked else None
                lsz, lr = _sb_terms(qs, k, None)
                lrm = jnp.where(mask, lr, 0.0) if masked else lr
                incl = _block_cumsum(lrm, lower_incl, ksub)
                dw = _dot(dos, v, _NT)
                ws, ews = [], []
                for j in range(ksub):
                    w = jnp.exp(_sub(lsz, j) + (tt - pf - incl[j]))
                    if masked:
                        w = jnp.where(_sub(mask, j), w, 0.0)
                    pf = pf + jnp.sum(_sub(lrm, j), axis=1, keepdims=True)
                    ws.append(w)
                    ews.append(_sub(dw, j) * w)
                before = _block_cumsum(jnp.concatenate(ews, axis=1), lower_strict, ksub)
                dzs = []
                for j in range(ksub):
                    dlr = ef + before[j]
                    ef = ef + jnp.sum(ews[j], axis=1, keepdims=True)
                    dz = ews[j] * jnp.exp(_sub(lr, j)) - dlr * jnp.exp(_sub(lsz, j))
                    if masked:
                        dz = jnp.where(_sub(mask, j), dz, 0.0)
                    dzs.append(dz.astype(BF16))
                dz = jnp.concatenate(dzs, axis=1)
                w = jnp.concatenate([w.astype(BF16) for w in ws], axis=1)
                dk_ref[pl.ds(s0, kw), :] += _dot(dz, qs, _TN)
                dv_ref[pl.ds(s0, kw), :] += _dot(w, dos, _TN)
                return dq + _dot(dz[:BLK], k0) + _dot(dz[BLK:], k1), pf, ef

            zc = jnp.zeros((2 * BLK, 1), F32)
            carry = lax.fori_loop(0, diag, lambda kb, cr: kblock(kb, cr, False), (jnp.zeros((BLK, BLK), F32), zc, zc))
            dq_ref[pl.ds(t0, BLK), :] = kblock(diag, carry, True)[0] * _SCALE
            return c

        lax.fori_loop(0, nq, qblock, 0)

    t = nb * seq
    return pl.pallas_call(
        body, name=name, grid=(nb, PAIRS),
        in_specs=[_pair_spec(seq, 0), _pair_spec(seq, PAIRS), _pair_spec(seq, 2 * PAIRS), _pair_spec(seq, 0), _stat_spec(seq)],
        out_specs=[_pair_spec(seq, 0)] * 3,
        out_shape=[jax.ShapeDtypeStruct((t, D_MODEL), F32)] * 3,
        **_ATT_PARAMS,
    )(p, p, p, do, tot)


def _fox_cum(f, bf, nb, seq, *, name):
    def body(f_ref, bf_ref, cc_ref, cr_ref):
        row, col = _iotas()
        lower = (col <= row).astype(BF16)
        carry = jnp.zeros((1, BLK), F32)
        for blk in range(seq // BLK):
            rs = slice(blk * BLK, (blk + 1) * BLK)
            lf = jnp.where(col < HEADS, _log_sigmoid(f_ref[rs, :] + bf_ref[...]), 0.0)
            cc = _dot3_left(lower, lf) + carry
            cc_ref[rs, :] = cc
            cr_ref[:, rs] = cc.T[0:HEADS, :]
            carry = carry + _colsum(lf)

    return pl.pallas_call(
        body, name=name, grid=(nb,),
        in_specs=[pl.BlockSpec((seq, BLK), lambda b: (b, 0)), pl.BlockSpec((1, BLK), lambda b: (0, 0))],
        out_specs=[pl.BlockSpec((seq, BLK), lambda b: (b, 0)), pl.BlockSpec((None, HEADS, seq), lambda b: (b, 0, 0))],
        out_shape=[jax.ShapeDtypeStruct((nb * seq, BLK), F32), jax.ShapeDtypeStruct((nb, HEADS, seq), F32)],
        compiler_params=pltpu.CompilerParams(dimension_semantics=("parallel",)),
    )(f, bf)


def _fox_cum_bwd(dcr, dcc, f, bf, nb, seq, *, name):
    def body(dcr_ref, dcc_ref, f_ref, bf_ref, df_ref, dbf_ref):
        row, col = _iotas()
        upper_incl = (col >= row).astype(BF16)

        @pl.when(pl.program_id(0) == 0)
        def _():
            dbf_ref[...] = jnp.zeros((1, BLK), F32)

        carry = jnp.zeros((1, BLK), F32)
        for blk in reversed(range(seq // BLK)):
            rs = slice(blk * BLK, (blk + 1) * BLK)
            dc = dcr_ref[:, rs].T + dcc_ref[rs, :]
            dlf = _dot3_left(upper_incl, dc) + carry
            carry = carry + _colsum(dc)
            fl = f_ref[rs, :] + bf_ref[...]
            df = jnp.where(col < HEADS, dlf * _sigmoid(-fl), 0.0)
            df_ref[rs, :] = df
            dbf_ref[...] += _colsum(df)

    return pl.pallas_call(
        body, name=name, grid=(nb,),
        in_specs=[pl.BlockSpec((None, BLK, seq), lambda b: (b, 0, 0)), pl.BlockSpec((seq, BLK), lambda b: (b, 0)),
                  pl.BlockSpec((seq, BLK), lambda b: (b, 0)), pl.BlockSpec((1, BLK), lambda b: (0, 0))],
        out_specs=[pl.BlockSpec((seq, BLK), lambda b: (b, 0)), pl.BlockSpec((1, BLK), lambda b: (0, 0))],
        out_shape=[jax.ShapeDtypeStruct((nb * seq, BLK), F32), jax.ShapeDtypeStruct((1, BLK), F32)],
        compiler_params=pltpu.CompilerParams(dimension_semantics=("arbitrary",)),
    )(dcr, dcc, f, bf)


def _fox_cum_cols(cc_ref, t0, col, hp):
    cc = cc_ref[pl.ds(t0, BLK), :]
    c0 = jnp.sum(jnp.where(col == 2 * hp, cc, 0.0), axis=1, keepdims=True)
    c1 = jnp.sum(jnp.where(col == 2 * hp + 1, cc, 0.0), axis=1, keepdims=True)
    return c0, c1


def _fox_bias(c0, c1, cr_ref, s0, kw):
    return jnp.concatenate([c0 - cr_ref[0:1, pl.ds(s0, kw)], c1 - cr_ref[1:2, pl.ds(s0, kw)]], axis=0)


def _fox_fwd(p, cc, cr, nb, seq, *, name):
    nq = seq // BLK
    kw = min(KEY_BLOCK, seq)
    ksub = kw // BLK

    def body(q_ref, k_ref, v_ref, cc_ref, cr_ref, o_ref, lse_ref):
        hp = pl.program_id(1)
        row, col = _iotas()
        head0 = col < HEAD_DIM

        def qblock(qb, c):
            t0 = pl.multiple_of(qb * BLK, BLK)
            qs = _stack_heads(q_ref[pl.ds(t0, BLK), :], head0, _SCALE)
            c0, c1 = _fox_cum_cols(cc_ref, t0, col, hp)
            diag = qb // ksub

            def kblock(kb, carry, masked):
                acc, m, l = carry
                s0 = pl.multiple_of(kb * kw, kw)
                k = k_ref[pl.ds(s0, kw), :].astype(BF16)
                v0, v1 = _heads(v_ref[pl.ds(s0, kw), :], head0)
                s = _dot(qs, k, _NT) + _fox_bias(c0, c1, cr_ref, s0, kw)
                if masked:
                    s = jnp.where(_causal_mask(t0, s0, kw, True), s, -jnp.inf)
                m_new = jnp.maximum(m, jnp.max(s, axis=1, keepdims=True))
                pr = jnp.exp(s - m_new)
                alpha = jnp.exp(m - m_new)
                l = alpha * l + jnp.sum(pr, axis=1, keepdims=True)
                pr = pr.astype(BF16)
                acc = acc * jnp.where(head0, alpha[:BLK], alpha[BLK:]) + _dot(pr[:BLK], v0) + _dot(pr[BLK:], v1)
                return acc, m_new, l

            init = (jnp.zeros((BLK, BLK), F32), jnp.full((2 * BLK, 1), -jnp.inf, F32), jnp.zeros((2 * BLK, 1), F32))
            carry = lax.fori_loop(0, diag, lambda kb, cr: kblock(kb, cr, False), init)
            acc, m, l = kblock(diag, carry, True)
            o_ref[pl.ds(t0, BLK), :] = acc / jnp.where(head0, l[:BLK], l[BLK:])
            lse = m + jnp.log(l)
            lse_ref[pl.ds(t0, BLK), :] = jnp.where(head0, lse[:BLK], lse[BLK:])
            return c

        lax.fori_loop(0, nq, qblock, 0)

    return pl.pallas_call(
        body, name=name, grid=(nb, PAIRS),
        in_specs=[_pair_spec(seq, 0), _pair_spec(seq, PAIRS), _pair_spec(seq, 2 * PAIRS),
                  pl.BlockSpec((seq, BLK), lambda b, hp: (b, 0)), pl.BlockSpec((None, None, 8, seq), lambda b, hp: (b, hp, 0, 0))],
        out_specs=[_pair_spec(seq, 0), _stat_spec(seq)],
        out_shape=[jax.ShapeDtypeStruct((nb * seq, D_MODEL), F32), jax.ShapeDtypeStruct((nb, PAIRS, seq, BLK), F32)],
        **_ATT_PARAMS,
    )(p, p, p, cc, cr)


def _fox_bwd(p, do, o, lse, cc, cr, nb, seq, *, name):
    nq = seq // BLK
    kw = min(KEY_BLOCK, seq)
    ksub = kw // BLK

    def body(q_ref, k_ref, v_ref, do_ref, o_ref, lse_ref, cc_ref, cr_ref, dq_ref, dk_ref, dv_ref, dcr_ref, dcc_ref):
        hp = pl.program_id(1)
        row, col = _iotas()
        head0 = col < HEAD_DIM
        dk_ref[...] = jnp.zeros(dk_ref.shape, F32)
        dv_ref[...] = jnp.zeros(dv_ref.shape, F32)
        dcr_ref[...] = jnp.zeros(dcr_ref.shape, F32)

        def qblock(qb, c):
            t0 = pl.multiple_of(qb * BLK, BLK)
            qs = _stack_heads(q_ref[pl.ds(t0, BLK), :], head0, _SCALE)
            dof = do_ref[pl.ds(t0, BLK), :]
            dos = _stack_heads(dof, head0)
            prod = dof * o_ref[pl.ds(t0, BLK), :]
            dl = jnp.concatenate([jnp.sum(jnp.where(head0, prod, 0.0), axis=1, keepdims=True),
                                  jnp.sum(jnp.where(head0, 0.0, prod), axis=1, keepdims=True)], axis=0)
            lse = _pair_cols(lse_ref[pl.ds(t0, BLK), :], head0, -jnp.inf)
            c0, c1 = _fox_cum_cols(cc_ref, t0, col, hp)
            diag = qb // ksub

            def kblock(kb, carry, masked):
                dq, rs = carry
                s0 = pl.multiple_of(kb * kw, kw)
                kf = k_ref[pl.ds(s0, kw), :]
                k = kf.astype(BF16)
                k0, k1 = _heads(kf, head0)
                v = v_ref[pl.ds(s0, kw), :].astype(BF16)
                pr = jnp.exp(_dot(qs, k, _NT) + _fox_bias(c0, c1, cr_ref, s0, kw) - lse)
                if masked:
                    pr = jnp.where(_causal_mask(t0, s0, kw, True), pr, 0.0)
                ds = pr * (_dot(dos, v, _NT) - dl)
                dcr_ref[0:1, pl.ds(s0, kw)] -= _colsum(ds[:BLK])
                dcr_ref[1:2, pl.ds(s0, kw)] -= _colsum(ds[BLK:])
                rs = rs + jnp.sum(ds, axis=1, keepdims=True)
                ds = ds.astype(BF16)
                dk_ref[pl.ds(s0, kw), :] += _dot(ds, qs, _TN)
                dv_ref[pl.ds(s0, kw), :] += _dot(pr.astype(BF16), dos, _TN)
                return dq + _dot(ds[:BLK], k0) + _dot(ds[BLK:], k1), rs

            init = (jnp.zeros((BLK, BLK), F32), jnp.zeros((2 * BLK, 1), F32))
            carry = lax.fori_loop(0, diag, lambda kb, cr: kblock(kb, cr, False), init)
            dq, rs = kblock(diag, carry, True)
            dq_ref[pl.ds(t0, BLK), :] = dq * _SCALE
            dcc_ref[pl.ds(t0, BLK), :] = jnp.where(head0, rs[:BLK], rs[BLK:])
            return c

        lax.fori_loop(0, nq, qblock, 0)

    t = nb * seq
    return pl.pallas_call(
        body, name=name, grid=(nb, PAIRS),
        in_specs=[_pair_spec(seq, 0), _pair_spec(seq, PAIRS), _pair_spec(seq, 2 * PAIRS), _pair_spec(seq, 0), _pair_spec(seq, 0),
                  _stat_spec(seq), pl.BlockSpec((seq, BLK), lambda b, hp: (b, 0)),
                  pl.BlockSpec((None, None, 8, seq), lambda b, hp: (b, hp, 0, 0))],
        out_specs=[_pair_spec(seq, 0)] * 3 + [pl.BlockSpec((None, None, 8, seq), lambda b, hp: (b, hp, 0, 0)), _stat_spec(seq)],
        out_shape=[jax.ShapeDtypeStruct((t, D_MODEL), F32)] * 3 + [jax.ShapeDtypeStruct((nb, PAIRS, 8, seq), F32),
                                                                     jax.ShapeDtypeStruct((nb, PAIRS, seq, BLK), F32)],
        **_ATT_PARAMS,
    )(p, p, p, do, o, lse, cc, cr)


def _local_step(x3, tgt3, w):
    nb, seq, d = x3.shape
    t = nb * seq
    x0, tgt = x3.reshape(t, d), tgt3.reshape(t, d)
    g = {}

    a_gain = w["a_norm"].reshape(1, d)
    h_a = _rmsnorm(x0, a_gain, name="a_norm_fwd")
    p_a = _matmul(h_a, w["a_w_in"], name="a_in_fwd")
    o_a, tot_a = _sb_fwd(p_a, nb, seq, name="a_attn_fwd")
    y_a = _gate(o_a, p_a, 3, name="a_gate_fwd")
    x1 = _matmul(y_a, w["a_w_out"], name="a_out_fwd", residual=x0)

    b_gain = w["b_norm"].reshape(1, d)
    b_lg, b_lb = w["b_v_ln_g"].reshape(1, GM_W), w["b_v_ln_b"].reshape(1, GM_W)
    b_ws, b_bst = w["b_w_s"].reshape(GM_G, BLK, BLK), w["b_b_s"].reshape(GM_G, BLK).T
    h_b = _rmsnorm(x1, b_gain, name="b_norm_fwd")
    p_b = _matmul(h_b, w["b_w_in"], name="b_in_fwd")
    y_b = _gmlp_fwd(p_b, b_lg, b_lb, b_ws, b_bst, name="b_mix_fwd")
    x2 = _matmul(y_b, w["b_w_out"], name="b_out_fwd", residual=x1)

    c_gain = w["c_norm"].reshape(1, d)
    c_cw = jnp.repeat(w["c_conv_w"].reshape(CV_K, CV_W), SUBLANES, axis=0)
    c_cb = w["c_conv_b"].reshape(1, CV_W)
    c_lg, c_lb = w["c_ln_g"].reshape(1, CV_W), w["c_ln_b"].reshape(1, CV_W)
    h_c = _rmsnorm(x2, c_gain, name="c_norm_fwd")
    p_c = _matmul(h_c, w["c_w_in"], name="c_in_fwd")
    y_c = _conv_fwd(p_c, c_cw, c_cb, c_lg, c_lb, seq, name="c_conv_fwd")
    x3_ = _matmul(y_c, w["c_w_out"], name="c_out_fwd", residual=x2)

    d_gain = w["d_norm"].reshape(1, d)
    d_win = w["d_w_in"].reshape(d, 4 * D_MODEL + HEADS)
    d_wmain = d_win[:, :4 * D_MODEL]
    d_wf = jnp.pad(d_win[:, 4 * D_MODEL:], ((0, 0), (0, BLK - HEADS)))
    d_bf = jnp.pad(w["d_b_f"].reshape(1, HEADS), ((0, 0), (0, BLK - HEADS)))
    h_d = _rmsnorm(x3_, d_gain, name="d_norm_fwd")
    p_d = _matmul(h_d, d_wmain, name="d_in_fwd")
    f_d = _matmul(h_d, d_wf, name="d_inf_fwd")
    cc, cr = _fox_cum(f_d, d_bf, nb, seq, name="d_cum_fwd")
    cr = jnp.pad(cr.reshape(nb, PAIRS, 2, seq), ((0, 0), (0, 0), (0, 6), (0, 0)))
    o_d, lse_d = _fox_fwd(p_d, cc, cr, nb, seq, name="d_attn_fwd")
    y_d = _gate(o_d, p_d, 3, name="d_gate_fwd")
    x4 = _matmul(y_d, w["d_w_out"], name="d_out_fwd", residual=x3_)

    f_gain = w["final_norm"].reshape(1, d)
    dx, g_fn, loss_row = _loss_head(x4, f_gain, tgt, name="loss_head")
    g["final_norm"] = g_fn

    g["d_w_out"] = _matmul(y_d.T, dx, name="d_out_dw")
    dy = _matmul(dx, w["d_w_out"], name="d_out_dy", mode="nt")
    do_d, dg_d = _gate_bwd(dy, o_d, p_d, 3, name="d_gate_bwd")
    dq, dk, dv, dcr, dcc = _fox_bwd(p_d, do_d, o_d, lse_d, cc, cr, nb, seq, name="d_attn_bwd")
    dcr = jnp.pad(dcr[:, :, :2, :].reshape(nb, HEADS, seq), ((0, 0), (0, BLK - HEADS), (0, 0)))
    dcc = dcc[:, :, :, ::HEAD_DIM].transpose(0, 2, 1, 3).reshape(t, HEADS)
    dcc = jnp.pad(dcc, ((0, 0), (0, BLK - HEADS)))
    df, dbf = _fox_cum_bwd(dcr, dcc, f_d, d_bf, nb, seq, name="d_cum_bwd")
    g["d_b_f"] = dbf[:, :HEADS]
    parts = [dq, dk, dv, dg_d]
    ht_d = h_d.T
    dws = [_matmul(ht_d, pt, name=f"d_in_dw{n}") for n, pt in enumerate(parts)]
    dwf = _matmul(ht_d, df, name="d_inf_dw")
    g["d_w_in"] = jnp.concatenate(dws + [dwf[:, :HEADS]], axis=1)
    dh = _matmul(df, d_wf, name="d_inf_dh", mode="nt")
    for n, pt in enumerate(parts):
        dh = _matmul(pt, d_wmain[:, n * D_MODEL:(n + 1) * D_MODEL], name=f"d_in_dh{n}", mode="nt", residual=dh)
    dx, g["d_norm"] = _rmsnorm_bwd(dh, x3_, d_gain, dx, name="d_norm_bwd")

    g["c_w_out"] = _matmul(y_c.T, dx, name="c_out_dw")
    dy = _matmul(dx, w["c_w_out"], name="c_out_dy", mode="nt")
    dy1, dgate, g["c_ln_g"], g["c_ln_b"], g["c_conv_b"], g["c_conv_w"] = _conv_bwd_post(
        dy, p_c, c_cw, c_cb, c_lg, c_lb, seq, name="c_conv_bwd_post")
    dp = _conv_bwd_pre(dy1, dgate, p_c, c_cw, seq, name="c_conv_bwd_pre")
    g["c_w_in"] = _matmul(h_c.T, dp, name="c_in_dw")
    dh = _matmul(dp, w["c_w_in"], name="c_in_dh", mode="nt")
    dx, g["c_norm"] = _rmsnorm_bwd(dh, x2, c_gain, dx, name="c_norm_bwd")

    g["b_w_out"] = _matmul(y_b.T, dx, name="b_out_dw")
    dy = _matmul(dx, w["b_w_out"], name="b_out_dy", mode="nt")
    dp, g["b_v_ln_g"], g["b_v_ln_b"], g["b_w_s"], dbst = _gmlp_bwd(dy, p_b, b_lg, b_lb, b_ws, b_bst, name="b_mix_bwd")
    g["b_b_s"] = dbst.T
    g["b_w_in"] = _matmul(h_b.T, dp, name="b_in_dw")
    dh = _matmul(dp, w["b_w_in"], name="b_in_dh", mode="nt")
    dx, g["b_norm"] = _rmsnorm_bwd(dh, x1, b_gain, dx, name="b_norm_bwd")

    g["a_w_out"] = _matmul(y_a.T, dx, name="a_out_dw")
    dy = _matmul(dx, w["a_w_out"], name="a_out_dy", mode="nt")
    do_a, dg_a = _gate_bwd(dy, o_a, p_a, 3, name="a_gate_bwd")
    dq, dk, dv = _sb_bwd(p_a, do_a, tot_a, nb, seq, name="a_attn_bwd")
    parts = [dq, dk, dv, dg_a]
    ht_a = h_a.T
    g["a_w_in"] = jnp.concatenate([_matmul(ht_a, pt, name=f"a_in_dw{n}") for n, pt in enumerate(parts)], axis=1)
    dh = None
    for n, pt in enumerate(parts):
        dh = _matmul(pt, w["a_w_in"][:, n * D_MODEL:(n + 1) * D_MODEL], name=f"a_in_dh{n}", mode="nt", residual=dh)
    dx, g["a_norm"] = _rmsnorm_bwd(dh, x0, a_gain, dx, name="a_norm_bwd")

    return loss_row[0, 0], dx.reshape(nb, seq, d), g


_HBM = pl.BlockSpec(memory_space=pltpu.HBM)


def _place():
    return lax.axis_index("x"), lax.axis_index("y"), lax.axis_index("c")


def _other_chips(x, y):
    return [(1 - x, y), (x, 1 - y), (1 - x, 1 - y)]


def _allgather_chips(s, *, name):
    r, c_ = s.shape
    h = r // 2
    assert r % 32 == 0

    def body(s_ref, o_ref, send_sems, recv_sems, local_sem):
        x, y, c = _place()
        me = 2 * x + y
        chips = _other_chips(x, y)

        def half(j, hc):
            return o_ref.at[j, pl.ds(hc * h, h), :]

        def copy(kk, src, dst, to):
            return pltpu.make_async_remote_copy(src_ref=src, dst_ref=dst, send_sem=send_sems.at[kk], recv_sem=recv_sems.at[kk],
                                                device_id=to, device_id_type=MESH)

        mine = pltpu.make_async_copy(s_ref, o_ref.at[me], local_sem)
        mine.start()
        first = [copy(kk, s_ref.at[pl.ds(c * h, h), :], half(me, c), (cx, cy, c)) for kk, (cx, cy) in enumerate(chips)]
        for cp in first:
            cp.start()
        passed = []
        for kk, (cx, cy) in enumerate(chips):
            blk = half(2 * cx + cy, c)
            copy(kk, blk, blk, (cx, cy, c)).wait_recv()
            fwd = copy(3 + kk, blk, blk, (x, y, 1 - c))
            fwd.start()
            passed.append(fwd)
        for kk, (cx, cy) in enumerate(chips):
            blk = half(2 * cx + cy, 1 - c)
            copy(3 + kk, blk, blk, (x, y, 1 - c)).wait_recv()
        for cp in first + passed:
            cp.wait_send()
        mine.wait()

    return pl.pallas_call(
        body, name=name, in_specs=[_HBM], out_specs=_HBM,
        out_shape=jax.ShapeDtypeStruct((N_CHIPS, r, c_), s.dtype),
        scratch_shapes=[pltpu.SemaphoreType.DMA((6,)), pltpu.SemaphoreType.DMA((6,)), pltpu.SemaphoreType.DMA],
    )(s)


def _dma_sems(n):
    return [pltpu.SemaphoreType.DMA((n,)), pltpu.SemaphoreType.DMA((n,))]


def _swap_halves(gps, *, name):
    n_ops = len(gps)

    def body(*refs):
        g_refs, o_refs, (send_sems, recv_sems) = refs[:n_ops], refs[n_ops:2 * n_ops], refs[2 * n_ops:]
        x, y, c = _place()
        cps = []
        for i, (g_ref, o_ref) in enumerate(zip(g_refs, o_refs)):
            h = g_ref.shape[1] // 2
            cps.append(pltpu.make_async_remote_copy(
                src_ref=g_ref.at[:, pl.ds((1 - c) * h, h), :], dst_ref=o_ref, send_sem=send_sems.at[i], recv_sem=recv_sems.at[i],
                device_id=(x, y, 1 - c), device_id_type=MESH))
        for cp in cps:
            cp.start()
        for cp in cps:
            cp.wait()

    return pl.pallas_call(
        body, name=name, in_specs=[_HBM] * n_ops, out_specs=[_HBM] * n_ops,
        out_shape=[jax.ShapeDtypeStruct((g.shape[0], g.shape[1] // 2, g.shape[2]), g.dtype) for g in gps],
        scratch_shapes=_dma_sems(n_ops),
    )(*gps)


def _scatter_chips(hps, *, name):
    n_ops = len(hps)

    def body(*refs):
        h_refs, o_refs, (send_sems, recv_sems) = refs[:n_ops], refs[n_ops:2 * n_ops], refs[2 * n_ops:]
        x, y, c = _place()
        cps = [pltpu.make_async_remote_copy(src_ref=h_ref.at[2 * cx + cy], dst_ref=o_ref.at[kk], send_sem=send_sems.at[3 * i + kk],
                                            recv_sem=recv_sems.at[3 * i + kk], device_id=(cx, cy, c), device_id_type=MESH)
               for i, (h_ref, o_ref) in enumerate(zip(h_refs, o_refs)) for kk, (cx, cy) in enumerate(_other_chips(x, y))]
        for cp in cps:
            cp.start()
        for cp in cps:
            cp.wait()

    return pl.pallas_call(
        body, name=name, in_specs=[_HBM] * n_ops, out_specs=[_HBM] * n_ops,
        out_shape=[jax.ShapeDtypeStruct((3,) + hp.shape[1:], hp.dtype) for hp in hps],
        scratch_shapes=_dma_sems(3 * n_ops),
    )(*hps)


def _join_halves(fs, *, name):
    n_ops = len(fs)

    def body(*refs):
        f_refs, o_refs, (send_sems, recv_sems, local_sems) = refs[:n_ops], refs[n_ops:2 * n_ops], refs[2 * n_ops:]
        x, y, c = _place()
        local, sent, awaited = [], [], []
        for i, (f_ref, o_ref) in enumerate(zip(f_refs, o_refs)):
            h = f_ref.shape[0]

            def to_sibling(half):
                return pltpu.make_async_remote_copy(src_ref=f_ref, dst_ref=o_ref.at[pl.ds(half * h, h), :], send_sem=send_sems.at[i],
                                                    recv_sem=recv_sems.at[i], device_id=(x, y, 1 - c), device_id_type=MESH)

            local.append(pltpu.make_async_copy(f_ref, o_ref.at[pl.ds(c * h, h), :], local_sems.at[i]))
            sent.append(to_sibling(c))
            awaited.append(to_sibling(1 - c))
        for cp in local + sent:
            cp.start()
        for cp in awaited:
            cp.wait_recv()
        for cp in sent:
            cp.wait_send()
        for cp in local:
            cp.wait()

    return pl.pallas_call(
        body, name=name, in_specs=[_HBM] * n_ops, out_specs=[_HBM] * n_ops,
        out_shape=[jax.ShapeDtypeStruct((2 * f.shape[0], f.shape[1]), f.dtype) for f in fs],
        scratch_shapes=_dma_sems(n_ops) + [pltpu.SemaphoreType.DMA((n_ops,))],
    )(*fs)


def _add_halves(gp, ra, wire_dtype, *, name, bm=256):
    n, r, c_ = gp.shape
    h = r // 2
    bm = _tile(h, bm)
    per = h // bm
    c = lax.axis_index("c").astype(jnp.int32).reshape(1)

    def body(c_ref, g_ref, ra_ref, o_ref, ow_ref):
        s = g_ref[...] + ra_ref[...]
        o_ref[...] = s
        ow_ref[...] = s.astype(wire_dtype)

    mine = pl.BlockSpec((None, bm, c_), lambda j, i, cr: (j, i, 0))
    return pl.pallas_call(
        body, name=name,
        grid_spec=pltpu.PrefetchScalarGridSpec(
            num_scalar_prefetch=1, grid=(n, per),
            in_specs=[pl.BlockSpec((None, bm, c_), lambda j, i, cr: (j, cr[0] * per + i, 0)), mine],
            out_specs=[mine, mine]),
        out_shape=[jax.ShapeDtypeStruct((n, h, c_), F32), jax.ShapeDtypeStruct((n, h, c_), wire_dtype)],
        compiler_params=pltpu.CompilerParams(dimension_semantics=("parallel", "parallel")),
    )(c, gp, ra)


def _add_chips(hp, rb, *, name, bm=256):
    n, h, c_ = hp.shape
    bm = _tile(h, bm)
    me = (2 * lax.axis_index("x") + lax.axis_index("y")).astype(jnp.int32).reshape(1)

    def body(me_ref, h_ref, rb_ref, o_ref):
        o_ref[...] = ((h_ref[...] + rb_ref[0].astype(F32)) + rb_ref[1].astype(F32)) + rb_ref[2].astype(F32)

    return pl.pallas_call(
        body, name=name,
        grid_spec=pltpu.PrefetchScalarGridSpec(
            num_scalar_prefetch=1, grid=(h // bm,),
            in_specs=[pl.BlockSpec((None, bm, c_), lambda i, mr: (mr[0], i, 0)),
                      pl.BlockSpec((3, bm, c_), lambda i, mr: (0, i, 0))],
            out_specs=pl.BlockSpec((bm, c_), lambda i, mr: (i, 0))),
        out_shape=jax.ShapeDtypeStruct((h, c_), F32),
        compiler_params=pltpu.CompilerParams(dimension_semantics=("parallel",)),
    )(me, hp, rb)


def _reduce_scatter(gps, wire_dtypes, *, tag):
    ras = _swap_halves(gps, name=f"{tag}_swap_halves")
    hps = [_add_halves(gp, ra, wd, name=f"{tag}_add_halves{i}") for i, (gp, ra, wd) in enumerate(zip(gps, ras, wire_dtypes))]
    rbs = _scatter_chips([hw for _, hw in hps], name=f"{tag}_scatter_chips")
    fs = [_add_chips(hf, rb, name=f"{tag}_add_chips{i}") for i, ((hf, _), rb) in enumerate(zip(hps, rbs))]
    return _join_halves(fs, name=f"{tag}_join_halves")


def _adamw(w, g, m, v, *, name):
    r, c_ = w.shape
    bm = r
    for cand in (512, 256, 128, 64, 32, 16, 8):
        if r % cand == 0:
            bm = cand
            break
    c1 = 1.0 - ADAM_B1 ** ADAM_STEP
    c2 = 1.0 - ADAM_B2 ** ADAM_STEP

    def body(w_ref, g_ref, m_ref, v_ref, d_ref, nm_ref, nv_ref):
        g_ = g_ref[...]
        m_ = ADAM_B1 * m_ref[...] + (1.0 - ADAM_B1) * g_
        v_ = ADAM_B2 * v_ref[...] + (1.0 - ADAM_B2) * (g_ * g_)
        d_ref[...] = -ADAM_LR * ((m_ / c1) / (jnp.sqrt(v_ / c2) + ADAM_EPS) + ADAM_WD * w_ref[...])
        nm_ref[...] = m_
        nv_ref[...] = v_

    spec = pl.BlockSpec((bm, c_), lambda i: (i, 0))
    return pl.pallas_call(
        body, name=name, grid=(r // bm,), in_specs=[spec] * 4, out_specs=[spec] * 3,
        out_shape=[jax.ShapeDtypeStruct((r, c_), F32)] * 3,
        compiler_params=pltpu.CompilerParams(dimension_semantics=("parallel",)),
    )(w, g, m, v)


_WEIGHTS = ["a_norm", "a_w_in", "a_w_out", "b_norm", "b_w_in", "b_v_ln_g", "b_v_ln_b", "b_w_s", "b_b_s", "b_w_out",
            "c_norm", "c_w_in", "c_conv_w", "c_conv_b", "c_ln_g", "c_ln_b", "c_w_out", "d_norm", "d_w_in", "d_b_f",
            "d_w_out", "final_norm"]
_SHARD_AXIS = {"a_norm": None, "a_w_in": 2, "a_w_out": 1, "b_norm": 1, "b_w_in": 2, "b_v_ln_g": 1, "b_v_ln_b": 1, "b_w_s": None,
               "b_b_s": None, "b_w_out": 1, "c_norm": 1, "c_w_in": 2, "c_conv_w": 2, "c_conv_b": 1, "c_ln_g": 1, "c_ln_b": 1,
               "c_w_out": 1, "d_norm": 1, "d_w_in": 2, "d_b_f": None, "d_w_out": 1, "final_norm": None}
_BIG = ["a_w_in", "a_w_out", "b_w_in", "b_w_out", "c_w_in", "c_w_out", "d_w_in", "d_w_out"]
_SMALL_SHARDED = [n for n in _WEIGHTS if _SHARD_AXIS[n] is not None and n not in _BIG]
_REPLICATED = [n for n in _WEIGHTS if _SHARD_AXIS[n] is None]
_ROW_ALIGN = 32
_ROW_ALIGN_SUMMED = 128


def _pack(pieces, dtype, align=_ROW_ALIGN):
    flat = jnp.concatenate([p.reshape(-1).astype(dtype) for p in pieces])
    unit = align * PACK_C
    total = -(-flat.shape[0] // unit) * unit
    return jnp.pad(flat, (0, total - flat.shape[0])).reshape(total // PACK_C, PACK_C)


def _unpack(flat, shapes):
    out, off = [], 0
    for s in shapes:
        n = math.prod(s)
        out.append(flat[off:off + n].reshape(s))
        off += n
    return out


def _full_shape(local_shape, axis):
    s = list(local_shape)
    if axis is not None:
        s[axis] *= N_CHIPS
    return tuple(s)


def _gather_weights(local):
    full = {n: local[n] for n in _REPLICATED}
    for names, dtype, tag in ((_BIG, BF16, "gather_big"), (_SMALL_SHARDED, F32, "gather_small")):
        got = _allgather_chips(_pack([local[n] for n in names], dtype), name=tag)
        got = got.reshape(N_CHIPS, -1)
        shards = [_unpack(got[j], [local[n].shape for n in names]) for j in range(N_CHIPS)]
        for i, n in enumerate(names):
            full[n] = jnp.concatenate([shards[j][i] for j in range(N_CHIPS)], axis=_SHARD_AXIS[n])
    return full


def _repl_piece_len(local):
    total = sum(math.prod(local[n].shape) for n in _REPLICATED)
    return -(-total // N_CHIPS)


def _reduce_grads(g, local):
    rep_flat = jnp.concatenate([g[n].reshape(-1) for n in _REPLICATED])
    piece = _repl_piece_len(local)
    rep_flat = jnp.pad(rep_flat, (0, N_CHIPS * piece - rep_flat.shape[0]))

    def shard(n, j):
        full = g[n].reshape(_full_shape(local[n].shape, _SHARD_AXIS[n]))
        width = local[n].shape[_SHARD_AXIS[n]]
        return lax.slice_in_dim(full, j * width, (j + 1) * width, axis=_SHARD_AXIS[n])

    big = jnp.stack([_pack([shard(n, j) for n in _BIG], F32, _ROW_ALIGN_SUMMED) for j in range(N_CHIPS)])
    small = jnp.stack([_pack([shard(n, j) for n in _SMALL_SHARDED] + [rep_flat[j * piece:(j + 1) * piece]], F32)
                       for j in range(N_CHIPS)])
    big, small = _reduce_scatter([big, small], [BF16, F32], tag="grads")
    red = dict(zip(_BIG, _unpack(big.reshape(-1), [local[n].shape for n in _BIG])))
    out = _unpack(small.reshape(-1), [local[n].shape for n in _SMALL_SHARDED] + [(piece,)])
    red.update(zip(_SMALL_SHARDED, out[:-1]))
    rep = _allgather_chips(_pack([out[-1]], F32), name="gather_replicated_grads").reshape(N_CHIPS, -1)[:, :piece].reshape(-1)
    for n, val in zip(_REPLICATED, _unpack(rep, [local[n].shape for n in _REPLICATED])):
        red[n] = val
    return red


def _update(local, grads, m, v):
    delta, new_m, new_v = {}, {}, {}
    for n in _BIG:
        shp = local[n].shape
        two = (shp[-2], shp[-1])
        res = _adamw(local[n].reshape(two), grads[n].reshape(two), m[n].reshape(two), v[n].reshape(two), name=f"adamw_{n}")
        delta[n], new_m[n], new_v[n] = [r.reshape(shp) for r in res]
    small = [n for n in _WEIGHTS if n not in _BIG]
    shapes = [local[n].shape for n in small]
    packed = [_pack([src[n] for n in small], F32) for src in (local, grads, m, v)]
    res = _adamw(*packed, name="adamw_small")
    for dst, r in zip((delta, new_m, new_v), res):
        for n, val in zip(small, _unpack(r.reshape(-1), shapes)):
            dst[n] = val
    return delta, new_m, new_v


def kernel(x, a_norm, a_w_in, a_w_out, b_norm, b_w_in, b_v_ln_g, b_v_ln_b, b_w_s, b_b_s, b_w_out, c_norm, c_w_in, c_conv_w, c_conv_b, c_ln_g, c_ln_b, c_w_out, d_norm, d_w_in, d_b_f, d_w_out, final_norm, loss_target, m_a_norm, m_a_w_in, m_a_w_out, m_b_norm, m_b_w_in, m_b_v_ln_g, m_b_v_ln_b, m_b_w_s, m_b_b_s, m_b_w_out, m_c_norm, m_c_w_in, m_c_conv_w, m_c_conv_b, m_c_ln_g, m_c_ln_b, m_c_w_out, m_d_norm, m_d_w_in, m_d_b_f, m_d_w_out, m_final_norm, v_a_norm, v_a_w_in, v_a_w_out, v_b_norm, v_b_w_in, v_b_v_ln_g, v_b_v_ln_b, v_b_w_s, v_b_b_s, v_b_w_out, v_c_norm, v_c_w_in, v_c_conv_w, v_c_conv_b, v_c_ln_g, v_c_ln_b, v_c_w_out, v_d_norm, v_d_w_in, v_d_b_f, v_d_w_out, v_final_norm):
    local = dict(zip(_WEIGHTS, (a_norm, a_w_in, a_w_out, b_norm, b_w_in, b_v_ln_g, b_v_ln_b, b_w_s, b_b_s, b_w_out, c_norm, c_w_in,
                                c_conv_w, c_conv_b, c_ln_g, c_ln_b, c_w_out, d_norm, d_w_in, d_b_f, d_w_out, final_norm)))
    m = dict(zip(_WEIGHTS, (m_a_norm, m_a_w_in, m_a_w_out, m_b_norm, m_b_w_in, m_b_v_ln_g, m_b_v_ln_b, m_b_w_s, m_b_b_s, m_b_w_out,
                            m_c_norm, m_c_w_in, m_c_conv_w, m_c_conv_b, m_c_ln_g, m_c_ln_b, m_c_w_out, m_d_norm, m_d_w_in, m_d_b_f,
                            m_d_w_out, m_final_norm)))
    v = dict(zip(_WEIGHTS, (v_a_norm, v_a_w_in, v_a_w_out, v_b_norm, v_b_w_in, v_b_v_ln_g, v_b_v_ln_b, v_b_w_s, v_b_b_s, v_b_w_out,
                            v_c_norm, v_c_w_in, v_c_conv_w, v_c_conv_b, v_c_ln_g, v_c_ln_b, v_c_w_out, v_d_norm, v_d_w_in, v_d_b_f,
                            v_d_w_out, v_final_norm)))
    full = _gather_weights(local)
    w = {n: full[n][0] if n != "final_norm" else full[n] for n in _WEIGHTS}
    loss_part, grad_x, g = _local_step(x, loss_target, w)
    loss = lax.psum(loss_part, ("x", "y", "c"))
    grads = _reduce_grads(g, local)
    delta, new_m, new_v = _update(local, grads, m, v)
    return (loss, grad_x, *[grads[n] for n in _WEIGHTS], *[delta[n] for n in _WEIGHTS],
            *[new_m[n] for n in _WEIGHTS], *[new_v[n] for n in _WEIGHTS])
```

```python
import functools
import math

import jax
import jax.numpy as jnp
from jax import lax
from jax.experimental import pallas as pl
from jax.experimental.pallas import tpu as pltpu

F32, BF16 = jnp.float32, jnp.bfloat16
MESH = pl.DeviceIdType.MESH

D_MODEL = 1024
HEADS = 16
HEAD_DIM = 64
BLK = 128
PAIRS = HEADS // 2
GM_W = 2048
GM_G = 16
CV_W = 2048
CV_K = 31
HALO = 32
EPS = 1e-6
N_CHIPS = 4
PACK_C = 1024
ADAM_LR, ADAM_B1, ADAM_B2, ADAM_EPS, ADAM_WD, ADAM_STEP = 0.001, 0.9, 0.999, 1e-08, 0.01, 10

_NT = (((1,), (1,)), ((), ()))
_TN = (((0,), (0,)), ((), ()))
_NN = (((1,), (0,)), ((), ()))


def _dot(a, b, dims=_NN):
    return lax.dot_general(a, b, dims, preferred_element_type=F32)


def _split3(x):
    hi = x.astype(BF16)
    r = x - hi.astype(F32)
    mid = r.astype(BF16)
    lo = (r - mid.astype(F32)).astype(BF16)
    return hi, mid, lo


def _dot3_right(x, m):
    hi, mid, lo = _split3(x)
    return _dot(hi, m) + _dot(mid, m) + _dot(lo, m)


def _dot3_left(m, x):
    hi, mid, lo = _split3(x)
    return _dot(m, hi) + _dot(m, mid) + _dot(m, lo)


def _sigmoid(x):
    return 1.0 / (1.0 + jnp.exp(-x))


def _silu(x):
    return x * _sigmoid(x)


def _dsilu(x):
    s = _sigmoid(x)
    return s * (1.0 + x * (1.0 - s))


_GELU_C = math.sqrt(2.0 / math.pi)
_GELU_A = 0.044715


def _gelu(x):
    return 0.5 * x * (1.0 + jnp.tanh(_GELU_C * (x + _GELU_A * x * x * x)))


def _dgelu(x):
    t = jnp.tanh(_GELU_C * (x + _GELU_A * x * x * x))
    return 0.5 * (1.0 + t) + 0.5 * x * (1.0 - t * t) * _GELU_C * (1.0 + 3.0 * _GELU_A * x * x)


def _log_sigmoid(x):
    return jnp.minimum(x, 0.0) - jnp.log(1.0 + jnp.exp(-jnp.abs(x)))


def _rms_fwd(x, g):
    r = lax.rsqrt(jnp.mean(x * x, axis=-1, keepdims=True) + EPS)
    return x * r * g


def _rms_bwd(dy, x, g):
    r = lax.rsqrt(jnp.mean(x * x, axis=-1, keepdims=True) + EPS)
    xh = x * r
    dxh = dy * g
    dx = r * (dxh - xh * jnp.mean(dxh * xh, axis=-1, keepdims=True))
    return dx, dy * xh


def _ln_stats(x):
    mu = jnp.mean(x, axis=-1, keepdims=True)
    xc = x - mu
    r = lax.rsqrt(jnp.mean(xc * xc, axis=-1, keepdims=True) + EPS)
    return xc * r, r


def _ln_bwd(dy, xh, r, g):
    dxh = dy * g
    return r * (dxh - jnp.mean(dxh, axis=-1, keepdims=True) - xh * jnp.mean(dxh * xh, axis=-1, keepdims=True))


def _colsum(x):
    return jnp.sum(x, axis=0, keepdims=True)


def _tile(n, want):
    for t in range(min(n, want), 7, -1):
        if n % t == 0 and t % 8 == 0:
            return t
    return n


MM_TILE = 1024


def _matmul(a, b, *, name, mode="nn", residual=None, out_shards=1):
    (m, k) = a.shape
    b_shards = b.shape[0] if b.ndim == 3 else 1
    if mode == "nn":
        n = b.shape[-1] * b_shards
        tn, tk = _tile(n // max(b_shards, out_shards), MM_TILE), _tile(k, MM_TILE)
    else:
        n = b.shape[-2]
        tn, tk = _tile(n // out_shards, MM_TILE), _tile(k // b_shards, MM_TILE)
    tm = _tile(m, MM_TILE)
    nk = k // tk
    a_spec = pl.BlockSpec((tm, tk), lambda i, j, kk: (i, kk))
    if mode == "nn":
        dims = _NN
        if b_shards == 1:
            b_spec = pl.BlockSpec((tk, tn), lambda i, j, kk: (kk, j))
        else:
            per_b = n // b_shards // tn
            b_spec = pl.BlockSpec((None, tk, tn), lambda i, j, kk: (j // per_b, kk, j % per_b))
    else:
        dims = _NT
        if b_shards == 1:
            b_spec = pl.BlockSpec((tn, tk), lambda i, j, kk: (j, kk))
        else:
            per_b = k // b_shards // tk
            b_spec = pl.BlockSpec((None, tn, tk), lambda i, j, kk: (kk // per_b, j, kk % per_b))
    if out_shards == 1:
        o_spec = pl.BlockSpec((tm, tn), lambda i, j, kk: (i, j))
        o_shape = (m, n)
    else:
        per_o = n // out_shards // tn
        o_spec = pl.BlockSpec((None, tm, tn), lambda i, j, kk: (j // per_o, i, j % per_o))
        o_shape = (out_shards, m, n // out_shards)
    has_res = residual is not None

    def body(a_ref, b_ref, *rest):
        o_ref = rest[-1]
        kk = pl.program_id(2)
        part = _dot(a_ref[...].astype(BF16), b_ref[...].astype(BF16), dims)
        if has_res:
            @pl.when(kk == 0)
            def _():
                o_ref[...] = part + rest[0][...]
        else:
            @pl.when(kk == 0)
            def _():
                o_ref[...] = part

        @pl.when(kk > 0)
        def _():
            o_ref[...] += part

    return pl.pallas_call(
        body, name=name, grid=(m // tm, n // tn, nk),
        in_specs=[a_spec, b_spec] + ([o_spec] if has_res else []),
        out_specs=o_spec, out_shape=jax.ShapeDtypeStruct(o_shape, F32),
        compiler_params=pltpu.CompilerParams(dimension_semantics=("parallel", "parallel", "arbitrary")),
    )(a, b, *([residual] if has_res else []))


def _rows(fn, *, name, steps, ins, outs, accs=(), scratch=()):
    ni, no, na = len(ins), len(outs), len(accs)

    def body(*refs):
        in_refs, out_refs = refs[:ni], refs[ni:ni + no]
        acc_refs, scr = refs[ni + no:ni + no + na], refs[ni + no + na:]
        i = pl.program_id(0)

        @pl.when(i == 0)
        def _():
            for r in acc_refs:
                r[...] = jnp.zeros(r.shape, r.dtype)

        fn(i, in_refs, out_refs, acc_refs, scr)

    def full(shape):
        nd = len(shape)
        return pl.BlockSpec(tuple(shape), lambda i: (0,) * nd)

    res = pl.pallas_call(
        body, name=name, grid=(steps,),
        in_specs=[pl.BlockSpec(bs, im) for _, bs, im in ins],
        out_specs=[pl.BlockSpec(bs, im) for _, _, bs, im in outs] + [full(s) for s, _ in accs],
        out_shape=[jax.ShapeDtypeStruct(s, d) for s, d, _, _ in outs] + [jax.ShapeDtypeStruct(s, d) for s, d in accs],
        scratch_shapes=list(scratch),
        compiler_params=pltpu.CompilerParams(dimension_semantics=("arbitrary",)),
    )(*[a for a, _, _ in ins])
    return res


def _rb(arr, bm, cb=0, width=None):
    w = arr.shape[1] if width is None else width
    return (arr, (bm, w), lambda i: (i, cb))


def _const(arr):
    nd = arr.ndim
    return (arr, tuple(arr.shape), lambda i: (0,) * nd)


def _ro(t, w, dtype, bm):
    return ((t, w), dtype, (bm, w), lambda i: (i, 0))


def _rmsnorm(x, g, *, name, bm=512):
    t, d = x.shape
    bm = _tile(t, bm)

    def fn(i, ins, outs, accs, scr):
        outs[0][...] = _rms_fwd(ins[0][...], ins[1][...]).astype(BF16)

    return _rows(fn, name=name, steps=t // bm, ins=[_rb(x, bm), _const(g)], outs=[_ro(t, d, BF16, bm)])[0]


def _rmsnorm_bwd(dh, x, g, dres, *, name, bm=512):
    t, d = x.shape
    bm = _tile(t, bm)

    def fn(i, ins, outs, accs, scr):
        dx, dgrow = _rms_bwd(ins[0][...], ins[1][...], ins[2][...])
        outs[0][...] = ins[3][...] + dx
        accs[0][...] += _colsum(dgrow)

    return _rows(fn, name=name, steps=t // bm, ins=[_rb(dh, bm), _rb(x, bm), _const(g), _rb(dres, bm)],
                 outs=[_ro(t, d, F32, bm)], accs=[((1, d), F32)])


def _gate(o, p, gcb, *, name, bm=512):
    t, w = o.shape
    bm = _tile(t, bm)

    def fn(i, ins, outs, accs, scr):
        outs[0][...] = (ins[0][...] * _silu(ins[1][...])).astype(BF16)

    return _rows(fn, name=name, steps=t // bm, ins=[_rb(o, bm), _rb(p, bm, gcb, w)], outs=[_ro(t, w, BF16, bm)])[0]


def _gate_bwd(dy, o, p, gcb, *, name, bm=512):
    t, w = o.shape
    bm = _tile(t, bm)

    def fn(i, ins, outs, accs, scr):
        dy_, o_, g_ = ins[0][...], ins[1][...], ins[2][...]
        outs[0][...] = dy_ * _silu(g_)
        outs[1][...] = dy_ * o_ * _dsilu(g_)

    return _rows(fn, name=name, steps=t // bm, ins=[_rb(dy, bm), _rb(o, bm), _rb(p, bm, gcb, w)],
                 outs=[_ro(t, w, F32, bm), _ro(t, w, F32, bm)])


def _loss_head(x, g, tgt, *, name, bm=512):
    t, d = x.shape
    bm = _tile(t, bm)

    def fn(i, ins, outs, accs, scr):
        x_, g_, tg = ins[0][...], ins[1][...], ins[2][...]
        err = _rms_fwd(x_, g_) - tg
        part = 0.5 * jnp.sum(jnp.sum(err * err, axis=-1, keepdims=True), axis=0, keepdims=True) / d
        dx, dgrow = _rms_bwd(err / d, x_, g_)
        outs[0][...] = dx
        accs[0][...] += _colsum(dgrow)
        accs[1][...] += jnp.broadcast_to(part, (1, BLK))

    return _rows(fn, name=name, steps=t // bm, ins=[_rb(x, bm), _const(g), _rb(tgt, bm)],
                 outs=[_ro(t, d, F32, bm)], accs=[((1, d), F32), ((1, BLK), F32)])


def _gmlp_mix_weights(ws_ref, g):
    row = lax.broadcasted_iota(jnp.int32, (BLK, BLK), 0)
    col = lax.broadcasted_iota(jnp.int32, (BLK, BLK), 1)
    tril = col <= row
    return jnp.where(tril, ws_ref[g], 0.0), tril


def _gmlp_fwd(p, ln_g, ln_b, w_s, bs_t, *, name):
    t = p.shape[0]

    def fn(i, ins, outs, accs, scr):
        p_ref, lg, lb, ws_ref, bst = ins
        vn = _ln_stats(_gelu(p_ref[:, GM_W:2 * GM_W]))[0] * lg[...] + lb[...]
        for g in range(GM_G):
            cs = slice(g * BLK, (g + 1) * BLK)
            wt, _ = _gmlp_mix_weights(ws_ref, g)
            s = _dot(wt.astype(BF16), vn[:, cs].astype(BF16)) + bst[:, g:g + 1]
            u = _gelu(p_ref[:, cs])
            gate = p_ref[:, 2 * GM_W + g * BLK:2 * GM_W + (g + 1) * BLK]
            outs[0][:, cs] = (u * s * _silu(gate)).astype(BF16)

    return _rows(fn, name=name, steps=t // BLK, ins=[_rb(p, BLK), _const(ln_g), _const(ln_b), _const(w_s), _const(bs_t)],
                 outs=[_ro(t, GM_W, BF16, BLK)])[0]


def _gmlp_bwd(dy, p, ln_g, ln_b, w_s, bs_t, *, name):
    t = p.shape[0]

    def fn(i, ins, outs, accs, scr):
        dy_ref, p_ref, lg, lb, ws_ref, bst = ins
        dp_ref = outs[0]
        dlg, dlb, dws, dbst = accs
        dvn_ref = scr[0]
        v_pre = p_ref[:, GM_W:2 * GM_W]
        xh, r = _ln_stats(_gelu(v_pre))
        vn = xh * lg[...] + lb[...]
        for g in range(GM_G):
            cs = slice(g * BLK, (g + 1) * BLK)
            gs = slice(2 * GM_W + g * BLK, 2 * GM_W + (g + 1) * BLK)
            wt, tril = _gmlp_mix_weights(ws_ref, g)
            vg = vn[:, cs].astype(BF16)
            s = _dot(wt.astype(BF16), vg) + bst[:, g:g + 1]
            u_pre, gate, dyg = p_ref[:, cs], p_ref[:, gs], dy_ref[:, cs]
            u = _gelu(u_pre)
            dos = dyg * _silu(gate)
            dp_ref[:, gs] = dyg * u * s * _dsilu(gate)
            dp_ref[:, cs] = dos * s * _dgelu(u_pre)
            ds = (dos * u).astype(BF16)
            dws[g] += jnp.where(tril, _dot(ds, vg, _NT), 0.0)
            dbst[:, g:g + 1] += jnp.sum(dos * u, axis=1, keepdims=True)
            dvn_ref[:, cs] = _dot(wt.astype(BF16), ds, _TN)
        dvn = dvn_ref[...]
        dlg[...] += _colsum(dvn * xh)
        dlb[...] += _colsum(dvn)
        dp_ref[:, GM_W:2 * GM_W] = _ln_bwd(dvn, xh, r, lg[...]) * _dgelu(v_pre)

    return _rows(fn, name=name, steps=t // BLK,
                 ins=[_rb(dy, BLK), _rb(p, BLK), _const(ln_g), _const(ln_b), _const(w_s), _const(bs_t)],
                 outs=[_ro(t, 3 * GM_W, F32, BLK)],
                 accs=[((1, GM_W), F32), ((1, GM_W), F32), ((GM_G, BLK, BLK), F32), ((BLK, GM_G), F32)],
                 scratch=[pltpu.VMEM((BLK, GM_W), F32)])


CV_BM = 128
CV_RC = 8
SUBLANES = 8
CV_FWD_OFFS = [HALO - (CV_K - 1) + k for k in range(CV_K)]
CV_BWD_OFFS = [CV_K - 1 - k for k in range(CV_K)]


def _conv_halo_prev(p, cb, bm):
    per = bm // HALO
    return (p, (HALO, CV_W), lambda i: (jnp.maximum(i * per - 1, 0), cb))


def _conv_scratch(bm):
    return [pltpu.VMEM((bm + HALO, CV_W), F32), pltpu.VMEM((SUBLANES - 1, bm + HALO - SUBLANES, CV_W), F32),
            pltpu.VMEM((bm, CV_W), F32)]


def _conv_shift_copies(ext_ref, sh_ref):
    rows = sh_ref.shape[1]
    for b in range(1, SUBLANES):
        sh_ref[b - 1] = ext_ref[pl.ds(b, rows), :]


def _conv_window(ext_ref, sh_ref, off, r0, rows):
    b = off % SUBLANES
    src = ext_ref if b == 0 else sh_ref.at[b - 1]
    return src[pl.ds(r0 + (off - b), rows), :]


def _conv_taps(ext_ref, sh_ref, cw_ref, y_ref, offs):
    bm = y_ref.shape[0]

    def chunk(ci, c):
        r0 = pl.multiple_of(ci * CV_RC, CV_RC)
        acc = jnp.zeros((CV_RC, CV_W), F32)
        for k in range(CV_K):
            acc = acc + cw_ref[pl.ds(k * SUBLANES, CV_RC), :] * _conv_window(ext_ref, sh_ref, offs[k], r0, CV_RC)
        y_ref[pl.ds(r0, CV_RC), :] = acc
        return c

    lax.fori_loop(0, bm // CV_RC, chunk, 0)


def _conv_dweights(dy1_ref, ext_ref, sh_ref, dcw_ref):
    bm = dy1_ref.shape[0]
    for k in range(CV_K):
        def chunk(ci, acc, off=CV_FWD_OFFS[k]):
            r0 = pl.multiple_of(ci * CV_RC, CV_RC)
            return acc + dy1_ref[pl.ds(r0, CV_RC), :] * _conv_window(ext_ref, sh_ref, off, r0, CV_RC)

        dcw_ref[k:k + 1, :] += _colsum(lax.fori_loop(0, bm // CV_RC, chunk, jnp.zeros((CV_RC, CV_W), F32)))


def _conv_fill(i, ext_ref, a_prev, b_prev, a, b, bm, seq):
    keep = jnp.where((i % (seq // bm)) == 0, 0.0, 1.0)
    ext_ref[pl.ds(0, HALO), :] = keep * (a_prev * _sigmoid(b_prev))
    ext_ref[pl.ds(HALO, bm), :] = a * _sigmoid(b)


def _conv_fwd(p, cw, cb, ln_g, ln_b, seq, *, name, bm=CV_BM):
    t = p.shape[0]

    def fn(i, ins, outs, accs, scr):
        a, b, gate, ap, bp = [r[...] for r in ins[:5]]
        cw_ref, cb_, lg, lb = ins[5], ins[6][...], ins[7][...], ins[8][...]
        ext, sh, y = scr
        _conv_fill(i, ext, ap, bp, a, b, bm, seq)
        _conv_shift_copies(ext, sh)
        _conv_taps(ext, sh, cw_ref, y, CV_FWD_OFFS)
        y2 = _ln_stats(y[...] + cb_)[0] * lg + lb
        outs[0][...] = (_silu(y2) * _silu(gate)).astype(BF16)

    return _rows(fn, name=name, steps=t // bm,
                 ins=[_rb(p, bm, 0, CV_W), _rb(p, bm, 1, CV_W), _rb(p, bm, 2, CV_W),
                      _conv_halo_prev(p, 0, bm), _conv_halo_prev(p, 1, bm),
                      _const(cw), _const(cb), _const(ln_g), _const(ln_b)],
                 outs=[_ro(t, CV_W, BF16, bm)], scratch=_conv_scratch(bm))[0]


def _conv_bwd_post(dy, p, cw, cb, ln_g, ln_b, seq, *, name, bm=CV_BM):
    t = p.shape[0]

    def fn(i, ins, outs, accs, scr):
        dy_, a, b, gate, ap, bp = [r[...] for r in ins[:6]]
        cw_ref, cb_, lg, lb = ins[6], ins[7][...], ins[8][...], ins[9][...]
        dlg, dlb, dcb, dcw = accs
        ext, sh, y = scr
        _conv_fill(i, ext, ap, bp, a, b, bm, seq)
        _conv_shift_copies(ext, sh)
        _conv_taps(ext, sh, cw_ref, y, CV_FWD_OFFS)
        xh, r = _ln_stats(y[...] + cb_)
        y2 = xh * lg + lb
        outs[1][...] = dy_ * _silu(y2) * _dsilu(gate)
        dy2 = dy_ * _silu(gate) * _dsilu(y2)
        dlg[...] += _colsum(dy2 * xh)
        dlb[...] += _colsum(dy2)
        dy1 = _ln_bwd(dy2, xh, r, lg)
        outs[0][...] = dy1
        dcb[...] += _colsum(dy1)
        _conv_dweights(outs[0], ext, sh, dcw)

    return _rows(fn, name=name, steps=t // bm,
                 ins=[_rb(dy, bm), _rb(p, bm, 0, CV_W), _rb(p, bm, 1, CV_W), _rb(p, bm, 2, CV_W),
                      _conv_halo_prev(p, 0, bm), _conv_halo_prev(p, 1, bm),
                      _const(cw), _const(cb), _const(ln_g), _const(ln_b)],
                 outs=[_ro(t, CV_W, F32, bm), _ro(t, CV_W, F32, bm)],
                 accs=[((1, CV_W), F32), ((1, CV_W), F32), ((1, CV_W), F32), ((CV_K, CV_W), F32)],
                 scratch=_conv_scratch(bm))


def _conv_bwd_pre(dy1, dgate, p, cw, seq, *, name, bm=CV_BM):
    t = p.shape[0]
    per = bm // HALO
    last_halo = t // HALO - 1

    def fn(i, ins, outs, accs, scr):
        d1, d1n, dg, a, b = [r[...] for r in ins[:5]]
        ext, sh, y = scr
        keep = jnp.where((i % (seq // bm)) == (seq // bm - 1), 0.0, 1.0)
        ext[pl.ds(0, bm), :] = d1
        ext[pl.ds(bm, HALO), :] = keep * d1n
        _conv_shift_copies(ext, sh)
        _conv_taps(ext, sh, ins[5], y, CV_BWD_OFFS)
        dy0 = y[...]
        sb = _sigmoid(b)
        outs[0][:, 0:CV_W] = dy0 * sb
        outs[0][:, CV_W:2 * CV_W] = dy0 * a * sb * (1.0 - sb)
        outs[0][:, 2 * CV_W:3 * CV_W] = dg

    return _rows(fn, name=name, steps=t // bm,
                 ins=[_rb(dy1, bm), (dy1, (HALO, CV_W), lambda i: (jnp.minimum((i + 1) * per, last_halo), 0)),
                      _rb(dgate, bm), _rb(p, bm, 0, CV_W), _rb(p, bm, 1, CV_W), _const(cw)],
                 outs=[_ro(t, 3 * CV_W, F32, bm)], scratch=_conv_scratch(bm))[0]


def _iotas():
    row = lax.broadcasted_iota(jnp.int32, (BLK, BLK), 0)
    col = lax.broadcasted_iota(jnp.int32, (BLK, BLK), 1)
    return row, col


def _heads(x, head0):
    if head0.shape != x.shape:
        head0 = lax.broadcasted_iota(jnp.int32, x.shape, 1) < HEAD_DIM
    return jnp.where(head0, x, 0.0).astype(BF16), jnp.where(head0, 0.0, x).astype(BF16)


def _pair_spec(seq, off):
    return pl.BlockSpec((seq, BLK), lambda b, hp: (b, off + hp))


def _stat_spec(seq):
    return pl.BlockSpec((None, None, seq, BLK), lambda b, hp: (b, hp, 0, 0))


_ATT_PARAMS = dict(compiler_params=pltpu.CompilerParams(dimension_semantics=("parallel", "parallel")))
_SCALE = 1.0 / math.sqrt(HEAD_DIM)


KEY_BLOCK = 512


def _stack_heads(x, head0, scale=None):
    if scale is not None:
        x = x * scale
    return jnp.concatenate(_heads(x, head0), axis=0)


def _pair_cols(x, head0, fill):
    a = jnp.max(jnp.where(head0, x, fill), axis=1, keepdims=True)
    b = jnp.max(jnp.where(head0, fill, x), axis=1, keepdims=True)
    return jnp.concatenate([a, b], axis=0)


def _causal_mask(t0, s0, kw, inclusive):
    row = lax.broadcasted_iota(jnp.int32, (2 * BLK, kw), 0) & (BLK - 1)
    col = lax.broadcasted_iota(jnp.int32, (2 * BLK, kw), 1)
    return (s0 + col) <= (t0 + row) if inclusive else (s0 + col) < (t0 + row)


def _sub(x, j):
    return x[:, j * BLK:(j + 1) * BLK]


def _block_cumsum(x, tri, ksub):
    parts = _split3(x)
    cs = _dot(jnp.concatenate([_sub(pt, j) for pt in parts for j in range(ksub)], axis=0), tri)
    n = 2 * BLK
    return [cs[j * n:(j + 1) * n] + cs[(ksub + j) * n:(ksub + j + 1) * n] + cs[(2 * ksub + j) * n:(2 * ksub + j + 1) * n]
            for j in range(ksub)]


def _sb_terms(qs, k, mask):
    z = _dot(qs, k, _NT)
    t = jnp.log(1.0 + jnp.exp(-jnp.abs(z)))
    lsz = jnp.minimum(z, 0.0) - t
    lr = jnp.minimum(-z, 0.0) - t
    if mask is not None:
        lr = jnp.where(mask, lr, 0.0)
    return lsz, lr


def _sb_fwd(p, nb, seq, *, name):
    nq = seq // BLK
    kw = min(KEY_BLOCK, seq)
    ksub = kw // BLK

    def body(q_ref, k_ref, v_ref, o_ref, tot_ref):
        row, col = _iotas()
        head0 = col < HEAD_DIM
        upper = (row > col).astype(BF16)

        def qblock(qb, c):
            t0 = pl.multiple_of(qb * BLK, BLK)
            qs = _stack_heads(q_ref[pl.ds(t0, BLK), :], head0, _SCALE)
            diag = qb // ksub

            def kblock(kb, carry, masked):
                acc, run = carry
                s0 = pl.multiple_of(kb * kw, kw)
                k = k_ref[pl.ds(s0, kw), :].astype(BF16)
                v0, v1 = _heads(v_ref[pl.ds(s0, kw), :], head0)
                mask = _causal_mask(t0, s0, kw, False) if masked else None
                lsz, lr = _sb_terms(qs, k, mask)
                inblock = _block_cumsum(lr, upper, ksub)
                ws = [None] * ksub
                for j in reversed(range(ksub)):
                    w = jnp.exp(_sub(lsz, j) + inblock[j] + run)
                    if masked:
                        w = jnp.where(_sub(mask, j), w, 0.0)
                    ws[j] = w.astype(BF16)
                    run = run + jnp.sum(_sub(lr, j), axis=1, keepdims=True)
                w = jnp.concatenate(ws, axis=1)
                return acc + _dot(w[:BLK], v0) + _dot(w[BLK:], v1), run

            carry = kblock(diag, (jnp.zeros((BLK, BLK), F32), jnp.zeros((2 * BLK, 1), F32)), True)
            acc, run = lax.fori_loop(0, diag, lambda it, cr: kblock(diag - 1 - it, cr, False), carry)
            o_ref[pl.ds(t0, BLK), :] = acc
            tot_ref[pl.ds(t0, BLK), :] = jnp.where(head0, run[:BLK], run[BLK:])
            return c

        lax.fori_loop(0, nq, qblock, 0)

    return pl.pallas_call(
        body, name=name, grid=(nb, PAIRS),
        in_specs=[_pair_spec(seq, 0), _pair_spec(seq, PAIRS), _pair_spec(seq, 2 * PAIRS)],
        out_specs=[_pair_spec(seq, 0), _stat_spec(seq)],
        out_shape=[jax.ShapeDtypeStruct((nb * seq, D_MODEL), F32), jax.ShapeDtypeStruct((nb, PAIRS, seq, BLK), F32)],
        **_ATT_PARAMS,
    )(p, p, p)


def _sb_bwd(p, do, tot, nb, seq, *, name):
    nq = seq // BLK
    kw = min(KEY_BLOCK, seq)
    ksub = kw // BLK

    def body(q_ref, k_ref, v_ref, do_ref, tot_ref, dq_ref, dk_ref, dv_ref):
        row, col = _iotas()
        head0 = col < HEAD_DIM
        lower_incl = (row <= col).astype(BF16)
        lower_strict = (row < col).astype(BF16)
        dk_ref[...] = jnp.zeros(dk_ref.shape, F32)
        dv_ref[...] = jnp.zeros(dv_ref.shape, F32)

        def qblock(qb, c):
            t0 = pl.multiple_of(qb * BLK, BLK)
            qs = _stack_heads(q_ref[pl.ds(t0, BLK), :], head0, _SCALE)
            dos = _stack_heads(do_ref[pl.ds(t0, BLK), :], head0)
            tt = _pair_cols(tot_ref[pl.ds(t0, BLK), :], head0, -jnp.inf)
            diag = qb // ksub

            def kblock(kb, carry, masked):
                dq, pf, ef = carry
                s0 = pl.multiple_of(kb * kw, kw)
                kf = k_ref[pl.ds(s0, kw), :]
                k = kf.astype(BF16)
                k0, k1 = _heads(kf, head0)
                v = v_ref[pl.ds(s0, kw), :].astype(BF16)
                mask = _causal_mask(t0, s0, kw, False) if masked else None
                lsz, lr = _sb_terms(qs, k, None)
                lrm = jnp.where(mask, lr, 0.0) if masked else lr
                incl = _block_cumsum(lrm, lower_incl, ksub)
                dw = _dot(dos, v, _NT)
                ws, ews = [], []
                for j in range(ksub):
                    w = jnp.exp(_sub(lsz, j) + (tt - pf - incl[j]))
                    if masked:
                        w = jnp.where(_sub(mask, j), w, 0.0)
                    pf = pf + jnp.sum(_sub(lrm, j), axis=1, keepdims=True)
                    ws.append(w)
                    ews.append(_sub(dw, j) * w)
                before = _block_cumsum(jnp.concatenate(ews, axis=1), lower_strict, ksub)
                dzs = []
                for j in range(ksub):
                    dlr = ef + before[j]
                    ef = ef + jnp.sum(ews[j], axis=1, keepdims=True)
                    dz = ews[j] * jnp.exp(_sub(lr, j)) - dlr * jnp.exp(_sub(lsz, j))
                    if masked:
                        dz = jnp.where(_sub(mask, j), dz, 0.0)
                    dzs.append(dz.astype(BF16))
                dz = jnp.concatenate(dzs, axis=1)
                w = jnp.concatenate([w.astype(BF16) for w in ws], axis=1)
                dk_ref[pl.ds(s0, kw), :] += _dot(dz, qs, _TN)
                dv_ref[pl.ds(s0, kw), :] += _dot(w, dos, _TN)
                return dq + _dot(dz[:BLK], k0) + _dot(dz[BLK:], k1), pf, ef

            zc = jnp.zeros((2 * BLK, 1), F32)
            carry = lax.fori_loop(0, diag, lambda kb, cr: kblock(kb, cr, False), (jnp.zeros((BLK, BLK), F32), zc, zc))
            dq_ref[pl.ds(t0, BLK), :] = kblock(diag, carry, True)[0] * _SCALE
            return c

        lax.fori_loop(0, nq, qblock, 0)

    t = nb * seq
    return pl.pallas_call(
        body, name=name, grid=(nb, PAIRS),
        in_specs=[_pair_spec(seq, 0), _pair_spec(seq, PAIRS), _pair_spec(seq, 2 * PAIRS), _pair_spec(seq, 0), _stat_spec(seq)],
        out_specs=[_pair_spec(seq, 0)] * 3,
        out_shape=[jax.ShapeDtypeStruct((t, D_MODEL), F32)] * 3,
        **_ATT_PARAMS,
    )(p, p, p, do, tot)


def _fox_cum(f, bf, nb, seq, *, name):
    def body(f_ref, bf_ref, cc_ref, cr_ref):
        row, col = _iotas()
        lower = (col <= row).astype(BF16)
        carry = jnp.zeros((1, BLK), F32)
        for blk in range(seq // BLK):
            rs = slice(blk * BLK, (blk + 1) * BLK)
            lf = jnp.where(col < HEADS, _log_sigmoid(f_ref[rs, :] + bf_ref[...]), 0.0)
            cc = _dot3_left(lower, lf) + carry
            cc_ref[rs, :] = cc
            cr_ref[:, rs] = cc.T[0:HEADS, :]
            carry = carry + _colsum(lf)

    return pl.pallas_call(
        body, name=name, grid=(nb,),
        in_specs=[pl.BlockSpec((seq, BLK), lambda b: (b, 0)), pl.BlockSpec((1, BLK), lambda b: (0, 0))],
        out_specs=[pl.BlockSpec((seq, BLK), lambda b: (b, 0)), pl.BlockSpec((None, HEADS, seq), lambda b: (b, 0, 0))],
        out_shape=[jax.ShapeDtypeStruct((nb * seq, BLK), F32), jax.ShapeDtypeStruct((nb, HEADS, seq), F32)],
        compiler_params=pltpu.CompilerParams(dimension_semantics=("parallel",)),
    )(f, bf)


def _fox_cum_bwd(dcr, dcc, f, bf, nb, seq, *, name):
    def body(dcr_ref, dcc_ref, f_ref, bf_ref, df_ref, dbf_ref):
        row, col = _iotas()
        upper_incl = (col >= row).astype(BF16)

        @pl.when(pl.program_id(0) == 0)
        def _():
            dbf_ref[...] = jnp.zeros((1, BLK), F32)

        carry = jnp.zeros((1, BLK), F32)
        for blk in reversed(range(seq // BLK)):
            rs = slice(blk * BLK, (blk + 1) * BLK)
            dc = dcr_ref[:, rs].T + dcc_ref[rs, :]
            dlf = _dot3_left(upper_incl, dc) + carry
            carry = carry + _colsum(dc)
            fl = f_ref[rs, :] + bf_ref[...]
            df = jnp.where(col < HEADS, dlf * _sigmoid(-fl), 0.0)
            df_ref[rs, :] = df
            dbf_ref[...] += _colsum(df)

    return pl.pallas_call(
        body, name=name, grid=(nb,),
        in_specs=[pl.BlockSpec((None, BLK, seq), lambda b: (b, 0, 0)), pl.BlockSpec((seq, BLK), lambda b: (b, 0)),
                  pl.BlockSpec((seq, BLK), lambda b: (b, 0)), pl.BlockSpec((1, BLK), lambda b: (0, 0))],
        out_specs=[pl.BlockSpec((seq, BLK), lambda b: (b, 0)), pl.BlockSpec((1, BLK), lambda b: (0, 0))],
        out_shape=[jax.ShapeDtypeStruct((nb * seq, BLK), F32), jax.ShapeDtypeStruct((1, BLK), F32)],
        compiler_params=pltpu.CompilerParams(dimension_semantics=("arbitrary",)),
    )(dcr, dcc, f, bf)


def _fox_cum_cols(cc_ref, t0, col, hp):
    cc = cc_ref[pl.ds(t0, BLK), :]
    c0 = jnp.sum(jnp.where(col == 2 * hp, cc, 0.0), axis=1, keepdims=True)
    c1 = jnp.sum(jnp.where(col == 2 * hp + 1, cc, 0.0), axis=1, keepdims=True)
    return c0, c1


def _fox_bias(c0, c1, cr_ref, s0, kw):
    return jnp.concatenate([c0 - cr_ref[0:1, pl.ds(s0, kw)], c1 - cr_ref[1:2, pl.ds(s0, kw)]], axis=0)


def _fox_fwd(p, cc, cr, nb, seq, *, name):
    nq = seq // BLK
    kw = min(KEY_BLOCK, seq)
    ksub = kw // BLK

    def body(q_ref, k_ref, v_ref, cc_ref, cr_ref, o_ref, lse_ref):
        hp = pl.program_id(1)
        row, col = _iotas()
        head0 = col < HEAD_DIM

        def qblock(qb, c):
            t0 = pl.multiple_of(qb * BLK, BLK)
            qs = _stack_heads(q_ref[pl.ds(t0, BLK), :], head0, _SCALE)
            c0, c1 = _fox_cum_cols(cc_ref, t0, col, hp)
            diag = qb // ksub

            def kblock(kb, carry, masked):
                acc, m, l = carry
                s0 = pl.multiple_of(kb * kw, kw)
                k = k_ref[pl.ds(s0, kw), :].astype(BF16)
                v0, v1 = _heads(v_ref[pl.ds(s0, kw), :], head0)
                s = _dot(qs, k, _NT) + _fox_bias(c0, c1, cr_ref, s0, kw)
                if masked:
                    s = jnp.where(_causal_mask(t0, s0, kw, True), s, -jnp.inf)
                m_new = jnp.maximum(m, jnp.max(s, axis=1, keepdims=True))
                pr = jnp.exp(s - m_new)
                alpha = jnp.exp(m - m_new)
                l = alpha * l + jnp.sum(pr, axis=1, keepdims=True)
                pr = pr.astype(BF16)
                acc = acc * jnp.where(head0, alpha[:BLK], alpha[BLK:]) + _dot(pr[:BLK], v0) + _dot(pr[BLK:], v1)
                return acc, m_new, l

            init = (jnp.zeros((BLK, BLK), F32), jnp.full((2 * BLK, 1), -jnp.inf, F32), jnp.zeros((2 * BLK, 1), F32))
            carry = lax.fori_loop(0, diag, lambda kb, cr: kblock(kb, cr, False), init)
            acc, m, l = kblock(diag, carry, True)
            o_ref[pl.ds(t0, BLK), :] = acc / jnp.where(head0, l[:BLK], l[BLK:])
            lse = m + jnp.log(l)
            lse_ref[pl.ds(t0, BLK), :] = jnp.where(head0, lse[:BLK], lse[BLK:])
            return c

        lax.fori_loop(0, nq, qblock, 0)

    return pl.pallas_call(
        body, name=name, grid=(nb, PAIRS),
        in_specs=[_pair_spec(seq, 0), _pair_spec(seq, PAIRS), _pair_spec(seq, 2 * PAIRS),
                  pl.BlockSpec((seq, BLK), lambda b, hp: (b, 0)), pl.BlockSpec((None, None, 8, seq), lambda b, hp: (b, hp, 0, 0))],
        out_specs=[_pair_spec(seq, 0), _stat_spec(seq)],
        out_shape=[jax.ShapeDtypeStruct((nb * seq, D_MODEL), F32), jax.ShapeDtypeStruct((nb, PAIRS, seq, BLK), F32)],
        **_ATT_PARAMS,
    )(p, p, p, cc, cr)


def _fox_bwd(p, do, o, lse, cc, cr, nb, seq, *, name):
    nq = seq // BLK
    kw = min(KEY_BLOCK, seq)
    ksub = kw // BLK

    def body(q_ref, k_ref, v_ref, do_ref, o_ref, lse_ref, cc_ref, cr_ref, dq_ref, dk_ref, dv_ref, dcr_ref, dcc_ref):
        hp = pl.program_id(1)
        row, col = _iotas()
        head0 = col < HEAD_DIM
        dk_ref[...] = jnp.zeros(dk_ref.shape, F32)
        dv_ref[...] = jnp.zeros(dv_ref.shape, F32)
        dcr_ref[...] = jnp.zeros(dcr_ref.shape, F32)

        def qblock(qb, c):
            t0 = pl.multiple_of(qb * BLK, BLK)
            qs = _stack_heads(q_ref[pl.ds(t0, BLK), :], head0, _SCALE)
            dof = do_ref[pl.ds(t0, BLK), :]
            dos = _stack_heads(dof, head0)
            prod = dof * o_ref[pl.ds(t0, BLK), :]
            dl = jnp.concatenate([jnp.sum(jnp.where(head0, prod, 0.0), axis=1, keepdims=True),
                                  jnp.sum(jnp.where(head0, 0.0, prod), axis=1, keepdims=True)], axis=0)
            lse = _pair_cols(lse_ref[pl.ds(t0, BLK), :], head0, -jnp.inf)
            c0, c1 = _fox_cum_cols(cc_ref, t0, col, hp)
            diag = qb // ksub

            def kblock(kb, carry, masked):
                dq, rs = carry
                s0 = pl.multiple_of(kb * kw, kw)
                kf = k_ref[pl.ds(s0, kw), :]
                k = kf.astype(BF16)
                k0, k1 = _heads(kf, head0)
                v = v_ref[pl.ds(s0, kw), :].astype(BF16)
                pr = jnp.exp(_dot(qs, k, _NT) + _fox_bias(c0, c1, cr_ref, s0, kw) - lse)
                if masked:
                    pr = jnp.where(_causal_mask(t0, s0, kw, True), pr, 0.0)
                ds = pr * (_dot(dos, v, _NT) - dl)
                dcr_ref[0:1, pl.ds(s0, kw)] -= _colsum(ds[:BLK])
                dcr_ref[1:2, pl.ds(s0, kw)] -= _colsum(ds[BLK:])
                rs = rs + jnp.sum(ds, axis=1, keepdims=True)
                ds = ds.astype(BF16)
                dk_ref[pl.ds(s0, kw), :] += _dot(ds, qs, _TN)
                dv_ref[pl.ds(s0, kw), :] += _dot(pr.astype(BF16), dos, _TN)
                return dq + _dot(ds[:BLK], k0) + _dot(ds[BLK:], k1), rs

            init = (jnp.zeros((BLK, BLK), F32), jnp.zeros((2 * BLK, 1), F32))
            carry = lax.fori_loop(0, diag, lambda kb, cr: kblock(kb, cr, False), init)
            dq, rs = kblock(diag, carry, True)
            dq_ref[pl.ds(t0, BLK), :] = dq * _SCALE
            dcc_ref[pl.ds(t0, BLK), :] = jnp.where(head0, rs[:BLK], rs[BLK:])
            return c

        lax.fori_loop(0, nq, qblock, 0)

    t = nb * seq
    return pl.pallas_call(
        body, name=name, grid=(nb, PAIRS),
        in_specs=[_pair_spec(seq, 0), _pair_spec(seq, PAIRS), _pair_spec(seq, 2 * PAIRS), _pair_spec(seq, 0), _pair_spec(seq, 0),
                  _stat_spec(seq), pl.BlockSpec((seq, BLK), lambda b, hp: (b, 0)),
                  pl.BlockSpec((None, None, 8, seq), lambda b, hp: (b, hp, 0, 0))],
        out_specs=[_pair_spec(seq, 0)] * 3 + [pl.BlockSpec((None, None, 8, seq), lambda b, hp: (b, hp, 0, 0)), _stat_spec(seq)],
        out_shape=[jax.ShapeDtypeStruct((t, D_MODEL), F32)] * 3 + [jax.ShapeDtypeStruct((nb, PAIRS, 8, seq), F32),
                                                                     jax.ShapeDtypeStruct((nb, PAIRS, seq, BLK), F32)],
        **_ATT_PARAMS,
    )(p, p, p, do, o, lse, cc, cr)


def _row_shards(x):
    return x.reshape(N_CHIPS, x.shape[0] // N_CHIPS, x.shape[1])


def _local_step(x3, tgt3, w):
    nb, seq, d = x3.shape
    t = nb * seq
    x0, tgt = x3.reshape(t, d), tgt3.reshape(t, d)
    g = {}

    a_gain = w["a_norm"].reshape(1, d)
    h_a = _rmsnorm(x0, a_gain, name="a_norm_fwd")
    p_a = _matmul(h_a, w["a_w_in"], name="a_in_fwd")
    o_a, tot_a = _sb_fwd(p_a, nb, seq, name="a_attn_fwd")
    y_a = _gate(o_a, p_a, 3, name="a_gate_fwd")
    x1 = _matmul(y_a, w["a_w_out"], name="a_out_fwd", residual=x0)

    b_gain = w["b_norm"].reshape(1, d)
    b_lg, b_lb = w["b_v_ln_g"].reshape(1, GM_W), w["b_v_ln_b"].reshape(1, GM_W)
    b_ws, b_bst = w["b_w_s"].reshape(GM_G, BLK, BLK), w["b_b_s"].reshape(GM_G, BLK).T
    h_b = _rmsnorm(x1, b_gain, name="b_norm_fwd")
    p_b = _matmul(h_b, w["b_w_in"], name="b_in_fwd")
    y_b = _gmlp_fwd(p_b, b_lg, b_lb, b_ws, b_bst, name="b_mix_fwd")
    x2 = _matmul(y_b, w["b_w_out"], name="b_out_fwd", residual=x1)

    c_gain = w["c_norm"].reshape(1, d)
    c_cw = jnp.repeat(w["c_conv_w"].reshape(CV_K, CV_W), SUBLANES, axis=0)
    c_cb = w["c_conv_b"].reshape(1, CV_W)
    c_lg, c_lb = w["c_ln_g"].reshape(1, CV_W), w["c_ln_b"].reshape(1, CV_W)
    h_c = _rmsnorm(x2, c_gain, name="c_norm_fwd")
    p_c = _matmul(h_c, w["c_w_in"], name="c_in_fwd")
    y_c = _conv_fwd(p_c, c_cw, c_cb, c_lg, c_lb, seq, name="c_conv_fwd")
    x3_ = _matmul(y_c, w["c_w_out"], name="c_out_fwd", residual=x2)

    d_gain = w["d_norm"].reshape(1, d)
    d_win = w["d_w_in"].reshape(d, 4 * D_MODEL + HEADS)
    d_wmain = d_win[:, :4 * D_MODEL]
    d_wf = jnp.pad(d_win[:, 4 * D_MODEL:], ((0, 0), (0, BLK - HEADS)))
    d_bf = jnp.pad(w["d_b_f"].reshape(1, HEADS), ((0, 0), (0, BLK - HEADS)))
    h_d = _rmsnorm(x3_, d_gain, name="d_norm_fwd")
    p_d = _matmul(h_d, d_wmain, name="d_in_fwd")
    f_d = _matmul(h_d, d_wf, name="d_inf_fwd")
    cc, cr = _fox_cum(f_d, d_bf, nb, seq, name="d_cum_fwd")
    cr = jnp.pad(cr.reshape(nb, PAIRS, 2, seq), ((0, 0), (0, 0), (0, 6), (0, 0)))
    o_d, lse_d = _fox_fwd(p_d, cc, cr, nb, seq, name="d_attn_fwd")
    y_d = _gate(o_d, p_d, 3, name="d_gate_fwd")
    x4 = _matmul(y_d, w["d_w_out"], name="d_out_fwd", residual=x3_)

    f_gain = w["final_norm"].reshape(1, d)
    dx, g_fn, loss_row = _loss_head(x4, f_gain, tgt, name="loss_head")
    g["final_norm"] = g_fn

    g["d_w_out"] = _row_shards(_matmul(y_d.T, dx, name="d_out_dw"))
    dy = _matmul(dx, w["d_w_out"], name="d_out_dy", mode="nt")
    do_d, dg_d = _gate_bwd(dy, o_d, p_d, 3, name="d_gate_bwd")
    dq, dk, dv, dcr, dcc = _fox_bwd(p_d, do_d, o_d, lse_d, cc, cr, nb, seq, name="d_attn_bwd")
    dcr = jnp.pad(dcr[:, :, :2, :].reshape(nb, HEADS, seq), ((0, 0), (0, BLK - HEADS), (0, 0)))
    dcc = dcc[:, :, :, ::HEAD_DIM].transpose(0, 2, 1, 3).reshape(t, HEADS)
    dcc = jnp.pad(dcc, ((0, 0), (0, BLK - HEADS)))
    df, dbf = _fox_cum_bwd(dcr, dcc, f_d, d_bf, nb, seq, name="d_cum_bwd")
    g["d_b_f"] = dbf[:, :HEADS]
    parts = [dq, dk, dv, dg_d]
    ht_d = h_d.T
    dws = [_matmul(ht_d, pt, name=f"d_in_dw{n}") for n, pt in enumerate(parts)]
    dwf = _matmul(ht_d, df, name="d_inf_dw")
    g["d_w_in"] = jnp.concatenate(dws + [dwf[:, :HEADS]], axis=1).reshape(d, N_CHIPS, -1).transpose(1, 0, 2)
    dh = _matmul(df, d_wf, name="d_inf_dh", mode="nt")
    for n, pt in enumerate(parts):
        dh = _matmul(pt, d_wmain[:, n * D_MODEL:(n + 1) * D_MODEL], name=f"d_in_dh{n}", mode="nt", residual=dh)
    dx, g["d_norm"] = _rmsnorm_bwd(dh, x3_, d_gain, dx, name="d_norm_bwd")

    g["c_w_out"] = _row_shards(_matmul(y_c.T, dx, name="c_out_dw"))
    dy = _matmul(dx, w["c_w_out"], name="c_out_dy", mode="nt")
    dy1, dgate, g["c_ln_g"], g["c_ln_b"], g["c_conv_b"], g["c_conv_w"] = _conv_bwd_post(
        dy, p_c, c_cw, c_cb, c_lg, c_lb, seq, name="c_conv_bwd_post")
    dp = _conv_bwd_pre(dy1, dgate, p_c, c_cw, seq, name="c_conv_bwd_pre")
    g["c_w_in"] = _matmul(h_c.T, dp, name="c_in_dw", out_shards=N_CHIPS)
    dh = _matmul(dp, w["c_w_in"], name="c_in_dh", mode="nt")
    dx, g["c_norm"] = _rmsnorm_bwd(dh, x2, c_gain, dx, name="c_norm_bwd")

    g["b_w_out"] = _row_shards(_matmul(y_b.T, dx, name="b_out_dw"))
    dy = _matmul(dx, w["b_w_out"], name="b_out_dy", mode="nt")
    dp, g["b_v_ln_g"], g["b_v_ln_b"], g["b_w_s"], dbst = _gmlp_bwd(dy, p_b, b_lg, b_lb, b_ws, b_bst, name="b_mix_bwd")
    g["b_b_s"] = dbst.T
    g["b_w_in"] = _matmul(h_b.T, dp, name="b_in_dw", out_shards=N_CHIPS)
    dh = _matmul(dp, w["b_w_in"], name="b_in_dh", mode="nt")
    dx, g["b_norm"] = _rmsnorm_bwd(dh, x1, b_gain, dx, name="b_norm_bwd")

    g["a_w_out"] = _row_shards(_matmul(y_a.T, dx, name="a_out_dw"))
    dy = _matmul(dx, w["a_w_out"], name="a_out_dy", mode="nt")
    do_a, dg_a = _gate_bwd(dy, o_a, p_a, 3, name="a_gate_bwd")
    dq, dk, dv = _sb_bwd(p_a, do_a, tot_a, nb, seq, name="a_attn_bwd")
    parts = [dq, dk, dv, dg_a]
    ht_a = h_a.T
    g["a_w_in"] = jnp.stack([_matmul(ht_a, pt, name=f"a_in_dw{n}") for n, pt in enumerate(parts)])
    dh = None
    for n, pt in enumerate(parts):
        dh = _matmul(pt, w["a_w_in"][n], name=f"a_in_dh{n}", mode="nt", residual=dh)
    dx, g["a_norm"] = _rmsnorm_bwd(dh, x0, a_gain, dx, name="a_norm_bwd")

    return loss_row[0, 0], dx.reshape(nb, seq, d), g


_HBM = pl.BlockSpec(memory_space=pltpu.HBM)


def _place():
    return lax.axis_index("x"), lax.axis_index("y"), lax.axis_index("c")


def _other_chips(x, y):
    return [(1 - x, y), (x, 1 - y), (1 - x, 1 - y)]


def _allgather_chips(ss, *, name):
    n_ops = len(ss)

    def body(*refs):
        s_refs, o_refs, (send_sems, recv_sems) = refs[:n_ops], refs[n_ops:2 * n_ops], refs[2 * n_ops:]
        x, y, c = _place()
        me = 2 * x + y
        chips = _other_chips(x, y)

        def copy(i, kk, src, dst, to):
            return pltpu.make_async_remote_copy(src_ref=src, dst_ref=dst, send_sem=send_sems.at[6 * i + kk],
                                                recv_sem=recv_sems.at[6 * i + kk], device_id=to, device_id_type=MESH)

        def half(i, j, hc):
            h = s_refs[i].shape[0] // 2
            return o_refs[i].at[j, pl.ds(hc * h, h), :]

        first = [copy(i, kk, s_refs[i].at[pl.ds(c * (s_refs[i].shape[0] // 2), s_refs[i].shape[0] // 2), :], half(i, me, c),
                      (cx, cy, c)) for kk, (cx, cy) in enumerate(chips) for i in range(n_ops)]
        for cp in first:
            cp.start()
        passed = []
        for kk, (cx, cy) in enumerate(chips):
            for i in range(n_ops):
                blk = half(i, 2 * cx + cy, c)
                copy(i, kk, blk, blk, (cx, cy, c)).wait_recv()
                fwd = copy(i, 3 + kk, blk, blk, (x, y, 1 - c))
                fwd.start()
                passed.append(fwd)
        for kk, (cx, cy) in enumerate(chips):
            for i in range(n_ops):
                blk = half(i, 2 * cx + cy, 1 - c)
                copy(i, 3 + kk, blk, blk, (x, y, 1 - c)).wait_recv()
        for cp in first + passed:
            cp.wait_send()

    for s in ss:
        assert s.shape[0] % 32 == 0, s.shape
    return pl.pallas_call(
        body, name=name, in_specs=[_HBM] * n_ops, out_specs=[_HBM] * n_ops,
        out_shape=[jax.ShapeDtypeStruct((N_CHIPS,) + s.shape, s.dtype) for s in ss],
        scratch_shapes=[pltpu.SemaphoreType.DMA((6 * n_ops,)), pltpu.SemaphoreType.DMA((6 * n_ops,))],
    )(*ss)


def _own_block(gathered, s):
    me = 2 * lax.axis_index("x") + lax.axis_index("y")
    return lax.dynamic_update_slice(gathered, s[None], (me,) + (0,) * s.ndim)


def _dma_sems(n):
    return [pltpu.SemaphoreType.DMA((n,)), pltpu.SemaphoreType.DMA((n,))]


def _swap_halves(gps, *, name):
    n_ops = len(gps)

    def body(*refs):
        g_refs, o_refs, (send_sems, recv_sems) = refs[:n_ops], refs[n_ops:2 * n_ops], refs[2 * n_ops:]
        x, y, c = _place()
        cps = []
        for i, (g_ref, o_ref) in enumerate(zip(g_refs, o_refs)):
            h = g_ref.shape[1] // 2
            cps.append(pltpu.make_async_remote_copy(
                src_ref=g_ref.at[:, pl.ds((1 - c) * h, h), :], dst_ref=o_ref, send_sem=send_sems.at[i], recv_sem=recv_sems.at[i],
                device_id=(x, y, 1 - c), device_id_type=MESH))
        for cp in cps:
            cp.start()
        for cp in cps:
            cp.wait()

    return pl.pallas_call(
        body, name=name, in_specs=[_HBM] * n_ops, out_specs=[_HBM] * n_ops,
        out_shape=[jax.ShapeDtypeStruct((g.shape[0], g.shape[1] // 2, g.shape[2]), g.dtype) for g in gps],
        scratch_shapes=_dma_sems(n_ops),
    )(*gps)


def _scatter_chips(hps, *, name):
    n_ops = len(hps)

    def body(*refs):
        h_refs, o_refs, (send_sems, recv_sems) = refs[:n_ops], refs[n_ops:2 * n_ops], refs[2 * n_ops:]
        x, y, c = _place()
        cps = [pltpu.make_async_remote_copy(src_ref=h_ref.at[2 * cx + cy], dst_ref=o_ref.at[kk], send_sem=send_sems.at[3 * i + kk],
                                            recv_sem=recv_sems.at[3 * i + kk], device_id=(cx, cy, c), device_id_type=MESH)
               for i, (h_ref, o_ref) in enumerate(zip(h_refs, o_refs)) for kk, (cx, cy) in enumerate(_other_chips(x, y))]
        for cp in cps:
            cp.start()
        for cp in cps:
            cp.wait()

    return pl.pallas_call(
        body, name=name, in_specs=[_HBM] * n_ops, out_specs=[_HBM] * n_ops,
        out_shape=[jax.ShapeDtypeStruct((3,) + hp.shape[1:], hp.dtype) for hp in hps],
        scratch_shapes=_dma_sems(3 * n_ops),
    )(*hps)


def _join_halves(fs, *, name):
    n_ops = len(fs)

    def body(*refs):
        f_refs, o_refs, (send_sems, recv_sems) = refs[:n_ops], refs[n_ops:2 * n_ops], refs[2 * n_ops:]
        x, y, c = _place()
        cps = [pltpu.make_async_remote_copy(src_ref=f_ref, dst_ref=o_ref, send_sem=send_sems.at[i], recv_sem=recv_sems.at[i],
                                            device_id=(x, y, 1 - c), device_id_type=MESH)
               for i, (f_ref, o_ref) in enumerate(zip(f_refs, o_refs))]
        for cp in cps:
            cp.start()
        for cp in cps:
            cp.wait()

    theirs = pl.pallas_call(
        body, name=name, in_specs=[_HBM] * n_ops, out_specs=[_HBM] * n_ops,
        out_shape=[jax.ShapeDtypeStruct(f.shape, f.dtype) for f in fs], scratch_shapes=_dma_sems(n_ops),
    )(*fs)
    south = lax.axis_index("c") == 0
    return [jnp.concatenate([jnp.where(south, f, t), jnp.where(south, t, f)], axis=0) for f, t in zip(fs, theirs)]


def _add_halves(gp, ra, wire_dtype, *, name, bm=256):
    n, r, c_ = gp.shape
    h = r // 2
    bm = _tile(h, bm)
    per = h // bm
    c = lax.axis_index("c").astype(jnp.int32).reshape(1)

    def body(c_ref, g_ref, ra_ref, o_ref, ow_ref):
        s = g_ref[...] + ra_ref[...]
        o_ref[...] = s
        ow_ref[...] = s.astype(wire_dtype)

    mine = pl.BlockSpec((None, bm, c_), lambda j, i, cr: (j, i, 0))
    return pl.pallas_call(
        body, name=name,
        grid_spec=pltpu.PrefetchScalarGridSpec(
            num_scalar_prefetch=1, grid=(n, per),
            in_specs=[pl.BlockSpec((None, bm, c_), lambda j, i, cr: (j, cr[0] * per + i, 0)), mine],
            out_specs=[mine, mine]),
        out_shape=[jax.ShapeDtypeStruct((n, h, c_), F32), jax.ShapeDtypeStruct((n, h, c_), wire_dtype)],
        compiler_params=pltpu.CompilerParams(dimension_semantics=("parallel", "parallel")),
    )(c, gp, ra)


def _add_chips(hp, rb, *, name, bm=256):
    n, h, c_ = hp.shape
    bm = _tile(h, bm)
    me = (2 * lax.axis_index("x") + lax.axis_index("y")).astype(jnp.int32).reshape(1)

    def body(me_ref, h_ref, rb_ref, o_ref):
        o_ref[...] = ((h_ref[...] + rb_ref[0].astype(F32)) + rb_ref[1].astype(F32)) + rb_ref[2].astype(F32)

    return pl.pallas_call(
        body, name=name,
        grid_spec=pltpu.PrefetchScalarGridSpec(
            num_scalar_prefetch=1, grid=(h // bm,),
            in_specs=[pl.BlockSpec((None, bm, c_), lambda i, mr: (mr[0], i, 0)),
                      pl.BlockSpec((3, bm, c_), lambda i, mr: (0, i, 0))],
            out_specs=pl.BlockSpec((bm, c_), lambda i, mr: (i, 0))),
        out_shape=jax.ShapeDtypeStruct((h, c_), F32),
        compiler_params=pltpu.CompilerParams(dimension_semantics=("parallel",)),
    )(me, hp, rb)


def _reduce_scatter(gps, wire_dtypes, *, tag):
    ras = _swap_halves(gps, name=f"{tag}_swap_halves")
    hps = [_add_halves(gp, ra, wd, name=f"{tag}_add_halves{i}") for i, (gp, ra, wd) in enumerate(zip(gps, ras, wire_dtypes))]
    rbs = _scatter_chips([hw for _, hw in hps], name=f"{tag}_scatter_chips")
    fs = [_add_chips(hf, rb, name=f"{tag}_add_chips{i}") for i, ((hf, _), rb) in enumerate(zip(hps, rbs))]
    return _join_halves(fs, name=f"{tag}_join_halves")


def _adamw(w, g, m, v, *, name):
    r, c_ = w.shape
    bm = r
    for cand in (512, 256, 128, 64, 32, 16, 8):
        if r % cand == 0:
            bm = cand
            break
    c1 = 1.0 - ADAM_B1 ** ADAM_STEP
    c2 = 1.0 - ADAM_B2 ** ADAM_STEP

    def body(w_ref, g_ref, m_ref, v_ref, d_ref, nm_ref, nv_ref):
        g_ = g_ref[...]
        m_ = ADAM_B1 * m_ref[...] + (1.0 - ADAM_B1) * g_
        v_ = ADAM_B2 * v_ref[...] + (1.0 - ADAM_B2) * (g_ * g_)
        d_ref[...] = -ADAM_LR * ((m_ / c1) / (jnp.sqrt(v_ / c2) + ADAM_EPS) + ADAM_WD * w_ref[...])
        nm_ref[...] = m_
        nv_ref[...] = v_

    spec = pl.BlockSpec((bm, c_), lambda i: (i, 0))
    return pl.pallas_call(
        body, name=name, grid=(r // bm,), in_specs=[spec] * 4, out_specs=[spec] * 3,
        out_shape=[jax.ShapeDtypeStruct((r, c_), F32)] * 3,
        compiler_params=pltpu.CompilerParams(dimension_semantics=("parallel",)),
    )(w, g, m, v)


_WEIGHTS = ["a_norm", "a_w_in", "a_w_out", "b_norm", "b_w_in", "b_v_ln_g", "b_v_ln_b", "b_w_s", "b_b_s", "b_w_out",
            "c_norm", "c_w_in", "c_conv_w", "c_conv_b", "c_ln_g", "c_ln_b", "c_w_out", "d_norm", "d_w_in", "d_b_f",
            "d_w_out", "final_norm"]
_SHARD_AXIS = {"a_norm": None, "a_w_in": 2, "a_w_out": 1, "b_norm": 1, "b_w_in": 2, "b_v_ln_g": 1, "b_v_ln_b": 1, "b_w_s": None,
               "b_b_s": None, "b_w_out": 1, "c_norm": 1, "c_w_in": 2, "c_conv_w": 2, "c_conv_b": 1, "c_ln_g": 1, "c_ln_b": 1,
               "c_w_out": 1, "d_norm": 1, "d_w_in": 2, "d_b_f": None, "d_w_out": 1, "final_norm": None}
_BIG = ["a_w_in", "a_w_out", "b_w_in", "b_w_out", "c_w_in", "c_w_out", "d_w_in", "d_w_out"]
_SMALL_SHARDED = [n for n in _WEIGHTS if _SHARD_AXIS[n] is not None and n not in _BIG]
_REPLICATED = [n for n in _WEIGHTS if _SHARD_AXIS[n] is None]
_ROW_ALIGN = 32
_ROW_ALIGN_SUMMED = 128


def _pack(pieces, dtype, align=_ROW_ALIGN):
    flat = jnp.concatenate([p.reshape(-1).astype(dtype) for p in pieces])
    unit = align * PACK_C
    total = -(-flat.shape[0] // unit) * unit
    return jnp.pad(flat, (0, total - flat.shape[0])).reshape(total // PACK_C, PACK_C)


def _unpack(flat, shapes):
    out, off = [], 0
    for s in shapes:
        n = math.prod(s)
        out.append(flat[off:off + n].reshape(s))
        off += n
    return out


def _full_shape(local_shape, axis):
    s = list(local_shape)
    if axis is not None:
        s[axis] *= N_CHIPS
    return tuple(s)


def _gather_weights(local):
    full = {n: local[n][0] if n != "final_norm" else local[n] for n in _REPLICATED}
    mine = [local[n][0].astype(BF16) for n in _BIG] + [_pack([local[n] for n in _SMALL_SHARDED], F32)]
    got = [_own_block(gt, s) for gt, s in zip(_allgather_chips(mine, name="gather_weights"), mine)]
    for n, gt in zip(_BIG, got):
        if _SHARD_AXIS[n] == 1:
            full[n] = gt.reshape(-1, gt.shape[-1])
        elif n == "d_w_in":
            full[n] = gt.transpose(1, 0, 2).reshape(gt.shape[1], -1)
        else:
            full[n] = gt
    small = got[-1].reshape(N_CHIPS, -1)
    shards = [_unpack(small[j], [local[n].shape[1:] for n in _SMALL_SHARDED]) for j in range(N_CHIPS)]
    for i, n in enumerate(_SMALL_SHARDED):
        full[n] = jnp.concatenate([shards[j][i] for j in range(N_CHIPS)], axis=_SHARD_AXIS[n] - 1)
    return full


def _repl_piece_len(local):
    total = sum(math.prod(local[n].shape) for n in _REPLICATED)
    return -(-total // N_CHIPS)


def _reduce_grads(g, local):
    rep_flat = jnp.concatenate([g[n].reshape(-1) for n in _REPLICATED])
    piece = _repl_piece_len(local)
    rep_flat = jnp.pad(rep_flat, (0, N_CHIPS * piece - rep_flat.shape[0]))

    def shard(n, j):
        full = g[n].reshape(_full_shape(local[n].shape, _SHARD_AXIS[n]))
        width = local[n].shape[_SHARD_AXIS[n]]
        return lax.slice_in_dim(full, j * width, (j + 1) * width, axis=_SHARD_AXIS[n])

    small = jnp.stack([_pack([shard(n, j) for n in _SMALL_SHARDED] + [rep_flat[j * piece:(j + 1) * piece]], F32)
                       for j in range(N_CHIPS)])
    summed = _reduce_scatter([g[n] for n in _BIG] + [small], [BF16] * len(_BIG) + [F32], tag="grads")
    red = {n: s.reshape(local[n].shape) for n, s in zip(_BIG, summed)}
    out = _unpack(summed[-1].reshape(-1), [local[n].shape for n in _SMALL_SHARDED] + [(piece,)])
    red.update(zip(_SMALL_SHARDED, out[:-1]))
    rep_mine = _pack([out[-1]], F32)
    rep = _own_block(_allgather_chips([rep_mine], name="gather_replicated_grads")[0], rep_mine)
    rep = rep.reshape(N_CHIPS, -1)[:, :piece].reshape(-1)
    for n, val in zip(_REPLICATED, _unpack(rep, [local[n].shape for n in _REPLICATED])):
        red[n] = val
    return red


def _update(local, grads, m, v):
    delta, new_m, new_v = {}, {}, {}
    for n in _BIG:
        shp = local[n].shape
        two = (shp[-2], shp[-1])
        res = _adamw(local[n].reshape(two), grads[n].reshape(two), m[n].reshape(two), v[n].reshape(two), name=f"adamw_{n}")
        delta[n], new_m[n], new_v[n] = [r.reshape(shp) for r in res]
    small = [n for n in _WEIGHTS if n not in _BIG]
    shapes = [local[n].shape for n in small]
    packed = [_pack([src[n] for n in small], F32) for src in (local, grads, m, v)]
    res = _adamw(*packed, name="adamw_small")
    for dst, r in zip((delta, new_m, new_v), res):
        for n, val in zip(small, _unpack(r.reshape(-1), shapes)):
            dst[n] = val
    return delta, new_m, new_v


def kernel(x, a_norm, a_w_in, a_w_out, b_norm, b_w_in, b_v_ln_g, b_v_ln_b, b_w_s, b_b_s, b_w_out, c_norm, c_w_in, c_conv_w, c_conv_b, c_ln_g, c_ln_b, c_w_out, d_norm, d_w_in, d_b_f, d_w_out, final_norm, loss_target, m_a_norm, m_a_w_in, m_a_w_out, m_b_norm, m_b_w_in, m_b_v_ln_g, m_b_v_ln_b, m_b_w_s, m_b_b_s, m_b_w_out, m_c_norm, m_c_w_in, m_c_conv_w, m_c_conv_b, m_c_ln_g, m_c_ln_b, m_c_w_out, m_d_norm, m_d_w_in, m_d_b_f, m_d_w_out, m_final_norm, v_a_norm, v_a_w_in, v_a_w_out, v_b_norm, v_b_w_in, v_b_v_ln_g, v_b_v_ln_b, v_b_w_s, v_b_b_s, v_b_w_out, v_c_norm, v_c_w_in, v_c_conv_w, v_c_conv_b, v_c_ln_g, v_c_ln_b, v_c_w_out, v_d_norm, v_d_w_in, v_d_b_f, v_d_w_out, v_final_norm):
    local = dict(zip(_WEIGHTS, (a_norm, a_w_in, a_w_out, b_norm, b_w_in, b_v_ln_g, b_v_ln_b, b_w_s, b_b_s, b_w_out, c_norm, c_w_in,
                                c_conv_w, c_conv_b, c_ln_g, c_ln_b, c_w_out, d_norm, d_w_in, d_b_f, d_w_out, final_norm)))
    m = dict(zip(_WEIGHTS, (m_a_norm, m_a_w_in, m_a_w_out, m_b_norm, m_b_w_in, m_b_v_ln_g, m_b_v_ln_b, m_b_w_s, m_b_b_s, m_b_w_out,
                            m_c_norm, m_c_w_in, m_c_conv_w, m_c_conv_b, m_c_ln_g, m_c_ln_b, m_c_w_out, m_d_norm, m_d_w_in, m_d_b_f,
                            m_d_w_out, m_final_norm)))
    v = dict(zip(_WEIGHTS, (v_a_norm, v_a_w_in, v_a_w_out, v_b_norm, v_b_w_in, v_b_v_ln_g, v_b_v_ln_b, v_b_w_s, v_b_b_s, v_b_w_out,
                            v_c_norm, v_c_w_in, v_c_conv_w, v_c_conv_b, v_c_ln_g, v_c_ln_b, v_c_w_out, v_d_norm, v_d_w_in, v_d_b_f,
                            v_d_w_out, v_final_norm)))
    loss_part, grad_x, g = _local_step(x, loss_target, _gather_weights(local))
    loss = lax.psum(loss_part, ("x", "y", "c"))
    grads = _reduce_grads(g, local)
    delta, new_m, new_v = _update(local, grads, m, v)
    return (loss, grad_x, *[grads[n] for n in _WEIGHTS], *[delta[n] for n in _WEIGHTS],
            *[new_m[n] for n in _WEIGHTS], *[new_v[n] for n in _WEIGHTS])
```

```python
import functools
import math

import jax
import jax.numpy as jnp
from jax import lax
from jax.experimental import pallas as pl
from jax.experimental.pallas import tpu as pltpu

F32, BF16 = jnp.float32, jnp.bfloat16
MESH = pl.DeviceIdType.MESH

D_MODEL = 1024
HEADS = 16
HEAD_DIM = 64
BLK = 128
PAIRS = HEADS // 2
GM_W = 2048
GM_G = 16
CV_W = 2048
CV_K = 31
HALO = 32
EPS = 1e-6
N_CHIPS = 4
PACK_C = 1024
ADAM_LR, ADAM_B1, ADAM_B2, ADAM_EPS, ADAM_WD, ADAM_STEP = 0.001, 0.9, 0.999, 1e-08, 0.01, 10

_NT = (((1,), (1,)), ((), ()))
_TN = (((0,), (0,)), ((), ()))
_NN = (((1,), (0,)), ((), ()))


def _dot(a, b, dims=_NN):
    return lax.dot_general(a, b, dims, preferred_element_type=F32)


def _split3(x):
    hi = x.astype(BF16)
    r = x - hi.astype(F32)
    mid = r.astype(BF16)
    lo = (r - mid.astype(F32)).astype(BF16)
    return hi, mid, lo


def _dot3_right(x, m):
    hi, mid, lo = _split3(x)
    return _dot(hi, m) + _dot(mid, m) + _dot(lo, m)


def _dot3_left(m, x):
    hi, mid, lo = _split3(x)
    return _dot(m, hi) + _dot(m, mid) + _dot(m, lo)


def _sigmoid(x):
    return 1.0 / (1.0 + jnp.exp(-x))


def _silu(x):
    return x * _sigmoid(x)


def _dsilu(x):
    s = _sigmoid(x)
    return s * (1.0 + x * (1.0 - s))


_GELU_C = math.sqrt(2.0 / math.pi)
_GELU_A = 0.044715


def _gelu(x):
    return 0.5 * x * (1.0 + jnp.tanh(_GELU_C * (x + _GELU_A * x * x * x)))


def _dgelu(x):
    t = jnp.tanh(_GELU_C * (x + _GELU_A * x * x * x))
    return 0.5 * (1.0 + t) + 0.5 * x * (1.0 - t * t) * _GELU_C * (1.0 + 3.0 * _GELU_A * x * x)


def _log_sigmoid(x):
    return jnp.minimum(x, 0.0) - jnp.log(1.0 + jnp.exp(-jnp.abs(x)))


def _rms_fwd(x, g):
    r = lax.rsqrt(jnp.mean(x * x, axis=-1, keepdims=True) + EPS)
    return x * r * g


def _rms_bwd(dy, x, g):
    r = lax.rsqrt(jnp.mean(x * x, axis=-1, keepdims=True) + EPS)
    xh = x * r
    dxh = dy * g
    dx = r * (dxh - xh * jnp.mean(dxh * xh, axis=-1, keepdims=True))
    return dx, dy * xh


def _ln_stats(x):
    mu = jnp.mean(x, axis=-1, keepdims=True)
    xc = x - mu
    r = lax.rsqrt(jnp.mean(xc * xc, axis=-1, keepdims=True) + EPS)
    return xc * r, r


def _ln_bwd(dy, xh, r, g):
    dxh = dy * g
    return r * (dxh - jnp.mean(dxh, axis=-1, keepdims=True) - xh * jnp.mean(dxh * xh, axis=-1, keepdims=True))


def _colsum(x):
    return jnp.sum(x, axis=0, keepdims=True)


def _tile(n, want):
    for t in range(min(n, want), 7, -1):
        if n % t == 0 and t % 8 == 0:
            return t
    return n


MM_TILE = 1024


def _matmul(a, b, *, name, mode="nn", residual=None, out_shards=1):
    (m, k) = a.shape
    b_shards = b.shape[0] if b.ndim == 3 else 1
    if mode == "nn":
        n = b.shape[-1] * b_shards
        tn, tk = _tile(n // max(b_shards, out_shards), MM_TILE), _tile(k, MM_TILE)
    else:
        n = b.shape[-2]
        tn, tk = _tile(n // out_shards, MM_TILE), _tile(k // b_shards, MM_TILE)
    tm = _tile(m, MM_TILE)
    nk = k // tk
    a_spec = pl.BlockSpec((tm, tk), lambda i, j, kk: (i, kk))
    if mode == "nn":
        dims = _NN
        if b_shards == 1:
            b_spec = pl.BlockSpec((tk, tn), lambda i, j, kk: (kk, j))
        else:
            per_b = n // b_shards // tn
            b_spec = pl.BlockSpec((None, tk, tn), lambda i, j, kk: (j // per_b, kk, j % per_b))
    else:
        dims = _NT
        if b_shards == 1:
            b_spec = pl.BlockSpec((tn, tk), lambda i, j, kk: (j, kk))
        else:
            per_b = k // b_shards // tk
            b_spec = pl.BlockSpec((None, tn, tk), lambda i, j, kk: (kk // per_b, j, kk % per_b))
    if out_shards == 1:
        o_spec = pl.BlockSpec((tm, tn), lambda i, j, kk: (i, j))
        o_shape = (m, n)
    else:
        per_o = n // out_shards // tn
        o_spec = pl.BlockSpec((None, tm, tn), lambda i, j, kk: (j // per_o, i, j % per_o))
        o_shape = (out_shards, m, n // out_shards)
    has_res = residual is not None

    def body(a_ref, b_ref, *rest):
        o_ref = rest[-1]
        kk = pl.program_id(2)
        part = _dot(a_ref[...].astype(BF16), b_ref[...].astype(BF16), dims)
        if has_res:
            @pl.when(kk == 0)
            def _():
                o_ref[...] = part + rest[0][...]
        else:
            @pl.when(kk == 0)
            def _():
                o_ref[...] = part

        @pl.when(kk > 0)
        def _():
            o_ref[...] += part

    return pl.pallas_call(
        body, name=name, grid=(m // tm, n // tn, nk),
        in_specs=[a_spec, b_spec] + ([o_spec] if has_res else []),
        out_specs=o_spec, out_shape=jax.ShapeDtypeStruct(o_shape, F32),
        compiler_params=pltpu.CompilerParams(dimension_semantics=("parallel", "parallel", "arbitrary")),
    )(a, b, *([residual] if has_res else []))


def _rows(fn, *, name, steps, ins, outs, accs=(), scratch=()):
    ni, no, na = len(ins), len(outs), len(accs)

    def body(*refs):
        in_refs, out_refs = refs[:ni], refs[ni:ni + no]
        acc_refs, scr = refs[ni + no:ni + no + na], refs[ni + no + na:]
        i = pl.program_id(0)

        @pl.when(i == 0)
        def _():
            for r in acc_refs:
                r[...] = jnp.zeros(r.shape, r.dtype)

        fn(i, in_refs, out_refs, acc_refs, scr)

    def full(shape):
        nd = len(shape)
        return pl.BlockSpec(tuple(shape), lambda i: (0,) * nd)

    res = pl.pallas_call(
        body, name=name, grid=(steps,),
        in_specs=[pl.BlockSpec(bs, im) for _, bs, im in ins],
        out_specs=[pl.BlockSpec(bs, im) for _, _, bs, im in outs] + [full(s) for s, _ in accs],
        out_shape=[jax.ShapeDtypeStruct(s, d) for s, d, _, _ in outs] + [jax.ShapeDtypeStruct(s, d) for s, d in accs],
        scratch_shapes=list(scratch),
        compiler_params=pltpu.CompilerParams(dimension_semantics=("arbitrary",)),
    )(*[a for a, _, _ in ins])
    return res


def _rb(arr, bm, cb=0, width=None):
    w = arr.shape[1] if width is None else width
    return (arr, (bm, w), lambda i: (i, cb))


def _const(arr):
    nd = arr.ndim
    return (arr, tuple(arr.shape), lambda i: (0,) * nd)


def _ro(t, w, dtype, bm):
    return ((t, w), dtype, (bm, w), lambda i: (i, 0))


def _rot(t, w, dtype, bm):
    return ((w, t), dtype, (w, bm), lambda i: (0, i))


def _rmsnorm(x, g, *, name, bm=512):
    t, d = x.shape
    bm = _tile(t, bm)

    def fn(i, ins, outs, accs, scr):
        h = _rms_fwd(ins[0][...], ins[1][...])
        outs[0][...] = h.astype(BF16)
        outs[1][...] = h.T.astype(BF16)

    return _rows(fn, name=name, steps=t // bm, ins=[_rb(x, bm), _const(g)], outs=[_ro(t, d, BF16, bm), _rot(t, d, BF16, bm)])


def _rmsnorm_bwd(dh, x, g, dres, *, name, bm=512):
    t, d = x.shape
    bm = _tile(t, bm)

    def fn(i, ins, outs, accs, scr):
        dx, dgrow = _rms_bwd(ins[0][...], ins[1][...], ins[2][...])
        outs[0][...] = ins[3][...] + dx
        accs[0][...] += _colsum(dgrow)

    return _rows(fn, name=name, steps=t // bm, ins=[_rb(dh, bm), _rb(x, bm), _const(g), _rb(dres, bm)],
                 outs=[_ro(t, d, F32, bm)], accs=[((1, d), F32)])


def _gate(o, p, gcb, *, name, bm=512):
    t, w = o.shape
    bm = _tile(t, bm)

    def fn(i, ins, outs, accs, scr):
        y = ins[0][...] * _silu(ins[1][...])
        outs[0][...] = y.astype(BF16)
        outs[1][...] = y.T.astype(BF16)

    return _rows(fn, name=name, steps=t // bm, ins=[_rb(o, bm), _rb(p, bm, gcb, w)],
                 outs=[_ro(t, w, BF16, bm), _rot(t, w, BF16, bm)])


def _gate_bwd(dy, o, p, gcb, *, name, bm=512):
    t, w = o.shape
    bm = _tile(t, bm)

    def fn(i, ins, outs, accs, scr):
        dy_, o_, g_ = ins[0][...], ins[1][...], ins[2][...]
        outs[0][...] = dy_ * _silu(g_)
        outs[1][...] = dy_ * o_ * _dsilu(g_)

    return _rows(fn, name=name, steps=t // bm, ins=[_rb(dy, bm), _rb(o, bm), _rb(p, bm, gcb, w)],
                 outs=[_ro(t, w, F32, bm), _ro(t, w, F32, bm)])


def _loss_head(x, g, tgt, *, name, bm=512):
    t, d = x.shape
    bm = _tile(t, bm)

    def fn(i, ins, outs, accs, scr):
        x_, g_, tg = ins[0][...], ins[1][...], ins[2][...]
        err = _rms_fwd(x_, g_) - tg
        part = 0.5 * jnp.sum(jnp.sum(err * err, axis=-1, keepdims=True), axis=0, keepdims=True) / d
        dx, dgrow = _rms_bwd(err / d, x_, g_)
        outs[0][...] = dx
        accs[0][...] += _colsum(dgrow)
        accs[1][...] += jnp.broadcast_to(part, (1, BLK))

    return _rows(fn, name=name, steps=t // bm, ins=[_rb(x, bm), _const(g), _rb(tgt, bm)],
                 outs=[_ro(t, d, F32, bm)], accs=[((1, d), F32), ((1, BLK), F32)])


def _gmlp_mix_weights(ws_ref, g):
    row = lax.broadcasted_iota(jnp.int32, (BLK, BLK), 0)
    col = lax.broadcasted_iota(jnp.int32, (BLK, BLK), 1)
    tril = col <= row
    return jnp.where(tril, ws_ref[g], 0.0), tril


def _gmlp_fwd(p, ln_g, ln_b, w_s, bs_t, *, name):
    t = p.shape[0]

    def fn(i, ins, outs, accs, scr):
        p_ref, lg, lb, ws_ref, bst = ins
        vn = _ln_stats(_gelu(p_ref[:, GM_W:2 * GM_W]))[0] * lg[...] + lb[...]
        for g in range(GM_G):
            cs = slice(g * BLK, (g + 1) * BLK)
            wt, _ = _gmlp_mix_weights(ws_ref, g)
            s = _dot(wt.astype(BF16), vn[:, cs].astype(BF16)) + bst[:, g:g + 1]
            u = _gelu(p_ref[:, cs])
            gate = p_ref[:, 2 * GM_W + g * BLK:2 * GM_W + (g + 1) * BLK]
            y = u * s * _silu(gate)
            outs[0][:, cs] = y.astype(BF16)
            outs[1][cs, :] = y.T.astype(BF16)

    return _rows(fn, name=name, steps=t // BLK, ins=[_rb(p, BLK), _const(ln_g), _const(ln_b), _const(w_s), _const(bs_t)],
                 outs=[_ro(t, GM_W, BF16, BLK), _rot(t, GM_W, BF16, BLK)])


def _gmlp_bwd(dy, p, ln_g, ln_b, w_s, bs_t, *, name):
    t = p.shape[0]

    def fn(i, ins, outs, accs, scr):
        dy_ref, p_ref, lg, lb, ws_ref, bst = ins
        dp_ref = outs[0]
        dlg, dlb, dws, dbst = accs
        dvn_ref = scr[0]
        v_pre = p_ref[:, GM_W:2 * GM_W]
        xh, r = _ln_stats(_gelu(v_pre))
        vn = xh * lg[...] + lb[...]
        for g in range(GM_G):
            cs = slice(g * BLK, (g + 1) * BLK)
            gs = slice(2 * GM_W + g * BLK, 2 * GM_W + (g + 1) * BLK)
            wt, tril = _gmlp_mix_weights(ws_ref, g)
            vg = vn[:, cs].astype(BF16)
            s = _dot(wt.astype(BF16), vg) + bst[:, g:g + 1]
            u_pre, gate, dyg = p_ref[:, cs], p_ref[:, gs], dy_ref[:, cs]
            u = _gelu(u_pre)
            dos = dyg * _silu(gate)
            dp_ref[:, gs] = dyg * u * s * _dsilu(gate)
            dp_ref[:, cs] = dos * s * _dgelu(u_pre)
            ds = (dos * u).astype(BF16)
            dws[g] += jnp.where(tril, _dot(ds, vg, _NT), 0.0)
            dbst[:, g:g + 1] += jnp.sum(dos * u, axis=1, keepdims=True)
            dvn_ref[:, cs] = _dot(wt.astype(BF16), ds, _TN)
        dvn = dvn_ref[...]
        dlg[...] += _colsum(dvn * xh)
        dlb[...] += _colsum(dvn)
        dp_ref[:, GM_W:2 * GM_W] = _ln_bwd(dvn, xh, r, lg[...]) * _dgelu(v_pre)

    return _rows(fn, name=name, steps=t // BLK,
                 ins=[_rb(dy, BLK), _rb(p, BLK), _const(ln_g), _const(ln_b), _const(w_s), _const(bs_t)],
                 outs=[_ro(t, 3 * GM_W, F32, BLK)],
                 accs=[((1, GM_W), F32), ((1, GM_W), F32), ((GM_G, BLK, BLK), F32), ((BLK, GM_G), F32)],
                 scratch=[pltpu.VMEM((BLK, GM_W), F32)])


CV_BM = 128
CV_RC = 8
SUBLANES = 8
CV_FWD_OFFS = [HALO - (CV_K - 1) + k for k in range(CV_K)]
CV_BWD_OFFS = [CV_K - 1 - k for k in range(CV_K)]


def _conv_halo_prev(p, cb, bm):
    per = bm // HALO
    return (p, (HALO, CV_W), lambda i: (jnp.maximum(i * per - 1, 0), cb))


def _conv_scratch(bm):
    return [pltpu.VMEM((bm + HALO, CV_W), F32), pltpu.VMEM((SUBLANES - 1, bm + HALO - SUBLANES, CV_W), F32),
            pltpu.VMEM((bm, CV_W), F32)]


def _conv_shift_copies(ext_ref, sh_ref):
    rows = sh_ref.shape[1]
    for b in range(1, SUBLANES):
        sh_ref[b - 1] = ext_ref[pl.ds(b, rows), :]


def _conv_window(ext_ref, sh_ref, off, r0, rows):
    b = off % SUBLANES
    src = ext_ref if b == 0 else sh_ref.at[b - 1]
    return src[pl.ds(r0 + (off - b), rows), :]


def _conv_taps(ext_ref, sh_ref, cw_ref, y_ref, offs):
    bm = y_ref.shape[0]

    def chunk(ci, c):
        r0 = pl.multiple_of(ci * CV_RC, CV_RC)
        acc = jnp.zeros((CV_RC, CV_W), F32)
        for k in range(CV_K):
            acc = acc + cw_ref[pl.ds(k * SUBLANES, CV_RC), :] * _conv_window(ext_ref, sh_ref, offs[k], r0, CV_RC)
        y_ref[pl.ds(r0, CV_RC), :] = acc
        return c

    lax.fori_loop(0, bm // CV_RC, chunk, 0)


def _conv_dweights(dy1_ref, ext_ref, sh_ref, dcw_ref):
    bm = dy1_ref.shape[0]
    for k in range(CV_K):
        def chunk(ci, acc, off=CV_FWD_OFFS[k]):
            r0 = pl.multiple_of(ci * CV_RC, CV_RC)
            return acc + dy1_ref[pl.ds(r0, CV_RC), :] * _conv_window(ext_ref, sh_ref, off, r0, CV_RC)

        dcw_ref[k:k + 1, :] += _colsum(lax.fori_loop(0, bm // CV_RC, chunk, jnp.zeros((CV_RC, CV_W), F32)))


def _conv_fill(i, ext_ref, a_prev, b_prev, a, b, bm, seq):
    keep = jnp.where((i % (seq // bm)) == 0, 0.0, 1.0)
    ext_ref[pl.ds(0, HALO), :] = keep * (a_prev * _sigmoid(b_prev))
    ext_ref[pl.ds(HALO, bm), :] = a * _sigmoid(b)


def _conv_fwd(p, cw, cb, ln_g, ln_b, seq, *, name, bm=CV_BM):
    t = p.shape[0]

    def fn(i, ins, outs, accs, scr):
        a, b, gate, ap, bp = [r[...] for r in ins[:5]]
        cw_ref, cb_, lg, lb = ins[5], ins[6][...], ins[7][...], ins[8][...]
        ext, sh, y = scr
        _conv_fill(i, ext, ap, bp, a, b, bm, seq)
        _conv_shift_copies(ext, sh)
        _conv_taps(ext, sh, cw_ref, y, CV_FWD_OFFS)
        y2 = _ln_stats(y[...] + cb_)[0] * lg + lb
        out = _silu(y2) * _silu(gate)
        outs[0][...] = out.astype(BF16)
        outs[1][...] = out.T.astype(BF16)

    return _rows(fn, name=name, steps=t // bm,
                 ins=[_rb(p, bm, 0, CV_W), _rb(p, bm, 1, CV_W), _rb(p, bm, 2, CV_W),
                      _conv_halo_prev(p, 0, bm), _conv_halo_prev(p, 1, bm),
                      _const(cw), _const(cb), _const(ln_g), _const(ln_b)],
                 outs=[_ro(t, CV_W, BF16, bm), _rot(t, CV_W, BF16, bm)], scratch=_conv_scratch(bm))


def _conv_bwd_post(dy, p, cw, cb, ln_g, ln_b, seq, *, name, bm=CV_BM):
    t = p.shape[0]

    def fn(i, ins, outs, accs, scr):
        dy_, a, b, gate, ap, bp = [r[...] for r in ins[:6]]
        cw_ref, cb_, lg, lb = ins[6], ins[7][...], ins[8][...], ins[9][...]
        dlg, dlb, dcb, dcw = accs
        ext, sh, y = scr
        _conv_fill(i, ext, ap, bp, a, b, bm, seq)
        _conv_shift_copies(ext, sh)
        _conv_taps(ext, sh, cw_ref, y, CV_FWD_OFFS)
        xh, r = _ln_stats(y[...] + cb_)
        y2 = xh * lg + lb
        outs[1][...] = dy_ * _silu(y2) * _dsilu(gate)
        dy2 = dy_ * _silu(gate) * _dsilu(y2)
        dlg[...] += _colsum(dy2 * xh)
        dlb[...] += _colsum(dy2)
        dy1 = _ln_bwd(dy2, xh, r, lg)
        outs[0][...] = dy1
        dcb[...] += _colsum(dy1)
        _conv_dweights(outs[0], ext, sh, dcw)

    return _rows(fn, name=name, steps=t // bm,
                 ins=[_rb(dy, bm), _rb(p, bm, 0, CV_W), _rb(p, bm, 1, CV_W), _rb(p, bm, 2, CV_W),
                      _conv_halo_prev(p, 0, bm), _conv_halo_prev(p, 1, bm),
                      _const(cw), _const(cb), _const(ln_g), _const(ln_b)],
                 outs=[_ro(t, CV_W, F32, bm), _ro(t, CV_W, F32, bm)],
                 accs=[((1, CV_W), F32), ((1, CV_W), F32), ((1, CV_W), F32), ((CV_K, CV_W), F32)],
                 scratch=_conv_scratch(bm))


def _conv_bwd_pre(dy1, dgate, p, cw, seq, *, name, bm=CV_BM):
    t = p.shape[0]
    per = bm // HALO
    last_halo = t // HALO - 1

    def fn(i, ins, outs, accs, scr):
        d1, d1n, dg, a, b = [r[...] for r in ins[:5]]
        ext, sh, y = scr
        keep = jnp.where((i % (seq // bm)) == (seq // bm - 1), 0.0, 1.0)
        ext[pl.ds(0, bm), :] = d1
        ext[pl.ds(bm, HALO), :] = keep * d1n
        _conv_shift_copies(ext, sh)
        _conv_taps(ext, sh, ins[5], y, CV_BWD_OFFS)
        dy0 = y[...]
        sb = _sigmoid(b)
        outs[0][:, 0:CV_W] = dy0 * sb
        outs[0][:, CV_W:2 * CV_W] = dy0 * a * sb * (1.0 - sb)
        outs[0][:, 2 * CV_W:3 * CV_W] = dg

    return _rows(fn, name=name, steps=t // bm,
                 ins=[_rb(dy1, bm), (dy1, (HALO, CV_W), lambda i: (jnp.minimum((i + 1) * per, last_halo), 0)),
                      _rb(dgate, bm), _rb(p, bm, 0, CV_W), _rb(p, bm, 1, CV_W), _const(cw)],
                 outs=[_ro(t, 3 * CV_W, F32, bm)], scratch=_conv_scratch(bm))[0]


def _iotas():
    row = lax.broadcasted_iota(jnp.int32, (BLK, BLK), 0)
    col = lax.broadcasted_iota(jnp.int32, (BLK, BLK), 1)
    return row, col


def _heads(x, head0):
    if head0.shape != x.shape:
        head0 = lax.broadcasted_iota(jnp.int32, x.shape, 1) < HEAD_DIM
    return jnp.where(head0, x, 0.0).astype(BF16), jnp.where(head0, 0.0, x).astype(BF16)


def _pair_spec(seq, off):
    return pl.BlockSpec((seq, BLK), lambda b, hp: (b, off + hp))


def _stat_spec(seq):
    return pl.BlockSpec((None, None, seq, BLK), lambda b, hp: (b, hp, 0, 0))


_ATT_PARAMS = dict(compiler_params=pltpu.CompilerParams(dimension_semantics=("parallel", "parallel")))
_SCALE = 1.0 / math.sqrt(HEAD_DIM)


Q_BLOCK = 256
KEY_BLOCK = 256


def _stack_heads(x, head0, scale=None):
    if scale is not None:
        x = x * scale
    return jnp.concatenate(_heads(x, head0), axis=0)


def _pair_cols(x, head0, fill):
    a = jnp.max(jnp.where(head0, x, fill), axis=1, keepdims=True)
    b = jnp.max(jnp.where(head0, fill, x), axis=1, keepdims=True)
    return jnp.concatenate([a, b], axis=0)


def _causal_mask(t0, s0, tq, kw, inclusive):
    row = lax.broadcasted_iota(jnp.int32, (2 * tq, kw), 0) & (tq - 1)
    col = lax.broadcasted_iota(jnp.int32, (2 * tq, kw), 1)
    return (s0 + col) <= (t0 + row) if inclusive else (s0 + col) < (t0 + row)


def _sub(x, j):
    return x[:, j * BLK:(j + 1) * BLK]


def _block_cumsum(x, tri, ksub):
    hi = x.astype(BF16)
    lo = (x - hi.astype(F32)).astype(BF16)
    cs = _dot(jnp.concatenate([_sub(pt, j) for pt in (hi, lo) for j in range(ksub)], axis=0), tri)
    n = x.shape[0]
    return [cs[j * n:(j + 1) * n] + cs[(ksub + j) * n:(ksub + j + 1) * n] for j in range(ksub)]


def _sb_terms(qs, k, mask):
    z = _dot(qs, k, _NT)
    t = jnp.log(1.0 + jnp.exp(-jnp.abs(z)))
    lsz = jnp.minimum(z, 0.0) - t
    lr = lsz - z
    if mask is not None:
        lr = jnp.where(mask, lr, 0.0)
    return lsz, lr


def _sb_fwd(p, nb, seq, *, name):
    tq = min(Q_BLOCK, seq)
    nq = seq // tq
    kw = min(KEY_BLOCK, seq)
    ksub = kw // BLK

    def body(q_ref, k_ref, v_ref, o_ref, tot_ref):
        row, col = _iotas()
        colq = lax.broadcasted_iota(jnp.int32, (tq, BLK), 1)
        head0 = colq < HEAD_DIM
        upper = (row > col).astype(BF16)

        def qblock(qb, c):
            t0 = pl.multiple_of(qb * tq, tq)
            qs = _stack_heads(q_ref[pl.ds(t0, tq), :], head0, _SCALE)
            diag = (t0 + tq - 1) // kw

            def kblock(kb, carry, masked):
                acc, run = carry
                s0 = pl.multiple_of(kb * kw, kw)
                k = k_ref[pl.ds(s0, kw), :].astype(BF16)
                v0, v1 = _heads(v_ref[pl.ds(s0, kw), :], head0)
                mask = _causal_mask(t0, s0, tq, kw, False) if masked else None
                lsz, lr = _sb_terms(qs, k, mask)
                inblock = _block_cumsum(lr, upper, ksub)
                ws = [None] * ksub
                for j in reversed(range(ksub)):
                    w = jnp.exp(_sub(lsz, j) + inblock[j] + run)
                    if masked:
                        w = jnp.where(_sub(mask, j), w, 0.0)
                    ws[j] = w.astype(BF16)
                    run = run + jnp.sum(_sub(lr, j), axis=1, keepdims=True)
                w = jnp.concatenate(ws, axis=1)
                return acc + _dot(w[:tq], v0) + _dot(w[tq:], v1), run

            carry = kblock(diag, (jnp.zeros((tq, BLK), F32), jnp.zeros((2 * tq, 1), F32)), True)
            acc, run = lax.fori_loop(0, diag, lambda it, cr: kblock(diag - 1 - it, cr, False), carry)
            o_ref[pl.ds(t0, tq), :] = acc
            tot_ref[pl.ds(t0, tq), :] = jnp.where(head0, run[:tq], run[tq:])
            return c

        lax.fori_loop(0, nq, qblock, 0)

    return pl.pallas_call(
        body, name=name, grid=(nb, PAIRS),
        in_specs=[_pair_spec(seq, 0), _pair_spec(seq, PAIRS), _pair_spec(seq, 2 * PAIRS)],
        out_specs=[_pair_spec(seq, 0), _stat_spec(seq)],
        out_shape=[jax.ShapeDtypeStruct((nb * seq, D_MODEL), F32), jax.ShapeDtypeStruct((nb, PAIRS, seq, BLK), F32)],
        **_ATT_PARAMS,
    )(p, p, p)


def _sb_bwd(p, do, tot, nb, seq, *, name):
    tq = min(Q_BLOCK, seq)
    nq = seq // tq
    kw = min(KEY_BLOCK, seq)
    ksub = kw // BLK

    def body(q_ref, k_ref, v_ref, do_ref, tot_ref, dq_ref, dk_ref, dv_ref):
        row, col = _iotas()
        colq = lax.broadcasted_iota(jnp.int32, (tq, BLK), 1)
        head0 = colq < HEAD_DIM
        lower_incl = (row <= col).astype(BF16)
        lower_strict = (row < col).astype(BF16)
        dk_ref[...] = jnp.zeros(dk_ref.shape, F32)
        dv_ref[...] = jnp.zeros(dv_ref.shape, F32)

        def qblock(qb, c):
            t0 = pl.multiple_of(qb * tq, tq)
            qs = _stack_heads(q_ref[pl.ds(t0, tq), :], head0, _SCALE)
            dos = _stack_heads(do_ref[pl.ds(t0, tq), :], head0)
            tt = _pair_cols(tot_ref[pl.ds(t0, tq), :], head0, -jnp.inf)
            diag = (t0 + tq - 1) // kw

            def kblock(kb, carry, masked):
                dq, pf, ef = carry
                s0 = pl.multiple_of(kb * kw, kw)
                kf = k_ref[pl.ds(s0, kw), :]
                k = kf.astype(BF16)
                k0, k1 = _heads(kf, head0)
                v = v_ref[pl.ds(s0, kw), :].astype(BF16)
                mask = _causal_mask(t0, s0, tq, kw, False) if masked else None
                lsz, lr = _sb_terms(qs, k, None)
                lrm = jnp.where(mask, lr, 0.0) if masked else lr
                incl = _block_cumsum(lrm, lower_incl, ksub)
                dw = _dot(dos, v, _NT)
                ws, ews = [], []
                for j in range(ksub):
                    w = jnp.exp(_sub(lsz, j) + (tt - pf - incl[j]))
                    if masked:
                        w = jnp.where(_sub(mask, j), w, 0.0)
                    pf = pf + jnp.sum(_sub(lrm, j), axis=1, keepdims=True)
                    ws.append(w)
                    ews.append(_sub(dw, j) * w)
                before = _block_cumsum(jnp.concatenate(ews, axis=1), lower_strict, ksub)
                dzs = []
                for j in range(ksub):
                    dlr = ef + before[j]
                    ef = ef + jnp.sum(ews[j], axis=1, keepdims=True)
                    dz = ews[j] * jnp.exp(_sub(lr, j)) - dlr * jnp.exp(_sub(lsz, j))
                    if masked:
                        dz = jnp.where(_sub(mask, j), dz, 0.0)
                    dzs.append(dz.astype(BF16))
                dz = jnp.concatenate(dzs, axis=1)
                w = jnp.concatenate([w.astype(BF16) for w in ws], axis=1)
                dk_ref[pl.ds(s0, kw), :] += _dot(dz, qs, _TN)
                dv_ref[pl.ds(s0, kw), :] += _dot(w, dos, _TN)
                return dq + _dot(dz[:tq], k0) + _dot(dz[tq:], k1), pf, ef

            zc = jnp.zeros((2 * tq, 1), F32)
            carry = lax.fori_loop(0, diag, lambda kb, cr: kblock(kb, cr, False), (jnp.zeros((tq, BLK), F32), zc, zc))
            dq_ref[pl.ds(t0, tq), :] = kblock(diag, carry, True)[0] * _SCALE
            return c

        lax.fori_loop(0, nq, qblock, 0)

    t = nb * seq
    return pl.pallas_call(
        body, name=name, grid=(nb, PAIRS),
        in_specs=[_pair_spec(seq, 0), _pair_spec(seq, PAIRS), _pair_spec(seq, 2 * PAIRS), _pair_spec(seq, 0), _stat_spec(seq)],
        out_specs=[_pair_spec(seq, 0)] * 3,
        out_shape=[jax.ShapeDtypeStruct((t, D_MODEL), F32)] * 3,
        **_ATT_PARAMS,
    )(p, p, p, do, tot)


def _fox_cum(f, bf, nb, seq, *, name):
    def body(f_ref, bf_ref, cc_ref, cr_ref):
        row, col = _iotas()
        lower = (col <= row).astype(BF16)
        carry = jnp.zeros((1, BLK), F32)
        for blk in range(seq // BLK):
            rs = slice(blk * BLK, (blk + 1) * BLK)
            lf = jnp.where(col < HEADS, _log_sigmoid(f_ref[rs, :] + bf_ref[...]), 0.0)
            cc = _dot3_left(lower, lf) + carry
            cc_ref[rs, :] = cc
            cr_ref[:, rs] = cc.T[0:HEADS, :]
            carry = carry + _colsum(lf)

    return pl.pallas_call(
        body, name=name, grid=(nb,),
        in_specs=[pl.BlockSpec((seq, BLK), lambda b: (b, 0)), pl.BlockSpec((1, BLK), lambda b: (0, 0))],
        out_specs=[pl.BlockSpec((seq, BLK), lambda b: (b, 0)), pl.BlockSpec((None, HEADS, seq), lambda b: (b, 0, 0))],
        out_shape=[jax.ShapeDtypeStruct((nb * seq, BLK), F32), jax.ShapeDtypeStruct((nb, HEADS, seq), F32)],
        compiler_params=pltpu.CompilerParams(dimension_semantics=("parallel",)),
    )(f, bf)


def _fox_cum_bwd(dcr, dcc, f, bf, nb, seq, *, name):
    def body(dcr_ref, dcc_ref, f_ref, bf_ref, df_ref, dbf_ref):
        row, col = _iotas()
        upper_incl = (col >= row).astype(BF16)

        @pl.when(pl.program_id(0) == 0)
        def _():
            dbf_ref[...] = jnp.zeros((1, BLK), F32)

        carry = jnp.zeros((1, BLK), F32)
        for blk in reversed(range(seq // BLK)):
            rs = slice(blk * BLK, (blk + 1) * BLK)
            dc = dcr_ref[:, rs].T + dcc_ref[rs, :]
            dlf = _dot3_left(upper_incl, dc) + carry
            carry = carry + _colsum(dc)
            fl = f_ref[rs, :] + bf_ref[...]
            df = jnp.where(col < HEADS, dlf * _sigmoid(-fl), 0.0)
            df_ref[rs, :] = df
            dbf_ref[...] += _colsum(df)

    return pl.pallas_call(
        body, name=name, grid=(nb,),
        in_specs=[pl.BlockSpec((None, BLK, seq), lambda b: (b, 0, 0)), pl.BlockSpec((seq, BLK), lambda b: (b, 0)),
                  pl.BlockSpec((seq, BLK), lambda b: (b, 0)), pl.BlockSpec((1, BLK), lambda b: (0, 0))],
        out_specs=[pl.BlockSpec((seq, BLK), lambda b: (b, 0)), pl.BlockSpec((1, BLK), lambda b: (0, 0))],
        out_shape=[jax.ShapeDtypeStruct((nb * seq, BLK), F32), jax.ShapeDtypeStruct((1, BLK), F32)],
        compiler_params=pltpu.CompilerParams(dimension_semantics=("arbitrary",)),
    )(dcr, dcc, f, bf)


def _fox_cum_cols(cc_ref, t0, tq, colq, hp):
    cc = cc_ref[pl.ds(t0, tq), :]
    c0 = jnp.sum(jnp.where(colq == 2 * hp, cc, 0.0), axis=1, keepdims=True)
    c1 = jnp.sum(jnp.where(colq == 2 * hp + 1, cc, 0.0), axis=1, keepdims=True)
    return c0, c1


def _fox_bias(c0, c1, cr_ref, s0, kw):
    return jnp.concatenate([c0 - cr_ref[0:1, pl.ds(s0, kw)], c1 - cr_ref[1:2, pl.ds(s0, kw)]], axis=0)


def _fox_fwd(p, cc, cr, nb, seq, *, name):
    tq = min(Q_BLOCK, seq)
    nq = seq // tq
    kw = min(KEY_BLOCK, seq)
    ksub = kw // BLK

    def body(q_ref, k_ref, v_ref, cc_ref, cr_ref, o_ref, lse_ref):
        hp = pl.program_id(1)
        row, col = _iotas()
        colq = lax.broadcasted_iota(jnp.int32, (tq, BLK), 1)
        head0 = colq < HEAD_DIM

        def qblock(qb, c):
            t0 = pl.multiple_of(qb * tq, tq)
            qs = _stack_heads(q_ref[pl.ds(t0, tq), :], head0, _SCALE)
            c0, c1 = _fox_cum_cols(cc_ref, t0, tq, colq, hp)
            diag = (t0 + tq - 1) // kw

            def kblock(kb, carry, masked):
                acc, m, l = carry
                s0 = pl.multiple_of(kb * kw, kw)
                k = k_ref[pl.ds(s0, kw), :].astype(BF16)
                v0, v1 = _heads(v_ref[pl.ds(s0, kw), :], head0)
                s = _dot(qs, k, _NT) + _fox_bias(c0, c1, cr_ref, s0, kw)
                if masked:
                    s = jnp.where(_causal_mask(t0, s0, tq, kw, True), s, -jnp.inf)
                m_new = jnp.maximum(m, jnp.max(s, axis=1, keepdims=True))
                pr = jnp.exp(s - m_new)
                alpha = jnp.exp(m - m_new)
                l = alpha * l + jnp.sum(pr, axis=1, keepdims=True)
                pr = pr.astype(BF16)
                acc = acc * jnp.where(head0, alpha[:tq], alpha[tq:]) + _dot(pr[:tq], v0) + _dot(pr[tq:], v1)
                return acc, m_new, l

            init = (jnp.zeros((tq, BLK), F32), jnp.full((2 * tq, 1), -jnp.inf, F32), jnp.zeros((2 * tq, 1), F32))
            carry = lax.fori_loop(0, diag, lambda kb, cr: kblock(kb, cr, False), init)
            acc, m, l = kblock(diag, carry, True)
            o_ref[pl.ds(t0, tq), :] = acc / jnp.where(head0, l[:tq], l[tq:])
            lse = m + jnp.log(l)
            lse_ref[pl.ds(t0, tq), :] = jnp.where(head0, lse[:tq], lse[tq:])
            return c

        lax.fori_loop(0, nq, qblock, 0)

    return pl.pallas_call(
        body, name=name, grid=(nb, PAIRS),
        in_specs=[_pair_spec(seq, 0), _pair_spec(seq, PAIRS), _pair_spec(seq, 2 * PAIRS),
                  pl.BlockSpec((seq, BLK), lambda b, hp: (b, 0)), pl.BlockSpec((None, None, 8, seq), lambda b, hp: (b, hp, 0, 0))],
        out_specs=[_pair_spec(seq, 0), _stat_spec(seq)],
        out_shape=[jax.ShapeDtypeStruct((nb * seq, D_MODEL), F32), jax.ShapeDtypeStruct((nb, PAIRS, seq, BLK), F32)],
        **_ATT_PARAMS,
    )(p, p, p, cc, cr)


def _fox_bwd(p, do, o, lse, cc, cr, nb, seq, *, name):
    tq = min(Q_BLOCK, seq)
    nq = seq // tq
    kw = min(KEY_BLOCK, seq)
    ksub = kw // BLK

    def body(q_ref, k_ref, v_ref, do_ref, o_ref, lse_ref, cc_ref, cr_ref, dq_ref, dk_ref, dv_ref, dcr_ref, dcc_ref):
        hp = pl.program_id(1)
        row, col = _iotas()
        colq = lax.broadcasted_iota(jnp.int32, (tq, BLK), 1)
        head0 = colq < HEAD_DIM
        dk_ref[...] = jnp.zeros(dk_ref.shape, F32)
        dv_ref[...] = jnp.zeros(dv_ref.shape, F32)
        dcr_ref[...] = jnp.zeros(dcr_ref.shape, F32)

        @pl.when(hp == 0)
        def _():
            dcc_ref[...] = jnp.zeros(dcc_ref.shape, F32)

        def qblock(qb, c):
            t0 = pl.multiple_of(qb * tq, tq)
            qs = _stack_heads(q_ref[pl.ds(t0, tq), :], head0, _SCALE)
            dof = do_ref[pl.ds(t0, tq), :]
            dos = _stack_heads(dof, head0)
            prod = dof * o_ref[pl.ds(t0, tq), :]
            dl = jnp.concatenate([jnp.sum(jnp.where(head0, prod, 0.0), axis=1, keepdims=True),
                                  jnp.sum(jnp.where(head0, 0.0, prod), axis=1, keepdims=True)], axis=0)
            lse = _pair_cols(lse_ref[pl.ds(t0, tq), :], head0, -jnp.inf)
            c0, c1 = _fox_cum_cols(cc_ref, t0, tq, colq, hp)
            diag = (t0 + tq - 1) // kw

            def kblock(kb, carry, masked):
                dq, rs = carry
                s0 = pl.multiple_of(kb * kw, kw)
                kf = k_ref[pl.ds(s0, kw), :]
                k = kf.astype(BF16)
                k0, k1 = _heads(kf, head0)
                v = v_ref[pl.ds(s0, kw), :].astype(BF16)
                pr = jnp.exp(_dot(qs, k, _NT) + _fox_bias(c0, c1, cr_ref, s0, kw) - lse)
                if masked:
                    pr = jnp.where(_causal_mask(t0, s0, tq, kw, True), pr, 0.0)
                ds = pr * (_dot(dos, v, _NT) - dl)
                dcr_ref[0:1, pl.ds(s0, kw)] -= _colsum(ds[:tq])
                dcr_ref[1:2, pl.ds(s0, kw)] -= _colsum(ds[tq:])
                rs = rs + jnp.sum(ds, axis=1, keepdims=True)
                ds = ds.astype(BF16)
                dk_ref[pl.ds(s0, kw), :] += _dot(ds, qs, _TN)
                dv_ref[pl.ds(s0, kw), :] += _dot(pr.astype(BF16), dos, _TN)
                return dq + _dot(ds[:tq], k0) + _dot(ds[tq:], k1), rs

            init = (jnp.zeros((tq, BLK), F32), jnp.zeros((2 * tq, 1), F32))
            carry = lax.fori_loop(0, diag, lambda kb, cr: kblock(kb, cr, False), init)
            dq, rs = kblock(diag, carry, True)
            dq_ref[pl.ds(t0, tq), :] = dq * _SCALE
            dcc_ref[pl.ds(t0, tq), :] += jnp.where(colq == 2 * hp, rs[:tq], 0.0) + jnp.where(colq == 2 * hp + 1, rs[tq:], 0.0)
            return c

        lax.fori_loop(0, nq, qblock, 0)

    t = nb * seq
    return pl.pallas_call(
        body, name=name, grid=(nb, PAIRS),
        in_specs=[_pair_spec(seq, 0), _pair_spec(seq, PAIRS), _pair_spec(seq, 2 * PAIRS), _pair_spec(seq, 0), _pair_spec(seq, 0),
                  _stat_spec(seq), pl.BlockSpec((seq, BLK), lambda b, hp: (b, 0)),
                  pl.BlockSpec((None, None, 8, seq), lambda b, hp: (b, hp, 0, 0))],
        out_specs=[_pair_spec(seq, 0)] * 3 + [pl.BlockSpec((None, None, 8, seq), lambda b, hp: (b, hp, 0, 0)),
                                              pl.BlockSpec((seq, BLK), lambda b, hp: (b, 0))],
        out_shape=[jax.ShapeDtypeStruct((t, D_MODEL), F32)] * 3 + [jax.ShapeDtypeStruct((nb, PAIRS, 8, seq), F32),
                                                                     jax.ShapeDtypeStruct((t, BLK), F32)],
        compiler_params=pltpu.CompilerParams(dimension_semantics=("parallel", "arbitrary")),
    )(p, p, p, do, o, lse, cc, cr)


def _row_shards(x):
    return x.reshape(N_CHIPS, x.shape[0] // N_CHIPS, x.shape[1])


def _local_step(x3, tgt3, w):
    nb, seq, d = x3.shape
    t = nb * seq
    x0, tgt = x3.reshape(t, d), tgt3.reshape(t, d)
    g = {}

    a_gain = w["a_norm"].reshape(1, d)
    h_a, ht_a = _rmsnorm(x0, a_gain, name="a_norm_fwd")
    p_a = _matmul(h_a, w["a_w_in"], name="a_in_fwd")
    o_a, tot_a = _sb_fwd(p_a, nb, seq, name="a_attn_fwd")
    y_a, yt_a = _gate(o_a, p_a, 3, name="a_gate_fwd")
    x1 = _matmul(y_a, w["a_w_out"], name="a_out_fwd", residual=x0)

    b_gain = w["b_norm"].reshape(1, d)
    b_lg, b_lb = w["b_v_ln_g"].reshape(1, GM_W), w["b_v_ln_b"].reshape(1, GM_W)
    b_ws, b_bst = w["b_w_s"].reshape(GM_G, BLK, BLK), w["b_b_s"].reshape(GM_G, BLK).T
    h_b, ht_b = _rmsnorm(x1, b_gain, name="b_norm_fwd")
    p_b = _matmul(h_b, w["b_w_in"], name="b_in_fwd")
    y_b, yt_b = _gmlp_fwd(p_b, b_lg, b_lb, b_ws, b_bst, name="b_mix_fwd")
    x2 = _matmul(y_b, w["b_w_out"], name="b_out_fwd", residual=x1)

    c_gain = w["c_norm"].reshape(1, d)
    c_cw = jnp.repeat(w["c_conv_w"].reshape(CV_K, CV_W), SUBLANES, axis=0)
    c_cb = w["c_conv_b"].reshape(1, CV_W)
    c_lg, c_lb = w["c_ln_g"].reshape(1, CV_W), w["c_ln_b"].reshape(1, CV_W)
    h_c, ht_c = _rmsnorm(x2, c_gain, name="c_norm_fwd")
    p_c = _matmul(h_c, w["c_w_in"], name="c_in_fwd")
    y_c, yt_c = _conv_fwd(p_c, c_cw, c_cb, c_lg, c_lb, seq, name="c_conv_fwd")
    x3_ = _matmul(y_c, w["c_w_out"], name="c_out_fwd", residual=x2)

    d_gain = w["d_norm"].reshape(1, d)
    d_win = w["d_w_in"].reshape(d, 4 * D_MODEL + HEADS)
    d_wmain = d_win[:, :4 * D_MODEL]
    d_wf = jnp.pad(d_win[:, 4 * D_MODEL:], ((0, 0), (0, BLK - HEADS)))
    d_bf = jnp.pad(w["d_b_f"].reshape(1, HEADS), ((0, 0), (0, BLK - HEADS)))
    h_d, ht_d = _rmsnorm(x3_, d_gain, name="d_norm_fwd")
    p_d = _matmul(h_d, d_wmain, name="d_in_fwd")
    f_d = _matmul(h_d, d_wf, name="d_inf_fwd")
    cc, cr = _fox_cum(f_d, d_bf, nb, seq, name="d_cum_fwd")
    cr = jnp.pad(cr.reshape(nb, PAIRS, 2, seq), ((0, 0), (0, 0), (0, 6), (0, 0)))
    o_d, lse_d = _fox_fwd(p_d, cc, cr, nb, seq, name="d_attn_fwd")
    y_d, yt_d = _gate(o_d, p_d, 3, name="d_gate_fwd")
    x4 = _matmul(y_d, w["d_w_out"], name="d_out_fwd", residual=x3_)

    f_gain = w["final_norm"].reshape(1, d)
    dx, g_fn, loss_row = _loss_head(x4, f_gain, tgt, name="loss_head")
    g["final_norm"] = g_fn

    g["d_w_out"] = _row_shards(_matmul(yt_d, dx, name="d_out_dw"))
    dy = _matmul(dx, w["d_w_out"], name="d_out_dy", mode="nt")
    do_d, dg_d = _gate_bwd(dy, o_d, p_d, 3, name="d_gate_bwd")
    dq, dk, dv, dcr, dcc = _fox_bwd(p_d, do_d, o_d, lse_d, cc, cr, nb, seq, name="d_attn_bwd")
    dcr = jnp.pad(dcr[:, :, :2, :].reshape(nb, HEADS, seq), ((0, 0), (0, BLK - HEADS), (0, 0)))
    df, dbf = _fox_cum_bwd(dcr, dcc, f_d, d_bf, nb, seq, name="d_cum_bwd")
    g["d_b_f"] = dbf[:, :HEADS]
    parts = [dq, dk, dv, dg_d]
    dws = [_matmul(ht_d, pt, name=f"d_in_dw{n}") for n, pt in enumerate(parts)]
    dwf = _matmul(ht_d, df, name="d_inf_dw")
    g["d_w_in"] = jnp.concatenate(dws + [dwf[:, :HEADS]], axis=1).reshape(d, N_CHIPS, -1).transpose(1, 0, 2)
    dh = _matmul(df, d_wf, name="d_inf_dh", mode="nt")
    for n, pt in enumerate(parts):
        dh = _matmul(pt, d_wmain[:, n * D_MODEL:(n + 1) * D_MODEL], name=f"d_in_dh{n}", mode="nt", residual=dh)
    dx, g["d_norm"] = _rmsnorm_bwd(dh, x3_, d_gain, dx, name="d_norm_bwd")

    g["c_w_out"] = _row_shards(_matmul(yt_c, dx, name="c_out_dw"))
    dy = _matmul(dx, w["c_w_out"], name="c_out_dy", mode="nt")
    dy1, dgate, g["c_ln_g"], g["c_ln_b"], g["c_conv_b"], g["c_conv_w"] = _conv_bwd_post(
        dy, p_c, c_cw, c_cb, c_lg, c_lb, seq, name="c_conv_bwd_post")
    dp = _conv_bwd_pre(dy1, dgate, p_c, c_cw, seq, name="c_conv_bwd_pre")
    g["c_w_in"] = _matmul(ht_c, dp, name="c_in_dw", out_shards=N_CHIPS)
    dh = _matmul(dp, w["c_w_in"], name="c_in_dh", mode="nt")
    dx, g["c_norm"] = _rmsnorm_bwd(dh, x2, c_gain, dx, name="c_norm_bwd")

    g["b_w_out"] = _row_shards(_matmul(yt_b, dx, name="b_out_dw"))
    dy = _matmul(dx, w["b_w_out"], name="b_out_dy", mode="nt")
    dp, g["b_v_ln_g"], g["b_v_ln_b"], g["b_w_s"], dbst = _gmlp_bwd(dy, p_b, b_lg, b_lb, b_ws, b_bst, name="b_mix_bwd")
    g["b_b_s"] = dbst.T
    g["b_w_in"] = _matmul(ht_b, dp, name="b_in_dw", out_shards=N_CHIPS)
    dh = _matmul(dp, w["b_w_in"], name="b_in_dh", mode="nt")
    dx, g["b_norm"] = _rmsnorm_bwd(dh, x1, b_gain, dx, name="b_norm_bwd")

    g["a_w_out"] = _row_shards(_matmul(yt_a, dx, name="a_out_dw"))
    dy = _matmul(dx, w["a_w_out"], name="a_out_dy", mode="nt")
    do_a, dg_a = _gate_bwd(dy, o_a, p_a, 3, name="a_gate_bwd")
    dq, dk, dv = _sb_bwd(p_a, do_a, tot_a, nb, seq, name="a_attn_bwd")
    parts = [dq, dk, dv, dg_a]
    g["a_w_in"] = jnp.stack([_matmul(ht_a, pt, name=f"a_in_dw{n}") for n, pt in enumerate(parts)])
    dh = None
    for n, pt in enumerate(parts):
        dh = _matmul(pt, w["a_w_in"][n], name=f"a_in_dh{n}", mode="nt", residual=dh)
    dx, g["a_norm"] = _rmsnorm_bwd(dh, x0, a_gain, dx, name="a_norm_bwd")

    return loss_row[0, 0], dx.reshape(nb, seq, d), g


_HBM = pl.BlockSpec(memory_space=pltpu.HBM)


def _place():
    return lax.axis_index("x"), lax.axis_index("y"), lax.axis_index("c")


def _other_chips(x, y):
    return [(1 - x, y), (x, 1 - y), (1 - x, 1 - y)]


def _allgather_chips(ss, *, name):
    n_ops = len(ss)

    def body(*refs):
        s_refs, o_refs, (send_sems, recv_sems) = refs[:n_ops], refs[n_ops:2 * n_ops], refs[2 * n_ops:]
        x, y, c = _place()
        me = 2 * x + y
        chips = _other_chips(x, y)

        def copy(i, kk, src, dst, to):
            return pltpu.make_async_remote_copy(src_ref=src, dst_ref=dst, send_sem=send_sems.at[6 * i + kk],
                                                recv_sem=recv_sems.at[6 * i + kk], device_id=to, device_id_type=MESH)

        def half(i, j, hc):
            h = s_refs[i].shape[0] // 2
            return o_refs[i].at[j, pl.ds(hc * h, h), :]

        first = [copy(i, kk, s_refs[i].at[pl.ds(c * (s_refs[i].shape[0] // 2), s_refs[i].shape[0] // 2), :], half(i, me, c),
                      (cx, cy, c)) for kk, (cx, cy) in enumerate(chips) for i in range(n_ops)]
        for cp in first:
            cp.start()
        passed = []
        for kk, (cx, cy) in enumerate(chips):
            for i in range(n_ops):
                blk = half(i, 2 * cx + cy, c)
                copy(i, kk, blk, blk, (cx, cy, c)).wait_recv()
                fwd = copy(i, 3 + kk, blk, blk, (x, y, 1 - c))
                fwd.start()
                passed.append(fwd)
        for kk, (cx, cy) in enumerate(chips):
            for i in range(n_ops):
                blk = half(i, 2 * cx + cy, 1 - c)
                copy(i, 3 + kk, blk, blk, (x, y, 1 - c)).wait_recv()
        for cp in first + passed:
            cp.wait_send()

    for s in ss:
        assert s.shape[0] % 32 == 0, s.shape
    return pl.pallas_call(
        body, name=name, in_specs=[_HBM] * n_ops, out_specs=[_HBM] * n_ops,
        out_shape=[jax.ShapeDtypeStruct((N_CHIPS,) + s.shape, s.dtype) for s in ss],
        scratch_shapes=[pltpu.SemaphoreType.DMA((6 * n_ops,)), pltpu.SemaphoreType.DMA((6 * n_ops,))],
    )(*ss)


def _own_block(gathered, s):
    me = 2 * lax.axis_index("x") + lax.axis_index("y")
    return lax.dynamic_update_slice(gathered, s[None], (me,) + (0,) * s.ndim)


def _dma_sems(n):
    return [pltpu.SemaphoreType.DMA((n,)), pltpu.SemaphoreType.DMA((n,))]


def _swap_halves(gps, *, name):
    n_ops = len(gps)

    def body(*refs):
        g_refs, o_refs, (send_sems, recv_sems) = refs[:n_ops], refs[n_ops:2 * n_ops], refs[2 * n_ops:]
        x, y, c = _place()
        cps = []
        for i, (g_ref, o_ref) in enumerate(zip(g_refs, o_refs)):
            h = g_ref.shape[1] // 2
            cps.append(pltpu.make_async_remote_copy(
                src_ref=g_ref.at[:, pl.ds((1 - c) * h, h), :], dst_ref=o_ref, send_sem=send_sems.at[i], recv_sem=recv_sems.at[i],
                device_id=(x, y, 1 - c), device_id_type=MESH))
        for cp in cps:
            cp.start()
        for cp in cps:
            cp.wait()

    return pl.pallas_call(
        body, name=name, in_specs=[_HBM] * n_ops, out_specs=[_HBM] * n_ops,
        out_shape=[jax.ShapeDtypeStruct((g.shape[0], g.shape[1] // 2, g.shape[2]), g.dtype) for g in gps],
        scratch_shapes=_dma_sems(n_ops),
    )(*gps)


def _scatter_chips(hps, *, name):
    n_ops = len(hps)

    def body(*refs):
        h_refs, o_refs, (send_sems, recv_sems) = refs[:n_ops], refs[n_ops:2 * n_ops], refs[2 * n_ops:]
        x, y, c = _place()
        cps = [pltpu.make_async_remote_copy(src_ref=h_ref.at[2 * cx + cy], dst_ref=o_ref.at[kk], send_sem=send_sems.at[3 * i + kk],
                                            recv_sem=recv_sems.at[3 * i + kk], device_id=(cx, cy, c), device_id_type=MESH)
               for i, (h_ref, o_ref) in enumerate(zip(h_refs, o_refs)) for kk, (cx, cy) in enumerate(_other_chips(x, y))]
        for cp in cps:
            cp.start()
        for cp in cps:
            cp.wait()

    return pl.pallas_call(
        body, name=name, in_specs=[_HBM] * n_ops, out_specs=[_HBM] * n_ops,
        out_shape=[jax.ShapeDtypeStruct((3,) + hp.shape[1:], hp.dtype) for hp in hps],
        scratch_shapes=_dma_sems(3 * n_ops),
    )(*hps)


def _join_halves(fs, *, name):
    n_ops = len(fs)

    def body(*refs):
        f_refs, o_refs, (send_sems, recv_sems) = refs[:n_ops], refs[n_ops:2 * n_ops], refs[2 * n_ops:]
        x, y, c = _place()
        cps = [pltpu.make_async_remote_copy(src_ref=f_ref, dst_ref=o_ref, send_sem=send_sems.at[i], recv_sem=recv_sems.at[i],
                                            device_id=(x, y, 1 - c), device_id_type=MESH)
               for i, (f_ref, o_ref) in enumerate(zip(f_refs, o_refs))]
        for cp in cps:
            cp.start()
        for cp in cps:
            cp.wait()

    theirs = pl.pallas_call(
        body, name=name, in_specs=[_HBM] * n_ops, out_specs=[_HBM] * n_ops,
        out_shape=[jax.ShapeDtypeStruct(f.shape, f.dtype) for f in fs], scratch_shapes=_dma_sems(n_ops),
    )(*fs)
    south = lax.axis_index("c") == 0
    return [jnp.concatenate([jnp.where(south, f, t), jnp.where(south, t, f)], axis=0) for f, t in zip(fs, theirs)]


def _add_halves(gp, ra, wire_dtype, *, name, bm=256):
    n, r, c_ = gp.shape
    h = r // 2
    bm = _tile(h, bm)
    per = h // bm
    c = lax.axis_index("c").astype(jnp.int32).reshape(1)

    def body(c_ref, g_ref, ra_ref, o_ref, ow_ref):
        s = g_ref[...] + ra_ref[...]
        o_ref[...] = s
        ow_ref[...] = s.astype(wire_dtype)

    mine = pl.BlockSpec((None, bm, c_), lambda j, i, cr: (j, i, 0))
    return pl.pallas_call(
        body, name=name,
        grid_spec=pltpu.PrefetchScalarGridSpec(
            num_scalar_prefetch=1, grid=(n, per),
            in_specs=[pl.BlockSpec((None, bm, c_), lambda j, i, cr: (j, cr[0] * per + i, 0)), mine],
            out_specs=[mine, mine]),
        out_shape=[jax.ShapeDtypeStruct((n, h, c_), F32), jax.ShapeDtypeStruct((n, h, c_), wire_dtype)],
        compiler_params=pltpu.CompilerParams(dimension_semantics=("parallel", "parallel")),
    )(c, gp, ra)


def _add_chips(hp, rb, *, name, bm=256):
    n, h, c_ = hp.shape
    bm = _tile(h, bm)
    me = (2 * lax.axis_index("x") + lax.axis_index("y")).astype(jnp.int32).reshape(1)

    def body(me_ref, h_ref, rb_ref, o_ref):
        o_ref[...] = ((h_ref[...] + rb_ref[0].astype(F32)) + rb_ref[1].astype(F32)) + rb_ref[2].astype(F32)

    return pl.pallas_call(
        body, name=name,
        grid_spec=pltpu.PrefetchScalarGridSpec(
            num_scalar_prefetch=1, grid=(h // bm,),
            in_specs=[pl.BlockSpec((None, bm, c_), lambda i, mr: (mr[0], i, 0)),
                      pl.BlockSpec((3, bm, c_), lambda i, mr: (0, i, 0))],
            out_specs=pl.BlockSpec((bm, c_), lambda i, mr: (i, 0))),
        out_shape=jax.ShapeDtypeStruct((h, c_), F32),
        compiler_params=pltpu.CompilerParams(dimension_semantics=("parallel",)),
    )(me, hp, rb)


def _reduce_scatter(gps, wire_dtypes, *, tag):
    ras = _swap_halves(gps, name=f"{tag}_swap_halves")
    hps = [_add_halves(gp, ra, wd, name=f"{tag}_add_halves{i}") for i, (gp, ra, wd) in enumerate(zip(gps, ras, wire_dtypes))]
    rbs = _scatter_chips([hw for _, hw in hps], name=f"{tag}_scatter_chips")
    fs = [_add_chips(hf, rb, name=f"{tag}_add_chips{i}") for i, ((hf, _), rb) in enumerate(zip(hps, rbs))]
    return _join_halves(fs, name=f"{tag}_join_halves")


def _adamw(w, g, m, v, *, name):
    r, c_ = w.shape
    bm = r
    for cand in (512, 256, 128, 64, 32, 16, 8):
        if r % cand == 0:
            bm = cand
            break
    c1 = 1.0 - ADAM_B1 ** ADAM_STEP
    c2 = 1.0 - ADAM_B2 ** ADAM_STEP

    def body(w_ref, g_ref, m_ref, v_ref, d_ref, nm_ref, nv_ref):
        g_ = g_ref[...]
        m_ = ADAM_B1 * m_ref[...] + (1.0 - ADAM_B1) * g_
        v_ = ADAM_B2 * v_ref[...] + (1.0 - ADAM_B2) * (g_ * g_)
        d_ref[...] = -ADAM_LR * ((m_ / c1) / (jnp.sqrt(v_ / c2) + ADAM_EPS) + ADAM_WD * w_ref[...])
        nm_ref[...] = m_
        nv_ref[...] = v_

    spec = pl.BlockSpec((bm, c_), lambda i: (i, 0))
    return pl.pallas_call(
        body, name=name, grid=(r // bm,), in_specs=[spec] * 4, out_specs=[spec] * 3,
        out_shape=[jax.ShapeDtypeStruct((r, c_), F32)] * 3,
        compiler_params=pltpu.CompilerParams(dimension_semantics=("parallel",)),
    )(w, g, m, v)


_WEIGHTS = ["a_norm", "a_w_in", "a_w_out", "b_norm", "b_w_in", "b_v_ln_g", "b_v_ln_b", "b_w_s", "b_b_s", "b_w_out",
            "c_norm", "c_w_in", "c_conv_w", "c_conv_b", "c_ln_g", "c_ln_b", "c_w_out", "d_norm", "d_w_in", "d_b_f",
            "d_w_out", "final_norm"]
_SHARD_AXIS = {"a_norm": None, "a_w_in": 2, "a_w_out": 1, "b_norm": 1, "b_w_in": 2, "b_v_ln_g": 1, "b_v_ln_b": 1, "b_w_s": None,
               "b_b_s": None, "b_w_out": 1, "c_norm": 1, "c_w_in": 2, "c_conv_w": 2, "c_conv_b": 1, "c_ln_g": 1, "c_ln_b": 1,
               "c_w_out": 1, "d_norm": 1, "d_w_in": 2, "d_b_f": None, "d_w_out": 1, "final_norm": None}
_BIG = ["a_w_in", "a_w_out", "b_w_in", "b_w_out", "c_w_in", "c_w_out", "d_w_in", "d_w_out"]
_SMALL_SHARDED = [n for n in _WEIGHTS if _SHARD_AXIS[n] is not None and n not in _BIG]
_REPLICATED = [n for n in _WEIGHTS if _SHARD_AXIS[n] is None]
_ROW_ALIGN = 32
_ROW_ALIGN_SUMMED = 128


def _pack(pieces, dtype, align=_ROW_ALIGN):
    flat = jnp.concatenate([p.reshape(-1).astype(dtype) for p in pieces])
    unit = align * PACK_C
    total = -(-flat.shape[0] // unit) * unit
    return jnp.pad(flat, (0, total - flat.shape[0])).reshape(total // PACK_C, PACK_C)


def _unpack(flat, shapes):
    out, off = [], 0
    for s in shapes:
        n = math.prod(s)
        out.append(flat[off:off + n].reshape(s))
        off += n
    return out


def _full_shape(local_shape, axis):
    s = list(local_shape)
    if axis is not None:
        s[axis] *= N_CHIPS
    return tuple(s)


def _gather_weights(local):
    full = {n: local[n][0] if n != "final_norm" else local[n] for n in _REPLICATED}
    mine = [local[n][0].astype(BF16) for n in _BIG] + [_pack([local[n] for n in _SMALL_SHARDED], F32)]
    got = [_own_block(gt, s) for gt, s in zip(_allgather_chips(mine, name="gather_weights"), mine)]
    for n, gt in zip(_BIG, got):
        if _SHARD_AXIS[n] == 1:
            full[n] = gt.reshape(-1, gt.shape[-1])
        elif n == "d_w_in":
            full[n] = gt.transpose(1, 0, 2).reshape(gt.shape[1], -1)
        else:
            full[n] = gt
    small = got[-1].reshape(N_CHIPS, -1)
    shards = [_unpack(small[j], [local[n].shape[1:] for n in _SMALL_SHARDED]) for j in range(N_CHIPS)]
    for i, n in enumerate(_SMALL_SHARDED):
        full[n] = jnp.concatenate([shards[j][i] for j in range(N_CHIPS)], axis=_SHARD_AXIS[n] - 1)
    return full


def _repl_piece_len(local):
    total = sum(math.prod(local[n].shape) for n in _REPLICATED)
    return -(-total // N_CHIPS)


def _reduce_grads(g, local):
    rep_flat = jnp.concatenate([g[n].reshape(-1) for n in _REPLICATED])
    piece = _repl_piece_len(local)
    rep_flat = jnp.pad(rep_flat, (0, N_CHIPS * piece - rep_flat.shape[0]))

    def shard(n, j):
        full = g[n].reshape(_full_shape(local[n].shape, _SHARD_AXIS[n]))
        width = local[n].shape[_SHARD_AXIS[n]]
        return lax.slice_in_dim(full, j * width, (j + 1) * width, axis=_SHARD_AXIS[n])

    small = jnp.stack([_pack([shard(n, j) for n in _SMALL_SHARDED] + [rep_flat[j * piece:(j + 1) * piece]], F32)
                       for j in range(N_CHIPS)])
    summed = _reduce_scatter([g[n] for n in _BIG] + [small], [BF16] * len(_BIG) + [F32], tag="grads")
    red = {n: s.reshape(local[n].shape) for n, s in zip(_BIG, summed)}
    out = _unpack(summed[-1].reshape(-1), [local[n].shape for n in _SMALL_SHARDED] + [(piece,)])
    red.update(zip(_SMALL_SHARDED, out[:-1]))
    rep_mine = _pack([out[-1]], F32)
    rep = _own_block(_allgather_chips([rep_mine], name="gather_replicated_grads")[0], rep_mine)
    rep = rep.reshape(N_CHIPS, -1)[:, :piece].reshape(-1)
    for n, val in zip(_REPLICATED, _unpack(rep, [local[n].shape for n in _REPLICATED])):
        red[n] = val
    return red


def _update(local, grads, m, v):
    delta, new_m, new_v = {}, {}, {}
    for n in _BIG:
        shp = local[n].shape
        two = (shp[-2], shp[-1])
        res = _adamw(local[n].reshape(two), grads[n].reshape(two), m[n].reshape(two), v[n].reshape(two), name=f"adamw_{n}")
        delta[n], new_m[n], new_v[n] = [r.reshape(shp) for r in res]
    small = [n for n in _WEIGHTS if n not in _BIG]
    shapes = [local[n].shape for n in small]
    packed = [_pack([src[n] for n in small], F32) for src in (local, grads, m, v)]
    res = _adamw(*packed, name="adamw_small")
    for dst, r in zip((delta, new_m, new_v), res):
        for n, val in zip(small, _unpack(r.reshape(-1), shapes)):
            dst[n] = val
    return delta, new_m, new_v


def kernel(x, a_norm, a_w_in, a_w_out, b_norm, b_w_in, b_v_ln_g, b_v_ln_b, b_w_s, b_b_s, b_w_out, c_norm, c_w_in, c_conv_w, c_conv_b, c_ln_g, c_ln_b, c_w_out, d_norm, d_w_in, d_b_f, d_w_out, final_norm, loss_target, m_a_norm, m_a_w_in, m_a_w_out, m_b_norm, m_b_w_in, m_b_v_ln_g, m_b_v_ln_b, m_b_w_s, m_b_b_s, m_b_w_out, m_c_norm, m_c_w_in, m_c_conv_w, m_c_conv_b, m_c_ln_g, m_c_ln_b, m_c_w_out, m_d_norm, m_d_w_in, m_d_b_f, m_d_w_out, m_final_norm, v_a_norm, v_a_w_in, v_a_w_out, v_b_norm, v_b_w_in, v_b_v_ln_g, v_b_v_ln_b, v_b_w_s, v_b_b_s, v_b_w_out, v_c_norm, v_c_w_in, v_c_conv_w, v_c_conv_b, v_c_ln_g, v_c_ln_b, v_c_w_out, v_d_norm, v_d_w_in, v_d_b_f, v_d_w_out, v_final_norm):
    local = dict(zip(_WEIGHTS, (a_norm, a_w_in, a_w_out, b_norm, b_w_in, b_v_ln_g, b_v_ln_b, b_w_s, b_b_s, b_w_out, c_norm, c_w_in,
                                c_conv_w, c_conv_b, c_ln_g, c_ln_b, c_w_out, d_norm, d_w_in, d_b_f, d_w_out, final_norm)))
    m = dict(zip(_WEIGHTS, (m_a_norm, m_a_w_in, m_a_w_out, m_b_norm, m_b_w_in, m_b_v_ln_g, m_b_v_ln_b, m_b_w_s, m_b_b_s, m_b_w_out,
                            m_c_norm, m_c_w_in, m_c_conv_w, m_c_conv_b, m_c_ln_g, m_c_ln_b, m_c_w_out, m_d_norm, m_d_w_in, m_d_b_f,
                            m_d_w_out, m_final_norm)))
    v = dict(zip(_WEIGHTS, (v_a_norm, v_a_w_in, v_a_w_out, v_b_norm, v_b_w_in, v_b_v_ln_g, v_b_v_ln_b, v_b_w_s, v_b_b_s, v_b_w_out,
                            v_c_norm, v_c_w_in, v_c_conv_w, v_c_conv_b, v_c_ln_g, v_c_ln_b, v_c_w_out, v_d_norm, v_d_w_in, v_d_b_f,
                            v_d_w_out, v_final_norm)))
    loss_part, grad_x, g = _local_step(x, loss_target, _gather_weights(local))
    loss = lax.psum(loss_part, ("x", "y", "c"))
    grads = _reduce_grads(g, local)
    delta, new_m, new_v = _update(local, grads, m, v)
    return (loss, grad_x, *[grads[n] for n in _WEIGHTS], *[delta[n] for n in _WEIGHTS],
            *[new_m[n] for n in _WEIGHTS], *[new_v[n] for n in _WEIGHTS])
```

```python
import functools
import math

import jax
import jax.numpy as jnp
from jax import lax
from jax.experimental import pallas as pl
from jax.experimental.pallas import tpu as pltpu

F32, BF16 = jnp.float32, jnp.bfloat16
MESH = pl.DeviceIdType.MESH

D_MODEL = 1024
HEADS = 16
HEAD_DIM = 64
BLK = 128
PAIRS = HEADS // 2
GM_W = 2048
GM_G = 16
CV_W = 2048
CV_K = 31
HALO = 32
EPS = 1e-6
N_CHIPS = 4
PACK_C = 1024
ADAM_LR, ADAM_B1, ADAM_B2, ADAM_EPS, ADAM_WD, ADAM_STEP = 0.001, 0.9, 0.999, 1e-08, 0.01, 10

_NT = (((1,), (1,)), ((), ()))
_TN = (((0,), (0,)), ((), ()))
_NN = (((1,), (0,)), ((), ()))


def _dot(a, b, dims=_NN):
    return lax.dot_general(a, b, dims, preferred_element_type=F32)


def _split3(x):
    hi = x.astype(BF16)
    r = x - hi.astype(F32)
    mid = r.astype(BF16)
    lo = (r - mid.astype(F32)).astype(BF16)
    return hi, mid, lo


def _dot3_right(x, m):
    hi, mid, lo = _split3(x)
    return _dot(hi, m) + _dot(mid, m) + _dot(lo, m)


def _dot3_left(m, x):
    hi, mid, lo = _split3(x)
    return _dot(m, hi) + _dot(m, mid) + _dot(m, lo)


def _sigmoid(x):
    return 1.0 / (1.0 + jnp.exp(-x))


def _silu(x):
    return x * _sigmoid(x)


def _dsilu(x):
    s = _sigmoid(x)
    return s * (1.0 + x * (1.0 - s))


_GELU_C = math.sqrt(2.0 / math.pi)
_GELU_A = 0.044715


def _gelu(x):
    return 0.5 * x * (1.0 + jnp.tanh(_GELU_C * (x + _GELU_A * x * x * x)))


def _dgelu(x):
    t = jnp.tanh(_GELU_C * (x + _GELU_A * x * x * x))
    return 0.5 * (1.0 + t) + 0.5 * x * (1.0 - t * t) * _GELU_C * (1.0 + 3.0 * _GELU_A * x * x)


def _log_sigmoid(x):
    return jnp.minimum(x, 0.0) - jnp.log(1.0 + jnp.exp(-jnp.abs(x)))


def _rms_fwd(x, g):
    r = lax.rsqrt(jnp.mean(x * x, axis=-1, keepdims=True) + EPS)
    return x * r * g


def _rms_bwd(dy, x, g):
    r = lax.rsqrt(jnp.mean(x * x, axis=-1, keepdims=True) + EPS)
    xh = x * r
    dxh = dy * g
    dx = r * (dxh - xh * jnp.mean(dxh * xh, axis=-1, keepdims=True))
    return dx, dy * xh


def _ln_stats(x):
    mu = jnp.mean(x, axis=-1, keepdims=True)
    xc = x - mu
    r = lax.rsqrt(jnp.mean(xc * xc, axis=-1, keepdims=True) + EPS)
    return xc * r, r


def _ln_bwd(dy, xh, r, g):
    dxh = dy * g
    return r * (dxh - jnp.mean(dxh, axis=-1, keepdims=True) - xh * jnp.mean(dxh * xh, axis=-1, keepdims=True))


def _colsum(x):
    return jnp.sum(x, axis=0, keepdims=True)


def _tile(n, want):
    for t in range(min(n, want), 7, -1):
        if n % t == 0 and t % 8 == 0:
            return t
    return n


MM_TILE = 1024


def _matmul(a, b, *, name, mode="nn", residual=None, out_shards=1):
    (m, k) = a.shape
    b_shards = b.shape[0] if b.ndim == 3 else 1
    if mode == "nn":
        n = b.shape[-1] * b_shards
        tn, tk = _tile(n // max(b_shards, out_shards), MM_TILE), _tile(k, MM_TILE)
    else:
        n = b.shape[-2]
        tn, tk = _tile(n // out_shards, MM_TILE), _tile(k // b_shards, MM_TILE)
    tm = _tile(m, MM_TILE)
    nk = k // tk
    a_spec = pl.BlockSpec((tm, tk), lambda i, j, kk: (i, kk))
    if mode == "nn":
        dims = _NN
        if b_shards == 1:
            b_spec = pl.BlockSpec((tk, tn), lambda i, j, kk: (kk, j))
        else:
            per_b = n // b_shards // tn
            b_spec = pl.BlockSpec((None, tk, tn), lambda i, j, kk: (j // per_b, kk, j % per_b))
    else:
        dims = _NT
        if b_shards == 1:
            b_spec = pl.BlockSpec((tn, tk), lambda i, j, kk: (j, kk))
        else:
            per_b = k // b_shards // tk
            b_spec = pl.BlockSpec((None, tn, tk), lambda i, j, kk: (kk // per_b, j, kk % per_b))
    if out_shards == 1:
        o_spec = pl.BlockSpec((tm, tn), lambda i, j, kk: (i, j))
        o_shape = (m, n)
    else:
        per_o = n // out_shards // tn
        o_spec = pl.BlockSpec((None, tm, tn), lambda i, j, kk: (j // per_o, i, j % per_o))
        o_shape = (out_shards, m, n // out_shards)
    has_res = residual is not None

    def body(a_ref, b_ref, *rest):
        o_ref = rest[-1]
        kk = pl.program_id(2)
        part = _dot(a_ref[...].astype(BF16), b_ref[...].astype(BF16), dims)
        if has_res:
            @pl.when(kk == 0)
            def _():
                o_ref[...] = part + rest[0][...]
        else:
            @pl.when(kk == 0)
            def _():
                o_ref[...] = part

        @pl.when(kk > 0)
        def _():
            o_ref[...] += part

    return pl.pallas_call(
        body, name=name, grid=(m // tm, n // tn, nk),
        in_specs=[a_spec, b_spec] + ([o_spec] if has_res else []),
        out_specs=o_spec, out_shape=jax.ShapeDtypeStruct(o_shape, F32),
        compiler_params=pltpu.CompilerParams(dimension_semantics=("parallel", "parallel", "arbitrary")),
    )(a, b, *([residual] if has_res else []))


def _rows(fn, *, name, steps, ins, outs, accs=(), scratch=()):
    ni, no, na = len(ins), len(outs), len(accs)

    def body(*refs):
        in_refs, out_refs = refs[:ni], refs[ni:ni + no]
        acc_refs, scr = refs[ni + no:ni + no + na], refs[ni + no + na:]
        i = pl.program_id(0)

        @pl.when(i == 0)
        def _():
            for r in acc_refs:
                r[...] = jnp.zeros(r.shape, r.dtype)

        fn(i, in_refs, out_refs, acc_refs, scr)

    def full(shape):
        nd = len(shape)
        return pl.BlockSpec(tuple(shape), lambda i: (0,) * nd)

    res = pl.pallas_call(
        body, name=name, grid=(steps,),
        in_specs=[pl.BlockSpec(bs, im) for _, bs, im in ins],
        out_specs=[pl.BlockSpec(bs, im) for _, _, bs, im in outs] + [full(s) for s, _ in accs],
        out_shape=[jax.ShapeDtypeStruct(s, d) for s, d, _, _ in outs] + [jax.ShapeDtypeStruct(s, d) for s, d in accs],
        scratch_shapes=list(scratch),
        compiler_params=pltpu.CompilerParams(dimension_semantics=("arbitrary",)),
    )(*[a for a, _, _ in ins])
    return res


def _rb(arr, bm, cb=0, width=None):
    w = arr.shape[1] if width is None else width
    return (arr, (bm, w), lambda i: (i, cb))


def _const(arr):
    nd = arr.ndim
    return (arr, tuple(arr.shape), lambda i: (0,) * nd)


def _ro(t, w, dtype, bm):
    return ((t, w), dtype, (bm, w), lambda i: (i, 0))


def _rot(t, w, dtype, bm):
    return ((w, t), dtype, (w, bm), lambda i: (0, i))


def _rmsnorm(x, g, *, name, bm=512):
    t, d = x.shape
    bm = _tile(t, bm)

    def fn(i, ins, outs, accs, scr):
        h = _rms_fwd(ins[0][...], ins[1][...])
        outs[0][...] = h.astype(BF16)
        outs[1][...] = h.T.astype(BF16)

    return _rows(fn, name=name, steps=t // bm, ins=[_rb(x, bm), _const(g)], outs=[_ro(t, d, BF16, bm), _rot(t, d, BF16, bm)])


def _rmsnorm_bwd(dh, x, g, dres, *, name, bm=512):
    t, d = x.shape
    bm = _tile(t, bm)

    def fn(i, ins, outs, accs, scr):
        dx, dgrow = _rms_bwd(ins[0][...], ins[1][...], ins[2][...])
        outs[0][...] = ins[3][...] + dx
        accs[0][...] += _colsum(dgrow)

    return _rows(fn, name=name, steps=t // bm, ins=[_rb(dh, bm), _rb(x, bm), _const(g), _rb(dres, bm)],
                 outs=[_ro(t, d, F32, bm)], accs=[((1, d), F32)])


def _gate(o, p, gcb, *, name, bm=512):
    t, w = o.shape
    bm = _tile(t, bm)

    def fn(i, ins, outs, accs, scr):
        y = ins[0][...] * _silu(ins[1][...])
        outs[0][...] = y.astype(BF16)
        outs[1][...] = y.T.astype(BF16)

    return _rows(fn, name=name, steps=t // bm, ins=[_rb(o, bm), _rb(p, bm, gcb, w)],
                 outs=[_ro(t, w, BF16, bm), _rot(t, w, BF16, bm)])


def _gate_bwd(dy, o, p, gcb, *, name, bm=512):
    t, w = o.shape
    bm = _tile(t, bm)

    def fn(i, ins, outs, accs, scr):
        dy_, o_, g_ = ins[0][...], ins[1][...], ins[2][...]
        outs[0][...] = dy_ * _silu(g_)
        outs[1][...] = dy_ * o_ * _dsilu(g_)

    return _rows(fn, name=name, steps=t // bm, ins=[_rb(dy, bm), _rb(o, bm), _rb(p, bm, gcb, w)],
                 outs=[_ro(t, w, F32, bm), _ro(t, w, F32, bm)])


def _loss_head(x, g, tgt, *, name, bm=512):
    t, d = x.shape
    bm = _tile(t, bm)

    def fn(i, ins, outs, accs, scr):
        x_, g_, tg = ins[0][...], ins[1][...], ins[2][...]
        err = _rms_fwd(x_, g_) - tg
        part = 0.5 * jnp.sum(jnp.sum(err * err, axis=-1, keepdims=True), axis=0, keepdims=True) / d
        dx, dgrow = _rms_bwd(err / d, x_, g_)
        outs[0][...] = dx
        accs[0][...] += _colsum(dgrow)
        accs[1][...] += jnp.broadcast_to(part, (1, BLK))

    return _rows(fn, name=name, steps=t // bm, ins=[_rb(x, bm), _const(g), _rb(tgt, bm)],
                 outs=[_ro(t, d, F32, bm)], accs=[((1, d), F32), ((1, BLK), F32)])


def _gmlp_mix_weights(ws_ref, g):
    row = lax.broadcasted_iota(jnp.int32, (BLK, BLK), 0)
    col = lax.broadcasted_iota(jnp.int32, (BLK, BLK), 1)
    tril = col <= row
    return jnp.where(tril, ws_ref[g], 0.0), tril


def _gmlp_fwd(p, ln_g, ln_b, w_s, bs_t, *, name):
    t = p.shape[0]

    def fn(i, ins, outs, accs, scr):
        p_ref, lg, lb, ws_ref, bst = ins
        vn = _ln_stats(_gelu(p_ref[:, GM_W:2 * GM_W]))[0] * lg[...] + lb[...]
        for g in range(GM_G):
            cs = slice(g * BLK, (g + 1) * BLK)
            wt, _ = _gmlp_mix_weights(ws_ref, g)
            s = _dot(wt.astype(BF16), vn[:, cs].astype(BF16)) + bst[:, g:g + 1]
            u = _gelu(p_ref[:, cs])
            gate = p_ref[:, 2 * GM_W + g * BLK:2 * GM_W + (g + 1) * BLK]
            y = u * s * _silu(gate)
            outs[0][:, cs] = y.astype(BF16)
            outs[1][cs, :] = y.T.astype(BF16)

    return _rows(fn, name=name, steps=t // BLK, ins=[_rb(p, BLK), _const(ln_g), _const(ln_b), _const(w_s), _const(bs_t)],
                 outs=[_ro(t, GM_W, BF16, BLK), _rot(t, GM_W, BF16, BLK)])


def _gmlp_bwd(dy, p, ln_g, ln_b, w_s, bs_t, *, name):
    t = p.shape[0]

    def fn(i, ins, outs, accs, scr):
        dy_ref, p_ref, lg, lb, ws_ref, bst = ins
        dp_ref = outs[0]
        dlg, dlb, dws, dbst = accs
        dvn_ref = scr[0]
        v_pre = p_ref[:, GM_W:2 * GM_W]
        xh, r = _ln_stats(_gelu(v_pre))
        vn = xh * lg[...] + lb[...]
        for g in range(GM_G):
            cs = slice(g * BLK, (g + 1) * BLK)
            gs = slice(2 * GM_W + g * BLK, 2 * GM_W + (g + 1) * BLK)
            wt, tril = _gmlp_mix_weights(ws_ref, g)
            vg = vn[:, cs].astype(BF16)
            s = _dot(wt.astype(BF16), vg) + bst[:, g:g + 1]
            u_pre, gate, dyg = p_ref[:, cs], p_ref[:, gs], dy_ref[:, cs]
            u = _gelu(u_pre)
            dos = dyg * _silu(gate)
            dp_ref[:, gs] = dyg * u * s * _dsilu(gate)
            dp_ref[:, cs] = dos * s * _dgelu(u_pre)
            ds = (dos * u).astype(BF16)
            dws[g] += jnp.where(tril, _dot(ds, vg, _NT), 0.0)
            dbst[:, g:g + 1] += jnp.sum(dos * u, axis=1, keepdims=True)
            dvn_ref[:, cs] = _dot(wt.astype(BF16), ds, _TN)
        dvn = dvn_ref[...]
        dlg[...] += _colsum(dvn * xh)
        dlb[...] += _colsum(dvn)
        dp_ref[:, GM_W:2 * GM_W] = _ln_bwd(dvn, xh, r, lg[...]) * _dgelu(v_pre)

    return _rows(fn, name=name, steps=t // BLK,
                 ins=[_rb(dy, BLK), _rb(p, BLK), _const(ln_g), _const(ln_b), _const(w_s), _const(bs_t)],
                 outs=[_ro(t, 3 * GM_W, F32, BLK)],
                 accs=[((1, GM_W), F32), ((1, GM_W), F32), ((GM_G, BLK, BLK), F32), ((BLK, GM_G), F32)],
                 scratch=[pltpu.VMEM((BLK, GM_W), F32)])


CV_BM = 128
CV_RC = 8
SUBLANES = 8
CV_FWD_OFFS = [HALO - (CV_K - 1) + k for k in range(CV_K)]
CV_BWD_OFFS = [CV_K - 1 - k for k in range(CV_K)]


def _conv_halo_prev(p, cb, bm):
    per = bm // HALO
    return (p, (HALO, CV_W), lambda i: (jnp.maximum(i * per - 1, 0), cb))


def _conv_scratch(bm):
    return [pltpu.VMEM((bm + HALO, CV_W), F32), pltpu.VMEM((SUBLANES - 1, bm + HALO - SUBLANES, CV_W), F32),
            pltpu.VMEM((bm, CV_W), F32)]


def _conv_shift_copies(ext_ref, sh_ref):
    rows = sh_ref.shape[1]
    for b in range(1, SUBLANES):
        sh_ref[b - 1] = ext_ref[pl.ds(b, rows), :]


def _conv_window(ext_ref, sh_ref, off, r0, rows):
    b = off % SUBLANES
    src = ext_ref if b == 0 else sh_ref.at[b - 1]
    return src[pl.ds(r0 + (off - b), rows), :]


def _conv_taps(ext_ref, sh_ref, cw_ref, y_ref, offs):
    bm = y_ref.shape[0]

    def chunk(ci, c):
        r0 = pl.multiple_of(ci * CV_RC, CV_RC)
        acc = jnp.zeros((CV_RC, CV_W), F32)
        for k in range(CV_K):
            acc = acc + cw_ref[pl.ds(k * SUBLANES, CV_RC), :] * _conv_window(ext_ref, sh_ref, offs[k], r0, CV_RC)
        y_ref[pl.ds(r0, CV_RC), :] = acc
        return c

    lax.fori_loop(0, bm // CV_RC, chunk, 0)


def _conv_dweights(dy1_ref, ext_ref, sh_ref, dcw_ref):
    bm = dy1_ref.shape[0]
    groups = 4
    for k in range(CV_K):
        def step(ci, acc, off=CV_FWD_OFFS[k]):
            prods = []
            for u in range(groups):
                r0 = pl.multiple_of((ci * groups + u) * CV_RC, CV_RC)
                prods.append(dy1_ref[pl.ds(r0, CV_RC), :] * _conv_window(ext_ref, sh_ref, off, r0, CV_RC))
            return acc + ((prods[0] + prods[1]) + (prods[2] + prods[3]))

        dcw_ref[k:k + 1, :] += _colsum(lax.fori_loop(0, bm // (CV_RC * groups), step, jnp.zeros((CV_RC, CV_W), F32)))


def _conv_fill(i, ext_ref, a_prev, b_prev, a, b, bm, seq):
    keep = jnp.where((i % (seq // bm)) == 0, 0.0, 1.0)
    ext_ref[pl.ds(0, HALO), :] = keep * (a_prev * _sigmoid(b_prev))
    ext_ref[pl.ds(HALO, bm), :] = a * _sigmoid(b)


def _conv_fwd(p, cw, cb, ln_g, ln_b, seq, *, name, bm=CV_BM):
    t = p.shape[0]

    def fn(i, ins, outs, accs, scr):
        a, b, gate, ap, bp = [r[...] for r in ins[:5]]
        cw_ref, cb_, lg, lb = ins[5], ins[6][...], ins[7][...], ins[8][...]
        ext, sh, y = scr
        _conv_fill(i, ext, ap, bp, a, b, bm, seq)
        _conv_shift_copies(ext, sh)
        _conv_taps(ext, sh, cw_ref, y, CV_FWD_OFFS)
        y2 = _ln_stats(y[...] + cb_)[0] * lg + lb
        out = _silu(y2) * _silu(gate)
        outs[0][...] = out.astype(BF16)
        outs[1][...] = out.T.astype(BF16)

    return _rows(fn, name=name, steps=t // bm,
                 ins=[_rb(p, bm, 0, CV_W), _rb(p, bm, 1, CV_W), _rb(p, bm, 2, CV_W),
                      _conv_halo_prev(p, 0, bm), _conv_halo_prev(p, 1, bm),
                      _const(cw), _const(cb), _const(ln_g), _const(ln_b)],
                 outs=[_ro(t, CV_W, BF16, bm), _rot(t, CV_W, BF16, bm)], scratch=_conv_scratch(bm))


def _conv_bwd_post(dy, p, cw, cb, ln_g, ln_b, seq, *, name, bm=CV_BM):
    t = p.shape[0]

    def fn(i, ins, outs, accs, scr):
        dy_, a, b, gate, ap, bp = [r[...] for r in ins[:6]]
        cw_ref, cb_, lg, lb = ins[6], ins[7][...], ins[8][...], ins[9][...]
        dlg, dlb, dcb, dcw = accs
        ext, sh, y = scr
        _conv_fill(i, ext, ap, bp, a, b, bm, seq)
        _conv_shift_copies(ext, sh)
        _conv_taps(ext, sh, cw_ref, y, CV_FWD_OFFS)
        xh, r = _ln_stats(y[...] + cb_)
        y2 = xh * lg + lb
        outs[1][...] = dy_ * _silu(y2) * _dsilu(gate)
        dy2 = dy_ * _silu(gate) * _dsilu(y2)
        dlg[...] += _colsum(dy2 * xh)
        dlb[...] += _colsum(dy2)
        dy1 = _ln_bwd(dy2, xh, r, lg)
        outs[0][...] = dy1
        dcb[...] += _colsum(dy1)
        _conv_dweights(outs[0], ext, sh, dcw)

    return _rows(fn, name=name, steps=t // bm,
                 ins=[_rb(dy, bm), _rb(p, bm, 0, CV_W), _rb(p, bm, 1, CV_W), _rb(p, bm, 2, CV_W),
                      _conv_halo_prev(p, 0, bm), _conv_halo_prev(p, 1, bm),
                      _const(cw), _const(cb), _const(ln_g), _const(ln_b)],
                 outs=[_ro(t, CV_W, F32, bm), _ro(t, CV_W, F32, bm)],
                 accs=[((1, CV_W), F32), ((1, CV_W), F32), ((1, CV_W), F32), ((CV_K, CV_W), F32)],
                 scratch=_conv_scratch(bm))


def _conv_bwd_pre(dy1, dgate, p, cw, seq, *, name, bm=CV_BM):
    t = p.shape[0]
    per = bm // HALO
    last_halo = t // HALO - 1

    def fn(i, ins, outs, accs, scr):
        d1, d1n, dg, a, b = [r[...] for r in ins[:5]]
        ext, sh, y = scr
        keep = jnp.where((i % (seq // bm)) == (seq // bm - 1), 0.0, 1.0)
        ext[pl.ds(0, bm), :] = d1
        ext[pl.ds(bm, HALO), :] = keep * d1n
        _conv_shift_copies(ext, sh)
        _conv_taps(ext, sh, ins[5], y, CV_BWD_OFFS)
        dy0 = y[...]
        sb = _sigmoid(b)
        outs[0][:, 0:CV_W] = dy0 * sb
        outs[0][:, CV_W:2 * CV_W] = dy0 * a * sb * (1.0 - sb)
        outs[0][:, 2 * CV_W:3 * CV_W] = dg

    return _rows(fn, name=name, steps=t // bm,
                 ins=[_rb(dy1, bm), (dy1, (HALO, CV_W), lambda i: (jnp.minimum((i + 1) * per, last_halo), 0)),
                      _rb(dgate, bm), _rb(p, bm, 0, CV_W), _rb(p, bm, 1, CV_W), _const(cw)],
                 outs=[_ro(t, 3 * CV_W, F32, bm)], scratch=_conv_scratch(bm))[0]


def _iotas():
    row = lax.broadcasted_iota(jnp.int32, (BLK, BLK), 0)
    col = lax.broadcasted_iota(jnp.int32, (BLK, BLK), 1)
    return row, col


def _heads(x, head0):
    if head0.shape != x.shape:
        head0 = lax.broadcasted_iota(jnp.int32, x.shape, 1) < HEAD_DIM
    return jnp.where(head0, x, 0.0).astype(BF16), jnp.where(head0, 0.0, x).astype(BF16)


def _pair_spec(seq, off):
    return pl.BlockSpec((seq, BLK), lambda b, hp: (b, off + hp))


def _stat_spec(seq):
    return pl.BlockSpec((None, None, seq, BLK), lambda b, hp: (b, hp, 0, 0))


_ATT_PARAMS = dict(compiler_params=pltpu.CompilerParams(dimension_semantics=("parallel", "parallel")))
_SCALE = 1.0 / math.sqrt(HEAD_DIM)


Q_BLOCK = 256
KEY_BLOCK = 256


def _stack_heads(x, head0, scale=None):
    if scale is not None:
        x = x * scale
    return jnp.concatenate(_heads(x, head0), axis=0)


def _pair_cols(x, head0, fill):
    a = jnp.max(jnp.where(head0, x, fill), axis=1, keepdims=True)
    b = jnp.max(jnp.where(head0, fill, x), axis=1, keepdims=True)
    return jnp.concatenate([a, b], axis=0)


def _causal_mask(t0, s0, tq, kw, inclusive):
    row = lax.broadcasted_iota(jnp.int32, (2 * tq, kw), 0) & (tq - 1)
    col = lax.broadcasted_iota(jnp.int32, (2 * tq, kw), 1)
    return (s0 + col) <= (t0 + row) if inclusive else (s0 + col) < (t0 + row)


def _sub(x, j):
    return x[:, j * BLK:(j + 1) * BLK]


def _block_cumsum(x, tri, ksub):
    hi = x.astype(BF16)
    lo = (x - hi.astype(F32)).astype(BF16)
    cs = _dot(jnp.concatenate([_sub(pt, j) for pt in (hi, lo) for j in range(ksub)], axis=0), tri)
    n = x.shape[0]
    return ([cs[j * n:(j + 1) * n] + cs[(ksub + j) * n:(ksub + j + 1) * n] for j in range(ksub)],
            [jnp.sum(_sub(x, j), axis=1, keepdims=True) for j in range(ksub)])


def _sb_terms(qs, k, mask):
    return _sb_terms_z(_dot(qs, k, _NT), mask)


def _sb_terms_z(z, mask):
    t = jnp.log(1.0 + jnp.exp(-jnp.abs(z)))
    lsz = jnp.minimum(z, 0.0) - t
    lr = lsz - z
    if mask is not None:
        lr = jnp.where(mask, lr, 0.0)
    return lsz, lr


def _sb_fwd(p, nb, seq, *, name):
    tq = min(Q_BLOCK, seq)
    nq = seq // tq
    kw = min(KEY_BLOCK, seq)
    ksub = kw // BLK

    def body(q_ref, k_ref, v_ref, o_ref, tot_ref):
        row, col = _iotas()
        colq = lax.broadcasted_iota(jnp.int32, (tq, BLK), 1)
        head0 = colq < HEAD_DIM
        upper = (row > col).astype(BF16)

        def qblock(qb, c):
            t0 = pl.multiple_of(qb * tq, tq)
            qs = _stack_heads(q_ref[pl.ds(t0, tq), :], head0, _SCALE)
            diag = (t0 + tq - 1) // kw

            def kblock(kb, carry, masked):
                acc, run = carry
                s0 = pl.multiple_of(kb * kw, kw)
                k = k_ref[pl.ds(s0, kw), :].astype(BF16)
                v0, v1 = _heads(v_ref[pl.ds(s0, kw), :], head0)
                mask = _causal_mask(t0, s0, tq, kw, False)[:tq] if masked else None
                zs = [_dot(qs[h * tq:(h + 1) * tq], k, _NT) for h in range(2)]
                terms = []
                for h in range(2):
                    lsz, lr = _sb_terms_z(zs[h], mask)
                    terms.append((lsz,) + _block_cumsum(lr, upper, ksub))
                runs = []
                for h, vh in enumerate((v0, v1)):
                    lsz, after, total = terms[h]
                    r = run[h]
                    ws = [None] * ksub
                    for j in reversed(range(ksub)):
                        w = jnp.exp(_sub(lsz, j) + after[j] + r)
                        if masked:
                            w = jnp.where(_sub(mask, j), w, 0.0)
                        ws[j] = w.astype(BF16)
                        r = r + total[j]
                    acc = acc + _dot(jnp.concatenate(ws, axis=1), vh)
                    runs.append(r)
                return acc, tuple(runs)

            zc = jnp.zeros((tq, 1), F32)
            carry = kblock(diag, (jnp.zeros((tq, BLK), F32), (zc, zc)), True)
            acc, run = lax.fori_loop(0, diag, lambda it, cr: kblock(diag - 1 - it, cr, False), carry)
            o_ref[pl.ds(t0, tq), :] = acc
            tot_ref[pl.ds(t0, tq), :] = jnp.where(head0, run[0], run[1])
            return c

        lax.fori_loop(0, nq, qblock, 0)

    return pl.pallas_call(
        body, name=name, grid=(nb, PAIRS),
        in_specs=[_pair_spec(seq, 0), _pair_spec(seq, PAIRS), _pair_spec(seq, 2 * PAIRS)],
        out_specs=[_pair_spec(seq, 0), _stat_spec(seq)],
        out_shape=[jax.ShapeDtypeStruct((nb * seq, D_MODEL), F32), jax.ShapeDtypeStruct((nb, PAIRS, seq, BLK), F32)],
        **_ATT_PARAMS,
    )(p, p, p)


def _sb_bwd(p, do, tot, nb, seq, *, name):
    tq = min(Q_BLOCK, seq)
    nq = seq // tq
    kw = min(KEY_BLOCK, seq)
    ksub = kw // BLK

    def body(q_ref, k_ref, v_ref, do_ref, tot_ref, dq_ref, dk_ref, dv_ref):
        row, col = _iotas()
        colq = lax.broadcasted_iota(jnp.int32, (tq, BLK), 1)
        head0 = colq < HEAD_DIM
        lower_incl = (row <= col).astype(BF16)
        lower_strict = (row < col).astype(BF16)
        dk_ref[...] = jnp.zeros(dk_ref.shape, F32)
        dv_ref[...] = jnp.zeros(dv_ref.shape, F32)

        def qblock(qb, c):
            t0 = pl.multiple_of(qb * tq, tq)
            qs = _stack_heads(q_ref[pl.ds(t0, tq), :], head0, _SCALE)
            dos = _stack_heads(do_ref[pl.ds(t0, tq), :], head0)
            tot = tot_ref[pl.ds(t0, tq), :]
            swapped = pltpu.roll(tot, HEAD_DIM, 1)
            tts = (jnp.where(head0, tot, swapped), jnp.where(head0, swapped, tot))
            diag = (t0 + tq - 1) // kw

            def kblock(kb, carry, masked):
                dq, pfs, efs = carry
                s0 = pl.multiple_of(kb * kw, kw)
                kf = k_ref[pl.ds(s0, kw), :]
                k = kf.astype(BF16)
                khs = _heads(kf, head0)
                v = v_ref[pl.ds(s0, kw), :].astype(BF16)
                mask = _causal_mask(t0, s0, tq, kw, False)[:tq] if masked else None
                zs = [_dot(qs[h * tq:(h + 1) * tq], k, _NT) for h in range(2)]
                dws = [_dot(dos[h * tq:(h + 1) * tq], v, _NT) for h in range(2)]
                first = []
                for h in range(2):
                    lsz, lr = _sb_terms_z(zs[h], None)
                    lrm = jnp.where(mask, lr, 0.0) if masked else lr
                    first.append((lsz, lr) + _block_cumsum(lrm, lower_incl, ksub))
                second, pfs_out = [], []
                for h in range(2):
                    lsz, lr, incl, total = first[h]
                    pf = pfs[h]
                    ws, ews = [], []
                    for j in range(ksub):
                        w = jnp.exp(_sub(lsz, j) + (tts[h] - pf - incl[j]))
                        if masked:
                            w = jnp.where(_sub(mask, j), w, 0.0)
                        pf = pf + total[j]
                        ws.append(w.astype(BF16))
                        ews.append(_sub(dws[h], j) * w)
                    pfs_out.append(pf)
                    second.append((ws, ews) + _block_cumsum(jnp.concatenate(ews, axis=1), lower_strict, ksub))
                dz_h, efs_out = [], []
                for h in range(2):
                    lsz, lr = first[h][:2]
                    ws, ews, before, etotal = second[h]
                    ef = efs[h]
                    dzs = []
                    for j in range(ksub):
                        dz = ews[j] * jnp.exp(_sub(lr, j)) - (ef + before[j]) * jnp.exp(_sub(lsz, j))
                        ef = ef + etotal[j]
                        if masked:
                            dz = jnp.where(_sub(mask, j), dz, 0.0)
                        dzs.append(dz.astype(BF16))
                    efs_out.append(ef)
                    dz_h.append(jnp.concatenate(dzs, axis=1))
                    dq = dq + _dot(dz_h[h], khs[h])
                w = jnp.concatenate([jnp.concatenate(second[h][0], axis=1) for h in range(2)], axis=0)
                dk_ref[pl.ds(s0, kw), :] += _dot(jnp.concatenate(dz_h, axis=0), qs, _TN)
                dv_ref[pl.ds(s0, kw), :] += _dot(w, dos, _TN)
                return dq, tuple(pfs_out), tuple(efs_out)

            zc = jnp.zeros((tq, 1), F32)
            carry = lax.fori_loop(0, diag, lambda kb, cr: kblock(kb, cr, False), (jnp.zeros((tq, BLK), F32), (zc, zc), (zc, zc)))
            dq_ref[pl.ds(t0, tq), :] = kblock(diag, carry, True)[0] * _SCALE
            return c

        lax.fori_loop(0, nq, qblock, 0)

    t = nb * seq
    return pl.pallas_call(
        body, name=name, grid=(nb, PAIRS),
        in_specs=[_pair_spec(seq, 0), _pair_spec(seq, PAIRS), _pair_spec(seq, 2 * PAIRS), _pair_spec(seq, 0), _stat_spec(seq)],
        out_specs=[_pair_spec(seq, 0)] * 3,
        out_shape=[jax.ShapeDtypeStruct((t, D_MODEL), F32)] * 3,
        **_ATT_PARAMS,
    )(p, p, p, do, tot)


def _fox_cum(f, bf, nb, seq, *, name):
    def body(f_ref, bf_ref, cc_ref, cr_ref):
        row, col = _iotas()
        lower = (col <= row).astype(BF16)
        carry = jnp.zeros((1, BLK), F32)
        for blk in range(seq // BLK):
            rs = slice(blk * BLK, (blk + 1) * BLK)
            lf = jnp.where(col < HEADS, _log_sigmoid(f_ref[rs, :] + bf_ref[...]), 0.0)
            cc = _dot3_left(lower, lf) + carry
            cc_ref[rs, :] = cc
            cr_ref[:, rs] = cc.T[0:HEADS, :]
            carry = carry + _colsum(lf)

    return pl.pallas_call(
        body, name=name, grid=(nb,),
        in_specs=[pl.BlockSpec((seq, BLK), lambda b: (b, 0)), pl.BlockSpec((1, BLK), lambda b: (0, 0))],
        out_specs=[pl.BlockSpec((seq, BLK), lambda b: (b, 0)), pl.BlockSpec((None, HEADS, seq), lambda b: (b, 0, 0))],
        out_shape=[jax.ShapeDtypeStruct((nb * seq, BLK), F32), jax.ShapeDtypeStruct((nb, HEADS, seq), F32)],
        compiler_params=pltpu.CompilerParams(dimension_semantics=("parallel",)),
    )(f, bf)


def _fox_cum_bwd(dcr, dcc, f, bf, nb, seq, *, name):
    def body(dcr_ref, dcc_ref, f_ref, bf_ref, df_ref, dbf_ref):
        row, col = _iotas()
        upper_incl = (col >= row).astype(BF16)

        @pl.when(pl.program_id(0) == 0)
        def _():
            dbf_ref[...] = jnp.zeros((1, BLK), F32)

        carry = jnp.zeros((1, BLK), F32)
        for blk in reversed(range(seq // BLK)):
            rs = slice(blk * BLK, (blk + 1) * BLK)
            dc = dcr_ref[:, rs].T + dcc_ref[rs, :]
            dlf = _dot3_left(upper_incl, dc) + carry
            carry = carry + _colsum(dc)
            fl = f_ref[rs, :] + bf_ref[...]
            df = jnp.where(col < HEADS, dlf * _sigmoid(-fl), 0.0)
            df_ref[rs, :] = df
            dbf_ref[...] += _colsum(df)

    return pl.pallas_call(
        body, name=name, grid=(nb,),
        in_specs=[pl.BlockSpec((None, BLK, seq), lambda b: (b, 0, 0)), pl.BlockSpec((seq, BLK), lambda b: (b, 0)),
                  pl.BlockSpec((seq, BLK), lambda b: (b, 0)), pl.BlockSpec((1, BLK), lambda b: (0, 0))],
        out_specs=[pl.BlockSpec((seq, BLK), lambda b: (b, 0)), pl.BlockSpec((1, BLK), lambda b: (0, 0))],
        out_shape=[jax.ShapeDtypeStruct((nb * seq, BLK), F32), jax.ShapeDtypeStruct((1, BLK), F32)],
        compiler_params=pltpu.CompilerParams(dimension_semantics=("arbitrary",)),
    )(dcr, dcc, f, bf)


def _fox_cum_cols(cc_ref, t0, tq, colq, hp):
    cc = cc_ref[pl.ds(t0, tq), :]
    c0 = jnp.sum(jnp.where(colq == 2 * hp, cc, 0.0), axis=1, keepdims=True)
    c1 = jnp.sum(jnp.where(colq == 2 * hp + 1, cc, 0.0), axis=1, keepdims=True)
    return c0, c1


def _fox_bias(c0, c1, cr_ref, s0, kw):
    return jnp.concatenate([c0 - cr_ref[0:1, pl.ds(s0, kw)], c1 - cr_ref[1:2, pl.ds(s0, kw)]], axis=0)


def _fox_fwd(p, cc, cr, nb, seq, *, name):
    tq = min(Q_BLOCK, seq)
    nq = seq // tq
    kw = min(KEY_BLOCK, seq)
    ksub = kw // BLK

    def body(q_ref, k_ref, v_ref, cc_ref, cr_ref, o_ref, lse_ref):
        hp = pl.program_id(1)
        row, col = _iotas()
        colq = lax.broadcasted_iota(jnp.int32, (tq, BLK), 1)
        head0 = colq < HEAD_DIM

        def qblock(qb, c):
            t0 = pl.multiple_of(qb * tq, tq)
            qs = _stack_heads(q_ref[pl.ds(t0, tq), :], head0, _SCALE)
            c0, c1 = _fox_cum_cols(cc_ref, t0, tq, colq, hp)
            diag = (t0 + tq - 1) // kw

            def kblock(kb, carry, masked):
                accs, ms = carry
                s0 = pl.multiple_of(kb * kw, kw)
                k = k_ref[pl.ds(s0, kw), :].astype(BF16)
                vf = v_ref[pl.ds(s0, kw), :]
                own0 = lax.broadcasted_iota(jnp.int32, vf.shape, 1) < HEAD_DIM
                vs = (jnp.where(own0, vf, 1.0).astype(BF16), jnp.where(own0, 1.0, vf).astype(BF16))
                mask = _causal_mask(t0, s0, tq, kw, True)[:tq] if masked else None
                zs = [_dot(qs[h * tq:(h + 1) * tq], k, _NT) for h in range(2)]
                parts = []
                for h, ch in enumerate((c0, c1)):
                    s = zs[h] + (ch - cr_ref[h:h + 1, pl.ds(s0, kw)])
                    if masked:
                        s = jnp.where(mask, s, -jnp.inf)
                    m_new = jnp.maximum(ms[h], jnp.max(s, axis=1, keepdims=True))
                    parts.append((jnp.exp(s - m_new).astype(BF16), jnp.exp(ms[h] - m_new), m_new))
                return (tuple(accs[h] * parts[h][1] + _dot(parts[h][0], vs[h]) for h in range(2)),
                        tuple(parts[h][2] for h in range(2)))

            zeros, ninf = jnp.zeros((tq, BLK), F32), jnp.full((tq, 1), -jnp.inf, F32)
            carry = lax.fori_loop(0, diag, lambda kb, cr: kblock(kb, cr, False), ((zeros, zeros), (ninf, ninf)))
            (acc0, acc1), (m0, m1) = kblock(diag, carry, True)
            l = jnp.where(head0, pltpu.roll(acc0, HEAD_DIM, 1), pltpu.roll(acc1, HEAD_DIM, 1))
            o_ref[pl.ds(t0, tq), :] = jnp.where(head0, acc0, acc1) / l
            lse_ref[pl.ds(t0, tq), :] = jnp.where(head0, m0, m1) + jnp.log(l)
            return c

        lax.fori_loop(0, nq, qblock, 0)

    return pl.pallas_call(
        body, name=name, grid=(nb, PAIRS),
        in_specs=[_pair_spec(seq, 0), _pair_spec(seq, PAIRS), _pair_spec(seq, 2 * PAIRS),
                  pl.BlockSpec((seq, BLK), lambda b, hp: (b, 0)), pl.BlockSpec((None, None, 8, seq), lambda b, hp: (b, hp, 0, 0))],
        out_specs=[_pair_spec(seq, 0), _stat_spec(seq)],
        out_shape=[jax.ShapeDtypeStruct((nb * seq, D_MODEL), F32), jax.ShapeDtypeStruct((nb, PAIRS, seq, BLK), F32)],
        **_ATT_PARAMS,
    )(p, p, p, cc, cr)


def _fox_bwd(p, do, o, lse, cc, cr, nb, seq, *, name):
    tq = min(Q_BLOCK, seq)
    nq = seq // tq
    kw = min(KEY_BLOCK, seq)
    ksub = kw // BLK

    def body(q_ref, k_ref, v_ref, do_ref, o_ref, lse_ref, cc_ref, cr_ref, dq_ref, dk_ref, dv_ref, dcr_ref, dcc_ref):
        hp = pl.program_id(1)
        row, col = _iotas()
        colq = lax.broadcasted_iota(jnp.int32, (tq, BLK), 1)
        head0 = colq < HEAD_DIM
        dk_ref[...] = jnp.zeros(dk_ref.shape, F32)
        dv_ref[...] = jnp.zeros(dv_ref.shape, F32)
        dcr_ref[...] = jnp.zeros(dcr_ref.shape, F32)

        @pl.when(hp == 0)
        def _():
            dcc_ref[...] = jnp.zeros(dcc_ref.shape, F32)

        def qblock(qb, c):
            t0 = pl.multiple_of(qb * tq, tq)
            qs = _stack_heads(q_ref[pl.ds(t0, tq), :], head0, _SCALE)
            dof = do_ref[pl.ds(t0, tq), :]
            dos = _stack_heads(dof, head0)
            prod = dof * o_ref[pl.ds(t0, tq), :]
            dl = jnp.concatenate([jnp.sum(jnp.where(head0, prod, 0.0), axis=1, keepdims=True),
                                  jnp.sum(jnp.where(head0, 0.0, prod), axis=1, keepdims=True)], axis=0)
            lse = _pair_cols(lse_ref[pl.ds(t0, tq), :], head0, -jnp.inf)
            c0, c1 = _fox_cum_cols(cc_ref, t0, tq, colq, hp)
            diag = (t0 + tq - 1) // kw

            def kblock(kb, carry, masked):
                dq, rs = carry
                s0 = pl.multiple_of(kb * kw, kw)
                kf = k_ref[pl.ds(s0, kw), :]
                k = kf.astype(BF16)
                k0, k1 = _heads(kf, head0)
                v = v_ref[pl.ds(s0, kw), :].astype(BF16)
                mask = _causal_mask(t0, s0, tq, kw, True)[:tq] if masked else None
                zs = [_dot(qs[h * tq:(h + 1) * tq], k, _NT) for h in range(2)]
                dps = [_dot(dos[h * tq:(h + 1) * tq], v, _NT) for h in range(2)]
                prs, dss, rss = [], [], []
                for h, (ch, kh) in enumerate(((c0, k0), (c1, k1))):
                    rows = slice(h * tq, (h + 1) * tq)
                    pr = jnp.exp(zs[h] + (ch - cr_ref[h:h + 1, pl.ds(s0, kw)]) - lse[rows])
                    if masked:
                        pr = jnp.where(mask, pr, 0.0)
                    ds = pr * (dps[h] - dl[rows])
                    dcr_ref[h:h + 1, pl.ds(s0, kw)] -= _colsum(ds)
                    rss.append(rs[rows] + jnp.sum(ds, axis=1, keepdims=True))
                    prs.append(pr.astype(BF16))
                    dss.append(ds.astype(BF16))
                    dq = dq + _dot(dss[h], kh)
                dk_ref[pl.ds(s0, kw), :] += _dot(jnp.concatenate(dss, axis=0), qs, _TN)
                dv_ref[pl.ds(s0, kw), :] += _dot(jnp.concatenate(prs, axis=0), dos, _TN)
                return dq, jnp.concatenate(rss, axis=0)

            init = (jnp.zeros((tq, BLK), F32), jnp.zeros((2 * tq, 1), F32))
            carry = lax.fori_loop(0, diag, lambda kb, cr: kblock(kb, cr, False), init)
            dq, rs = kblock(diag, carry, True)
            dq_ref[pl.ds(t0, tq), :] = dq * _SCALE
            dcc_ref[pl.ds(t0, tq), :] += jnp.where(colq == 2 * hp, rs[:tq], 0.0) + jnp.where(colq == 2 * hp + 1, rs[tq:], 0.0)
            return c

        lax.fori_loop(0, nq, qblock, 0)

    t = nb * seq
    return pl.pallas_call(
        body, name=name, grid=(nb, PAIRS),
        in_specs=[_pair_spec(seq, 0), _pair_spec(seq, PAIRS), _pair_spec(seq, 2 * PAIRS), _pair_spec(seq, 0), _pair_spec(seq, 0),
                  _stat_spec(seq), pl.BlockSpec((seq, BLK), lambda b, hp: (b, 0)),
                  pl.BlockSpec((None, None, 8, seq), lambda b, hp: (b, hp, 0, 0))],
        out_specs=[_pair_spec(seq, 0)] * 3 + [pl.BlockSpec((None, None, 8, seq), lambda b, hp: (b, hp, 0, 0)),
                                              pl.BlockSpec((seq, BLK), lambda b, hp: (b, 0))],
        out_shape=[jax.ShapeDtypeStruct((t, D_MODEL), F32)] * 3 + [jax.ShapeDtypeStruct((nb, PAIRS, 8, seq), F32),
                                                                     jax.ShapeDtypeStruct((t, BLK), F32)],
        compiler_params=pltpu.CompilerParams(dimension_semantics=("parallel", "arbitrary")),
    )(p, p, p, do, o, lse, cc, cr)


def _row_shards(x):
    return x.reshape(N_CHIPS, x.shape[0] // N_CHIPS, x.shape[1])


def _local_step(x3, tgt3, w):
    nb, seq, d = x3.shape
    t = nb * seq
    x0, tgt = x3.reshape(t, d), tgt3.reshape(t, d)
    g = {}

    a_gain = w["a_norm"].reshape(1, d)
    h_a, ht_a = _rmsnorm(x0, a_gain, name="a_norm_fwd")
    p_a = _matmul(h_a, w["a_w_in"], name="a_in_fwd")
    o_a, tot_a = _sb_fwd(p_a, nb, seq, name="a_attn_fwd")
    y_a, yt_a = _gate(o_a, p_a, 3, name="a_gate_fwd")
    x1 = _matmul(y_a, w["a_w_out"], name="a_out_fwd", residual=x0)

    b_gain = w["b_norm"].reshape(1, d)
    b_lg, b_lb = w["b_v_ln_g"].reshape(1, GM_W), w["b_v_ln_b"].reshape(1, GM_W)
    b_ws, b_bst = w["b_w_s"].reshape(GM_G, BLK, BLK), w["b_b_s"].reshape(GM_G, BLK).T
    h_b, ht_b = _rmsnorm(x1, b_gain, name="b_norm_fwd")
    p_b = _matmul(h_b, w["b_w_in"], name="b_in_fwd")
    y_b, yt_b = _gmlp_fwd(p_b, b_lg, b_lb, b_ws, b_bst, name="b_mix_fwd")
    x2 = _matmul(y_b, w["b_w_out"], name="b_out_fwd", residual=x1)

    c_gain = w["c_norm"].reshape(1, d)
    c_cw = jnp.repeat(w["c_conv_w"].reshape(CV_K, CV_W), SUBLANES, axis=0)
    c_cb = w["c_conv_b"].reshape(1, CV_W)
    c_lg, c_lb = w["c_ln_g"].reshape(1, CV_W), w["c_ln_b"].reshape(1, CV_W)
    h_c, ht_c = _rmsnorm(x2, c_gain, name="c_norm_fwd")
    p_c = _matmul(h_c, w["c_w_in"], name="c_in_fwd")
    y_c, yt_c = _conv_fwd(p_c, c_cw, c_cb, c_lg, c_lb, seq, name="c_conv_fwd")
    x3_ = _matmul(y_c, w["c_w_out"], name="c_out_fwd", residual=x2)

    d_gain = w["d_norm"].reshape(1, d)
    d_win = w["d_w_in"].reshape(d, 4 * D_MODEL + HEADS)
    d_wmain = d_win[:, :4 * D_MODEL]
    d_wf = jnp.pad(d_win[:, 4 * D_MODEL:], ((0, 0), (0, BLK - HEADS)))
    d_bf = jnp.pad(w["d_b_f"].reshape(1, HEADS), ((0, 0), (0, BLK - HEADS)))
    h_d, ht_d = _rmsnorm(x3_, d_gain, name="d_norm_fwd")
    p_d = _matmul(h_d, d_wmain, name="d_in_fwd")
    f_d = _matmul(h_d, d_wf, name="d_inf_fwd")
    cc, cr = _fox_cum(f_d, d_bf, nb, seq, name="d_cum_fwd")
    cr = jnp.pad(cr.reshape(nb, PAIRS, 2, seq), ((0, 0), (0, 0), (0, 6), (0, 0)))
    o_d, lse_d = _fox_fwd(p_d, cc, cr, nb, seq, name="d_attn_fwd")
    y_d, yt_d = _gate(o_d, p_d, 3, name="d_gate_fwd")
    x4 = _matmul(y_d, w["d_w_out"], name="d_out_fwd", residual=x3_)

    f_gain = w["final_norm"].reshape(1, d)
    dx, g_fn, loss_row = _loss_head(x4, f_gain, tgt, name="loss_head")
    g["final_norm"] = g_fn

    g["d_w_out"] = _row_shards(_matmul(yt_d, dx, name="d_out_dw"))
    dy = _matmul(dx, w["d_w_out"], name="d_out_dy", mode="nt")
    do_d, dg_d = _gate_bwd(dy, o_d, p_d, 3, name="d_gate_bwd")
    dq, dk, dv, dcr, dcc = _fox_bwd(p_d, do_d, o_d, lse_d, cc, cr, nb, seq, name="d_attn_bwd")
    dcr = jnp.pad(dcr[:, :, :2, :].reshape(nb, HEADS, seq), ((0, 0), (0, BLK - HEADS), (0, 0)))
    df, dbf = _fox_cum_bwd(dcr, dcc, f_d, d_bf, nb, seq, name="d_cum_bwd")
    g["d_b_f"] = dbf[:, :HEADS]
    parts = [dq, dk, dv, dg_d]
    dws = [_matmul(ht_d, pt, name=f"d_in_dw{n}") for n, pt in enumerate(parts)]
    dwf = _matmul(ht_d, df, name="d_inf_dw")
    g["d_w_in"] = jnp.concatenate(dws + [dwf[:, :HEADS]], axis=1).reshape(d, N_CHIPS, -1).transpose(1, 0, 2)
    dh = _matmul(df, d_wf, name="d_inf_dh", mode="nt")
    for n, pt in enumerate(parts):
        dh = _matmul(pt, d_wmain[:, n * D_MODEL:(n + 1) * D_MODEL], name=f"d_in_dh{n}", mode="nt", residual=dh)
    dx, g["d_norm"] = _rmsnorm_bwd(dh, x3_, d_gain, dx, name="d_norm_bwd")

    g["c_w_out"] = _row_shards(_matmul(yt_c, dx, name="c_out_dw"))
    dy = _matmul(dx, w["c_w_out"], name="c_out_dy", mode="nt")
    dy1, dgate, g["c_ln_g"], g["c_ln_b"], g["c_conv_b"], g["c_conv_w"] = _conv_bwd_post(
        dy, p_c, c_cw, c_cb, c_lg, c_lb, seq, name="c_conv_bwd_post")
    dp = _conv_bwd_pre(dy1, dgate, p_c, c_cw, seq, name="c_conv_bwd_pre")
    g["c_w_in"] = _matmul(ht_c, dp, name="c_in_dw", out_shards=N_CHIPS)
    dh = _matmul(dp, w["c_w_in"], name="c_in_dh", mode="nt")
    dx, g["c_norm"] = _rmsnorm_bwd(dh, x2, c_gain, dx, name="c_norm_bwd")

    g["b_w_out"] = _row_shards(_matmul(yt_b, dx, name="b_out_dw"))
    dy = _matmul(dx, w["b_w_out"], name="b_out_dy", mode="nt")
    dp, g["b_v_ln_g"], g["b_v_ln_b"], g["b_w_s"], dbst = _gmlp_bwd(dy, p_b, b_lg, b_lb, b_ws, b_bst, name="b_mix_bwd")
    g["b_b_s"] = dbst.T
    g["b_w_in"] = _matmul(ht_b, dp, name="b_in_dw", out_shards=N_CHIPS)
    dh = _matmul(dp, w["b_w_in"], name="b_in_dh", mode="nt")
    dx, g["b_norm"] = _rmsnorm_bwd(dh, x1, b_gain, dx, name="b_norm_bwd")

    g["a_w_out"] = _row_shards(_matmul(yt_a, dx, name="a_out_dw"))
    dy = _matmul(dx, w["a_w_out"], name="a_out_dy", mode="nt")
    do_a, dg_a = _gate_bwd(dy, o_a, p_a, 3, name="a_gate_bwd")
    dq, dk, dv = _sb_bwd(p_a, do_a, tot_a, nb, seq, name="a_attn_bwd")
    parts = [dq, dk, dv, dg_a]
    g["a_w_in"] = jnp.stack([_matmul(ht_a, pt, name=f"a_in_dw{n}") for n, pt in enumerate(parts)])
    dh = None
    for n, pt in enumerate(parts):
        dh = _matmul(pt, w["a_w_in"][n], name=f"a_in_dh{n}", mode="nt", residual=dh)
    dx, g["a_norm"] = _rmsnorm_bwd(dh, x0, a_gain, dx, name="a_norm_bwd")

    return loss_row[0, 0], dx.reshape(nb, seq, d), g


_HBM = pl.BlockSpec(memory_space=pltpu.HBM)


def _place():
    return lax.axis_index("x"), lax.axis_index("y"), lax.axis_index("c")


def _other_chips(x, y):
    return [(1 - x, y), (x, 1 - y), (1 - x, 1 - y)]


def _allgather_chips(ss, *, name):
    n_ops = len(ss)

    def body(*refs):
        s_refs, o_refs, (send_sems, recv_sems) = refs[:n_ops], refs[n_ops:2 * n_ops], refs[2 * n_ops:]
        x, y, c = _place()
        me = 2 * x + y
        chips = _other_chips(x, y)

        def copy(i, kk, src, dst, to):
            return pltpu.make_async_remote_copy(src_ref=src, dst_ref=dst, send_sem=send_sems.at[6 * i + kk],
                                                recv_sem=recv_sems.at[6 * i + kk], device_id=to, device_id_type=MESH)

        def half(i, j, hc):
            h = s_refs[i].shape[0] // 2
            return o_refs[i].at[j, pl.ds(hc * h, h), :]

        first = [copy(i, kk, s_refs[i].at[pl.ds(c * (s_refs[i].shape[0] // 2), s_refs[i].shape[0] // 2), :], half(i, me, c),
                      (cx, cy, c)) for kk, (cx, cy) in enumerate(chips) for i in range(n_ops)]
        for cp in first:
            cp.start()
        passed = []
        for kk, (cx, cy) in enumerate(chips):
            for i in range(n_ops):
                blk = half(i, 2 * cx + cy, c)
                copy(i, kk, blk, blk, (cx, cy, c)).wait_recv()
                fwd = copy(i, 3 + kk, blk, blk, (x, y, 1 - c))
                fwd.start()
                passed.append(fwd)
        for kk, (cx, cy) in enumerate(chips):
            for i in range(n_ops):
                blk = half(i, 2 * cx + cy, 1 - c)
                copy(i, 3 + kk, blk, blk, (x, y, 1 - c)).wait_recv()
        for cp in first + passed:
            cp.wait_send()

    for s in ss:
        assert s.shape[0] % 32 == 0, s.shape
    return pl.pallas_call(
        body, name=name, in_specs=[_HBM] * n_ops, out_specs=[_HBM] * n_ops,
        out_shape=[jax.ShapeDtypeStruct((N_CHIPS,) + s.shape, s.dtype) for s in ss],
        scratch_shapes=[pltpu.SemaphoreType.DMA((6 * n_ops,)), pltpu.SemaphoreType.DMA((6 * n_ops,))],
    )(*ss)


def _own_block(gathered, s):
    me = 2 * lax.axis_index("x") + lax.axis_index("y")
    return lax.dynamic_update_slice(gathered, s[None], (me,) + (0,) * s.ndim)


def _dma_sems(n):
    return [pltpu.SemaphoreType.DMA((n,)), pltpu.SemaphoreType.DMA((n,))]


def _swap_halves(gps, *, name):
    n_ops = len(gps)

    def body(*refs):
        g_refs, o_refs, (send_sems, recv_sems) = refs[:n_ops], refs[n_ops:2 * n_ops], refs[2 * n_ops:]
        x, y, c = _place()
        cps = []
        for i, (g_ref, o_ref) in enumerate(zip(g_refs, o_refs)):
            h = g_ref.shape[1] // 2
            cps.append(pltpu.make_async_remote_copy(
                src_ref=g_ref.at[:, pl.ds((1 - c) * h, h), :], dst_ref=o_ref, send_sem=send_sems.at[i], recv_sem=recv_sems.at[i],
                device_id=(x, y, 1 - c), device_id_type=MESH))
        for cp in cps:
            cp.start()
        for cp in cps:
            cp.wait()

    return pl.pallas_call(
        body, name=name, in_specs=[_HBM] * n_ops, out_specs=[_HBM] * n_ops,
        out_shape=[jax.ShapeDtypeStruct((g.shape[0], g.shape[1] // 2, g.shape[2]), g.dtype) for g in gps],
        scratch_shapes=_dma_sems(n_ops),
    )(*gps)


def _scatter_chips(hps, *, name):
    n_ops = len(hps)

    def body(*refs):
        h_refs, o_refs, (send_sems, recv_sems) = refs[:n_ops], refs[n_ops:2 * n_ops], refs[2 * n_ops:]
        x, y, c = _place()
        cps = [pltpu.make_async_remote_copy(src_ref=h_ref.at[2 * cx + cy], dst_ref=o_ref.at[kk], send_sem=send_sems.at[3 * i + kk],
                                            recv_sem=recv_sems.at[3 * i + kk], device_id=(cx, cy, c), device_id_type=MESH)
               for i, (h_ref, o_ref) in enumerate(zip(h_refs, o_refs)) for kk, (cx, cy) in enumerate(_other_chips(x, y))]
        for cp in cps:
            cp.start()
        for cp in cps:
            cp.wait()

    return pl.pallas_call(
        body, name=name, in_specs=[_HBM] * n_ops, out_specs=[_HBM] * n_ops,
        out_shape=[jax.ShapeDtypeStruct((3,) + hp.shape[1:], hp.dtype) for hp in hps],
        scratch_shapes=_dma_sems(3 * n_ops),
    )(*hps)


def _join_halves(fs, *, name):
    n_ops = len(fs)

    def body(*refs):
        f_refs, o_refs, (send_sems, recv_sems) = refs[:n_ops], refs[n_ops:2 * n_ops], refs[2 * n_ops:]
        x, y, c = _place()
        cps = [pltpu.make_async_remote_copy(src_ref=f_ref, dst_ref=o_ref, send_sem=send_sems.at[i], recv_sem=recv_sems.at[i],
                                            device_id=(x, y, 1 - c), device_id_type=MESH)
               for i, (f_ref, o_ref) in enumerate(zip(f_refs, o_refs))]
        for cp in cps:
            cp.start()
        for cp in cps:
            cp.wait()

    theirs = pl.pallas_call(
        body, name=name, in_specs=[_HBM] * n_ops, out_specs=[_HBM] * n_ops,
        out_shape=[jax.ShapeDtypeStruct(f.shape, f.dtype) for f in fs], scratch_shapes=_dma_sems(n_ops),
    )(*fs)
    south = lax.axis_index("c") == 0
    return [jnp.concatenate([jnp.where(south, f, t), jnp.where(south, t, f)], axis=0) for f, t in zip(fs, theirs)]


def _add_halves(gp, ra, wire_dtype, *, name, bm=256):
    n, r, c_ = gp.shape
    h = r // 2
    bm = _tile(h, bm)
    per = h // bm
    c = lax.axis_index("c").astype(jnp.int32).reshape(1)

    def body(c_ref, g_ref, ra_ref, o_ref, ow_ref):
        s = g_ref[...] + ra_ref[...]
        o_ref[...] = s
        ow_ref[...] = s.astype(wire_dtype)

    mine = pl.BlockSpec((None, bm, c_), lambda j, i, cr: (j, i, 0))
    return pl.pallas_call(
        body, name=name,
        grid_spec=pltpu.PrefetchScalarGridSpec(
            num_scalar_prefetch=1, grid=(n, per),
            in_specs=[pl.BlockSpec((None, bm, c_), lambda j, i, cr: (j, cr[0] * per + i, 0)), mine],
            out_specs=[mine, mine]),
        out_shape=[jax.ShapeDtypeStruct((n, h, c_), F32), jax.ShapeDtypeStruct((n, h, c_), wire_dtype)],
        compiler_params=pltpu.CompilerParams(dimension_semantics=("parallel", "parallel")),
    )(c, gp, ra)


def _add_chips(hp, rb, *, name, bm=256):
    n, h, c_ = hp.shape
    bm = _tile(h, bm)
    me = (2 * lax.axis_index("x") + lax.axis_index("y")).astype(jnp.int32).reshape(1)

    def body(me_ref, h_ref, rb_ref, o_ref):
        o_ref[...] = ((h_ref[...] + rb_ref[0].astype(F32)) + rb_ref[1].astype(F32)) + rb_ref[2].astype(F32)

    return pl.pallas_call(
        body, name=name,
        grid_spec=pltpu.PrefetchScalarGridSpec(
            num_scalar_prefetch=1, grid=(h // bm,),
            in_specs=[pl.BlockSpec((None, bm, c_), lambda i, mr: (mr[0], i, 0)),
                      pl.BlockSpec((3, bm, c_), lambda i, mr: (0, i, 0))],
            out_specs=pl.BlockSpec((bm, c_), lambda i, mr: (i, 0))),
        out_shape=jax.ShapeDtypeStruct((h, c_), F32),
        compiler_params=pltpu.CompilerParams(dimension_semantics=("parallel",)),
    )(me, hp, rb)


def _reduce_scatter(gps, wire_dtypes, *, tag):
    ras = _swap_halves(gps, name=f"{tag}_swap_halves")
    hps = [_add_halves(gp, ra, wd, name=f"{tag}_add_halves{i}") for i, (gp, ra, wd) in enumerate(zip(gps, ras, wire_dtypes))]
    rbs = _scatter_chips([hw for _, hw in hps], name=f"{tag}_scatter_chips")
    fs = [_add_chips(hf, rb, name=f"{tag}_add_chips{i}") for i, ((hf, _), rb) in enumerate(zip(hps, rbs))]
    return _join_halves(fs, name=f"{tag}_join_halves")


def _adamw(w, g, m, v, *, name):
    r, c_ = w.shape
    bm = r
    for cand in (512, 256, 128, 64, 32, 16, 8):
        if r % cand == 0:
            bm = cand
            break
    c1 = 1.0 - ADAM_B1 ** ADAM_STEP
    c2 = 1.0 - ADAM_B2 ** ADAM_STEP

    def body(w_ref, g_ref, m_ref, v_ref, d_ref, nm_ref, nv_ref):
        g_ = g_ref[...]
        m_ = ADAM_B1 * m_ref[...] + (1.0 - ADAM_B1) * g_
        v_ = ADAM_B2 * v_ref[...] + (1.0 - ADAM_B2) * (g_ * g_)
        d_ref[...] = -ADAM_LR * ((m_ / c1) / (jnp.sqrt(v_ / c2) + ADAM_EPS) + ADAM_WD * w_ref[...])
        nm_ref[...] = m_
        nv_ref[...] = v_

    spec = pl.BlockSpec((bm, c_), lambda i: (i, 0))
    return pl.pallas_call(
        body, name=name, grid=(r // bm,), in_specs=[spec] * 4, out_specs=[spec] * 3,
        out_shape=[jax.ShapeDtypeStruct((r, c_), F32)] * 3,
        compiler_params=pltpu.CompilerParams(dimension_semantics=("parallel",)),
    )(w, g, m, v)


_WEIGHTS = ["a_norm", "a_w_in", "a_w_out", "b_norm", "b_w_in", "b_v_ln_g", "b_v_ln_b", "b_w_s", "b_b_s", "b_w_out",
            "c_norm", "c_w_in", "c_conv_w", "c_conv_b", "c_ln_g", "c_ln_b", "c_w_out", "d_norm", "d_w_in", "d_b_f",
            "d_w_out", "final_norm"]
_SHARD_AXIS = {"a_norm": None, "a_w_in": 2, "a_w_out": 1, "b_norm": 1, "b_w_in": 2, "b_v_ln_g": 1, "b_v_ln_b": 1, "b_w_s": None,
               "b_b_s": None, "b_w_out": 1, "c_norm": 1, "c_w_in": 2, "c_conv_w": 2, "c_conv_b": 1, "c_ln_g": 1, "c_ln_b": 1,
               "c_w_out": 1, "d_norm": 1, "d_w_in": 2, "d_b_f": None, "d_w_out": 1, "final_norm": None}
_BIG = ["a_w_in", "a_w_out", "b_w_in", "b_w_out", "c_w_in", "c_w_out", "d_w_in", "d_w_out"]
_SMALL_SHARDED = [n for n in _WEIGHTS if _SHARD_AXIS[n] is not None and n not in _BIG]
_REPLICATED = [n for n in _WEIGHTS if _SHARD_AXIS[n] is None]
_ROW_ALIGN = 32
_ROW_ALIGN_SUMMED = 128


def _pack(pieces, dtype, align=_ROW_ALIGN):
    flat = jnp.concatenate([p.reshape(-1).astype(dtype) for p in pieces])
    unit = align * PACK_C
    total = -(-flat.shape[0] // unit) * unit
    return jnp.pad(flat, (0, total - flat.shape[0])).reshape(total // PACK_C, PACK_C)


def _unpack(flat, shapes):
    out, off = [], 0
    for s in shapes:
        n = math.prod(s)
        out.append(flat[off:off + n].reshape(s))
        off += n
    return out


def _full_shape(local_shape, axis):
    s = list(local_shape)
    if axis is not None:
        s[axis] *= N_CHIPS
    return tuple(s)


def _gather_weights(local):
    full = {n: local[n][0] if n != "final_norm" else local[n] for n in _REPLICATED}
    mine = [local[n][0].astype(BF16) for n in _BIG] + [_pack([local[n] for n in _SMALL_SHARDED], F32)]
    got = [_own_block(gt, s) for gt, s in zip(_allgather_chips(mine, name="gather_weights"), mine)]
    for n, gt in zip(_BIG, got):
        if _SHARD_AXIS[n] == 1:
            full[n] = gt.reshape(-1, gt.shape[-1])
        elif n == "d_w_in":
            full[n] = gt.transpose(1, 0, 2).reshape(gt.shape[1], -1)
        else:
            full[n] = gt
    small = got[-1].reshape(N_CHIPS, -1)
    shards = [_unpack(small[j], [local[n].shape[1:] for n in _SMALL_SHARDED]) for j in range(N_CHIPS)]
    for i, n in enumerate(_SMALL_SHARDED):
        full[n] = jnp.concatenate([shards[j][i] for j in range(N_CHIPS)], axis=_SHARD_AXIS[n] - 1)
    return full


def _repl_piece_len(local):
    total = sum(math.prod(local[n].shape) for n in _REPLICATED)
    return -(-total // N_CHIPS)


def _reduce_grads(g, local):
    rep_flat = jnp.concatenate([g[n].reshape(-1) for n in _REPLICATED])
    piece = _repl_piece_len(local)
    rep_flat = jnp.pad(rep_flat, (0, N_CHIPS * piece - rep_flat.shape[0]))

    def shard(n, j):
        full = g[n].reshape(_full_shape(local[n].shape, _SHARD_AXIS[n]))
        width = local[n].shape[_SHARD_AXIS[n]]
        return lax.slice_in_dim(full, j * width, (j + 1) * width, axis=_SHARD_AXIS[n])

    small = jnp.stack([_pack([shard(n, j) for n in _SMALL_SHARDED] + [rep_flat[j * piece:(j + 1) * piece]], F32)
                       for j in range(N_CHIPS)])
    summed = _reduce_scatter([g[n] for n in _BIG] + [small], [BF16] * len(_BIG) + [F32], tag="grads")
    red = {n: s.reshape(local[n].shape) for n, s in zip(_BIG, summed)}
    out = _unpack(summed[-1].reshape(-1), [local[n].shape for n in _SMALL_SHARDED] + [(piece,)])
    red.update(zip(_SMALL_SHARDED, out[:-1]))
    rep_mine = _pack([out[-1]], F32)
    rep = _own_block(_allgather_chips([rep_mine], name="gather_replicated_grads")[0], rep_mine)
    rep = rep.reshape(N_CHIPS, -1)[:, :piece].reshape(-1)
    for n, val in zip(_REPLICATED, _unpack(rep, [local[n].shape for n in _REPLICATED])):
        red[n] = val
    return red


def _update(local, grads, m, v):
    delta, new_m, new_v = {}, {}, {}
    for n in _BIG:
        shp = local[n].shape
        two = (shp[-2], shp[-1])
        res = _adamw(local[n].reshape(two), grads[n].reshape(two), m[n].reshape(two), v[n].reshape(two), name=f"adamw_{n}")
        delta[n], new_m[n], new_v[n] = [r.reshape(shp) for r in res]
    small = [n for n in _WEIGHTS if n not in _BIG]
    shapes = [local[n].shape for n in small]
    packed = [_pack([src[n] for n in small], F32) for src in (local, grads, m, v)]
    res = _adamw(*packed, name="adamw_small")
    for dst, r in zip((delta, new_m, new_v), res):
        for n, val in zip(small, _unpack(r.reshape(-1), shapes)):
            dst[n] = val
    return delta, new_m, new_v


def kernel(x, a_norm, a_w_in, a_w_out, b_norm, b_w_in, b_v_ln_g, b_v_ln_b, b_w_s, b_b_s, b_w_out, c_norm, c_w_in, c_conv_w, c_conv_b, c_ln_g, c_ln_b, c_w_out, d_norm, d_w_in, d_b_f, d_w_out, final_norm, loss_target, m_a_norm, m_a_w_in, m_a_w_out, m_b_norm, m_b_w_in, m_b_v_ln_g, m_b_v_ln_b, m_b_w_s, m_b_b_s, m_b_w_out, m_c_norm, m_c_w_in, m_c_conv_w, m_c_conv_b, m_c_ln_g, m_c_ln_b, m_c_w_out, m_d_norm, m_d_w_in, m_d_b_f, m_d_w_out, m_final_norm, v_a_norm, v_a_w_in, v_a_w_out, v_b_norm, v_b_w_in, v_b_v_ln_g, v_b_v_ln_b, v_b_w_s, v_b_b_s, v_b_w_out, v_c_norm, v_c_w_in, v_c_conv_w, v_c_conv_b, v_c_ln_g, v_c_ln_b, v_c_w_out, v_d_norm, v_d_w_in, v_d_b_f, v_d_w_out, v_final_norm):
    local = dict(zip(_WEIGHTS, (a_norm, a_w_in, a_w_out, b_norm, b_w_in, b_v_ln_g, b_v_ln_b, b_w_s, b_b_s, b_w_out, c_norm, c_w_in,
                                c_conv_w, c_conv_b, c_ln_g, c_ln_b, c_w_out, d_norm, d_w_in, d_b_f, d_w_out, final_norm)))
    m = dict(zip(_WEIGHTS, (m_a_norm, m_a_w_in, m_a_w_out, m_b_norm, m_b_w_in, m_b_v_ln_g, m_b_v_ln_b, m_b_w_s, m_b_b_s, m_b_w_out,
                            m_c_norm, m_c_w_in, m_c_conv_w, m_c_conv_b, m_c_ln_g, m_c_ln_b, m_c_w_out, m_d_norm, m_d_w_in, m_d_b_f,
                            m_d_w_out, m_final_norm)))
    v = dict(zip(_WEIGHTS, (v_a_norm, v_a_w_in, v_a_w_out, v_b_norm, v_b_w_in, v_b_v_ln_g, v_b_v_ln_b, v_b_w_s, v_b_b_s, v_b_w_out,
                            v_c_norm, v_c_w_in, v_c_conv_w, v_c_conv_b, v_c_ln_g, v_c_ln_b, v_c_w_out, v_d_norm, v_d_w_in, v_d_b_f,
                            v_d_w_out, v_final_norm)))
    loss_part, grad_x, g = _local_step(x, loss_target, _gather_weights(local))
    loss = lax.psum(loss_part, ("x", "y", "c"))
    grads = _reduce_grads(g, local)
    delta, new_m, new_v = _update(local, grads, m, v)
    return (loss, grad_x, *[grads[n] for n in _WEIGHTS], *[delta[n] for n in _WEIGHTS],
            *[new_m[n] for n in _WEIGHTS], *[new_v[n] for n in _WEIGHTS])
```

```python
import functools
import math

import jax
import jax.numpy as jnp
from jax import lax
from jax.experimental import pallas as pl
from jax.experimental.pallas import tpu as pltpu

F32, BF16 = jnp.float32, jnp.bfloat16
MESH = pl.DeviceIdType.MESH

D_MODEL = 1024
HEADS = 16
HEAD_DIM = 64
BLK = 128
PAIRS = HEADS // 2
GM_W = 2048
GM_G = 16
CV_W = 2048
CV_K = 31
HALO = 32
EPS = 1e-6
N_CHIPS = 4
PACK_C = 1024
ADAM_LR, ADAM_B1, ADAM_B2, ADAM_EPS, ADAM_WD, ADAM_STEP = 0.001, 0.9, 0.999, 1e-08, 0.01, 10

_NT = (((1,), (1,)), ((), ()))
_TN = (((0,), (0,)), ((), ()))
_NN = (((1,), (0,)), ((), ()))


def _dot(a, b, dims=_NN):
    return lax.dot_general(a, b, dims, preferred_element_type=F32)


def _split3(x):
    hi = x.astype(BF16)
    r = x - hi.astype(F32)
    mid = r.astype(BF16)
    lo = (r - mid.astype(F32)).astype(BF16)
    return hi, mid, lo


def _dot3_right(x, m):
    hi, mid, lo = _split3(x)
    return _dot(hi, m) + _dot(mid, m) + _dot(lo, m)


def _dot3_left(m, x):
    hi, mid, lo = _split3(x)
    return _dot(m, hi) + _dot(m, mid) + _dot(m, lo)


def _sigmoid(x):
    return 1.0 / (1.0 + jnp.exp(-x))


def _silu(x):
    return x * _sigmoid(x)


def _dsilu(x):
    s = _sigmoid(x)
    return s * (1.0 + x * (1.0 - s))


_GELU_C = math.sqrt(2.0 / math.pi)
_GELU_A = 0.044715


def _gelu(x):
    return 0.5 * x * (1.0 + jnp.tanh(_GELU_C * (x + _GELU_A * x * x * x)))


def _dgelu(x):
    t = jnp.tanh(_GELU_C * (x + _GELU_A * x * x * x))
    return 0.5 * (1.0 + t) + 0.5 * x * (1.0 - t * t) * _GELU_C * (1.0 + 3.0 * _GELU_A * x * x)


def _log_sigmoid(x):
    return jnp.minimum(x, 0.0) - jnp.log(1.0 + jnp.exp(-jnp.abs(x)))


def _rms_fwd(x, g):
    r = lax.rsqrt(jnp.mean(x * x, axis=-1, keepdims=True) + EPS)
    return x * r * g


def _rms_bwd(dy, x, g):
    r = lax.rsqrt(jnp.mean(x * x, axis=-1, keepdims=True) + EPS)
    xh = x * r
    dxh = dy * g
    dx = r * (dxh - xh * jnp.mean(dxh * xh, axis=-1, keepdims=True))
    return dx, dy * xh


def _ln_stats(x):
    mu = jnp.mean(x, axis=-1, keepdims=True)
    xc = x - mu
    r = lax.rsqrt(jnp.mean(xc * xc, axis=-1, keepdims=True) + EPS)
    return xc * r, r


def _ln_bwd(dy, xh, r, g):
    dxh = dy * g
    return r * (dxh - jnp.mean(dxh, axis=-1, keepdims=True) - xh * jnp.mean(dxh * xh, axis=-1, keepdims=True))


def _colsum(x):
    return jnp.sum(x, axis=0, keepdims=True)


def _tile(n, want):
    for t in range(min(n, want), 7, -1):
        if n % t == 0 and t % 8 == 0:
            return t
    return n


MM_TILE = 1024


def _matmul(a, b, *, name, mode="nn", residual=None, out_shards=1):
    (m, k) = a.shape
    b_shards = b.shape[0] if b.ndim == 3 else 1
    if mode == "nn":
        n = b.shape[-1] * b_shards
        tn, tk = _tile(n // max(b_shards, out_shards), MM_TILE), _tile(k, MM_TILE)
    else:
        n = b.shape[-2]
        tn, tk = _tile(n // out_shards, MM_TILE), _tile(k // b_shards, MM_TILE)
    tm = _tile(m, MM_TILE)
    nk = k // tk
    a_spec = pl.BlockSpec((tm, tk), lambda i, j, kk: (i, kk))
    if mode == "nn":
        dims = _NN
        if b_shards == 1:
            b_spec = pl.BlockSpec((tk, tn), lambda i, j, kk: (kk, j))
        else:
            per_b = n // b_shards // tn
            b_spec = pl.BlockSpec((None, tk, tn), lambda i, j, kk: (j // per_b, kk, j % per_b))
    else:
        dims = _NT
        if b_shards == 1:
            b_spec = pl.BlockSpec((tn, tk), lambda i, j, kk: (j, kk))
        else:
            per_b = k // b_shards // tk
            b_spec = pl.BlockSpec((None, tn, tk), lambda i, j, kk: (kk // per_b, j, kk % per_b))
    if out_shards == 1:
        o_spec = pl.BlockSpec((tm, tn), lambda i, j, kk: (i, j))
        o_shape = (m, n)
    else:
        per_o = n // out_shards // tn
        o_spec = pl.BlockSpec((None, tm, tn), lambda i, j, kk: (j // per_o, i, j % per_o))
        o_shape = (out_shards, m, n // out_shards)
    has_res = residual is not None

    def body(a_ref, b_ref, *rest):
        o_ref = rest[-1]
        kk = pl.program_id(2)
        part = _dot(a_ref[...].astype(BF16), b_ref[...].astype(BF16), dims)
        if has_res:
            @pl.when(kk == 0)
            def _():
                o_ref[...] = part + rest[0][...]
        else:
            @pl.when(kk == 0)
            def _():
                o_ref[...] = part

        @pl.when(kk > 0)
        def _():
            o_ref[...] += part

    return pl.pallas_call(
        body, name=name, grid=(m // tm, n // tn, nk),
        in_specs=[a_spec, b_spec] + ([o_spec] if has_res else []),
        out_specs=o_spec, out_shape=jax.ShapeDtypeStruct(o_shape, F32),
        compiler_params=pltpu.CompilerParams(dimension_semantics=("parallel", "parallel", "arbitrary")),
    )(a, b, *([residual] if has_res else []))


def _rows(fn, *, name, steps, ins, outs, accs=(), scratch=()):
    ni, no, na = len(ins), len(outs), len(accs)

    def body(*refs):
        in_refs, out_refs = refs[:ni], refs[ni:ni + no]
        acc_refs, scr = refs[ni + no:ni + no + na], refs[ni + no + na:]
        i = pl.program_id(0)

        @pl.when(i == 0)
        def _():
            for r in acc_refs:
                r[...] = jnp.zeros(r.shape, r.dtype)

        fn(i, in_refs, out_refs, acc_refs, scr)

    def full(shape):
        nd = len(shape)
        return pl.BlockSpec(tuple(shape), lambda i: (0,) * nd)

    res = pl.pallas_call(
        body, name=name, grid=(steps,),
        in_specs=[pl.BlockSpec(bs, im) for _, bs, im in ins],
        out_specs=[pl.BlockSpec(bs, im) for _, _, bs, im in outs] + [full(s) for s, _ in accs],
        out_shape=[jax.ShapeDtypeStruct(s, d) for s, d, _, _ in outs] + [jax.ShapeDtypeStruct(s, d) for s, d in accs],
        scratch_shapes=list(scratch),
        compiler_params=pltpu.CompilerParams(dimension_semantics=("arbitrary",)),
    )(*[a for a, _, _ in ins])
    return res


def _rb(arr, bm, cb=0, width=None):
    w = arr.shape[1] if width is None else width
    return (arr, (bm, w), lambda i: (i, cb))


def _const(arr):
    nd = arr.ndim
    return (arr, tuple(arr.shape), lambda i: (0,) * nd)


def _ro(t, w, dtype, bm):
    return ((t, w), dtype, (bm, w), lambda i: (i, 0))


def _rot(t, w, dtype, bm):
    return ((w, t), dtype, (w, bm), lambda i: (0, i))


def _rmsnorm(x, g, *, name, bm=512):
    t, d = x.shape
    bm = _tile(t, bm)

    def fn(i, ins, outs, accs, scr):
        h = _rms_fwd(ins[0][...], ins[1][...])
        outs[0][...] = h.astype(BF16)
        outs[1][...] = h.T.astype(BF16)

    return _rows(fn, name=name, steps=t // bm, ins=[_rb(x, bm), _const(g)], outs=[_ro(t, d, BF16, bm), _rot(t, d, BF16, bm)])


def _rmsnorm_bwd(dh, x, g, dres, *, name, bm=512):
    t, d = x.shape
    bm = _tile(t, bm)

    def fn(i, ins, outs, accs, scr):
        dx, dgrow = _rms_bwd(ins[0][...], ins[1][...], ins[2][...])
        outs[0][...] = ins[3][...] + dx
        accs[0][...] += _colsum(dgrow)

    return _rows(fn, name=name, steps=t // bm, ins=[_rb(dh, bm), _rb(x, bm), _const(g), _rb(dres, bm)],
                 outs=[_ro(t, d, F32, bm)], accs=[((1, d), F32)])


def _gate(o, p, gcb, *, name, bm=512):
    t, w = o.shape
    bm = _tile(t, bm)

    def fn(i, ins, outs, accs, scr):
        y = ins[0][...] * _silu(ins[1][...])
        outs[0][...] = y.astype(BF16)
        outs[1][...] = y.T.astype(BF16)

    return _rows(fn, name=name, steps=t // bm, ins=[_rb(o, bm), _rb(p, bm, gcb, w)],
                 outs=[_ro(t, w, BF16, bm), _rot(t, w, BF16, bm)])


def _gate_bwd(dy, o, p, gcb, *, name, bm=512):
    t, w = o.shape
    bm = _tile(t, bm)

    def fn(i, ins, outs, accs, scr):
        dy_, o_, g_ = ins[0][...], ins[1][...], ins[2][...]
        outs[0][...] = dy_ * _silu(g_)
        outs[1][...] = dy_ * o_ * _dsilu(g_)

    return _rows(fn, name=name, steps=t // bm, ins=[_rb(dy, bm), _rb(o, bm), _rb(p, bm, gcb, w)],
                 outs=[_ro(t, w, F32, bm), _ro(t, w, F32, bm)])


def _loss_head(x, g, tgt, *, name, bm=512):
    t, d = x.shape
    bm = _tile(t, bm)

    def fn(i, ins, outs, accs, scr):
        x_, g_, tg = ins[0][...], ins[1][...], ins[2][...]
        err = _rms_fwd(x_, g_) - tg
        part = 0.5 * jnp.sum(jnp.sum(err * err, axis=-1, keepdims=True), axis=0, keepdims=True) / d
        dx, dgrow = _rms_bwd(err / d, x_, g_)
        outs[0][...] = dx
        accs[0][...] += _colsum(dgrow)
        accs[1][...] += jnp.broadcast_to(part, (1, BLK))

    return _rows(fn, name=name, steps=t // bm, ins=[_rb(x, bm), _const(g), _rb(tgt, bm)],
                 outs=[_ro(t, d, F32, bm)], accs=[((1, d), F32), ((1, BLK), F32)])


def _gmlp_mix_weights(ws_ref, g):
    row = lax.broadcasted_iota(jnp.int32, (BLK, BLK), 0)
    col = lax.broadcasted_iota(jnp.int32, (BLK, BLK), 1)
    tril = col <= row
    return jnp.where(tril, ws_ref[g], 0.0), tril


def _gmlp_fwd(p, ln_g, ln_b, w_s, bs_t, *, name):
    t = p.shape[0]

    def fn(i, ins, outs, accs, scr):
        p_ref, lg, lb, ws_ref, bst = ins
        vn = _ln_stats(_gelu(p_ref[:, GM_W:2 * GM_W]))[0] * lg[...] + lb[...]
        for g in range(GM_G):
            cs = slice(g * BLK, (g + 1) * BLK)
            wt, _ = _gmlp_mix_weights(ws_ref, g)
            s = _dot(wt.astype(BF16), vn[:, cs].astype(BF16)) + bst[:, g:g + 1]
            u = _gelu(p_ref[:, cs])
            gate = p_ref[:, 2 * GM_W + g * BLK:2 * GM_W + (g + 1) * BLK]
            y = u * s * _silu(gate)
            outs[0][:, cs] = y.astype(BF16)
            outs[1][cs, :] = y.T.astype(BF16)

    return _rows(fn, name=name, steps=t // BLK, ins=[_rb(p, BLK), _const(ln_g), _const(ln_b), _const(w_s), _const(bs_t)],
                 outs=[_ro(t, GM_W, BF16, BLK), _rot(t, GM_W, BF16, BLK)])


def _gmlp_bwd(dy, p, ln_g, ln_b, w_s, bs_t, *, name):
    t = p.shape[0]

    def fn(i, ins, outs, accs, scr):
        dy_ref, p_ref, lg, lb, ws_ref, bst = ins
        dp_ref = outs[0]
        dlg, dlb, dws, dbst = accs
        dvn_ref = scr[0]
        v_pre = p_ref[:, GM_W:2 * GM_W]
        xh, r = _ln_stats(_gelu(v_pre))
        vn = xh * lg[...] + lb[...]
        for g in range(GM_G):
            cs = slice(g * BLK, (g + 1) * BLK)
            gs = slice(2 * GM_W + g * BLK, 2 * GM_W + (g + 1) * BLK)
            wt, tril = _gmlp_mix_weights(ws_ref, g)
            vg = vn[:, cs].astype(BF16)
            s = _dot(wt.astype(BF16), vg) + bst[:, g:g + 1]
            u_pre, gate, dyg = p_ref[:, cs], p_ref[:, gs], dy_ref[:, cs]
            u = _gelu(u_pre)
            dos = dyg * _silu(gate)
            dp_ref[:, gs] = dyg * u * s * _dsilu(gate)
            dp_ref[:, cs] = dos * s * _dgelu(u_pre)
            ds = (dos * u).astype(BF16)
            dws[g] += jnp.where(tril, _dot(ds, vg, _NT), 0.0)
            dbst[:, g:g + 1] += jnp.sum(dos * u, axis=1, keepdims=True)
            dvn_ref[:, cs] = _dot(wt.astype(BF16), ds, _TN)
        dvn = dvn_ref[...]
        dlg[...] += _colsum(dvn * xh)
        dlb[...] += _colsum(dvn)
        dp_ref[:, GM_W:2 * GM_W] = _ln_bwd(dvn, xh, r, lg[...]) * _dgelu(v_pre)

    return _rows(fn, name=name, steps=t // BLK,
                 ins=[_rb(dy, BLK), _rb(p, BLK), _const(ln_g), _const(ln_b), _const(w_s), _const(bs_t)],
                 outs=[_ro(t, 3 * GM_W, F32, BLK)],
                 accs=[((1, GM_W), F32), ((1, GM_W), F32), ((GM_G, BLK, BLK), F32), ((BLK, GM_G), F32)],
                 scratch=[pltpu.VMEM((BLK, GM_W), F32)])


CV_BM = 128
CV_RC = 8
SUBLANES = 8
CV_FWD_OFFS = [HALO - (CV_K - 1) + k for k in range(CV_K)]
CV_BWD_OFFS = [CV_K - 1 - k for k in range(CV_K)]


def _conv_halo_prev(p, cb, bm):
    per = bm // HALO
    return (p, (HALO, CV_W), lambda i: (jnp.maximum(i * per - 1, 0), cb))


def _conv_scratch(bm):
    return [pltpu.VMEM((bm + HALO, CV_W), F32), pltpu.VMEM((SUBLANES - 1, bm + HALO - SUBLANES, CV_W), F32),
            pltpu.VMEM((bm, CV_W), F32)]


def _conv_shift_copies(ext_ref, sh_ref):
    rows = sh_ref.shape[1]
    for b in range(1, SUBLANES):
        sh_ref[b - 1] = ext_ref[pl.ds(b, rows), :]


def _conv_window(ext_ref, sh_ref, off, r0, rows):
    b = off % SUBLANES
    src = ext_ref if b == 0 else sh_ref.at[b - 1]
    return src[pl.ds(r0 + (off - b), rows), :]


def _conv_taps(ext_ref, sh_ref, cw_ref, y_ref, offs):
    bm = y_ref.shape[0]

    def chunk(ci, c):
        r0 = pl.multiple_of(ci * CV_RC, CV_RC)
        acc = jnp.zeros((CV_RC, CV_W), F32)
        for k in range(CV_K):
            acc = acc + cw_ref[pl.ds(k * SUBLANES, CV_RC), :] * _conv_window(ext_ref, sh_ref, offs[k], r0, CV_RC)
        y_ref[pl.ds(r0, CV_RC), :] = acc
        return c

    lax.fori_loop(0, bm // CV_RC, chunk, 0)


def _conv_dweights(dy1_ref, ext_ref, sh_ref, dcw_ref):
    bm = dy1_ref.shape[0]
    groups = 4
    for k in range(CV_K):
        def step(ci, acc, off=CV_FWD_OFFS[k]):
            prods = []
            for u in range(groups):
                r0 = pl.multiple_of((ci * groups + u) * CV_RC, CV_RC)
                prods.append(dy1_ref[pl.ds(r0, CV_RC), :] * _conv_window(ext_ref, sh_ref, off, r0, CV_RC))
            return acc + ((prods[0] + prods[1]) + (prods[2] + prods[3]))

        dcw_ref[k:k + 1, :] += _colsum(lax.fori_loop(0, bm // (CV_RC * groups), step, jnp.zeros((CV_RC, CV_W), F32)))


def _conv_fill(i, ext_ref, a_prev, b_prev, a, b, bm, seq):
    keep = jnp.where((i % (seq // bm)) == 0, 0.0, 1.0)
    ext_ref[pl.ds(0, HALO), :] = keep * (a_prev * _sigmoid(b_prev))
    ext_ref[pl.ds(HALO, bm), :] = a * _sigmoid(b)


def _conv_fwd(p, cw, cb, ln_g, ln_b, seq, *, name, bm=CV_BM):
    t = p.shape[0]

    def fn(i, ins, outs, accs, scr):
        a, b, gate, ap, bp = [r[...] for r in ins[:5]]
        cw_ref, cb_, lg, lb = ins[5], ins[6][...], ins[7][...], ins[8][...]
        ext, sh, y = scr
        _conv_fill(i, ext, ap, bp, a, b, bm, seq)
        _conv_shift_copies(ext, sh)
        _conv_taps(ext, sh, cw_ref, y, CV_FWD_OFFS)
        y2 = _ln_stats(y[...] + cb_)[0] * lg + lb
        out = _silu(y2) * _silu(gate)
        outs[0][...] = out.astype(BF16)
        outs[1][...] = out.T.astype(BF16)

    return _rows(fn, name=name, steps=t // bm,
                 ins=[_rb(p, bm, 0, CV_W), _rb(p, bm, 1, CV_W), _rb(p, bm, 2, CV_W),
                      _conv_halo_prev(p, 0, bm), _conv_halo_prev(p, 1, bm),
                      _const(cw), _const(cb), _const(ln_g), _const(ln_b)],
                 outs=[_ro(t, CV_W, BF16, bm), _rot(t, CV_W, BF16, bm)], scratch=_conv_scratch(bm))


def _conv_bwd_post(dy, p, cw, cb, ln_g, ln_b, seq, *, name, bm=CV_BM):
    t = p.shape[0]

    def fn(i, ins, outs, accs, scr):
        dy_, a, b, gate, ap, bp = [r[...] for r in ins[:6]]
        cw_ref, cb_, lg, lb = ins[6], ins[7][...], ins[8][...], ins[9][...]
        dlg, dlb, dcb, dcw = accs
        ext, sh, y = scr
        _conv_fill(i, ext, ap, bp, a, b, bm, seq)
        _conv_shift_copies(ext, sh)
        _conv_taps(ext, sh, cw_ref, y, CV_FWD_OFFS)
        xh, r = _ln_stats(y[...] + cb_)
        y2 = xh * lg + lb
        outs[1][...] = dy_ * _silu(y2) * _dsilu(gate)
        dy2 = dy_ * _silu(gate) * _dsilu(y2)
        dlg[...] += _colsum(dy2 * xh)
        dlb[...] += _colsum(dy2)
        dy1 = _ln_bwd(dy2, xh, r, lg)
        outs[0][...] = dy1
        dcb[...] += _colsum(dy1)
        _conv_dweights(outs[0], ext, sh, dcw)

    return _rows(fn, name=name, steps=t // bm,
                 ins=[_rb(dy, bm), _rb(p, bm, 0, CV_W), _rb(p, bm, 1, CV_W), _rb(p, bm, 2, CV_W),
                      _conv_halo_prev(p, 0, bm), _conv_halo_prev(p, 1, bm),
                      _const(cw), _const(cb), _const(ln_g), _const(ln_b)],
                 outs=[_ro(t, CV_W, F32, bm), _ro(t, CV_W, F32, bm)],
                 accs=[((1, CV_W), F32), ((1, CV_W), F32), ((1, CV_W), F32), ((CV_K, CV_W), F32)],
                 scratch=_conv_scratch(bm))


def _conv_bwd_pre(dy1, dgate, p, cw, seq, *, name, bm=CV_BM):
    t = p.shape[0]
    per = bm // HALO
    last_halo = t // HALO - 1

    def fn(i, ins, outs, accs, scr):
        d1, d1n, dg, a, b = [r[...] for r in ins[:5]]
        ext, sh, y = scr
        keep = jnp.where((i % (seq // bm)) == (seq // bm - 1), 0.0, 1.0)
        ext[pl.ds(0, bm), :] = d1
        ext[pl.ds(bm, HALO), :] = keep * d1n
        _conv_shift_copies(ext, sh)
        _conv_taps(ext, sh, ins[5], y, CV_BWD_OFFS)
        dy0 = y[...]
        sb = _sigmoid(b)
        outs[0][:, 0:CV_W] = dy0 * sb
        outs[0][:, CV_W:2 * CV_W] = dy0 * a * sb * (1.0 - sb)
        outs[0][:, 2 * CV_W:3 * CV_W] = dg

    return _rows(fn, name=name, steps=t // bm,
                 ins=[_rb(dy1, bm), (dy1, (HALO, CV_W), lambda i: (jnp.minimum((i + 1) * per, last_halo), 0)),
                      _rb(dgate, bm), _rb(p, bm, 0, CV_W), _rb(p, bm, 1, CV_W), _const(cw)],
                 outs=[_ro(t, 3 * CV_W, F32, bm)], scratch=_conv_scratch(bm))[0]


def _iotas():
    row = lax.broadcasted_iota(jnp.int32, (BLK, BLK), 0)
    col = lax.broadcasted_iota(jnp.int32, (BLK, BLK), 1)
    return row, col


def _heads(x, head0):
    if head0.shape != x.shape:
        head0 = lax.broadcasted_iota(jnp.int32, x.shape, 1) < HEAD_DIM
    return jnp.where(head0, x, 0.0).astype(BF16), jnp.where(head0, 0.0, x).astype(BF16)


def _pair_spec(seq, off):
    return pl.BlockSpec((seq, BLK), lambda b, hp: (b, off + hp))


def _stat_spec(seq):
    return pl.BlockSpec((None, None, seq, BLK), lambda b, hp: (b, hp, 0, 0))


_ATT_PARAMS = dict(compiler_params=pltpu.CompilerParams(dimension_semantics=("parallel", "parallel")))
_SCALE = 1.0 / math.sqrt(HEAD_DIM)


Q_BLOCK = 256
KEY_BLOCK = 256


def _stack_heads(x, head0, scale=None):
    if scale is not None:
        x = x * scale
    return jnp.concatenate(_heads(x, head0), axis=0)


def _pair_cols(x, head0, fill):
    a = jnp.max(jnp.where(head0, x, fill), axis=1, keepdims=True)
    b = jnp.max(jnp.where(head0, fill, x), axis=1, keepdims=True)
    return jnp.concatenate([a, b], axis=0)


def _causal_mask(t0, s0, tq, kw, inclusive):
    row = lax.broadcasted_iota(jnp.int32, (2 * tq, kw), 0) & (tq - 1)
    col = lax.broadcasted_iota(jnp.int32, (2 * tq, kw), 1)
    return (s0 + col) <= (t0 + row) if inclusive else (s0 + col) < (t0 + row)


def _sub(x, j):
    return x[:, j * BLK:(j + 1) * BLK]


def _block_cumsum(x, tri, ksub):
    hi = x.astype(BF16)
    lo = (x - hi.astype(F32)).astype(BF16)
    cs = _dot(jnp.concatenate([_sub(pt, j) for pt in (hi, lo) for j in range(ksub)], axis=0), tri)
    n = x.shape[0]
    return ([cs[j * n:(j + 1) * n] + cs[(ksub + j) * n:(ksub + j + 1) * n] for j in range(ksub)],
            [jnp.sum(_sub(x, j), axis=1, keepdims=True) for j in range(ksub)])


def _sb_terms(qs, k, mask):
    return _sb_terms_z(_dot(qs, k, _NT), mask)


def _sb_terms_z(z, mask):
    t = jnp.log(1.0 + jnp.exp(-jnp.abs(z)))
    lsz = jnp.minimum(z, 0.0) - t
    lr = lsz - z
    if mask is not None:
        lr = jnp.where(mask, lr, 0.0)
    return lsz, lr


def _sb_fwd(p, nb, seq, *, name):
    tq = min(Q_BLOCK, seq)
    nq = seq // tq
    kw = min(KEY_BLOCK, seq)
    ksub = kw // BLK

    def body(q_ref, k_ref, v_ref, o_ref, tot_ref):
        row, col = _iotas()
        colq = lax.broadcasted_iota(jnp.int32, (tq, BLK), 1)
        head0 = colq < HEAD_DIM
        upper = (row > col).astype(BF16)

        def qblock(qb, c):
            t0 = pl.multiple_of(qb * tq, tq)
            qs = _stack_heads(q_ref[pl.ds(t0, tq), :], head0, _SCALE)
            diag = (t0 + tq - 1) // kw

            def kblock(kb, carry, masked):
                acc, run = carry
                s0 = pl.multiple_of(kb * kw, kw)
                k = k_ref[pl.ds(s0, kw), :].astype(BF16)
                v0, v1 = _heads(v_ref[pl.ds(s0, kw), :], head0)
                mask = _causal_mask(t0, s0, tq, kw, False)[:tq] if masked else None
                zs = [_dot(qs[h * tq:(h + 1) * tq], k, _NT) for h in range(2)]
                terms = []
                for h in range(2):
                    lsz, lr = _sb_terms_z(zs[h], mask)
                    terms.append((lsz,) + _block_cumsum(lr, upper, ksub))
                runs = []
                for h, vh in enumerate((v0, v1)):
                    lsz, after, total = terms[h]
                    r = run[h]
                    ws = [None] * ksub
                    for j in reversed(range(ksub)):
                        w = jnp.exp(_sub(lsz, j) + after[j] + r)
                        if masked:
                            w = jnp.where(_sub(mask, j), w, 0.0)
                        ws[j] = w.astype(BF16)
                        r = r + total[j]
                    acc = acc + _dot(jnp.concatenate(ws, axis=1), vh)
                    runs.append(r)
                return acc, tuple(runs)

            zc = jnp.zeros((tq, 1), F32)
            carry = kblock(diag, (jnp.zeros((tq, BLK), F32), (zc, zc)), True)
            acc, run = lax.fori_loop(0, diag, lambda it, cr: kblock(diag - 1 - it, cr, False), carry)
            o_ref[pl.ds(t0, tq), :] = acc
            tot_ref[pl.ds(t0, tq), :] = jnp.where(head0, run[0], run[1])
            return c

        lax.fori_loop(0, nq, qblock, 0)

    return pl.pallas_call(
        body, name=name, grid=(nb, PAIRS),
        in_specs=[_pair_spec(seq, 0), _pair_spec(seq, PAIRS), _pair_spec(seq, 2 * PAIRS)],
        out_specs=[_pair_spec(seq, 0), _stat_spec(seq)],
        out_shape=[jax.ShapeDtypeStruct((nb * seq, D_MODEL), F32), jax.ShapeDtypeStruct((nb, PAIRS, seq, BLK), F32)],
        **_ATT_PARAMS,
    )(p, p, p)


def _sb_bwd(p, do, tot, nb, seq, *, name):
    tq = min(Q_BLOCK, seq)
    nq = seq // tq
    kw = min(KEY_BLOCK, seq)
    ksub = kw // BLK

    def body(q_ref, k_ref, v_ref, do_ref, tot_ref, dq_ref, dk_ref, dv_ref):
        row, col = _iotas()
        colq = lax.broadcasted_iota(jnp.int32, (tq, BLK), 1)
        head0 = colq < HEAD_DIM
        lower_incl = (row <= col).astype(BF16)
        lower_strict = (row < col).astype(BF16)
        dk_ref[...] = jnp.zeros(dk_ref.shape, F32)
        dv_ref[...] = jnp.zeros(dv_ref.shape, F32)

        def qblock(qb, c):
            t0 = pl.multiple_of(qb * tq, tq)
            qs = _stack_heads(q_ref[pl.ds(t0, tq), :], head0, _SCALE)
            dos = _stack_heads(do_ref[pl.ds(t0, tq), :], head0)
            tot = tot_ref[pl.ds(t0, tq), :]
            swapped = pltpu.roll(tot, HEAD_DIM, 1)
            tts = (jnp.where(head0, tot, swapped), jnp.where(head0, swapped, tot))
            diag = (t0 + tq - 1) // kw

            def kblock(kb, carry, masked):
                dq, pfs, efs = carry
                s0 = pl.multiple_of(kb * kw, kw)
                kf = k_ref[pl.ds(s0, kw), :]
                k = kf.astype(BF16)
                khs = _heads(kf, head0)
                v = v_ref[pl.ds(s0, kw), :].astype(BF16)
                mask = _causal_mask(t0, s0, tq, kw, False)[:tq] if masked else None
                zs = [_dot(qs[h * tq:(h + 1) * tq], k, _NT) for h in range(2)]
                dws = [_dot(dos[h * tq:(h + 1) * tq], v, _NT) for h in range(2)]
                first = []
                for h in range(2):
                    lsz, lr = _sb_terms_z(zs[h], None)
                    lrm = jnp.where(mask, lr, 0.0) if masked else lr
                    first.append((lsz, lr) + _block_cumsum(lrm, lower_incl, ksub))
                second, pfs_out = [], []
                for h in range(2):
                    lsz, lr, incl, total = first[h]
                    pf = pfs[h]
                    ws, ews = [], []
                    for j in range(ksub):
                        w = jnp.exp(_sub(lsz, j) + (tts[h] - pf - incl[j]))
                        if masked:
                            w = jnp.where(_sub(mask, j), w, 0.0)
                        pf = pf + total[j]
                        ws.append(w.astype(BF16))
                        ews.append(_sub(dws[h], j) * w)
                    pfs_out.append(pf)
                    second.append((ws, ews) + _block_cumsum(jnp.concatenate(ews, axis=1), lower_strict, ksub))
                dz_h, efs_out = [], []
                for h in range(2):
                    lsz, lr = first[h][:2]
                    ws, ews, before, etotal = second[h]
                    ef = efs[h]
                    dzs = []
                    for j in range(ksub):
                        dz = ews[j] * jnp.exp(_sub(lr, j)) - (ef + before[j]) * jnp.exp(_sub(lsz, j))
                        ef = ef + etotal[j]
                        if masked:
                            dz = jnp.where(_sub(mask, j), dz, 0.0)
                        dzs.append(dz.astype(BF16))
                    efs_out.append(ef)
                    dz_h.append(jnp.concatenate(dzs, axis=1))
                    dq = dq + _dot(dz_h[h], khs[h])
                w = jnp.concatenate([jnp.concatenate(second[h][0], axis=1) for h in range(2)], axis=0)
                dk_ref[pl.ds(s0, kw), :] += _dot(jnp.concatenate(dz_h, axis=0), qs, _TN)
                dv_ref[pl.ds(s0, kw), :] += _dot(w, dos, _TN)
                return dq, tuple(pfs_out), tuple(efs_out)

            zc = jnp.zeros((tq, 1), F32)
            carry = lax.fori_loop(0, diag, lambda kb, cr: kblock(kb, cr, False), (jnp.zeros((tq, BLK), F32), (zc, zc), (zc, zc)))
            dq_ref[pl.ds(t0, tq), :] = kblock(diag, carry, True)[0] * _SCALE
            return c

        lax.fori_loop(0, nq, qblock, 0)

    t = nb * seq
    return pl.pallas_call(
        body, name=name, grid=(nb, PAIRS),
        in_specs=[_pair_spec(seq, 0), _pair_spec(seq, PAIRS), _pair_spec(seq, 2 * PAIRS), _pair_spec(seq, 0), _stat_spec(seq)],
        out_specs=[_pair_spec(seq, 0)] * 3,
        out_shape=[jax.ShapeDtypeStruct((t, D_MODEL), F32)] * 3,
        **_ATT_PARAMS,
    )(p, p, p, do, tot)


def _fox_cum(f, bf, nb, seq, *, name):
    def body(f_ref, bf_ref, cc_ref, cr_ref):
        row, col = _iotas()
        lower = (col <= row).astype(BF16)
        carry = jnp.zeros((1, BLK), F32)
        for blk in range(seq // BLK):
            rs = slice(blk * BLK, (blk + 1) * BLK)
            lf = jnp.where(col < HEADS, _log_sigmoid(f_ref[rs, :] + bf_ref[...]), 0.0)
            cc = _dot3_left(lower, lf) + carry
            cc_ref[rs, :] = cc
            cr_ref[:, rs] = cc.T[0:HEADS, :]
            carry = carry + _colsum(lf)

    return pl.pallas_call(
        body, name=name, grid=(nb,),
        in_specs=[pl.BlockSpec((seq, BLK), lambda b: (b, 0)), pl.BlockSpec((1, BLK), lambda b: (0, 0))],
        out_specs=[pl.BlockSpec((seq, BLK), lambda b: (b, 0)), pl.BlockSpec((None, HEADS, seq), lambda b: (b, 0, 0))],
        out_shape=[jax.ShapeDtypeStruct((nb * seq, BLK), F32), jax.ShapeDtypeStruct((nb, HEADS, seq), F32)],
        compiler_params=pltpu.CompilerParams(dimension_semantics=("parallel",)),
    )(f, bf)


def _fox_cum_bwd(dcr, dcc, f, bf, nb, seq, *, name):
    def body(dcr_ref, dcc_ref, f_ref, bf_ref, df_ref, dbf_ref):
        row, col = _iotas()
        upper_incl = (col >= row).astype(BF16)

        @pl.when(pl.program_id(0) == 0)
        def _():
            dbf_ref[...] = jnp.zeros((1, BLK), F32)

        carry = jnp.zeros((1, BLK), F32)
        for blk in reversed(range(seq // BLK)):
            rs = slice(blk * BLK, (blk + 1) * BLK)
            dc = dcr_ref[:, rs].T + dcc_ref[rs, :]
            dlf = _dot3_left(upper_incl, dc) + carry
            carry = carry + _colsum(dc)
            fl = f_ref[rs, :] + bf_ref[...]
            df = jnp.where(col < HEADS, dlf * _sigmoid(-fl), 0.0)
            df_ref[rs, :] = df
            dbf_ref[...] += _colsum(df)

    return pl.pallas_call(
        body, name=name, grid=(nb,),
        in_specs=[pl.BlockSpec((None, BLK, seq), lambda b: (b, 0, 0)), pl.BlockSpec((seq, BLK), lambda b: (b, 0)),
                  pl.BlockSpec((seq, BLK), lambda b: (b, 0)), pl.BlockSpec((1, BLK), lambda b: (0, 0))],
        out_specs=[pl.BlockSpec((seq, BLK), lambda b: (b, 0)), pl.BlockSpec((1, BLK), lambda b: (0, 0))],
        out_shape=[jax.ShapeDtypeStruct((nb * seq, BLK), F32), jax.ShapeDtypeStruct((1, BLK), F32)],
        compiler_params=pltpu.CompilerParams(dimension_semantics=("arbitrary",)),
    )(dcr, dcc, f, bf)


def _fox_cum_cols(cc_ref, t0, tq, colq, hp):
    cc = cc_ref[pl.ds(t0, tq), :]
    c0 = jnp.sum(jnp.where(colq == 2 * hp, cc, 0.0), axis=1, keepdims=True)
    c1 = jnp.sum(jnp.where(colq == 2 * hp + 1, cc, 0.0), axis=1, keepdims=True)
    return c0, c1


def _fox_bias(c0, c1, cr_ref, s0, kw):
    return jnp.concatenate([c0 - cr_ref[0:1, pl.ds(s0, kw)], c1 - cr_ref[1:2, pl.ds(s0, kw)]], axis=0)


def _fox_fwd(p, cc, cr, nb, seq, *, name):
    tq = min(Q_BLOCK, seq)
    nq = seq // tq
    kw = min(KEY_BLOCK, seq)
    ksub = kw // BLK

    def body(q_ref, k_ref, v_ref, cc_ref, cr_ref, o_ref, lse_ref):
        hp = pl.program_id(1)
        row, col = _iotas()
        colq = lax.broadcasted_iota(jnp.int32, (tq, BLK), 1)
        head0 = colq < HEAD_DIM

        def qblock(qb, c):
            t0 = pl.multiple_of(qb * tq, tq)
            qs = _stack_heads(q_ref[pl.ds(t0, tq), :], head0, _SCALE)
            c0, c1 = _fox_cum_cols(cc_ref, t0, tq, colq, hp)
            diag = (t0 + tq - 1) // kw

            def kblock(kb, carry, masked):
                accs, ms = carry
                s0 = pl.multiple_of(kb * kw, kw)
                k = k_ref[pl.ds(s0, kw), :].astype(BF16)
                vf = v_ref[pl.ds(s0, kw), :]
                own0 = lax.broadcasted_iota(jnp.int32, vf.shape, 1) < HEAD_DIM
                vs = (jnp.where(own0, vf, 1.0).astype(BF16), jnp.where(own0, 1.0, vf).astype(BF16))
                mask = _causal_mask(t0, s0, tq, kw, True)[:tq] if masked else None
                zs = [_dot(qs[h * tq:(h + 1) * tq], k, _NT) for h in range(2)]
                parts = []
                for h, ch in enumerate((c0, c1)):
                    s = zs[h] + (ch - cr_ref[h:h + 1, pl.ds(s0, kw)])
                    if masked:
                        s = jnp.where(mask, s, -jnp.inf)
                    m_new = jnp.maximum(ms[h], jnp.max(s, axis=1, keepdims=True))
                    parts.append((jnp.exp(s - m_new).astype(BF16), jnp.exp(ms[h] - m_new), m_new))
                return (tuple(accs[h] * parts[h][1] + _dot(parts[h][0], vs[h]) for h in range(2)),
                        tuple(parts[h][2] for h in range(2)))

            zeros, ninf = jnp.zeros((tq, BLK), F32), jnp.full((tq, 1), -jnp.inf, F32)
            carry = lax.fori_loop(0, diag, lambda kb, cr: kblock(kb, cr, False), ((zeros, zeros), (ninf, ninf)))
            (acc0, acc1), (m0, m1) = kblock(diag, carry, True)
            l = jnp.where(head0, pltpu.roll(acc0, HEAD_DIM, 1), pltpu.roll(acc1, HEAD_DIM, 1))
            o_ref[pl.ds(t0, tq), :] = jnp.where(head0, acc0, acc1) / l
            lse_ref[pl.ds(t0, tq), :] = jnp.where(head0, m0, m1) + jnp.log(l)
            return c

        lax.fori_loop(0, nq, qblock, 0)

    return pl.pallas_call(
        body, name=name, grid=(nb, PAIRS),
        in_specs=[_pair_spec(seq, 0), _pair_spec(seq, PAIRS), _pair_spec(seq, 2 * PAIRS),
                  pl.BlockSpec((seq, BLK), lambda b, hp: (b, 0)), pl.BlockSpec((None, None, 8, seq), lambda b, hp: (b, hp, 0, 0))],
        out_specs=[_pair_spec(seq, 0), _stat_spec(seq)],
        out_shape=[jax.ShapeDtypeStruct((nb * seq, D_MODEL), F32), jax.ShapeDtypeStruct((nb, PAIRS, seq, BLK), F32)],
        **_ATT_PARAMS,
    )(p, p, p, cc, cr)


def _fox_bwd(p, do, o, lse, cc, cr, nb, seq, *, name):
    tq = min(Q_BLOCK, seq)
    nq = seq // tq
    kw = min(KEY_BLOCK, seq)
    ksub = kw // BLK

    def body(q_ref, k_ref, v_ref, do_ref, o_ref, lse_ref, cc_ref, cr_ref, dq_ref, dk_ref, dv_ref, dcr_ref, dcc_ref):
        hp = pl.program_id(1)
        row, col = _iotas()
        colq = lax.broadcasted_iota(jnp.int32, (tq, BLK), 1)
        head0 = colq < HEAD_DIM
        dk_ref[...] = jnp.zeros(dk_ref.shape, F32)
        dv_ref[...] = jnp.zeros(dv_ref.shape, F32)
        dcr_ref[...] = jnp.zeros(dcr_ref.shape, F32)

        @pl.when(hp == 0)
        def _():
            dcc_ref[...] = jnp.zeros(dcc_ref.shape, F32)

        def qblock(qb, c):
            t0 = pl.multiple_of(qb * tq, tq)
            qs = _stack_heads(q_ref[pl.ds(t0, tq), :], head0, _SCALE)
            dof = do_ref[pl.ds(t0, tq), :]
            dos = _stack_heads(dof, head0)
            prod = dof * o_ref[pl.ds(t0, tq), :]
            dl = jnp.concatenate([jnp.sum(jnp.where(head0, prod, 0.0), axis=1, keepdims=True),
                                  jnp.sum(jnp.where(head0, 0.0, prod), axis=1, keepdims=True)], axis=0)
            lse = _pair_cols(lse_ref[pl.ds(t0, tq), :], head0, -jnp.inf)
            c0, c1 = _fox_cum_cols(cc_ref, t0, tq, colq, hp)
            diag = (t0 + tq - 1) // kw

            def kblock(kb, carry, masked):
                dq, rs = carry
                s0 = pl.multiple_of(kb * kw, kw)
                kf = k_ref[pl.ds(s0, kw), :]
                k = kf.astype(BF16)
                k0, k1 = _heads(kf, head0)
                v = v_ref[pl.ds(s0, kw), :].astype(BF16)
                mask = _causal_mask(t0, s0, tq, kw, True)[:tq] if masked else None
                zs = [_dot(qs[h * tq:(h + 1) * tq], k, _NT) for h in range(2)]
                dps = [_dot(dos[h * tq:(h + 1) * tq], v, _NT) for h in range(2)]
                prs, dss, rss = [], [], []
                for h, (ch, kh) in enumerate(((c0, k0), (c1, k1))):
                    rows = slice(h * tq, (h + 1) * tq)
                    pr = jnp.exp(zs[h] + (ch - cr_ref[h:h + 1, pl.ds(s0, kw)]) - lse[rows])
                    if masked:
                        pr = jnp.where(mask, pr, 0.0)
                    ds = pr * (dps[h] - dl[rows])
                    dcr_ref[h:h + 1, pl.ds(s0, kw)] -= _colsum(ds)
                    rss.append(rs[rows] + jnp.sum(ds, axis=1, keepdims=True))
                    prs.append(pr.astype(BF16))
                    dss.append(ds.astype(BF16))
                    dq = dq + _dot(dss[h], kh)
                dk_ref[pl.ds(s0, kw), :] += _dot(jnp.concatenate(dss, axis=0), qs, _TN)
                dv_ref[pl.ds(s0, kw), :] += _dot(jnp.concatenate(prs, axis=0), dos, _TN)
                return dq, jnp.concatenate(rss, axis=0)

            init = (jnp.zeros((tq, BLK), F32), jnp.zeros((2 * tq, 1), F32))
            carry = lax.fori_loop(0, diag, lambda kb, cr: kblock(kb, cr, False), init)
            dq, rs = kblock(diag, carry, True)
            dq_ref[pl.ds(t0, tq), :] = dq * _SCALE
            dcc_ref[pl.ds(t0, tq), :] += jnp.where(colq == 2 * hp, rs[:tq], 0.0) + jnp.where(colq == 2 * hp + 1, rs[tq:], 0.0)
            return c

        lax.fori_loop(0, nq, qblock, 0)

    t = nb * seq
    return pl.pallas_call(
        body, name=name, grid=(nb, PAIRS),
        in_specs=[_pair_spec(seq, 0), _pair_spec(seq, PAIRS), _pair_spec(seq, 2 * PAIRS), _pair_spec(seq, 0), _pair_spec(seq, 0),
                  _stat_spec(seq), pl.BlockSpec((seq, BLK), lambda b, hp: (b, 0)),
                  pl.BlockSpec((None, None, 8, seq), lambda b, hp: (b, hp, 0, 0))],
        out_specs=[_pair_spec(seq, 0)] * 3 + [pl.BlockSpec((None, None, 8, seq), lambda b, hp: (b, hp, 0, 0)),
                                              pl.BlockSpec((seq, BLK), lambda b, hp: (b, 0))],
        out_shape=[jax.ShapeDtypeStruct((t, D_MODEL), F32)] * 3 + [jax.ShapeDtypeStruct((nb, PAIRS, 8, seq), F32),
                                                                     jax.ShapeDtypeStruct((t, BLK), F32)],
        compiler_params=pltpu.CompilerParams(dimension_semantics=("parallel", "arbitrary")),
    )(p, p, p, do, o, lse, cc, cr)


def _row_shards(x):
    return x.reshape(N_CHIPS, x.shape[0] // N_CHIPS, x.shape[1])


def _local_step(x3, tgt3, w, later=None):
    nb, seq, d = x3.shape
    t = nb * seq
    x0, tgt = x3.reshape(t, d), tgt3.reshape(t, d)
    g = {}

    a_gain = w["a_norm"].reshape(1, d)
    h_a, ht_a = _rmsnorm(x0, a_gain, name="a_norm_fwd")
    p_a = _matmul(h_a, w["a_w_in"], name="a_in_fwd")
    o_a, tot_a = _sb_fwd(p_a, nb, seq, name="a_attn_fwd")
    y_a, yt_a = _gate(o_a, p_a, 3, name="a_gate_fwd")
    x1 = _matmul(y_a, w["a_w_out"], name="a_out_fwd", residual=x0)

    b_gain = w["b_norm"].reshape(1, d)
    b_lg, b_lb = w["b_v_ln_g"].reshape(1, GM_W), w["b_v_ln_b"].reshape(1, GM_W)
    b_ws, b_bst = w["b_w_s"].reshape(GM_G, BLK, BLK), w["b_b_s"].reshape(GM_G, BLK).T
    h_b, ht_b = _rmsnorm(x1, b_gain, name="b_norm_fwd")
    p_b = _matmul(h_b, w["b_w_in"], name="b_in_fwd")
    y_b, yt_b = _gmlp_fwd(p_b, b_lg, b_lb, b_ws, b_bst, name="b_mix_fwd")
    x2 = _matmul(y_b, w["b_w_out"], name="b_out_fwd", residual=x1)

    if later is not None:
        w = {**w, **later(x2)}
    c_gain = w["c_norm"].reshape(1, d)
    c_cw = jnp.repeat(w["c_conv_w"].reshape(CV_K, CV_W), SUBLANES, axis=0)
    c_cb = w["c_conv_b"].reshape(1, CV_W)
    c_lg, c_lb = w["c_ln_g"].reshape(1, CV_W), w["c_ln_b"].reshape(1, CV_W)
    h_c, ht_c = _rmsnorm(x2, c_gain, name="c_norm_fwd")
    p_c = _matmul(h_c, w["c_w_in"], name="c_in_fwd")
    y_c, yt_c = _conv_fwd(p_c, c_cw, c_cb, c_lg, c_lb, seq, name="c_conv_fwd")
    x3_ = _matmul(y_c, w["c_w_out"], name="c_out_fwd", residual=x2)

    d_gain = w["d_norm"].reshape(1, d)
    d_win = w["d_w_in"].reshape(d, 4 * D_MODEL + HEADS)
    d_wmain = d_win[:, :4 * D_MODEL]
    d_wf = jnp.pad(d_win[:, 4 * D_MODEL:], ((0, 0), (0, BLK - HEADS)))
    d_bf = jnp.pad(w["d_b_f"].reshape(1, HEADS), ((0, 0), (0, BLK - HEADS)))
    h_d, ht_d = _rmsnorm(x3_, d_gain, name="d_norm_fwd")
    p_d = _matmul(h_d, d_wmain, name="d_in_fwd")
    f_d = _matmul(h_d, d_wf, name="d_inf_fwd")
    cc, cr = _fox_cum(f_d, d_bf, nb, seq, name="d_cum_fwd")
    cr = jnp.pad(cr.reshape(nb, PAIRS, 2, seq), ((0, 0), (0, 0), (0, 6), (0, 0)))
    o_d, lse_d = _fox_fwd(p_d, cc, cr, nb, seq, name="d_attn_fwd")
    y_d, yt_d = _gate(o_d, p_d, 3, name="d_gate_fwd")
    x4 = _matmul(y_d, w["d_w_out"], name="d_out_fwd", residual=x3_)

    f_gain = w["final_norm"].reshape(1, d)
    dx, g_fn, loss_row = _loss_head(x4, f_gain, tgt, name="loss_head")
    g["final_norm"] = g_fn

    g["d_w_out"] = _row_shards(_matmul(yt_d, dx, name="d_out_dw"))
    dy = _matmul(dx, w["d_w_out"], name="d_out_dy", mode="nt")
    do_d, dg_d = _gate_bwd(dy, o_d, p_d, 3, name="d_gate_bwd")
    dq, dk, dv, dcr, dcc = _fox_bwd(p_d, do_d, o_d, lse_d, cc, cr, nb, seq, name="d_attn_bwd")
    dcr = jnp.pad(dcr[:, :, :2, :].reshape(nb, HEADS, seq), ((0, 0), (0, BLK - HEADS), (0, 0)))
    df, dbf = _fox_cum_bwd(dcr, dcc, f_d, d_bf, nb, seq, name="d_cum_bwd")
    g["d_b_f"] = dbf[:, :HEADS]
    parts = [dq, dk, dv, dg_d]
    dws = [_matmul(ht_d, pt, name=f"d_in_dw{n}") for n, pt in enumerate(parts)]
    dwf = _matmul(ht_d, df, name="d_inf_dw")
    g["d_w_in"] = jnp.concatenate(dws + [dwf[:, :HEADS]], axis=1).reshape(d, N_CHIPS, -1).transpose(1, 0, 2)
    dh = _matmul(df, d_wf, name="d_inf_dh", mode="nt")
    for n, pt in enumerate(parts):
        dh = _matmul(pt, d_wmain[:, n * D_MODEL:(n + 1) * D_MODEL], name=f"d_in_dh{n}", mode="nt", residual=dh)
    dx, g["d_norm"] = _rmsnorm_bwd(dh, x3_, d_gain, dx, name="d_norm_bwd")

    g["c_w_out"] = _row_shards(_matmul(yt_c, dx, name="c_out_dw"))
    dy = _matmul(dx, w["c_w_out"], name="c_out_dy", mode="nt")
    dy1, dgate, g["c_ln_g"], g["c_ln_b"], g["c_conv_b"], g["c_conv_w"] = _conv_bwd_post(
        dy, p_c, c_cw, c_cb, c_lg, c_lb, seq, name="c_conv_bwd_post")
    dp = _conv_bwd_pre(dy1, dgate, p_c, c_cw, seq, name="c_conv_bwd_pre")
    g["c_w_in"] = _matmul(ht_c, dp, name="c_in_dw", out_shards=N_CHIPS)
    dh = _matmul(dp, w["c_w_in"], name="c_in_dh", mode="nt")
    dx, g["c_norm"] = _rmsnorm_bwd(dh, x2, c_gain, dx, name="c_norm_bwd")

    g["b_w_out"] = _row_shards(_matmul(yt_b, dx, name="b_out_dw"))
    dy = _matmul(dx, w["b_w_out"], name="b_out_dy", mode="nt")
    dp, g["b_v_ln_g"], g["b_v_ln_b"], g["b_w_s"], dbst = _gmlp_bwd(dy, p_b, b_lg, b_lb, b_ws, b_bst, name="b_mix_bwd")
    g["b_b_s"] = dbst.T
    g["b_w_in"] = _matmul(ht_b, dp, name="b_in_dw", out_shards=N_CHIPS)
    dh = _matmul(dp, w["b_w_in"], name="b_in_dh", mode="nt")
    dx, g["b_norm"] = _rmsnorm_bwd(dh, x1, b_gain, dx, name="b_norm_bwd")

    g["a_w_out"] = _row_shards(_matmul(yt_a, dx, name="a_out_dw"))
    dy = _matmul(dx, w["a_w_out"], name="a_out_dy", mode="nt")
    do_a, dg_a = _gate_bwd(dy, o_a, p_a, 3, name="a_gate_bwd")
    dq, dk, dv = _sb_bwd(p_a, do_a, tot_a, nb, seq, name="a_attn_bwd")
    parts = [dq, dk, dv, dg_a]
    g["a_w_in"] = jnp.stack([_matmul(ht_a, pt, name=f"a_in_dw{n}") for n, pt in enumerate(parts)])
    dh = None
    for n, pt in enumerate(parts):
        dh = _matmul(pt, w["a_w_in"][n], name=f"a_in_dh{n}", mode="nt", residual=dh)
    dx, g["a_norm"] = _rmsnorm_bwd(dh, x0, a_gain, dx, name="a_norm_bwd")

    return loss_row[0, 0], dx.reshape(nb, seq, d), g


_HBM = pl.BlockSpec(memory_space=pltpu.HBM)


def _place():
    return lax.axis_index("x"), lax.axis_index("y"), lax.axis_index("c")


def _other_chips(x, y):
    return [(1 - x, y), (x, 1 - y), (1 - x, 1 - y)]


def _allgather_chips(ss, *, name):
    n_ops = len(ss)

    def body(*refs):
        s_refs, o_refs, (send_sems, recv_sems) = refs[:n_ops], refs[n_ops:2 * n_ops], refs[2 * n_ops:]
        x, y, c = _place()
        me = 2 * x + y
        chips = _other_chips(x, y)

        def copy(i, kk, src, dst, to):
            return pltpu.make_async_remote_copy(src_ref=src, dst_ref=dst, send_sem=send_sems.at[6 * i + kk],
                                                recv_sem=recv_sems.at[6 * i + kk], device_id=to, device_id_type=MESH)

        def half(i, j, hc):
            h = s_refs[i].shape[0] // 2
            return o_refs[i].at[j, pl.ds(hc * h, h), :]

        first = [copy(i, kk, s_refs[i].at[pl.ds(c * (s_refs[i].shape[0] // 2), s_refs[i].shape[0] // 2), :], half(i, me, c),
                      (cx, cy, c)) for kk, (cx, cy) in enumerate(chips) for i in range(n_ops)]
        for cp in first:
            cp.start()
        passed = []
        for kk, (cx, cy) in enumerate(chips):
            for i in range(n_ops):
                blk = half(i, 2 * cx + cy, c)
                copy(i, kk, blk, blk, (cx, cy, c)).wait_recv()
                fwd = copy(i, 3 + kk, blk, blk, (x, y, 1 - c))
                fwd.start()
                passed.append(fwd)
        for kk, (cx, cy) in enumerate(chips):
            for i in range(n_ops):
                blk = half(i, 2 * cx + cy, 1 - c)
                copy(i, 3 + kk, blk, blk, (x, y, 1 - c)).wait_recv()
        for cp in first + passed:
            cp.wait_send()

    for s in ss:
        assert s.shape[0] % 32 == 0, s.shape
    return pl.pallas_call(
        body, name=name, in_specs=[_HBM] * n_ops, out_specs=[_HBM] * n_ops,
        out_shape=[jax.ShapeDtypeStruct((N_CHIPS,) + s.shape, s.dtype) for s in ss],
        scratch_shapes=[pltpu.SemaphoreType.DMA((6 * n_ops,)), pltpu.SemaphoreType.DMA((6 * n_ops,))],
    )(*ss)


_SEM = pl.BlockSpec(memory_space=pltpu.SEMAPHORE)
_ANY = pl.BlockSpec(memory_space=pl.ANY)
_DATAFLOW = pltpu.SideEffectType.DATAFLOW_SIDE_EFFECTING


def _chip_copies(s_refs, land_refs, send_sems, recv_sems):
    x, y, c = _place()
    me = 2 * x + y
    cps = []
    for i, (s_ref, land_ref) in enumerate(zip(s_refs, land_refs)):
        h = s_ref.shape[0] // 2
        for kk, (cx, cy) in enumerate(_other_chips(x, y)):
            cps.append(pltpu.make_async_remote_copy(
                src_ref=s_ref.at[pl.ds(c * h, h), :], dst_ref=land_ref.at[me, pl.ds(c * h, h), :], send_sem=send_sems.at[3 * i + kk],
                recv_sem=recv_sems.at[3 * i + kk], device_id=(cx, cy, c), device_id_type=MESH))
    return cps


def _gather_start(ss, after, *, name):
    n = len(ss)
    lands = [lax.empty((N_CHIPS,) + s.shape, s.dtype) for s in ss]

    def body(*refs):
        s_refs, land_refs = refs[:n], refs[n:2 * n]
        send_sems, recv_sems = refs[2 * n + 1], refs[2 * n + 2]
        token = refs[-1]
        for cp in _chip_copies(s_refs, land_refs, send_sems, recv_sems):
            cp.start()
        token[...] = jnp.zeros(token.shape, token.dtype)

    hbm = [pltpu.HBM(a.shape, a.dtype) for a in list(ss) + lands]
    res = pl.pallas_call(
        body, name=name,
        out_shape=(pltpu.SemaphoreType.DMA((3 * n,)), pltpu.SemaphoreType.DMA((3 * n,)), *hbm, jax.ShapeDtypeStruct((8, BLK), F32)),
        in_specs=[_HBM] * (2 * n) + [_ANY],
        out_specs=(_SEM, _SEM, *([_HBM] * (2 * n)), pl.BlockSpec(memory_space=pltpu.VMEM)),
        input_output_aliases={i: 2 + i for i in range(2 * n)},
        compiler_params=pltpu.CompilerParams(has_side_effects=_DATAFLOW),
    )(*[pltpu.with_memory_space_constraint(a, pltpu.HBM) for a in list(ss) + lands], after)
    return res[:-1], res[-1]


def _gather_wait(started, after, *, name):
    send_sems, recv_sems = started[0], started[1]
    n = (len(started) - 2) // 2

    def body(*refs):
        s_refs, land_refs = refs[:n], refs[n:2 * n]
        for cp in _chip_copies(s_refs, land_refs, refs[2 * n], refs[2 * n + 1]):
            cp.wait_send()
            cp.wait_recv()

    res = pl.pallas_call(
        body, name=name, out_shape=tuple(pltpu.HBM(a.shape, a.dtype) for a in started[2:]),
        in_specs=[_HBM] * (2 * n) + [_SEM, _SEM, _ANY], out_specs=tuple([_HBM] * (2 * n)),
        input_output_aliases={i: i for i in range(2 * n)},
        compiler_params=pltpu.CompilerParams(has_side_effects=_DATAFLOW),
    )(*started[2:], send_sems, recv_sems, after)
    return list(res[n:])


def _sibling_exchange(lands, *, name):
    n = len(lands)

    def body(*refs):
        o_refs, (send_sems, recv_sems) = refs[n:2 * n], refs[2 * n:]
        x, y, c = _place()
        cps = []
        for i, o_ref in enumerate(o_refs):
            h = o_ref.shape[1] // 2
            for kk, (cx, cy) in enumerate(_other_chips(x, y)):
                def half(hc):
                    return o_ref.at[2 * cx + cy, pl.ds(hc * h, h), :]
                sent = pltpu.make_async_remote_copy(src_ref=half(c), dst_ref=half(c), send_sem=send_sems.at[3 * i + kk],
                                                    recv_sem=recv_sems.at[3 * i + kk], device_id=(x, y, 1 - c), device_id_type=MESH)
                awaited = pltpu.make_async_remote_copy(src_ref=half(1 - c), dst_ref=half(1 - c), send_sem=send_sems.at[3 * i + kk],
                                                       recv_sem=recv_sems.at[3 * i + kk], device_id=(x, y, 1 - c),
                                                       device_id_type=MESH)
                cps.append((sent, awaited))
        for sent, _ in cps:
            sent.start()
        for sent, awaited in cps:
            awaited.wait_recv()
            sent.wait_send()

    return pl.pallas_call(
        body, name=name, in_specs=[_HBM] * n, out_specs=[_HBM] * n,
        out_shape=[jax.ShapeDtypeStruct(a.shape, a.dtype) for a in lands], scratch_shapes=_dma_sems(3 * n),
        input_output_aliases={i: i for i in range(n)},
    )(*lands)


def _own_block(gathered, s):
    me = 2 * lax.axis_index("x") + lax.axis_index("y")
    return lax.dynamic_update_slice(gathered, s[None], (me,) + (0,) * s.ndim)


def _dma_sems(n):
    return [pltpu.SemaphoreType.DMA((n,)), pltpu.SemaphoreType.DMA((n,))]


def _swap_halves(gps, *, name):
    n_ops = len(gps)

    def body(*refs):
        g_refs, o_refs, (send_sems, recv_sems) = refs[:n_ops], refs[n_ops:2 * n_ops], refs[2 * n_ops:]
        x, y, c = _place()
        cps = []
        for i, (g_ref, o_ref) in enumerate(zip(g_refs, o_refs)):
            h = g_ref.shape[1] // 2
            cps.append(pltpu.make_async_remote_copy(
                src_ref=g_ref.at[:, pl.ds((1 - c) * h, h), :], dst_ref=o_ref, send_sem=send_sems.at[i], recv_sem=recv_sems.at[i],
                device_id=(x, y, 1 - c), device_id_type=MESH))
        for cp in cps:
            cp.start()
        for cp in cps:
            cp.wait()

    return pl.pallas_call(
        body, name=name, in_specs=[_HBM] * n_ops, out_specs=[_HBM] * n_ops,
        out_shape=[jax.ShapeDtypeStruct((g.shape[0], g.shape[1] // 2, g.shape[2]), g.dtype) for g in gps],
        scratch_shapes=_dma_sems(n_ops),
    )(*gps)


def _scatter_chips(hps, *, name):
    n_ops = len(hps)

    def body(*refs):
        h_refs, o_refs, (send_sems, recv_sems) = refs[:n_ops], refs[n_ops:2 * n_ops], refs[2 * n_ops:]
        x, y, c = _place()
        cps = [pltpu.make_async_remote_copy(src_ref=h_ref.at[2 * cx + cy], dst_ref=o_ref.at[kk], send_sem=send_sems.at[3 * i + kk],
                                            recv_sem=recv_sems.at[3 * i + kk], device_id=(cx, cy, c), device_id_type=MESH)
               for i, (h_ref, o_ref) in enumerate(zip(h_refs, o_refs)) for kk, (cx, cy) in enumerate(_other_chips(x, y))]
        for cp in cps:
            cp.start()
        for cp in cps:
            cp.wait()

    return pl.pallas_call(
        body, name=name, in_specs=[_HBM] * n_ops, out_specs=[_HBM] * n_ops,
        out_shape=[jax.ShapeDtypeStruct((3,) + hp.shape[1:], hp.dtype) for hp in hps],
        scratch_shapes=_dma_sems(3 * n_ops),
    )(*hps)


def _join_halves(fs, *, name):
    n_ops = len(fs)

    def body(*refs):
        f_refs, o_refs, (send_sems, recv_sems) = refs[:n_ops], refs[n_ops:2 * n_ops], refs[2 * n_ops:]
        x, y, c = _place()
        cps = [pltpu.make_async_remote_copy(src_ref=f_ref, dst_ref=o_ref, send_sem=send_sems.at[i], recv_sem=recv_sems.at[i],
                                            device_id=(x, y, 1 - c), device_id_type=MESH)
               for i, (f_ref, o_ref) in enumerate(zip(f_refs, o_refs))]
        for cp in cps:
            cp.start()
        for cp in cps:
            cp.wait()

    theirs = pl.pallas_call(
        body, name=name, in_specs=[_HBM] * n_ops, out_specs=[_HBM] * n_ops,
        out_shape=[jax.ShapeDtypeStruct(f.shape, f.dtype) for f in fs], scratch_shapes=_dma_sems(n_ops),
    )(*fs)
    south = lax.axis_index("c") == 0
    return [jnp.concatenate([jnp.where(south, f, t), jnp.where(south, t, f)], axis=0) for f, t in zip(fs, theirs)]


def _add_halves(gp, ra, wire_dtype, *, name, bm=256):
    n, r, c_ = gp.shape
    h = r // 2
    bm = _tile(h, bm)
    per = h // bm
    c = lax.axis_index("c").astype(jnp.int32).reshape(1)

    def body(c_ref, g_ref, ra_ref, o_ref, ow_ref):
        s = g_ref[...] + ra_ref[...]
        o_ref[...] = s
        ow_ref[...] = s.astype(wire_dtype)

    mine = pl.BlockSpec((None, bm, c_), lambda j, i, cr: (j, i, 0))
    return pl.pallas_call(
        body, name=name,
        grid_spec=pltpu.PrefetchScalarGridSpec(
            num_scalar_prefetch=1, grid=(n, per),
            in_specs=[pl.BlockSpec((None, bm, c_), lambda j, i, cr: (j, cr[0] * per + i, 0)), mine],
            out_specs=[mine, mine]),
        out_shape=[jax.ShapeDtypeStruct((n, h, c_), F32), jax.ShapeDtypeStruct((n, h, c_), wire_dtype)],
        compiler_params=pltpu.CompilerParams(dimension_semantics=("parallel", "parallel")),
    )(c, gp, ra)


def _add_chips(hp, rb, *, name, bm=256):
    n, h, c_ = hp.shape
    bm = _tile(h, bm)
    me = (2 * lax.axis_index("x") + lax.axis_index("y")).astype(jnp.int32).reshape(1)

    def body(me_ref, h_ref, rb_ref, o_ref):
        o_ref[...] = ((h_ref[...] + rb_ref[0].astype(F32)) + rb_ref[1].astype(F32)) + rb_ref[2].astype(F32)

    return pl.pallas_call(
        body, name=name,
        grid_spec=pltpu.PrefetchScalarGridSpec(
            num_scalar_prefetch=1, grid=(h // bm,),
            in_specs=[pl.BlockSpec((None, bm, c_), lambda i, mr: (mr[0], i, 0)),
                      pl.BlockSpec((3, bm, c_), lambda i, mr: (0, i, 0))],
            out_specs=pl.BlockSpec((bm, c_), lambda i, mr: (i, 0))),
        out_shape=jax.ShapeDtypeStruct((h, c_), F32),
        compiler_params=pltpu.CompilerParams(dimension_semantics=("parallel",)),
    )(me, hp, rb)


def _reduce_scatter(gps, wire_dtypes, *, tag):
    ras = _swap_halves(gps, name=f"{tag}_swap_halves")
    hps = [_add_halves(gp, ra, wd, name=f"{tag}_add_halves{i}") for i, (gp, ra, wd) in enumerate(zip(gps, ras, wire_dtypes))]
    rbs = _scatter_chips([hw for _, hw in hps], name=f"{tag}_scatter_chips")
    fs = [_add_chips(hf, rb, name=f"{tag}_add_chips{i}") for i, ((hf, _), rb) in enumerate(zip(hps, rbs))]
    return _join_halves(fs, name=f"{tag}_join_halves")


def _adamw(w, g, m, v, *, name):
    r, c_ = w.shape
    bm = r
    for cand in (512, 256, 128, 64, 32, 16, 8):
        if r % cand == 0:
            bm = cand
            break
    c1 = 1.0 - ADAM_B1 ** ADAM_STEP
    c2 = 1.0 - ADAM_B2 ** ADAM_STEP

    def body(w_ref, g_ref, m_ref, v_ref, d_ref, nm_ref, nv_ref):
        g_ = g_ref[...]
        m_ = ADAM_B1 * m_ref[...] + (1.0 - ADAM_B1) * g_
        v_ = ADAM_B2 * v_ref[...] + (1.0 - ADAM_B2) * (g_ * g_)
        d_ref[...] = -ADAM_LR * ((m_ / c1) / (jnp.sqrt(v_ / c2) + ADAM_EPS) + ADAM_WD * w_ref[...])
        nm_ref[...] = m_
        nv_ref[...] = v_

    spec = pl.BlockSpec((bm, c_), lambda i: (i, 0))
    return pl.pallas_call(
        body, name=name, grid=(r // bm,), in_specs=[spec] * 4, out_specs=[spec] * 3,
        out_shape=[jax.ShapeDtypeStruct((r, c_), F32)] * 3,
        compiler_params=pltpu.CompilerParams(dimension_semantics=("parallel",)),
    )(w, g, m, v)


_WEIGHTS = ["a_norm", "a_w_in", "a_w_out", "b_norm", "b_w_in", "b_v_ln_g", "b_v_ln_b", "b_w_s", "b_b_s", "b_w_out",
            "c_norm", "c_w_in", "c_conv_w", "c_conv_b", "c_ln_g", "c_ln_b", "c_w_out", "d_norm", "d_w_in", "d_b_f",
            "d_w_out", "final_norm"]
_SHARD_AXIS = {"a_norm": None, "a_w_in": 2, "a_w_out": 1, "b_norm": 1, "b_w_in": 2, "b_v_ln_g": 1, "b_v_ln_b": 1, "b_w_s": None,
               "b_b_s": None, "b_w_out": 1, "c_norm": 1, "c_w_in": 2, "c_conv_w": 2, "c_conv_b": 1, "c_ln_g": 1, "c_ln_b": 1,
               "c_w_out": 1, "d_norm": 1, "d_w_in": 2, "d_b_f": None, "d_w_out": 1, "final_norm": None}
_BIG = ["a_w_in", "a_w_out", "b_w_in", "b_w_out", "c_w_in", "c_w_out", "d_w_in", "d_w_out"]
_EARLY = ("a_w_in", "a_w_out", "b_w_in", "b_w_out")
_SMALL_SHARDED = [n for n in _WEIGHTS if _SHARD_AXIS[n] is not None and n not in _BIG]
_REPLICATED = [n for n in _WEIGHTS if _SHARD_AXIS[n] is None]
_ROW_ALIGN = 32
_ROW_ALIGN_SUMMED = 128


def _pack(pieces, dtype, align=_ROW_ALIGN):
    flat = jnp.concatenate([p.reshape(-1).astype(dtype) for p in pieces])
    unit = align * PACK_C
    total = -(-flat.shape[0] // unit) * unit
    return jnp.pad(flat, (0, total - flat.shape[0])).reshape(total // PACK_C, PACK_C)


def _unpack(flat, shapes):
    out, off = [], 0
    for s in shapes:
        n = math.prod(s)
        out.append(flat[off:off + n].reshape(s))
        off += n
    return out


def _full_shape(local_shape, axis):
    s = list(local_shape)
    if axis is not None:
        s[axis] *= N_CHIPS
    return tuple(s)


def _gather_weights(local):
    def whole(n, gt):
        if _SHARD_AXIS[n] == 1:
            return gt.reshape(-1, gt.shape[-1])
        if n == "d_w_in":
            return gt.transpose(1, 0, 2).reshape(gt.shape[1], -1)
        return gt

    full = {n: local[n][0] if n != "final_norm" else local[n] for n in _REPLICATED}
    first = [n for n in _BIG if n in _EARLY]
    later = [n for n in _BIG if n not in _EARLY]
    mine = [local[n][0].astype(BF16) for n in first] + [_pack([local[n] for n in _SMALL_SHARDED], F32)]
    got = [_own_block(gt, s) for gt, s in zip(_allgather_chips(mine, name="gather_weights"), mine)]
    full.update({n: whole(n, gt) for n, gt in zip(first, got)})
    small = got[-1].reshape(N_CHIPS, -1)
    shards = [_unpack(small[j], [local[n].shape[1:] for n in _SMALL_SHARDED]) for j in range(N_CHIPS)]
    for i, n in enumerate(_SMALL_SHARDED):
        full[n] = jnp.concatenate([shards[j][i] for j in range(N_CHIPS)], axis=_SHARD_AXIS[n] - 1)

    mine_later = [local[n][0].astype(BF16) for n in later]
    started, token = _gather_start(mine_later, got[0], name="gather_later_start")
    full["a_norm"] = full["a_norm"] + token[0, 0]

    def rest(after):
        lands = _sibling_exchange(_gather_wait(started, after, name="gather_later_wait"), name="gather_later_exchange")
        return {n: whole(n, _own_block(gt, s)) for n, gt, s in zip(later, lands, mine_later)}

    return full, rest


def _repl_piece_len(local):
    total = sum(math.prod(local[n].shape) for n in _REPLICATED)
    return -(-total // N_CHIPS)


def _reduce_grads(g, local):
    rep_flat = jnp.concatenate([g[n].reshape(-1) for n in _REPLICATED])
    piece = _repl_piece_len(local)
    rep_flat = jnp.pad(rep_flat, (0, N_CHIPS * piece - rep_flat.shape[0]))

    def shard(n, j):
        full = g[n].reshape(_full_shape(local[n].shape, _SHARD_AXIS[n]))
        width = local[n].shape[_SHARD_AXIS[n]]
        return lax.slice_in_dim(full, j * width, (j + 1) * width, axis=_SHARD_AXIS[n])

    small = jnp.stack([_pack([shard(n, j) for n in _SMALL_SHARDED] + [rep_flat[j * piece:(j + 1) * piece]], F32)
                       for j in range(N_CHIPS)])
    summed = _reduce_scatter([g[n] for n in _BIG] + [small], [BF16] * len(_BIG) + [F32], tag="grads")
    red = {n: s.reshape(local[n].shape) for n, s in zip(_BIG, summed)}
    out = _unpack(summed[-1].reshape(-1), [local[n].shape for n in _SMALL_SHARDED] + [(piece,)])
    red.update(zip(_SMALL_SHARDED, out[:-1]))
    rep_mine = _pack([out[-1]], F32)
    rep = _own_block(_allgather_chips([rep_mine], name="gather_replicated_grads")[0], rep_mine)
    rep = rep.reshape(N_CHIPS, -1)[:, :piece].reshape(-1)
    for n, val in zip(_REPLICATED, _unpack(rep, [local[n].shape for n in _REPLICATED])):
        red[n] = val
    return red


def _update(local, grads, m, v):
    delta, new_m, new_v = {}, {}, {}
    for n in _BIG:
        shp = local[n].shape
        two = (shp[-2], shp[-1])
        res = _adamw(local[n].reshape(two), grads[n].reshape(two), m[n].reshape(two), v[n].reshape(two), name=f"adamw_{n}")
        delta[n], new_m[n], new_v[n] = [r.reshape(shp) for r in res]
    small = [n for n in _WEIGHTS if n not in _BIG]
    shapes = [local[n].shape for n in small]
    packed = [_pack([src[n] for n in small], F32) for src in (local, grads, m, v)]
    res = _adamw(*packed, name="adamw_small")
    for dst, r in zip((delta, new_m, new_v), res):
        for n, val in zip(small, _unpack(r.reshape(-1), shapes)):
            dst[n] = val
    return delta, new_m, new_v


def kernel(x, a_norm, a_w_in, a_w_out, b_norm, b_w_in, b_v_ln_g, b_v_ln_b, b_w_s, b_b_s, b_w_out, c_norm, c_w_in, c_conv_w, c_conv_b, c_ln_g, c_ln_b, c_w_out, d_norm, d_w_in, d_b_f, d_w_out, final_norm, loss_target, m_a_norm, m_a_w_in, m_a_w_out, m_b_norm, m_b_w_in, m_b_v_ln_g, m_b_v_ln_b, m_b_w_s, m_b_b_s, m_b_w_out, m_c_norm, m_c_w_in, m_c_conv_w, m_c_conv_b, m_c_ln_g, m_c_ln_b, m_c_w_out, m_d_norm, m_d_w_in, m_d_b_f, m_d_w_out, m_final_norm, v_a_norm, v_a_w_in, v_a_w_out, v_b_norm, v_b_w_in, v_b_v_ln_g, v_b_v_ln_b, v_b_w_s, v_b_b_s, v_b_w_out, v_c_norm, v_c_w_in, v_c_conv_w, v_c_conv_b, v_c_ln_g, v_c_ln_b, v_c_w_out, v_d_norm, v_d_w_in, v_d_b_f, v_d_w_out, v_final_norm):
    local = dict(zip(_WEIGHTS, (a_norm, a_w_in, a_w_out, b_norm, b_w_in, b_v_ln_g, b_v_ln_b, b_w_s, b_b_s, b_w_out, c_norm, c_w_in,
                                c_conv_w, c_conv_b, c_ln_g, c_ln_b, c_w_out, d_norm, d_w_in, d_b_f, d_w_out, final_norm)))
    m = dict(zip(_WEIGHTS, (m_a_norm, m_a_w_in, m_a_w_out, m_b_norm, m_b_w_in, m_b_v_ln_g, m_b_v_ln_b, m_b_w_s, m_b_b_s, m_b_w_out,
                            m_c_norm, m_c_w_in, m_c_conv_w, m_c_conv_b, m_c_ln_g, m_c_ln_b, m_c_w_out, m_d_norm, m_d_w_in, m_d_b_f,
                            m_d_w_out, m_final_norm)))
    v = dict(zip(_WEIGHTS, (v_a_norm, v_a_w_in, v_a_w_out, v_b_norm, v_b_w_in, v_b_v_ln_g, v_b_v_ln_b, v_b_w_s, v_b_b_s, v_b_w_out,
                            v_c_norm, v_c_w_in, v_c_conv_w, v_c_conv_b, v_c_ln_g, v_c_ln_b, v_c_w_out, v_d_norm, v_d_w_in, v_d_b_f,
                            v_d_w_out, v_final_norm)))
    loss_part, grad_x, g = _local_step(x, loss_target, *_gather_weights(local))
    loss = lax.psum(loss_part, ("x", "y", "c"))
    grads = _reduce_grads(g, local)
    delta, new_m, new_v = _update(local, grads, m, v)
    return (loss, grad_x, *[grads[n] for n in _WEIGHTS], *[delta[n] for n in _WEIGHTS],
            *[new_m[n] for n in _WEIGHTS], *[new_v[n] for n in _WEIGHTS])
```

```python
import functools
import math

import jax
import jax.numpy as jnp
from jax import lax
from jax.experimental import pallas as pl
from jax.experimental.pallas import tpu as pltpu

F32, BF16 = jnp.float32, jnp.bfloat16
MESH = pl.DeviceIdType.MESH

D_MODEL = 1024
HEADS = 16
HEAD_DIM = 64
BLK = 128
PAIRS = HEADS // 2
GM_W = 2048
GM_G = 16
CV_W = 2048
CV_K = 31
HALO = 32
EPS = 1e-6
N_CHIPS = 4
PACK_C = 1024
ADAM_LR, ADAM_B1, ADAM_B2, ADAM_EPS, ADAM_WD, ADAM_STEP = 0.001, 0.9, 0.999, 1e-08, 0.01, 10

_NT = (((1,), (1,)), ((), ()))
_TN = (((0,), (0,)), ((), ()))
_NN = (((1,), (0,)), ((), ()))


def _dot(a, b, dims=_NN):
    return lax.dot_general(a, b, dims, preferred_element_type=F32)


def _split3(x):
    hi = x.astype(BF16)
    r = x - hi.astype(F32)
    mid = r.astype(BF16)
    lo = (r - mid.astype(F32)).astype(BF16)
    return hi, mid, lo


def _dot3_right(x, m):
    hi, mid, lo = _split3(x)
    return _dot(hi, m) + _dot(mid, m) + _dot(lo, m)


def _dot3_left(m, x):
    hi, mid, lo = _split3(x)
    return _dot(m, hi) + _dot(m, mid) + _dot(m, lo)


def _sigmoid(x):
    return 1.0 / (1.0 + jnp.exp(-x))


def _silu(x):
    return x * _sigmoid(x)


def _dsilu(x):
    s = _sigmoid(x)
    return s * (1.0 + x * (1.0 - s))


_GELU_C = math.sqrt(2.0 / math.pi)
_GELU_A = 0.044715


def _gelu(x):
    return 0.5 * x * (1.0 + jnp.tanh(_GELU_C * (x + _GELU_A * x * x * x)))


def _dgelu(x):
    t = jnp.tanh(_GELU_C * (x + _GELU_A * x * x * x))
    return 0.5 * (1.0 + t) + 0.5 * x * (1.0 - t * t) * _GELU_C * (1.0 + 3.0 * _GELU_A * x * x)


def _log_sigmoid(x):
    return jnp.minimum(x, 0.0) - jnp.log(1.0 + jnp.exp(-jnp.abs(x)))


def _rms_fwd(x, g):
    r = lax.rsqrt(jnp.mean(x * x, axis=-1, keepdims=True) + EPS)
    return x * r * g


def _rms_bwd(dy, x, g):
    r = lax.rsqrt(jnp.mean(x * x, axis=-1, keepdims=True) + EPS)
    xh = x * r
    dxh = dy * g
    dx = r * (dxh - xh * jnp.mean(dxh * xh, axis=-1, keepdims=True))
    return dx, dy * xh


def _ln_stats(x):
    mu = jnp.mean(x, axis=-1, keepdims=True)
    xc = x - mu
    r = lax.rsqrt(jnp.mean(xc * xc, axis=-1, keepdims=True) + EPS)
    return xc * r, r


def _ln_bwd(dy, xh, r, g):
    dxh = dy * g
    return r * (dxh - jnp.mean(dxh, axis=-1, keepdims=True) - xh * jnp.mean(dxh * xh, axis=-1, keepdims=True))


def _colsum(x):
    return jnp.sum(x, axis=0, keepdims=True)


def _tile(n, want):
    for t in range(min(n, want), 7, -1):
        if n % t == 0 and t % 8 == 0:
            return t
    return n


MM_TILE = 1024


def _matmul(a, b, *, name, mode="nn", residual=None, out_shards=1):
    (m, k) = a.shape
    b_shards = b.shape[0] if b.ndim == 3 else 1
    if mode == "nn":
        n = b.shape[-1] * b_shards
        tn, tk = _tile(n // max(b_shards, out_shards), MM_TILE), _tile(k, MM_TILE)
    else:
        n = b.shape[-2]
        tn, tk = _tile(n // out_shards, MM_TILE), _tile(k // b_shards, MM_TILE)
    tm = _tile(m, MM_TILE)
    nk = k // tk
    a_spec = pl.BlockSpec((tm, tk), lambda i, j, kk: (i, kk))
    if mode == "nn":
        dims = _NN
        if b_shards == 1:
            b_spec = pl.BlockSpec((tk, tn), lambda i, j, kk: (kk, j))
        else:
            per_b = n // b_shards // tn
            b_spec = pl.BlockSpec((None, tk, tn), lambda i, j, kk: (j // per_b, kk, j % per_b))
    else:
        dims = _NT
        if b_shards == 1:
            b_spec = pl.BlockSpec((tn, tk), lambda i, j, kk: (j, kk))
        else:
            per_b = k // b_shards // tk
            b_spec = pl.BlockSpec((None, tn, tk), lambda i, j, kk: (kk // per_b, j, kk % per_b))
    if out_shards == 1:
        o_spec = pl.BlockSpec((tm, tn), lambda i, j, kk: (i, j))
        o_shape = (m, n)
    else:
        per_o = n // out_shards // tn
        o_spec = pl.BlockSpec((None, tm, tn), lambda i, j, kk: (j // per_o, i, j % per_o))
        o_shape = (out_shards, m, n // out_shards)
    has_res = residual is not None

    def body(a_ref, b_ref, *rest):
        o_ref = rest[-1]
        kk = pl.program_id(2)
        part = _dot(a_ref[...].astype(BF16), b_ref[...].astype(BF16), dims)
        if has_res:
            @pl.when(kk == 0)
            def _():
                o_ref[...] = part + rest[0][...]
        else:
            @pl.when(kk == 0)
            def _():
                o_ref[...] = part

        @pl.when(kk > 0)
        def _():
            o_ref[...] += part

    return pl.pallas_call(
        body, name=name, grid=(m // tm, n // tn, nk),
        in_specs=[a_spec, b_spec] + ([o_spec] if has_res else []),
        out_specs=o_spec, out_shape=jax.ShapeDtypeStruct(o_shape, F32),
        compiler_params=pltpu.CompilerParams(dimension_semantics=("parallel", "parallel", "arbitrary")),
    )(a, b, *([residual] if has_res else []))


def _rows(fn, *, name, steps, ins, outs, accs=(), scratch=()):
    ni, no, na = len(ins), len(outs), len(accs)

    def body(*refs):
        in_refs, out_refs = refs[:ni], refs[ni:ni + no]
        acc_refs, scr = refs[ni + no:ni + no + na], refs[ni + no + na:]
        i = pl.program_id(0)

        @pl.when(i == 0)
        def _():
            for r in acc_refs:
                r[...] = jnp.zeros(r.shape, r.dtype)

        fn(i, in_refs, out_refs, acc_refs, scr)

    def full(shape):
        nd = len(shape)
        return pl.BlockSpec(tuple(shape), lambda i: (0,) * nd)

    res = pl.pallas_call(
        body, name=name, grid=(steps,),
        in_specs=[pl.BlockSpec(bs, im) for _, bs, im in ins],
        out_specs=[pl.BlockSpec(bs, im) for _, _, bs, im in outs] + [full(s) for s, _ in accs],
        out_shape=[jax.ShapeDtypeStruct(s, d) for s, d, _, _ in outs] + [jax.ShapeDtypeStruct(s, d) for s, d in accs],
        scratch_shapes=list(scratch),
        compiler_params=pltpu.CompilerParams(dimension_semantics=("arbitrary",)),
    )(*[a for a, _, _ in ins])
    return res


def _rb(arr, bm, cb=0, width=None):
    w = arr.shape[1] if width is None else width
    return (arr, (bm, w), lambda i: (i, cb))


def _const(arr):
    nd = arr.ndim
    return (arr, tuple(arr.shape), lambda i: (0,) * nd)


def _ro(t, w, dtype, bm):
    return ((t, w), dtype, (bm, w), lambda i: (i, 0))


def _rot(t, w, dtype, bm):
    return ((w, t), dtype, (w, bm), lambda i: (0, i))


def _rmsnorm(x, g, *, name, bm=512):
    t, d = x.shape
    bm = _tile(t, bm)

    def fn(i, ins, outs, accs, scr):
        h = _rms_fwd(ins[0][...], ins[1][...])
        outs[0][...] = h.astype(BF16)
        outs[1][...] = h.T.astype(BF16)

    return _rows(fn, name=name, steps=t // bm, ins=[_rb(x, bm), _const(g)], outs=[_ro(t, d, BF16, bm), _rot(t, d, BF16, bm)])


def _rmsnorm_bwd(dh, x, g, dres, *, name, bm=512):
    t, d = x.shape
    bm = _tile(t, bm)

    def fn(i, ins, outs, accs, scr):
        dx, dgrow = _rms_bwd(ins[0][...], ins[1][...], ins[2][...])
        outs[0][...] = ins[3][...] + dx
        accs[0][...] += _colsum(dgrow)

    return _rows(fn, name=name, steps=t // bm, ins=[_rb(dh, bm), _rb(x, bm), _const(g), _rb(dres, bm)],
                 outs=[_ro(t, d, F32, bm)], accs=[((1, d), F32)])


def _gate(o, p, gcb, *, name, bm=512):
    t, w = o.shape
    bm = _tile(t, bm)

    def fn(i, ins, outs, accs, scr):
        y = ins[0][...] * _silu(ins[1][...])
        outs[0][...] = y.astype(BF16)
        outs[1][...] = y.T.astype(BF16)

    return _rows(fn, name=name, steps=t // bm, ins=[_rb(o, bm), _rb(p, bm, gcb, w)],
                 outs=[_ro(t, w, BF16, bm), _rot(t, w, BF16, bm)])


def _gate_bwd(dy, o, p, gcb, *, name, bm=512):
    t, w = o.shape
    bm = _tile(t, bm)

    def fn(i, ins, outs, accs, scr):
        dy_, o_, g_ = ins[0][...], ins[1][...], ins[2][...]
        outs[0][...] = dy_ * _silu(g_)
        outs[1][...] = dy_ * o_ * _dsilu(g_)

    return _rows(fn, name=name, steps=t // bm, ins=[_rb(dy, bm), _rb(o, bm), _rb(p, bm, gcb, w)],
                 outs=[_ro(t, w, F32, bm), _ro(t, w, F32, bm)])


def _loss_head(x, g, tgt, *, name, bm=512):
    t, d = x.shape
    bm = _tile(t, bm)

    def fn(i, ins, outs, accs, scr):
        x_, g_, tg = ins[0][...], ins[1][...], ins[2][...]
        err = _rms_fwd(x_, g_) - tg
        part = 0.5 * jnp.sum(jnp.sum(err * err, axis=-1, keepdims=True), axis=0, keepdims=True) / d
        dx, dgrow = _rms_bwd(err / d, x_, g_)
        outs[0][...] = dx
        accs[0][...] += _colsum(dgrow)
        accs[1][...] += jnp.broadcast_to(part, (1, BLK))

    return _rows(fn, name=name, steps=t // bm, ins=[_rb(x, bm), _const(g), _rb(tgt, bm)],
                 outs=[_ro(t, d, F32, bm)], accs=[((1, d), F32), ((1, BLK), F32)])


def _gmlp_mix_weights(ws_ref, g):
    row = lax.broadcasted_iota(jnp.int32, (BLK, BLK), 0)
    col = lax.broadcasted_iota(jnp.int32, (BLK, BLK), 1)
    tril = col <= row
    return jnp.where(tril, ws_ref[g], 0.0), tril


def _gmlp_fwd(p, ln_g, ln_b, w_s, bs_t, *, name):
    t = p.shape[0]

    def fn(i, ins, outs, accs, scr):
        p_ref, lg, lb, ws_ref, bst = ins
        vn = _ln_stats(_gelu(p_ref[:, GM_W:2 * GM_W]))[0] * lg[...] + lb[...]
        for g in range(GM_G):
            cs = slice(g * BLK, (g + 1) * BLK)
            wt, _ = _gmlp_mix_weights(ws_ref, g)
            s = _dot(wt.astype(BF16), vn[:, cs].astype(BF16)) + bst[:, g:g + 1]
            u = _gelu(p_ref[:, cs])
            gate = p_ref[:, 2 * GM_W + g * BLK:2 * GM_W + (g + 1) * BLK]
            y = u * s * _silu(gate)
            outs[0][:, cs] = y.astype(BF16)
            outs[1][cs, :] = y.T.astype(BF16)

    return _rows(fn, name=name, steps=t // BLK, ins=[_rb(p, BLK), _const(ln_g), _const(ln_b), _const(w_s), _const(bs_t)],
                 outs=[_ro(t, GM_W, BF16, BLK), _rot(t, GM_W, BF16, BLK)])


def _gmlp_bwd(dy, p, ln_g, ln_b, w_s, bs_t, *, name):
    t = p.shape[0]

    def fn(i, ins, outs, accs, scr):
        dy_ref, p_ref, lg, lb, ws_ref, bst = ins
        dp_ref = outs[0]
        dlg, dlb, dws, dbst = accs
        dvn_ref = scr[0]
        v_pre = p_ref[:, GM_W:2 * GM_W]
        xh, r = _ln_stats(_gelu(v_pre))
        vn = xh * lg[...] + lb[...]
        for g in range(GM_G):
            cs = slice(g * BLK, (g + 1) * BLK)
            gs = slice(2 * GM_W + g * BLK, 2 * GM_W + (g + 1) * BLK)
            wt, tril = _gmlp_mix_weights(ws_ref, g)
            vg = vn[:, cs].astype(BF16)
            s = _dot(wt.astype(BF16), vg) + bst[:, g:g + 1]
            u_pre, gate, dyg = p_ref[:, cs], p_ref[:, gs], dy_ref[:, cs]
            u = _gelu(u_pre)
            dos = dyg * _silu(gate)
            dp_ref[:, gs] = dyg * u * s * _dsilu(gate)
            dp_ref[:, cs] = dos * s * _dgelu(u_pre)
            ds = (dos * u).astype(BF16)
            dws[g] += jnp.where(tril, _dot(ds, vg, _NT), 0.0)
            dbst[:, g:g + 1] += jnp.sum(dos * u, axis=1, keepdims=True)
            dvn_ref[:, cs] = _dot(wt.astype(BF16), ds, _TN)
        dvn = dvn_ref[...]
        dlg[...] += _colsum(dvn * xh)
        dlb[...] += _colsum(dvn)
        dp_ref[:, GM_W:2 * GM_W] = _ln_bwd(dvn, xh, r, lg[...]) * _dgelu(v_pre)

    return _rows(fn, name=name, steps=t // BLK,
                 ins=[_rb(dy, BLK), _rb(p, BLK), _const(ln_g), _const(ln_b), _const(w_s), _const(bs_t)],
                 outs=[_ro(t, 3 * GM_W, F32, BLK)],
                 accs=[((1, GM_W), F32), ((1, GM_W), F32), ((GM_G, BLK, BLK), F32), ((BLK, GM_G), F32)],
                 scratch=[pltpu.VMEM((BLK, GM_W), F32)])


CV_BM = 128
CV_RC = 8
SUBLANES = 8
CV_FWD_OFFS = [HALO - (CV_K - 1) + k for k in range(CV_K)]
CV_BWD_OFFS = [CV_K - 1 - k for k in range(CV_K)]


def _conv_halo_prev(p, cb, bm):
    per = bm // HALO
    return (p, (HALO, CV_W), lambda i: (jnp.maximum(i * per - 1, 0), cb))


def _conv_scratch(bm):
    return [pltpu.VMEM((bm + HALO, CV_W), F32), pltpu.VMEM((SUBLANES - 1, bm + HALO - SUBLANES, CV_W), F32),
            pltpu.VMEM((bm, CV_W), F32)]


def _conv_shift_copies(ext_ref, sh_ref):
    rows = sh_ref.shape[1]
    for b in range(1, SUBLANES):
        sh_ref[b - 1] = ext_ref[pl.ds(b, rows), :]


def _conv_window(ext_ref, sh_ref, off, r0, rows):
    b = off % SUBLANES
    src = ext_ref if b == 0 else sh_ref.at[b - 1]
    return src[pl.ds(r0 + (off - b), rows), :]


def _conv_taps(ext_ref, sh_ref, cw_ref, y_ref, offs):
    bm = y_ref.shape[0]

    def chunk(ci, c):
        r0 = pl.multiple_of(ci * CV_RC, CV_RC)
        acc = jnp.zeros((CV_RC, CV_W), F32)
        for k in range(CV_K):
            acc = acc + cw_ref[pl.ds(k * SUBLANES, CV_RC), :] * _conv_window(ext_ref, sh_ref, offs[k], r0, CV_RC)
        y_ref[pl.ds(r0, CV_RC), :] = acc
        return c

    lax.fori_loop(0, bm // CV_RC, chunk, 0)


def _conv_dweights(dy1_ref, ext_ref, sh_ref, dcw_ref):
    bm = dy1_ref.shape[0]
    groups = 4
    for k in range(CV_K):
        def step(ci, acc, off=CV_FWD_OFFS[k]):
            prods = []
            for u in range(groups):
                r0 = pl.multiple_of((ci * groups + u) * CV_RC, CV_RC)
                prods.append(dy1_ref[pl.ds(r0, CV_RC), :] * _conv_window(ext_ref, sh_ref, off, r0, CV_RC))
            return acc + ((prods[0] + prods[1]) + (prods[2] + prods[3]))

        dcw_ref[k:k + 1, :] += _colsum(lax.fori_loop(0, bm // (CV_RC * groups), step, jnp.zeros((CV_RC, CV_W), F32)))


def _conv_fill(i, ext_ref, a_prev, b_prev, a, b, bm, seq):
    keep = jnp.where((i % (seq // bm)) == 0, 0.0, 1.0)
    ext_ref[pl.ds(0, HALO), :] = keep * (a_prev * _sigmoid(b_prev))
    ext_ref[pl.ds(HALO, bm), :] = a * _sigmoid(b)


def _conv_fwd(p, cw, cb, ln_g, ln_b, seq, *, name, bm=CV_BM):
    t = p.shape[0]

    def fn(i, ins, outs, accs, scr):
        a, b, gate, ap, bp = [r[...] for r in ins[:5]]
        cw_ref, cb_, lg, lb = ins[5], ins[6][...], ins[7][...], ins[8][...]
        ext, sh, y = scr
        _conv_fill(i, ext, ap, bp, a, b, bm, seq)
        _conv_shift_copies(ext, sh)
        _conv_taps(ext, sh, cw_ref, y, CV_FWD_OFFS)
        y2 = _ln_stats(y[...] + cb_)[0] * lg + lb
        out = _silu(y2) * _silu(gate)
        outs[0][...] = out.astype(BF16)
        outs[1][...] = out.T.astype(BF16)

    return _rows(fn, name=name, steps=t // bm,
                 ins=[_rb(p, bm, 0, CV_W), _rb(p, bm, 1, CV_W), _rb(p, bm, 2, CV_W),
                      _conv_halo_prev(p, 0, bm), _conv_halo_prev(p, 1, bm),
                      _const(cw), _const(cb), _const(ln_g), _const(ln_b)],
                 outs=[_ro(t, CV_W, BF16, bm), _rot(t, CV_W, BF16, bm)], scratch=_conv_scratch(bm))


def _conv_bwd_post(dy, p, cw, cb, ln_g, ln_b, seq, *, name, bm=CV_BM):
    t = p.shape[0]

    def fn(i, ins, outs, accs, scr):
        dy_, a, b, gate, ap, bp = [r[...] for r in ins[:6]]
        cw_ref, cb_, lg, lb = ins[6], ins[7][...], ins[8][...], ins[9][...]
        dlg, dlb, dcb, dcw = accs
        ext, sh, y = scr
        _conv_fill(i, ext, ap, bp, a, b, bm, seq)
        _conv_shift_copies(ext, sh)
        _conv_taps(ext, sh, cw_ref, y, CV_FWD_OFFS)
        xh, r = _ln_stats(y[...] + cb_)
        y2 = xh * lg + lb
        outs[1][...] = dy_ * _silu(y2) * _dsilu(gate)
        dy2 = dy_ * _silu(gate) * _dsilu(y2)
        dlg[...] += _colsum(dy2 * xh)
        dlb[...] += _colsum(dy2)
        dy1 = _ln_bwd(dy2, xh, r, lg)
        outs[0][...] = dy1
        dcb[...] += _colsum(dy1)
        _conv_dweights(outs[0], ext, sh, dcw)

    return _rows(fn, name=name, steps=t // bm,
                 ins=[_rb(dy, bm), _rb(p, bm, 0, CV_W), _rb(p, bm, 1, CV_W), _rb(p, bm, 2, CV_W),
                      _conv_halo_prev(p, 0, bm), _conv_halo_prev(p, 1, bm),
                      _const(cw), _const(cb), _const(ln_g), _const(ln_b)],
                 outs=[_ro(t, CV_W, F32, bm), _ro(t, CV_W, F32, bm)],
                 accs=[((1, CV_W), F32), ((1, CV_W), F32), ((1, CV_W), F32), ((CV_K, CV_W), F32)],
                 scratch=_conv_scratch(bm))


def _conv_bwd_pre(dy1, dgate, p, cw, seq, *, name, bm=CV_BM):
    t = p.shape[0]
    per = bm // HALO
    last_halo = t // HALO - 1

    def fn(i, ins, outs, accs, scr):
        d1, d1n, dg, a, b = [r[...] for r in ins[:5]]
        ext, sh, y = scr
        keep = jnp.where((i % (seq // bm)) == (seq // bm - 1), 0.0, 1.0)
        ext[pl.ds(0, bm), :] = d1
        ext[pl.ds(bm, HALO), :] = keep * d1n
        _conv_shift_copies(ext, sh)
        _conv_taps(ext, sh, ins[5], y, CV_BWD_OFFS)
        dy0 = y[...]
        sb = _sigmoid(b)
        outs[0][:, 0:CV_W] = dy0 * sb
        outs[0][:, CV_W:2 * CV_W] = dy0 * a * sb * (1.0 - sb)
        outs[0][:, 2 * CV_W:3 * CV_W] = dg

    return _rows(fn, name=name, steps=t // bm,
                 ins=[_rb(dy1, bm), (dy1, (HALO, CV_W), lambda i: (jnp.minimum((i + 1) * per, last_halo), 0)),
                      _rb(dgate, bm), _rb(p, bm, 0, CV_W), _rb(p, bm, 1, CV_W), _const(cw)],
                 outs=[_ro(t, 3 * CV_W, F32, bm)], scratch=_conv_scratch(bm))[0]


def _iotas():
    row = lax.broadcasted_iota(jnp.int32, (BLK, BLK), 0)
    col = lax.broadcasted_iota(jnp.int32, (BLK, BLK), 1)
    return row, col


def _heads(x, head0):
    if head0.shape != x.shape:
        head0 = lax.broadcasted_iota(jnp.int32, x.shape, 1) < HEAD_DIM
    return jnp.where(head0, x, 0.0).astype(BF16), jnp.where(head0, 0.0, x).astype(BF16)


def _pair_spec(seq, off):
    return pl.BlockSpec((seq, BLK), lambda b, hp: (b, off + hp))


def _stat_spec(seq):
    return pl.BlockSpec((None, None, seq, BLK), lambda b, hp: (b, hp, 0, 0))


_ATT_PARAMS = dict(compiler_params=pltpu.CompilerParams(dimension_semantics=("parallel", "parallel")))
_SCALE = 1.0 / math.sqrt(HEAD_DIM)


Q_BLOCK = 256
KEY_BLOCK = 256


def _stack_heads(x, head0, scale=None):
    if scale is not None:
        x = x * scale
    return jnp.concatenate(_heads(x, head0), axis=0)


def _pair_cols(x, head0, fill):
    a = jnp.max(jnp.where(head0, x, fill), axis=1, keepdims=True)
    b = jnp.max(jnp.where(head0, fill, x), axis=1, keepdims=True)
    return jnp.concatenate([a, b], axis=0)


def _causal_mask(t0, s0, tq, kw, inclusive):
    row = lax.broadcasted_iota(jnp.int32, (2 * tq, kw), 0) & (tq - 1)
    col = lax.broadcasted_iota(jnp.int32, (2 * tq, kw), 1)
    return (s0 + col) <= (t0 + row) if inclusive else (s0 + col) < (t0 + row)


def _sub(x, j):
    return x[:, j * BLK:(j + 1) * BLK]


def _block_cumsum(x, tri, ksub):
    hi = x.astype(BF16)
    lo = (x - hi.astype(F32)).astype(BF16)
    cs = _dot(jnp.concatenate([_sub(pt, j) for pt in (hi, lo) for j in range(ksub)], axis=0), tri)
    n = x.shape[0]
    return ([cs[j * n:(j + 1) * n] + cs[(ksub + j) * n:(ksub + j + 1) * n] for j in range(ksub)],
            [jnp.sum(_sub(x, j), axis=1, keepdims=True) for j in range(ksub)])


def _sb_terms(qs, k, mask):
    return _sb_terms_z(_dot(qs, k, _NT), mask)


def _sb_terms_z(z, mask):
    t = jnp.log(1.0 + jnp.exp(-jnp.abs(z)))
    lsz = jnp.minimum(z, 0.0) - t
    lr = lsz - z
    if mask is not None:
        lr = jnp.where(mask, lr, 0.0)
    return lsz, lr


def _sb_fwd(p, nb, seq, *, name):
    tq = min(Q_BLOCK, seq)
    nq = seq // tq
    kw = min(KEY_BLOCK, seq)
    ksub = kw // BLK

    def body(q_ref, k_ref, v_ref, o_ref, tot_ref):
        row, col = _iotas()
        colq = lax.broadcasted_iota(jnp.int32, (tq, BLK), 1)
        head0 = colq < HEAD_DIM
        upper = (row > col).astype(BF16)

        def qblock(qb, c):
            t0 = pl.multiple_of(qb * tq, tq)
            qs = _stack_heads(q_ref[pl.ds(t0, tq), :], head0, _SCALE)
            diag = (t0 + tq - 1) // kw

            def kblock(kb, carry, masked):
                acc, run = carry
                s0 = pl.multiple_of(kb * kw, kw)
                k = k_ref[pl.ds(s0, kw), :].astype(BF16)
                v0, v1 = _heads(v_ref[pl.ds(s0, kw), :], head0)
                mask = _causal_mask(t0, s0, tq, kw, False)[:tq] if masked else None
                zs = [_dot(qs[h * tq:(h + 1) * tq], k, _NT) for h in range(2)]
                terms = []
                for h in range(2):
                    lsz, lr = _sb_terms_z(zs[h], mask)
                    terms.append((lsz,) + _block_cumsum(lr, upper, ksub))
                runs = []
                for h, vh in enumerate((v0, v1)):
                    lsz, after, total = terms[h]
                    r = run[h]
                    ws = [None] * ksub
                    for j in reversed(range(ksub)):
                        w = jnp.exp(_sub(lsz, j) + after[j] + r)
                        if masked:
                            w = jnp.where(_sub(mask, j), w, 0.0)
                        ws[j] = w.astype(BF16)
                        r = r + total[j]
                    acc = acc + _dot(jnp.concatenate(ws, axis=1), vh)
                    runs.append(r)
                return acc, tuple(runs)

            zc = jnp.zeros((tq, 1), F32)
            carry = kblock(diag, (jnp.zeros((tq, BLK), F32), (zc, zc)), True)
            acc, run = lax.fori_loop(0, diag, lambda it, cr: kblock(diag - 1 - it, cr, False), carry)
            o_ref[pl.ds(t0, tq), :] = acc
            tot_ref[pl.ds(t0, tq), :] = jnp.where(head0, run[0], run[1])
            return c

        lax.fori_loop(0, nq, qblock, 0)

    return pl.pallas_call(
        body, name=name, grid=(nb, PAIRS),
        in_specs=[_pair_spec(seq, 0), _pair_spec(seq, PAIRS), _pair_spec(seq, 2 * PAIRS)],
        out_specs=[_pair_spec(seq, 0), _stat_spec(seq)],
        out_shape=[jax.ShapeDtypeStruct((nb * seq, D_MODEL), F32), jax.ShapeDtypeStruct((nb, PAIRS, seq, BLK), F32)],
        **_ATT_PARAMS,
    )(p, p, p)


def _sb_bwd(p, do, tot, nb, seq, *, name):
    tq = min(Q_BLOCK, seq)
    nq = seq // tq
    kw = min(KEY_BLOCK, seq)
    ksub = kw // BLK

    def body(q_ref, k_ref, v_ref, do_ref, tot_ref, dq_ref, dk_ref, dv_ref):
        row, col = _iotas()
        colq = lax.broadcasted_iota(jnp.int32, (tq, BLK), 1)
        head0 = colq < HEAD_DIM
        lower_incl = (row <= col).astype(BF16)
        lower_strict = (row < col).astype(BF16)
        dk_ref[...] = jnp.zeros(dk_ref.shape, F32)
        dv_ref[...] = jnp.zeros(dv_ref.shape, F32)

        def qblock(qb, c):
            t0 = pl.multiple_of(qb * tq, tq)
            qs = _stack_heads(q_ref[pl.ds(t0, tq), :], head0, _SCALE)
            dos = _stack_heads(do_ref[pl.ds(t0, tq), :], head0)
            tot = tot_ref[pl.ds(t0, tq), :]
            swapped = pltpu.roll(tot, HEAD_DIM, 1)
            tts = (jnp.where(head0, tot, swapped), jnp.where(head0, swapped, tot))
            diag = (t0 + tq - 1) // kw

            def kblock(kb, carry, masked):
                dq, pfs, efs = carry
                s0 = pl.multiple_of(kb * kw, kw)
                kf = k_ref[pl.ds(s0, kw), :]
                k = kf.astype(BF16)
                khs = _heads(kf, head0)
                v = v_ref[pl.ds(s0, kw), :].astype(BF16)
                mask = _causal_mask(t0, s0, tq, kw, False)[:tq] if masked else None
                zs = [_dot(qs[h * tq:(h + 1) * tq], k, _NT) for h in range(2)]
                dws = [_dot(dos[h * tq:(h + 1) * tq], v, _NT) for h in range(2)]
                first = []
                for h in range(2):
                    lsz, lr = _sb_terms_z(zs[h], None)
                    lrm = jnp.where(mask, lr, 0.0) if masked else lr
                    first.append((lsz, lr) + _block_cumsum(lrm, lower_incl, ksub))
                second, pfs_out = [], []
                for h in range(2):
                    lsz, lr, incl, total = first[h]
                    pf = pfs[h]
                    ws, ews = [], []
                    for j in range(ksub):
                        w = jnp.exp(_sub(lsz, j) + (tts[h] - pf - incl[j]))
                        if masked:
                            w = jnp.where(_sub(mask, j), w, 0.0)
                        pf = pf + total[j]
                        ws.append(w.astype(BF16))
                        ews.append(_sub(dws[h], j) * w)
                    pfs_out.append(pf)
                    second.append((ws, ews) + _block_cumsum(jnp.concatenate(ews, axis=1), lower_strict, ksub))
                dz_h, efs_out = [], []
                for h in range(2):
                    lsz, lr = first[h][:2]
                    ws, ews, before, etotal = second[h]
                    ef = efs[h]
                    dzs = []
                    for j in range(ksub):
                        dz = ews[j] * jnp.exp(_sub(lr, j)) - (ef + before[j]) * jnp.exp(_sub(lsz, j))
                        ef = ef + etotal[j]
                        if masked:
                            dz = jnp.where(_sub(mask, j), dz, 0.0)
                        dzs.append(dz.astype(BF16))
                    efs_out.append(ef)
                    dz_h.append(jnp.concatenate(dzs, axis=1))
                    dq = dq + _dot(dz_h[h], khs[h])
                w = jnp.concatenate([jnp.concatenate(second[h][0], axis=1) for h in range(2)], axis=0)
                dk_ref[pl.ds(s0, kw), :] += _dot(jnp.concatenate(dz_h, axis=0), qs, _TN)
                dv_ref[pl.ds(s0, kw), :] += _dot(w, dos, _TN)
                return dq, tuple(pfs_out), tuple(efs_out)

            zc = jnp.zeros((tq, 1), F32)
            carry = lax.fori_loop(0, diag, lambda kb, cr: kblock(kb, cr, False), (jnp.zeros((tq, BLK), F32), (zc, zc), (zc, zc)))
            dq_ref[pl.ds(t0, tq), :] = kblock(diag, carry, True)[0] * _SCALE
            return c

        lax.fori_loop(0, nq, qblock, 0)

    t = nb * seq
    return pl.pallas_call(
        body, name=name, grid=(nb, PAIRS),
        in_specs=[_pair_spec(seq, 0), _pair_spec(seq, PAIRS), _pair_spec(seq, 2 * PAIRS), _pair_spec(seq, 0), _stat_spec(seq)],
        out_specs=[_pair_spec(seq, 0)] * 3,
        out_shape=[jax.ShapeDtypeStruct((t, D_MODEL), F32)] * 3,
        **_ATT_PARAMS,
    )(p, p, p, do, tot)


def _fox_cum(f, bf, nb, seq, *, name):
    def body(f_ref, bf_ref, cc_ref, cr_ref):
        row, col = _iotas()
        lower = (col <= row).astype(BF16)
        carry = jnp.zeros((1, BLK), F32)
        for blk in range(seq // BLK):
            rs = slice(blk * BLK, (blk + 1) * BLK)
            lf = jnp.where(col < HEADS, _log_sigmoid(f_ref[rs, :] + bf_ref[...]), 0.0)
            cc = _dot3_left(lower, lf) + carry
            cc_ref[rs, :] = cc
            cr_ref[:, rs] = cc.T[0:HEADS, :]
            carry = carry + _colsum(lf)

    return pl.pallas_call(
        body, name=name, grid=(nb,),
        in_specs=[pl.BlockSpec((seq, BLK), lambda b: (b, 0)), pl.BlockSpec((1, BLK), lambda b: (0, 0))],
        out_specs=[pl.BlockSpec((seq, BLK), lambda b: (b, 0)), pl.BlockSpec((None, HEADS, seq), lambda b: (b, 0, 0))],
        out_shape=[jax.ShapeDtypeStruct((nb * seq, BLK), F32), jax.ShapeDtypeStruct((nb, HEADS, seq), F32)],
        compiler_params=pltpu.CompilerParams(dimension_semantics=("parallel",)),
    )(f, bf)


def _fox_cum_bwd(dcr, dcc, f, bf, nb, seq, *, name):
    def body(dcr_ref, dcc_ref, f_ref, bf_ref, df_ref, dbf_ref):
        row, col = _iotas()
        upper_incl = (col >= row).astype(BF16)

        @pl.when(pl.program_id(0) == 0)
        def _():
            dbf_ref[...] = jnp.zeros((1, BLK), F32)

        carry = jnp.zeros((1, BLK), F32)
        for blk in reversed(range(seq // BLK)):
            rs = slice(blk * BLK, (blk + 1) * BLK)
            dc = dcr_ref[:, rs].T + dcc_ref[rs, :]
            dlf = _dot3_left(upper_incl, dc) + carry
            carry = carry + _colsum(dc)
            fl = f_ref[rs, :] + bf_ref[...]
            df = jnp.where(col < HEADS, dlf * _sigmoid(-fl), 0.0)
            df_ref[rs, :] = df
            dbf_ref[...] += _colsum(df)

    return pl.pallas_call(
        body, name=name, grid=(nb,),
        in_specs=[pl.BlockSpec((None, BLK, seq), lambda b: (b, 0, 0)), pl.BlockSpec((seq, BLK), lambda b: (b, 0)),
                  pl.BlockSpec((seq, BLK), lambda b: (b, 0)), pl.BlockSpec((1, BLK), lambda b: (0, 0))],
        out_specs=[pl.BlockSpec((seq, BLK), lambda b: (b, 0)), pl.BlockSpec((1, BLK), lambda b: (0, 0))],
        out_shape=[jax.ShapeDtypeStruct((nb * seq, BLK), F32), jax.ShapeDtypeStruct((1, BLK), F32)],
        compiler_params=pltpu.CompilerParams(dimension_semantics=("arbitrary",)),
    )(dcr, dcc, f, bf)


def _fox_cum_cols(cc_ref, t0, tq, colq, hp):
    cc = cc_ref[pl.ds(t0, tq), :]
    c0 = jnp.sum(jnp.where(colq == 2 * hp, cc, 0.0), axis=1, keepdims=True)
    c1 = jnp.sum(jnp.where(colq == 2 * hp + 1, cc, 0.0), axis=1, keepdims=True)
    return c0, c1


def _fox_bias(c0, c1, cr_ref, s0, kw):
    return jnp.concatenate([c0 - cr_ref[0:1, pl.ds(s0, kw)], c1 - cr_ref[1:2, pl.ds(s0, kw)]], axis=0)


def _fox_fwd(p, cc, cr, nb, seq, *, name):
    tq = min(Q_BLOCK, seq)
    nq = seq // tq
    kw = min(KEY_BLOCK, seq)
    ksub = kw // BLK

    def body(q_ref, k_ref, v_ref, cc_ref, cr_ref, o_ref, lse_ref):
        hp = pl.program_id(1)
        row, col = _iotas()
        colq = lax.broadcasted_iota(jnp.int32, (tq, BLK), 1)
        head0 = colq < HEAD_DIM

        def qblock(qb, c):
            t0 = pl.multiple_of(qb * tq, tq)
            qs = _stack_heads(q_ref[pl.ds(t0, tq), :], head0, _SCALE)
            c0, c1 = _fox_cum_cols(cc_ref, t0, tq, colq, hp)
            diag = (t0 + tq - 1) // kw

            def kblock(kb, carry, masked):
                accs, ms = carry
                s0 = pl.multiple_of(kb * kw, kw)
                k = k_ref[pl.ds(s0, kw), :].astype(BF16)
                vf = v_ref[pl.ds(s0, kw), :]
                own0 = lax.broadcasted_iota(jnp.int32, vf.shape, 1) < HEAD_DIM
                vs = (jnp.where(own0, vf, 1.0).astype(BF16), jnp.where(own0, 1.0, vf).astype(BF16))
                mask = _causal_mask(t0, s0, tq, kw, True)[:tq] if masked else None
                zs = [_dot(qs[h * tq:(h + 1) * tq], k, _NT) for h in range(2)]
                parts = []
                for h, ch in enumerate((c0, c1)):
                    s = zs[h] + (ch - cr_ref[h:h + 1, pl.ds(s0, kw)])
                    if masked:
                        s = jnp.where(mask, s, -jnp.inf)
                    m_new = jnp.maximum(ms[h], jnp.max(s, axis=1, keepdims=True))
                    parts.append((jnp.exp(s - m_new).astype(BF16), jnp.exp(ms[h] - m_new), m_new))
                return (tuple(accs[h] * parts[h][1] + _dot(parts[h][0], vs[h]) for h in range(2)),
                        tuple(parts[h][2] for h in range(2)))

            zeros, ninf = jnp.zeros((tq, BLK), F32), jnp.full((tq, 1), -jnp.inf, F32)
            carry = lax.fori_loop(0, diag, lambda kb, cr: kblock(kb, cr, False), ((zeros, zeros), (ninf, ninf)))
            (acc0, acc1), (m0, m1) = kblock(diag, carry, True)
            l = jnp.where(head0, pltpu.roll(acc0, HEAD_DIM, 1), pltpu.roll(acc1, HEAD_DIM, 1))
            o_ref[pl.ds(t0, tq), :] = jnp.where(head0, acc0, acc1) / l
            lse_ref[pl.ds(t0, tq), :] = jnp.where(head0, m0, m1) + jnp.log(l)
            return c

        lax.fori_loop(0, nq, qblock, 0)

    return pl.pallas_call(
        body, name=name, grid=(nb, PAIRS),
        in_specs=[_pair_spec(seq, 0), _pair_spec(seq, PAIRS), _pair_spec(seq, 2 * PAIRS),
                  pl.BlockSpec((seq, BLK), lambda b, hp: (b, 0)), pl.BlockSpec((None, None, 8, seq), lambda b, hp: (b, hp, 0, 0))],
        out_specs=[_pair_spec(seq, 0), _stat_spec(seq)],
        out_shape=[jax.ShapeDtypeStruct((nb * seq, D_MODEL), F32), jax.ShapeDtypeStruct((nb, PAIRS, seq, BLK), F32)],
        **_ATT_PARAMS,
    )(p, p, p, cc, cr)


def _fox_bwd(p, do, o, lse, cc, cr, nb, seq, *, name):
    tq = min(Q_BLOCK, seq)
    nq = seq // tq
    kw = min(KEY_BLOCK, seq)
    ksub = kw // BLK

    def body(q_ref, k_ref, v_ref, do_ref, o_ref, lse_ref, cc_ref, cr_ref, dq_ref, dk_ref, dv_ref, dcr_ref, dcc_ref):
        hp = pl.program_id(1)
        row, col = _iotas()
        colq = lax.broadcasted_iota(jnp.int32, (tq, BLK), 1)
        head0 = colq < HEAD_DIM
        dk_ref[...] = jnp.zeros(dk_ref.shape, F32)
        dv_ref[...] = jnp.zeros(dv_ref.shape, F32)
        dcr_ref[...] = jnp.zeros(dcr_ref.shape, F32)

        @pl.when(hp == 0)
        def _():
            dcc_ref[...] = jnp.zeros(dcc_ref.shape, F32)

        def qblock(qb, c):
            t0 = pl.multiple_of(qb * tq, tq)
            qs = _stack_heads(q_ref[pl.ds(t0, tq), :], head0, _SCALE)
            dof = do_ref[pl.ds(t0, tq), :]
            dos = _stack_heads(dof, head0)
            prod = dof * o_ref[pl.ds(t0, tq), :]
            dl = jnp.concatenate([jnp.sum(jnp.where(head0, prod, 0.0), axis=1, keepdims=True),
                                  jnp.sum(jnp.where(head0, 0.0, prod), axis=1, keepdims=True)], axis=0)
            lse = _pair_cols(lse_ref[pl.ds(t0, tq), :], head0, -jnp.inf)
            c0, c1 = _fox_cum_cols(cc_ref, t0, tq, colq, hp)
            diag = (t0 + tq - 1) // kw

            def kblock(kb, carry, masked):
                dq, rs = carry
                s0 = pl.multiple_of(kb * kw, kw)
                kf = k_ref[pl.ds(s0, kw), :]
                k = kf.astype(BF16)
                k0, k1 = _heads(kf, head0)
                v = v_ref[pl.ds(s0, kw), :].astype(BF16)
                mask = _causal_mask(t0, s0, tq, kw, True)[:tq] if masked else None
                zs = [_dot(qs[h * tq:(h + 1) * tq], k, _NT) for h in range(2)]
                dps = [_dot(dos[h * tq:(h + 1) * tq], v, _NT) for h in range(2)]
                prs, dss, rss = [], [], []
                for h, (ch, kh) in enumerate(((c0, k0), (c1, k1))):
                    rows = slice(h * tq, (h + 1) * tq)
                    pr = jnp.exp(zs[h] + (ch - cr_ref[h:h + 1, pl.ds(s0, kw)]) - lse[rows])
                    if masked:
                        pr = jnp.where(mask, pr, 0.0)
                    ds = pr * (dps[h] - dl[rows])
                    dcr_ref[h:h + 1, pl.ds(s0, kw)] -= _colsum(ds)
                    rss.append(rs[rows] + jnp.sum(ds, axis=1, keepdims=True))
                    prs.append(pr.astype(BF16))
                    dss.append(ds.astype(BF16))
                    dq = dq + _dot(dss[h], kh)
                dk_ref[pl.ds(s0, kw), :] += _dot(jnp.concatenate(dss, axis=0), qs, _TN)
                dv_ref[pl.ds(s0, kw), :] += _dot(jnp.concatenate(prs, axis=0), dos, _TN)
                return dq, jnp.concatenate(rss, axis=0)

            init = (jnp.zeros((tq, BLK), F32), jnp.zeros((2 * tq, 1), F32))
            carry = lax.fori_loop(0, diag, lambda kb, cr: kblock(kb, cr, False), init)
            dq, rs = kblock(diag, carry, True)
            dq_ref[pl.ds(t0, tq), :] = dq * _SCALE
            dcc_ref[pl.ds(t0, tq), :] += jnp.where(colq == 2 * hp, rs[:tq], 0.0) + jnp.where(colq == 2 * hp + 1, rs[tq:], 0.0)
            return c

        lax.fori_loop(0, nq, qblock, 0)

    t = nb * seq
    return pl.pallas_call(
        body, name=name, grid=(nb, PAIRS),
        in_specs=[_pair_spec(seq, 0), _pair_spec(seq, PAIRS), _pair_spec(seq, 2 * PAIRS), _pair_spec(seq, 0), _pair_spec(seq, 0),
                  _stat_spec(seq), pl.BlockSpec((seq, BLK), lambda b, hp: (b, 0)),
                  pl.BlockSpec((None, None, 8, seq), lambda b, hp: (b, hp, 0, 0))],
        out_specs=[_pair_spec(seq, 0)] * 3 + [pl.BlockSpec((None, None, 8, seq), lambda b, hp: (b, hp, 0, 0)),
                                              pl.BlockSpec((seq, BLK), lambda b, hp: (b, 0))],
        out_shape=[jax.ShapeDtypeStruct((t, D_MODEL), F32)] * 3 + [jax.ShapeDtypeStruct((nb, PAIRS, 8, seq), F32),
                                                                     jax.ShapeDtypeStruct((t, BLK), F32)],
        compiler_params=pltpu.CompilerParams(dimension_semantics=("parallel", "arbitrary")),
    )(p, p, p, do, o, lse, cc, cr)


def _row_shards(x):
    return x.reshape(N_CHIPS, x.shape[0] // N_CHIPS, x.shape[1])


def _local_step(x3, tgt3, w, later=None, start_reduce=None):
    nb, seq, d = x3.shape
    t = nb * seq
    x0, tgt = x3.reshape(t, d), tgt3.reshape(t, d)
    g = {}

    a_gain = w["a_norm"].reshape(1, d)
    h_a, ht_a = _rmsnorm(x0, a_gain, name="a_norm_fwd")
    p_a = _matmul(h_a, w["a_w_in"], name="a_in_fwd")
    o_a, tot_a = _sb_fwd(p_a, nb, seq, name="a_attn_fwd")
    y_a, yt_a = _gate(o_a, p_a, 3, name="a_gate_fwd")
    x1 = _matmul(y_a, w["a_w_out"], name="a_out_fwd", residual=x0)

    b_gain = w["b_norm"].reshape(1, d)
    b_lg, b_lb = w["b_v_ln_g"].reshape(1, GM_W), w["b_v_ln_b"].reshape(1, GM_W)
    b_ws, b_bst = w["b_w_s"].reshape(GM_G, BLK, BLK), w["b_b_s"].reshape(GM_G, BLK).T
    h_b, ht_b = _rmsnorm(x1, b_gain, name="b_norm_fwd")
    p_b = _matmul(h_b, w["b_w_in"], name="b_in_fwd")
    y_b, yt_b = _gmlp_fwd(p_b, b_lg, b_lb, b_ws, b_bst, name="b_mix_fwd")
    x2 = _matmul(y_b, w["b_w_out"], name="b_out_fwd", residual=x1)

    if later is not None:
        w = {**w, **later(x2)}
    c_gain = w["c_norm"].reshape(1, d)
    c_cw = jnp.repeat(w["c_conv_w"].reshape(CV_K, CV_W), SUBLANES, axis=0)
    c_cb = w["c_conv_b"].reshape(1, CV_W)
    c_lg, c_lb = w["c_ln_g"].reshape(1, CV_W), w["c_ln_b"].reshape(1, CV_W)
    h_c, ht_c = _rmsnorm(x2, c_gain, name="c_norm_fwd")
    p_c = _matmul(h_c, w["c_w_in"], name="c_in_fwd")
    y_c, yt_c = _conv_fwd(p_c, c_cw, c_cb, c_lg, c_lb, seq, name="c_conv_fwd")
    x3_ = _matmul(y_c, w["c_w_out"], name="c_out_fwd", residual=x2)

    d_gain = w["d_norm"].reshape(1, d)
    d_win = w["d_w_in"].reshape(d, 4 * D_MODEL + HEADS)
    d_wmain = d_win[:, :4 * D_MODEL]
    d_wf = jnp.pad(d_win[:, 4 * D_MODEL:], ((0, 0), (0, BLK - HEADS)))
    d_bf = jnp.pad(w["d_b_f"].reshape(1, HEADS), ((0, 0), (0, BLK - HEADS)))
    h_d, ht_d = _rmsnorm(x3_, d_gain, name="d_norm_fwd")
    p_d = _matmul(h_d, d_wmain, name="d_in_fwd")
    f_d = _matmul(h_d, d_wf, name="d_inf_fwd")
    cc, cr = _fox_cum(f_d, d_bf, nb, seq, name="d_cum_fwd")
    cr = jnp.pad(cr.reshape(nb, PAIRS, 2, seq), ((0, 0), (0, 0), (0, 6), (0, 0)))
    o_d, lse_d = _fox_fwd(p_d, cc, cr, nb, seq, name="d_attn_fwd")
    y_d, yt_d = _gate(o_d, p_d, 3, name="d_gate_fwd")
    x4 = _matmul(y_d, w["d_w_out"], name="d_out_fwd", residual=x3_)

    f_gain = w["final_norm"].reshape(1, d)
    dx, g_fn, loss_row = _loss_head(x4, f_gain, tgt, name="loss_head")
    g["final_norm"] = g_fn

    g["d_w_out"] = _row_shards(_matmul(yt_d, dx, name="d_out_dw"))
    dy = _matmul(dx, w["d_w_out"], name="d_out_dy", mode="nt")
    do_d, dg_d = _gate_bwd(dy, o_d, p_d, 3, name="d_gate_bwd")
    dq, dk, dv, dcr, dcc = _fox_bwd(p_d, do_d, o_d, lse_d, cc, cr, nb, seq, name="d_attn_bwd")
    dcr = jnp.pad(dcr[:, :, :2, :].reshape(nb, HEADS, seq), ((0, 0), (0, BLK - HEADS), (0, 0)))
    df, dbf = _fox_cum_bwd(dcr, dcc, f_d, d_bf, nb, seq, name="d_cum_bwd")
    g["d_b_f"] = dbf[:, :HEADS]
    parts = [dq, dk, dv, dg_d]
    dws = [_matmul(ht_d, pt, name=f"d_in_dw{n}") for n, pt in enumerate(parts)]
    dwf = _matmul(ht_d, df, name="d_inf_dw")
    g["d_w_in"] = jnp.concatenate(dws + [dwf[:, :HEADS]], axis=1).reshape(d, N_CHIPS, -1).transpose(1, 0, 2)
    dh = _matmul(df, d_wf, name="d_inf_dh", mode="nt")
    for n, pt in enumerate(parts):
        dh = _matmul(pt, d_wmain[:, n * D_MODEL:(n + 1) * D_MODEL], name=f"d_in_dh{n}", mode="nt", residual=dh)
    dx, g["d_norm"] = _rmsnorm_bwd(dh, x3_, d_gain, dx, name="d_norm_bwd")

    g["c_w_out"] = _row_shards(_matmul(yt_c, dx, name="c_out_dw"))
    dy = _matmul(dx, w["c_w_out"], name="c_out_dy", mode="nt")
    dy1, dgate, g["c_ln_g"], g["c_ln_b"], g["c_conv_b"], g["c_conv_w"] = _conv_bwd_post(
        dy, p_c, c_cw, c_cb, c_lg, c_lb, seq, name="c_conv_bwd_post")
    dp = _conv_bwd_pre(dy1, dgate, p_c, c_cw, seq, name="c_conv_bwd_pre")
    g["c_w_in"] = _matmul(ht_c, dp, name="c_in_dw", out_shards=N_CHIPS)
    dh = _matmul(dp, w["c_w_in"], name="c_in_dh", mode="nt")
    dx, g["c_norm"] = _rmsnorm_bwd(dh, x2, c_gain, dx, name="c_norm_bwd")

    early, b_wout = None, w["b_w_out"]
    if start_reduce is not None:
        early, token = start_reduce({n: g[n] for n in ("d_w_in", "d_w_out", "c_w_in", "c_w_out")})
        b_wout = b_wout + token[0, 0].astype(b_wout.dtype)
    g["b_w_out"] = _row_shards(_matmul(yt_b, dx, name="b_out_dw"))
    dy = _matmul(dx, b_wout, name="b_out_dy", mode="nt")
    dp, g["b_v_ln_g"], g["b_v_ln_b"], g["b_w_s"], dbst = _gmlp_bwd(dy, p_b, b_lg, b_lb, b_ws, b_bst, name="b_mix_bwd")
    g["b_b_s"] = dbst.T
    g["b_w_in"] = _matmul(ht_b, dp, name="b_in_dw", out_shards=N_CHIPS)
    dh = _matmul(dp, w["b_w_in"], name="b_in_dh", mode="nt")
    dx, g["b_norm"] = _rmsnorm_bwd(dh, x1, b_gain, dx, name="b_norm_bwd")

    g["a_w_out"] = _row_shards(_matmul(yt_a, dx, name="a_out_dw"))
    dy = _matmul(dx, w["a_w_out"], name="a_out_dy", mode="nt")
    do_a, dg_a = _gate_bwd(dy, o_a, p_a, 3, name="a_gate_bwd")
    dq, dk, dv = _sb_bwd(p_a, do_a, tot_a, nb, seq, name="a_attn_bwd")
    parts = [dq, dk, dv, dg_a]
    g["a_w_in"] = jnp.stack([_matmul(ht_a, pt, name=f"a_in_dw{n}") for n, pt in enumerate(parts)])
    dh = None
    for n, pt in enumerate(parts):
        dh = _matmul(pt, w["a_w_in"][n], name=f"a_in_dh{n}", mode="nt", residual=dh)
    dx, g["a_norm"] = _rmsnorm_bwd(dh, x0, a_gain, dx, name="a_norm_bwd")

    return loss_row[0, 0], dx.reshape(nb, seq, d), g, early


_HBM = pl.BlockSpec(memory_space=pltpu.HBM)


def _place():
    return lax.axis_index("x"), lax.axis_index("y"), lax.axis_index("c")


def _other_chips(x, y):
    return [(1 - x, y), (x, 1 - y), (1 - x, 1 - y)]


def _allgather_chips(ss, *, name):
    n_ops = len(ss)

    def body(*refs):
        s_refs, o_refs, (send_sems, recv_sems) = refs[:n_ops], refs[n_ops:2 * n_ops], refs[2 * n_ops:]
        x, y, c = _place()
        me = 2 * x + y
        chips = _other_chips(x, y)

        def copy(i, kk, src, dst, to):
            return pltpu.make_async_remote_copy(src_ref=src, dst_ref=dst, send_sem=send_sems.at[6 * i + kk],
                                                recv_sem=recv_sems.at[6 * i + kk], device_id=to, device_id_type=MESH)

        def half(i, j, hc):
            h = s_refs[i].shape[0] // 2
            return o_refs[i].at[j, pl.ds(hc * h, h), :]

        first = [copy(i, kk, s_refs[i].at[pl.ds(c * (s_refs[i].shape[0] // 2), s_refs[i].shape[0] // 2), :], half(i, me, c),
                      (cx, cy, c)) for kk, (cx, cy) in enumerate(chips) for i in range(n_ops)]
        for cp in first:
            cp.start()
        passed = []
        for kk, (cx, cy) in enumerate(chips):
            for i in range(n_ops):
                blk = half(i, 2 * cx + cy, c)
                copy(i, kk, blk, blk, (cx, cy, c)).wait_recv()
                fwd = copy(i, 3 + kk, blk, blk, (x, y, 1 - c))
                fwd.start()
                passed.append(fwd)
        for kk, (cx, cy) in enumerate(chips):
            for i in range(n_ops):
                blk = half(i, 2 * cx + cy, 1 - c)
                copy(i, 3 + kk, blk, blk, (x, y, 1 - c)).wait_recv()
        for cp in first + passed:
            cp.wait_send()

    for s in ss:
        assert s.shape[0] % 32 == 0, s.shape
    return pl.pallas_call(
        body, name=name, in_specs=[_HBM] * n_ops, out_specs=[_HBM] * n_ops,
        out_shape=[jax.ShapeDtypeStruct((N_CHIPS,) + s.shape, s.dtype) for s in ss],
        scratch_shapes=[pltpu.SemaphoreType.DMA((6 * n_ops,)), pltpu.SemaphoreType.DMA((6 * n_ops,))],
    )(*ss)


_SEM = pl.BlockSpec(memory_space=pltpu.SEMAPHORE)
_ANY = pl.BlockSpec(memory_space=pl.ANY)
_DATAFLOW = pltpu.SideEffectType.DATAFLOW_SIDE_EFFECTING


def _chip_copies(s_refs, land_refs, send_sems, recv_sems):
    x, y, c = _place()
    me = 2 * x + y
    cps = []
    for i, (s_ref, land_ref) in enumerate(zip(s_refs, land_refs)):
        h = s_ref.shape[0] // 2
        for kk, (cx, cy) in enumerate(_other_chips(x, y)):
            cps.append(pltpu.make_async_remote_copy(
                src_ref=s_ref.at[pl.ds(c * h, h), :], dst_ref=land_ref.at[me, pl.ds(c * h, h), :], send_sem=send_sems.at[3 * i + kk],
                recv_sem=recv_sems.at[3 * i + kk], device_id=(cx, cy, c), device_id_type=MESH))
    return cps


def _gather_start(ss, after, *, name):
    n = len(ss)
    lands = [lax.empty((N_CHIPS,) + s.shape, s.dtype) for s in ss]

    def body(*refs):
        s_refs, land_refs = refs[:n], refs[n:2 * n]
        send_sems, recv_sems = refs[2 * n + 1], refs[2 * n + 2]
        token = refs[-1]
        for cp in _chip_copies(s_refs, land_refs, send_sems, recv_sems):
            cp.start()
        token[...] = jnp.zeros(token.shape, token.dtype)

    hbm = [pltpu.HBM(a.shape, a.dtype) for a in list(ss) + lands]
    res = pl.pallas_call(
        body, name=name,
        out_shape=(pltpu.SemaphoreType.DMA((3 * n,)), pltpu.SemaphoreType.DMA((3 * n,)), *hbm, jax.ShapeDtypeStruct((8, BLK), F32)),
        in_specs=[_HBM] * (2 * n) + [_ANY],
        out_specs=(_SEM, _SEM, *([_HBM] * (2 * n)), pl.BlockSpec(memory_space=pltpu.VMEM)),
        input_output_aliases={i: 2 + i for i in range(2 * n)},
        compiler_params=pltpu.CompilerParams(has_side_effects=_DATAFLOW),
    )(*[pltpu.with_memory_space_constraint(a, pltpu.HBM) for a in list(ss) + lands], after)
    return res[:-1], res[-1]


def _gather_wait(started, after, *, name):
    send_sems, recv_sems = started[0], started[1]
    n = (len(started) - 2) // 2

    def body(*refs):
        s_refs, land_refs = refs[:n], refs[n:2 * n]
        for cp in _chip_copies(s_refs, land_refs, refs[2 * n], refs[2 * n + 1]):
            cp.wait_send()
            cp.wait_recv()

    res = pl.pallas_call(
        body, name=name, out_shape=tuple(pltpu.HBM(a.shape, a.dtype) for a in started[2:]),
        in_specs=[_HBM] * (2 * n) + [_SEM, _SEM, _ANY], out_specs=tuple([_HBM] * (2 * n)),
        input_output_aliases={i: i for i in range(2 * n)},
        compiler_params=pltpu.CompilerParams(has_side_effects=_DATAFLOW),
    )(*started[2:], send_sems, recv_sems, after)
    return list(res[n:])


def _sibling_exchange(lands, *, name):
    n = len(lands)

    def body(*refs):
        o_refs, (send_sems, recv_sems) = refs[n:2 * n], refs[2 * n:]
        x, y, c = _place()
        cps = []
        for i, o_ref in enumerate(o_refs):
            h = o_ref.shape[1] // 2
            for kk, (cx, cy) in enumerate(_other_chips(x, y)):
                def half(hc):
                    return o_ref.at[2 * cx + cy, pl.ds(hc * h, h), :]
                sent = pltpu.make_async_remote_copy(src_ref=half(c), dst_ref=half(c), send_sem=send_sems.at[3 * i + kk],
                                                    recv_sem=recv_sems.at[3 * i + kk], device_id=(x, y, 1 - c), device_id_type=MESH)
                awaited = pltpu.make_async_remote_copy(src_ref=half(1 - c), dst_ref=half(1 - c), send_sem=send_sems.at[3 * i + kk],
                                                       recv_sem=recv_sems.at[3 * i + kk], device_id=(x, y, 1 - c),
                                                       device_id_type=MESH)
                cps.append((sent, awaited))
        for sent, _ in cps:
            sent.start()
        for sent, awaited in cps:
            awaited.wait_recv()
            sent.wait_send()

    return pl.pallas_call(
        body, name=name, in_specs=[_HBM] * n, out_specs=[_HBM] * n,
        out_shape=[jax.ShapeDtypeStruct(a.shape, a.dtype) for a in lands], scratch_shapes=_dma_sems(3 * n),
        input_output_aliases={i: i for i in range(n)},
    )(*lands)


def _own_block(gathered, s):
    me = 2 * lax.axis_index("x") + lax.axis_index("y")
    return lax.dynamic_update_slice(gathered, s[None], (me,) + (0,) * s.ndim)


def _dma_sems(n):
    return [pltpu.SemaphoreType.DMA((n,)), pltpu.SemaphoreType.DMA((n,))]


def _swap_halves(gps, *, name):
    n_ops = len(gps)

    def body(*refs):
        g_refs, o_refs, (send_sems, recv_sems) = refs[:n_ops], refs[n_ops:2 * n_ops], refs[2 * n_ops:]
        x, y, c = _place()
        cps = []
        for i, (g_ref, o_ref) in enumerate(zip(g_refs, o_refs)):
            h = g_ref.shape[1] // 2
            cps.append(pltpu.make_async_remote_copy(
                src_ref=g_ref.at[:, pl.ds((1 - c) * h, h), :], dst_ref=o_ref, send_sem=send_sems.at[i], recv_sem=recv_sems.at[i],
                device_id=(x, y, 1 - c), device_id_type=MESH))
        for cp in cps:
            cp.start()
        for cp in cps:
            cp.wait()

    return pl.pallas_call(
        body, name=name, in_specs=[_HBM] * n_ops, out_specs=[_HBM] * n_ops,
        out_shape=[jax.ShapeDtypeStruct((g.shape[0], g.shape[1] // 2, g.shape[2]), g.dtype) for g in gps],
        scratch_shapes=_dma_sems(n_ops),
    )(*gps)


def _scatter_chips(hps, *, name):
    n_ops = len(hps)

    def body(*refs):
        h_refs, o_refs, (send_sems, recv_sems) = refs[:n_ops], refs[n_ops:2 * n_ops], refs[2 * n_ops:]
        x, y, c = _place()
        cps = [pltpu.make_async_remote_copy(src_ref=h_ref.at[2 * cx + cy], dst_ref=o_ref.at[kk], send_sem=send_sems.at[3 * i + kk],
                                            recv_sem=recv_sems.at[3 * i + kk], device_id=(cx, cy, c), device_id_type=MESH)
               for i, (h_ref, o_ref) in enumerate(zip(h_refs, o_refs)) for kk, (cx, cy) in enumerate(_other_chips(x, y))]
        for cp in cps:
            cp.start()
        for cp in cps:
            cp.wait()

    return pl.pallas_call(
        body, name=name, in_specs=[_HBM] * n_ops, out_specs=[_HBM] * n_ops,
        out_shape=[jax.ShapeDtypeStruct((3,) + hp.shape[1:], hp.dtype) for hp in hps],
        scratch_shapes=_dma_sems(3 * n_ops),
    )(*hps)


def _join_halves(fs, *, name):
    n_ops = len(fs)

    def body(*refs):
        f_refs, o_refs, (send_sems, recv_sems) = refs[:n_ops], refs[n_ops:2 * n_ops], refs[2 * n_ops:]
        x, y, c = _place()
        cps = [pltpu.make_async_remote_copy(src_ref=f_ref, dst_ref=o_ref, send_sem=send_sems.at[i], recv_sem=recv_sems.at[i],
                                            device_id=(x, y, 1 - c), device_id_type=MESH)
               for i, (f_ref, o_ref) in enumerate(zip(f_refs, o_refs))]
        for cp in cps:
            cp.start()
        for cp in cps:
            cp.wait()

    theirs = pl.pallas_call(
        body, name=name, in_specs=[_HBM] * n_ops, out_specs=[_HBM] * n_ops,
        out_shape=[jax.ShapeDtypeStruct(f.shape, f.dtype) for f in fs], scratch_shapes=_dma_sems(n_ops),
    )(*fs)
    south = lax.axis_index("c") == 0
    return [jnp.concatenate([jnp.where(south, f, t), jnp.where(south, t, f)], axis=0) for f, t in zip(fs, theirs)]


def _add_halves(gp, ra, wire_dtype, *, name, bm=256):
    n, r, c_ = gp.shape
    h = r // 2
    bm = _tile(h, bm)
    per = h // bm
    c = lax.axis_index("c").astype(jnp.int32).reshape(1)

    def body(c_ref, g_ref, ra_ref, o_ref, ow_ref):
        s = g_ref[...] + ra_ref[...]
        o_ref[...] = s
        ow_ref[...] = s.astype(wire_dtype)

    mine = pl.BlockSpec((None, bm, c_), lambda j, i, cr: (j, i, 0))
    return pl.pallas_call(
        body, name=name,
        grid_spec=pltpu.PrefetchScalarGridSpec(
            num_scalar_prefetch=1, grid=(n, per),
            in_specs=[pl.BlockSpec((None, bm, c_), lambda j, i, cr: (j, cr[0] * per + i, 0)), mine],
            out_specs=[mine, mine]),
        out_shape=[jax.ShapeDtypeStruct((n, h, c_), F32), jax.ShapeDtypeStruct((n, h, c_), wire_dtype)],
        compiler_params=pltpu.CompilerParams(dimension_semantics=("parallel", "parallel")),
    )(c, gp, ra)


def _add_chips(hp, rb, *, name, bm=256):
    n, h, c_ = hp.shape
    bm = _tile(h, bm)
    me = (2 * lax.axis_index("x") + lax.axis_index("y")).astype(jnp.int32).reshape(1)

    def body(me_ref, h_ref, rb_ref, o_ref):
        o_ref[...] = ((h_ref[...] + rb_ref[0].astype(F32)) + rb_ref[1].astype(F32)) + rb_ref[2].astype(F32)

    return pl.pallas_call(
        body, name=name,
        grid_spec=pltpu.PrefetchScalarGridSpec(
            num_scalar_prefetch=1, grid=(h // bm,),
            in_specs=[pl.BlockSpec((None, bm, c_), lambda i, mr: (mr[0], i, 0)),
                      pl.BlockSpec((3, bm, c_), lambda i, mr: (0, i, 0))],
            out_specs=pl.BlockSpec((bm, c_), lambda i, mr: (i, 0))),
        out_shape=jax.ShapeDtypeStruct((h, c_), F32),
        compiler_params=pltpu.CompilerParams(dimension_semantics=("parallel",)),
    )(me, hp, rb)


def _scatter_copies(h_refs, land_refs, send_sems, recv_sems):
    x, y, c = _place()
    return [pltpu.make_async_remote_copy(src_ref=h_ref.at[2 * cx + cy], dst_ref=land_ref.at[kk], send_sem=send_sems.at[3 * i + kk],
                                         recv_sem=recv_sems.at[3 * i + kk], device_id=(cx, cy, c), device_id_type=MESH)
            for i, (h_ref, land_ref) in enumerate(zip(h_refs, land_refs)) for kk, (cx, cy) in enumerate(_other_chips(x, y))]


def _scatter_start(hps, after, *, name):
    n = len(hps)
    lands = [lax.empty((3,) + hp.shape[1:], hp.dtype) for hp in hps]

    def body(*refs):
        for cp in _scatter_copies(refs[:n], refs[n:2 * n], refs[2 * n + 1], refs[2 * n + 2]):
            cp.start()
        refs[-1][...] = jnp.zeros(refs[-1].shape, refs[-1].dtype)

    hbm = [pltpu.HBM(a.shape, a.dtype) for a in list(hps) + lands]
    res = pl.pallas_call(
        body, name=name,
        out_shape=(pltpu.SemaphoreType.DMA((3 * n,)), pltpu.SemaphoreType.DMA((3 * n,)), *hbm, jax.ShapeDtypeStruct((8, BLK), F32)),
        in_specs=[_HBM] * (2 * n) + [_ANY],
        out_specs=(_SEM, _SEM, *([_HBM] * (2 * n)), pl.BlockSpec(memory_space=pltpu.VMEM)),
        input_output_aliases={i: 2 + i for i in range(2 * n)},
        compiler_params=pltpu.CompilerParams(has_side_effects=_DATAFLOW),
    )(*[pltpu.with_memory_space_constraint(a, pltpu.HBM) for a in list(hps) + lands], after)
    return res[:-1], res[-1]


def _scatter_wait(started, after, *, name):
    n = (len(started) - 2) // 2

    def body(*refs):
        for cp in _scatter_copies(refs[:n], refs[n:2 * n], refs[2 * n], refs[2 * n + 1]):
            cp.wait_send()
            cp.wait_recv()

    res = pl.pallas_call(
        body, name=name, out_shape=tuple(pltpu.HBM(a.shape, a.dtype) for a in started[2:]),
        in_specs=[_HBM] * (2 * n) + [_SEM, _SEM, _ANY], out_specs=tuple([_HBM] * (2 * n)),
        input_output_aliases={i: i for i in range(2 * n)},
        compiler_params=pltpu.CompilerParams(has_side_effects=_DATAFLOW),
    )(*started[2:], started[0], started[1], after)
    return list(res[n:])


def _reduce_to_chips(gps, wire_dtypes, *, tag):
    ras = _swap_halves(gps, name=f"{tag}_swap_halves")
    return [_add_halves(gp, ra, wd, name=f"{tag}_add_halves{i}") for i, (gp, ra, wd) in enumerate(zip(gps, ras, wire_dtypes))]


def _start_reduce(early):
    names = list(early)
    hps = _reduce_to_chips([early[n] for n in names], [BF16] * len(names), tag="grads_early")
    started, token = _scatter_start([hw for _, hw in hps], hps[-1][1], name="grads_early_scatter_start")
    return (names, [hf for hf, _ in hps], started), token


def _adamw(w, g, m, v, *, name):
    r, c_ = w.shape
    bm = r
    for cand in (512, 256, 128, 64, 32, 16, 8):
        if r % cand == 0:
            bm = cand
            break
    c1 = 1.0 - ADAM_B1 ** ADAM_STEP
    c2 = 1.0 - ADAM_B2 ** ADAM_STEP

    def body(w_ref, g_ref, m_ref, v_ref, d_ref, nm_ref, nv_ref):
        g_ = g_ref[...]
        m_ = ADAM_B1 * m_ref[...] + (1.0 - ADAM_B1) * g_
        v_ = ADAM_B2 * v_ref[...] + (1.0 - ADAM_B2) * (g_ * g_)
        d_ref[...] = -ADAM_LR * ((m_ / c1) / (jnp.sqrt(v_ / c2) + ADAM_EPS) + ADAM_WD * w_ref[...])
        nm_ref[...] = m_
        nv_ref[...] = v_

    spec = pl.BlockSpec((bm, c_), lambda i: (i, 0))
    return pl.pallas_call(
        body, name=name, grid=(r // bm,), in_specs=[spec] * 4, out_specs=[spec] * 3,
        out_shape=[jax.ShapeDtypeStruct((r, c_), F32)] * 3,
        compiler_params=pltpu.CompilerParams(dimension_semantics=("parallel",)),
    )(w, g, m, v)


_WEIGHTS = ["a_norm", "a_w_in", "a_w_out", "b_norm", "b_w_in", "b_v_ln_g", "b_v_ln_b", "b_w_s", "b_b_s", "b_w_out",
            "c_norm", "c_w_in", "c_conv_w", "c_conv_b", "c_ln_g", "c_ln_b", "c_w_out", "d_norm", "d_w_in", "d_b_f",
            "d_w_out", "final_norm"]
_SHARD_AXIS = {"a_norm": None, "a_w_in": 2, "a_w_out": 1, "b_norm": 1, "b_w_in": 2, "b_v_ln_g": 1, "b_v_ln_b": 1, "b_w_s": None,
               "b_b_s": None, "b_w_out": 1, "c_norm": 1, "c_w_in": 2, "c_conv_w": 2, "c_conv_b": 1, "c_ln_g": 1, "c_ln_b": 1,
               "c_w_out": 1, "d_norm": 1, "d_w_in": 2, "d_b_f": None, "d_w_out": 1, "final_norm": None}
_BIG = ["a_w_in", "a_w_out", "b_w_in", "b_w_out", "c_w_in", "c_w_out", "d_w_in", "d_w_out"]
_EARLY = ("a_w_in", "a_w_out", "b_w_in", "b_w_out")
_SMALL_SHARDED = [n for n in _WEIGHTS if _SHARD_AXIS[n] is not None and n not in _BIG]
_REPLICATED = [n for n in _WEIGHTS if _SHARD_AXIS[n] is None]
_ROW_ALIGN = 32
_ROW_ALIGN_SUMMED = 128


def _pack(pieces, dtype, align=_ROW_ALIGN):
    flat = jnp.concatenate([p.reshape(-1).astype(dtype) for p in pieces])
    unit = align * PACK_C
    total = -(-flat.shape[0] // unit) * unit
    return jnp.pad(flat, (0, total - flat.shape[0])).reshape(total // PACK_C, PACK_C)


def _unpack(flat, shapes):
    out, off = [], 0
    for s in shapes:
        n = math.prod(s)
        out.append(flat[off:off + n].reshape(s))
        off += n
    return out


def _full_shape(local_shape, axis):
    s = list(local_shape)
    if axis is not None:
        s[axis] *= N_CHIPS
    return tuple(s)


def _gather_weights(local):
    def whole(n, gt):
        if _SHARD_AXIS[n] == 1:
            return gt.reshape(-1, gt.shape[-1])
        if n == "d_w_in":
            return gt.transpose(1, 0, 2).reshape(gt.shape[1], -1)
        return gt

    full = {n: local[n][0] if n != "final_norm" else local[n] for n in _REPLICATED}
    first = [n for n in _BIG if n in _EARLY]
    later = [n for n in _BIG if n not in _EARLY]
    mine = [local[n][0].astype(BF16) for n in first] + [_pack([local[n] for n in _SMALL_SHARDED], F32)]
    got = [_own_block(gt, s) for gt, s in zip(_allgather_chips(mine, name="gather_weights"), mine)]
    full.update({n: whole(n, gt) for n, gt in zip(first, got)})
    small = got[-1].reshape(N_CHIPS, -1)
    shards = [_unpack(small[j], [local[n].shape[1:] for n in _SMALL_SHARDED]) for j in range(N_CHIPS)]
    for i, n in enumerate(_SMALL_SHARDED):
        full[n] = jnp.concatenate([shards[j][i] for j in range(N_CHIPS)], axis=_SHARD_AXIS[n] - 1)

    mine_later = [local[n][0].astype(BF16) for n in later]
    started, token = _gather_start(mine_later, got[0], name="gather_later_start")
    full["a_norm"] = full["a_norm"] + token[0, 0]

    def rest(after):
        lands = _sibling_exchange(_gather_wait(started, after, name="gather_later_wait"), name="gather_later_exchange")
        return {n: whole(n, _own_block(gt, s)) for n, gt, s in zip(later, lands, mine_later)}

    return full, rest


def _repl_piece_len(local):
    total = sum(math.prod(local[n].shape) for n in _REPLICATED)
    return -(-total // N_CHIPS)


def _reduce_grads(g, local, early):
    rep_flat = jnp.concatenate([g[n].reshape(-1) for n in _REPLICATED])
    piece = _repl_piece_len(local)
    rep_flat = jnp.pad(rep_flat, (0, N_CHIPS * piece - rep_flat.shape[0]))

    def shard(n, j):
        full = g[n].reshape(_full_shape(local[n].shape, _SHARD_AXIS[n]))
        width = local[n].shape[_SHARD_AXIS[n]]
        return lax.slice_in_dim(full, j * width, (j + 1) * width, axis=_SHARD_AXIS[n])

    small = jnp.stack([_pack([shard(n, j) for n in _SMALL_SHARDED] + [rep_flat[j * piece:(j + 1) * piece]], F32)
                       for j in range(N_CHIPS)])
    early_names, early_halves, started = early
    late = [n for n in _BIG if n not in early_names]
    hps = _reduce_to_chips([g[n] for n in late] + [small], [BF16] * len(late) + [F32], tag="grads")
    rbs = list(_scatter_chips([hw for _, hw in hps], name="grads_scatter_chips"))
    early_rbs = _scatter_wait(started, rbs[0], name="grads_early_scatter_wait")
    halves = early_halves + [hf for hf, _ in hps]
    fs = [_add_chips(hf, rb, name=f"grads_add_chips{i}") for i, (hf, rb) in enumerate(zip(halves, early_rbs + rbs))]
    summed = _join_halves(fs, name="grads_join_halves")
    red = {n: s.reshape(local[n].shape) for n, s in zip(early_names + late, summed)}
    out = _unpack(summed[-1].reshape(-1), [local[n].shape for n in _SMALL_SHARDED] + [(piece,)])
    red.update(zip(_SMALL_SHARDED, out[:-1]))
    rep_mine = _pack([out[-1]], F32)
    rep = _own_block(_allgather_chips([rep_mine], name="gather_replicated_grads")[0], rep_mine)
    rep = rep.reshape(N_CHIPS, -1)[:, :piece].reshape(-1)
    for n, val in zip(_REPLICATED, _unpack(rep, [local[n].shape for n in _REPLICATED])):
        red[n] = val
    return red


def _update(local, grads, m, v):
    delta, new_m, new_v = {}, {}, {}
    for n in _BIG:
        shp = local[n].shape
        two = (shp[-2], shp[-1])
        res = _adamw(local[n].reshape(two), grads[n].reshape(two), m[n].reshape(two), v[n].reshape(two), name=f"adamw_{n}")
        delta[n], new_m[n], new_v[n] = [r.reshape(shp) for r in res]
    small = [n for n in _WEIGHTS if n not in _BIG]
    shapes = [local[n].shape for n in small]
    packed = [_pack([src[n] for n in small], F32) for src in (local, grads, m, v)]
    res = _adamw(*packed, name="adamw_small")
    for dst, r in zip((delta, new_m, new_v), res):
        for n, val in zip(small, _unpack(r.reshape(-1), shapes)):
            dst[n] = val
    return delta, new_m, new_v


def kernel(x, a_norm, a_w_in, a_w_out, b_norm, b_w_in, b_v_ln_g, b_v_ln_b, b_w_s, b_b_s, b_w_out, c_norm, c_w_in, c_conv_w, c_conv_b, c_ln_g, c_ln_b, c_w_out, d_norm, d_w_in, d_b_f, d_w_out, final_norm, loss_target, m_a_norm, m_a_w_in, m_a_w_out, m_b_norm, m_b_w_in, m_b_v_ln_g, m_b_v_ln_b, m_b_w_s, m_b_b_s, m_b_w_out, m_c_norm, m_c_w_in, m_c_conv_w, m_c_conv_b, m_c_ln_g, m_c_ln_b, m_c_w_out, m_d_norm, m_d_w_in, m_d_b_f, m_d_w_out, m_final_norm, v_a_norm, v_a_w_in, v_a_w_out, v_b_norm, v_b_w_in, v_b_v_ln_g, v_b_v_ln_b, v_b_w_s, v_b_b_s, v_b_w_out, v_c_norm, v_c_w_in, v_c_conv_w, v_c_conv_b, v_c_ln_g, v_c_ln_b, v_c_w_out, v_d_norm, v_d_w_in, v_d_b_f, v_d_w_out, v_final_norm):
    local = dict(zip(_WEIGHTS, (a_norm, a_w_in, a_w_out, b_norm, b_w_in, b_v_ln_g, b_v_ln_b, b_w_s, b_b_s, b_w_out, c_norm, c_w_in,
                                c_conv_w, c_conv_b, c_ln_g, c_ln_b, c_w_out, d_norm, d_w_in, d_b_f, d_w_out, final_norm)))
    m = dict(zip(_WEIGHTS, (m_a_norm, m_a_w_in, m_a_w_out, m_b_norm, m_b_w_in, m_b_v_ln_g, m_b_v_ln_b, m_b_w_s, m_b_b_s, m_b_w_out,
                            m_c_norm, m_c_w_in, m_c_conv_w, m_c_conv_b, m_c_ln_g, m_c_ln_b, m_c_w_out, m_d_norm, m_d_w_in, m_d_b_f,
                            m_d_w_out, m_final_norm)))
    v = dict(zip(_WEIGHTS, (v_a_norm, v_a_w_in, v_a_w_out, v_b_norm, v_b_w_in, v_b_v_ln_g, v_b_v_ln_b, v_b_w_s, v_b_b_s, v_b_w_out,
                            v_c_norm, v_c_w_in, v_c_conv_w, v_c_conv_b, v_c_ln_g, v_c_ln_b, v_c_w_out, v_d_norm, v_d_w_in, v_d_b_f,
                            v_d_w_out, v_final_norm)))
    loss_part, grad_x, g, early = _local_step(x, loss_target, *_gather_weights(local), _start_reduce)
    loss = lax.psum(loss_part, ("x", "y", "c"))
    grads = _reduce_grads(g, local, early)
    delta, new_m, new_v = _update(local, grads, m, v)
    return (loss, grad_x, *[grads[n] for n in _WEIGHTS], *[delta[n] for n in _WEIGHTS],
            *[new_m[n] for n in _WEIGHTS], *[new_v[n] for n in _WEIGHTS])
```

```python
import functools
import math

import jax
import jax.numpy as jnp
from jax import lax
from jax.experimental import pallas as pl
from jax.experimental.pallas import tpu as pltpu

F32, BF16 = jnp.float32, jnp.bfloat16
MESH = pl.DeviceIdType.MESH

D_MODEL = 1024
HEADS = 16
HEAD_DIM = 64
BLK = 128
PAIRS = HEADS // 2
GM_W = 2048
GM_G = 16
CV_W = 2048
CV_K = 31
HALO = 32
EPS = 1e-6
N_CHIPS = 4
PACK_C = 1024
ADAM_LR, ADAM_B1, ADAM_B2, ADAM_EPS, ADAM_WD, ADAM_STEP = 0.001, 0.9, 0.999, 1e-08, 0.01, 10

_NT = (((1,), (1,)), ((), ()))
_TN = (((0,), (0,)), ((), ()))
_NN = (((1,), (0,)), ((), ()))


def _dot(a, b, dims=_NN):
    return lax.dot_general(a, b, dims, preferred_element_type=F32)


def _split3(x):
    hi = x.astype(BF16)
    r = x - hi.astype(F32)
    mid = r.astype(BF16)
    lo = (r - mid.astype(F32)).astype(BF16)
    return hi, mid, lo


def _dot3_right(x, m):
    hi, mid, lo = _split3(x)
    return _dot(hi, m) + _dot(mid, m) + _dot(lo, m)


def _dot3_left(m, x):
    hi, mid, lo = _split3(x)
    return _dot(m, hi) + _dot(m, mid) + _dot(m, lo)


def _sigmoid(x):
    return 1.0 / (1.0 + jnp.exp(-x))


def _silu(x):
    return x * _sigmoid(x)


def _dsilu(x):
    s = _sigmoid(x)
    return s * (1.0 + x * (1.0 - s))


_GELU_C = math.sqrt(2.0 / math.pi)
_GELU_A = 0.044715


def _gelu(x):
    return 0.5 * x * (1.0 + jnp.tanh(_GELU_C * (x + _GELU_A * x * x * x)))


def _dgelu(x):
    t = jnp.tanh(_GELU_C * (x + _GELU_A * x * x * x))
    return 0.5 * (1.0 + t) + 0.5 * x * (1.0 - t * t) * _GELU_C * (1.0 + 3.0 * _GELU_A * x * x)


def _log_sigmoid(x):
    return jnp.minimum(x, 0.0) - jnp.log(1.0 + jnp.exp(-jnp.abs(x)))


def _rms_fwd(x, g):
    r = lax.rsqrt(jnp.mean(x * x, axis=-1, keepdims=True) + EPS)
    return x * r * g


def _rms_bwd(dy, x, g):
    r = lax.rsqrt(jnp.mean(x * x, axis=-1, keepdims=True) + EPS)
    xh = x * r
    dxh = dy * g
    dx = r * (dxh - xh * jnp.mean(dxh * xh, axis=-1, keepdims=True))
    return dx, dy * xh


def _ln_stats(x):
    mu = jnp.mean(x, axis=-1, keepdims=True)
    xc = x - mu
    r = lax.rsqrt(jnp.mean(xc * xc, axis=-1, keepdims=True) + EPS)
    return xc * r, r


def _ln_bwd(dy, xh, r, g):
    dxh = dy * g
    return r * (dxh - jnp.mean(dxh, axis=-1, keepdims=True) - xh * jnp.mean(dxh * xh, axis=-1, keepdims=True))


def _colsum(x):
    return jnp.sum(x, axis=0, keepdims=True)


def _tile(n, want):
    for t in range(min(n, want), 7, -1):
        if n % t == 0 and t % 8 == 0:
            return t
    return n


MM_TILE = 1024


def _matmul(a, b, *, name, mode="nn", residual=None, out_shards=1):
    (m, k) = a.shape
    b_shards = b.shape[0] if b.ndim == 3 else 1
    if mode == "nn":
        n = b.shape[-1] * b_shards
        tn, tk = _tile(n // max(b_shards, out_shards), MM_TILE), _tile(k, MM_TILE)
    else:
        n = b.shape[-2]
        tn, tk = _tile(n // out_shards, MM_TILE), _tile(k // b_shards, MM_TILE)
    tm = _tile(m, MM_TILE)
    nk = k // tk
    a_spec = pl.BlockSpec((tm, tk), lambda i, j, kk: (i, kk))
    if mode == "nn":
        dims = _NN
        if b_shards == 1:
            b_spec = pl.BlockSpec((tk, tn), lambda i, j, kk: (kk, j))
        else:
            per_b = n // b_shards // tn
            b_spec = pl.BlockSpec((None, tk, tn), lambda i, j, kk: (j // per_b, kk, j % per_b))
    else:
        dims = _NT
        if b_shards == 1:
            b_spec = pl.BlockSpec((tn, tk), lambda i, j, kk: (j, kk))
        else:
            per_b = k // b_shards // tk
            b_spec = pl.BlockSpec((None, tn, tk), lambda i, j, kk: (kk // per_b, j, kk % per_b))
    if out_shards == 1:
        o_spec = pl.BlockSpec((tm, tn), lambda i, j, kk: (i, j))
        o_shape = (m, n)
    else:
        per_o = n // out_shards // tn
        o_spec = pl.BlockSpec((None, tm, tn), lambda i, j, kk: (j // per_o, i, j % per_o))
        o_shape = (out_shards, m, n // out_shards)
    has_res = residual is not None

    def body(a_ref, b_ref, *rest):
        o_ref = rest[-1]
        kk = pl.program_id(2)
        part = _dot(a_ref[...].astype(BF16), b_ref[...].astype(BF16), dims)
        if has_res:
            @pl.when(kk == 0)
            def _():
                o_ref[...] = part + rest[0][...]
        else:
            @pl.when(kk == 0)
            def _():
                o_ref[...] = part

        @pl.when(kk > 0)
        def _():
            o_ref[...] += part

    return pl.pallas_call(
        body, name=name, grid=(m // tm, n // tn, nk),
        in_specs=[a_spec, b_spec] + ([o_spec] if has_res else []),
        out_specs=o_spec, out_shape=jax.ShapeDtypeStruct(o_shape, F32),
        compiler_params=pltpu.CompilerParams(dimension_semantics=("parallel", "parallel", "arbitrary")),
    )(a, b, *([residual] if has_res else []))


def _rows(fn, *, name, steps, ins, outs, accs=(), scratch=()):
    ni, no, na = len(ins), len(outs), len(accs)

    def body(*refs):
        in_refs, out_refs = refs[:ni], refs[ni:ni + no]
        acc_refs, scr = refs[ni + no:ni + no + na], refs[ni + no + na:]
        i = pl.program_id(0)

        @pl.when(i == 0)
        def _():
            for r in acc_refs:
                r[...] = jnp.zeros(r.shape, r.dtype)

        fn(i, in_refs, out_refs, acc_refs, scr)

    def full(shape):
        nd = len(shape)
        return pl.BlockSpec(tuple(shape), lambda i: (0,) * nd)

    res = pl.pallas_call(
        body, name=name, grid=(steps,),
        in_specs=[pl.BlockSpec(bs, im) for _, bs, im in ins],
        out_specs=[pl.BlockSpec(bs, im) for _, _, bs, im in outs] + [full(s) for s, _ in accs],
        out_shape=[jax.ShapeDtypeStruct(s, d) for s, d, _, _ in outs] + [jax.ShapeDtypeStruct(s, d) for s, d in accs],
        scratch_shapes=list(scratch),
        compiler_params=pltpu.CompilerParams(dimension_semantics=("arbitrary",)),
    )(*[a for a, _, _ in ins])
    return res


def _rb(arr, bm, cb=0, width=None):
    w = arr.shape[1] if width is None else width
    return (arr, (bm, w), lambda i: (i, cb))


def _const(arr):
    nd = arr.ndim
    return (arr, tuple(arr.shape), lambda i: (0,) * nd)


def _ro(t, w, dtype, bm):
    return ((t, w), dtype, (bm, w), lambda i: (i, 0))


def _rot(t, w, dtype, bm):
    return ((w, t), dtype, (w, bm), lambda i: (0, i))


def _rmsnorm(x, g, *, name, bm=512):
    t, d = x.shape
    bm = _tile(t, bm)

    def fn(i, ins, outs, accs, scr):
        h = _rms_fwd(ins[0][...], ins[1][...])
        outs[0][...] = h.astype(BF16)
        outs[1][...] = h.T.astype(BF16)

    return _rows(fn, name=name, steps=t // bm, ins=[_rb(x, bm), _const(g)], outs=[_ro(t, d, BF16, bm), _rot(t, d, BF16, bm)])


def _rmsnorm_bwd(dh, x, g, dres, *, name, bm=512):
    t, d = x.shape
    bm = _tile(t, bm)

    def fn(i, ins, outs, accs, scr):
        dx, dgrow = _rms_bwd(ins[0][...], ins[1][...], ins[2][...])
        outs[0][...] = ins[3][...] + dx
        accs[0][...] += _colsum(dgrow)

    return _rows(fn, name=name, steps=t // bm, ins=[_rb(dh, bm), _rb(x, bm), _const(g), _rb(dres, bm)],
                 outs=[_ro(t, d, F32, bm)], accs=[((1, d), F32)])


def _gate(o, p, gcb, *, name, bm=512):
    t, w = o.shape
    bm = _tile(t, bm)

    def fn(i, ins, outs, accs, scr):
        y = ins[0][...] * _silu(ins[1][...])
        outs[0][...] = y.astype(BF16)
        outs[1][...] = y.T.astype(BF16)

    return _rows(fn, name=name, steps=t // bm, ins=[_rb(o, bm), _rb(p, bm, gcb, w)],
                 outs=[_ro(t, w, BF16, bm), _rot(t, w, BF16, bm)])


def _gate_bwd(dy, o, p, gcb, *, name, bm=512):
    t, w = o.shape
    bm = _tile(t, bm)

    def fn(i, ins, outs, accs, scr):
        dy_, o_, g_ = ins[0][...], ins[1][...], ins[2][...]
        outs[0][...] = dy_ * _silu(g_)
        outs[1][...] = dy_ * o_ * _dsilu(g_)

    return _rows(fn, name=name, steps=t // bm, ins=[_rb(dy, bm), _rb(o, bm), _rb(p, bm, gcb, w)],
                 outs=[_ro(t, w, F32, bm), _ro(t, w, F32, bm)])


def _loss_head(x, g, tgt, *, name, bm=512):
    t, d = x.shape
    bm = _tile(t, bm)

    def fn(i, ins, outs, accs, scr):
        x_, g_, tg = ins[0][...], ins[1][...], ins[2][...]
        err = _rms_fwd(x_, g_) - tg
        part = 0.5 * jnp.sum(jnp.sum(err * err, axis=-1, keepdims=True), axis=0, keepdims=True) / d
        dx, dgrow = _rms_bwd(err / d, x_, g_)
        outs[0][...] = dx
        accs[0][...] += _colsum(dgrow)
        accs[1][...] += jnp.broadcast_to(part, (1, BLK))

    return _rows(fn, name=name, steps=t // bm, ins=[_rb(x, bm), _const(g), _rb(tgt, bm)],
                 outs=[_ro(t, d, F32, bm)], accs=[((1, d), F32), ((1, BLK), F32)])


def _gmlp_mix_weights(ws_ref, g):
    row = lax.broadcasted_iota(jnp.int32, (BLK, BLK), 0)
    col = lax.broadcasted_iota(jnp.int32, (BLK, BLK), 1)
    tril = col <= row
    return jnp.where(tril, ws_ref[g], 0.0), tril


def _gmlp_fwd(p, ln_g, ln_b, w_s, bs_t, *, name):
    t = p.shape[0]

    def fn(i, ins, outs, accs, scr):
        p_ref, lg, lb, ws_ref, bst = ins
        vn = _ln_stats(_gelu(p_ref[:, GM_W:2 * GM_W]))[0] * lg[...] + lb[...]
        for g in range(GM_G):
            cs = slice(g * BLK, (g + 1) * BLK)
            wt, _ = _gmlp_mix_weights(ws_ref, g)
            s = _dot(wt.astype(BF16), vn[:, cs].astype(BF16)) + bst[:, g:g + 1]
            u = _gelu(p_ref[:, cs])
            gate = p_ref[:, 2 * GM_W + g * BLK:2 * GM_W + (g + 1) * BLK]
            y = u * s * _silu(gate)
            outs[0][:, cs] = y.astype(BF16)
            outs[1][cs, :] = y.T.astype(BF16)

    return _rows(fn, name=name, steps=t // BLK, ins=[_rb(p, BLK), _const(ln_g), _const(ln_b), _const(w_s), _const(bs_t)],
                 outs=[_ro(t, GM_W, BF16, BLK), _rot(t, GM_W, BF16, BLK)])


def _gmlp_bwd(dy, p, ln_g, ln_b, w_s, bs_t, *, name):
    t = p.shape[0]

    def fn(i, ins, outs, accs, scr):
        dy_ref, p_ref, lg, lb, ws_ref, bst = ins
        dp_ref = outs[0]
        dlg, dlb, dws, dbst = accs
        dvn_ref = scr[0]
        v_pre = p_ref[:, GM_W:2 * GM_W]
        xh, r = _ln_stats(_gelu(v_pre))
        vn = xh * lg[...] + lb[...]
        for g in range(GM_G):
            cs = slice(g * BLK, (g + 1) * BLK)
            gs = slice(2 * GM_W + g * BLK, 2 * GM_W + (g + 1) * BLK)
            wt, tril = _gmlp_mix_weights(ws_ref, g)
            vg = vn[:, cs].astype(BF16)
            s = _dot(wt.astype(BF16), vg) + bst[:, g:g + 1]
            u_pre, gate, dyg = p_ref[:, cs], p_ref[:, gs], dy_ref[:, cs]
            u = _gelu(u_pre)
            dos = dyg * _silu(gate)
            dp_ref[:, gs] = dyg * u * s * _dsilu(gate)
            dp_ref[:, cs] = dos * s * _dgelu(u_pre)
            ds = (dos * u).astype(BF16)
            dws[g] += jnp.where(tril, _dot(ds, vg, _NT), 0.0)
            dbst[:, g:g + 1] += jnp.sum(dos * u, axis=1, keepdims=True)
            dvn_ref[:, cs] = _dot(wt.astype(BF16), ds, _TN)
        dvn = dvn_ref[...]
        dlg[...] += _colsum(dvn * xh)
        dlb[...] += _colsum(dvn)
        dp_ref[:, GM_W:2 * GM_W] = _ln_bwd(dvn, xh, r, lg[...]) * _dgelu(v_pre)

    return _rows(fn, name=name, steps=t // BLK,
                 ins=[_rb(dy, BLK), _rb(p, BLK), _const(ln_g), _const(ln_b), _const(w_s), _const(bs_t)],
                 outs=[_ro(t, 3 * GM_W, F32, BLK)],
                 accs=[((1, GM_W), F32), ((1, GM_W), F32), ((GM_G, BLK, BLK), F32), ((BLK, GM_G), F32)],
                 scratch=[pltpu.VMEM((BLK, GM_W), F32)])


CV_BM = 128
CV_RC = 8
SUBLANES = 8
CV_FWD_OFFS = [HALO - (CV_K - 1) + k for k in range(CV_K)]
CV_BWD_OFFS = [CV_K - 1 - k for k in range(CV_K)]


def _conv_halo_prev(p, cb, bm):
    per = bm // HALO
    return (p, (HALO, CV_W), lambda i: (jnp.maximum(i * per - 1, 0), cb))


def _conv_scratch(bm):
    return [pltpu.VMEM((bm + HALO, CV_W), F32), pltpu.VMEM((SUBLANES - 1, bm + HALO - SUBLANES, CV_W), F32),
            pltpu.VMEM((bm, CV_W), F32)]


def _conv_shift_copies(ext_ref, sh_ref):
    rows = sh_ref.shape[1]
    for b in range(1, SUBLANES):
        sh_ref[b - 1] = ext_ref[pl.ds(b, rows), :]


def _conv_window(ext_ref, sh_ref, off, r0, rows):
    b = off % SUBLANES
    src = ext_ref if b == 0 else sh_ref.at[b - 1]
    return src[pl.ds(r0 + (off - b), rows), :]


def _conv_taps(ext_ref, sh_ref, cw_ref, y_ref, offs):
    bm = y_ref.shape[0]

    def chunk(ci, c):
        r0 = pl.multiple_of(ci * CV_RC, CV_RC)
        acc = jnp.zeros((CV_RC, CV_W), F32)
        for k in range(CV_K):
            acc = acc + cw_ref[pl.ds(k * SUBLANES, CV_RC), :] * _conv_window(ext_ref, sh_ref, offs[k], r0, CV_RC)
        y_ref[pl.ds(r0, CV_RC), :] = acc
        return c

    lax.fori_loop(0, bm // CV_RC, chunk, 0)


def _conv_dweights(dy1_ref, ext_ref, sh_ref, dcw_ref):
    bm = dy1_ref.shape[0]
    groups = 4
    for k in range(CV_K):
        def step(ci, acc, off=CV_FWD_OFFS[k]):
            prods = []
            for u in range(groups):
                r0 = pl.multiple_of((ci * groups + u) * CV_RC, CV_RC)
                prods.append(dy1_ref[pl.ds(r0, CV_RC), :] * _conv_window(ext_ref, sh_ref, off, r0, CV_RC))
            return acc + ((prods[0] + prods[1]) + (prods[2] + prods[3]))

        dcw_ref[k:k + 1, :] += _colsum(lax.fori_loop(0, bm // (CV_RC * groups), step, jnp.zeros((CV_RC, CV_W), F32)))


def _conv_fill(i, ext_ref, a_prev, b_prev, a, b, bm, seq):
    keep = jnp.where((i % (seq // bm)) == 0, 0.0, 1.0)
    ext_ref[pl.ds(0, HALO), :] = keep * (a_prev * _sigmoid(b_prev))
    ext_ref[pl.ds(HALO, bm), :] = a * _sigmoid(b)


def _conv_fwd(p, cw, cb, ln_g, ln_b, seq, *, name, bm=CV_BM):
    t = p.shape[0]

    def fn(i, ins, outs, accs, scr):
        a, b, gate, ap, bp = [r[...] for r in ins[:5]]
        cw_ref, cb_, lg, lb = ins[5], ins[6][...], ins[7][...], ins[8][...]
        ext, sh, y = scr
        _conv_fill(i, ext, ap, bp, a, b, bm, seq)
        _conv_shift_copies(ext, sh)
        _conv_taps(ext, sh, cw_ref, y, CV_FWD_OFFS)
        y2 = _ln_stats(y[...] + cb_)[0] * lg + lb
        out = _silu(y2) * _silu(gate)
        outs[0][...] = out.astype(BF16)
        outs[1][...] = out.T.astype(BF16)

    return _rows(fn, name=name, steps=t // bm,
                 ins=[_rb(p, bm, 0, CV_W), _rb(p, bm, 1, CV_W), _rb(p, bm, 2, CV_W),
                      _conv_halo_prev(p, 0, bm), _conv_halo_prev(p, 1, bm),
                      _const(cw), _const(cb), _const(ln_g), _const(ln_b)],
                 outs=[_ro(t, CV_W, BF16, bm), _rot(t, CV_W, BF16, bm)], scratch=_conv_scratch(bm))


def _conv_bwd_post(dy, p, cw, cb, ln_g, ln_b, seq, *, name, bm=CV_BM):
    t = p.shape[0]

    def fn(i, ins, outs, accs, scr):
        dy_, a, b, gate, ap, bp = [r[...] for r in ins[:6]]
        cw_ref, cb_, lg, lb = ins[6], ins[7][...], ins[8][...], ins[9][...]
        dlg, dlb, dcb, dcw = accs
        ext, sh, y = scr
        _conv_fill(i, ext, ap, bp, a, b, bm, seq)
        _conv_shift_copies(ext, sh)
        _conv_taps(ext, sh, cw_ref, y, CV_FWD_OFFS)
        xh, r = _ln_stats(y[...] + cb_)
        y2 = xh * lg + lb
        outs[1][...] = dy_ * _silu(y2) * _dsilu(gate)
        dy2 = dy_ * _silu(gate) * _dsilu(y2)
        dlg[...] += _colsum(dy2 * xh)
        dlb[...] += _colsum(dy2)
        dy1 = _ln_bwd(dy2, xh, r, lg)
        outs[0][...] = dy1
        dcb[...] += _colsum(dy1)
        _conv_dweights(outs[0], ext, sh, dcw)

    return _rows(fn, name=name, steps=t // bm,
                 ins=[_rb(dy, bm), _rb(p, bm, 0, CV_W), _rb(p, bm, 1, CV_W), _rb(p, bm, 2, CV_W),
                      _conv_halo_prev(p, 0, bm), _conv_halo_prev(p, 1, bm),
                      _const(cw), _const(cb), _const(ln_g), _const(ln_b)],
                 outs=[_ro(t, CV_W, F32, bm), _ro(t, CV_W, F32, bm)],
                 accs=[((1, CV_W), F32), ((1, CV_W), F32), ((1, CV_W), F32), ((CV_K, CV_W), F32)],
                 scratch=_conv_scratch(bm))


def _conv_bwd_pre(dy1, dgate, p, cw, seq, *, name, bm=CV_BM):
    t = p.shape[0]
    per = bm // HALO
    last_halo = t // HALO - 1

    def fn(i, ins, outs, accs, scr):
        d1, d1n, dg, a, b = [r[...] for r in ins[:5]]
        ext, sh, y = scr
        keep = jnp.where((i % (seq // bm)) == (seq // bm - 1), 0.0, 1.0)
        ext[pl.ds(0, bm), :] = d1
        ext[pl.ds(bm, HALO), :] = keep * d1n
        _conv_shift_copies(ext, sh)
        _conv_taps(ext, sh, ins[5], y, CV_BWD_OFFS)
        dy0 = y[...]
        sb = _sigmoid(b)
        outs[0][:, 0:CV_W] = dy0 * sb
        outs[0][:, CV_W:2 * CV_W] = dy0 * a * sb * (1.0 - sb)
        outs[0][:, 2 * CV_W:3 * CV_W] = dg

    return _rows(fn, name=name, steps=t // bm,
                 ins=[_rb(dy1, bm), (dy1, (HALO, CV_W), lambda i: (jnp.minimum((i + 1) * per, last_halo), 0)),
                      _rb(dgate, bm), _rb(p, bm, 0, CV_W), _rb(p, bm, 1, CV_W), _const(cw)],
                 outs=[_ro(t, 3 * CV_W, F32, bm)], scratch=_conv_scratch(bm))[0]


def _iotas():
    row = lax.broadcasted_iota(jnp.int32, (BLK, BLK), 0)
    col = lax.broadcasted_iota(jnp.int32, (BLK, BLK), 1)
    return row, col


def _heads(x, head0):
    if head0.shape != x.shape:
        head0 = lax.broadcasted_iota(jnp.int32, x.shape, 1) < HEAD_DIM
    return jnp.where(head0, x, 0.0).astype(BF16), jnp.where(head0, 0.0, x).astype(BF16)


def _pair_spec(seq, off):
    return pl.BlockSpec((seq, BLK), lambda b, hp: (b, off + hp))


def _stat_spec(seq):
    return pl.BlockSpec((None, None, seq, BLK), lambda b, hp: (b, hp, 0, 0))


_ATT_PARAMS = dict(compiler_params=pltpu.CompilerParams(dimension_semantics=("parallel", "parallel")))
_SCALE = 1.0 / math.sqrt(HEAD_DIM)


Q_BLOCK = 256
KEY_BLOCK = 256


def _stack_heads(x, head0, scale=None):
    if scale is not None:
        x = x * scale
    return jnp.concatenate(_heads(x, head0), axis=0)


def _pair_cols(x, head0, fill):
    a = jnp.max(jnp.where(head0, x, fill), axis=1, keepdims=True)
    b = jnp.max(jnp.where(head0, fill, x), axis=1, keepdims=True)
    return jnp.concatenate([a, b], axis=0)


def _causal_mask(t0, s0, tq, kw, inclusive):
    row = lax.broadcasted_iota(jnp.int32, (2 * tq, kw), 0) & (tq - 1)
    col = lax.broadcasted_iota(jnp.int32, (2 * tq, kw), 1)
    return (s0 + col) <= (t0 + row) if inclusive else (s0 + col) < (t0 + row)


def _sub(x, j):
    return x[:, j * BLK:(j + 1) * BLK]


def _tri_blocks(kw, relation):
    r = lax.broadcasted_iota(jnp.int32, (kw, kw), 0)
    c = lax.broadcasted_iota(jnp.int32, (kw, kw), 1)
    return (((r // BLK) == (c // BLK)) & relation(r, c)).astype(BF16)


def _block_cumsum(x, tri, ksub):
    hi = x.astype(BF16)
    lo = (x - hi.astype(F32)).astype(BF16)
    cs = _dot(jnp.concatenate([hi, lo], axis=0), tri)
    n = x.shape[0]
    cs = cs[:n] + cs[n:]
    return [_sub(cs, j) for j in range(ksub)], [jnp.sum(_sub(x, j), axis=1, keepdims=True) for j in range(ksub)]


def _sb_terms(qs, k, mask):
    return _sb_terms_z(_dot(qs, k, _NT), mask)


def _sb_terms_z(z, mask):
    t = jnp.log(1.0 + jnp.exp(-jnp.abs(z)))
    lsz = jnp.minimum(z, 0.0) - t
    lr = lsz - z
    if mask is not None:
        lr = jnp.where(mask, lr, 0.0)
    return lsz, lr


def _sb_fwd(p, nb, seq, *, name):
    tq = min(Q_BLOCK, seq)
    nq = seq // tq
    kw = min(KEY_BLOCK, seq)
    ksub = kw // BLK

    def body(q_ref, k_ref, v_ref, o_ref, tot_ref):
        row, col = _iotas()
        colq = lax.broadcasted_iota(jnp.int32, (tq, BLK), 1)
        head0 = colq < HEAD_DIM
        upper = _tri_blocks(kw, lambda j, s: j > s)

        def qblock(qb, c):
            t0 = pl.multiple_of(qb * tq, tq)
            qs = _stack_heads(q_ref[pl.ds(t0, tq), :], head0, _SCALE)
            diag = (t0 + tq - 1) // kw

            def kblock(kb, carry, masked):
                acc, run = carry
                s0 = pl.multiple_of(kb * kw, kw)
                k = k_ref[pl.ds(s0, kw), :].astype(BF16)
                v0, v1 = _heads(v_ref[pl.ds(s0, kw), :], head0)
                mask = _causal_mask(t0, s0, tq, kw, False)[:tq] if masked else None
                zs = [_dot(qs[h * tq:(h + 1) * tq], k, _NT) for h in range(2)]
                terms = []
                for h in range(2):
                    lsz, lr = _sb_terms_z(zs[h], mask)
                    terms.append((lsz,) + _block_cumsum(lr, upper, ksub))
                runs = []
                for h, vh in enumerate((v0, v1)):
                    lsz, after, total = terms[h]
                    r = run[h]
                    ws = [None] * ksub
                    for j in reversed(range(ksub)):
                        w = jnp.exp(_sub(lsz, j) + after[j] + r)
                        if masked:
                            w = jnp.where(_sub(mask, j), w, 0.0)
                        ws[j] = w.astype(BF16)
                        r = r + total[j]
                    acc = acc + _dot(jnp.concatenate(ws, axis=1), vh)
                    runs.append(r)
                return acc, tuple(runs)

            zc = jnp.zeros((tq, 1), F32)
            carry = kblock(diag, (jnp.zeros((tq, BLK), F32), (zc, zc)), True)
            acc, run = lax.fori_loop(0, diag, lambda it, cr: kblock(diag - 1 - it, cr, False), carry)
            o_ref[pl.ds(t0, tq), :] = acc
            tot_ref[pl.ds(t0, tq), :] = jnp.where(head0, run[0], run[1])
            return c

        lax.fori_loop(0, nq, qblock, 0)

    return pl.pallas_call(
        body, name=name, grid=(nb, PAIRS),
        in_specs=[_pair_spec(seq, 0), _pair_spec(seq, PAIRS), _pair_spec(seq, 2 * PAIRS)],
        out_specs=[_pair_spec(seq, 0), _stat_spec(seq)],
        out_shape=[jax.ShapeDtypeStruct((nb * seq, D_MODEL), F32), jax.ShapeDtypeStruct((nb, PAIRS, seq, BLK), F32)],
        **_ATT_PARAMS,
    )(p, p, p)


def _sb_bwd(p, do, tot, nb, seq, *, name):
    tq = min(Q_BLOCK, seq)
    nq = seq // tq
    kw = min(KEY_BLOCK, seq)
    ksub = kw // BLK

    def body(q_ref, k_ref, v_ref, do_ref, tot_ref, dq_ref, dk_ref, dv_ref):
        row, col = _iotas()
        colq = lax.broadcasted_iota(jnp.int32, (tq, BLK), 1)
        head0 = colq < HEAD_DIM
        lower_incl = _tri_blocks(kw, lambda j, s: j <= s)
        lower_strict = _tri_blocks(kw, lambda s, j: s < j)
        dk_ref[...] = jnp.zeros(dk_ref.shape, F32)
        dv_ref[...] = jnp.zeros(dv_ref.shape, F32)

        def qblock(qb, c):
            t0 = pl.multiple_of(qb * tq, tq)
            qs = _stack_heads(q_ref[pl.ds(t0, tq), :], head0, _SCALE)
            dos = _stack_heads(do_ref[pl.ds(t0, tq), :], head0)
            tot = tot_ref[pl.ds(t0, tq), :]
            swapped = pltpu.roll(tot, HEAD_DIM, 1)
            tts = (jnp.where(head0, tot, swapped), jnp.where(head0, swapped, tot))
            diag = (t0 + tq - 1) // kw

            def kblock(kb, carry, masked):
                dq, pfs, efs = carry
                s0 = pl.multiple_of(kb * kw, kw)
                kf = k_ref[pl.ds(s0, kw), :]
                k = kf.astype(BF16)
                khs = _heads(kf, head0)
                v = v_ref[pl.ds(s0, kw), :].astype(BF16)
                mask = _causal_mask(t0, s0, tq, kw, False)[:tq] if masked else None
                zs = [_dot(qs[h * tq:(h + 1) * tq], k, _NT) for h in range(2)]
                dws = [_dot(dos[h * tq:(h + 1) * tq], v, _NT) for h in range(2)]
                first = []
                for h in range(2):
                    lsz, lr = _sb_terms_z(zs[h], None)
                    lrm = jnp.where(mask, lr, 0.0) if masked else lr
                    first.append((lsz, lr) + _block_cumsum(lrm, lower_incl, ksub))
                second, pfs_out = [], []
                for h in range(2):
                    lsz, lr, incl, total = first[h]
                    pf = pfs[h]
                    ws, ews = [], []
                    for j in range(ksub):
                        w = jnp.exp(_sub(lsz, j) + (tts[h] - pf - incl[j]))
                        if masked:
                            w = jnp.where(_sub(mask, j), w, 0.0)
                        pf = pf + total[j]
                        ws.append(w.astype(BF16))
                        ews.append(_sub(dws[h], j) * w)
                    pfs_out.append(pf)
                    second.append((ws, ews) + _block_cumsum(jnp.concatenate(ews, axis=1), lower_strict, ksub))
                dz_h, efs_out = [], []
                for h in range(2):
                    lsz, lr = first[h][:2]
                    ws, ews, before, etotal = second[h]
                    ef = efs[h]
                    dzs = []
                    for j in range(ksub):
                        dz = ews[j] * jnp.exp(_sub(lr, j)) - (ef + before[j]) * jnp.exp(_sub(lsz, j))
                        ef = ef + etotal[j]
                        if masked:
                            dz = jnp.where(_sub(mask, j), dz, 0.0)
                        dzs.append(dz.astype(BF16))
                    efs_out.append(ef)
                    dz_h.append(jnp.concatenate(dzs, axis=1))
                    dq = dq + _dot(dz_h[h], khs[h])
                w = jnp.concatenate([jnp.concatenate(second[h][0], axis=1) for h in range(2)], axis=0)
                dk_ref[pl.ds(s0, kw), :] += _dot(jnp.concatenate(dz_h, axis=0), qs, _TN)
                dv_ref[pl.ds(s0, kw), :] += _dot(w, dos, _TN)
                return dq, tuple(pfs_out), tuple(efs_out)

            zc = jnp.zeros((tq, 1), F32)
            carry = lax.fori_loop(0, diag, lambda kb, cr: kblock(kb, cr, False), (jnp.zeros((tq, BLK), F32), (zc, zc), (zc, zc)))
            dq_ref[pl.ds(t0, tq), :] = kblock(diag, carry, True)[0] * _SCALE
            return c

        lax.fori_loop(0, nq, qblock, 0)

    t = nb * seq
    return pl.pallas_call(
        body, name=name, grid=(nb, PAIRS),
        in_specs=[_pair_spec(seq, 0), _pair_spec(seq, PAIRS), _pair_spec(seq, 2 * PAIRS), _pair_spec(seq, 0), _stat_spec(seq)],
        out_specs=[_pair_spec(seq, 0)] * 3,
        out_shape=[jax.ShapeDtypeStruct((t, D_MODEL), F32)] * 3,
        **_ATT_PARAMS,
    )(p, p, p, do, tot)


def _fox_cum(f, bf, nb, seq, *, name):
    def body(f_ref, bf_ref, cc_ref, cr_ref):
        row, col = _iotas()
        lower = (col <= row).astype(BF16)
        carry = jnp.zeros((1, BLK), F32)
        for blk in range(seq // BLK):
            rs = slice(blk * BLK, (blk + 1) * BLK)
            lf = jnp.where(col < HEADS, _log_sigmoid(f_ref[rs, :] + bf_ref[...]), 0.0)
            cc = _dot3_left(lower, lf) + carry
            cc_ref[rs, :] = cc
            cr_ref[:, rs] = cc.T[0:HEADS, :]
            carry = carry + _colsum(lf)

    return pl.pallas_call(
        body, name=name, grid=(nb,),
        in_specs=[pl.BlockSpec((seq, BLK), lambda b: (b, 0)), pl.BlockSpec((1, BLK), lambda b: (0, 0))],
        out_specs=[pl.BlockSpec((seq, BLK), lambda b: (b, 0)), pl.BlockSpec((None, HEADS, seq), lambda b: (b, 0, 0))],
        out_shape=[jax.ShapeDtypeStruct((nb * seq, BLK), F32), jax.ShapeDtypeStruct((nb, HEADS, seq), F32)],
        compiler_params=pltpu.CompilerParams(dimension_semantics=("parallel",)),
    )(f, bf)


def _fox_cum_bwd(dcr, dcc, f, bf, nb, seq, *, name):
    def body(dcr_ref, dcc_ref, f_ref, bf_ref, df_ref, dbf_ref):
        row, col = _iotas()
        upper_incl = (col >= row).astype(BF16)

        @pl.when(pl.program_id(0) == 0)
        def _():
            dbf_ref[...] = jnp.zeros((1, BLK), F32)

        carry = jnp.zeros((1, BLK), F32)
        for blk in reversed(range(seq // BLK)):
            rs = slice(blk * BLK, (blk + 1) * BLK)
            dc = dcr_ref[:, rs].T + dcc_ref[rs, :]
            dlf = _dot3_left(upper_incl, dc) + carry
            carry = carry + _colsum(dc)
            fl = f_ref[rs, :] + bf_ref[...]
            df = jnp.where(col < HEADS, dlf * _sigmoid(-fl), 0.0)
            df_ref[rs, :] = df
            dbf_ref[...] += _colsum(df)

    return pl.pallas_call(
        body, name=name, grid=(nb,),
        in_specs=[pl.BlockSpec((None, BLK, seq), lambda b: (b, 0, 0)), pl.BlockSpec((seq, BLK), lambda b: (b, 0)),
                  pl.BlockSpec((seq, BLK), lambda b: (b, 0)), pl.BlockSpec((1, BLK), lambda b: (0, 0))],
        out_specs=[pl.BlockSpec((seq, BLK), lambda b: (b, 0)), pl.BlockSpec((1, BLK), lambda b: (0, 0))],
        out_shape=[jax.ShapeDtypeStruct((nb * seq, BLK), F32), jax.ShapeDtypeStruct((1, BLK), F32)],
        compiler_params=pltpu.CompilerParams(dimension_semantics=("arbitrary",)),
    )(dcr, dcc, f, bf)


def _fox_cum_cols(cc_ref, t0, tq, colq, hp):
    cc = cc_ref[pl.ds(t0, tq), :]
    c0 = jnp.sum(jnp.where(colq == 2 * hp, cc, 0.0), axis=1, keepdims=True)
    c1 = jnp.sum(jnp.where(colq == 2 * hp + 1, cc, 0.0), axis=1, keepdims=True)
    return c0, c1


def _fox_bias(c0, c1, cr_ref, s0, kw):
    return jnp.concatenate([c0 - cr_ref[0:1, pl.ds(s0, kw)], c1 - cr_ref[1:2, pl.ds(s0, kw)]], axis=0)


def _fox_fwd(p, cc, cr, nb, seq, *, name):
    tq = min(Q_BLOCK, seq)
    nq = seq // tq
    kw = min(KEY_BLOCK, seq)
    ksub = kw // BLK

    def body(q_ref, k_ref, v_ref, cc_ref, cr_ref, o_ref, lse_ref):
        hp = pl.program_id(1)
        row, col = _iotas()
        colq = lax.broadcasted_iota(jnp.int32, (tq, BLK), 1)
        head0 = colq < HEAD_DIM

        def qblock(qb, c):
            t0 = pl.multiple_of(qb * tq, tq)
            qs = _stack_heads(q_ref[pl.ds(t0, tq), :], head0, _SCALE)
            c0, c1 = _fox_cum_cols(cc_ref, t0, tq, colq, hp)
            diag = (t0 + tq - 1) // kw

            def kblock(kb, carry, masked):
                accs, ms = carry
                s0 = pl.multiple_of(kb * kw, kw)
                k = k_ref[pl.ds(s0, kw), :].astype(BF16)
                vf = v_ref[pl.ds(s0, kw), :]
                own0 = lax.broadcasted_iota(jnp.int32, vf.shape, 1) < HEAD_DIM
                vs = (jnp.where(own0, vf, 1.0).astype(BF16), jnp.where(own0, 1.0, vf).astype(BF16))
                mask = _causal_mask(t0, s0, tq, kw, True)[:tq] if masked else None
                zs = [_dot(qs[h * tq:(h + 1) * tq], k, _NT) for h in range(2)]
                parts = []
                for h, ch in enumerate((c0, c1)):
                    s = zs[h] + (ch - cr_ref[h:h + 1, pl.ds(s0, kw)])
                    if masked:
                        s = jnp.where(mask, s, -jnp.inf)
                    m_new = jnp.maximum(ms[h], jnp.max(s, axis=1, keepdims=True))
                    parts.append((jnp.exp(s - m_new).astype(BF16), jnp.exp(ms[h] - m_new), m_new))
                return (tuple(accs[h] * parts[h][1] + _dot(parts[h][0], vs[h]) for h in range(2)),
                        tuple(parts[h][2] for h in range(2)))

            zeros, ninf = jnp.zeros((tq, BLK), F32), jnp.full((tq, 1), -jnp.inf, F32)
            carry = lax.fori_loop(0, diag, lambda kb, cr: kblock(kb, cr, False), ((zeros, zeros), (ninf, ninf)))
            (acc0, acc1), (m0, m1) = kblock(diag, carry, True)
            l = jnp.where(head0, pltpu.roll(acc0, HEAD_DIM, 1), pltpu.roll(acc1, HEAD_DIM, 1))
            o_ref[pl.ds(t0, tq), :] = jnp.where(head0, acc0, acc1) / l
            lse_ref[pl.ds(t0, tq), :] = jnp.where(head0, m0, m1) + jnp.log(l)
            return c

        lax.fori_loop(0, nq, qblock, 0)

    return pl.pallas_call(
        body, name=name, grid=(nb, PAIRS),
        in_specs=[_pair_spec(seq, 0), _pair_spec(seq, PAIRS), _pair_spec(seq, 2 * PAIRS),
                  pl.BlockSpec((seq, BLK), lambda b, hp: (b, 0)), pl.BlockSpec((None, None, 8, seq), lambda b, hp: (b, hp, 0, 0))],
        out_specs=[_pair_spec(seq, 0), _stat_spec(seq)],
        out_shape=[jax.ShapeDtypeStruct((nb * seq, D_MODEL), F32), jax.ShapeDtypeStruct((nb, PAIRS, seq, BLK), F32)],
        **_ATT_PARAMS,
    )(p, p, p, cc, cr)


def _fox_bwd(p, do, o, lse, cc, cr, nb, seq, *, name):
    tq = min(Q_BLOCK, seq)
    nq = seq // tq
    kw = min(KEY_BLOCK, seq)
    ksub = kw // BLK

    def body(q_ref, k_ref, v_ref, do_ref, o_ref, lse_ref, cc_ref, cr_ref, dq_ref, dk_ref, dv_ref, dcr_ref, dcc_ref):
        hp = pl.program_id(1)
        row, col = _iotas()
        colq = lax.broadcasted_iota(jnp.int32, (tq, BLK), 1)
        head0 = colq < HEAD_DIM
        dk_ref[...] = jnp.zeros(dk_ref.shape, F32)
        dv_ref[...] = jnp.zeros(dv_ref.shape, F32)
        dcr_ref[...] = jnp.zeros(dcr_ref.shape, F32)

        @pl.when(hp == 0)
        def _():
            dcc_ref[...] = jnp.zeros(dcc_ref.shape, F32)

        def qblock(qb, c):
            t0 = pl.multiple_of(qb * tq, tq)
            qs = _stack_heads(q_ref[pl.ds(t0, tq), :], head0, _SCALE)
            dof = do_ref[pl.ds(t0, tq), :]
            dos = _stack_heads(dof, head0)
            prod = dof * o_ref[pl.ds(t0, tq), :]
            dl = jnp.concatenate([jnp.sum(jnp.where(head0, prod, 0.0), axis=1, keepdims=True),
                                  jnp.sum(jnp.where(head0, 0.0, prod), axis=1, keepdims=True)], axis=0)
            lse = _pair_cols(lse_ref[pl.ds(t0, tq), :], head0, -jnp.inf)
            c0, c1 = _fox_cum_cols(cc_ref, t0, tq, colq, hp)
            diag = (t0 + tq - 1) // kw

            def kblock(kb, carry, masked):
                dq, rs = carry
                s0 = pl.multiple_of(kb * kw, kw)
                kf = k_ref[pl.ds(s0, kw), :]
                k = kf.astype(BF16)
                k0, k1 = _heads(kf, head0)
                v = v_ref[pl.ds(s0, kw), :].astype(BF16)
                mask = _causal_mask(t0, s0, tq, kw, True)[:tq] if masked else None
                zs = [_dot(qs[h * tq:(h + 1) * tq], k, _NT) for h in range(2)]
                dps = [_dot(dos[h * tq:(h + 1) * tq], v, _NT) for h in range(2)]
                prs, dss, rss = [], [], []
                for h, (ch, kh) in enumerate(((c0, k0), (c1, k1))):
                    rows = slice(h * tq, (h + 1) * tq)
                    pr = jnp.exp(zs[h] + (ch - cr_ref[h:h + 1, pl.ds(s0, kw)]) - lse[rows])
                    if masked:
                        pr = jnp.where(mask, pr, 0.0)
                    ds = pr * (dps[h] - dl[rows])
                    dcr_ref[h:h + 1, pl.ds(s0, kw)] -= _colsum(ds)
                    rss.append(rs[rows] + jnp.sum(ds, axis=1, keepdims=True))
                    prs.append(pr.astype(BF16))
                    dss.append(ds.astype(BF16))
                    dq = dq + _dot(dss[h], kh)
                dk_ref[pl.ds(s0, kw), :] += _dot(jnp.concatenate(dss, axis=0), qs, _TN)
                dv_ref[pl.ds(s0, kw), :] += _dot(jnp.concatenate(prs, axis=0), dos, _TN)
                return dq, jnp.concatenate(rss, axis=0)

            init = (jnp.zeros((tq, BLK), F32), jnp.zeros((2 * tq, 1), F32))
            carry = lax.fori_loop(0, diag, lambda kb, cr: kblock(kb, cr, False), init)
            dq, rs = kblock(diag, carry, True)
            dq_ref[pl.ds(t0, tq), :] = dq * _SCALE
            dcc_ref[pl.ds(t0, tq), :] += jnp.where(colq == 2 * hp, rs[:tq], 0.0) + jnp.where(colq == 2 * hp + 1, rs[tq:], 0.0)
            return c

        lax.fori_loop(0, nq, qblock, 0)

    t = nb * seq
    return pl.pallas_call(
        body, name=name, grid=(nb, PAIRS),
        in_specs=[_pair_spec(seq, 0), _pair_spec(seq, PAIRS), _pair_spec(seq, 2 * PAIRS), _pair_spec(seq, 0), _pair_spec(seq, 0),
                  _stat_spec(seq), pl.BlockSpec((seq, BLK), lambda b, hp: (b, 0)),
                  pl.BlockSpec((None, None, 8, seq), lambda b, hp: (b, hp, 0, 0))],
        out_specs=[_pair_spec(seq, 0)] * 3 + [pl.BlockSpec((None, None, 8, seq), lambda b, hp: (b, hp, 0, 0)),
                                              pl.BlockSpec((seq, BLK), lambda b, hp: (b, 0))],
        out_shape=[jax.ShapeDtypeStruct((t, D_MODEL), F32)] * 3 + [jax.ShapeDtypeStruct((nb, PAIRS, 8, seq), F32),
                                                                     jax.ShapeDtypeStruct((t, BLK), F32)],
        compiler_params=pltpu.CompilerParams(dimension_semantics=("parallel", "arbitrary")),
    )(p, p, p, do, o, lse, cc, cr)


def _row_shards(x):
    return x.reshape(N_CHIPS, x.shape[0] // N_CHIPS, x.shape[1])


def _local_step(x3, tgt3, w, later=None, start_reduce=None):
    nb, seq, d = x3.shape
    t = nb * seq
    x0, tgt = x3.reshape(t, d), tgt3.reshape(t, d)
    g = {}

    a_gain = w["a_norm"].reshape(1, d)
    h_a, ht_a = _rmsnorm(x0, a_gain, name="a_norm_fwd")
    p_a = _matmul(h_a, w["a_w_in"], name="a_in_fwd")
    o_a, tot_a = _sb_fwd(p_a, nb, seq, name="a_attn_fwd")
    y_a, yt_a = _gate(o_a, p_a, 3, name="a_gate_fwd")
    x1 = _matmul(y_a, w["a_w_out"], name="a_out_fwd", residual=x0)

    if later:
        w = {**w, **later[0](x1)}
    b_gain = w["b_norm"].reshape(1, d)
    b_lg, b_lb = w["b_v_ln_g"].reshape(1, GM_W), w["b_v_ln_b"].reshape(1, GM_W)
    b_ws, b_bst = w["b_w_s"].reshape(GM_G, BLK, BLK), w["b_b_s"].reshape(GM_G, BLK).T
    h_b, ht_b = _rmsnorm(x1, b_gain, name="b_norm_fwd")
    p_b = _matmul(h_b, w["b_w_in"], name="b_in_fwd")
    y_b, yt_b = _gmlp_fwd(p_b, b_lg, b_lb, b_ws, b_bst, name="b_mix_fwd")
    x2 = _matmul(y_b, w["b_w_out"], name="b_out_fwd", residual=x1)

    if later:
        w = {**w, **later[1](x2)}
    c_gain = w["c_norm"].reshape(1, d)
    c_cw = jnp.repeat(w["c_conv_w"].reshape(CV_K, CV_W), SUBLANES, axis=0)
    c_cb = w["c_conv_b"].reshape(1, CV_W)
    c_lg, c_lb = w["c_ln_g"].reshape(1, CV_W), w["c_ln_b"].reshape(1, CV_W)
    h_c, ht_c = _rmsnorm(x2, c_gain, name="c_norm_fwd")
    p_c = _matmul(h_c, w["c_w_in"], name="c_in_fwd")
    y_c, yt_c = _conv_fwd(p_c, c_cw, c_cb, c_lg, c_lb, seq, name="c_conv_fwd")
    x3_ = _matmul(y_c, w["c_w_out"], name="c_out_fwd", residual=x2)

    d_gain = w["d_norm"].reshape(1, d)
    d_win = w["d_w_in"].reshape(d, 4 * D_MODEL + HEADS)
    d_wmain = d_win[:, :4 * D_MODEL]
    d_wf = jnp.pad(d_win[:, 4 * D_MODEL:], ((0, 0), (0, BLK - HEADS)))
    d_bf = jnp.pad(w["d_b_f"].reshape(1, HEADS), ((0, 0), (0, BLK - HEADS)))
    h_d, ht_d = _rmsnorm(x3_, d_gain, name="d_norm_fwd")
    p_d = _matmul(h_d, d_wmain, name="d_in_fwd")
    f_d = _matmul(h_d, d_wf, name="d_inf_fwd")
    cc, cr = _fox_cum(f_d, d_bf, nb, seq, name="d_cum_fwd")
    cr = jnp.pad(cr.reshape(nb, PAIRS, 2, seq), ((0, 0), (0, 0), (0, 6), (0, 0)))
    o_d, lse_d = _fox_fwd(p_d, cc, cr, nb, seq, name="d_attn_fwd")
    y_d, yt_d = _gate(o_d, p_d, 3, name="d_gate_fwd")
    x4 = _matmul(y_d, w["d_w_out"], name="d_out_fwd", residual=x3_)

    f_gain = w["final_norm"].reshape(1, d)
    dx, g_fn, loss_row = _loss_head(x4, f_gain, tgt, name="loss_head")
    g["final_norm"] = g_fn

    g["d_w_out"] = _row_shards(_matmul(yt_d, dx, name="d_out_dw"))
    dy = _matmul(dx, w["d_w_out"], name="d_out_dy", mode="nt")
    do_d, dg_d = _gate_bwd(dy, o_d, p_d, 3, name="d_gate_bwd")
    dq, dk, dv, dcr, dcc = _fox_bwd(p_d, do_d, o_d, lse_d, cc, cr, nb, seq, name="d_attn_bwd")
    dcr = jnp.pad(dcr[:, :, :2, :].reshape(nb, HEADS, seq), ((0, 0), (0, BLK - HEADS), (0, 0)))
    df, dbf = _fox_cum_bwd(dcr, dcc, f_d, d_bf, nb, seq, name="d_cum_bwd")
    g["d_b_f"] = dbf[:, :HEADS]
    parts = [dq, dk, dv, dg_d]
    dws = [_matmul(ht_d, pt, name=f"d_in_dw{n}") for n, pt in enumerate(parts)]
    dwf = _matmul(ht_d, df, name="d_inf_dw")
    g["d_w_in"] = jnp.concatenate(dws + [dwf[:, :HEADS]], axis=1).reshape(d, N_CHIPS, -1).transpose(1, 0, 2)
    dh = _matmul(df, d_wf, name="d_inf_dh", mode="nt")
    for n, pt in enumerate(parts):
        dh = _matmul(pt, d_wmain[:, n * D_MODEL:(n + 1) * D_MODEL], name=f"d_in_dh{n}", mode="nt", residual=dh)
    dx, g["d_norm"] = _rmsnorm_bwd(dh, x3_, d_gain, dx, name="d_norm_bwd")

    g["c_w_out"] = _row_shards(_matmul(yt_c, dx, name="c_out_dw"))
    dy = _matmul(dx, w["c_w_out"], name="c_out_dy", mode="nt")
    dy1, dgate, g["c_ln_g"], g["c_ln_b"], g["c_conv_b"], g["c_conv_w"] = _conv_bwd_post(
        dy, p_c, c_cw, c_cb, c_lg, c_lb, seq, name="c_conv_bwd_post")
    dp = _conv_bwd_pre(dy1, dgate, p_c, c_cw, seq, name="c_conv_bwd_pre")
    g["c_w_in"] = _matmul(ht_c, dp, name="c_in_dw", out_shards=N_CHIPS)
    dh = _matmul(dp, w["c_w_in"], name="c_in_dh", mode="nt")
    dx, g["c_norm"] = _rmsnorm_bwd(dh, x2, c_gain, dx, name="c_norm_bwd")

    early, b_wout, a_wout = [], w["b_w_out"], w["a_w_out"]
    if start_reduce is not None:
        begun, token = start_reduce({n: g[n] for n in ("d_w_in", "d_w_out", "c_w_in", "c_w_out")}, "grads_cd")
        early.append(begun)
        b_wout = b_wout + token[0, 0].astype(b_wout.dtype)
    g["b_w_out"] = _row_shards(_matmul(yt_b, dx, name="b_out_dw"))
    dy = _matmul(dx, b_wout, name="b_out_dy", mode="nt")
    dp, g["b_v_ln_g"], g["b_v_ln_b"], g["b_w_s"], dbst = _gmlp_bwd(dy, p_b, b_lg, b_lb, b_ws, b_bst, name="b_mix_bwd")
    g["b_b_s"] = dbst.T
    g["b_w_in"] = _matmul(ht_b, dp, name="b_in_dw", out_shards=N_CHIPS)
    dh = _matmul(dp, w["b_w_in"], name="b_in_dh", mode="nt")
    dx, g["b_norm"] = _rmsnorm_bwd(dh, x1, b_gain, dx, name="b_norm_bwd")

    if start_reduce is not None:
        begun, token = start_reduce({n: g[n] for n in ("b_w_in", "b_w_out")}, "grads_b")
        early.append(begun)
        a_wout = a_wout + token[0, 0].astype(a_wout.dtype)
    g["a_w_out"] = _row_shards(_matmul(yt_a, dx, name="a_out_dw"))
    dy = _matmul(dx, a_wout, name="a_out_dy", mode="nt")
    do_a, dg_a = _gate_bwd(dy, o_a, p_a, 3, name="a_gate_bwd")
    dq, dk, dv = _sb_bwd(p_a, do_a, tot_a, nb, seq, name="a_attn_bwd")
    parts = [dq, dk, dv, dg_a]
    g["a_w_in"] = jnp.stack([_matmul(ht_a, pt, name=f"a_in_dw{n}") for n, pt in enumerate(parts)])
    dh = None
    for n, pt in enumerate(parts):
        dh = _matmul(pt, w["a_w_in"][n], name=f"a_in_dh{n}", mode="nt", residual=dh)
    dx, g["a_norm"] = _rmsnorm_bwd(dh, x0, a_gain, dx, name="a_norm_bwd")

    return loss_row[0, 0], dx.reshape(nb, seq, d), g, early


_HBM = pl.BlockSpec(memory_space=pltpu.HBM)


def _place():
    return lax.axis_index("x"), lax.axis_index("y"), lax.axis_index("c")


def _other_chips(x, y):
    return [(1 - x, y), (x, 1 - y), (1 - x, 1 - y)]


def _allgather_chips(ss, *, name):
    n_ops = len(ss)

    def body(*refs):
        s_refs, o_refs, (send_sems, recv_sems) = refs[:n_ops], refs[n_ops:2 * n_ops], refs[2 * n_ops:]
        x, y, c = _place()
        me = 2 * x + y
        chips = _other_chips(x, y)

        def copy(i, kk, src, dst, to):
            return pltpu.make_async_remote_copy(src_ref=src, dst_ref=dst, send_sem=send_sems.at[6 * i + kk],
                                                recv_sem=recv_sems.at[6 * i + kk], device_id=to, device_id_type=MESH)

        def half(i, j, hc):
            h = s_refs[i].shape[0] // 2
            return o_refs[i].at[j, pl.ds(hc * h, h), :]

        first = [copy(i, kk, s_refs[i].at[pl.ds(c * (s_refs[i].shape[0] // 2), s_refs[i].shape[0] // 2), :], half(i, me, c),
                      (cx, cy, c)) for kk, (cx, cy) in enumerate(chips) for i in range(n_ops)]
        for cp in first:
            cp.start()
        passed = []
        for kk, (cx, cy) in enumerate(chips):
            for i in range(n_ops):
                blk = half(i, 2 * cx + cy, c)
                copy(i, kk, blk, blk, (cx, cy, c)).wait_recv()
                fwd = copy(i, 3 + kk, blk, blk, (x, y, 1 - c))
                fwd.start()
                passed.append(fwd)
        for kk, (cx, cy) in enumerate(chips):
            for i in range(n_ops):
                blk = half(i, 2 * cx + cy, 1 - c)
                copy(i, 3 + kk, blk, blk, (x, y, 1 - c)).wait_recv()
        for cp in first + passed:
            cp.wait_send()

    for s in ss:
        assert s.shape[0] % 32 == 0, s.shape
    return pl.pallas_call(
        body, name=name, in_specs=[_HBM] * n_ops, out_specs=[_HBM] * n_ops,
        out_shape=[jax.ShapeDtypeStruct((N_CHIPS,) + s.shape, s.dtype) for s in ss],
        scratch_shapes=[pltpu.SemaphoreType.DMA((6 * n_ops,)), pltpu.SemaphoreType.DMA((6 * n_ops,))],
    )(*ss)


_SEM = pl.BlockSpec(memory_space=pltpu.SEMAPHORE)
_ANY = pl.BlockSpec(memory_space=pl.ANY)
_DATAFLOW = pltpu.SideEffectType.DATAFLOW_SIDE_EFFECTING


def _chip_copies(s_refs, land_refs, send_sems, recv_sems):
    x, y, c = _place()
    me = 2 * x + y
    cps = []
    for i, (s_ref, land_ref) in enumerate(zip(s_refs, land_refs)):
        h = s_ref.shape[0] // 2
        for kk, (cx, cy) in enumerate(_other_chips(x, y)):
            cps.append(pltpu.make_async_remote_copy(
                src_ref=s_ref.at[pl.ds(c * h, h), :], dst_ref=land_ref.at[me, pl.ds(c * h, h), :], send_sem=send_sems.at[3 * i + kk],
                recv_sem=recv_sems.at[3 * i + kk], device_id=(cx, cy, c), device_id_type=MESH))
    return cps


def _gather_start(ss, after, *, name):
    n = len(ss)
    lands = [lax.empty((N_CHIPS,) + s.shape, s.dtype) for s in ss]

    def body(*refs):
        s_refs, land_refs = refs[:n], refs[n:2 * n]
        send_sems, recv_sems = refs[2 * n + 1], refs[2 * n + 2]
        token = refs[-1]
        for cp in _chip_copies(s_refs, land_refs, send_sems, recv_sems):
            cp.start()
        token[...] = jnp.zeros(token.shape, token.dtype)

    hbm = [pltpu.HBM(a.shape, a.dtype) for a in list(ss) + lands]
    res = pl.pallas_call(
        body, name=name,
        out_shape=(pltpu.SemaphoreType.DMA((3 * n,)), pltpu.SemaphoreType.DMA((3 * n,)), *hbm, jax.ShapeDtypeStruct((8, BLK), F32)),
        in_specs=[_HBM] * (2 * n) + [_ANY],
        out_specs=(_SEM, _SEM, *([_HBM] * (2 * n)), pl.BlockSpec(memory_space=pltpu.VMEM)),
        input_output_aliases={i: 2 + i for i in range(2 * n)},
        compiler_params=pltpu.CompilerParams(has_side_effects=_DATAFLOW),
    )(*[pltpu.with_memory_space_constraint(a, pltpu.HBM) for a in list(ss) + lands], after)
    return res[:-1], res[-1]


def _gather_wait(started, after, *, name):
    send_sems, recv_sems = started[0], started[1]
    n = (len(started) - 2) // 2

    def body(*refs):
        s_refs, land_refs = refs[:n], refs[n:2 * n]
        for cp in _chip_copies(s_refs, land_refs, refs[2 * n], refs[2 * n + 1]):
            cp.wait_send()
            cp.wait_recv()

    res = pl.pallas_call(
        body, name=name, out_shape=tuple(pltpu.HBM(a.shape, a.dtype) for a in started[2:]),
        in_specs=[_HBM] * (2 * n) + [_SEM, _SEM, _ANY], out_specs=tuple([_HBM] * (2 * n)),
        input_output_aliases={i: i for i in range(2 * n)},
        compiler_params=pltpu.CompilerParams(has_side_effects=_DATAFLOW),
    )(*started[2:], send_sems, recv_sems, after)
    return list(res[n:])


def _sibling_exchange(lands, *, name):
    n = len(lands)

    def body(*refs):
        o_refs, (send_sems, recv_sems) = refs[n:2 * n], refs[2 * n:]
        x, y, c = _place()
        cps = []
        for i, o_ref in enumerate(o_refs):
            h = o_ref.shape[1] // 2
            for kk, (cx, cy) in enumerate(_other_chips(x, y)):
                def half(hc):
                    return o_ref.at[2 * cx + cy, pl.ds(hc * h, h), :]
                sent = pltpu.make_async_remote_copy(src_ref=half(c), dst_ref=half(c), send_sem=send_sems.at[3 * i + kk],
                                                    recv_sem=recv_sems.at[3 * i + kk], device_id=(x, y, 1 - c), device_id_type=MESH)
                awaited = pltpu.make_async_remote_copy(src_ref=half(1 - c), dst_ref=half(1 - c), send_sem=send_sems.at[3 * i + kk],
                                                       recv_sem=recv_sems.at[3 * i + kk], device_id=(x, y, 1 - c),
                                                       device_id_type=MESH)
                cps.append((sent, awaited))
        for sent, _ in cps:
            sent.start()
        for sent, awaited in cps:
            awaited.wait_recv()
            sent.wait_send()

    return pl.pallas_call(
        body, name=name, in_specs=[_HBM] * n, out_specs=[_HBM] * n,
        out_shape=[jax.ShapeDtypeStruct(a.shape, a.dtype) for a in lands], scratch_shapes=_dma_sems(3 * n),
        input_output_aliases={i: i for i in range(n)},
    )(*lands)


def _own_block(gathered, s):
    me = 2 * lax.axis_index("x") + lax.axis_index("y")
    return lax.dynamic_update_slice(gathered, s[None], (me,) + (0,) * s.ndim)


def _dma_sems(n):
    return [pltpu.SemaphoreType.DMA((n,)), pltpu.SemaphoreType.DMA((n,))]


def _swap_halves(gps, *, name):
    n_ops = len(gps)

    def body(*refs):
        g_refs, o_refs, (send_sems, recv_sems) = refs[:n_ops], refs[n_ops:2 * n_ops], refs[2 * n_ops:]
        x, y, c = _place()
        cps = []
        for i, (g_ref, o_ref) in enumerate(zip(g_refs, o_refs)):
            h = g_ref.shape[1] // 2
            cps.append(pltpu.make_async_remote_copy(
                src_ref=g_ref.at[:, pl.ds((1 - c) * h, h), :], dst_ref=o_ref, send_sem=send_sems.at[i], recv_sem=recv_sems.at[i],
                device_id=(x, y, 1 - c), device_id_type=MESH))
        for cp in cps:
            cp.start()
        for cp in cps:
            cp.wait()

    return pl.pallas_call(
        body, name=name, in_specs=[_HBM] * n_ops, out_specs=[_HBM] * n_ops,
        out_shape=[jax.ShapeDtypeStruct((g.shape[0], g.shape[1] // 2, g.shape[2]), g.dtype) for g in gps],
        scratch_shapes=_dma_sems(n_ops),
    )(*gps)


def _scatter_chips(hps, *, name):
    n_ops = len(hps)

    def body(*refs):
        h_refs, o_refs, (send_sems, recv_sems) = refs[:n_ops], refs[n_ops:2 * n_ops], refs[2 * n_ops:]
        x, y, c = _place()
        cps = [pltpu.make_async_remote_copy(src_ref=h_ref.at[2 * cx + cy], dst_ref=o_ref.at[kk], send_sem=send_sems.at[3 * i + kk],
                                            recv_sem=recv_sems.at[3 * i + kk], device_id=(cx, cy, c), device_id_type=MESH)
               for i, (h_ref, o_ref) in enumerate(zip(h_refs, o_refs)) for kk, (cx, cy) in enumerate(_other_chips(x, y))]
        for cp in cps:
            cp.start()
        for cp in cps:
            cp.wait()

    return pl.pallas_call(
        body, name=name, in_specs=[_HBM] * n_ops, out_specs=[_HBM] * n_ops,
        out_shape=[jax.ShapeDtypeStruct((3,) + hp.shape[1:], hp.dtype) for hp in hps],
        scratch_shapes=_dma_sems(3 * n_ops),
    )(*hps)


def _join_halves(fs, *, name):
    n_ops = len(fs)

    def body(*refs):
        f_refs, o_refs, (send_sems, recv_sems) = refs[:n_ops], refs[n_ops:2 * n_ops], refs[2 * n_ops:]
        x, y, c = _place()
        cps = [pltpu.make_async_remote_copy(src_ref=f_ref, dst_ref=o_ref, send_sem=send_sems.at[i], recv_sem=recv_sems.at[i],
                                            device_id=(x, y, 1 - c), device_id_type=MESH)
               for i, (f_ref, o_ref) in enumerate(zip(f_refs, o_refs))]
        for cp in cps:
            cp.start()
        for cp in cps:
            cp.wait()

    theirs = pl.pallas_call(
        body, name=name, in_specs=[_HBM] * n_ops, out_specs=[_HBM] * n_ops,
        out_shape=[jax.ShapeDtypeStruct(f.shape, f.dtype) for f in fs], scratch_shapes=_dma_sems(n_ops),
    )(*fs)
    south = lax.axis_index("c") == 0
    return [jnp.concatenate([jnp.where(south, f, t), jnp.where(south, t, f)], axis=0) for f, t in zip(fs, theirs)]


def _add_halves(gp, ra, wire_dtype, *, name, bm=256):
    n, r, c_ = gp.shape
    h = r // 2
    bm = _tile(h, bm)
    per = h // bm
    c = lax.axis_index("c").astype(jnp.int32).reshape(1)

    def body(c_ref, g_ref, ra_ref, o_ref, ow_ref):
        s = g_ref[...] + ra_ref[...]
        o_ref[...] = s
        ow_ref[...] = s.astype(wire_dtype)

    mine = pl.BlockSpec((None, bm, c_), lambda j, i, cr: (j, i, 0))
    return pl.pallas_call(
        body, name=name,
        grid_spec=pltpu.PrefetchScalarGridSpec(
            num_scalar_prefetch=1, grid=(n, per),
            in_specs=[pl.BlockSpec((None, bm, c_), lambda j, i, cr: (j, cr[0] * per + i, 0)), mine],
            out_specs=[mine, mine]),
        out_shape=[jax.ShapeDtypeStruct((n, h, c_), F32), jax.ShapeDtypeStruct((n, h, c_), wire_dtype)],
        compiler_params=pltpu.CompilerParams(dimension_semantics=("parallel", "parallel")),
    )(c, gp, ra)


def _add_chips(hp, rb, *, name, bm=256):
    n, h, c_ = hp.shape
    bm = _tile(h, bm)
    me = (2 * lax.axis_index("x") + lax.axis_index("y")).astype(jnp.int32).reshape(1)

    def body(me_ref, h_ref, rb_ref, o_ref):
        o_ref[...] = ((h_ref[...] + rb_ref[0].astype(F32)) + rb_ref[1].astype(F32)) + rb_ref[2].astype(F32)

    return pl.pallas_call(
        body, name=name,
        grid_spec=pltpu.PrefetchScalarGridSpec(
            num_scalar_prefetch=1, grid=(h // bm,),
            in_specs=[pl.BlockSpec((None, bm, c_), lambda i, mr: (mr[0], i, 0)),
                      pl.BlockSpec((3, bm, c_), lambda i, mr: (0, i, 0))],
            out_specs=pl.BlockSpec((bm, c_), lambda i, mr: (i, 0))),
        out_shape=jax.ShapeDtypeStruct((h, c_), F32),
        compiler_params=pltpu.CompilerParams(dimension_semantics=("parallel",)),
    )(me, hp, rb)


def _scatter_copies(h_refs, land_refs, send_sems, recv_sems):
    x, y, c = _place()
    return [pltpu.make_async_remote_copy(src_ref=h_ref.at[2 * cx + cy], dst_ref=land_ref.at[kk], send_sem=send_sems.at[3 * i + kk],
                                         recv_sem=recv_sems.at[3 * i + kk], device_id=(cx, cy, c), device_id_type=MESH)
            for i, (h_ref, land_ref) in enumerate(zip(h_refs, land_refs)) for kk, (cx, cy) in enumerate(_other_chips(x, y))]


def _scatter_start(hps, after, *, name):
    n = len(hps)
    lands = [lax.empty((3,) + hp.shape[1:], hp.dtype) for hp in hps]

    def body(*refs):
        for cp in _scatter_copies(refs[:n], refs[n:2 * n], refs[2 * n + 1], refs[2 * n + 2]):
            cp.start()
        refs[-1][...] = jnp.zeros(refs[-1].shape, refs[-1].dtype)

    hbm = [pltpu.HBM(a.shape, a.dtype) for a in list(hps) + lands]
    res = pl.pallas_call(
        body, name=name,
        out_shape=(pltpu.SemaphoreType.DMA((3 * n,)), pltpu.SemaphoreType.DMA((3 * n,)), *hbm, jax.ShapeDtypeStruct((8, BLK), F32)),
        in_specs=[_HBM] * (2 * n) + [_ANY],
        out_specs=(_SEM, _SEM, *([_HBM] * (2 * n)), pl.BlockSpec(memory_space=pltpu.VMEM)),
        input_output_aliases={i: 2 + i for i in range(2 * n)},
        compiler_params=pltpu.CompilerParams(has_side_effects=_DATAFLOW),
    )(*[pltpu.with_memory_space_constraint(a, pltpu.HBM) for a in list(hps) + lands], after)
    return res[:-1], res[-1]


def _scatter_wait(started, after, *, name):
    n = (len(started) - 2) // 2

    def body(*refs):
        for cp in _scatter_copies(refs[:n], refs[n:2 * n], refs[2 * n], refs[2 * n + 1]):
            cp.wait_send()
            cp.wait_recv()

    res = pl.pallas_call(
        body, name=name, out_shape=tuple(pltpu.HBM(a.shape, a.dtype) for a in started[2:]),
        in_specs=[_HBM] * (2 * n) + [_SEM, _SEM, _ANY], out_specs=tuple([_HBM] * (2 * n)),
        input_output_aliases={i: i for i in range(2 * n)},
        compiler_params=pltpu.CompilerParams(has_side_effects=_DATAFLOW),
    )(*started[2:], started[0], started[1], after)
    return list(res[n:])


def _reduce_to_chips(gps, wire_dtypes, *, tag):
    ras = _swap_halves(gps, name=f"{tag}_swap_halves")
    return [_add_halves(gp, ra, wd, name=f"{tag}_add_halves{i}") for i, (gp, ra, wd) in enumerate(zip(gps, ras, wire_dtypes))]


def _start_reduce(early, tag):
    names = list(early)
    hps = _reduce_to_chips([early[n] for n in names], [BF16] * len(names), tag=tag)
    started, token = _scatter_start([hw for _, hw in hps], hps[-1][1], name=f"{tag}_scatter_start")
    return (tag, names, [hf for hf, _ in hps], started), token


def _adamw(w, g, m, v, *, name):
    r, c_ = w.shape
    bm = r
    for cand in (512, 256, 128, 64, 32, 16, 8):
        if r % cand == 0:
            bm = cand
            break
    c1 = 1.0 - ADAM_B1 ** ADAM_STEP
    c2 = 1.0 - ADAM_B2 ** ADAM_STEP

    def body(w_ref, g_ref, m_ref, v_ref, d_ref, nm_ref, nv_ref):
        g_ = g_ref[...]
        m_ = ADAM_B1 * m_ref[...] + (1.0 - ADAM_B1) * g_
        v_ = ADAM_B2 * v_ref[...] + (1.0 - ADAM_B2) * (g_ * g_)
        d_ref[...] = -ADAM_LR * ((m_ / c1) / (jnp.sqrt(v_ / c2) + ADAM_EPS) + ADAM_WD * w_ref[...])
        nm_ref[...] = m_
        nv_ref[...] = v_

    spec = pl.BlockSpec((bm, c_), lambda i: (i, 0))
    return pl.pallas_call(
        body, name=name, grid=(r // bm,), in_specs=[spec] * 4, out_specs=[spec] * 3,
        out_shape=[jax.ShapeDtypeStruct((r, c_), F32)] * 3,
        compiler_params=pltpu.CompilerParams(dimension_semantics=("parallel",)),
    )(w, g, m, v)


_WEIGHTS = ["a_norm", "a_w_in", "a_w_out", "b_norm", "b_w_in", "b_v_ln_g", "b_v_ln_b", "b_w_s", "b_b_s", "b_w_out",
            "c_norm", "c_w_in", "c_conv_w", "c_conv_b", "c_ln_g", "c_ln_b", "c_w_out", "d_norm", "d_w_in", "d_b_f",
            "d_w_out", "final_norm"]
_SHARD_AXIS = {"a_norm": None, "a_w_in": 2, "a_w_out": 1, "b_norm": 1, "b_w_in": 2, "b_v_ln_g": 1, "b_v_ln_b": 1, "b_w_s": None,
               "b_b_s": None, "b_w_out": 1, "c_norm": 1, "c_w_in": 2, "c_conv_w": 2, "c_conv_b": 1, "c_ln_g": 1, "c_ln_b": 1,
               "c_w_out": 1, "d_norm": 1, "d_w_in": 2, "d_b_f": None, "d_w_out": 1, "final_norm": None}
_BIG = ["a_w_in", "a_w_out", "b_w_in", "b_w_out", "c_w_in", "c_w_out", "d_w_in", "d_w_out"]
_GATHER_GROUPS = (("a_w_in", "a_w_out"), ("b_w_in", "b_w_out"), ("c_w_in", "c_w_out", "d_w_in", "d_w_out"))
_SMALL_SHARDED = [n for n in _WEIGHTS if _SHARD_AXIS[n] is not None and n not in _BIG]
_REPLICATED = [n for n in _WEIGHTS if _SHARD_AXIS[n] is None]
_ROW_ALIGN = 32
_ROW_ALIGN_SUMMED = 128


def _pack(pieces, dtype, align=_ROW_ALIGN):
    flat = jnp.concatenate([p.reshape(-1).astype(dtype) for p in pieces])
    unit = align * PACK_C
    total = -(-flat.shape[0] // unit) * unit
    return jnp.pad(flat, (0, total - flat.shape[0])).reshape(total // PACK_C, PACK_C)


def _unpack(flat, shapes):
    out, off = [], 0
    for s in shapes:
        n = math.prod(s)
        out.append(flat[off:off + n].reshape(s))
        off += n
    return out


def _full_shape(local_shape, axis):
    s = list(local_shape)
    if axis is not None:
        s[axis] *= N_CHIPS
    return tuple(s)


def _gather_weights(local):
    def whole(n, gt):
        if _SHARD_AXIS[n] == 1:
            return gt.reshape(-1, gt.shape[-1])
        if n == "d_w_in":
            return gt.transpose(1, 0, 2).reshape(gt.shape[1], -1)
        return gt

    full = {n: local[n][0] if n != "final_norm" else local[n] for n in _REPLICATED}
    first = list(_GATHER_GROUPS[0])
    mine = [local[n][0].astype(BF16) for n in first] + [_pack([local[n] for n in _SMALL_SHARDED], F32)]
    got = [_own_block(gt, s) for gt, s in zip(_allgather_chips(mine, name="gather_weights"), mine)]
    full.update({n: whole(n, gt) for n, gt in zip(first, got)})
    small = got[-1].reshape(N_CHIPS, -1)
    shards = [_unpack(small[j], [local[n].shape[1:] for n in _SMALL_SHARDED]) for j in range(N_CHIPS)]
    for i, n in enumerate(_SMALL_SHARDED):
        full[n] = jnp.concatenate([shards[j][i] for j in range(N_CHIPS)], axis=_SHARD_AXIS[n] - 1)

    def begin(k, after):
        shards_k = [local[n][0].astype(BF16) for n in _GATHER_GROUPS[k]]
        started, token = _gather_start(shards_k, after, name=f"gather{k}_start")
        return shards_k, started, token

    pending = [begin(1, got[0])]
    full["a_norm"] = full["a_norm"] + pending[0][2][0, 0]

    def finish(k):
        def weights(after):
            shards_k, started, _ = pending[k - 1]
            lands = _gather_wait(started, after, name=f"gather{k}_wait")
            token = None
            if k + 1 < len(_GATHER_GROUPS):
                pending.append(begin(k + 1, lands[0]))
                token = pending[k][2]
            lands = _sibling_exchange(lands, name=f"gather{k}_exchange")
            out = {n: whole(n, _own_block(gt, s)) for n, gt, s in zip(_GATHER_GROUPS[k], lands, shards_k)}
            if token is not None:
                gain = _GATHER_GROUPS[k][0][0] + "_norm"
                out[gain] = full[gain] + token[0, 0]
            return out
        return weights

    return full, [finish(k) for k in range(1, len(_GATHER_GROUPS))]


def _repl_piece_len(local):
    total = sum(math.prod(local[n].shape) for n in _REPLICATED)
    return -(-total // N_CHIPS)


def _reduce_grads(g, local, early):
    rep_flat = jnp.concatenate([g[n].reshape(-1) for n in _REPLICATED])
    piece = _repl_piece_len(local)
    rep_flat = jnp.pad(rep_flat, (0, N_CHIPS * piece - rep_flat.shape[0]))

    def shard(n, j):
        full = g[n].reshape(_full_shape(local[n].shape, _SHARD_AXIS[n]))
        width = local[n].shape[_SHARD_AXIS[n]]
        return lax.slice_in_dim(full, j * width, (j + 1) * width, axis=_SHARD_AXIS[n])

    small = jnp.stack([_pack([shard(n, j) for n in _SMALL_SHARDED] + [rep_flat[j * piece:(j + 1) * piece]], F32)
                       for j in range(N_CHIPS)])
    early_names = [n for _, names, _, _ in early for n in names]
    late = [n for n in _BIG if n not in early_names]
    hps = _reduce_to_chips([g[n] for n in late] + [small], [BF16] * len(late) + [F32], tag="grads")
    rbs = list(_scatter_chips([hw for _, hw in hps], name="grads_scatter_chips"))
    early_halves, early_rbs = [], []
    for tag, _, halves_k, started in early:
        early_halves += halves_k
        early_rbs += _scatter_wait(started, rbs[0], name=f"{tag}_scatter_wait")
    halves = early_halves + [hf for hf, _ in hps]
    fs = [_add_chips(hf, rb, name=f"grads_add_chips{i}") for i, (hf, rb) in enumerate(zip(halves, early_rbs + rbs))]
    summed = _join_halves(fs, name="grads_join_halves")
    red = {n: s.reshape(local[n].shape) for n, s in zip(early_names + late, summed)}
    out = _unpack(summed[-1].reshape(-1), [local[n].shape for n in _SMALL_SHARDED] + [(piece,)])
    red.update(zip(_SMALL_SHARDED, out[:-1]))
    rep_mine = _pack([out[-1]], F32)
    rep = _own_block(_allgather_chips([rep_mine], name="gather_replicated_grads")[0], rep_mine)
    rep = rep.reshape(N_CHIPS, -1)[:, :piece].reshape(-1)
    for n, val in zip(_REPLICATED, _unpack(rep, [local[n].shape for n in _REPLICATED])):
        red[n] = val
    return red


def _update(local, grads, m, v):
    delta, new_m, new_v = {}, {}, {}
    for n in _BIG:
        shp = local[n].shape
        two = (shp[-2], shp[-1])
        res = _adamw(local[n].reshape(two), grads[n].reshape(two), m[n].reshape(two), v[n].reshape(two), name=f"adamw_{n}")
        delta[n], new_m[n], new_v[n] = [r.reshape(shp) for r in res]
    small = [n for n in _WEIGHTS if n not in _BIG]
    shapes = [local[n].shape for n in small]
    packed = [_pack([src[n] for n in small], F32) for src in (local, grads, m, v)]
    res = _adamw(*packed, name="adamw_small")
    for dst, r in zip((delta, new_m, new_v), res):
        for n, val in zip(small, _unpack(r.reshape(-1), shapes)):
            dst[n] = val
    return delta, new_m, new_v


def kernel(x, a_norm, a_w_in, a_w_out, b_norm, b_w_in, b_v_ln_g, b_v_ln_b, b_w_s, b_b_s, b_w_out, c_norm, c_w_in, c_conv_w, c_conv_b, c_ln_g, c_ln_b, c_w_out, d_norm, d_w_in, d_b_f, d_w_out, final_norm, loss_target, m_a_norm, m_a_w_in, m_a_w_out, m_b_norm, m_b_w_in, m_b_v_ln_g, m_b_v_ln_b, m_b_w_s, m_b_b_s, m_b_w_out, m_c_norm, m_c_w_in, m_c_conv_w, m_c_conv_b, m_c_ln_g, m_c_ln_b, m_c_w_out, m_d_norm, m_d_w_in, m_d_b_f, m_d_w_out, m_final_norm, v_a_norm, v_a_w_in, v_a_w_out, v_b_norm, v_b_w_in, v_b_v_ln_g, v_b_v_ln_b, v_b_w_s, v_b_b_s, v_b_w_out, v_c_norm, v_c_w_in, v_c_conv_w, v_c_conv_b, v_c_ln_g, v_c_ln_b, v_c_w_out, v_d_norm, v_d_w_in, v_d_b_f, v_d_w_out, v_final_norm):
    local = dict(zip(_WEIGHTS, (a_norm, a_w_in, a_w_out, b_norm, b_w_in, b_v_ln_g, b_v_ln_b, b_w_s, b_b_s, b_w_out, c_norm, c_w_in,
                                c_conv_w, c_conv_b, c_ln_g, c_ln_b, c_w_out, d_norm, d_w_in, d_b_f, d_w_out, final_norm)))
    m = dict(zip(_WEIGHTS, (m_a_norm, m_a_w_in, m_a_w_out, m_b_norm, m_b_w_in, m_b_v_ln_g, m_b_v_ln_b, m_b_w_s, m_b_b_s, m_b_w_out,
                            m_c_norm, m_c_w_in, m_c_conv_w, m_c_conv_b, m_c_ln_g, m_c_ln_b, m_c_w_out, m_d_norm, m_d_w_in, m_d_b_f,
                            m_d_w_out, m_final_norm)))
    v = dict(zip(_WEIGHTS, (v_a_norm, v_a_w_in, v_a_w_out, v_b_norm, v_b_w_in, v_b_v_ln_g, v_b_v_ln_b, v_b_w_s, v_b_b_s, v_b_w_out,
                            v_c_norm, v_c_w_in, v_c_conv_w, v_c_conv_b, v_c_ln_g, v_c_ln_b, v_c_w_out, v_d_norm, v_d_w_in, v_d_b_f,
                            v_d_w_out, v_final_norm)))
    loss_part, grad_x, g, early = _local_step(x, loss_target, *_gather_weights(local), _start_reduce)
    loss = lax.psum(loss_part, ("x", "y", "c"))
    grads = _reduce_grads(g, local, early)
    delta, new_m, new_v = _update(local, grads, m, v)
    return (loss, grad_x, *[grads[n] for n in _WEIGHTS], *[delta[n] for n in _WEIGHTS],
            *[new_m[n] for n in _WEIGHTS], *[new_v[n] for n in _WEIGHTS])
```

```python
import functools
import math

import jax
import jax.numpy as jnp
from jax import lax
from jax.experimental import pallas as pl
from jax.experimental.pallas import tpu as pltpu

F32, BF16 = jnp.float32, jnp.bfloat16
MESH = pl.DeviceIdType.MESH

D_MODEL = 1024
HEADS = 16
HEAD_DIM = 64
BLK = 128
PAIRS = HEADS // 2
GM_W = 2048
GM_G = 16
CV_W = 2048
CV_K = 31
HALO = 32
EPS = 1e-6
N_CHIPS = 4
PACK_C = 1024
ADAM_LR, ADAM_B1, ADAM_B2, ADAM_EPS, ADAM_WD, ADAM_STEP = 0.001, 0.9, 0.999, 1e-08, 0.01, 10

_NT = (((1,), (1,)), ((), ()))
_TN = (((0,), (0,)), ((), ()))
_NN = (((1,), (0,)), ((), ()))


def _dot(a, b, dims=_NN):
    return lax.dot_general(a, b, dims, preferred_element_type=F32)


def _split3(x):
    hi = x.astype(BF16)
    r = x - hi.astype(F32)
    mid = r.astype(BF16)
    lo = (r - mid.astype(F32)).astype(BF16)
    return hi, mid, lo


def _dot3_right(x, m):
    hi, mid, lo = _split3(x)
    return _dot(hi, m) + _dot(mid, m) + _dot(lo, m)


def _dot3_left(m, x):
    hi, mid, lo = _split3(x)
    return _dot(m, hi) + _dot(m, mid) + _dot(m, lo)


def _sigmoid(x):
    return 1.0 / (1.0 + jnp.exp(-x))


def _silu(x):
    return x * _sigmoid(x)


def _dsilu(x):
    s = _sigmoid(x)
    return s * (1.0 + x * (1.0 - s))


_GELU_C = math.sqrt(2.0 / math.pi)
_GELU_A = 0.044715


def _gelu(x):
    return 0.5 * x * (1.0 + jnp.tanh(_GELU_C * (x + _GELU_A * x * x * x)))


def _dgelu(x):
    t = jnp.tanh(_GELU_C * (x + _GELU_A * x * x * x))
    return 0.5 * (1.0 + t) + 0.5 * x * (1.0 - t * t) * _GELU_C * (1.0 + 3.0 * _GELU_A * x * x)


def _log_sigmoid(x):
    return jnp.minimum(x, 0.0) - jnp.log(1.0 + jnp.exp(-jnp.abs(x)))


def _rms_fwd(x, g):
    r = lax.rsqrt(jnp.mean(x * x, axis=-1, keepdims=True) + EPS)
    return x * r * g


def _rms_bwd(dy, x, g):
    r = lax.rsqrt(jnp.mean(x * x, axis=-1, keepdims=True) + EPS)
    xh = x * r
    dxh = dy * g
    dx = r * (dxh - xh * jnp.mean(dxh * xh, axis=-1, keepdims=True))
    return dx, dy * xh


def _ln_stats(x):
    mu = jnp.mean(x, axis=-1, keepdims=True)
    xc = x - mu
    r = lax.rsqrt(jnp.mean(xc * xc, axis=-1, keepdims=True) + EPS)
    return xc * r, r


def _ln_bwd(dy, xh, r, g):
    dxh = dy * g
    return r * (dxh - jnp.mean(dxh, axis=-1, keepdims=True) - xh * jnp.mean(dxh * xh, axis=-1, keepdims=True))


def _colsum(x):
    return jnp.sum(x, axis=0, keepdims=True)


def _tile(n, want):
    for t in range(min(n, want), 7, -1):
        if n % t == 0 and t % 8 == 0:
            return t
    return n


MM_TILE = 1024


def _matmul(a, b, *, name, mode="nn", residual=None, out_shards=1, out_dtype=F32):
    (m, k) = a.shape
    b_shards = b.shape[0] if b.ndim == 3 else 1
    if mode == "nn":
        n = b.shape[-1] * b_shards
        tn, tk = _tile(n // max(b_shards, out_shards), MM_TILE), _tile(k, MM_TILE)
    else:
        n = b.shape[-2]
        tn, tk = _tile(n // out_shards, MM_TILE), _tile(k // b_shards, MM_TILE)
    tm = _tile(m, MM_TILE)
    nk = k // tk
    a_spec = pl.BlockSpec((tm, tk), lambda i, j, kk: (i, kk))
    if mode == "nn":
        dims = _NN
        if b_shards == 1:
            b_spec = pl.BlockSpec((tk, tn), lambda i, j, kk: (kk, j))
        else:
            per_b = n // b_shards // tn
            b_spec = pl.BlockSpec((None, tk, tn), lambda i, j, kk: (j // per_b, kk, j % per_b))
    else:
        dims = _NT
        if b_shards == 1:
            b_spec = pl.BlockSpec((tn, tk), lambda i, j, kk: (j, kk))
        else:
            per_b = k // b_shards // tk
            b_spec = pl.BlockSpec((None, tn, tk), lambda i, j, kk: (kk // per_b, j, kk % per_b))
    if out_shards == 1:
        o_spec = pl.BlockSpec((tm, tn), lambda i, j, kk: (i, j))
        o_shape = (m, n)
    else:
        per_o = n // out_shards // tn
        o_spec = pl.BlockSpec((None, tm, tn), lambda i, j, kk: (j // per_o, i, j % per_o))
        o_shape = (out_shards, m, n // out_shards)
    has_res = residual is not None

    def body(a_ref, b_ref, *rest):
        o_ref = rest[-1]
        kk = pl.program_id(2)
        part = _dot(a_ref[...].astype(BF16), b_ref[...].astype(BF16), dims)
        if has_res:
            @pl.when(kk == 0)
            def _():
                o_ref[...] = part + rest[0][...]
        else:
            @pl.when(kk == 0)
            def _():
                o_ref[...] = part.astype(out_dtype)

        if nk > 1:
            @pl.when(kk > 0)
            def _():
                o_ref[...] += part

    assert out_dtype == F32 or nk == 1
    return pl.pallas_call(
        body, name=name, grid=(m // tm, n // tn, nk),
        in_specs=[a_spec, b_spec] + ([o_spec] if has_res else []),
        out_specs=o_spec, out_shape=jax.ShapeDtypeStruct(o_shape, out_dtype),
        compiler_params=pltpu.CompilerParams(dimension_semantics=("parallel", "parallel", "arbitrary")),
    )(a, b, *([residual] if has_res else []))


def _rows(fn, *, name, steps, ins, outs, accs=(), scratch=()):
    ni, no, na = len(ins), len(outs), len(accs)

    def body(*refs):
        in_refs, out_refs = refs[:ni], refs[ni:ni + no]
        acc_refs, scr = refs[ni + no:ni + no + na], refs[ni + no + na:]
        i = pl.program_id(0)

        @pl.when(i == 0)
        def _():
            for r in acc_refs:
                r[...] = jnp.zeros(r.shape, r.dtype)

        fn(i, in_refs, out_refs, acc_refs, scr)

    def full(shape):
        nd = len(shape)
        return pl.BlockSpec(tuple(shape), lambda i: (0,) * nd)

    res = pl.pallas_call(
        body, name=name, grid=(steps,),
        in_specs=[pl.BlockSpec(bs, im) for _, bs, im in ins],
        out_specs=[pl.BlockSpec(bs, im) for _, _, bs, im in outs] + [full(s) for s, _ in accs],
        out_shape=[jax.ShapeDtypeStruct(s, d) for s, d, _, _ in outs] + [jax.ShapeDtypeStruct(s, d) for s, d in accs],
        scratch_shapes=list(scratch),
        compiler_params=pltpu.CompilerParams(dimension_semantics=("arbitrary",)),
    )(*[a for a, _, _ in ins])
    return res


def _rb(arr, bm, cb=0, width=None):
    w = arr.shape[1] if width is None else width
    return (arr, (bm, w), lambda i: (i, cb))


def _const(arr):
    nd = arr.ndim
    return (arr, tuple(arr.shape), lambda i: (0,) * nd)


def _ro(t, w, dtype, bm):
    return ((t, w), dtype, (bm, w), lambda i: (i, 0))


def _rot(t, w, dtype, bm):
    return ((w, t), dtype, (w, bm), lambda i: (0, i))


def _rmsnorm(x, g, *, name, bm=512):
    t, d = x.shape
    bm = _tile(t, bm)

    def fn(i, ins, outs, accs, scr):
        h = _rms_fwd(ins[0][...], ins[1][...])
        outs[0][...] = h.astype(BF16)
        outs[1][...] = h.T.astype(BF16)

    return _rows(fn, name=name, steps=t // bm, ins=[_rb(x, bm), _const(g)], outs=[_ro(t, d, BF16, bm), _rot(t, d, BF16, bm)])


def _rmsnorm_bwd(dh, x, g, dres, *, name, bm=512):
    t, d = x.shape
    bm = _tile(t, bm)

    def fn(i, ins, outs, accs, scr):
        dx, dgrow = _rms_bwd(ins[0][...], ins[1][...], ins[2][...])
        outs[0][...] = ins[3][...] + dx
        accs[0][...] += _colsum(dgrow)

    return _rows(fn, name=name, steps=t // bm, ins=[_rb(dh, bm), _rb(x, bm), _const(g), _rb(dres, bm)],
                 outs=[_ro(t, d, F32, bm)], accs=[((1, d), F32)])


def _gate(o, p, gcb, *, name, bm=512):
    t, w = o.shape
    bm = _tile(t, bm)

    def fn(i, ins, outs, accs, scr):
        y = ins[0][...] * _silu(ins[1][...])
        outs[0][...] = y.astype(BF16)
        outs[1][...] = y.T.astype(BF16)

    return _rows(fn, name=name, steps=t // bm, ins=[_rb(o, bm), _rb(p, bm, gcb, w)],
                 outs=[_ro(t, w, BF16, bm), _rot(t, w, BF16, bm)])


def _gate_bwd(dy, o, p, gcb, *, name, bm=512):
    t, w = o.shape
    bm = _tile(t, bm)

    def fn(i, ins, outs, accs, scr):
        dy_, o_, g_ = ins[0][...], ins[1][...], ins[2][...]
        outs[0][...] = dy_ * _silu(g_)
        outs[1][...] = dy_ * o_ * _dsilu(g_)

    return _rows(fn, name=name, steps=t // bm, ins=[_rb(dy, bm), _rb(o, bm), _rb(p, bm, gcb, w)],
                 outs=[_ro(t, w, F32, bm), _ro(t, w, F32, bm)])


def _loss_head(x, g, tgt, *, name, bm=512):
    t, d = x.shape
    bm = _tile(t, bm)

    def fn(i, ins, outs, accs, scr):
        x_, g_, tg = ins[0][...], ins[1][...], ins[2][...]
        err = _rms_fwd(x_, g_) - tg
        part = 0.5 * jnp.sum(jnp.sum(err * err, axis=-1, keepdims=True), axis=0, keepdims=True) / d
        dx, dgrow = _rms_bwd(err / d, x_, g_)
        outs[0][...] = dx
        accs[0][...] += _colsum(dgrow)
        accs[1][...] += jnp.broadcast_to(part, (1, BLK))

    return _rows(fn, name=name, steps=t // bm, ins=[_rb(x, bm), _const(g), _rb(tgt, bm)],
                 outs=[_ro(t, d, F32, bm)], accs=[((1, d), F32), ((1, BLK), F32)])


def _gmlp_mix_weights(ws_ref, g):
    row = lax.broadcasted_iota(jnp.int32, (BLK, BLK), 0)
    col = lax.broadcasted_iota(jnp.int32, (BLK, BLK), 1)
    tril = col <= row
    return jnp.where(tril, ws_ref[g], 0.0), tril


def _gmlp_fwd(p, ln_g, ln_b, w_s, bs_t, *, name):
    t = p.shape[0]

    def fn(i, ins, outs, accs, scr):
        p_ref, lg, lb, ws_ref, bst = ins
        vn = _ln_stats(_gelu(p_ref[:, GM_W:2 * GM_W]))[0] * lg[...] + lb[...]
        for g in range(GM_G):
            cs = slice(g * BLK, (g + 1) * BLK)
            wt, _ = _gmlp_mix_weights(ws_ref, g)
            s = _dot(wt.astype(BF16), vn[:, cs].astype(BF16)) + bst[:, g:g + 1]
            u = _gelu(p_ref[:, cs])
            gate = p_ref[:, 2 * GM_W + g * BLK:2 * GM_W + (g + 1) * BLK]
            y = u * s * _silu(gate)
            outs[0][:, cs] = y.astype(BF16)
            outs[1][cs, :] = y.T.astype(BF16)

    return _rows(fn, name=name, steps=t // BLK, ins=[_rb(p, BLK), _const(ln_g), _const(ln_b), _const(w_s), _const(bs_t)],
                 outs=[_ro(t, GM_W, BF16, BLK), _rot(t, GM_W, BF16, BLK)])


def _gmlp_bwd(dy, p, ln_g, ln_b, w_s, bs_t, *, name):
    t = p.shape[0]

    def fn(i, ins, outs, accs, scr):
        dy_ref, p_ref, lg, lb, ws_ref, bst = ins
        dp_ref = outs[0]
        dlg, dlb, dws, dbst = accs
        dvn_ref = scr[0]
        v_pre = p_ref[:, GM_W:2 * GM_W]
        xh, r = _ln_stats(_gelu(v_pre))
        vn = xh * lg[...] + lb[...]
        for g in range(GM_G):
            cs = slice(g * BLK, (g + 1) * BLK)
            gs = slice(2 * GM_W + g * BLK, 2 * GM_W + (g + 1) * BLK)
            wt, tril = _gmlp_mix_weights(ws_ref, g)
            vg = vn[:, cs].astype(BF16)
            s = _dot(wt.astype(BF16), vg) + bst[:, g:g + 1]
            u_pre, gate, dyg = p_ref[:, cs], p_ref[:, gs], dy_ref[:, cs]
            u = _gelu(u_pre)
            dos = dyg * _silu(gate)
            dp_ref[:, gs] = dyg * u * s * _dsilu(gate)
            dp_ref[:, cs] = dos * s * _dgelu(u_pre)
            ds = (dos * u).astype(BF16)
            dws[g] += jnp.where(tril, _dot(ds, vg, _NT), 0.0)
            dbst[:, g:g + 1] += jnp.sum(dos * u, axis=1, keepdims=True)
            dvn_ref[:, cs] = _dot(wt.astype(BF16), ds, _TN)
        dvn = dvn_ref[...]
        dlg[...] += _colsum(dvn * xh)
        dlb[...] += _colsum(dvn)
        dp_ref[:, GM_W:2 * GM_W] = _ln_bwd(dvn, xh, r, lg[...]) * _dgelu(v_pre)

    return _rows(fn, name=name, steps=t // BLK,
                 ins=[_rb(dy, BLK), _rb(p, BLK), _const(ln_g), _const(ln_b), _const(w_s), _const(bs_t)],
                 outs=[_ro(t, 3 * GM_W, F32, BLK)],
                 accs=[((1, GM_W), F32), ((1, GM_W), F32), ((GM_G, BLK, BLK), F32), ((BLK, GM_G), F32)],
                 scratch=[pltpu.VMEM((BLK, GM_W), F32)])


CV_BM = 128
CV_RC = 8
SUBLANES = 8
CV_FWD_OFFS = [HALO - (CV_K - 1) + k for k in range(CV_K)]
CV_BWD_OFFS = [CV_K - 1 - k for k in range(CV_K)]


def _conv_halo_prev(p, cb, bm):
    per = bm // HALO
    return (p, (HALO, CV_W), lambda i: (jnp.maximum(i * per - 1, 0), cb))


def _conv_scratch(bm):
    return [pltpu.VMEM((bm + HALO, CV_W), F32), pltpu.VMEM((SUBLANES - 1, bm + HALO - SUBLANES, CV_W), F32),
            pltpu.VMEM((bm, CV_W), F32)]


def _conv_shift_copies(ext_ref, sh_ref):
    rows = sh_ref.shape[1]
    for b in range(1, SUBLANES):
        sh_ref[b - 1] = ext_ref[pl.ds(b, rows), :]


def _conv_window(ext_ref, sh_ref, off, r0, rows):
    b = off % SUBLANES
    src = ext_ref if b == 0 else sh_ref.at[b - 1]
    return src[pl.ds(r0 + (off - b), rows), :]


def _conv_taps(ext_ref, sh_ref, cw_ref, y_ref, offs):
    bm = y_ref.shape[0]

    def chunk(ci, c):
        r0 = pl.multiple_of(ci * CV_RC, CV_RC)
        acc = jnp.zeros((CV_RC, CV_W), F32)
        for k in range(CV_K):
            acc = acc + cw_ref[pl.ds(k * SUBLANES, CV_RC), :] * _conv_window(ext_ref, sh_ref, offs[k], r0, CV_RC)
        y_ref[pl.ds(r0, CV_RC), :] = acc
        return c

    lax.fori_loop(0, bm // CV_RC, chunk, 0)


def _conv_dweights(dy1_ref, ext_ref, sh_ref, dcw_ref):
    bm = dy1_ref.shape[0]
    groups = 4
    for k in range(CV_K):
        def step(ci, acc, off=CV_FWD_OFFS[k]):
            prods = []
            for u in range(groups):
                r0 = pl.multiple_of((ci * groups + u) * CV_RC, CV_RC)
                prods.append(dy1_ref[pl.ds(r0, CV_RC), :] * _conv_window(ext_ref, sh_ref, off, r0, CV_RC))
            return acc + ((prods[0] + prods[1]) + (prods[2] + prods[3]))

        dcw_ref[k:k + 1, :] += _colsum(lax.fori_loop(0, bm // (CV_RC * groups), step, jnp.zeros((CV_RC, CV_W), F32)))


def _conv_fill(i, ext_ref, a_prev, b_prev, a, b, bm, seq):
    keep = jnp.where((i % (seq // bm)) == 0, 0.0, 1.0)
    ext_ref[pl.ds(0, HALO), :] = keep * (a_prev * _sigmoid(b_prev))
    ext_ref[pl.ds(HALO, bm), :] = a * _sigmoid(b)


def _conv_fwd(p, cw, cb, ln_g, ln_b, seq, *, name, bm=CV_BM):
    t = p.shape[0]

    def fn(i, ins, outs, accs, scr):
        a, b, gate, ap, bp = [r[...] for r in ins[:5]]
        cw_ref, cb_, lg, lb = ins[5], ins[6][...], ins[7][...], ins[8][...]
        ext, sh, y = scr
        _conv_fill(i, ext, ap, bp, a, b, bm, seq)
        _conv_shift_copies(ext, sh)
        _conv_taps(ext, sh, cw_ref, y, CV_FWD_OFFS)
        y2 = _ln_stats(y[...] + cb_)[0] * lg + lb
        out = _silu(y2) * _silu(gate)
        outs[0][...] = out.astype(BF16)
        outs[1][...] = out.T.astype(BF16)

    return _rows(fn, name=name, steps=t // bm,
                 ins=[_rb(p, bm, 0, CV_W), _rb(p, bm, 1, CV_W), _rb(p, bm, 2, CV_W),
                      _conv_halo_prev(p, 0, bm), _conv_halo_prev(p, 1, bm),
                      _const(cw), _const(cb), _const(ln_g), _const(ln_b)],
                 outs=[_ro(t, CV_W, BF16, bm), _rot(t, CV_W, BF16, bm)], scratch=_conv_scratch(bm))


def _conv_bwd_post(dy, p, cw, cb, ln_g, ln_b, seq, *, name, bm=CV_BM):
    t = p.shape[0]

    def fn(i, ins, outs, accs, scr):
        dy_, a, b, gate, ap, bp = [r[...] for r in ins[:6]]
        cw_ref, cb_, lg, lb = ins[6], ins[7][...], ins[8][...], ins[9][...]
        dlg, dlb, dcb, dcw = accs
        ext, sh, y = scr
        _conv_fill(i, ext, ap, bp, a, b, bm, seq)
        _conv_shift_copies(ext, sh)
        _conv_taps(ext, sh, cw_ref, y, CV_FWD_OFFS)
        xh, r = _ln_stats(y[...] + cb_)
        y2 = xh * lg + lb
        outs[1][...] = dy_ * _silu(y2) * _dsilu(gate)
        dy2 = dy_ * _silu(gate) * _dsilu(y2)
        dlg[...] += _colsum(dy2 * xh)
        dlb[...] += _colsum(dy2)
        dy1 = _ln_bwd(dy2, xh, r, lg)
        outs[0][...] = dy1
        dcb[...] += _colsum(dy1)
        _conv_dweights(outs[0], ext, sh, dcw)

    return _rows(fn, name=name, steps=t // bm,
                 ins=[_rb(dy, bm), _rb(p, bm, 0, CV_W), _rb(p, bm, 1, CV_W), _rb(p, bm, 2, CV_W),
                      _conv_halo_prev(p, 0, bm), _conv_halo_prev(p, 1, bm),
                      _const(cw), _const(cb), _const(ln_g), _const(ln_b)],
                 outs=[_ro(t, CV_W, F32, bm), _ro(t, CV_W, F32, bm)],
                 accs=[((1, CV_W), F32), ((1, CV_W), F32), ((1, CV_W), F32), ((CV_K, CV_W), F32)],
                 scratch=_conv_scratch(bm))


def _conv_bwd_pre(dy1, dgate, p, cw, seq, *, name, bm=CV_BM):
    t = p.shape[0]
    per = bm // HALO
    last_halo = t // HALO - 1

    def fn(i, ins, outs, accs, scr):
        d1, d1n, dg, a, b = [r[...] for r in ins[:5]]
        ext, sh, y = scr
        keep = jnp.where((i % (seq // bm)) == (seq // bm - 1), 0.0, 1.0)
        ext[pl.ds(0, bm), :] = d1
        ext[pl.ds(bm, HALO), :] = keep * d1n
        _conv_shift_copies(ext, sh)
        _conv_taps(ext, sh, ins[5], y, CV_BWD_OFFS)
        dy0 = y[...]
        sb = _sigmoid(b)
        outs[0][:, 0:CV_W] = dy0 * sb
        outs[0][:, CV_W:2 * CV_W] = dy0 * a * sb * (1.0 - sb)
        outs[0][:, 2 * CV_W:3 * CV_W] = dg

    return _rows(fn, name=name, steps=t // bm,
                 ins=[_rb(dy1, bm), (dy1, (HALO, CV_W), lambda i: (jnp.minimum((i + 1) * per, last_halo), 0)),
                      _rb(dgate, bm), _rb(p, bm, 0, CV_W), _rb(p, bm, 1, CV_W), _const(cw)],
                 outs=[_ro(t, 3 * CV_W, F32, bm)], scratch=_conv_scratch(bm))[0]


def _iotas():
    row = lax.broadcasted_iota(jnp.int32, (BLK, BLK), 0)
    col = lax.broadcasted_iota(jnp.int32, (BLK, BLK), 1)
    return row, col


def _heads(x, head0):
    if head0.shape != x.shape:
        head0 = lax.broadcasted_iota(jnp.int32, x.shape, 1) < HEAD_DIM
    return jnp.where(head0, x, 0.0).astype(BF16), jnp.where(head0, 0.0, x).astype(BF16)


def _pair_spec(seq, off):
    return pl.BlockSpec((seq, BLK), lambda b, hp: (b, off + hp))


def _stat_spec(seq):
    return pl.BlockSpec((None, None, seq, BLK), lambda b, hp: (b, hp, 0, 0))


_ATT_PARAMS = dict(compiler_params=pltpu.CompilerParams(dimension_semantics=("parallel", "parallel")))
_SCALE = 1.0 / math.sqrt(HEAD_DIM)


Q_BLOCK = 256
KEY_BLOCK = 256


def _stack_heads(x, head0, scale=None):
    if scale is not None:
        x = x * scale
    return jnp.concatenate(_heads(x, head0), axis=0)


def _pair_cols(x, head0, fill):
    a = jnp.max(jnp.where(head0, x, fill), axis=1, keepdims=True)
    b = jnp.max(jnp.where(head0, fill, x), axis=1, keepdims=True)
    return jnp.concatenate([a, b], axis=0)


def _causal_mask(t0, s0, tq, kw, inclusive):
    row = lax.broadcasted_iota(jnp.int32, (2 * tq, kw), 0) & (tq - 1)
    col = lax.broadcasted_iota(jnp.int32, (2 * tq, kw), 1)
    return (s0 + col) <= (t0 + row) if inclusive else (s0 + col) < (t0 + row)


def _sub(x, j):
    return x[:, j * BLK:(j + 1) * BLK]


def _tri_blocks(kw, relation):
    r = lax.broadcasted_iota(jnp.int32, (kw, kw), 0)
    c = lax.broadcasted_iota(jnp.int32, (kw, kw), 1)
    return (((r // BLK) == (c // BLK)) & relation(r, c)).astype(BF16)


def _block_cumsum(x, tri, ksub):
    hi = x.astype(BF16)
    lo = (x - hi.astype(F32)).astype(BF16)
    cs = _dot(jnp.concatenate([hi, lo], axis=0), tri)
    n = x.shape[0]
    cs = cs[:n] + cs[n:]
    return [_sub(cs, j) for j in range(ksub)], [jnp.sum(_sub(x, j), axis=1, keepdims=True) for j in range(ksub)]


def _sb_terms(qs, k, mask):
    return _sb_terms_z(_dot(qs, k, _NT), mask)


def _sb_terms_z(z, mask):
    t = jnp.log(1.0 + jnp.exp(-jnp.abs(z)))
    lsz = jnp.minimum(z, 0.0) - t
    lr = lsz - z
    if mask is not None:
        lr = jnp.where(mask, lr, 0.0)
    return lsz, lr


def _sb_fwd(p, nb, seq, *, name):
    tq = min(Q_BLOCK, seq)
    nq = seq // tq
    kw = min(KEY_BLOCK, seq)
    ksub = kw // BLK

    def body(q_ref, k_ref, v_ref, o_ref, tot_ref):
        row, col = _iotas()
        colq = lax.broadcasted_iota(jnp.int32, (tq, BLK), 1)
        head0 = colq < HEAD_DIM
        upper = _tri_blocks(kw, lambda j, s: j > s)

        def qblock(qb, c):
            t0 = pl.multiple_of(qb * tq, tq)
            qs = _stack_heads(q_ref[pl.ds(t0, tq), :], head0, _SCALE)
            diag = (t0 + tq - 1) // kw

            def kblock(kb, carry, masked):
                acc, run = carry
                s0 = pl.multiple_of(kb * kw, kw)
                k = k_ref[pl.ds(s0, kw), :].astype(BF16)
                v0, v1 = _heads(v_ref[pl.ds(s0, kw), :], head0)
                mask = _causal_mask(t0, s0, tq, kw, False)[:tq] if masked else None
                zs = [_dot(qs[h * tq:(h + 1) * tq], k, _NT) for h in range(2)]
                terms = []
                for h in range(2):
                    lsz, lr = _sb_terms_z(zs[h], mask)
                    terms.append((lsz,) + _block_cumsum(lr, upper, ksub))
                runs = []
                for h, vh in enumerate((v0, v1)):
                    lsz, after, total = terms[h]
                    r = run[h]
                    ws = [None] * ksub
                    for j in reversed(range(ksub)):
                        w = jnp.exp(_sub(lsz, j) + after[j] + r)
                        if masked:
                            w = jnp.where(_sub(mask, j), w, 0.0)
                        ws[j] = w.astype(BF16)
                        r = r + total[j]
                    acc = acc + _dot(jnp.concatenate(ws, axis=1), vh)
                    runs.append(r)
                return acc, tuple(runs)

            zc = jnp.zeros((tq, 1), F32)
            carry = kblock(diag, (jnp.zeros((tq, BLK), F32), (zc, zc)), True)
            acc, run = lax.fori_loop(0, diag, lambda it, cr: kblock(diag - 1 - it, cr, False), carry)
            o_ref[pl.ds(t0, tq), :] = acc
            tot_ref[pl.ds(t0, tq), :] = jnp.where(head0, run[0], run[1])
            return c

        lax.fori_loop(0, nq, qblock, 0)

    return pl.pallas_call(
        body, name=name, grid=(nb, PAIRS),
        in_specs=[_pair_spec(seq, 0), _pair_spec(seq, PAIRS), _pair_spec(seq, 2 * PAIRS)],
        out_specs=[_pair_spec(seq, 0), _stat_spec(seq)],
        out_shape=[jax.ShapeDtypeStruct((nb * seq, D_MODEL), F32), jax.ShapeDtypeStruct((nb, PAIRS, seq, BLK), F32)],
        **_ATT_PARAMS,
    )(p, p, p)


def _sb_bwd(p, do, tot, nb, seq, *, name):
    tq = min(Q_BLOCK, seq)
    nq = seq // tq
    kw = min(KEY_BLOCK, seq)
    ksub = kw // BLK

    def body(q_ref, k_ref, v_ref, do_ref, tot_ref, dq_ref, dk_ref, dv_ref):
        row, col = _iotas()
        colq = lax.broadcasted_iota(jnp.int32, (tq, BLK), 1)
        head0 = colq < HEAD_DIM
        lower_incl = _tri_blocks(kw, lambda j, s: j <= s)
        lower_strict = _tri_blocks(kw, lambda s, j: s < j)
        dk_ref[...] = jnp.zeros(dk_ref.shape, F32)
        dv_ref[...] = jnp.zeros(dv_ref.shape, F32)

        def qblock(qb, c):
            t0 = pl.multiple_of(qb * tq, tq)
            qs = _stack_heads(q_ref[pl.ds(t0, tq), :], head0, _SCALE)
            dos = _stack_heads(do_ref[pl.ds(t0, tq), :], head0)
            tot = tot_ref[pl.ds(t0, tq), :]
            swapped = pltpu.roll(tot, HEAD_DIM, 1)
            tts = (jnp.where(head0, tot, swapped), jnp.where(head0, swapped, tot))
            diag = (t0 + tq - 1) // kw

            def kblock(kb, carry, masked):
                dq, pfs, efs = carry
                s0 = pl.multiple_of(kb * kw, kw)
                kf = k_ref[pl.ds(s0, kw), :]
                k = kf.astype(BF16)
                khs = _heads(kf, head0)
                v = v_ref[pl.ds(s0, kw), :].astype(BF16)
                mask = _causal_mask(t0, s0, tq, kw, False)[:tq] if masked else None
                zs = [_dot(qs[h * tq:(h + 1) * tq], k, _NT) for h in range(2)]
                dws = [_dot(dos[h * tq:(h + 1) * tq], v, _NT) for h in range(2)]
                first = []
                for h in range(2):
                    lsz, lr = _sb_terms_z(zs[h], None)
                    lrm = jnp.where(mask, lr, 0.0) if masked else lr
                    first.append((lsz, lr) + _block_cumsum(lrm, lower_incl, ksub))
                second, pfs_out = [], []
                for h in range(2):
                    lsz, lr, incl, total = first[h]
                    pf = pfs[h]
                    ws, ews = [], []
                    for j in range(ksub):
                        w = jnp.exp(_sub(lsz, j) + (tts[h] - pf - incl[j]))
                        if masked:
                            w = jnp.where(_sub(mask, j), w, 0.0)
                        pf = pf + total[j]
                        ws.append(w.astype(BF16))
                        ews.append(_sub(dws[h], j) * w)
                    pfs_out.append(pf)
                    second.append((ws, ews) + _block_cumsum(jnp.concatenate(ews, axis=1), lower_strict, ksub))
                dz_h, efs_out = [], []
                for h in range(2):
                    lsz, lr = first[h][:2]
                    ws, ews, before, etotal = second[h]
                    ef = efs[h]
                    dzs = []
                    for j in range(ksub):
                        dz = ews[j] * jnp.exp(_sub(lr, j)) - (ef + before[j]) * jnp.exp(_sub(lsz, j))
                        ef = ef + etotal[j]
                        if masked:
                            dz = jnp.where(_sub(mask, j), dz, 0.0)
                        dzs.append(dz.astype(BF16))
                    efs_out.append(ef)
                    dz_h.append(jnp.concatenate(dzs, axis=1))
                    dq = dq + _dot(dz_h[h], khs[h])
                w = jnp.concatenate([jnp.concatenate(second[h][0], axis=1) for h in range(2)], axis=0)
                dk_ref[pl.ds(s0, kw), :] += _dot(jnp.concatenate(dz_h, axis=0), qs, _TN)
                dv_ref[pl.ds(s0, kw), :] += _dot(w, dos, _TN)
                return dq, tuple(pfs_out), tuple(efs_out)

            zc = jnp.zeros((tq, 1), F32)
            carry = lax.fori_loop(0, diag, lambda kb, cr: kblock(kb, cr, False), (jnp.zeros((tq, BLK), F32), (zc, zc), (zc, zc)))
            dq_ref[pl.ds(t0, tq), :] = kblock(diag, carry, True)[0] * _SCALE
            return c

        lax.fori_loop(0, nq, qblock, 0)

    t = nb * seq
    return pl.pallas_call(
        body, name=name, grid=(nb, PAIRS),
        in_specs=[_pair_spec(seq, 0), _pair_spec(seq, PAIRS), _pair_spec(seq, 2 * PAIRS), _pair_spec(seq, 0), _stat_spec(seq)],
        out_specs=[_pair_spec(seq, 0)] * 3,
        out_shape=[jax.ShapeDtypeStruct((t, D_MODEL), F32)] * 3,
        **_ATT_PARAMS,
    )(p, p, p, do, tot)


def _fox_cum(f, bf, nb, seq, *, name):
    def body(f_ref, bf_ref, cc_ref, cr_ref):
        row, col = _iotas()
        lower = (col <= row).astype(BF16)
        carry = jnp.zeros((1, BLK), F32)
        for blk in range(seq // BLK):
            rs = slice(blk * BLK, (blk + 1) * BLK)
            lf = jnp.where(col < HEADS, _log_sigmoid(f_ref[rs, :] + bf_ref[...]), 0.0)
            cc = _dot3_left(lower, lf) + carry
            cc_ref[rs, :] = cc
            cr_ref[:, rs] = cc.T[0:HEADS, :]
            carry = carry + _colsum(lf)

    return pl.pallas_call(
        body, name=name, grid=(nb,),
        in_specs=[pl.BlockSpec((seq, BLK), lambda b: (b, 0)), pl.BlockSpec((1, BLK), lambda b: (0, 0))],
        out_specs=[pl.BlockSpec((seq, BLK), lambda b: (b, 0)), pl.BlockSpec((None, HEADS, seq), lambda b: (b, 0, 0))],
        out_shape=[jax.ShapeDtypeStruct((nb * seq, BLK), F32), jax.ShapeDtypeStruct((nb, HEADS, seq), F32)],
        compiler_params=pltpu.CompilerParams(dimension_semantics=("parallel",)),
    )(f, bf)


def _fox_cum_bwd(dcr, dcc, f, bf, nb, seq, *, name):
    def body(dcr_ref, dcc_ref, f_ref, bf_ref, df_ref, dbf_ref):
        row, col = _iotas()
        upper_incl = (col >= row).astype(BF16)

        @pl.when(pl.program_id(0) == 0)
        def _():
            dbf_ref[...] = jnp.zeros((1, BLK), F32)

        carry = jnp.zeros((1, BLK), F32)
        for blk in reversed(range(seq // BLK)):
            rs = slice(blk * BLK, (blk + 1) * BLK)
            dc = dcr_ref[:, rs].T + dcc_ref[rs, :]
            dlf = _dot3_left(upper_incl, dc) + carry
            carry = carry + _colsum(dc)
            fl = f_ref[rs, :] + bf_ref[...]
            df = jnp.where(col < HEADS, dlf * _sigmoid(-fl), 0.0)
            df_ref[rs, :] = df
            dbf_ref[...] += _colsum(df)

    return pl.pallas_call(
        body, name=name, grid=(nb,),
        in_specs=[pl.BlockSpec((None, BLK, seq), lambda b: (b, 0, 0)), pl.BlockSpec((seq, BLK), lambda b: (b, 0)),
                  pl.BlockSpec((seq, BLK), lambda b: (b, 0)), pl.BlockSpec((1, BLK), lambda b: (0, 0))],
        out_specs=[pl.BlockSpec((seq, BLK), lambda b: (b, 0)), pl.BlockSpec((1, BLK), lambda b: (0, 0))],
        out_shape=[jax.ShapeDtypeStruct((nb * seq, BLK), F32), jax.ShapeDtypeStruct((1, BLK), F32)],
        compiler_params=pltpu.CompilerParams(dimension_semantics=("arbitrary",)),
    )(dcr, dcc, f, bf)


def _fox_cum_cols(cc_ref, t0, tq, colq, hp):
    cc = cc_ref[pl.ds(t0, tq), :]
    c0 = jnp.sum(jnp.where(colq == 2 * hp, cc, 0.0), axis=1, keepdims=True)
    c1 = jnp.sum(jnp.where(colq == 2 * hp + 1, cc, 0.0), axis=1, keepdims=True)
    return c0, c1


def _fox_bias(c0, c1, cr_ref, s0, kw):
    return jnp.concatenate([c0 - cr_ref[0:1, pl.ds(s0, kw)], c1 - cr_ref[1:2, pl.ds(s0, kw)]], axis=0)


def _fox_fwd(p, cc, cr, nb, seq, *, name):
    tq = min(Q_BLOCK, seq)
    nq = seq // tq
    kw = min(KEY_BLOCK, seq)
    ksub = kw // BLK

    def body(q_ref, k_ref, v_ref, cc_ref, cr_ref, o_ref, lse_ref):
        hp = pl.program_id(1)
        row, col = _iotas()
        colq = lax.broadcasted_iota(jnp.int32, (tq, BLK), 1)
        head0 = colq < HEAD_DIM

        def qblock(qb, c):
            t0 = pl.multiple_of(qb * tq, tq)
            qs = _stack_heads(q_ref[pl.ds(t0, tq), :], head0, _SCALE)
            c0, c1 = _fox_cum_cols(cc_ref, t0, tq, colq, hp)
            diag = (t0 + tq - 1) // kw

            def kblock(kb, carry, masked):
                accs, ms = carry
                s0 = pl.multiple_of(kb * kw, kw)
                k = k_ref[pl.ds(s0, kw), :].astype(BF16)
                vf = v_ref[pl.ds(s0, kw), :]
                own0 = lax.broadcasted_iota(jnp.int32, vf.shape, 1) < HEAD_DIM
                vs = (jnp.where(own0, vf, 1.0).astype(BF16), jnp.where(own0, 1.0, vf).astype(BF16))
                mask = _causal_mask(t0, s0, tq, kw, True)[:tq] if masked else None
                zs = [_dot(qs[h * tq:(h + 1) * tq], k, _NT) for h in range(2)]
                parts = []
                for h, ch in enumerate((c0, c1)):
                    s = zs[h] + (ch - cr_ref[h:h + 1, pl.ds(s0, kw)])
                    if masked:
                        s = jnp.where(mask, s, -jnp.inf)
                    m_new = jnp.maximum(ms[h], jnp.max(s, axis=1, keepdims=True))
                    parts.append((jnp.exp(s - m_new).astype(BF16), jnp.exp(ms[h] - m_new), m_new))
                return (tuple(accs[h] * parts[h][1] + _dot(parts[h][0], vs[h]) for h in range(2)),
                        tuple(parts[h][2] for h in range(2)))

            zeros, ninf = jnp.zeros((tq, BLK), F32), jnp.full((tq, 1), -jnp.inf, F32)
            carry = lax.fori_loop(0, diag, lambda kb, cr: kblock(kb, cr, False), ((zeros, zeros), (ninf, ninf)))
            (acc0, acc1), (m0, m1) = kblock(diag, carry, True)
            l = jnp.where(head0, pltpu.roll(acc0, HEAD_DIM, 1), pltpu.roll(acc1, HEAD_DIM, 1))
            o_ref[pl.ds(t0, tq), :] = jnp.where(head0, acc0, acc1) / l
            lse_ref[pl.ds(t0, tq), :] = jnp.where(head0, m0, m1) + jnp.log(l)
            return c

        lax.fori_loop(0, nq, qblock, 0)

    return pl.pallas_call(
        body, name=name, grid=(nb, PAIRS),
        in_specs=[_pair_spec(seq, 0), _pair_spec(seq, PAIRS), _pair_spec(seq, 2 * PAIRS),
                  pl.BlockSpec((seq, BLK), lambda b, hp: (b, 0)), pl.BlockSpec((None, None, 8, seq), lambda b, hp: (b, hp, 0, 0))],
        out_specs=[_pair_spec(seq, 0), _stat_spec(seq)],
        out_shape=[jax.ShapeDtypeStruct((nb * seq, D_MODEL), F32), jax.ShapeDtypeStruct((nb, PAIRS, seq, BLK), F32)],
        **_ATT_PARAMS,
    )(p, p, p, cc, cr)


def _fox_bwd(p, do, o, lse, cc, cr, nb, seq, *, name):
    tq = min(Q_BLOCK, seq)
    nq = seq // tq
    kw = min(KEY_BLOCK, seq)
    ksub = kw // BLK

    def body(q_ref, k_ref, v_ref, do_ref, o_ref, lse_ref, cc_ref, cr_ref, dq_ref, dk_ref, dv_ref, dcr_ref, dcc_ref):
        hp = pl.program_id(1)
        row, col = _iotas()
        colq = lax.broadcasted_iota(jnp.int32, (tq, BLK), 1)
        head0 = colq < HEAD_DIM
        dk_ref[...] = jnp.zeros(dk_ref.shape, F32)
        dv_ref[...] = jnp.zeros(dv_ref.shape, F32)
        dcr_ref[...] = jnp.zeros(dcr_ref.shape, F32)

        @pl.when(hp == 0)
        def _():
            dcc_ref[...] = jnp.zeros(dcc_ref.shape, F32)

        def qblock(qb, c):
            t0 = pl.multiple_of(qb * tq, tq)
            qs = _stack_heads(q_ref[pl.ds(t0, tq), :], head0, _SCALE)
            dof = do_ref[pl.ds(t0, tq), :]
            dos = _stack_heads(dof, head0)
            prod = dof * o_ref[pl.ds(t0, tq), :]
            dl = jnp.concatenate([jnp.sum(jnp.where(head0, prod, 0.0), axis=1, keepdims=True),
                                  jnp.sum(jnp.where(head0, 0.0, prod), axis=1, keepdims=True)], axis=0)
            lse = _pair_cols(lse_ref[pl.ds(t0, tq), :], head0, -jnp.inf)
            c0, c1 = _fox_cum_cols(cc_ref, t0, tq, colq, hp)
            diag = (t0 + tq - 1) // kw

            def kblock(kb, carry, masked):
                dq, rs = carry
                s0 = pl.multiple_of(kb * kw, kw)
                kf = k_ref[pl.ds(s0, kw), :]
                k = kf.astype(BF16)
                k0, k1 = _heads(kf, head0)
                v = v_ref[pl.ds(s0, kw), :].astype(BF16)
                mask = _causal_mask(t0, s0, tq, kw, True)[:tq] if masked else None
                zs = [_dot(qs[h * tq:(h + 1) * tq], k, _NT) for h in range(2)]
                dps = [_dot(dos[h * tq:(h + 1) * tq], v, _NT) for h in range(2)]
                prs, dss, rss = [], [], []
                for h, (ch, kh) in enumerate(((c0, k0), (c1, k1))):
                    rows = slice(h * tq, (h + 1) * tq)
                    pr = jnp.exp(zs[h] + (ch - cr_ref[h:h + 1, pl.ds(s0, kw)]) - lse[rows])
                    if masked:
                        pr = jnp.where(mask, pr, 0.0)
                    ds = pr * (dps[h] - dl[rows])
                    dcr_ref[h:h + 1, pl.ds(s0, kw)] -= _colsum(ds)
                    rss.append(rs[rows] + jnp.sum(ds, axis=1, keepdims=True))
                    prs.append(pr.astype(BF16))
                    dss.append(ds.astype(BF16))
                    dq = dq + _dot(dss[h], kh)
                dk_ref[pl.ds(s0, kw), :] += _dot(jnp.concatenate(dss, axis=0), qs, _TN)
                dv_ref[pl.ds(s0, kw), :] += _dot(jnp.concatenate(prs, axis=0), dos, _TN)
                return dq, jnp.concatenate(rss, axis=0)

            init = (jnp.zeros((tq, BLK), F32), jnp.zeros((2 * tq, 1), F32))
            carry = lax.fori_loop(0, diag, lambda kb, cr: kblock(kb, cr, False), init)
            dq, rs = kblock(diag, carry, True)
            dq_ref[pl.ds(t0, tq), :] = dq * _SCALE
            dcc_ref[pl.ds(t0, tq), :] += jnp.where(colq == 2 * hp, rs[:tq], 0.0) + jnp.where(colq == 2 * hp + 1, rs[tq:], 0.0)
            return c

        lax.fori_loop(0, nq, qblock, 0)

    t = nb * seq
    return pl.pallas_call(
        body, name=name, grid=(nb, PAIRS),
        in_specs=[_pair_spec(seq, 0), _pair_spec(seq, PAIRS), _pair_spec(seq, 2 * PAIRS), _pair_spec(seq, 0), _pair_spec(seq, 0),
                  _stat_spec(seq), pl.BlockSpec((seq, BLK), lambda b, hp: (b, 0)),
                  pl.BlockSpec((None, None, 8, seq), lambda b, hp: (b, hp, 0, 0))],
        out_specs=[_pair_spec(seq, 0)] * 3 + [pl.BlockSpec((None, None, 8, seq), lambda b, hp: (b, hp, 0, 0)),
                                              pl.BlockSpec((seq, BLK), lambda b, hp: (b, 0))],
        out_shape=[jax.ShapeDtypeStruct((t, D_MODEL), F32)] * 3 + [jax.ShapeDtypeStruct((nb, PAIRS, 8, seq), F32),
                                                                     jax.ShapeDtypeStruct((t, BLK), F32)],
        compiler_params=pltpu.CompilerParams(dimension_semantics=("parallel", "arbitrary")),
    )(p, p, p, do, o, lse, cc, cr)


def _row_shards(x):
    return x.reshape(N_CHIPS, x.shape[0] // N_CHIPS, x.shape[1])


def _local_step(x3, tgt3, w, later=None, start_reduce=None):
    nb, seq, d = x3.shape
    t = nb * seq
    x0, tgt = x3.reshape(t, d), tgt3.reshape(t, d)
    g = {}

    a_gain = w["a_norm"].reshape(1, d)
    h_a, ht_a = _rmsnorm(x0, a_gain, name="a_norm_fwd")
    p_a = _matmul(h_a, w["a_w_in"][:3], name="a_in_fwd", out_dtype=BF16)
    gate_a = _matmul(h_a, w["a_w_in"][3], name="a_in_gate_fwd")
    o_a, tot_a = _sb_fwd(p_a, nb, seq, name="a_attn_fwd")
    y_a, yt_a = _gate(o_a, gate_a, 0, name="a_gate_fwd")
    x1 = _matmul(y_a, w["a_w_out"], name="a_out_fwd", residual=x0)

    if later:
        w = {**w, **later[0](x1)}
    b_gain = w["b_norm"].reshape(1, d)
    b_lg, b_lb = w["b_v_ln_g"].reshape(1, GM_W), w["b_v_ln_b"].reshape(1, GM_W)
    b_ws, b_bst = w["b_w_s"].reshape(GM_G, BLK, BLK), w["b_b_s"].reshape(GM_G, BLK).T
    h_b, ht_b = _rmsnorm(x1, b_gain, name="b_norm_fwd")
    p_b = _matmul(h_b, w["b_w_in"], name="b_in_fwd")
    y_b, yt_b = _gmlp_fwd(p_b, b_lg, b_lb, b_ws, b_bst, name="b_mix_fwd")
    x2 = _matmul(y_b, w["b_w_out"], name="b_out_fwd", residual=x1)

    if later:
        w = {**w, **later[1](x2)}
    c_gain = w["c_norm"].reshape(1, d)
    c_cw = jnp.repeat(w["c_conv_w"].reshape(CV_K, CV_W), SUBLANES, axis=0)
    c_cb = w["c_conv_b"].reshape(1, CV_W)
    c_lg, c_lb = w["c_ln_g"].reshape(1, CV_W), w["c_ln_b"].reshape(1, CV_W)
    h_c, ht_c = _rmsnorm(x2, c_gain, name="c_norm_fwd")
    p_c = _matmul(h_c, w["c_w_in"], name="c_in_fwd")
    y_c, yt_c = _conv_fwd(p_c, c_cw, c_cb, c_lg, c_lb, seq, name="c_conv_fwd")
    x3_ = _matmul(y_c, w["c_w_out"], name="c_out_fwd", residual=x2)

    d_gain = w["d_norm"].reshape(1, d)
    d_win = w["d_w_in"].reshape(d, 4 * D_MODEL + HEADS)
    d_wmain = d_win[:, :4 * D_MODEL]
    d_wf = jnp.pad(d_win[:, 4 * D_MODEL:], ((0, 0), (0, BLK - HEADS)))
    d_bf = jnp.pad(w["d_b_f"].reshape(1, HEADS), ((0, 0), (0, BLK - HEADS)))
    h_d, ht_d = _rmsnorm(x3_, d_gain, name="d_norm_fwd")
    p_d = _matmul(h_d, d_wmain[:, :3 * D_MODEL], name="d_in_fwd", out_dtype=BF16)
    gate_d = _matmul(h_d, d_wmain[:, 3 * D_MODEL:], name="d_in_gate_fwd")
    f_d = _matmul(h_d, d_wf, name="d_inf_fwd")
    cc, cr = _fox_cum(f_d, d_bf, nb, seq, name="d_cum_fwd")
    cr = jnp.pad(cr.reshape(nb, PAIRS, 2, seq), ((0, 0), (0, 0), (0, 6), (0, 0)))
    o_d, lse_d = _fox_fwd(p_d, cc, cr, nb, seq, name="d_attn_fwd")
    y_d, yt_d = _gate(o_d, gate_d, 0, name="d_gate_fwd")
    x4 = _matmul(y_d, w["d_w_out"], name="d_out_fwd", residual=x3_)

    f_gain = w["final_norm"].reshape(1, d)
    dx, g_fn, loss_row = _loss_head(x4, f_gain, tgt, name="loss_head")
    g["final_norm"] = g_fn

    g["d_w_out"] = _row_shards(_matmul(yt_d, dx, name="d_out_dw"))
    dy = _matmul(dx, w["d_w_out"], name="d_out_dy", mode="nt")
    do_d, dg_d = _gate_bwd(dy, o_d, gate_d, 0, name="d_gate_bwd")
    dq, dk, dv, dcr, dcc = _fox_bwd(p_d, do_d, o_d, lse_d, cc, cr, nb, seq, name="d_attn_bwd")
    dcr = jnp.pad(dcr[:, :, :2, :].reshape(nb, HEADS, seq), ((0, 0), (0, BLK - HEADS), (0, 0)))
    df, dbf = _fox_cum_bwd(dcr, dcc, f_d, d_bf, nb, seq, name="d_cum_bwd")
    g["d_b_f"] = dbf[:, :HEADS]
    parts = [dq, dk, dv, dg_d]
    dws = [_matmul(ht_d, pt, name=f"d_in_dw{n}") for n, pt in enumerate(parts)]
    dwf = _matmul(ht_d, df, name="d_inf_dw")
    g["d_w_in"] = jnp.concatenate(dws + [dwf[:, :HEADS]], axis=1).reshape(d, N_CHIPS, -1).transpose(1, 0, 2)
    dh = _matmul(df, d_wf, name="d_inf_dh", mode="nt")
    for n, pt in enumerate(parts):
        dh = _matmul(pt, d_wmain[:, n * D_MODEL:(n + 1) * D_MODEL], name=f"d_in_dh{n}", mode="nt", residual=dh)
    dx, g["d_norm"] = _rmsnorm_bwd(dh, x3_, d_gain, dx, name="d_norm_bwd")

    g["c_w_out"] = _row_shards(_matmul(yt_c, dx, name="c_out_dw"))
    dy = _matmul(dx, w["c_w_out"], name="c_out_dy", mode="nt")
    dy1, dgate, g["c_ln_g"], g["c_ln_b"], g["c_conv_b"], g["c_conv_w"] = _conv_bwd_post(
        dy, p_c, c_cw, c_cb, c_lg, c_lb, seq, name="c_conv_bwd_post")
    dp = _conv_bwd_pre(dy1, dgate, p_c, c_cw, seq, name="c_conv_bwd_pre")
    g["c_w_in"] = _matmul(ht_c, dp, name="c_in_dw", out_shards=N_CHIPS)
    dh = _matmul(dp, w["c_w_in"], name="c_in_dh", mode="nt")
    dx, g["c_norm"] = _rmsnorm_bwd(dh, x2, c_gain, dx, name="c_norm_bwd")

    early, b_wout, a_wout = [], w["b_w_out"], w["a_w_out"]
    if start_reduce is not None:
        begun, token = start_reduce({n: g[n] for n in ("d_w_in", "d_w_out", "c_w_in", "c_w_out")}, "grads_cd")
        early.append(begun)
        b_wout = b_wout + token[0, 0].astype(b_wout.dtype)
    g["b_w_out"] = _row_shards(_matmul(yt_b, dx, name="b_out_dw"))
    dy = _matmul(dx, b_wout, name="b_out_dy", mode="nt")
    dp, g["b_v_ln_g"], g["b_v_ln_b"], g["b_w_s"], dbst = _gmlp_bwd(dy, p_b, b_lg, b_lb, b_ws, b_bst, name="b_mix_bwd")
    g["b_b_s"] = dbst.T
    g["b_w_in"] = _matmul(ht_b, dp, name="b_in_dw", out_shards=N_CHIPS)
    dh = _matmul(dp, w["b_w_in"], name="b_in_dh", mode="nt")
    dx, g["b_norm"] = _rmsnorm_bwd(dh, x1, b_gain, dx, name="b_norm_bwd")

    if start_reduce is not None:
        begun, token = start_reduce({n: g[n] for n in ("b_w_in", "b_w_out")}, "grads_b")
        early.append(begun)
        a_wout = a_wout + token[0, 0].astype(a_wout.dtype)
    g["a_w_out"] = _row_shards(_matmul(yt_a, dx, name="a_out_dw"))
    dy = _matmul(dx, a_wout, name="a_out_dy", mode="nt")
    do_a, dg_a = _gate_bwd(dy, o_a, gate_a, 0, name="a_gate_bwd")
    dq, dk, dv = _sb_bwd(p_a, do_a, tot_a, nb, seq, name="a_attn_bwd")
    parts = [dq, dk, dv, dg_a]
    g["a_w_in"] = jnp.stack([_matmul(ht_a, pt, name=f"a_in_dw{n}") for n, pt in enumerate(parts)])
    dh = None
    for n, pt in enumerate(parts):
        dh = _matmul(pt, w["a_w_in"][n], name=f"a_in_dh{n}", mode="nt", residual=dh)
    dx, g["a_norm"] = _rmsnorm_bwd(dh, x0, a_gain, dx, name="a_norm_bwd")

    return loss_row[0, 0], dx.reshape(nb, seq, d), g, early


_HBM = pl.BlockSpec(memory_space=pltpu.HBM)


def _place():
    return lax.axis_index("x"), lax.axis_index("y"), lax.axis_index("c")


def _other_chips(x, y):
    return [(1 - x, y), (x, 1 - y), (1 - x, 1 - y)]


def _allgather_chips(ss, *, name):
    n_ops = len(ss)

    def body(*refs):
        s_refs, o_refs, (send_sems, recv_sems) = refs[:n_ops], refs[n_ops:2 * n_ops], refs[2 * n_ops:]
        x, y, c = _place()
        me = 2 * x + y
        chips = _other_chips(x, y)

        def copy(i, kk, src, dst, to):
            return pltpu.make_async_remote_copy(src_ref=src, dst_ref=dst, send_sem=send_sems.at[6 * i + kk],
                                                recv_sem=recv_sems.at[6 * i + kk], device_id=to, device_id_type=MESH)

        def half(i, j, hc):
            h = s_refs[i].shape[0] // 2
            return o_refs[i].at[j, pl.ds(hc * h, h), :]

        first = [copy(i, kk, s_refs[i].at[pl.ds(c * (s_refs[i].shape[0] // 2), s_refs[i].shape[0] // 2), :], half(i, me, c),
                      (cx, cy, c)) for kk, (cx, cy) in enumerate(chips) for i in range(n_ops)]
        for cp in first:
            cp.start()
        passed = []
        for kk, (cx, cy) in enumerate(chips):
            for i in range(n_ops):
                blk = half(i, 2 * cx + cy, c)
                copy(i, kk, blk, blk, (cx, cy, c)).wait_recv()
                fwd = copy(i, 3 + kk, blk, blk, (x, y, 1 - c))
                fwd.start()
                passed.append(fwd)
        for kk, (cx, cy) in enumerate(chips):
            for i in range(n_ops):
                blk = half(i, 2 * cx + cy, 1 - c)
                copy(i, 3 + kk, blk, blk, (x, y, 1 - c)).wait_recv()
        for cp in first + passed:
            cp.wait_send()

    for s in ss:
        assert s.shape[0] % 32 == 0, s.shape
    return pl.pallas_call(
        body, name=name, in_specs=[_HBM] * n_ops, out_specs=[_HBM] * n_ops,
        out_shape=[jax.ShapeDtypeStruct((N_CHIPS,) + s.shape, s.dtype) for s in ss],
        scratch_shapes=[pltpu.SemaphoreType.DMA((6 * n_ops,)), pltpu.SemaphoreType.DMA((6 * n_ops,))],
    )(*ss)


_SEM = pl.BlockSpec(memory_space=pltpu.SEMAPHORE)
_ANY = pl.BlockSpec(memory_space=pl.ANY)
_DATAFLOW = pltpu.SideEffectType.DATAFLOW_SIDE_EFFECTING


def _chip_copies(s_refs, land_refs, send_sems, recv_sems):
    x, y, c = _place()
    me = 2 * x + y
    cps = []
    for i, (s_ref, land_ref) in enumerate(zip(s_refs, land_refs)):
        h = s_ref.shape[0] // 2
        for kk, (cx, cy) in enumerate(_other_chips(x, y)):
            cps.append(pltpu.make_async_remote_copy(
                src_ref=s_ref.at[pl.ds(c * h, h), :], dst_ref=land_ref.at[me, pl.ds(c * h, h), :], send_sem=send_sems.at[3 * i + kk],
                recv_sem=recv_sems.at[3 * i + kk], device_id=(cx, cy, c), device_id_type=MESH))
    return cps


def _gather_start(ss, after, *, name):
    n = len(ss)
    lands = [lax.empty((N_CHIPS,) + s.shape, s.dtype) for s in ss]

    def body(*refs):
        s_refs, land_refs = refs[:n], refs[n:2 * n]
        send_sems, recv_sems = refs[2 * n + 1], refs[2 * n + 2]
        token = refs[-1]
        for cp in _chip_copies(s_refs, land_refs, send_sems, recv_sems):
            cp.start()
        token[...] = jnp.zeros(token.shape, token.dtype)

    hbm = [pltpu.HBM(a.shape, a.dtype) for a in list(ss) + lands]
    res = pl.pallas_call(
        body, name=name,
        out_shape=(pltpu.SemaphoreType.DMA((3 * n,)), pltpu.SemaphoreType.DMA((3 * n,)), *hbm, jax.ShapeDtypeStruct((8, BLK), F32)),
        in_specs=[_HBM] * (2 * n) + [_ANY],
        out_specs=(_SEM, _SEM, *([_HBM] * (2 * n)), pl.BlockSpec(memory_space=pltpu.VMEM)),
        input_output_aliases={i: 2 + i for i in range(2 * n)},
        compiler_params=pltpu.CompilerParams(has_side_effects=_DATAFLOW),
    )(*[pltpu.with_memory_space_constraint(a, pltpu.HBM) for a in list(ss) + lands], after)
    return res[:-1], res[-1]


def _gather_wait(started, after, *, name):
    send_sems, recv_sems = started[0], started[1]
    n = (len(started) - 2) // 2

    def body(*refs):
        s_refs, land_refs = refs[:n], refs[n:2 * n]
        for cp in _chip_copies(s_refs, land_refs, refs[2 * n], refs[2 * n + 1]):
            cp.wait_send()
            cp.wait_recv()

    res = pl.pallas_call(
        body, name=name, out_shape=tuple(pltpu.HBM(a.shape, a.dtype) for a in started[2:]),
        in_specs=[_HBM] * (2 * n) + [_SEM, _SEM, _ANY], out_specs=tuple([_HBM] * (2 * n)),
        input_output_aliases={i: i for i in range(2 * n)},
        compiler_params=pltpu.CompilerParams(has_side_effects=_DATAFLOW),
    )(*started[2:], send_sems, recv_sems, after)
    return list(res[n:])


def _sibling_exchange(lands, *, name):
    n = len(lands)

    def body(*refs):
        o_refs, (send_sems, recv_sems) = refs[n:2 * n], refs[2 * n:]
        x, y, c = _place()
        cps = []
        for i, o_ref in enumerate(o_refs):
            h = o_ref.shape[1] // 2
            for kk, (cx, cy) in enumerate(_other_chips(x, y)):
                def half(hc):
                    return o_ref.at[2 * cx + cy, pl.ds(hc * h, h), :]
                sent = pltpu.make_async_remote_copy(src_ref=half(c), dst_ref=half(c), send_sem=send_sems.at[3 * i + kk],
                                                    recv_sem=recv_sems.at[3 * i + kk], device_id=(x, y, 1 - c), device_id_type=MESH)
                awaited = pltpu.make_async_remote_copy(src_ref=half(1 - c), dst_ref=half(1 - c), send_sem=send_sems.at[3 * i + kk],
                                                       recv_sem=recv_sems.at[3 * i + kk], device_id=(x, y, 1 - c),
                                                       device_id_type=MESH)
                cps.append((sent, awaited))
        for sent, _ in cps:
            sent.start()
        for sent, awaited in cps:
            awaited.wait_recv()
            sent.wait_send()

    return pl.pallas_call(
        body, name=name, in_specs=[_HBM] * n, out_specs=[_HBM] * n,
        out_shape=[jax.ShapeDtypeStruct(a.shape, a.dtype) for a in lands], scratch_shapes=_dma_sems(3 * n),
        input_output_aliases={i: i for i in range(n)},
    )(*lands)


def _own_block(gathered, s):
    me = 2 * lax.axis_index("x") + lax.axis_index("y")
    return lax.dynamic_update_slice(gathered, s[None], (me,) + (0,) * s.ndim)


def _dma_sems(n):
    return [pltpu.SemaphoreType.DMA((n,)), pltpu.SemaphoreType.DMA((n,))]


def _swap_halves(gps, *, name):
    n_ops = len(gps)

    def body(*refs):
        g_refs, o_refs, (send_sems, recv_sems) = refs[:n_ops], refs[n_ops:2 * n_ops], refs[2 * n_ops:]
        x, y, c = _place()
        cps = []
        for i, (g_ref, o_ref) in enumerate(zip(g_refs, o_refs)):
            h = g_ref.shape[1] // 2
            cps.append(pltpu.make_async_remote_copy(
                src_ref=g_ref.at[:, pl.ds((1 - c) * h, h), :], dst_ref=o_ref, send_sem=send_sems.at[i], recv_sem=recv_sems.at[i],
                device_id=(x, y, 1 - c), device_id_type=MESH))
        for cp in cps:
            cp.start()
        for cp in cps:
            cp.wait()

    return pl.pallas_call(
        body, name=name, in_specs=[_HBM] * n_ops, out_specs=[_HBM] * n_ops,
        out_shape=[jax.ShapeDtypeStruct((g.shape[0], g.shape[1] // 2, g.shape[2]), g.dtype) for g in gps],
        scratch_shapes=_dma_sems(n_ops),
    )(*gps)


def _scatter_chips(hps, *, name):
    n_ops = len(hps)

    def body(*refs):
        h_refs, o_refs, (send_sems, recv_sems) = refs[:n_ops], refs[n_ops:2 * n_ops], refs[2 * n_ops:]
        x, y, c = _place()
        cps = [pltpu.make_async_remote_copy(src_ref=h_ref.at[2 * cx + cy], dst_ref=o_ref.at[kk], send_sem=send_sems.at[3 * i + kk],
                                            recv_sem=recv_sems.at[3 * i + kk], device_id=(cx, cy, c), device_id_type=MESH)
               for i, (h_ref, o_ref) in enumerate(zip(h_refs, o_refs)) for kk, (cx, cy) in enumerate(_other_chips(x, y))]
        for cp in cps:
            cp.start()
        for cp in cps:
            cp.wait()

    return pl.pallas_call(
        body, name=name, in_specs=[_HBM] * n_ops, out_specs=[_HBM] * n_ops,
        out_shape=[jax.ShapeDtypeStruct((3,) + hp.shape[1:], hp.dtype) for hp in hps],
        scratch_shapes=_dma_sems(3 * n_ops),
    )(*hps)


def _join_halves(fs, *, name):
    n_ops = len(fs)

    def body(*refs):
        f_refs, o_refs, (send_sems, recv_sems) = refs[:n_ops], refs[n_ops:2 * n_ops], refs[2 * n_ops:]
        x, y, c = _place()
        cps = [pltpu.make_async_remote_copy(src_ref=f_ref, dst_ref=o_ref, send_sem=send_sems.at[i], recv_sem=recv_sems.at[i],
                                            device_id=(x, y, 1 - c), device_id_type=MESH)
               for i, (f_ref, o_ref) in enumerate(zip(f_refs, o_refs))]
        for cp in cps:
            cp.start()
        for cp in cps:
            cp.wait()

    theirs = pl.pallas_call(
        body, name=name, in_specs=[_HBM] * n_ops, out_specs=[_HBM] * n_ops,
        out_shape=[jax.ShapeDtypeStruct(f.shape, f.dtype) for f in fs], scratch_shapes=_dma_sems(n_ops),
    )(*fs)
    south = lax.axis_index("c") == 0
    return [jnp.concatenate([jnp.where(south, f, t), jnp.where(south, t, f)], axis=0) for f, t in zip(fs, theirs)]


def _add_halves(gp, ra, wire_dtype, *, name, bm=256):
    n, r, c_ = gp.shape
    h = r // 2
    bm = _tile(h, bm)
    per = h // bm
    c = lax.axis_index("c").astype(jnp.int32).reshape(1)

    def body(c_ref, g_ref, ra_ref, o_ref, ow_ref):
        s = g_ref[...] + ra_ref[...]
        o_ref[...] = s
        ow_ref[...] = s.astype(wire_dtype)

    mine = pl.BlockSpec((None, bm, c_), lambda j, i, cr: (j, i, 0))
    return pl.pallas_call(
        body, name=name,
        grid_spec=pltpu.PrefetchScalarGridSpec(
            num_scalar_prefetch=1, grid=(n, per),
            in_specs=[pl.BlockSpec((None, bm, c_), lambda j, i, cr: (j, cr[0] * per + i, 0)), mine],
            out_specs=[mine, mine]),
        out_shape=[jax.ShapeDtypeStruct((n, h, c_), F32), jax.ShapeDtypeStruct((n, h, c_), wire_dtype)],
        compiler_params=pltpu.CompilerParams(dimension_semantics=("parallel", "parallel")),
    )(c, gp, ra)


def _add_chips(hp, rb, *, name, bm=256):
    n, h, c_ = hp.shape
    bm = _tile(h, bm)
    me = (2 * lax.axis_index("x") + lax.axis_index("y")).astype(jnp.int32).reshape(1)

    def body(me_ref, h_ref, rb_ref, o_ref):
        o_ref[...] = ((h_ref[...] + rb_ref[0].astype(F32)) + rb_ref[1].astype(F32)) + rb_ref[2].astype(F32)

    return pl.pallas_call(
        body, name=name,
        grid_spec=pltpu.PrefetchScalarGridSpec(
            num_scalar_prefetch=1, grid=(h // bm,),
            in_specs=[pl.BlockSpec((None, bm, c_), lambda i, mr: (mr[0], i, 0)),
                      pl.BlockSpec((3, bm, c_), lambda i, mr: (0, i, 0))],
            out_specs=pl.BlockSpec((bm, c_), lambda i, mr: (i, 0))),
        out_shape=jax.ShapeDtypeStruct((h, c_), F32),
        compiler_params=pltpu.CompilerParams(dimension_semantics=("parallel",)),
    )(me, hp, rb)


def _scatter_copies(h_refs, land_refs, send_sems, recv_sems):
    x, y, c = _place()
    return [pltpu.make_async_remote_copy(src_ref=h_ref.at[2 * cx + cy], dst_ref=land_ref.at[kk], send_sem=send_sems.at[3 * i + kk],
                                         recv_sem=recv_sems.at[3 * i + kk], device_id=(cx, cy, c), device_id_type=MESH)
            for i, (h_ref, land_ref) in enumerate(zip(h_refs, land_refs)) for kk, (cx, cy) in enumerate(_other_chips(x, y))]


def _scatter_start(hps, after, *, name):
    n = len(hps)
    lands = [lax.empty((3,) + hp.shape[1:], hp.dtype) for hp in hps]

    def body(*refs):
        for cp in _scatter_copies(refs[:n], refs[n:2 * n], refs[2 * n + 1], refs[2 * n + 2]):
            cp.start()
        refs[-1][...] = jnp.zeros(refs[-1].shape, refs[-1].dtype)

    hbm = [pltpu.HBM(a.shape, a.dtype) for a in list(hps) + lands]
    res = pl.pallas_call(
        body, name=name,
        out_shape=(pltpu.SemaphoreType.DMA((3 * n,)), pltpu.SemaphoreType.DMA((3 * n,)), *hbm, jax.ShapeDtypeStruct((8, BLK), F32)),
        in_specs=[_HBM] * (2 * n) + [_ANY],
        out_specs=(_SEM, _SEM, *([_HBM] * (2 * n)), pl.BlockSpec(memory_space=pltpu.VMEM)),
        input_output_aliases={i: 2 + i for i in range(2 * n)},
        compiler_params=pltpu.CompilerParams(has_side_effects=_DATAFLOW),
    )(*[pltpu.with_memory_space_constraint(a, pltpu.HBM) for a in list(hps) + lands], after)
    return res[:-1], res[-1]


def _scatter_wait(started, after, *, name):
    n = (len(started) - 2) // 2

    def body(*refs):
        for cp in _scatter_copies(refs[:n], refs[n:2 * n], refs[2 * n], refs[2 * n + 1]):
            cp.wait_send()
            cp.wait_recv()

    res = pl.pallas_call(
        body, name=name, out_shape=tuple(pltpu.HBM(a.shape, a.dtype) for a in started[2:]),
        in_specs=[_HBM] * (2 * n) + [_SEM, _SEM, _ANY], out_specs=tuple([_HBM] * (2 * n)),
        input_output_aliases={i: i for i in range(2 * n)},
        compiler_params=pltpu.CompilerParams(has_side_effects=_DATAFLOW),
    )(*started[2:], started[0], started[1], after)
    return list(res[n:])


def _reduce_to_chips(gps, wire_dtypes, *, tag):
    ras = _swap_halves(gps, name=f"{tag}_swap_halves")
    return [_add_halves(gp, ra, wd, name=f"{tag}_add_halves{i}") for i, (gp, ra, wd) in enumerate(zip(gps, ras, wire_dtypes))]


def _start_reduce(early, tag):
    names = list(early)
    hps = _reduce_to_chips([early[n] for n in names], [BF16] * len(names), tag=tag)
    started, token = _scatter_start([hw for _, hw in hps], hps[-1][1], name=f"{tag}_scatter_start")
    return (tag, names, [hf for hf, _ in hps], started), token


def _adamw(w, g, m, v, *, name):
    r, c_ = w.shape
    bm = r
    for cand in (512, 256, 128, 64, 32, 16, 8):
        if r % cand == 0:
            bm = cand
            break
    c1 = 1.0 - ADAM_B1 ** ADAM_STEP
    c2 = 1.0 - ADAM_B2 ** ADAM_STEP

    def body(w_ref, g_ref, m_ref, v_ref, d_ref, nm_ref, nv_ref):
        g_ = g_ref[...]
        m_ = ADAM_B1 * m_ref[...] + (1.0 - ADAM_B1) * g_
        v_ = ADAM_B2 * v_ref[...] + (1.0 - ADAM_B2) * (g_ * g_)
        d_ref[...] = -ADAM_LR * ((m_ / c1) / (jnp.sqrt(v_ / c2) + ADAM_EPS) + ADAM_WD * w_ref[...])
        nm_ref[...] = m_
        nv_ref[...] = v_

    spec = pl.BlockSpec((bm, c_), lambda i: (i, 0))
    return pl.pallas_call(
        body, name=name, grid=(r // bm,), in_specs=[spec] * 4, out_specs=[spec] * 3,
        out_shape=[jax.ShapeDtypeStruct((r, c_), F32)] * 3,
        compiler_params=pltpu.CompilerParams(dimension_semantics=("parallel",)),
    )(w, g, m, v)


_WEIGHTS = ["a_norm", "a_w_in", "a_w_out", "b_norm", "b_w_in", "b_v_ln_g", "b_v_ln_b", "b_w_s", "b_b_s", "b_w_out",
            "c_norm", "c_w_in", "c_conv_w", "c_conv_b", "c_ln_g", "c_ln_b", "c_w_out", "d_norm", "d_w_in", "d_b_f",
            "d_w_out", "final_norm"]
_SHARD_AXIS = {"a_norm": None, "a_w_in": 2, "a_w_out": 1, "b_norm": 1, "b_w_in": 2, "b_v_ln_g": 1, "b_v_ln_b": 1, "b_w_s": None,
               "b_b_s": None, "b_w_out": 1, "c_norm": 1, "c_w_in": 2, "c_conv_w": 2, "c_conv_b": 1, "c_ln_g": 1, "c_ln_b": 1,
               "c_w_out": 1, "d_norm": 1, "d_w_in": 2, "d_b_f": None, "d_w_out": 1, "final_norm": None}
_BIG = ["a_w_in", "a_w_out", "b_w_in", "b_w_out", "c_w_in", "c_w_out", "d_w_in", "d_w_out"]
_GATHER_GROUPS = (("a_w_in", "a_w_out"), ("b_w_in", "b_w_out"), ("c_w_in", "c_w_out", "d_w_in", "d_w_out"))
_SMALL_SHARDED = [n for n in _WEIGHTS if _SHARD_AXIS[n] is not None and n not in _BIG]
_REPLICATED = [n for n in _WEIGHTS if _SHARD_AXIS[n] is None]
_ROW_ALIGN = 32
_ROW_ALIGN_SUMMED = 128


def _pack(pieces, dtype, align=_ROW_ALIGN):
    flat = jnp.concatenate([p.reshape(-1).astype(dtype) for p in pieces])
    unit = align * PACK_C
    total = -(-flat.shape[0] // unit) * unit
    return jnp.pad(flat, (0, total - flat.shape[0])).reshape(total // PACK_C, PACK_C)


def _unpack(flat, shapes):
    out, off = [], 0
    for s in shapes:
        n = math.prod(s)
        out.append(flat[off:off + n].reshape(s))
        off += n
    return out


def _full_shape(local_shape, axis):
    s = list(local_shape)
    if axis is not None:
        s[axis] *= N_CHIPS
    return tuple(s)


def _gather_weights(local):
    def whole(n, gt):
        if _SHARD_AXIS[n] == 1:
            return gt.reshape(-1, gt.shape[-1])
        if n == "d_w_in":
            return gt.transpose(1, 0, 2).reshape(gt.shape[1], -1)
        return gt

    full = {n: local[n][0] if n != "final_norm" else local[n] for n in _REPLICATED}
    first = list(_GATHER_GROUPS[0])
    mine = [local[n][0].astype(BF16) for n in first] + [_pack([local[n] for n in _SMALL_SHARDED], F32)]
    got = [_own_block(gt, s) for gt, s in zip(_allgather_chips(mine, name="gather_weights"), mine)]
    full.update({n: whole(n, gt) for n, gt in zip(first, got)})
    small = got[-1].reshape(N_CHIPS, -1)
    shards = [_unpack(small[j], [local[n].shape[1:] for n in _SMALL_SHARDED]) for j in range(N_CHIPS)]
    for i, n in enumerate(_SMALL_SHARDED):
        full[n] = jnp.concatenate([shards[j][i] for j in range(N_CHIPS)], axis=_SHARD_AXIS[n] - 1)

    def begin(k, after):
        shards_k = [local[n][0].astype(BF16) for n in _GATHER_GROUPS[k]]
        started, token = _gather_start(shards_k, after, name=f"gather{k}_start")
        return shards_k, started, token

    pending = [begin(1, got[0])]
    full["a_norm"] = full["a_norm"] + pending[0][2][0, 0]

    def finish(k):
        def weights(after):
            shards_k, started, _ = pending[k - 1]
            lands = _gather_wait(started, after, name=f"gather{k}_wait")
            token = None
            if k + 1 < len(_GATHER_GROUPS):
                pending.append(begin(k + 1, lands[0]))
                token = pending[k][2]
            lands = _sibling_exchange(lands, name=f"gather{k}_exchange")
            out = {n: whole(n, _own_block(gt, s)) for n, gt, s in zip(_GATHER_GROUPS[k], lands, shards_k)}
            if token is not None:
                gain = _GATHER_GROUPS[k][0][0] + "_norm"
                out[gain] = full[gain] + token[0, 0]
            return out
        return weights

    return full, [finish(k) for k in range(1, len(_GATHER_GROUPS))]


def _repl_piece_len(local):
    total = sum(math.prod(local[n].shape) for n in _REPLICATED)
    return -(-total // N_CHIPS)


def _reduce_grads(g, local, early):
    rep_flat = jnp.concatenate([g[n].reshape(-1) for n in _REPLICATED])
    piece = _repl_piece_len(local)
    rep_flat = jnp.pad(rep_flat, (0, N_CHIPS * piece - rep_flat.shape[0]))

    def shard(n, j):
        full = g[n].reshape(_full_shape(local[n].shape, _SHARD_AXIS[n]))
        width = local[n].shape[_SHARD_AXIS[n]]
        return lax.slice_in_dim(full, j * width, (j + 1) * width, axis=_SHARD_AXIS[n])

    small = jnp.stack([_pack([shard(n, j) for n in _SMALL_SHARDED] + [rep_flat[j * piece:(j + 1) * piece]], F32)
                       for j in range(N_CHIPS)])
    early_names = [n for _, names, _, _ in early for n in names]
    late = [n for n in _BIG if n not in early_names]
    hps = _reduce_to_chips([g[n] for n in late] + [small], [BF16] * len(late) + [F32], tag="grads")
    rbs = list(_scatter_chips([hw for _, hw in hps], name="grads_scatter_chips"))
    early_halves, early_rbs = [], []
    for tag, _, halves_k, started in early:
        early_halves += halves_k
        early_rbs += _scatter_wait(started, rbs[0], name=f"{tag}_scatter_wait")
    halves = early_halves + [hf for hf, _ in hps]
    fs = [_add_chips(hf, rb, name=f"grads_add_chips{i}") for i, (hf, rb) in enumerate(zip(halves, early_rbs + rbs))]
    summed = _join_halves(fs, name="grads_join_halves")
    red = {n: s.reshape(local[n].shape) for n, s in zip(early_names + late, summed)}
    out = _unpack(summed[-1].reshape(-1), [local[n].shape for n in _SMALL_SHARDED] + [(piece,)])
    red.update(zip(_SMALL_SHARDED, out[:-1]))
    rep_mine = _pack([out[-1]], F32)
    rep = _own_block(_allgather_chips([rep_mine], name="gather_replicated_grads")[0], rep_mine)
    rep = rep.reshape(N_CHIPS, -1)[:, :piece].reshape(-1)
    for n, val in zip(_REPLICATED, _unpack(rep, [local[n].shape for n in _REPLICATED])):
        red[n] = val
    return red


def _update(local, grads, m, v):
    delta, new_m, new_v = {}, {}, {}
    for n in _BIG:
        shp = local[n].shape
        two = (shp[-2], shp[-1])
        res = _adamw(local[n].reshape(two), grads[n].reshape(two), m[n].reshape(two), v[n].reshape(two), name=f"adamw_{n}")
        delta[n], new_m[n], new_v[n] = [r.reshape(shp) for r in res]
    small = [n for n in _WEIGHTS if n not in _BIG]
    shapes = [local[n].shape for n in small]
    packed = [_pack([src[n] for n in small], F32) for src in (local, grads, m, v)]
    res = _adamw(*packed, name="adamw_small")
    for dst, r in zip((delta, new_m, new_v), res):
        for n, val in zip(small, _unpack(r.reshape(-1), shapes)):
            dst[n] = val
    return delta, new_m, new_v


def kernel(x, a_norm, a_w_in, a_w_out, b_norm, b_w_in, b_v_ln_g, b_v_ln_b, b_w_s, b_b_s, b_w_out, c_norm, c_w_in, c_conv_w, c_conv_b, c_ln_g, c_ln_b, c_w_out, d_norm, d_w_in, d_b_f, d_w_out, final_norm, loss_target, m_a_norm, m_a_w_in, m_a_w_out, m_b_norm, m_b_w_in, m_b_v_ln_g, m_b_v_ln_b, m_b_w_s, m_b_b_s, m_b_w_out, m_c_norm, m_c_w_in, m_c_conv_w, m_c_conv_b, m_c_ln_g, m_c_ln_b, m_c_w_out, m_d_norm, m_d_w_in, m_d_b_f, m_d_w_out, m_final_norm, v_a_norm, v_a_w_in, v_a_w_out, v_b_norm, v_b_w_in, v_b_v_ln_g, v_b_v_ln_b, v_b_w_s, v_b_b_s, v_b_w_out, v_c_norm, v_c_w_in, v_c_conv_w, v_c_conv_b, v_c_ln_g, v_c_ln_b, v_c_w_out, v_d_norm, v_d_w_in, v_d_b_f, v_d_w_out, v_final_norm):
    local = dict(zip(_WEIGHTS, (a_norm, a_w_in, a_w_out, b_norm, b_w_in, b_v_ln_g, b_v_ln_b, b_w_s, b_b_s, b_w_out, c_norm, c_w_in,
                                c_conv_w, c_conv_b, c_ln_g, c_ln_b, c_w_out, d_norm, d_w_in, d_b_f, d_w_out, final_norm)))
    m = dict(zip(_WEIGHTS, (m_a_norm, m_a_w_in, m_a_w_out, m_b_norm, m_b_w_in, m_b_v_ln_g, m_b_v_ln_b, m_b_w_s, m_b_b_s, m_b_w_out,
                            m_c_norm, m_c_w_in, m_c_conv_w, m_c_conv_b, m_c_ln_g, m_c_ln_b, m_c_w_out, m_d_norm, m_d_w_in, m_d_b_f,
                            m_d_w_out, m_final_norm)))
    v = dict(zip(_WEIGHTS, (v_a_norm, v_a_w_in, v_a_w_out, v_b_norm, v_b_w_in, v_b_v_ln_g, v_b_v_ln_b, v_b_w_s, v_b_b_s, v_b_w_out,
                            v_c_norm, v_c_w_in, v_c_conv_w, v_c_conv_b, v_c_ln_g, v_c_ln_b, v_c_w_out, v_d_norm, v_d_w_in, v_d_b_f,
                            v_d_w_out, v_final_norm)))
    loss_part, grad_x, g, early = _local_step(x, loss_target, *_gather_weights(local), _start_reduce)
    loss = lax.psum(loss_part, ("x", "y", "c"))
    grads = _reduce_grads(g, local, early)
    delta, new_m, new_v = _update(local, grads, m, v)
    return (loss, grad_x, *[grads[n] for n in _WEIGHTS], *[delta[n] for n in _WEIGHTS],
            *[new_m[n] for n in _WEIGHTS], *[new_v[n] for n in _WEIGHTS])
```

```python
import functools
import math

import jax
import jax.numpy as jnp
from jax import lax
from jax.experimental import pallas as pl
from jax.experimental.pallas import tpu as pltpu

F32, BF16 = jnp.float32, jnp.bfloat16
MESH = pl.DeviceIdType.MESH

D_MODEL = 1024
HEADS = 16
HEAD_DIM = 64
BLK = 128
PAIRS = HEADS // 2
GM_W = 2048
GM_G = 16
CV_W = 2048
CV_K = 31
HALO = 32
EPS = 1e-6
N_CHIPS = 4
PACK_C = 1024
ADAM_LR, ADAM_B1, ADAM_B2, ADAM_EPS, ADAM_WD, ADAM_STEP = 0.001, 0.9, 0.999, 1e-08, 0.01, 10

_NT = (((1,), (1,)), ((), ()))
_TN = (((0,), (0,)), ((), ()))
_NN = (((1,), (0,)), ((), ()))


def _dot(a, b, dims=_NN):
    return lax.dot_general(a, b, dims, preferred_element_type=F32)


def _split3(x):
    hi = x.astype(BF16)
    r = x - hi.astype(F32)
    mid = r.astype(BF16)
    lo = (r - mid.astype(F32)).astype(BF16)
    return hi, mid, lo


def _dot3_right(x, m):
    hi, mid, lo = _split3(x)
    return _dot(hi, m) + _dot(mid, m) + _dot(lo, m)


def _dot3_left(m, x):
    hi, mid, lo = _split3(x)
    return _dot(m, hi) + _dot(m, mid) + _dot(m, lo)


def _sigmoid(x):
    return 1.0 / (1.0 + jnp.exp(-x))


def _silu(x):
    return x * _sigmoid(x)


def _dsilu(x):
    s = _sigmoid(x)
    return s * (1.0 + x * (1.0 - s))


_GELU_C = math.sqrt(2.0 / math.pi)
_GELU_A = 0.044715


def _gelu(x):
    return 0.5 * x * (1.0 + jnp.tanh(_GELU_C * (x + _GELU_A * x * x * x)))


def _dgelu(x):
    t = jnp.tanh(_GELU_C * (x + _GELU_A * x * x * x))
    return 0.5 * (1.0 + t) + 0.5 * x * (1.0 - t * t) * _GELU_C * (1.0 + 3.0 * _GELU_A * x * x)


def _log_sigmoid(x):
    return jnp.minimum(x, 0.0) - jnp.log(1.0 + jnp.exp(-jnp.abs(x)))


def _rms_fwd(x, g):
    r = lax.rsqrt(jnp.mean(x * x, axis=-1, keepdims=True) + EPS)
    return x * r * g


def _rms_bwd(dy, x, g):
    r = lax.rsqrt(jnp.mean(x * x, axis=-1, keepdims=True) + EPS)
    xh = x * r
    dxh = dy * g
    dx = r * (dxh - xh * jnp.mean(dxh * xh, axis=-1, keepdims=True))
    return dx, dy * xh


def _ln_stats(x):
    mu = jnp.mean(x, axis=-1, keepdims=True)
    xc = x - mu
    r = lax.rsqrt(jnp.mean(xc * xc, axis=-1, keepdims=True) + EPS)
    return xc * r, r


def _ln_bwd(dy, xh, r, g):
    dxh = dy * g
    return r * (dxh - jnp.mean(dxh, axis=-1, keepdims=True) - xh * jnp.mean(dxh * xh, axis=-1, keepdims=True))


def _colsum(x):
    return jnp.sum(x, axis=0, keepdims=True)


def _tile(n, want):
    for t in range(min(n, want), 7, -1):
        if n % t == 0 and t % 8 == 0:
            return t
    return n


MM_TILE = 1024


def _matmul(a, b, *, name, mode="nn", residual=None, out_shards=1, out_dtype=F32):
    (m, k) = a.shape
    b_shards = b.shape[0] if b.ndim == 3 else 1
    if mode == "nn":
        n = b.shape[-1] * b_shards
        tn, tk = _tile(n // max(b_shards, out_shards), MM_TILE), _tile(k, MM_TILE)
    else:
        n = b.shape[-2]
        tn, tk = _tile(n // out_shards, MM_TILE), _tile(k // b_shards, MM_TILE)
    tm = _tile(m, MM_TILE)
    nk = k // tk
    a_spec = pl.BlockSpec((tm, tk), lambda i, j, kk: (i, kk))
    if mode == "nn":
        dims = _NN
        if b_shards == 1:
            b_spec = pl.BlockSpec((tk, tn), lambda i, j, kk: (kk, j))
        else:
            per_b = n // b_shards // tn
            b_spec = pl.BlockSpec((None, tk, tn), lambda i, j, kk: (j // per_b, kk, j % per_b))
    else:
        dims = _NT
        if b_shards == 1:
            b_spec = pl.BlockSpec((tn, tk), lambda i, j, kk: (j, kk))
        else:
            per_b = k // b_shards // tk
            b_spec = pl.BlockSpec((None, tn, tk), lambda i, j, kk: (kk // per_b, j, kk % per_b))
    if out_shards == 1:
        o_spec = pl.BlockSpec((tm, tn), lambda i, j, kk: (i, j))
        o_shape = (m, n)
    else:
        per_o = n // out_shards // tn
        o_spec = pl.BlockSpec((None, tm, tn), lambda i, j, kk: (j // per_o, i, j % per_o))
        o_shape = (out_shards, m, n // out_shards)
    has_res = residual is not None

    def body(a_ref, b_ref, *rest):
        o_ref = rest[-1]
        kk = pl.program_id(2)
        part = _dot(a_ref[...].astype(BF16), b_ref[...].astype(BF16), dims)
        if has_res:
            @pl.when(kk == 0)
            def _():
                o_ref[...] = part + rest[0][...]
        else:
            @pl.when(kk == 0)
            def _():
                o_ref[...] = part.astype(out_dtype)

        if nk > 1:
            @pl.when(kk > 0)
            def _():
                o_ref[...] += part

    assert out_dtype == F32 or nk == 1
    return pl.pallas_call(
        body, name=name, grid=(m // tm, n // tn, nk),
        in_specs=[a_spec, b_spec] + ([o_spec] if has_res else []),
        out_specs=o_spec, out_shape=jax.ShapeDtypeStruct(o_shape, out_dtype),
        compiler_params=pltpu.CompilerParams(dimension_semantics=("parallel", "parallel", "arbitrary")),
    )(a, b, *([residual] if has_res else []))


def _rows(fn, *, name, steps, ins, outs, accs=(), scratch=()):
    ni, no, na = len(ins), len(outs), len(accs)

    def body(*refs):
        in_refs, out_refs = refs[:ni], refs[ni:ni + no]
        acc_refs, scr = refs[ni + no:ni + no + na], refs[ni + no + na:]
        i = pl.program_id(0)

        @pl.when(i == 0)
        def _():
            for r in acc_refs:
                r[...] = jnp.zeros(r.shape, r.dtype)

        fn(i, in_refs, out_refs, acc_refs, scr)

    def full(shape):
        nd = len(shape)
        return pl.BlockSpec(tuple(shape), lambda i: (0,) * nd)

    res = pl.pallas_call(
        body, name=name, grid=(steps,),
        in_specs=[pl.BlockSpec(bs, im) for _, bs, im in ins],
        out_specs=[pl.BlockSpec(bs, im) for _, _, bs, im in outs] + [full(s) for s, _ in accs],
        out_shape=[jax.ShapeDtypeStruct(s, d) for s, d, _, _ in outs] + [jax.ShapeDtypeStruct(s, d) for s, d in accs],
        scratch_shapes=list(scratch),
        compiler_params=pltpu.CompilerParams(dimension_semantics=("arbitrary",)),
    )(*[a for a, _, _ in ins])
    return res


def _rb(arr, bm, cb=0, width=None):
    w = arr.shape[1] if width is None else width
    return (arr, (bm, w), lambda i: (i, cb))


def _const(arr):
    nd = arr.ndim
    return (arr, tuple(arr.shape), lambda i: (0,) * nd)


def _ro(t, w, dtype, bm):
    return ((t, w), dtype, (bm, w), lambda i: (i, 0))


def _rot(t, w, dtype, bm):
    return ((w, t), dtype, (w, bm), lambda i: (0, i))


def _rmsnorm(x, g, *, name, bm=512):
    t, d = x.shape
    bm = _tile(t, bm)

    def fn(i, ins, outs, accs, scr):
        h = _rms_fwd(ins[0][...], ins[1][...])
        outs[0][...] = h.astype(BF16)
        outs[1][...] = h.T.astype(BF16)

    return _rows(fn, name=name, steps=t // bm, ins=[_rb(x, bm), _const(g)], outs=[_ro(t, d, BF16, bm), _rot(t, d, BF16, bm)])


def _rmsnorm_bwd(dh, x, g, dres, *, name, bm=512):
    t, d = x.shape
    bm = _tile(t, bm)

    def fn(i, ins, outs, accs, scr):
        dx, dgrow = _rms_bwd(ins[0][...], ins[1][...], ins[2][...])
        outs[0][...] = ins[3][...] + dx
        accs[0][...] += _colsum(dgrow)

    return _rows(fn, name=name, steps=t // bm, ins=[_rb(dh, bm), _rb(x, bm), _const(g), _rb(dres, bm)],
                 outs=[_ro(t, d, F32, bm)], accs=[((1, d), F32)])


def _gate(o, p, gcb, *, name, bm=512):
    t, w = o.shape
    bm = _tile(t, bm)

    def fn(i, ins, outs, accs, scr):
        y = ins[0][...] * _silu(ins[1][...])
        outs[0][...] = y.astype(BF16)
        outs[1][...] = y.T.astype(BF16)

    return _rows(fn, name=name, steps=t // bm, ins=[_rb(o, bm), _rb(p, bm, gcb, w)],
                 outs=[_ro(t, w, BF16, bm), _rot(t, w, BF16, bm)])


def _gate_bwd(dy, o, p, gcb, *, name, bm=512):
    t, w = o.shape
    bm = _tile(t, bm)

    def fn(i, ins, outs, accs, scr):
        dy_, o_, g_ = ins[0][...], ins[1][...], ins[2][...]
        outs[0][...] = dy_ * _silu(g_)
        outs[1][...] = dy_ * o_ * _dsilu(g_)

    return _rows(fn, name=name, steps=t // bm, ins=[_rb(dy, bm), _rb(o, bm), _rb(p, bm, gcb, w)],
                 outs=[_ro(t, w, F32, bm), _ro(t, w, F32, bm)])


def _loss_head(x, g, tgt, *, name, bm=512):
    t, d = x.shape
    bm = _tile(t, bm)

    def fn(i, ins, outs, accs, scr):
        x_, g_, tg = ins[0][...], ins[1][...], ins[2][...]
        err = _rms_fwd(x_, g_) - tg
        part = 0.5 * jnp.sum(jnp.sum(err * err, axis=-1, keepdims=True), axis=0, keepdims=True) / d
        dx, dgrow = _rms_bwd(err / d, x_, g_)
        outs[0][...] = dx
        accs[0][...] += _colsum(dgrow)
        accs[1][...] += jnp.broadcast_to(part, (1, BLK))

    return _rows(fn, name=name, steps=t // bm, ins=[_rb(x, bm), _const(g), _rb(tgt, bm)],
                 outs=[_ro(t, d, F32, bm)], accs=[((1, d), F32), ((1, BLK), F32)])


def _gmlp_mix_weights(ws_ref, g):
    row = lax.broadcasted_iota(jnp.int32, (BLK, BLK), 0)
    col = lax.broadcasted_iota(jnp.int32, (BLK, BLK), 1)
    tril = col <= row
    return jnp.where(tril, ws_ref[g], 0.0), tril


def _gmlp_fwd(p, ln_g, ln_b, w_s, bs_t, *, name):
    t = p.shape[0]

    def fn(i, ins, outs, accs, scr):
        p_ref, lg, lb, ws_ref, bst = ins
        vn = _ln_stats(_gelu(p_ref[:, GM_W:2 * GM_W]))[0] * lg[...] + lb[...]
        for g in range(GM_G):
            cs = slice(g * BLK, (g + 1) * BLK)
            wt, _ = _gmlp_mix_weights(ws_ref, g)
            s = _dot(wt.astype(BF16), vn[:, cs].astype(BF16)) + bst[:, g:g + 1]
            u = _gelu(p_ref[:, cs])
            gate = p_ref[:, 2 * GM_W + g * BLK:2 * GM_W + (g + 1) * BLK]
            y = u * s * _silu(gate)
            outs[0][:, cs] = y.astype(BF16)
            outs[1][cs, :] = y.T.astype(BF16)

    return _rows(fn, name=name, steps=t // BLK, ins=[_rb(p, BLK), _const(ln_g), _const(ln_b), _const(w_s), _const(bs_t)],
                 outs=[_ro(t, GM_W, BF16, BLK), _rot(t, GM_W, BF16, BLK)])


def _gmlp_bwd(dy, p, ln_g, ln_b, w_s, bs_t, *, name):
    t = p.shape[0]

    def fn(i, ins, outs, accs, scr):
        dy_ref, p_ref, lg, lb, ws_ref, bst = ins
        dp_ref = outs[0]
        dlg, dlb, dws, dbst = accs
        dvn_ref = scr[0]
        v_pre = p_ref[:, GM_W:2 * GM_W]
        xh, r = _ln_stats(_gelu(v_pre))
        vn = xh * lg[...] + lb[...]
        for g in range(GM_G):
            cs = slice(g * BLK, (g + 1) * BLK)
            gs = slice(2 * GM_W + g * BLK, 2 * GM_W + (g + 1) * BLK)
            wt, tril = _gmlp_mix_weights(ws_ref, g)
            vg = vn[:, cs].astype(BF16)
            s = _dot(wt.astype(BF16), vg) + bst[:, g:g + 1]
            u_pre, gate, dyg = p_ref[:, cs], p_ref[:, gs], dy_ref[:, cs]
            u = _gelu(u_pre)
            dos = dyg * _silu(gate)
            dp_ref[:, gs] = dyg * u * s * _dsilu(gate)
            dp_ref[:, cs] = dos * s * _dgelu(u_pre)
            ds = (dos * u).astype(BF16)
            dws[g] += jnp.where(tril, _dot(ds, vg, _NT), 0.0)
            dbst[:, g:g + 1] += jnp.sum(dos * u, axis=1, keepdims=True)
            dvn_ref[:, cs] = _dot(wt.astype(BF16), ds, _TN)
        dvn = dvn_ref[...]
        dlg[...] += _colsum(dvn * xh)
        dlb[...] += _colsum(dvn)
        dp_ref[:, GM_W:2 * GM_W] = _ln_bwd(dvn, xh, r, lg[...]) * _dgelu(v_pre)

    return _rows(fn, name=name, steps=t // BLK,
                 ins=[_rb(dy, BLK), _rb(p, BLK), _const(ln_g), _const(ln_b), _const(w_s), _const(bs_t)],
                 outs=[_ro(t, 3 * GM_W, F32, BLK)],
                 accs=[((1, GM_W), F32), ((1, GM_W), F32), ((GM_G, BLK, BLK), F32), ((BLK, GM_G), F32)],
                 scratch=[pltpu.VMEM((BLK, GM_W), F32)])


CV_BM = 128
CV_RC = 8
SUBLANES = 8
CV_FWD_OFFS = [HALO - (CV_K - 1) + k for k in range(CV_K)]
CV_BWD_OFFS = [CV_K - 1 - k for k in range(CV_K)]


def _conv_halo_prev(p, cb, bm):
    per = bm // HALO
    return (p, (HALO, CV_W), lambda i: (jnp.maximum(i * per - 1, 0), cb))


def _conv_scratch(bm):
    return [pltpu.VMEM((bm + HALO, CV_W), F32), pltpu.VMEM((SUBLANES - 1, bm + HALO - SUBLANES, CV_W), F32),
            pltpu.VMEM((bm, CV_W), F32)]


def _conv_shift_copies(ext_ref, sh_ref):
    rows = sh_ref.shape[1]
    for b in range(1, SUBLANES):
        sh_ref[b - 1] = ext_ref[pl.ds(b, rows), :]


def _conv_window(ext_ref, sh_ref, off, r0, rows):
    b = off % SUBLANES
    src = ext_ref if b == 0 else sh_ref.at[b - 1]
    return src[pl.ds(r0 + (off - b), rows), :]


def _conv_taps(ext_ref, sh_ref, cw_ref, y_ref, offs):
    bm = y_ref.shape[0]

    def chunk(ci, c):
        r0 = pl.multiple_of(ci * CV_RC, CV_RC)
        acc = jnp.zeros((CV_RC, CV_W), F32)
        for k in range(CV_K):
            acc = acc + cw_ref[pl.ds(k * SUBLANES, CV_RC), :] * _conv_window(ext_ref, sh_ref, offs[k], r0, CV_RC)
        y_ref[pl.ds(r0, CV_RC), :] = acc
        return c

    lax.fori_loop(0, bm // CV_RC, chunk, 0)


def _conv_dweights(dy1_ref, ext_ref, sh_ref, dcw_ref):
    bm = dy1_ref.shape[0]
    groups = 4
    for k in range(CV_K):
        def step(ci, acc, off=CV_FWD_OFFS[k]):
            prods = []
            for u in range(groups):
                r0 = pl.multiple_of((ci * groups + u) * CV_RC, CV_RC)
                prods.append(dy1_ref[pl.ds(r0, CV_RC), :] * _conv_window(ext_ref, sh_ref, off, r0, CV_RC))
            return acc + ((prods[0] + prods[1]) + (prods[2] + prods[3]))

        dcw_ref[k:k + 1, :] += _colsum(lax.fori_loop(0, bm // (CV_RC * groups), step, jnp.zeros((CV_RC, CV_W), F32)))


def _conv_fill(i, ext_ref, a_prev, b_prev, a, b, bm, seq):
    keep = jnp.where((i % (seq // bm)) == 0, 0.0, 1.0)
    ext_ref[pl.ds(0, HALO), :] = keep * (a_prev * _sigmoid(b_prev))
    ext_ref[pl.ds(HALO, bm), :] = a * _sigmoid(b)


def _conv_fwd(p, cw, cb, ln_g, ln_b, seq, *, name, bm=CV_BM):
    t = p.shape[0]

    def fn(i, ins, outs, accs, scr):
        a, b, gate, ap, bp = [r[...] for r in ins[:5]]
        cw_ref, cb_, lg, lb = ins[5], ins[6][...], ins[7][...], ins[8][...]
        ext, sh, y = scr
        _conv_fill(i, ext, ap, bp, a, b, bm, seq)
        _conv_shift_copies(ext, sh)
        _conv_taps(ext, sh, cw_ref, y, CV_FWD_OFFS)
        y2 = _ln_stats(y[...] + cb_)[0] * lg + lb
        out = _silu(y2) * _silu(gate)
        outs[0][...] = out.astype(BF16)
        outs[1][...] = out.T.astype(BF16)

    return _rows(fn, name=name, steps=t // bm,
                 ins=[_rb(p, bm, 0, CV_W), _rb(p, bm, 1, CV_W), _rb(p, bm, 2, CV_W),
                      _conv_halo_prev(p, 0, bm), _conv_halo_prev(p, 1, bm),
                      _const(cw), _const(cb), _const(ln_g), _const(ln_b)],
                 outs=[_ro(t, CV_W, BF16, bm), _rot(t, CV_W, BF16, bm)], scratch=_conv_scratch(bm))


def _conv_bwd_post(dy, p, cw, cb, ln_g, ln_b, seq, *, name, bm=CV_BM):
    t = p.shape[0]

    def fn(i, ins, outs, accs, scr):
        dy_, a, b, gate, ap, bp = [r[...] for r in ins[:6]]
        cw_ref, cb_, lg, lb = ins[6], ins[7][...], ins[8][...], ins[9][...]
        dlg, dlb, dcb, dcw = accs
        ext, sh, y = scr
        _conv_fill(i, ext, ap, bp, a, b, bm, seq)
        _conv_shift_copies(ext, sh)
        _conv_taps(ext, sh, cw_ref, y, CV_FWD_OFFS)
        xh, r = _ln_stats(y[...] + cb_)
        y2 = xh * lg + lb
        outs[1][...] = dy_ * _silu(y2) * _dsilu(gate)
        dy2 = dy_ * _silu(gate) * _dsilu(y2)
        dlg[...] += _colsum(dy2 * xh)
        dlb[...] += _colsum(dy2)
        dy1 = _ln_bwd(dy2, xh, r, lg)
        outs[0][...] = dy1
        dcb[...] += _colsum(dy1)
        _conv_dweights(outs[0], ext, sh, dcw)

    return _rows(fn, name=name, steps=t // bm,
                 ins=[_rb(dy, bm), _rb(p, bm, 0, CV_W), _rb(p, bm, 1, CV_W), _rb(p, bm, 2, CV_W),
                      _conv_halo_prev(p, 0, bm), _conv_halo_prev(p, 1, bm),
                      _const(cw), _const(cb), _const(ln_g), _const(ln_b)],
                 outs=[_ro(t, CV_W, F32, bm), _ro(t, CV_W, F32, bm)],
                 accs=[((1, CV_W), F32), ((1, CV_W), F32), ((1, CV_W), F32), ((CV_K, CV_W), F32)],
                 scratch=_conv_scratch(bm))


def _conv_bwd_pre(dy1, dgate, p, cw, seq, *, name, bm=CV_BM):
    t = p.shape[0]
    per = bm // HALO
    last_halo = t // HALO - 1

    def fn(i, ins, outs, accs, scr):
        d1, d1n, dg, a, b = [r[...] for r in ins[:5]]
        ext, sh, y = scr
        keep = jnp.where((i % (seq // bm)) == (seq // bm - 1), 0.0, 1.0)
        ext[pl.ds(0, bm), :] = d1
        ext[pl.ds(bm, HALO), :] = keep * d1n
        _conv_shift_copies(ext, sh)
        _conv_taps(ext, sh, ins[5], y, CV_BWD_OFFS)
        dy0 = y[...]
        sb = _sigmoid(b)
        outs[0][:, 0:CV_W] = dy0 * sb
        outs[0][:, CV_W:2 * CV_W] = dy0 * a * sb * (1.0 - sb)
        outs[0][:, 2 * CV_W:3 * CV_W] = dg

    return _rows(fn, name=name, steps=t // bm,
                 ins=[_rb(dy1, bm), (dy1, (HALO, CV_W), lambda i: (jnp.minimum((i + 1) * per, last_halo), 0)),
                      _rb(dgate, bm), _rb(p, bm, 0, CV_W), _rb(p, bm, 1, CV_W), _const(cw)],
                 outs=[_ro(t, 3 * CV_W, F32, bm)], scratch=_conv_scratch(bm))[0]


def _iotas():
    row = lax.broadcasted_iota(jnp.int32, (BLK, BLK), 0)
    col = lax.broadcasted_iota(jnp.int32, (BLK, BLK), 1)
    return row, col


def _heads(x, head0):
    if head0.shape != x.shape:
        head0 = lax.broadcasted_iota(jnp.int32, x.shape, 1) < HEAD_DIM
    return jnp.where(head0, x, 0.0).astype(BF16), jnp.where(head0, 0.0, x).astype(BF16)


def _pair_spec(seq, off):
    return pl.BlockSpec((seq, BLK), lambda b, hp: (b, off + hp))


def _stat_spec(seq):
    return pl.BlockSpec((None, None, seq, BLK), lambda b, hp: (b, hp, 0, 0))


_ATT_PARAMS = dict(compiler_params=pltpu.CompilerParams(dimension_semantics=("parallel", "parallel")))
_SCALE = 1.0 / math.sqrt(HEAD_DIM)


Q_BLOCK = 256
KEY_BLOCK = 256


def _stack_heads(x, head0, scale=None):
    if scale is not None:
        x = x * scale
    return jnp.concatenate(_heads(x, head0), axis=0)


def _pair_cols(x, head0, fill):
    a = jnp.max(jnp.where(head0, x, fill), axis=1, keepdims=True)
    b = jnp.max(jnp.where(head0, fill, x), axis=1, keepdims=True)
    return jnp.concatenate([a, b], axis=0)


def _causal_mask(t0, s0, tq, kw, inclusive):
    row = lax.broadcasted_iota(jnp.int32, (2 * tq, kw), 0) & (tq - 1)
    col = lax.broadcasted_iota(jnp.int32, (2 * tq, kw), 1)
    return (s0 + col) <= (t0 + row) if inclusive else (s0 + col) < (t0 + row)


def _sub(x, j):
    return x[:, j * BLK:(j + 1) * BLK]


def _tri_blocks(kw, relation):
    r = lax.broadcasted_iota(jnp.int32, (kw, kw), 0)
    c = lax.broadcasted_iota(jnp.int32, (kw, kw), 1)
    return (((r // BLK) == (c // BLK)) & relation(r, c)).astype(BF16)


def _block_cumsum(x, tri, ksub):
    hi = x.astype(BF16)
    lo = (x - hi.astype(F32)).astype(BF16)
    cs = _dot(jnp.concatenate([hi, lo], axis=0), tri)
    n = x.shape[0]
    cs = cs[:n] + cs[n:]
    return [_sub(cs, j) for j in range(ksub)], [jnp.sum(_sub(x, j), axis=1, keepdims=True) for j in range(ksub)]


def _sb_terms(qs, k, mask):
    return _sb_terms_z(_dot(qs, k, _NT), mask)


def _sb_terms_z(z, mask):
    t = jnp.log(1.0 + jnp.exp(-jnp.abs(z)))
    lsz = jnp.minimum(z, 0.0) - t
    lr = lsz - z
    if mask is not None:
        lr = jnp.where(mask, lr, 0.0)
    return lsz, lr


def _sb_fwd(p, nb, seq, *, name):
    tq = min(Q_BLOCK, seq)
    nq = seq // tq
    kw = min(KEY_BLOCK, seq)
    ksub = kw // BLK

    def body(q_ref, k_ref, v_ref, o_ref, tot_ref):
        row, col = _iotas()
        colq = lax.broadcasted_iota(jnp.int32, (tq, BLK), 1)
        head0 = colq < HEAD_DIM
        upper = _tri_blocks(kw, lambda j, s: j > s)

        def qblock(qb, c):
            t0 = pl.multiple_of(qb * tq, tq)
            qs = _stack_heads(q_ref[pl.ds(t0, tq), :], head0, _SCALE)
            diag = (t0 + tq - 1) // kw

            def kblock(kb, carry, masked):
                acc, run = carry
                s0 = pl.multiple_of(kb * kw, kw)
                k = k_ref[pl.ds(s0, kw), :].astype(BF16)
                v0, v1 = _heads(v_ref[pl.ds(s0, kw), :], head0)
                mask = _causal_mask(t0, s0, tq, kw, False)[:tq] if masked else None
                zs = [_dot(qs[h * tq:(h + 1) * tq], k, _NT) for h in range(2)]
                terms = []
                for h in range(2):
                    lsz, lr = _sb_terms_z(zs[h], mask)
                    terms.append((lsz,) + _block_cumsum(lr, upper, ksub))
                runs = []
                for h, vh in enumerate((v0, v1)):
                    lsz, after, total = terms[h]
                    r = run[h]
                    ws = [None] * ksub
                    for j in reversed(range(ksub)):
                        w = jnp.exp(_sub(lsz, j) + after[j] + r)
                        if masked:
                            w = jnp.where(_sub(mask, j), w, 0.0)
                        ws[j] = w.astype(BF16)
                        r = r + total[j]
                    acc = acc + _dot(jnp.concatenate(ws, axis=1), vh)
                    runs.append(r)
                return acc, tuple(runs)

            zc = jnp.zeros((tq, 1), F32)
            carry = kblock(diag, (jnp.zeros((tq, BLK), F32), (zc, zc)), True)
            acc, run = lax.fori_loop(0, diag, lambda it, cr: kblock(diag - 1 - it, cr, False), carry)
            o_ref[pl.ds(t0, tq), :] = acc
            tot_ref[pl.ds(t0, tq), :] = jnp.where(head0, run[0], run[1])
            return c

        lax.fori_loop(0, nq, qblock, 0)

    return pl.pallas_call(
        body, name=name, grid=(nb, PAIRS),
        in_specs=[_pair_spec(seq, 0), _pair_spec(seq, PAIRS), _pair_spec(seq, 2 * PAIRS)],
        out_specs=[_pair_spec(seq, 0), _stat_spec(seq)],
        out_shape=[jax.ShapeDtypeStruct((nb * seq, D_MODEL), F32), jax.ShapeDtypeStruct((nb, PAIRS, seq, BLK), F32)],
        **_ATT_PARAMS,
    )(p, p, p)


def _sb_bwd(p, do, tot, nb, seq, *, name):
    tq = min(Q_BLOCK, seq)
    nq = seq // tq
    kw = min(KEY_BLOCK, seq)
    ksub = kw // BLK

    def body(q_ref, k_ref, v_ref, do_ref, tot_ref, dq_ref, dk_ref, dv_ref):
        row, col = _iotas()
        colq = lax.broadcasted_iota(jnp.int32, (tq, BLK), 1)
        head0 = colq < HEAD_DIM
        lower_incl = _tri_blocks(kw, lambda j, s: j <= s)
        lower_strict = _tri_blocks(kw, lambda s, j: s < j)
        dk_ref[...] = jnp.zeros(dk_ref.shape, F32)
        dv_ref[...] = jnp.zeros(dv_ref.shape, F32)

        def qblock(qb, c):
            t0 = pl.multiple_of(qb * tq, tq)
            qs = _stack_heads(q_ref[pl.ds(t0, tq), :], head0, _SCALE)
            dos = _stack_heads(do_ref[pl.ds(t0, tq), :], head0)
            tot = tot_ref[pl.ds(t0, tq), :]
            swapped = pltpu.roll(tot, HEAD_DIM, 1)
            tts = (jnp.where(head0, tot, swapped), jnp.where(head0, swapped, tot))
            diag = (t0 + tq - 1) // kw

            def kblock(kb, carry, masked):
                dq, pfs, efs = carry
                s0 = pl.multiple_of(kb * kw, kw)
                kf = k_ref[pl.ds(s0, kw), :]
                k = kf.astype(BF16)
                khs = _heads(kf, head0)
                v = v_ref[pl.ds(s0, kw), :].astype(BF16)
                mask = _causal_mask(t0, s0, tq, kw, False)[:tq] if masked else None
                zs = [_dot(qs[h * tq:(h + 1) * tq], k, _NT) for h in range(2)]
                dws = [_dot(dos[h * tq:(h + 1) * tq], v, _NT) for h in range(2)]
                first = []
                for h in range(2):
                    lsz, lr = _sb_terms_z(zs[h], None)
                    lrm = jnp.where(mask, lr, 0.0) if masked else lr
                    first.append((lsz, lr) + _block_cumsum(lrm, lower_incl, ksub))
                second, pfs_out = [], []
                for h in range(2):
                    lsz, lr, incl, total = first[h]
                    pf = pfs[h]
                    ws, ews = [], []
                    for j in range(ksub):
                        w = jnp.exp(_sub(lsz, j) + (tts[h] - pf - incl[j]))
                        if masked:
                            w = jnp.where(_sub(mask, j), w, 0.0)
                        pf = pf + total[j]
                        ws.append(w.astype(BF16))
                        ews.append(_sub(dws[h], j) * w)
                    pfs_out.append(pf)
                    second.append((ws, ews) + _block_cumsum(jnp.concatenate(ews, axis=1), lower_strict, ksub))
                dz_h, efs_out = [], []
                for h in range(2):
                    lsz, lr = first[h][:2]
                    ws, ews, before, etotal = second[h]
                    ef = efs[h]
                    dzs = []
                    for j in range(ksub):
                        dz = ews[j] * jnp.exp(_sub(lr, j)) - (ef + before[j]) * jnp.exp(_sub(lsz, j))
                        ef = ef + etotal[j]
                        if masked:
                            dz = jnp.where(_sub(mask, j), dz, 0.0)
                        dzs.append(dz.astype(BF16))
                    efs_out.append(ef)
                    dz_h.append(jnp.concatenate(dzs, axis=1))
                    dq = dq + _dot(dz_h[h], khs[h])
                w = jnp.concatenate([jnp.concatenate(second[h][0], axis=1) for h in range(2)], axis=0)
                dk_ref[pl.ds(s0, kw), :] += _dot(jnp.concatenate(dz_h, axis=0), qs, _TN)
                dv_ref[pl.ds(s0, kw), :] += _dot(w, dos, _TN)
                return dq, tuple(pfs_out), tuple(efs_out)

            zc = jnp.zeros((tq, 1), F32)
            carry = lax.fori_loop(0, diag, lambda kb, cr: kblock(kb, cr, False), (jnp.zeros((tq, BLK), F32), (zc, zc), (zc, zc)))
            dq_ref[pl.ds(t0, tq), :] = kblock(diag, carry, True)[0] * _SCALE
            return c

        lax.fori_loop(0, nq, qblock, 0)

    t = nb * seq
    return pl.pallas_call(
        body, name=name, grid=(nb, PAIRS),
        in_specs=[_pair_spec(seq, 0), _pair_spec(seq, PAIRS), _pair_spec(seq, 2 * PAIRS), _pair_spec(seq, 0), _stat_spec(seq)],
        out_specs=[_pair_spec(seq, 0)] * 3,
        out_shape=[jax.ShapeDtypeStruct((t, D_MODEL), F32)] * 3,
        **_ATT_PARAMS,
    )(p, p, p, do, tot)


def _fox_cum(f, bf, nb, seq, *, name):
    def body(f_ref, bf_ref, cc_ref, cr_ref):
        row, col = _iotas()
        lower = (col <= row).astype(BF16)
        carry = jnp.zeros((1, BLK), F32)
        for blk in range(seq // BLK):
            rs = slice(blk * BLK, (blk + 1) * BLK)
            lf = jnp.where(col < HEADS, _log_sigmoid(f_ref[rs, :] + bf_ref[...]), 0.0)
            cc = _dot3_left(lower, lf) + carry
            cc_ref[rs, :] = cc
            cr_ref[:, rs] = cc.T[0:HEADS, :]
            carry = carry + _colsum(lf)

    return pl.pallas_call(
        body, name=name, grid=(nb,),
        in_specs=[pl.BlockSpec((seq, BLK), lambda b: (b, 0)), pl.BlockSpec((1, BLK), lambda b: (0, 0))],
        out_specs=[pl.BlockSpec((seq, BLK), lambda b: (b, 0)), pl.BlockSpec((None, HEADS, seq), lambda b: (b, 0, 0))],
        out_shape=[jax.ShapeDtypeStruct((nb * seq, BLK), F32), jax.ShapeDtypeStruct((nb, HEADS, seq), F32)],
        compiler_params=pltpu.CompilerParams(dimension_semantics=("parallel",)),
    )(f, bf)


def _fox_cum_bwd(dcr, dcc, f, bf, nb, seq, *, name):
    def body(dcr_ref, dcc_ref, f_ref, bf_ref, df_ref, dbf_ref):
        row, col = _iotas()
        upper_incl = (col >= row).astype(BF16)

        @pl.when(pl.program_id(0) == 0)
        def _():
            dbf_ref[...] = jnp.zeros((1, BLK), F32)

        carry = jnp.zeros((1, BLK), F32)
        for blk in reversed(range(seq // BLK)):
            rs = slice(blk * BLK, (blk + 1) * BLK)
            dc = dcr_ref[:, rs].T + dcc_ref[rs, :]
            dlf = _dot3_left(upper_incl, dc) + carry
            carry = carry + _colsum(dc)
            fl = f_ref[rs, :] + bf_ref[...]
            df = jnp.where(col < HEADS, dlf * _sigmoid(-fl), 0.0)
            df_ref[rs, :] = df
            dbf_ref[...] += _colsum(df)

    return pl.pallas_call(
        body, name=name, grid=(nb,),
        in_specs=[pl.BlockSpec((None, BLK, seq), lambda b: (b, 0, 0)), pl.BlockSpec((seq, BLK), lambda b: (b, 0)),
                  pl.BlockSpec((seq, BLK), lambda b: (b, 0)), pl.BlockSpec((1, BLK), lambda b: (0, 0))],
        out_specs=[pl.BlockSpec((seq, BLK), lambda b: (b, 0)), pl.BlockSpec((1, BLK), lambda b: (0, 0))],
        out_shape=[jax.ShapeDtypeStruct((nb * seq, BLK), F32), jax.ShapeDtypeStruct((1, BLK), F32)],
        compiler_params=pltpu.CompilerParams(dimension_semantics=("arbitrary",)),
    )(dcr, dcc, f, bf)


def _fox_cum_cols(cc_ref, t0, tq, colq, hp):
    cc = cc_ref[pl.ds(t0, tq), :]
    c0 = jnp.sum(jnp.where(colq == 2 * hp, cc, 0.0), axis=1, keepdims=True)
    c1 = jnp.sum(jnp.where(colq == 2 * hp + 1, cc, 0.0), axis=1, keepdims=True)
    return c0, c1


def _fox_bias(c0, c1, cr_ref, s0, kw):
    return jnp.concatenate([c0 - cr_ref[0:1, pl.ds(s0, kw)], c1 - cr_ref[1:2, pl.ds(s0, kw)]], axis=0)


def _fox_fwd(p, cc, cr, nb, seq, *, name):
    tq = min(Q_BLOCK, seq)
    nq = seq // tq
    kw = min(KEY_BLOCK, seq)
    ksub = kw // BLK

    def body(q_ref, k_ref, v_ref, cc_ref, cr_ref, o_ref, lse_ref):
        hp = pl.program_id(1)
        row, col = _iotas()
        colq = lax.broadcasted_iota(jnp.int32, (tq, BLK), 1)
        head0 = colq < HEAD_DIM

        def qblock(qb, c):
            t0 = pl.multiple_of(qb * tq, tq)
            qs = _stack_heads(q_ref[pl.ds(t0, tq), :], head0, _SCALE)
            c0, c1 = _fox_cum_cols(cc_ref, t0, tq, colq, hp)
            diag = (t0 + tq - 1) // kw

            def kblock(kb, carry, masked):
                accs, ms = carry
                s0 = pl.multiple_of(kb * kw, kw)
                k = k_ref[pl.ds(s0, kw), :].astype(BF16)
                vf = v_ref[pl.ds(s0, kw), :]
                own0 = lax.broadcasted_iota(jnp.int32, vf.shape, 1) < HEAD_DIM
                vs = (jnp.where(own0, vf, 1.0).astype(BF16), jnp.where(own0, 1.0, vf).astype(BF16))
                mask = _causal_mask(t0, s0, tq, kw, True)[:tq] if masked else None
                zs = [_dot(qs[h * tq:(h + 1) * tq], k, _NT) for h in range(2)]
                parts = []
                for h, ch in enumerate((c0, c1)):
                    s = zs[h] + (ch - cr_ref[h:h + 1, pl.ds(s0, kw)])
                    if masked:
                        s = jnp.where(mask, s, -jnp.inf)
                    m_new = jnp.maximum(ms[h], jnp.max(s, axis=1, keepdims=True))
                    parts.append((jnp.exp(s - m_new).astype(BF16), jnp.exp(ms[h] - m_new), m_new))
                return (tuple(accs[h] * parts[h][1] + _dot(parts[h][0], vs[h]) for h in range(2)),
                        tuple(parts[h][2] for h in range(2)))

            zeros, ninf = jnp.zeros((tq, BLK), F32), jnp.full((tq, 1), -jnp.inf, F32)
            carry = lax.fori_loop(0, diag, lambda kb, cr: kblock(kb, cr, False), ((zeros, zeros), (ninf, ninf)))
            (acc0, acc1), (m0, m1) = kblock(diag, carry, True)
            l = jnp.where(head0, pltpu.roll(acc0, HEAD_DIM, 1), pltpu.roll(acc1, HEAD_DIM, 1))
            o_ref[pl.ds(t0, tq), :] = jnp.where(head0, acc0, acc1) / l
            lse_ref[pl.ds(t0, tq), :] = jnp.where(head0, m0, m1) + jnp.log(l)
            return c

        lax.fori_loop(0, nq, qblock, 0)

    return pl.pallas_call(
        body, name=name, grid=(nb, PAIRS),
        in_specs=[_pair_spec(seq, 0), _pair_spec(seq, PAIRS), _pair_spec(seq, 2 * PAIRS),
                  pl.BlockSpec((seq, BLK), lambda b, hp: (b, 0)), pl.BlockSpec((None, None, 8, seq), lambda b, hp: (b, hp, 0, 0))],
        out_specs=[_pair_spec(seq, 0), _stat_spec(seq)],
        out_shape=[jax.ShapeDtypeStruct((nb * seq, D_MODEL), F32), jax.ShapeDtypeStruct((nb, PAIRS, seq, BLK), F32)],
        **_ATT_PARAMS,
    )(p, p, p, cc, cr)


def _fox_bwd(p, do, o, lse, cc, cr, nb, seq, *, name):
    tq = min(Q_BLOCK, seq)
    nq = seq // tq
    kw = min(KEY_BLOCK, seq)
    ksub = kw // BLK

    def body(q_ref, k_ref, v_ref, do_ref, o_ref, lse_ref, cc_ref, cr_ref, dq_ref, dk_ref, dv_ref, dcr_ref, dcc_ref):
        hp = pl.program_id(1)
        row, col = _iotas()
        colq = lax.broadcasted_iota(jnp.int32, (tq, BLK), 1)
        head0 = colq < HEAD_DIM
        dk_ref[...] = jnp.zeros(dk_ref.shape, F32)
        dv_ref[...] = jnp.zeros(dv_ref.shape, F32)
        dcr_ref[...] = jnp.zeros(dcr_ref.shape, F32)

        @pl.when(hp == 0)
        def _():
            dcc_ref[...] = jnp.zeros(dcc_ref.shape, F32)

        def qblock(qb, c):
            t0 = pl.multiple_of(qb * tq, tq)
            qs = _stack_heads(q_ref[pl.ds(t0, tq), :], head0, _SCALE)
            dof = do_ref[pl.ds(t0, tq), :]
            dos = _stack_heads(dof, head0)
            prod = dof * o_ref[pl.ds(t0, tq), :]
            dl = jnp.concatenate([jnp.sum(jnp.where(head0, prod, 0.0), axis=1, keepdims=True),
                                  jnp.sum(jnp.where(head0, 0.0, prod), axis=1, keepdims=True)], axis=0)
            lse = _pair_cols(lse_ref[pl.ds(t0, tq), :], head0, -jnp.inf)
            c0, c1 = _fox_cum_cols(cc_ref, t0, tq, colq, hp)
            diag = (t0 + tq - 1) // kw

            def kblock(kb, carry, masked):
                dq, rs = carry
                s0 = pl.multiple_of(kb * kw, kw)
                kf = k_ref[pl.ds(s0, kw), :]
                k = kf.astype(BF16)
                k0, k1 = _heads(kf, head0)
                v = v_ref[pl.ds(s0, kw), :].astype(BF16)
                mask = _causal_mask(t0, s0, tq, kw, True)[:tq] if masked else None
                zs = [_dot(qs[h * tq:(h + 1) * tq], k, _NT) for h in range(2)]
                dps = [_dot(dos[h * tq:(h + 1) * tq], v, _NT) for h in range(2)]
                prs, dss, rss = [], [], []
                for h, (ch, kh) in enumerate(((c0, k0), (c1, k1))):
                    rows = slice(h * tq, (h + 1) * tq)
                    pr = jnp.exp(zs[h] + (ch - cr_ref[h:h + 1, pl.ds(s0, kw)]) - lse[rows])
                    if masked:
                        pr = jnp.where(mask, pr, 0.0)
                    ds = pr * (dps[h] - dl[rows])
                    dcr_ref[h:h + 1, pl.ds(s0, kw)] -= _colsum(ds)
                    rss.append(rs[rows] + jnp.sum(ds, axis=1, keepdims=True))
                    prs.append(pr.astype(BF16))
                    dss.append(ds.astype(BF16))
                    dq = dq + _dot(dss[h], kh)
                dk_ref[pl.ds(s0, kw), :] += _dot(jnp.concatenate(dss, axis=0), qs, _TN)
                dv_ref[pl.ds(s0, kw), :] += _dot(jnp.concatenate(prs, axis=0), dos, _TN)
                return dq, jnp.concatenate(rss, axis=0)

            init = (jnp.zeros((tq, BLK), F32), jnp.zeros((2 * tq, 1), F32))
            carry = lax.fori_loop(0, diag, lambda kb, cr: kblock(kb, cr, False), init)
            dq, rs = kblock(diag, carry, True)
            dq_ref[pl.ds(t0, tq), :] = dq * _SCALE
            dcc_ref[pl.ds(t0, tq), :] += jnp.where(colq == 2 * hp, rs[:tq], 0.0) + jnp.where(colq == 2 * hp + 1, rs[tq:], 0.0)
            return c

        lax.fori_loop(0, nq, qblock, 0)

    t = nb * seq
    return pl.pallas_call(
        body, name=name, grid=(nb, PAIRS),
        in_specs=[_pair_spec(seq, 0), _pair_spec(seq, PAIRS), _pair_spec(seq, 2 * PAIRS), _pair_spec(seq, 0), _pair_spec(seq, 0),
                  _stat_spec(seq), pl.BlockSpec((seq, BLK), lambda b, hp: (b, 0)),
                  pl.BlockSpec((None, None, 8, seq), lambda b, hp: (b, hp, 0, 0))],
        out_specs=[_pair_spec(seq, 0)] * 3 + [pl.BlockSpec((None, None, 8, seq), lambda b, hp: (b, hp, 0, 0)),
                                              pl.BlockSpec((seq, BLK), lambda b, hp: (b, 0))],
        out_shape=[jax.ShapeDtypeStruct((t, D_MODEL), F32)] * 3 + [jax.ShapeDtypeStruct((nb, PAIRS, 8, seq), F32),
                                                                     jax.ShapeDtypeStruct((t, BLK), F32)],
        compiler_params=pltpu.CompilerParams(dimension_semantics=("parallel", "arbitrary")),
    )(p, p, p, do, o, lse, cc, cr)


def _row_shards(x):
    return x.reshape(N_CHIPS, x.shape[0] // N_CHIPS, x.shape[1])


def _local_step(x3, tgt3, w, later=None, start_reduce=None):
    nb, seq, d = x3.shape
    t = nb * seq
    x0, tgt = x3.reshape(t, d), tgt3.reshape(t, d)
    g = {}

    a_gain = w["a_norm"].reshape(1, d)
    h_a, ht_a = _rmsnorm(x0, a_gain, name="a_norm_fwd")
    p_a = _matmul(h_a, w["a_w_in"][:3], name="a_in_fwd", out_dtype=BF16)
    gate_a = _matmul(h_a, w["a_w_in"][3], name="a_in_gate_fwd")
    o_a, tot_a = _sb_fwd(p_a, nb, seq, name="a_attn_fwd")
    y_a, yt_a = _gate(o_a, gate_a, 0, name="a_gate_fwd")
    x1 = _matmul(y_a, w["a_w_out"], name="a_out_fwd", residual=x0)

    if later:
        w = {**w, **later[0](x1)}
    b_gain = w["b_norm"].reshape(1, d)
    b_lg, b_lb = w["b_v_ln_g"].reshape(1, GM_W), w["b_v_ln_b"].reshape(1, GM_W)
    b_ws, b_bst = w["b_w_s"].reshape(GM_G, BLK, BLK), w["b_b_s"].reshape(GM_G, BLK).T
    h_b, ht_b = _rmsnorm(x1, b_gain, name="b_norm_fwd")
    p_b = _matmul(h_b, w["b_w_in"], name="b_in_fwd")
    y_b, yt_b = _gmlp_fwd(p_b, b_lg, b_lb, b_ws, b_bst, name="b_mix_fwd")
    x2 = _matmul(y_b, w["b_w_out"], name="b_out_fwd", residual=x1)

    if later:
        w = {**w, **later[1](x2)}
    c_gain = w["c_norm"].reshape(1, d)
    c_cw = jnp.repeat(w["c_conv_w"].reshape(CV_K, CV_W), SUBLANES, axis=0)
    c_cb = w["c_conv_b"].reshape(1, CV_W)
    c_lg, c_lb = w["c_ln_g"].reshape(1, CV_W), w["c_ln_b"].reshape(1, CV_W)
    h_c, ht_c = _rmsnorm(x2, c_gain, name="c_norm_fwd")
    p_c = _matmul(h_c, w["c_w_in"], name="c_in_fwd")
    y_c, yt_c = _conv_fwd(p_c, c_cw, c_cb, c_lg, c_lb, seq, name="c_conv_fwd")
    x3_ = _matmul(y_c, w["c_w_out"], name="c_out_fwd", residual=x2)

    d_gain = w["d_norm"].reshape(1, d)
    d_win = w["d_w_in"].reshape(d, 4 * D_MODEL + HEADS)
    d_wmain = d_win[:, :4 * D_MODEL]
    d_wf = jnp.pad(d_win[:, 4 * D_MODEL:], ((0, 0), (0, BLK - HEADS)))
    d_bf = jnp.pad(w["d_b_f"].reshape(1, HEADS), ((0, 0), (0, BLK - HEADS)))
    h_d, ht_d = _rmsnorm(x3_, d_gain, name="d_norm_fwd")
    p_d = _matmul(h_d, d_wmain[:, :3 * D_MODEL], name="d_in_fwd", out_dtype=BF16)
    gate_d = _matmul(h_d, d_wmain[:, 3 * D_MODEL:], name="d_in_gate_fwd")
    f_d = _matmul(h_d, d_wf, name="d_inf_fwd")
    cc, cr = _fox_cum(f_d, d_bf, nb, seq, name="d_cum_fwd")
    cr = jnp.pad(cr.reshape(nb, PAIRS, 2, seq), ((0, 0), (0, 0), (0, 6), (0, 0)))
    o_d, lse_d = _fox_fwd(p_d, cc, cr, nb, seq, name="d_attn_fwd")
    y_d, yt_d = _gate(o_d, gate_d, 0, name="d_gate_fwd")
    x4 = _matmul(y_d, w["d_w_out"], name="d_out_fwd", residual=x3_)

    f_gain = w["final_norm"].reshape(1, d)
    dx, g_fn, loss_row = _loss_head(x4, f_gain, tgt, name="loss_head")
    g["final_norm"] = g_fn

    g["d_w_out"] = _row_shards(_matmul(yt_d, dx, name="d_out_dw"))
    dy = _matmul(dx, w["d_w_out"], name="d_out_dy", mode="nt")
    do_d, dg_d = _gate_bwd(dy, o_d, gate_d, 0, name="d_gate_bwd")
    dq, dk, dv, dcr, dcc = _fox_bwd(p_d, do_d, o_d, lse_d, cc, cr, nb, seq, name="d_attn_bwd")
    dcr = jnp.pad(dcr[:, :, :2, :].reshape(nb, HEADS, seq), ((0, 0), (0, BLK - HEADS), (0, 0)))
    df, dbf = _fox_cum_bwd(dcr, dcc, f_d, d_bf, nb, seq, name="d_cum_bwd")
    g["d_b_f"] = dbf[:, :HEADS]
    parts = [dq, dk, dv, dg_d]
    dws = [_matmul(ht_d, pt, name=f"d_in_dw{n}") for n, pt in enumerate(parts)]
    dwf = _matmul(ht_d, df, name="d_inf_dw")
    g["d_w_in"] = jnp.concatenate(dws + [dwf[:, :HEADS]], axis=1).reshape(d, N_CHIPS, -1).transpose(1, 0, 2)
    dh = _matmul(df, d_wf, name="d_inf_dh", mode="nt")
    for n, pt in enumerate(parts):
        dh = _matmul(pt, d_wmain[:, n * D_MODEL:(n + 1) * D_MODEL], name=f"d_in_dh{n}", mode="nt", residual=dh)
    dx, g["d_norm"] = _rmsnorm_bwd(dh, x3_, d_gain, dx, name="d_norm_bwd")

    g["c_w_out"] = _row_shards(_matmul(yt_c, dx, name="c_out_dw"))
    dy = _matmul(dx, w["c_w_out"], name="c_out_dy", mode="nt")
    dy1, dgate, g["c_ln_g"], g["c_ln_b"], g["c_conv_b"], g["c_conv_w"] = _conv_bwd_post(
        dy, p_c, c_cw, c_cb, c_lg, c_lb, seq, name="c_conv_bwd_post")
    dp = _conv_bwd_pre(dy1, dgate, p_c, c_cw, seq, name="c_conv_bwd_pre")
    g["c_w_in"] = _matmul(ht_c, dp, name="c_in_dw", out_shards=N_CHIPS)
    dh = _matmul(dp, w["c_w_in"], name="c_in_dh", mode="nt")
    dx, g["c_norm"] = _rmsnorm_bwd(dh, x2, c_gain, dx, name="c_norm_bwd")

    early, b_wout, a_wout = [], w["b_w_out"], w["a_w_out"]
    if start_reduce is not None:
        begun, token = start_reduce({n: g[n] for n in ("d_w_in", "d_w_out", "c_w_in", "c_w_out")}, "grads_cd")
        early.append(begun)
        b_wout = b_wout + token[0, 0].astype(b_wout.dtype)
    g["b_w_out"] = _row_shards(_matmul(yt_b, dx, name="b_out_dw"))
    dy = _matmul(dx, b_wout, name="b_out_dy", mode="nt")
    dp, g["b_v_ln_g"], g["b_v_ln_b"], g["b_w_s"], dbst = _gmlp_bwd(dy, p_b, b_lg, b_lb, b_ws, b_bst, name="b_mix_bwd")
    g["b_b_s"] = dbst.T
    g["b_w_in"] = _matmul(ht_b, dp, name="b_in_dw", out_shards=N_CHIPS)
    dh = _matmul(dp, w["b_w_in"], name="b_in_dh", mode="nt")
    dx, g["b_norm"] = _rmsnorm_bwd(dh, x1, b_gain, dx, name="b_norm_bwd")

    if start_reduce is not None:
        begun, token = start_reduce({n: g[n] for n in ("b_w_in", "b_w_out")}, "grads_b")
        early.append(begun)
        a_wout = a_wout + token[0, 0].astype(a_wout.dtype)
    g["a_w_out"] = _row_shards(_matmul(yt_a, dx, name="a_out_dw"))
    dy = _matmul(dx, a_wout, name="a_out_dy", mode="nt")
    do_a, dg_a = _gate_bwd(dy, o_a, gate_a, 0, name="a_gate_bwd")
    dq, dk, dv = _sb_bwd(p_a, do_a, tot_a, nb, seq, name="a_attn_bwd")
    parts = [dq, dk, dv, dg_a]
    g["a_w_in"] = jnp.stack([_matmul(ht_a, pt, name=f"a_in_dw{n}") for n, pt in enumerate(parts)])
    dh = None
    for n, pt in enumerate(parts):
        dh = _matmul(pt, w["a_w_in"][n], name=f"a_in_dh{n}", mode="nt", residual=dh)
    dx, g["a_norm"] = _rmsnorm_bwd(dh, x0, a_gain, dx, name="a_norm_bwd")

    return loss_row[0, 0], dx.reshape(nb, seq, d), g, early


_HBM = pl.BlockSpec(memory_space=pltpu.HBM)


def _place():
    return lax.axis_index("x"), lax.axis_index("y"), lax.axis_index("c")


def _other_chips(x, y):
    return [(1 - x, y), (x, 1 - y), (1 - x, 1 - y)]


def _allgather_chips(ss, *, name):
    n_ops = len(ss)

    def body(*refs):
        s_refs, o_refs, (send_sems, recv_sems) = refs[:n_ops], refs[n_ops:2 * n_ops], refs[2 * n_ops:]
        x, y, c = _place()
        me = 2 * x + y
        chips = _other_chips(x, y)

        def copy(i, kk, src, dst, to):
            return pltpu.make_async_remote_copy(src_ref=src, dst_ref=dst, send_sem=send_sems.at[6 * i + kk],
                                                recv_sem=recv_sems.at[6 * i + kk], device_id=to, device_id_type=MESH)

        def half(i, j, hc):
            h = s_refs[i].shape[0] // 2
            return o_refs[i].at[j, pl.ds(hc * h, h), :]

        first = [copy(i, kk, s_refs[i].at[pl.ds(c * (s_refs[i].shape[0] // 2), s_refs[i].shape[0] // 2), :], half(i, me, c),
                      (cx, cy, c)) for kk, (cx, cy) in enumerate(chips) for i in range(n_ops)]
        for cp in first:
            cp.start()
        passed = []
        for kk, (cx, cy) in enumerate(chips):
            for i in range(n_ops):
                blk = half(i, 2 * cx + cy, c)
                copy(i, kk, blk, blk, (cx, cy, c)).wait_recv()
                fwd = copy(i, 3 + kk, blk, blk, (x, y, 1 - c))
                fwd.start()
                passed.append(fwd)
        for kk, (cx, cy) in enumerate(chips):
            for i in range(n_ops):
                blk = half(i, 2 * cx + cy, 1 - c)
                copy(i, 3 + kk, blk, blk, (x, y, 1 - c)).wait_recv()
        for cp in first + passed:
            cp.wait_send()

    for s in ss:
        assert s.shape[0] % 32 == 0, s.shape
    return pl.pallas_call(
        body, name=name, in_specs=[_HBM] * n_ops, out_specs=[_HBM] * n_ops,
        out_shape=[jax.ShapeDtypeStruct((N_CHIPS,) + s.shape, s.dtype) for s in ss],
        scratch_shapes=[pltpu.SemaphoreType.DMA((6 * n_ops,)), pltpu.SemaphoreType.DMA((6 * n_ops,))],
    )(*ss)


_SEM = pl.BlockSpec(memory_space=pltpu.SEMAPHORE)
_ANY = pl.BlockSpec(memory_space=pl.ANY)
_DATAFLOW = pltpu.SideEffectType.DATAFLOW_SIDE_EFFECTING


def _chip_copies(s_refs, land_refs, send_sems, recv_sems):
    x, y, c = _place()
    me = 2 * x + y
    cps = []
    for i, (s_ref, land_ref) in enumerate(zip(s_refs, land_refs)):
        h = s_ref.shape[0] // 2
        for kk, (cx, cy) in enumerate(_other_chips(x, y)):
            cps.append(pltpu.make_async_remote_copy(
                src_ref=s_ref.at[pl.ds(c * h, h), :], dst_ref=land_ref.at[me, pl.ds(c * h, h), :], send_sem=send_sems.at[3 * i + kk],
                recv_sem=recv_sems.at[3 * i + kk], device_id=(cx, cy, c), device_id_type=MESH))
    return cps


def _gather_start(ss, after, *, name):
    n = len(ss)
    lands = [lax.empty((N_CHIPS,) + s.shape, s.dtype) for s in ss]

    def body(*refs):
        s_refs, land_refs = refs[:n], refs[n:2 * n]
        send_sems, recv_sems = refs[2 * n + 1], refs[2 * n + 2]
        token = refs[-1]
        for cp in _chip_copies(s_refs, land_refs, send_sems, recv_sems):
            cp.start()
        token[...] = jnp.zeros(token.shape, token.dtype)

    hbm = [pltpu.HBM(a.shape, a.dtype) for a in list(ss) + lands]
    res = pl.pallas_call(
        body, name=name,
        out_shape=(pltpu.SemaphoreType.DMA((3 * n,)), pltpu.SemaphoreType.DMA((3 * n,)), *hbm, jax.ShapeDtypeStruct((8, BLK), F32)),
        in_specs=[_HBM] * (2 * n) + [_ANY],
        out_specs=(_SEM, _SEM, *([_HBM] * (2 * n)), pl.BlockSpec(memory_space=pltpu.VMEM)),
        input_output_aliases={i: 2 + i for i in range(2 * n)},
        compiler_params=pltpu.CompilerParams(has_side_effects=_DATAFLOW),
    )(*[pltpu.with_memory_space_constraint(a, pltpu.HBM) for a in list(ss) + lands], after)
    return res[:-1], res[-1]


def _gather_wait(started, after, *, name):
    send_sems, recv_sems = started[0], started[1]
    n = (len(started) - 2) // 2

    def body(*refs):
        s_refs, land_refs = refs[:n], refs[n:2 * n]
        for cp in _chip_copies(s_refs, land_refs, refs[2 * n], refs[2 * n + 1]):
            cp.wait_send()
            cp.wait_recv()

    res = pl.pallas_call(
        body, name=name, out_shape=tuple(pltpu.HBM(a.shape, a.dtype) for a in started[2:]),
        in_specs=[_HBM] * (2 * n) + [_SEM, _SEM, _ANY], out_specs=tuple([_HBM] * (2 * n)),
        input_output_aliases={i: i for i in range(2 * n)},
        compiler_params=pltpu.CompilerParams(has_side_effects=_DATAFLOW),
    )(*started[2:], send_sems, recv_sems, after)
    return list(res[n:])


def _sibling_exchange(lands, *, name):
    n = len(lands)

    def body(*refs):
        o_refs, (send_sems, recv_sems) = refs[n:2 * n], refs[2 * n:]
        x, y, c = _place()
        cps = []
        for i, o_ref in enumerate(o_refs):
            h = o_ref.shape[1] // 2
            for kk, (cx, cy) in enumerate(_other_chips(x, y)):
                def half(hc):
                    return o_ref.at[2 * cx + cy, pl.ds(hc * h, h), :]
                sent = pltpu.make_async_remote_copy(src_ref=half(c), dst_ref=half(c), send_sem=send_sems.at[3 * i + kk],
                                                    recv_sem=recv_sems.at[3 * i + kk], device_id=(x, y, 1 - c), device_id_type=MESH)
                awaited = pltpu.make_async_remote_copy(src_ref=half(1 - c), dst_ref=half(1 - c), send_sem=send_sems.at[3 * i + kk],
                                                       recv_sem=recv_sems.at[3 * i + kk], device_id=(x, y, 1 - c),
                                                       device_id_type=MESH)
                cps.append((sent, awaited))
        for sent, _ in cps:
            sent.start()
        for sent, awaited in cps:
            awaited.wait_recv()
            sent.wait_send()

    return pl.pallas_call(
        body, name=name, in_specs=[_HBM] * n, out_specs=[_HBM] * n,
        out_shape=[jax.ShapeDtypeStruct(a.shape, a.dtype) for a in lands], scratch_shapes=_dma_sems(3 * n),
        input_output_aliases={i: i for i in range(n)},
    )(*lands)


def _own_block(gathered, s):
    me = 2 * lax.axis_index("x") + lax.axis_index("y")
    return lax.dynamic_update_slice(gathered, s[None], (me,) + (0,) * s.ndim)


def _dma_sems(n):
    return [pltpu.SemaphoreType.DMA((n,)), pltpu.SemaphoreType.DMA((n,))]


def _swap_halves(gps, *, name):
    n_ops = len(gps)

    def body(*refs):
        g_refs, o_refs, (send_sems, recv_sems) = refs[:n_ops], refs[n_ops:2 * n_ops], refs[2 * n_ops:]
        x, y, c = _place()
        cps = []
        for i, (g_ref, o_ref) in enumerate(zip(g_refs, o_refs)):
            h = g_ref.shape[1] // 2
            cps.append(pltpu.make_async_remote_copy(
                src_ref=g_ref.at[:, pl.ds((1 - c) * h, h), :], dst_ref=o_ref, send_sem=send_sems.at[i], recv_sem=recv_sems.at[i],
                device_id=(x, y, 1 - c), device_id_type=MESH))
        for cp in cps:
            cp.start()
        for cp in cps:
            cp.wait()

    return pl.pallas_call(
        body, name=name, in_specs=[_HBM] * n_ops, out_specs=[_HBM] * n_ops,
        out_shape=[jax.ShapeDtypeStruct((g.shape[0], g.shape[1] // 2, g.shape[2]), g.dtype) for g in gps],
        scratch_shapes=_dma_sems(n_ops),
    )(*gps)


def _scatter_chips(hps, *, name):
    n_ops = len(hps)

    def body(*refs):
        h_refs, o_refs, (send_sems, recv_sems) = refs[:n_ops], refs[n_ops:2 * n_ops], refs[2 * n_ops:]
        x, y, c = _place()
        cps = [pltpu.make_async_remote_copy(src_ref=h_ref.at[2 * cx + cy], dst_ref=o_ref.at[kk], send_sem=send_sems.at[3 * i + kk],
                                            recv_sem=recv_sems.at[3 * i + kk], device_id=(cx, cy, c), device_id_type=MESH)
               for i, (h_ref, o_ref) in enumerate(zip(h_refs, o_refs)) for kk, (cx, cy) in enumerate(_other_chips(x, y))]
        for cp in cps:
            cp.start()
        for cp in cps:
            cp.wait()

    return pl.pallas_call(
        body, name=name, in_specs=[_HBM] * n_ops, out_specs=[_HBM] * n_ops,
        out_shape=[jax.ShapeDtypeStruct((3,) + hp.shape[1:], hp.dtype) for hp in hps],
        scratch_shapes=_dma_sems(3 * n_ops),
    )(*hps)


def _join_halves(fs, *, name):
    n_ops = len(fs)

    def body(*refs):
        f_refs, o_refs, (send_sems, recv_sems) = refs[:n_ops], refs[n_ops:2 * n_ops], refs[2 * n_ops:]
        x, y, c = _place()
        cps = [pltpu.make_async_remote_copy(src_ref=f_ref, dst_ref=o_ref, send_sem=send_sems.at[i], recv_sem=recv_sems.at[i],
                                            device_id=(x, y, 1 - c), device_id_type=MESH)
               for i, (f_ref, o_ref) in enumerate(zip(f_refs, o_refs))]
        for cp in cps:
            cp.start()
        for cp in cps:
            cp.wait()

    theirs = pl.pallas_call(
        body, name=name, in_specs=[_HBM] * n_ops, out_specs=[_HBM] * n_ops,
        out_shape=[jax.ShapeDtypeStruct(f.shape, f.dtype) for f in fs], scratch_shapes=_dma_sems(n_ops),
    )(*fs)
    south = lax.axis_index("c") == 0
    return [jnp.concatenate([jnp.where(south, f, t), jnp.where(south, t, f)], axis=0) for f, t in zip(fs, theirs)]


def _add_halves(gp, ra, wire_dtype, *, name, bm=256):
    n, r, c_ = gp.shape
    h = r // 2
    bm = _tile(h, bm)
    per = h // bm
    c = lax.axis_index("c").astype(jnp.int32).reshape(1)

    def body(c_ref, g_ref, ra_ref, o_ref, ow_ref):
        s = g_ref[...] + ra_ref[...]
        o_ref[...] = s
        ow_ref[...] = s.astype(wire_dtype)

    mine = pl.BlockSpec((None, bm, c_), lambda j, i, cr: (j, i, 0))
    return pl.pallas_call(
        body, name=name,
        grid_spec=pltpu.PrefetchScalarGridSpec(
            num_scalar_prefetch=1, grid=(n, per),
            in_specs=[pl.BlockSpec((None, bm, c_), lambda j, i, cr: (j, cr[0] * per + i, 0)), mine],
            out_specs=[mine, mine]),
        out_shape=[jax.ShapeDtypeStruct((n, h, c_), F32), jax.ShapeDtypeStruct((n, h, c_), wire_dtype)],
        compiler_params=pltpu.CompilerParams(dimension_semantics=("parallel", "parallel")),
    )(c, gp, ra)


def _add_chips(hp, rb, *, name, bm=256):
    n, h, c_ = hp.shape
    bm = _tile(h, bm)
    me = (2 * lax.axis_index("x") + lax.axis_index("y")).astype(jnp.int32).reshape(1)

    def body(me_ref, h_ref, rb_ref, o_ref):
        o_ref[...] = ((h_ref[...] + rb_ref[0].astype(F32)) + rb_ref[1].astype(F32)) + rb_ref[2].astype(F32)

    return pl.pallas_call(
        body, name=name,
        grid_spec=pltpu.PrefetchScalarGridSpec(
            num_scalar_prefetch=1, grid=(h // bm,),
            in_specs=[pl.BlockSpec((None, bm, c_), lambda i, mr: (mr[0], i, 0)),
                      pl.BlockSpec((3, bm, c_), lambda i, mr: (0, i, 0))],
            out_specs=pl.BlockSpec((bm, c_), lambda i, mr: (i, 0))),
        out_shape=jax.ShapeDtypeStruct((h, c_), F32),
        compiler_params=pltpu.CompilerParams(dimension_semantics=("parallel",)),
    )(me, hp, rb)


def _scatter_copies(h_refs, land_refs, send_sems, recv_sems):
    x, y, c = _place()
    return [pltpu.make_async_remote_copy(src_ref=h_ref.at[2 * cx + cy], dst_ref=land_ref.at[kk], send_sem=send_sems.at[3 * i + kk],
                                         recv_sem=recv_sems.at[3 * i + kk], device_id=(cx, cy, c), device_id_type=MESH)
            for i, (h_ref, land_ref) in enumerate(zip(h_refs, land_refs)) for kk, (cx, cy) in enumerate(_other_chips(x, y))]


def _scatter_start(hps, after, *, name):
    n = len(hps)
    lands = [lax.empty((3,) + hp.shape[1:], hp.dtype) for hp in hps]

    def body(*refs):
        for cp in _scatter_copies(refs[:n], refs[n:2 * n], refs[2 * n + 1], refs[2 * n + 2]):
            cp.start()
        refs[-1][...] = jnp.zeros(refs[-1].shape, refs[-1].dtype)

    hbm = [pltpu.HBM(a.shape, a.dtype) for a in list(hps) + lands]
    res = pl.pallas_call(
        body, name=name,
        out_shape=(pltpu.SemaphoreType.DMA((3 * n,)), pltpu.SemaphoreType.DMA((3 * n,)), *hbm, jax.ShapeDtypeStruct((8, BLK), F32)),
        in_specs=[_HBM] * (2 * n) + [_ANY],
        out_specs=(_SEM, _SEM, *([_HBM] * (2 * n)), pl.BlockSpec(memory_space=pltpu.VMEM)),
        input_output_aliases={i: 2 + i for i in range(2 * n)},
        compiler_params=pltpu.CompilerParams(has_side_effects=_DATAFLOW),
    )(*[pltpu.with_memory_space_constraint(a, pltpu.HBM) for a in list(hps) + lands], after)
    return res[:-1], res[-1]


def _scatter_wait(started, after, *, name):
    n = (len(started) - 2) // 2

    def body(*refs):
        for cp in _scatter_copies(refs[:n], refs[n:2 * n], refs[2 * n], refs[2 * n + 1]):
            cp.wait_send()
            cp.wait_recv()

    res = pl.pallas_call(
        body, name=name, out_shape=tuple(pltpu.HBM(a.shape, a.dtype) for a in started[2:]),
        in_specs=[_HBM] * (2 * n) + [_SEM, _SEM, _ANY], out_specs=tuple([_HBM] * (2 * n)),
        input_output_aliases={i: i for i in range(2 * n)},
        compiler_params=pltpu.CompilerParams(has_side_effects=_DATAFLOW),
    )(*started[2:], started[0], started[1], after)
    return list(res[n:])


def _reduce_to_chips(gps, wire_dtypes, *, tag):
    ras = _swap_halves(gps, name=f"{tag}_swap_halves")
    return [_add_halves(gp, ra, wd, name=f"{tag}_add_halves{i}") for i, (gp, ra, wd) in enumerate(zip(gps, ras, wire_dtypes))]


def _start_reduce(early, tag):
    names = list(early)
    hps = _reduce_to_chips([early[n] for n in names], [BF16] * len(names), tag=tag)
    started, token = _scatter_start([hw for _, hw in hps], hps[-1][1], name=f"{tag}_scatter_start")
    return (tag, names, [hf for hf, _ in hps], started), token


def _adamw_math(w_ref, g_ref, m_ref, v_ref, d_ref, nm_ref, nv_ref):
    c1 = 1.0 - ADAM_B1 ** ADAM_STEP
    c2 = 1.0 - ADAM_B2 ** ADAM_STEP
    g_ = g_ref[...]
    m_ = ADAM_B1 * m_ref[...] + (1.0 - ADAM_B1) * g_
    v_ = ADAM_B2 * v_ref[...] + (1.0 - ADAM_B2) * (g_ * g_)
    d_ref[...] = -ADAM_LR * ((m_ / c1) / (jnp.sqrt(v_ / c2) + ADAM_EPS) + ADAM_WD * w_ref[...])
    nm_ref[...] = m_
    nv_ref[...] = v_


def _adamw_many(groups, *, name):
    n = len(groups[0])
    flat = [a for grp in groups for a in grp]

    def body(*refs):
        ins, outs = refs[:4 * n], refs[4 * n:]
        for i in range(n):
            _adamw_math(ins[i], ins[n + i], ins[2 * n + i], ins[3 * n + i], outs[i], outs[n + i], outs[2 * n + i])

    vmem = pl.BlockSpec(memory_space=pltpu.VMEM)
    res = pl.pallas_call(
        body, name=name, in_specs=[vmem] * (4 * n), out_specs=[vmem] * (3 * n),
        out_shape=[jax.ShapeDtypeStruct(a.shape, F32) for _ in range(3) for a in groups[0]],
    )(*flat)
    return res[:n], res[n:2 * n], res[2 * n:]


def _adamw(w, g, m, v, *, name):
    r, c_ = w.shape
    bm = r
    for cand in (512, 256, 128, 64, 32, 16, 8):
        if r % cand == 0:
            bm = cand
            break

    def body(*refs):
        _adamw_math(*refs)

    spec = pl.BlockSpec((bm, c_), lambda i: (i, 0))
    return pl.pallas_call(
        body, name=name, grid=(r // bm,), in_specs=[spec] * 4, out_specs=[spec] * 3,
        out_shape=[jax.ShapeDtypeStruct((r, c_), F32)] * 3,
        compiler_params=pltpu.CompilerParams(dimension_semantics=("parallel",)),
    )(w, g, m, v)


_WEIGHTS = ["a_norm", "a_w_in", "a_w_out", "b_norm", "b_w_in", "b_v_ln_g", "b_v_ln_b", "b_w_s", "b_b_s", "b_w_out",
            "c_norm", "c_w_in", "c_conv_w", "c_conv_b", "c_ln_g", "c_ln_b", "c_w_out", "d_norm", "d_w_in", "d_b_f",
            "d_w_out", "final_norm"]
_SHARD_AXIS = {"a_norm": None, "a_w_in": 2, "a_w_out": 1, "b_norm": 1, "b_w_in": 2, "b_v_ln_g": 1, "b_v_ln_b": 1, "b_w_s": None,
               "b_b_s": None, "b_w_out": 1, "c_norm": 1, "c_w_in": 2, "c_conv_w": 2, "c_conv_b": 1, "c_ln_g": 1, "c_ln_b": 1,
               "c_w_out": 1, "d_norm": 1, "d_w_in": 2, "d_b_f": None, "d_w_out": 1, "final_norm": None}
_BIG = ["a_w_in", "a_w_out", "b_w_in", "b_w_out", "c_w_in", "c_w_out", "d_w_in", "d_w_out"]
_GATHER_GROUPS = (("a_w_in", "a_w_out"), ("b_w_in", "b_w_out"), ("c_w_in", "c_w_out", "d_w_in", "d_w_out"))
_SMALL_SHARDED = [n for n in _WEIGHTS if _SHARD_AXIS[n] is not None and n not in _BIG]
_REPLICATED = [n for n in _WEIGHTS if _SHARD_AXIS[n] is None]
_ROW_ALIGN = 32
_ROW_ALIGN_SUMMED = 128


def _pack(pieces, dtype, align=_ROW_ALIGN):
    flat = jnp.concatenate([p.reshape(-1).astype(dtype) for p in pieces])
    unit = align * PACK_C
    total = -(-flat.shape[0] // unit) * unit
    return jnp.pad(flat, (0, total - flat.shape[0])).reshape(total // PACK_C, PACK_C)


def _unpack(flat, shapes):
    out, off = [], 0
    for s in shapes:
        n = math.prod(s)
        out.append(flat[off:off + n].reshape(s))
        off += n
    return out


def _full_shape(local_shape, axis):
    s = list(local_shape)
    if axis is not None:
        s[axis] *= N_CHIPS
    return tuple(s)


def _gather_weights(local):
    def whole(n, gt):
        if _SHARD_AXIS[n] == 1:
            return gt.reshape(-1, gt.shape[-1])
        if n == "d_w_in":
            return gt.transpose(1, 0, 2).reshape(gt.shape[1], -1)
        return gt

    full = {n: local[n][0] if n != "final_norm" else local[n] for n in _REPLICATED}
    first = list(_GATHER_GROUPS[0])
    mine = [local[n][0].astype(BF16) for n in first] + [_pack([local[n] for n in _SMALL_SHARDED], F32)]
    got = [_own_block(gt, s) for gt, s in zip(_allgather_chips(mine, name="gather_weights"), mine)]
    full.update({n: whole(n, gt) for n, gt in zip(first, got)})
    small = got[-1].reshape(N_CHIPS, -1)
    shards = [_unpack(small[j], [local[n].shape[1:] for n in _SMALL_SHARDED]) for j in range(N_CHIPS)]
    for i, n in enumerate(_SMALL_SHARDED):
        full[n] = jnp.concatenate([shards[j][i] for j in range(N_CHIPS)], axis=_SHARD_AXIS[n] - 1)

    def begin(k, after):
        shards_k = [local[n][0].astype(BF16) for n in _GATHER_GROUPS[k]]
        started, token = _gather_start(shards_k, after, name=f"gather{k}_start")
        return shards_k, started, token

    pending = [begin(1, got[0])]
    full["a_norm"] = full["a_norm"] + pending[0][2][0, 0]

    def finish(k):
        def weights(after):
            shards_k, started, _ = pending[k - 1]
            lands = _gather_wait(started, after, name=f"gather{k}_wait")
            token = None
            if k + 1 < len(_GATHER_GROUPS):
                pending.append(begin(k + 1, lands[0]))
                token = pending[k][2]
            lands = _sibling_exchange(lands, name=f"gather{k}_exchange")
            out = {n: whole(n, _own_block(gt, s)) for n, gt, s in zip(_GATHER_GROUPS[k], lands, shards_k)}
            if token is not None:
                gain = _GATHER_GROUPS[k][0][0] + "_norm"
                out[gain] = full[gain] + token[0, 0]
            return out
        return weights

    return full, [finish(k) for k in range(1, len(_GATHER_GROUPS))]


def _repl_piece_len(local):
    total = sum(math.prod(local[n].shape) for n in _REPLICATED)
    return -(-total // N_CHIPS)


def _reduce_grads(g, local, early):
    rep_flat = jnp.concatenate([g[n].reshape(-1) for n in _REPLICATED])
    piece = _repl_piece_len(local)
    rep_flat = jnp.pad(rep_flat, (0, N_CHIPS * piece - rep_flat.shape[0]))

    def shard(n, j):
        full = g[n].reshape(_full_shape(local[n].shape, _SHARD_AXIS[n]))
        width = local[n].shape[_SHARD_AXIS[n]]
        return lax.slice_in_dim(full, j * width, (j + 1) * width, axis=_SHARD_AXIS[n])

    small = jnp.stack([_pack([shard(n, j) for n in _SMALL_SHARDED] + [rep_flat[j * piece:(j + 1) * piece]], F32)
                       for j in range(N_CHIPS)])
    early_names = [n for _, names, _, _ in early for n in names]
    late = [n for n in _BIG if n not in early_names]
    hps = _reduce_to_chips([g[n] for n in late] + [small], [BF16] * len(late) + [F32], tag="grads")
    rbs = list(_scatter_chips([hw for _, hw in hps], name="grads_scatter_chips"))
    early_halves, early_rbs = [], []
    for tag, _, halves_k, started in early:
        early_halves += halves_k
        early_rbs += _scatter_wait(started, rbs[0], name=f"{tag}_scatter_wait")
    halves = early_halves + [hf for hf, _ in hps]
    fs = [_add_chips(hf, rb, name=f"grads_add_chips{i}") for i, (hf, rb) in enumerate(zip(halves, early_rbs + rbs))]
    summed = _join_halves(fs, name="grads_join_halves")
    red = {n: s.reshape(local[n].shape) for n, s in zip(early_names + late, summed)}
    out = _unpack(summed[-1].reshape(-1), [local[n].shape for n in _SMALL_SHARDED] + [(piece,)])
    red.update(zip(_SMALL_SHARDED, out[:-1]))
    rep_mine = _pack([out[-1]], F32)
    rep = _own_block(_allgather_chips([rep_mine], name="gather_replicated_grads")[0], rep_mine)
    rep = rep.reshape(N_CHIPS, -1)[:, :piece].reshape(-1)
    for n, val in zip(_REPLICATED, _unpack(rep, [local[n].shape for n in _REPLICATED])):
        red[n] = val
    return red


def _update(local, grads, m, v):
    delta, new_m, new_v = {}, {}, {}
    for n in _BIG:
        shp = local[n].shape
        two = (shp[-2], shp[-1])
        res = _adamw(local[n].reshape(two), grads[n].reshape(two), m[n].reshape(two), v[n].reshape(two), name=f"adamw_{n}")
        delta[n], new_m[n], new_v[n] = [r.reshape(shp) for r in res]
    small = [n for n in _WEIGHTS if n not in _BIG]
    two = {n: (math.prod(local[n].shape[:-1]), local[n].shape[-1]) for n in small}
    res = _adamw_many([[src[n].reshape(two[n]) for n in small] for src in (local, grads, m, v)], name="adamw_small")
    for dst, rs in zip((delta, new_m, new_v), res):
        for n, val in zip(small, rs):
            dst[n] = val.reshape(local[n].shape)
    return delta, new_m, new_v


def kernel(x, a_norm, a_w_in, a_w_out, b_norm, b_w_in, b_v_ln_g, b_v_ln_b, b_w_s, b_b_s, b_w_out, c_norm, c_w_in, c_conv_w, c_conv_b, c_ln_g, c_ln_b, c_w_out, d_norm, d_w_in, d_b_f, d_w_out, final_norm, loss_target, m_a_norm, m_a_w_in, m_a_w_out, m_b_norm, m_b_w_in, m_b_v_ln_g, m_b_v_ln_b, m_b_w_s, m_b_b_s, m_b_w_out, m_c_norm, m_c_w_in, m_c_conv_w, m_c_conv_b, m_c_ln_g, m_c_ln_b, m_c_w_out, m_d_norm, m_d_w_in, m_d_b_f, m_d_w_out, m_final_norm, v_a_norm, v_a_w_in, v_a_w_out, v_b_norm, v_b_w_in, v_b_v_ln_g, v_b_v_ln_b, v_b_w_s, v_b_b_s, v_b_w_out, v_c_norm, v_c_w_in, v_c_conv_w, v_c_conv_b, v_c_ln_g, v_c_ln_b, v_c_w_out, v_d_norm, v_d_w_in, v_d_b_f, v_d_w_out, v_final_norm):
    local = dict(zip(_WEIGHTS, (a_norm, a_w_in, a_w_out, b_norm, b_w_in, b_v_ln_g, b_v_ln_b, b_w_s, b_b_s, b_w_out, c_norm, c_w_in,
                                c_conv_w, c_conv_b, c_ln_g, c_ln_b, c_w_out, d_norm, d_w_in, d_b_f, d_w_out, final_norm)))
    m = dict(zip(_WEIGHTS, (m_a_norm, m_a_w_in, m_a_w_out, m_b_norm, m_b_w_in, m_b_v_ln_g, m_b_v_ln_b, m_b_w_s, m_b_b_s, m_b_w_out,
                            m_c_norm, m_c_w_in, m_c_conv_w, m_c_conv_b, m_c_ln_g, m_c_ln_b, m_c_w_out, m_d_norm, m_d_w_in, m_d_b_f,
                            m_d_w_out, m_final_norm)))
    v = dict(zip(_WEIGHTS, (v_a_norm, v_a_w_in, v_a_w_out, v_b_norm, v_b_w_in, v_b_v_ln_g, v_b_v_ln_b, v_b_w_s, v_b_b_s, v_b_w_out,
                            v_c_norm, v_c_w_in, v_c_conv_w, v_c_conv_b, v_c_ln_g, v_c_ln_b, v_c_w_out, v_d_norm, v_d_w_in, v_d_b_f,
                            v_d_w_out, v_final_norm)))
    loss_part, grad_x, g, early = _local_step(x, loss_target, *_gather_weights(local), _start_reduce)
    loss = lax.psum(loss_part, ("x", "y", "c"))
    grads = _reduce_grads(g, local, early)
    delta, new_m, new_v = _update(local, grads, m, v)
    return (loss, grad_x, *[grads[n] for n in _WEIGHTS], *[delta[n] for n in _WEIGHTS],
            *[new_m[n] for n in _WEIGHTS], *[new_v[n] for n in _WEIGHTS])
```

```python
import functools
import math

import jax
import jax.numpy as jnp
from jax import lax
from jax.experimental import pallas as pl
from jax.experimental.pallas import tpu as pltpu

F32, BF16 = jnp.float32, jnp.bfloat16
MESH = pl.DeviceIdType.MESH

D_MODEL = 1024
HEADS = 16
HEAD_DIM = 64
BLK = 128
PAIRS = HEADS // 2
GM_W = 2048
GM_G = 16
CV_W = 2048
CV_K = 31
HALO = 32
EPS = 1e-6
N_CHIPS = 4
PACK_C = 1024
ADAM_LR, ADAM_B1, ADAM_B2, ADAM_EPS, ADAM_WD, ADAM_STEP = 0.001, 0.9, 0.999, 1e-08, 0.01, 10

_NT = (((1,), (1,)), ((), ()))
_TN = (((0,), (0,)), ((), ()))
_NN = (((1,), (0,)), ((), ()))


def _dot(a, b, dims=_NN):
    return lax.dot_general(a, b, dims, preferred_element_type=F32)


def _split3(x):
    hi = x.astype(BF16)
    r = x - hi.astype(F32)
    mid = r.astype(BF16)
    lo = (r - mid.astype(F32)).astype(BF16)
    return hi, mid, lo


def _dot3_right(x, m):
    hi, mid, lo = _split3(x)
    return _dot(hi, m) + _dot(mid, m) + _dot(lo, m)


def _dot3_left(m, x):
    hi, mid, lo = _split3(x)
    return _dot(m, hi) + _dot(m, mid) + _dot(m, lo)


def _sigmoid(x):
    return 1.0 / (1.0 + jnp.exp(-x))


def _silu(x):
    return x * _sigmoid(x)


def _dsilu(x):
    s = _sigmoid(x)
    return s * (1.0 + x * (1.0 - s))


_GELU_C = math.sqrt(2.0 / math.pi)
_GELU_A = 0.044715


def _gelu(x):
    return 0.5 * x * (1.0 + jnp.tanh(_GELU_C * (x + _GELU_A * x * x * x)))


def _dgelu(x):
    t = jnp.tanh(_GELU_C * (x + _GELU_A * x * x * x))
    return 0.5 * (1.0 + t) + 0.5 * x * (1.0 - t * t) * _GELU_C * (1.0 + 3.0 * _GELU_A * x * x)


def _log_sigmoid(x):
    return jnp.minimum(x, 0.0) - jnp.log(1.0 + jnp.exp(-jnp.abs(x)))


def _rms_fwd(x, g):
    r = lax.rsqrt(jnp.mean(x * x, axis=-1, keepdims=True) + EPS)
    return x * r * g


def _rms_bwd(dy, x, g):
    r = lax.rsqrt(jnp.mean(x * x, axis=-1, keepdims=True) + EPS)
    xh = x * r
    dxh = dy * g
    dx = r * (dxh - xh * jnp.mean(dxh * xh, axis=-1, keepdims=True))
    return dx, dy * xh


def _ln_stats(x):
    mu = jnp.mean(x, axis=-1, keepdims=True)
    xc = x - mu
    r = lax.rsqrt(jnp.mean(xc * xc, axis=-1, keepdims=True) + EPS)
    return xc * r, r


def _ln_bwd(dy, xh, r, g):
    dxh = dy * g
    return r * (dxh - jnp.mean(dxh, axis=-1, keepdims=True) - xh * jnp.mean(dxh * xh, axis=-1, keepdims=True))


def _colsum(x):
    return jnp.sum(x, axis=0, keepdims=True)


def _tile(n, want):
    for t in range(min(n, want), 7, -1):
        if n % t == 0 and t % 8 == 0:
            return t
    return n


MM_TILE = 1024


def _matmul(a, b, *, name, mode="nn", residual=None, out_shards=1, out_dtype=F32):
    (m, k) = a.shape
    b_shards = b.shape[0] if b.ndim == 3 else 1
    if mode == "nn":
        n = b.shape[-1] * b_shards
        tn, tk = _tile(n // max(b_shards, out_shards), MM_TILE), _tile(k, MM_TILE)
    else:
        n = b.shape[-2]
        tn, tk = _tile(n // out_shards, MM_TILE), _tile(k // b_shards, MM_TILE)
    tm = _tile(m, MM_TILE)
    nk = k // tk
    a_spec = pl.BlockSpec((tm, tk), lambda i, j, kk: (i, kk))
    if mode == "nn":
        dims = _NN
        if b_shards == 1:
            b_spec = pl.BlockSpec((tk, tn), lambda i, j, kk: (kk, j))
        else:
            per_b = n // b_shards // tn
            b_spec = pl.BlockSpec((None, tk, tn), lambda i, j, kk: (j // per_b, kk, j % per_b))
    else:
        dims = _NT
        if b_shards == 1:
            b_spec = pl.BlockSpec((tn, tk), lambda i, j, kk: (j, kk))
        else:
            per_b = k // b_shards // tk
            b_spec = pl.BlockSpec((None, tn, tk), lambda i, j, kk: (kk // per_b, j, kk % per_b))
    if out_shards == 1:
        o_spec = pl.BlockSpec((tm, tn), lambda i, j, kk: (i, j))
        o_shape = (m, n)
    else:
        per_o = n // out_shards // tn
        o_spec = pl.BlockSpec((None, tm, tn), lambda i, j, kk: (j // per_o, i, j % per_o))
        o_shape = (out_shards, m, n // out_shards)
    has_res = residual is not None

    def body(a_ref, b_ref, *rest):
        o_ref = rest[-1]
        kk = pl.program_id(2)
        part = _dot(a_ref[...].astype(BF16), b_ref[...].astype(BF16), dims)
        if has_res:
            @pl.when(kk == 0)
            def _():
                o_ref[...] = part + rest[0][...]
        else:
            @pl.when(kk == 0)
            def _():
                o_ref[...] = part.astype(out_dtype)

        if nk > 1:
            @pl.when(kk > 0)
            def _():
                o_ref[...] += part

    assert out_dtype == F32 or nk == 1
    return pl.pallas_call(
        body, name=name, grid=(m // tm, n // tn, nk),
        in_specs=[a_spec, b_spec] + ([o_spec] if has_res else []),
        out_specs=o_spec, out_shape=jax.ShapeDtypeStruct(o_shape, out_dtype),
        compiler_params=pltpu.CompilerParams(dimension_semantics=("parallel", "parallel", "arbitrary")),
    )(a, b, *([residual] if has_res else []))


def _rows(fn, *, name, steps, ins, outs, accs=(), scratch=()):
    ni, no, na = len(ins), len(outs), len(accs)

    def body(*refs):
        in_refs, out_refs = refs[:ni], refs[ni:ni + no]
        acc_refs, scr = refs[ni + no:ni + no + na], refs[ni + no + na:]
        i = pl.program_id(0)

        @pl.when(i == 0)
        def _():
            for r in acc_refs:
                r[...] = jnp.zeros(r.shape, r.dtype)

        fn(i, in_refs, out_refs, acc_refs, scr)

    def full(shape):
        nd = len(shape)
        return pl.BlockSpec(tuple(shape), lambda i: (0,) * nd)

    res = pl.pallas_call(
        body, name=name, grid=(steps,),
        in_specs=[pl.BlockSpec(bs, im) for _, bs, im in ins],
        out_specs=[pl.BlockSpec(bs, im) for _, _, bs, im in outs] + [full(s) for s, _ in accs],
        out_shape=[jax.ShapeDtypeStruct(s, d) for s, d, _, _ in outs] + [jax.ShapeDtypeStruct(s, d) for s, d in accs],
        scratch_shapes=list(scratch),
        compiler_params=pltpu.CompilerParams(dimension_semantics=("arbitrary",)),
    )(*[a for a, _, _ in ins])
    return res


def _rb(arr, bm, cb=0, width=None):
    w = arr.shape[1] if width is None else width
    return (arr, (bm, w), lambda i: (i, cb))


def _const(arr):
    nd = arr.ndim
    return (arr, tuple(arr.shape), lambda i: (0,) * nd)


def _ro(t, w, dtype, bm):
    return ((t, w), dtype, (bm, w), lambda i: (i, 0))


def _rot(t, w, dtype, bm):
    return ((w, t), dtype, (w, bm), lambda i: (0, i))


def _rmsnorm(x, g, *, name, bm=512):
    t, d = x.shape
    bm = _tile(t, bm)

    def fn(i, ins, outs, accs, scr):
        h = _rms_fwd(ins[0][...], ins[1][...])
        outs[0][...] = h.astype(BF16)
        outs[1][...] = h.T.astype(BF16)

    return _rows(fn, name=name, steps=t // bm, ins=[_rb(x, bm), _const(g)], outs=[_ro(t, d, BF16, bm), _rot(t, d, BF16, bm)])


def _rmsnorm_bwd(dh, x, g, dres, *, name, bm=512):
    t, d = x.shape
    bm = _tile(t, bm)

    def fn(i, ins, outs, accs, scr):
        dx, dgrow = _rms_bwd(ins[0][...], ins[1][...], ins[2][...])
        outs[0][...] = ins[3][...] + dx
        accs[0][...] += _colsum(dgrow)

    return _rows(fn, name=name, steps=t // bm, ins=[_rb(dh, bm), _rb(x, bm), _const(g), _rb(dres, bm)],
                 outs=[_ro(t, d, F32, bm)], accs=[((1, d), F32)])


def _gate(o, p, gcb, *, name, bm=512):
    t, w = o.shape
    bm = _tile(t, bm)

    def fn(i, ins, outs, accs, scr):
        y = ins[0][...] * _silu(ins[1][...])
        outs[0][...] = y.astype(BF16)
        outs[1][...] = y.T.astype(BF16)

    return _rows(fn, name=name, steps=t // bm, ins=[_rb(o, bm), _rb(p, bm, gcb, w)],
                 outs=[_ro(t, w, BF16, bm), _rot(t, w, BF16, bm)])


def _gate_bwd(dy, o, p, gcb, *, name, bm=512):
    t, w = o.shape
    bm = _tile(t, bm)

    def fn(i, ins, outs, accs, scr):
        dy_, o_, g_ = ins[0][...], ins[1][...], ins[2][...]
        outs[0][...] = dy_ * _silu(g_)
        outs[1][...] = dy_ * o_ * _dsilu(g_)

    return _rows(fn, name=name, steps=t // bm, ins=[_rb(dy, bm), _rb(o, bm), _rb(p, bm, gcb, w)],
                 outs=[_ro(t, w, F32, bm), _ro(t, w, F32, bm)])


def _loss_head(x, g, tgt, *, name, bm=512):
    t, d = x.shape
    bm = _tile(t, bm)

    def fn(i, ins, outs, accs, scr):
        x_, g_, tg = ins[0][...], ins[1][...], ins[2][...]
        err = _rms_fwd(x_, g_) - tg
        part = 0.5 * jnp.sum(jnp.sum(err * err, axis=-1, keepdims=True), axis=0, keepdims=True) / d
        dx, dgrow = _rms_bwd(err / d, x_, g_)
        outs[0][...] = dx
        accs[0][...] += _colsum(dgrow)
        accs[1][...] += jnp.broadcast_to(part, (1, BLK))

    return _rows(fn, name=name, steps=t // bm, ins=[_rb(x, bm), _const(g), _rb(tgt, bm)],
                 outs=[_ro(t, d, F32, bm)], accs=[((1, d), F32), ((1, BLK), F32)])


def _gmlp_mix_weights(ws_ref, g):
    row = lax.broadcasted_iota(jnp.int32, (BLK, BLK), 0)
    col = lax.broadcasted_iota(jnp.int32, (BLK, BLK), 1)
    tril = col <= row
    return jnp.where(tril, ws_ref[g], 0.0), tril


def _gmlp_fwd(p, ln_g, ln_b, w_s, bs_t, *, name):
    t = p.shape[0]

    def fn(i, ins, outs, accs, scr):
        p_ref, lg, lb, ws_ref, bst = ins
        vn = _ln_stats(_gelu(p_ref[:, GM_W:2 * GM_W]))[0] * lg[...] + lb[...]
        for g in range(GM_G):
            cs = slice(g * BLK, (g + 1) * BLK)
            wt, _ = _gmlp_mix_weights(ws_ref, g)
            s = _dot(wt.astype(BF16), vn[:, cs].astype(BF16)) + bst[:, g:g + 1]
            u = _gelu(p_ref[:, cs])
            gate = p_ref[:, 2 * GM_W + g * BLK:2 * GM_W + (g + 1) * BLK]
            y = u * s * _silu(gate)
            outs[0][:, cs] = y.astype(BF16)
            outs[1][cs, :] = y.T.astype(BF16)

    return _rows(fn, name=name, steps=t // BLK, ins=[_rb(p, BLK), _const(ln_g), _const(ln_b), _const(w_s), _const(bs_t)],
                 outs=[_ro(t, GM_W, BF16, BLK), _rot(t, GM_W, BF16, BLK)])


def _gmlp_bwd(dy, p, ln_g, ln_b, w_s, bs_t, *, name):
    t = p.shape[0]

    def fn(i, ins, outs, accs, scr):
        dy_ref, p_ref, lg, lb, ws_ref, bst = ins
        dp_ref = outs[0]
        dlg, dlb, dws, dbst = accs
        dvn_ref = scr[0]
        v_pre = p_ref[:, GM_W:2 * GM_W]
        xh, r = _ln_stats(_gelu(v_pre))
        vn = xh * lg[...] + lb[...]
        for g in range(GM_G):
            cs = slice(g * BLK, (g + 1) * BLK)
            gs = slice(2 * GM_W + g * BLK, 2 * GM_W + (g + 1) * BLK)
            wt, tril = _gmlp_mix_weights(ws_ref, g)
            vg = vn[:, cs].astype(BF16)
            s = _dot(wt.astype(BF16), vg) + bst[:, g:g + 1]
            u_pre, gate, dyg = p_ref[:, cs], p_ref[:, gs], dy_ref[:, cs]
            u = _gelu(u_pre)
            dos = dyg * _silu(gate)
            dp_ref[:, gs] = dyg * u * s * _dsilu(gate)
            dp_ref[:, cs] = dos * s * _dgelu(u_pre)
            ds = (dos * u).astype(BF16)
            dws[g] += jnp.where(tril, _dot(ds, vg, _NT), 0.0)
            dbst[:, g:g + 1] += jnp.sum(dos * u, axis=1, keepdims=True)
            dvn_ref[:, cs] = _dot(wt.astype(BF16), ds, _TN)
        dvn = dvn_ref[...]
        dlg[...] += _colsum(dvn * xh)
        dlb[...] += _colsum(dvn)
        dp_ref[:, GM_W:2 * GM_W] = _ln_bwd(dvn, xh, r, lg[...]) * _dgelu(v_pre)

    return _rows(fn, name=name, steps=t // BLK,
                 ins=[_rb(dy, BLK), _rb(p, BLK), _const(ln_g), _const(ln_b), _const(w_s), _const(bs_t)],
                 outs=[_ro(t, 3 * GM_W, F32, BLK)],
                 accs=[((1, GM_W), F32), ((1, GM_W), F32), ((GM_G, BLK, BLK), F32), ((BLK, GM_G), F32)],
                 scratch=[pltpu.VMEM((BLK, GM_W), F32)])


CV_BM = 128
CV_RC = 8
SUBLANES = 8
CV_FWD_OFFS = [HALO - (CV_K - 1) + k for k in range(CV_K)]
CV_BWD_OFFS = [CV_K - 1 - k for k in range(CV_K)]


def _conv_halo_prev(p, cb, bm):
    per = bm // HALO
    return (p, (HALO, CV_W), lambda i: (jnp.maximum(i * per - 1, 0), cb))


def _conv_scratch(bm):
    return [pltpu.VMEM((bm + HALO, CV_W), F32), pltpu.VMEM((SUBLANES - 1, bm + HALO - SUBLANES, CV_W), F32),
            pltpu.VMEM((bm, CV_W), F32)]


def _conv_shift_copies(ext_ref, sh_ref):
    rows = sh_ref.shape[1]
    for b in range(1, SUBLANES):
        sh_ref[b - 1] = ext_ref[pl.ds(b, rows), :]


def _conv_window(ext_ref, sh_ref, off, r0, rows):
    b = off % SUBLANES
    src = ext_ref if b == 0 else sh_ref.at[b - 1]
    return src[pl.ds(r0 + (off - b), rows), :]


def _conv_taps(ext_ref, sh_ref, cw_ref, y_ref, offs):
    bm = y_ref.shape[0]

    def chunk(ci, c):
        r0 = pl.multiple_of(ci * CV_RC, CV_RC)
        acc = jnp.zeros((CV_RC, CV_W), F32)
        for k in range(CV_K):
            acc = acc + cw_ref[pl.ds(k * SUBLANES, CV_RC), :] * _conv_window(ext_ref, sh_ref, offs[k], r0, CV_RC)
        y_ref[pl.ds(r0, CV_RC), :] = acc
        return c

    lax.fori_loop(0, bm // CV_RC, chunk, 0)


def _conv_dweights(dy1_ref, ext_ref, sh_ref, dcw_ref):
    bm = dy1_ref.shape[0]
    groups = 4
    for k in range(CV_K):
        def step(ci, acc, off=CV_FWD_OFFS[k]):
            prods = []
            for u in range(groups):
                r0 = pl.multiple_of((ci * groups + u) * CV_RC, CV_RC)
                prods.append(dy1_ref[pl.ds(r0, CV_RC), :] * _conv_window(ext_ref, sh_ref, off, r0, CV_RC))
            return acc + ((prods[0] + prods[1]) + (prods[2] + prods[3]))

        dcw_ref[k:k + 1, :] += _colsum(lax.fori_loop(0, bm // (CV_RC * groups), step, jnp.zeros((CV_RC, CV_W), F32)))


def _conv_fill(i, ext_ref, a_prev, b_prev, a, b, bm, seq):
    keep = jnp.where((i % (seq // bm)) == 0, 0.0, 1.0)
    ext_ref[pl.ds(0, HALO), :] = keep * (a_prev * _sigmoid(b_prev))
    ext_ref[pl.ds(HALO, bm), :] = a * _sigmoid(b)


def _conv_fwd(p, cw, cb, ln_g, ln_b, seq, *, name, bm=CV_BM):
    t = p.shape[0]

    def fn(i, ins, outs, accs, scr):
        a, b, gate, ap, bp = [r[...] for r in ins[:5]]
        cw_ref, cb_, lg, lb = ins[5], ins[6][...], ins[7][...], ins[8][...]
        ext, sh, y = scr
        _conv_fill(i, ext, ap, bp, a, b, bm, seq)
        _conv_shift_copies(ext, sh)
        _conv_taps(ext, sh, cw_ref, y, CV_FWD_OFFS)
        y2 = _ln_stats(y[...] + cb_)[0] * lg + lb
        out = _silu(y2) * _silu(gate)
        outs[0][...] = out.astype(BF16)
        outs[1][...] = out.T.astype(BF16)

    return _rows(fn, name=name, steps=t // bm,
                 ins=[_rb(p, bm, 0, CV_W), _rb(p, bm, 1, CV_W), _rb(p, bm, 2, CV_W),
                      _conv_halo_prev(p, 0, bm), _conv_halo_prev(p, 1, bm),
                      _const(cw), _const(cb), _const(ln_g), _const(ln_b)],
                 outs=[_ro(t, CV_W, BF16, bm), _rot(t, CV_W, BF16, bm)], scratch=_conv_scratch(bm))


def _conv_bwd_post(dy, p, cw, cb, ln_g, ln_b, seq, *, name, bm=CV_BM):
    t = p.shape[0]

    def fn(i, ins, outs, accs, scr):
        dy_, a, b, gate, ap, bp = [r[...] for r in ins[:6]]
        cw_ref, cb_, lg, lb = ins[6], ins[7][...], ins[8][...], ins[9][...]
        dlg, dlb, dcb, dcw = accs
        ext, sh, y = scr
        _conv_fill(i, ext, ap, bp, a, b, bm, seq)
        _conv_shift_copies(ext, sh)
        _conv_taps(ext, sh, cw_ref, y, CV_FWD_OFFS)
        xh, r = _ln_stats(y[...] + cb_)
        y2 = xh * lg + lb
        outs[1][...] = dy_ * _silu(y2) * _dsilu(gate)
        dy2 = dy_ * _silu(gate) * _dsilu(y2)
        dlg[...] += _colsum(dy2 * xh)
        dlb[...] += _colsum(dy2)
        dy1 = _ln_bwd(dy2, xh, r, lg)
        outs[0][...] = dy1
        dcb[...] += _colsum(dy1)
        _conv_dweights(outs[0], ext, sh, dcw)

    return _rows(fn, name=name, steps=t // bm,
                 ins=[_rb(dy, bm), _rb(p, bm, 0, CV_W), _rb(p, bm, 1, CV_W), _rb(p, bm, 2, CV_W),
                      _conv_halo_prev(p, 0, bm), _conv_halo_prev(p, 1, bm),
                      _const(cw), _const(cb), _const(ln_g), _const(ln_b)],
                 outs=[_ro(t, CV_W, F32, bm), _ro(t, CV_W, F32, bm)],
                 accs=[((1, CV_W), F32), ((1, CV_W), F32), ((1, CV_W), F32), ((CV_K, CV_W), F32)],
                 scratch=_conv_scratch(bm))


def _conv_bwd_pre(dy1, dgate, p, cw, seq, *, name, bm=CV_BM):
    t = p.shape[0]
    per = bm // HALO
    last_halo = t // HALO - 1

    def fn(i, ins, outs, accs, scr):
        d1, d1n, dg, a, b = [r[...] for r in ins[:5]]
        ext, sh, y = scr
        keep = jnp.where((i % (seq // bm)) == (seq // bm - 1), 0.0, 1.0)
        ext[pl.ds(0, bm), :] = d1
        ext[pl.ds(bm, HALO), :] = keep * d1n
        _conv_shift_copies(ext, sh)
        _conv_taps(ext, sh, ins[5], y, CV_BWD_OFFS)
        dy0 = y[...]
        sb = _sigmoid(b)
        outs[0][:, 0:CV_W] = dy0 * sb
        outs[0][:, CV_W:2 * CV_W] = dy0 * a * sb * (1.0 - sb)
        outs[0][:, 2 * CV_W:3 * CV_W] = dg

    return _rows(fn, name=name, steps=t // bm,
                 ins=[_rb(dy1, bm), (dy1, (HALO, CV_W), lambda i: (jnp.minimum((i + 1) * per, last_halo), 0)),
                      _rb(dgate, bm), _rb(p, bm, 0, CV_W), _rb(p, bm, 1, CV_W), _const(cw)],
                 outs=[_ro(t, 3 * CV_W, F32, bm)], scratch=_conv_scratch(bm))[0]


def _iotas():
    row = lax.broadcasted_iota(jnp.int32, (BLK, BLK), 0)
    col = lax.broadcasted_iota(jnp.int32, (BLK, BLK), 1)
    return row, col


def _heads(x, head0):
    if head0.shape != x.shape:
        head0 = lax.broadcasted_iota(jnp.int32, x.shape, 1) < HEAD_DIM
    return jnp.where(head0, x, 0.0).astype(BF16), jnp.where(head0, 0.0, x).astype(BF16)


def _pair_spec(seq, off):
    return pl.BlockSpec((seq, BLK), lambda b, hp: (b, off + hp))


def _stat_spec(seq):
    return pl.BlockSpec((None, None, seq, BLK), lambda b, hp: (b, hp, 0, 0))


_ATT_PARAMS = dict(compiler_params=pltpu.CompilerParams(dimension_semantics=("parallel", "parallel")))
_SCALE = 1.0 / math.sqrt(HEAD_DIM)


Q_BLOCK = 256
KEY_BLOCK = 256


def _stack_heads(x, head0, scale=None):
    if scale is not None:
        x = x * scale
    return jnp.concatenate(_heads(x, head0), axis=0)


def _pair_cols(x, head0, fill):
    a = jnp.max(jnp.where(head0, x, fill), axis=1, keepdims=True)
    b = jnp.max(jnp.where(head0, fill, x), axis=1, keepdims=True)
    return jnp.concatenate([a, b], axis=0)


def _causal_mask(t0, s0, tq, kw, inclusive):
    row = lax.broadcasted_iota(jnp.int32, (2 * tq, kw), 0) & (tq - 1)
    col = lax.broadcasted_iota(jnp.int32, (2 * tq, kw), 1)
    return (s0 + col) <= (t0 + row) if inclusive else (s0 + col) < (t0 + row)


def _sub(x, j):
    return x[:, j * BLK:(j + 1) * BLK]


def _tri_blocks(kw, relation):
    r = lax.broadcasted_iota(jnp.int32, (kw, kw), 0)
    c = lax.broadcasted_iota(jnp.int32, (kw, kw), 1)
    return (((r // BLK) == (c // BLK)) & relation(r, c)).astype(BF16)


def _block_cumsum(x, tri, ksub):
    hi = x.astype(BF16)
    lo = (x - hi.astype(F32)).astype(BF16)
    cs = _dot(jnp.concatenate([hi, lo], axis=0), tri)
    n = x.shape[0]
    cs = cs[:n] + cs[n:]
    return [_sub(cs, j) for j in range(ksub)], [jnp.sum(_sub(x, j), axis=1, keepdims=True) for j in range(ksub)]


def _sb_terms(qs, k, mask):
    return _sb_terms_z(_dot(qs, k, _NT), mask)


def _sb_terms_z(z, mask):
    t = jnp.log(1.0 + jnp.exp(-jnp.abs(z)))
    lsz = jnp.minimum(z, 0.0) - t
    lr = lsz - z
    if mask is not None:
        lr = jnp.where(mask, lr, 0.0)
    return lsz, lr


def _sb_fwd(p, nb, seq, *, name):
    tq = min(Q_BLOCK, seq)
    nq = seq // tq
    kw = min(KEY_BLOCK, seq)
    ksub = kw // BLK

    def body(q_ref, k_ref, v_ref, o_ref, tot_ref):
        row, col = _iotas()
        colq = lax.broadcasted_iota(jnp.int32, (tq, BLK), 1)
        head0 = colq < HEAD_DIM
        upper = _tri_blocks(kw, lambda j, s: j > s)

        def qblock(qb, c):
            t0 = pl.multiple_of(qb * tq, tq)
            qs = _stack_heads(q_ref[pl.ds(t0, tq), :], head0, _SCALE)
            diag = (t0 + tq - 1) // kw

            def kblock(kb, carry, masked):
                acc, run = carry
                s0 = pl.multiple_of(kb * kw, kw)
                k = k_ref[pl.ds(s0, kw), :].astype(BF16)
                v0, v1 = _heads(v_ref[pl.ds(s0, kw), :], head0)
                mask = _causal_mask(t0, s0, tq, kw, False)[:tq] if masked else None
                zs = [_dot(qs[h * tq:(h + 1) * tq], k, _NT) for h in range(2)]
                terms = []
                for h in range(2):
                    lsz, lr = _sb_terms_z(zs[h], mask)
                    terms.append((lsz,) + _block_cumsum(lr, upper, ksub))
                runs = []
                for h, vh in enumerate((v0, v1)):
                    lsz, after, total = terms[h]
                    r = run[h]
                    ws = [None] * ksub
                    for j in reversed(range(ksub)):
                        w = jnp.exp(_sub(lsz, j) + after[j] + r)
                        if masked:
                            w = jnp.where(_sub(mask, j), w, 0.0)
                        ws[j] = w.astype(BF16)
                        r = r + total[j]
                    acc = acc + _dot(jnp.concatenate(ws, axis=1), vh)
                    runs.append(r)
                return acc, tuple(runs)

            zc = jnp.zeros((tq, 1), F32)
            carry = kblock(diag, (jnp.zeros((tq, BLK), F32), (zc, zc)), True)
            acc, run = lax.fori_loop(0, diag, lambda it, cr: kblock(diag - 1 - it, cr, False), carry)
            o_ref[pl.ds(t0, tq), :] = acc
            tot_ref[pl.ds(t0, tq), :] = jnp.where(head0, run[0], run[1])
            return c

        lax.fori_loop(0, nq, qblock, 0)

    return pl.pallas_call(
        body, name=name, grid=(nb, PAIRS),
        in_specs=[_pair_spec(seq, 0), _pair_spec(seq, PAIRS), _pair_spec(seq, 2 * PAIRS)],
        out_specs=[_pair_spec(seq, 0), _stat_spec(seq)],
        out_shape=[jax.ShapeDtypeStruct((nb * seq, D_MODEL), F32), jax.ShapeDtypeStruct((nb, PAIRS, seq, BLK), F32)],
        **_ATT_PARAMS,
    )(p, p, p)


def _sb_bwd(p, do, tot, nb, seq, *, name):
    tq = min(Q_BLOCK, seq)
    nq = seq // tq
    kw = min(KEY_BLOCK, seq)
    ksub = kw // BLK

    def body(q_ref, k_ref, v_ref, do_ref, tot_ref, dq_ref, dk_ref, dv_ref):
        row, col = _iotas()
        colq = lax.broadcasted_iota(jnp.int32, (tq, BLK), 1)
        head0 = colq < HEAD_DIM
        lower_incl = _tri_blocks(kw, lambda j, s: j <= s)
        lower_strict = _tri_blocks(kw, lambda s, j: s < j)
        dk_ref[...] = jnp.zeros(dk_ref.shape, F32)
        dv_ref[...] = jnp.zeros(dv_ref.shape, F32)

        def qblock(qb, c):
            t0 = pl.multiple_of(qb * tq, tq)
            qs = _stack_heads(q_ref[pl.ds(t0, tq), :], head0, _SCALE)
            dos = _stack_heads(do_ref[pl.ds(t0, tq), :], head0)
            tot = tot_ref[pl.ds(t0, tq), :]
            swapped = pltpu.roll(tot, HEAD_DIM, 1)
            tts = (jnp.where(head0, tot, swapped), jnp.where(head0, swapped, tot))
            diag = (t0 + tq - 1) // kw

            def kblock(kb, carry, masked):
                dq, pfs, efs = carry
                s0 = pl.multiple_of(kb * kw, kw)
                kf = k_ref[pl.ds(s0, kw), :]
                k = kf.astype(BF16)
                khs = _heads(kf, head0)
                v = v_ref[pl.ds(s0, kw), :].astype(BF16)
                mask = _causal_mask(t0, s0, tq, kw, False)[:tq] if masked else None
                zs = [_dot(qs[h * tq:(h + 1) * tq], k, _NT) for h in range(2)]
                dws = [_dot(dos[h * tq:(h + 1) * tq], v, _NT) for h in range(2)]
                first = []
                for h in range(2):
                    lsz, lr = _sb_terms_z(zs[h], None)
                    lrm = jnp.where(mask, lr, 0.0) if masked else lr
                    first.append((lsz, lr) + _block_cumsum(lrm, lower_incl, ksub))
                second, pfs_out = [], []
                for h in range(2):
                    lsz, lr, incl, total = first[h]
                    pf = pfs[h]
                    ws, ews = [], []
                    for j in range(ksub):
                        w = jnp.exp(_sub(lsz, j) + (tts[h] - pf - incl[j]))
                        if masked:
                            w = jnp.where(_sub(mask, j), w, 0.0)
                        pf = pf + total[j]
                        ws.append(w.astype(BF16))
                        ews.append(_sub(dws[h], j) * w)
                    pfs_out.append(pf)
                    second.append((ws, ews) + _block_cumsum(jnp.concatenate(ews, axis=1), lower_strict, ksub))
                dz_h, efs_out = [], []
                for h in range(2):
                    lsz, lr = first[h][:2]
                    ws, ews, before, etotal = second[h]
                    ef = efs[h]
                    dzs = []
                    for j in range(ksub):
                        dz = ews[j] * jnp.exp(_sub(lr, j)) - (ef + before[j]) * jnp.exp(_sub(lsz, j))
                        ef = ef + etotal[j]
                        if masked:
                            dz = jnp.where(_sub(mask, j), dz, 0.0)
                        dzs.append(dz.astype(BF16))
                    efs_out.append(ef)
                    dz_h.append(jnp.concatenate(dzs, axis=1))
                    dq = dq + _dot(dz_h[h], khs[h])
                w = jnp.concatenate([jnp.concatenate(second[h][0], axis=1) for h in range(2)], axis=0)
                dk_ref[pl.ds(s0, kw), :] += _dot(jnp.concatenate(dz_h, axis=0), qs, _TN)
                dv_ref[pl.ds(s0, kw), :] += _dot(w, dos, _TN)
                return dq, tuple(pfs_out), tuple(efs_out)

            zc = jnp.zeros((tq, 1), F32)
            carry = lax.fori_loop(0, diag, lambda kb, cr: kblock(kb, cr, False), (jnp.zeros((tq, BLK), F32), (zc, zc), (zc, zc)))
            dq_ref[pl.ds(t0, tq), :] = kblock(diag, carry, True)[0] * _SCALE
            return c

        lax.fori_loop(0, nq, qblock, 0)

    t = nb * seq
    return pl.pallas_call(
        body, name=name, grid=(nb, PAIRS),
        in_specs=[_pair_spec(seq, 0), _pair_spec(seq, PAIRS), _pair_spec(seq, 2 * PAIRS), _pair_spec(seq, 0), _stat_spec(seq)],
        out_specs=[_pair_spec(seq, 0)] * 3,
        out_shape=[jax.ShapeDtypeStruct((t, D_MODEL), F32)] * 3,
        **_ATT_PARAMS,
    )(p, p, p, do, tot)


def _fox_cum(f, bf, nb, seq, *, name):
    def body(f_ref, bf_ref, cc_ref, cr_ref):
        row, col = _iotas()
        lower = (col <= row).astype(BF16)
        carry = jnp.zeros((1, BLK), F32)
        for blk in range(seq // BLK):
            rs = slice(blk * BLK, (blk + 1) * BLK)
            lf = jnp.where(col < HEADS, _log_sigmoid(f_ref[rs, :] + bf_ref[...]), 0.0)
            cc = _dot3_left(lower, lf) + carry
            cc_ref[rs, :] = cc
            cr_ref[:, rs] = cc.T[0:HEADS, :]
            carry = carry + _colsum(lf)

    return pl.pallas_call(
        body, name=name, grid=(nb,),
        in_specs=[pl.BlockSpec((seq, BLK), lambda b: (b, 0)), pl.BlockSpec((1, BLK), lambda b: (0, 0))],
        out_specs=[pl.BlockSpec((seq, BLK), lambda b: (b, 0)), pl.BlockSpec((None, HEADS, seq), lambda b: (b, 0, 0))],
        out_shape=[jax.ShapeDtypeStruct((nb * seq, BLK), F32), jax.ShapeDtypeStruct((nb, HEADS, seq), F32)],
        compiler_params=pltpu.CompilerParams(dimension_semantics=("parallel",)),
    )(f, bf)


def _fox_cum_bwd(dcr, dcc, f, bf, nb, seq, *, name):
    def body(dcr_ref, dcc_ref, f_ref, bf_ref, df_ref, dbf_ref):
        row, col = _iotas()
        upper_incl = (col >= row).astype(BF16)

        @pl.when(pl.program_id(0) == 0)
        def _():
            dbf_ref[...] = jnp.zeros((1, BLK), F32)

        carry = jnp.zeros((1, BLK), F32)
        for blk in reversed(range(seq // BLK)):
            rs = slice(blk * BLK, (blk + 1) * BLK)
            dc = dcr_ref[:, rs].T + dcc_ref[rs, :]
            dlf = _dot3_left(upper_incl, dc) + carry
            carry = carry + _colsum(dc)
            fl = f_ref[rs, :] + bf_ref[...]
            df = jnp.where(col < HEADS, dlf * _sigmoid(-fl), 0.0)
            df_ref[rs, :] = df
            dbf_ref[...] += _colsum(df)

    return pl.pallas_call(
        body, name=name, grid=(nb,),
        in_specs=[pl.BlockSpec((None, BLK, seq), lambda b: (b, 0, 0)), pl.BlockSpec((seq, BLK), lambda b: (b, 0)),
                  pl.BlockSpec((seq, BLK), lambda b: (b, 0)), pl.BlockSpec((1, BLK), lambda b: (0, 0))],
        out_specs=[pl.BlockSpec((seq, BLK), lambda b: (b, 0)), pl.BlockSpec((1, BLK), lambda b: (0, 0))],
        out_shape=[jax.ShapeDtypeStruct((nb * seq, BLK), F32), jax.ShapeDtypeStruct((1, BLK), F32)],
        compiler_params=pltpu.CompilerParams(dimension_semantics=("arbitrary",)),
    )(dcr, dcc, f, bf)


def _fox_cum_cols(cc_ref, t0, tq, colq, hp):
    cc = cc_ref[pl.ds(t0, tq), :]
    c0 = jnp.sum(jnp.where(colq == 2 * hp, cc, 0.0), axis=1, keepdims=True)
    c1 = jnp.sum(jnp.where(colq == 2 * hp + 1, cc, 0.0), axis=1, keepdims=True)
    return c0, c1


def _fox_bias(c0, c1, cr_ref, s0, kw):
    return jnp.concatenate([c0 - cr_ref[0:1, pl.ds(s0, kw)], c1 - cr_ref[1:2, pl.ds(s0, kw)]], axis=0)


def _fox_fwd(p, cc, cr, nb, seq, *, name):
    tq = min(Q_BLOCK, seq)
    nq = seq // tq
    kw = min(KEY_BLOCK, seq)
    ksub = kw // BLK

    def body(q_ref, k_ref, v_ref, cc_ref, cr_ref, o_ref, lse_ref):
        hp = pl.program_id(1)
        row, col = _iotas()
        colq = lax.broadcasted_iota(jnp.int32, (tq, BLK), 1)
        head0 = colq < HEAD_DIM

        def qblock(qb, c):
            t0 = pl.multiple_of(qb * tq, tq)
            qs = _stack_heads(q_ref[pl.ds(t0, tq), :], head0, _SCALE)
            c0, c1 = _fox_cum_cols(cc_ref, t0, tq, colq, hp)
            diag = (t0 + tq - 1) // kw

            def kblock(kb, carry, masked):
                accs, ms = carry
                s0 = pl.multiple_of(kb * kw, kw)
                k = k_ref[pl.ds(s0, kw), :].astype(BF16)
                vf = v_ref[pl.ds(s0, kw), :]
                own0 = lax.broadcasted_iota(jnp.int32, vf.shape, 1) < HEAD_DIM
                vs = (jnp.where(own0, vf, 1.0).astype(BF16), jnp.where(own0, 1.0, vf).astype(BF16))
                mask = _causal_mask(t0, s0, tq, kw, True)[:tq] if masked else None
                zs = [_dot(qs[h * tq:(h + 1) * tq], k, _NT) for h in range(2)]
                parts = []
                for h, ch in enumerate((c0, c1)):
                    s = zs[h] + (ch - cr_ref[h:h + 1, pl.ds(s0, kw)])
                    if masked:
                        s = jnp.where(mask, s, -jnp.inf)
                    m_new = jnp.maximum(ms[h], jnp.max(s, axis=1, keepdims=True))
                    parts.append((jnp.exp(s - m_new).astype(BF16), jnp.exp(ms[h] - m_new), m_new))
                return (tuple(accs[h] * parts[h][1] + _dot(parts[h][0], vs[h]) for h in range(2)),
                        tuple(parts[h][2] for h in range(2)))

            zeros, ninf = jnp.zeros((tq, BLK), F32), jnp.full((tq, 1), -jnp.inf, F32)
            carry = lax.fori_loop(0, diag, lambda kb, cr: kblock(kb, cr, False), ((zeros, zeros), (ninf, ninf)))
            (acc0, acc1), (m0, m1) = kblock(diag, carry, True)
            l = jnp.where(head0, pltpu.roll(acc0, HEAD_DIM, 1), pltpu.roll(acc1, HEAD_DIM, 1))
            o_ref[pl.ds(t0, tq), :] = jnp.where(head0, acc0, acc1) / l
            lse_ref[pl.ds(t0, tq), :] = jnp.where(head0, m0, m1) + jnp.log(l)
            return c

        lax.fori_loop(0, nq, qblock, 0)

    return pl.pallas_call(
        body, name=name, grid=(nb, PAIRS),
        in_specs=[_pair_spec(seq, 0), _pair_spec(seq, PAIRS), _pair_spec(seq, 2 * PAIRS),
                  pl.BlockSpec((seq, BLK), lambda b, hp: (b, 0)), pl.BlockSpec((None, None, 8, seq), lambda b, hp: (b, hp, 0, 0))],
        out_specs=[_pair_spec(seq, 0), _stat_spec(seq)],
        out_shape=[jax.ShapeDtypeStruct((nb * seq, D_MODEL), F32), jax.ShapeDtypeStruct((nb, PAIRS, seq, BLK), F32)],
        **_ATT_PARAMS,
    )(p, p, p, cc, cr)


def _fox_bwd(p, do, o, lse, cc, cr, nb, seq, *, name):
    tq = min(Q_BLOCK, seq)
    nq = seq // tq
    kw = min(KEY_BLOCK, seq)
    ksub = kw // BLK

    def body(q_ref, k_ref, v_ref, do_ref, o_ref, lse_ref, cc_ref, cr_ref, dq_ref, dk_ref, dv_ref, dcr_ref, dcc_ref):
        hp = pl.program_id(1)
        row, col = _iotas()
        colq = lax.broadcasted_iota(jnp.int32, (tq, BLK), 1)
        head0 = colq < HEAD_DIM
        dk_ref[...] = jnp.zeros(dk_ref.shape, F32)
        dv_ref[...] = jnp.zeros(dv_ref.shape, F32)
        dcr_ref[...] = jnp.zeros(dcr_ref.shape, F32)

        @pl.when(hp == 0)
        def _():
            dcc_ref[...] = jnp.zeros(dcc_ref.shape, F32)

        def qblock(qb, c):
            t0 = pl.multiple_of(qb * tq, tq)
            qs = _stack_heads(q_ref[pl.ds(t0, tq), :], head0, _SCALE)
            dof = do_ref[pl.ds(t0, tq), :]
            dos = _stack_heads(dof, head0)
            prod = dof * o_ref[pl.ds(t0, tq), :]
            dl = jnp.concatenate([jnp.sum(jnp.where(head0, prod, 0.0), axis=1, keepdims=True),
                                  jnp.sum(jnp.where(head0, 0.0, prod), axis=1, keepdims=True)], axis=0)
            lse = _pair_cols(lse_ref[pl.ds(t0, tq), :], head0, -jnp.inf)
            c0, c1 = _fox_cum_cols(cc_ref, t0, tq, colq, hp)
            diag = (t0 + tq - 1) // kw

            def kblock(kb, carry, masked):
                dq, rs = carry
                s0 = pl.multiple_of(kb * kw, kw)
                kf = k_ref[pl.ds(s0, kw), :]
                k = kf.astype(BF16)
                k0, k1 = _heads(kf, head0)
                v = v_ref[pl.ds(s0, kw), :].astype(BF16)
                mask = _causal_mask(t0, s0, tq, kw, True)[:tq] if masked else None
                zs = [_dot(qs[h * tq:(h + 1) * tq], k, _NT) for h in range(2)]
                dps = [_dot(dos[h * tq:(h + 1) * tq], v, _NT) for h in range(2)]
                prs, dss, rss = [], [], []
                for h, (ch, kh) in enumerate(((c0, k0), (c1, k1))):
                    rows = slice(h * tq, (h + 1) * tq)
                    pr = jnp.exp(zs[h] + (ch - cr_ref[h:h + 1, pl.ds(s0, kw)]) - lse[rows])
                    if masked:
                        pr = jnp.where(mask, pr, 0.0)
                    ds = pr * (dps[h] - dl[rows])
                    dcr_ref[h:h + 1, pl.ds(s0, kw)] -= _colsum(ds)
                    rss.append(rs[rows] + jnp.sum(ds, axis=1, keepdims=True))
                    prs.append(pr.astype(BF16))
                    dss.append(ds.astype(BF16))
                    dq = dq + _dot(dss[h], kh)
                dk_ref[pl.ds(s0, kw), :] += _dot(jnp.concatenate(dss, axis=0), qs, _TN)
                dv_ref[pl.ds(s0, kw), :] += _dot(jnp.concatenate(prs, axis=0), dos, _TN)
                return dq, jnp.concatenate(rss, axis=0)

            init = (jnp.zeros((tq, BLK), F32), jnp.zeros((2 * tq, 1), F32))
            carry = lax.fori_loop(0, diag, lambda kb, cr: kblock(kb, cr, False), init)
            dq, rs = kblock(diag, carry, True)
            dq_ref[pl.ds(t0, tq), :] = dq * _SCALE
            dcc_ref[pl.ds(t0, tq), :] += jnp.where(colq == 2 * hp, rs[:tq], 0.0) + jnp.where(colq == 2 * hp + 1, rs[tq:], 0.0)
            return c

        lax.fori_loop(0, nq, qblock, 0)

    t = nb * seq
    return pl.pallas_call(
        body, name=name, grid=(nb, PAIRS),
        in_specs=[_pair_spec(seq, 0), _pair_spec(seq, PAIRS), _pair_spec(seq, 2 * PAIRS), _pair_spec(seq, 0), _pair_spec(seq, 0),
                  _stat_spec(seq), pl.BlockSpec((seq, BLK), lambda b, hp: (b, 0)),
                  pl.BlockSpec((None, None, 8, seq), lambda b, hp: (b, hp, 0, 0))],
        out_specs=[_pair_spec(seq, 0)] * 3 + [pl.BlockSpec((None, None, 8, seq), lambda b, hp: (b, hp, 0, 0)),
                                              pl.BlockSpec((seq, BLK), lambda b, hp: (b, 0))],
        out_shape=[jax.ShapeDtypeStruct((t, D_MODEL), F32)] * 3 + [jax.ShapeDtypeStruct((nb, PAIRS, 8, seq), F32),
                                                                     jax.ShapeDtypeStruct((t, BLK), F32)],
        compiler_params=pltpu.CompilerParams(dimension_semantics=("parallel", "arbitrary")),
    )(p, p, p, do, o, lse, cc, cr)


def _row_shards(x):
    return x.reshape(N_CHIPS, x.shape[0] // N_CHIPS, x.shape[1])


def _local_step(x3, tgt3, w, later=None, start_reduce=None):
    nb, seq, d = x3.shape
    t = nb * seq
    x0, tgt = x3.reshape(t, d), tgt3.reshape(t, d)
    g = {}

    a_gain = w["a_norm"].reshape(1, d)
    h_a, ht_a = _rmsnorm(x0, a_gain, name="a_norm_fwd")
    p_a = _matmul(h_a, w["a_w_in"][:3], name="a_in_fwd", out_dtype=BF16)
    gate_a = _matmul(h_a, w["a_w_in"][3], name="a_in_gate_fwd")
    o_a, tot_a = _sb_fwd(p_a, nb, seq, name="a_attn_fwd")
    y_a, yt_a = _gate(o_a, gate_a, 0, name="a_gate_fwd")
    x1 = _matmul(y_a, w["a_w_out"], name="a_out_fwd", residual=x0)

    if later:
        w = {**w, **later[0](x1)}
    b_gain = w["b_norm"].reshape(1, d)
    b_lg, b_lb = w["b_v_ln_g"].reshape(1, GM_W), w["b_v_ln_b"].reshape(1, GM_W)
    b_ws, b_bst = w["b_w_s"].reshape(GM_G, BLK, BLK), w["b_b_s"].reshape(GM_G, BLK).T
    h_b, ht_b = _rmsnorm(x1, b_gain, name="b_norm_fwd")
    p_b = _matmul(h_b, w["b_w_in"], name="b_in_fwd")
    y_b, yt_b = _gmlp_fwd(p_b, b_lg, b_lb, b_ws, b_bst, name="b_mix_fwd")
    x2 = _matmul(y_b, w["b_w_out"], name="b_out_fwd", residual=x1)

    if later:
        w = {**w, **later[1](x2)}
    c_gain = w["c_norm"].reshape(1, d)
    c_cw = jnp.repeat(w["c_conv_w"].reshape(CV_K, CV_W), SUBLANES, axis=0)
    c_cb = w["c_conv_b"].reshape(1, CV_W)
    c_lg, c_lb = w["c_ln_g"].reshape(1, CV_W), w["c_ln_b"].reshape(1, CV_W)
    h_c, ht_c = _rmsnorm(x2, c_gain, name="c_norm_fwd")
    p_c = _matmul(h_c, w["c_w_in"], name="c_in_fwd")
    y_c, yt_c = _conv_fwd(p_c, c_cw, c_cb, c_lg, c_lb, seq, name="c_conv_fwd")
    x3_ = _matmul(y_c, w["c_w_out"], name="c_out_fwd", residual=x2)

    d_gain = w["d_norm"].reshape(1, d)
    d_win = w["d_w_in"].reshape(d, 4 * D_MODEL + HEADS)
    d_wmain = d_win[:, :4 * D_MODEL]
    d_wf = jnp.pad(d_win[:, 4 * D_MODEL:], ((0, 0), (0, BLK - HEADS)))
    d_bf = jnp.pad(w["d_b_f"].reshape(1, HEADS), ((0, 0), (0, BLK - HEADS)))
    h_d, ht_d = _rmsnorm(x3_, d_gain, name="d_norm_fwd")
    p_d = _matmul(h_d, d_wmain[:, :3 * D_MODEL], name="d_in_fwd", out_dtype=BF16)
    gate_d = _matmul(h_d, d_wmain[:, 3 * D_MODEL:], name="d_in_gate_fwd")
    f_d = _matmul(h_d, d_wf, name="d_inf_fwd")
    cc, cr = _fox_cum(f_d, d_bf, nb, seq, name="d_cum_fwd")
    cr = jnp.pad(cr.reshape(nb, PAIRS, 2, seq), ((0, 0), (0, 0), (0, 6), (0, 0)))
    o_d, lse_d = _fox_fwd(p_d, cc, cr, nb, seq, name="d_attn_fwd")
    y_d, yt_d = _gate(o_d, gate_d, 0, name="d_gate_fwd")
    x4 = _matmul(y_d, w["d_w_out"], name="d_out_fwd", residual=x3_)

    f_gain = w["final_norm"].reshape(1, d)
    dx, g_fn, loss_row = _loss_head(x4, f_gain, tgt, name="loss_head")
    g["final_norm"] = g_fn

    g["d_w_out"] = _row_shards(_matmul(yt_d, dx, name="d_out_dw"))
    dy = _matmul(dx, w["d_w_out"], name="d_out_dy", mode="nt")
    do_d, dg_d = _gate_bwd(dy, o_d, gate_d, 0, name="d_gate_bwd")
    dq, dk, dv, dcr, dcc = _fox_bwd(p_d, do_d, o_d, lse_d, cc, cr, nb, seq, name="d_attn_bwd")
    dcr = jnp.pad(dcr[:, :, :2, :].reshape(nb, HEADS, seq), ((0, 0), (0, BLK - HEADS), (0, 0)))
    df, dbf = _fox_cum_bwd(dcr, dcc, f_d, d_bf, nb, seq, name="d_cum_bwd")
    g["d_b_f"] = dbf[:, :HEADS]
    parts = [dq, dk, dv, dg_d]
    dws = [_matmul(ht_d, pt, name=f"d_in_dw{n}") for n, pt in enumerate(parts)]
    dwf = _matmul(ht_d, df, name="d_inf_dw")
    g["d_w_in"] = jnp.concatenate(dws + [dwf[:, :HEADS]], axis=1).reshape(d, N_CHIPS, -1).transpose(1, 0, 2)
    dh = _matmul(df, d_wf, name="d_inf_dh", mode="nt")
    for n, pt in enumerate(parts):
        dh = _matmul(pt, d_wmain[:, n * D_MODEL:(n + 1) * D_MODEL], name=f"d_in_dh{n}", mode="nt", residual=dh)
    dx, g["d_norm"] = _rmsnorm_bwd(dh, x3_, d_gain, dx, name="d_norm_bwd")

    g["c_w_out"] = _row_shards(_matmul(yt_c, dx, name="c_out_dw"))
    dy = _matmul(dx, w["c_w_out"], name="c_out_dy", mode="nt")
    dy1, dgate, g["c_ln_g"], g["c_ln_b"], g["c_conv_b"], g["c_conv_w"] = _conv_bwd_post(
        dy, p_c, c_cw, c_cb, c_lg, c_lb, seq, name="c_conv_bwd_post")
    dp = _conv_bwd_pre(dy1, dgate, p_c, c_cw, seq, name="c_conv_bwd_pre")
    g["c_w_in"] = _matmul(ht_c, dp, name="c_in_dw", out_shards=N_CHIPS)
    dh = _matmul(dp, w["c_w_in"], name="c_in_dh", mode="nt")
    dx, g["c_norm"] = _rmsnorm_bwd(dh, x2, c_gain, dx, name="c_norm_bwd")

    early, b_wout, a_wout = [], w["b_w_out"], w["a_w_out"]
    if start_reduce is not None:
        begun, token = start_reduce({n: g[n] for n in ("d_w_in", "d_w_out", "c_w_in", "c_w_out")}, "grads_cd")
        early.append(begun)
        b_wout = b_wout + token[0, 0].astype(b_wout.dtype)
    g["b_w_out"] = _row_shards(_matmul(yt_b, dx, name="b_out_dw"))
    dy = _matmul(dx, b_wout, name="b_out_dy", mode="nt")
    dp, g["b_v_ln_g"], g["b_v_ln_b"], g["b_w_s"], dbst = _gmlp_bwd(dy, p_b, b_lg, b_lb, b_ws, b_bst, name="b_mix_bwd")
    g["b_b_s"] = dbst.T
    g["b_w_in"] = _matmul(ht_b, dp, name="b_in_dw", out_shards=N_CHIPS)
    dh = _matmul(dp, w["b_w_in"], name="b_in_dh", mode="nt")
    dx, g["b_norm"] = _rmsnorm_bwd(dh, x1, b_gain, dx, name="b_norm_bwd")

    if start_reduce is not None:
        begun, token = start_reduce({n: g[n] for n in ("b_w_in", "b_w_out")}, "grads_b")
        early.append(begun)
        a_wout = a_wout + token[0, 0].astype(a_wout.dtype)
    g["a_w_out"] = _row_shards(_matmul(yt_a, dx, name="a_out_dw"))
    dy = _matmul(dx, a_wout, name="a_out_dy", mode="nt")
    do_a, dg_a = _gate_bwd(dy, o_a, gate_a, 0, name="a_gate_bwd")
    dq, dk, dv = _sb_bwd(p_a, do_a, tot_a, nb, seq, name="a_attn_bwd")
    parts = [dq, dk, dv, dg_a]
    g["a_w_in"] = jnp.stack([_matmul(ht_a, pt, name=f"a_in_dw{n}") for n, pt in enumerate(parts)])
    dh = None
    for n, pt in enumerate(parts):
        dh = _matmul(pt, w["a_w_in"][n], name=f"a_in_dh{n}", mode="nt", residual=dh)
    dx, g["a_norm"] = _rmsnorm_bwd(dh, x0, a_gain, dx, name="a_norm_bwd")

    return loss_row[0, 0], dx.reshape(nb, seq, d), g, early


_HBM = pl.BlockSpec(memory_space=pltpu.HBM)


def _place():
    return lax.axis_index("x"), lax.axis_index("y"), lax.axis_index("c")


def _other_chips(x, y):
    return [(1 - x, y), (x, 1 - y), (1 - x, 1 - y)]


def _allgather_chips(ss, *, name):
    n_ops = len(ss)

    def body(*refs):
        s_refs, o_refs, (send_sems, recv_sems) = refs[:n_ops], refs[n_ops:2 * n_ops], refs[2 * n_ops:]
        x, y, c = _place()
        me = 2 * x + y
        chips = _other_chips(x, y)

        def copy(i, kk, src, dst, to):
            return pltpu.make_async_remote_copy(src_ref=src, dst_ref=dst, send_sem=send_sems.at[6 * i + kk],
                                                recv_sem=recv_sems.at[6 * i + kk], device_id=to, device_id_type=MESH)

        def half(i, j, hc):
            h = s_refs[i].shape[0] // 2
            return o_refs[i].at[j, pl.ds(hc * h, h), :]

        first = [copy(i, kk, s_refs[i].at[pl.ds(c * (s_refs[i].shape[0] // 2), s_refs[i].shape[0] // 2), :], half(i, me, c),
                      (cx, cy, c)) for kk, (cx, cy) in enumerate(chips) for i in range(n_ops)]
        for cp in first:
            cp.start()
        passed = []
        for kk, (cx, cy) in enumerate(chips):
            for i in range(n_ops):
                blk = half(i, 2 * cx + cy, c)
                copy(i, kk, blk, blk, (cx, cy, c)).wait_recv()
                fwd = copy(i, 3 + kk, blk, blk, (x, y, 1 - c))
                fwd.start()
                passed.append(fwd)
        for kk, (cx, cy) in enumerate(chips):
            for i in range(n_ops):
                blk = half(i, 2 * cx + cy, 1 - c)
                copy(i, 3 + kk, blk, blk, (x, y, 1 - c)).wait_recv()
        for cp in first + passed:
            cp.wait_send()

    for s in ss:
        assert s.shape[0] % 32 == 0, s.shape
    return pl.pallas_call(
        body, name=name, in_specs=[_HBM] * n_ops, out_specs=[_HBM] * n_ops,
        out_shape=[jax.ShapeDtypeStruct((N_CHIPS,) + s.shape, s.dtype) for s in ss],
        scratch_shapes=[pltpu.SemaphoreType.DMA((6 * n_ops,)), pltpu.SemaphoreType.DMA((6 * n_ops,))],
    )(*ss)


_SEM = pl.BlockSpec(memory_space=pltpu.SEMAPHORE)
_ANY = pl.BlockSpec(memory_space=pl.ANY)
_DATAFLOW = pltpu.SideEffectType.DATAFLOW_SIDE_EFFECTING


def _chip_copies(s_refs, land_refs, send_sems, recv_sems):
    x, y, c = _place()
    me = 2 * x + y
    cps = []
    for i, (s_ref, land_ref) in enumerate(zip(s_refs, land_refs)):
        h = s_ref.shape[0] // 2
        for kk, (cx, cy) in enumerate(_other_chips(x, y)):
            cps.append(pltpu.make_async_remote_copy(
                src_ref=s_ref.at[pl.ds(c * h, h), :], dst_ref=land_ref.at[me, pl.ds(c * h, h), :], send_sem=send_sems.at[3 * i + kk],
                recv_sem=recv_sems.at[3 * i + kk], device_id=(cx, cy, c), device_id_type=MESH))
    return cps


def _gather_start(ss, after, *, name):
    n = len(ss)
    lands = [lax.empty((N_CHIPS,) + s.shape, s.dtype) for s in ss]

    def body(*refs):
        s_refs, land_refs = refs[:n], refs[n:2 * n]
        send_sems, recv_sems = refs[2 * n + 1], refs[2 * n + 2]
        token = refs[-1]
        for cp in _chip_copies(s_refs, land_refs, send_sems, recv_sems):
            cp.start()
        token[...] = jnp.zeros(token.shape, token.dtype)

    hbm = [pltpu.HBM(a.shape, a.dtype) for a in list(ss) + lands]
    res = pl.pallas_call(
        body, name=name,
        out_shape=(pltpu.SemaphoreType.DMA((3 * n,)), pltpu.SemaphoreType.DMA((3 * n,)), *hbm, jax.ShapeDtypeStruct((8, BLK), F32)),
        in_specs=[_HBM] * (2 * n) + [_ANY],
        out_specs=(_SEM, _SEM, *([_HBM] * (2 * n)), pl.BlockSpec(memory_space=pltpu.VMEM)),
        input_output_aliases={i: 2 + i for i in range(2 * n)},
        compiler_params=pltpu.CompilerParams(has_side_effects=_DATAFLOW),
    )(*[pltpu.with_memory_space_constraint(a, pltpu.HBM) for a in list(ss) + lands], after)
    return res[:-1], res[-1]


def _gather_wait(started, after, *, name):
    send_sems, recv_sems = started[0], started[1]
    n = (len(started) - 2) // 2

    def body(*refs):
        s_refs, land_refs = refs[:n], refs[n:2 * n]
        for cp in _chip_copies(s_refs, land_refs, refs[2 * n], refs[2 * n + 1]):
            cp.wait_send()
            cp.wait_recv()

    res = pl.pallas_call(
        body, name=name, out_shape=tuple(pltpu.HBM(a.shape, a.dtype) for a in started[2:]),
        in_specs=[_HBM] * (2 * n) + [_SEM, _SEM, _ANY], out_specs=tuple([_HBM] * (2 * n)),
        input_output_aliases={i: i for i in range(2 * n)},
        compiler_params=pltpu.CompilerParams(has_side_effects=_DATAFLOW),
    )(*started[2:], send_sems, recv_sems, after)
    return list(res[n:])


def _sibling_exchange(lands, *, name):
    n = len(lands)

    def body(*refs):
        o_refs, (send_sems, recv_sems) = refs[n:2 * n], refs[2 * n:]
        x, y, c = _place()
        cps = []
        for i, o_ref in enumerate(o_refs):
            h = o_ref.shape[1] // 2
            for kk, (cx, cy) in enumerate(_other_chips(x, y)):
                def half(hc):
                    return o_ref.at[2 * cx + cy, pl.ds(hc * h, h), :]
                sent = pltpu.make_async_remote_copy(src_ref=half(c), dst_ref=half(c), send_sem=send_sems.at[3 * i + kk],
                                                    recv_sem=recv_sems.at[3 * i + kk], device_id=(x, y, 1 - c), device_id_type=MESH)
                awaited = pltpu.make_async_remote_copy(src_ref=half(1 - c), dst_ref=half(1 - c), send_sem=send_sems.at[3 * i + kk],
                                                       recv_sem=recv_sems.at[3 * i + kk], device_id=(x, y, 1 - c),
                                                       device_id_type=MESH)
                cps.append((sent, awaited))
        for sent, _ in cps:
            sent.start()
        for sent, awaited in cps:
            awaited.wait_recv()
            sent.wait_send()

    return pl.pallas_call(
        body, name=name, in_specs=[_HBM] * n, out_specs=[_HBM] * n,
        out_shape=[jax.ShapeDtypeStruct(a.shape, a.dtype) for a in lands], scratch_shapes=_dma_sems(3 * n),
        input_output_aliases={i: i for i in range(n)},
    )(*lands)


def _own_block(gathered, s):
    me = 2 * lax.axis_index("x") + lax.axis_index("y")
    return lax.dynamic_update_slice(gathered, s[None], (me,) + (0,) * s.ndim)


def _dma_sems(n):
    return [pltpu.SemaphoreType.DMA((n,)), pltpu.SemaphoreType.DMA((n,))]


def _swap_halves(gps, *, name):
    n_ops = len(gps)

    def body(*refs):
        g_refs, o_refs, (send_sems, recv_sems) = refs[:n_ops], refs[n_ops:2 * n_ops], refs[2 * n_ops:]
        x, y, c = _place()
        cps = []
        for i, (g_ref, o_ref) in enumerate(zip(g_refs, o_refs)):
            h = g_ref.shape[1] // 2
            cps.append(pltpu.make_async_remote_copy(
                src_ref=g_ref.at[:, pl.ds((1 - c) * h, h), :], dst_ref=o_ref, send_sem=send_sems.at[i], recv_sem=recv_sems.at[i],
                device_id=(x, y, 1 - c), device_id_type=MESH))
        for cp in cps:
            cp.start()
        for cp in cps:
            cp.wait()

    return pl.pallas_call(
        body, name=name, in_specs=[_HBM] * n_ops, out_specs=[_HBM] * n_ops,
        out_shape=[jax.ShapeDtypeStruct((g.shape[0], g.shape[1] // 2, g.shape[2]), g.dtype) for g in gps],
        scratch_shapes=_dma_sems(n_ops),
    )(*gps)


def _scatter_chips(hps, *, name):
    n_ops = len(hps)

    def body(*refs):
        h_refs, o_refs, (send_sems, recv_sems) = refs[:n_ops], refs[n_ops:2 * n_ops], refs[2 * n_ops:]
        x, y, c = _place()
        cps = [pltpu.make_async_remote_copy(src_ref=h_ref.at[2 * cx + cy], dst_ref=o_ref.at[kk], send_sem=send_sems.at[3 * i + kk],
                                            recv_sem=recv_sems.at[3 * i + kk], device_id=(cx, cy, c), device_id_type=MESH)
               for i, (h_ref, o_ref) in enumerate(zip(h_refs, o_refs)) for kk, (cx, cy) in enumerate(_other_chips(x, y))]
        for cp in cps:
            cp.start()
        for cp in cps:
            cp.wait()

    return pl.pallas_call(
        body, name=name, in_specs=[_HBM] * n_ops, out_specs=[_HBM] * n_ops,
        out_shape=[jax.ShapeDtypeStruct((3,) + hp.shape[1:], hp.dtype) for hp in hps],
        scratch_shapes=_dma_sems(3 * n_ops),
    )(*hps)


def _join_halves(fs, *, name):
    n_ops = len(fs)

    def body(*refs):
        f_refs, o_refs, (send_sems, recv_sems) = refs[:n_ops], refs[n_ops:2 * n_ops], refs[2 * n_ops:]
        x, y, c = _place()
        cps = [pltpu.make_async_remote_copy(src_ref=f_ref, dst_ref=o_ref, send_sem=send_sems.at[i], recv_sem=recv_sems.at[i],
                                            device_id=(x, y, 1 - c), device_id_type=MESH)
               for i, (f_ref, o_ref) in enumerate(zip(f_refs, o_refs))]
        for cp in cps:
            cp.start()
        for cp in cps:
            cp.wait()

    return pl.pallas_call(
        body, name=name, in_specs=[_HBM] * n_ops, out_specs=[_HBM] * n_ops,
        out_shape=[jax.ShapeDtypeStruct(f.shape, f.dtype) for f in fs], scratch_shapes=_dma_sems(n_ops),
    )(*fs)


def _stitch(mine, theirs):
    south = lax.axis_index("c") == 0
    return jnp.concatenate([jnp.where(south, mine, theirs), jnp.where(south, theirs, mine)], axis=0)


def _add_halves(gp, ra, wire_dtype, *, name, bm=256):
    n, r, c_ = gp.shape
    h = r // 2
    bm = _tile(h, bm)
    per = h // bm
    c = lax.axis_index("c").astype(jnp.int32).reshape(1)

    def body(c_ref, g_ref, ra_ref, o_ref, ow_ref):
        s = g_ref[...] + ra_ref[...]
        o_ref[...] = s
        ow_ref[...] = s.astype(wire_dtype)

    mine = pl.BlockSpec((None, bm, c_), lambda j, i, cr: (j, i, 0))
    return pl.pallas_call(
        body, name=name,
        grid_spec=pltpu.PrefetchScalarGridSpec(
            num_scalar_prefetch=1, grid=(n, per),
            in_specs=[pl.BlockSpec((None, bm, c_), lambda j, i, cr: (j, cr[0] * per + i, 0)), mine],
            out_specs=[mine, mine]),
        out_shape=[jax.ShapeDtypeStruct((n, h, c_), F32), jax.ShapeDtypeStruct((n, h, c_), wire_dtype)],
        compiler_params=pltpu.CompilerParams(dimension_semantics=("parallel", "parallel")),
    )(c, gp, ra)


def _add_chips(hp, rb, *, name, bm=256):
    n, h, c_ = hp.shape
    bm = _tile(h, bm)
    me = (2 * lax.axis_index("x") + lax.axis_index("y")).astype(jnp.int32).reshape(1)

    def body(me_ref, h_ref, rb_ref, o_ref):
        o_ref[...] = ((h_ref[...] + rb_ref[0].astype(F32)) + rb_ref[1].astype(F32)) + rb_ref[2].astype(F32)

    return pl.pallas_call(
        body, name=name,
        grid_spec=pltpu.PrefetchScalarGridSpec(
            num_scalar_prefetch=1, grid=(h // bm,),
            in_specs=[pl.BlockSpec((None, bm, c_), lambda i, mr: (mr[0], i, 0)),
                      pl.BlockSpec((3, bm, c_), lambda i, mr: (0, i, 0))],
            out_specs=pl.BlockSpec((bm, c_), lambda i, mr: (i, 0))),
        out_shape=jax.ShapeDtypeStruct((h, c_), F32),
        compiler_params=pltpu.CompilerParams(dimension_semantics=("parallel",)),
    )(me, hp, rb)


def _scatter_copies(h_refs, land_refs, send_sems, recv_sems):
    x, y, c = _place()
    return [pltpu.make_async_remote_copy(src_ref=h_ref.at[2 * cx + cy], dst_ref=land_ref.at[kk], send_sem=send_sems.at[3 * i + kk],
                                         recv_sem=recv_sems.at[3 * i + kk], device_id=(cx, cy, c), device_id_type=MESH)
            for i, (h_ref, land_ref) in enumerate(zip(h_refs, land_refs)) for kk, (cx, cy) in enumerate(_other_chips(x, y))]


def _scatter_start(hps, after, *, name):
    n = len(hps)
    lands = [lax.empty((3,) + hp.shape[1:], hp.dtype) for hp in hps]

    def body(*refs):
        for cp in _scatter_copies(refs[:n], refs[n:2 * n], refs[2 * n + 1], refs[2 * n + 2]):
            cp.start()
        refs[-1][...] = jnp.zeros(refs[-1].shape, refs[-1].dtype)

    hbm = [pltpu.HBM(a.shape, a.dtype) for a in list(hps) + lands]
    res = pl.pallas_call(
        body, name=name,
        out_shape=(pltpu.SemaphoreType.DMA((3 * n,)), pltpu.SemaphoreType.DMA((3 * n,)), *hbm, jax.ShapeDtypeStruct((8, BLK), F32)),
        in_specs=[_HBM] * (2 * n) + [_ANY],
        out_specs=(_SEM, _SEM, *([_HBM] * (2 * n)), pl.BlockSpec(memory_space=pltpu.VMEM)),
        input_output_aliases={i: 2 + i for i in range(2 * n)},
        compiler_params=pltpu.CompilerParams(has_side_effects=_DATAFLOW),
    )(*[pltpu.with_memory_space_constraint(a, pltpu.HBM) for a in list(hps) + lands], after)
    return res[:-1], res[-1]


def _scatter_wait(started, after, *, name):
    n = (len(started) - 2) // 2

    def body(*refs):
        for cp in _scatter_copies(refs[:n], refs[n:2 * n], refs[2 * n], refs[2 * n + 1]):
            cp.wait_send()
            cp.wait_recv()

    res = pl.pallas_call(
        body, name=name, out_shape=tuple(pltpu.HBM(a.shape, a.dtype) for a in started[2:]),
        in_specs=[_HBM] * (2 * n) + [_SEM, _SEM, _ANY], out_specs=tuple([_HBM] * (2 * n)),
        input_output_aliases={i: i for i in range(2 * n)},
        compiler_params=pltpu.CompilerParams(has_side_effects=_DATAFLOW),
    )(*started[2:], started[0], started[1], after)
    return list(res[n:])


def _reduce_to_chips(gps, wire_dtypes, *, tag):
    ras = _swap_halves(gps, name=f"{tag}_swap_halves")
    return [_add_halves(gp, ra, wd, name=f"{tag}_add_halves{i}") for i, (gp, ra, wd) in enumerate(zip(gps, ras, wire_dtypes))]


def _start_reduce(early, tag):
    names = list(early)
    hps = _reduce_to_chips([early[n] for n in names], [BF16] * len(names), tag=tag)
    started, token = _scatter_start([hw for _, hw in hps], hps[-1][1], name=f"{tag}_scatter_start")
    return (tag, names, [hf for hf, _ in hps], started), token


def _adamw_math(w_ref, g_ref, m_ref, v_ref, d_ref, nm_ref, nv_ref):
    c1 = 1.0 - ADAM_B1 ** ADAM_STEP
    c2 = 1.0 - ADAM_B2 ** ADAM_STEP
    g_ = g_ref[...]
    m_ = ADAM_B1 * m_ref[...] + (1.0 - ADAM_B1) * g_
    v_ = ADAM_B2 * v_ref[...] + (1.0 - ADAM_B2) * (g_ * g_)
    d_ref[...] = -ADAM_LR * ((m_ / c1) / (jnp.sqrt(v_ / c2) + ADAM_EPS) + ADAM_WD * w_ref[...])
    nm_ref[...] = m_
    nv_ref[...] = v_


def _adamw_many(groups, *, name):
    n = len(groups[0])
    flat = [a for grp in groups for a in grp]

    def body(*refs):
        ins, outs = refs[:4 * n], refs[4 * n:]
        for i in range(n):
            _adamw_math(ins[i], ins[n + i], ins[2 * n + i], ins[3 * n + i], outs[i], outs[n + i], outs[2 * n + i])

    vmem = pl.BlockSpec(memory_space=pltpu.VMEM)
    res = pl.pallas_call(
        body, name=name, in_specs=[vmem] * (4 * n), out_specs=[vmem] * (3 * n),
        out_shape=[jax.ShapeDtypeStruct(a.shape, F32) for _ in range(3) for a in groups[0]],
    )(*flat)
    return res[:n], res[n:2 * n], res[2 * n:]


def _adamw(w, g_mine, g_theirs, m, v, *, name):
    r, c_ = w.shape
    h = r // 2
    bm = _tile(h, 256)
    per = h // bm
    c = lax.axis_index("c").astype(jnp.int32).reshape(1)

    def body(c_ref, w_ref, f_ref, t_ref, m_ref, v_ref, g_ref, d_ref, nm_ref, nv_ref):
        first_half = pl.program_id(0) < per
        mine = jnp.where(jnp.where(first_half, c_ref[0] == 0, c_ref[0] == 1), 1.0, 0.0)
        g_ref[...] = mine * f_ref[...] + (1.0 - mine) * t_ref[...]
        _adamw_math(w_ref, g_ref, m_ref, v_ref, d_ref, nm_ref, nv_ref)

    full = pl.BlockSpec((bm, c_), lambda i, cr: (i, 0))
    half = pl.BlockSpec((bm, c_), lambda i, cr: (i % per, 0))
    return pl.pallas_call(
        body, name=name,
        grid_spec=pltpu.PrefetchScalarGridSpec(num_scalar_prefetch=1, grid=(r // bm,), in_specs=[full, half, half, full, full],
                                               out_specs=[full] * 4),
        out_shape=[jax.ShapeDtypeStruct((r, c_), F32)] * 4,
        compiler_params=pltpu.CompilerParams(dimension_semantics=("parallel",)),
    )(c, w, g_mine, g_theirs, m, v)


_WEIGHTS = ["a_norm", "a_w_in", "a_w_out", "b_norm", "b_w_in", "b_v_ln_g", "b_v_ln_b", "b_w_s", "b_b_s", "b_w_out",
            "c_norm", "c_w_in", "c_conv_w", "c_conv_b", "c_ln_g", "c_ln_b", "c_w_out", "d_norm", "d_w_in", "d_b_f",
            "d_w_out", "final_norm"]
_SHARD_AXIS = {"a_norm": None, "a_w_in": 2, "a_w_out": 1, "b_norm": 1, "b_w_in": 2, "b_v_ln_g": 1, "b_v_ln_b": 1, "b_w_s": None,
               "b_b_s": None, "b_w_out": 1, "c_norm": 1, "c_w_in": 2, "c_conv_w": 2, "c_conv_b": 1, "c_ln_g": 1, "c_ln_b": 1,
               "c_w_out": 1, "d_norm": 1, "d_w_in": 2, "d_b_f": None, "d_w_out": 1, "final_norm": None}
_BIG = ["a_w_in", "a_w_out", "b_w_in", "b_w_out", "c_w_in", "c_w_out", "d_w_in", "d_w_out"]
_GATHER_GROUPS = (("a_w_in", "a_w_out"), ("b_w_in", "b_w_out"), ("c_w_in", "c_w_out", "d_w_in", "d_w_out"))
_SMALL_SHARDED = [n for n in _WEIGHTS if _SHARD_AXIS[n] is not None and n not in _BIG]
_REPLICATED = [n for n in _WEIGHTS if _SHARD_AXIS[n] is None]
_ROW_ALIGN = 32
_ROW_ALIGN_SUMMED = 128


def _pack(pieces, dtype, align=_ROW_ALIGN):
    flat = jnp.concatenate([p.reshape(-1).astype(dtype) for p in pieces])
    unit = align * PACK_C
    total = -(-flat.shape[0] // unit) * unit
    return jnp.pad(flat, (0, total - flat.shape[0])).reshape(total // PACK_C, PACK_C)


def _unpack(flat, shapes):
    out, off = [], 0
    for s in shapes:
        n = math.prod(s)
        out.append(flat[off:off + n].reshape(s))
        off += n
    return out


def _full_shape(local_shape, axis):
    s = list(local_shape)
    if axis is not None:
        s[axis] *= N_CHIPS
    return tuple(s)


def _gather_weights(local):
    def whole(n, gt):
        if _SHARD_AXIS[n] == 1:
            return gt.reshape(-1, gt.shape[-1])
        if n == "d_w_in":
            return gt.transpose(1, 0, 2).reshape(gt.shape[1], -1)
        return gt

    full = {n: local[n][0] if n != "final_norm" else local[n] for n in _REPLICATED}
    first = list(_GATHER_GROUPS[0])
    mine = [local[n][0].astype(BF16) for n in first] + [_pack([local[n] for n in _SMALL_SHARDED], F32)]
    got = [_own_block(gt, s) for gt, s in zip(_allgather_chips(mine, name="gather_weights"), mine)]
    full.update({n: whole(n, gt) for n, gt in zip(first, got)})
    small = got[-1].reshape(N_CHIPS, -1)
    shards = [_unpack(small[j], [local[n].shape[1:] for n in _SMALL_SHARDED]) for j in range(N_CHIPS)]
    for i, n in enumerate(_SMALL_SHARDED):
        full[n] = jnp.concatenate([shards[j][i] for j in range(N_CHIPS)], axis=_SHARD_AXIS[n] - 1)

    def begin(k, after):
        shards_k = [local[n][0].astype(BF16) for n in _GATHER_GROUPS[k]]
        started, token = _gather_start(shards_k, after, name=f"gather{k}_start")
        return shards_k, started, token

    pending = [begin(1, got[0])]
    full["a_norm"] = full["a_norm"] + pending[0][2][0, 0]

    def finish(k):
        def weights(after):
            shards_k, started, _ = pending[k - 1]
            lands = _gather_wait(started, after, name=f"gather{k}_wait")
            token = None
            if k + 1 < len(_GATHER_GROUPS):
                pending.append(begin(k + 1, lands[0]))
                token = pending[k][2]
            lands = _sibling_exchange(lands, name=f"gather{k}_exchange")
            out = {n: whole(n, _own_block(gt, s)) for n, gt, s in zip(_GATHER_GROUPS[k], lands, shards_k)}
            if token is not None:
                gain = _GATHER_GROUPS[k][0][0] + "_norm"
                out[gain] = full[gain] + token[0, 0]
            return out
        return weights

    return full, [finish(k) for k in range(1, len(_GATHER_GROUPS))]


def _repl_piece_len(local):
    total = sum(math.prod(local[n].shape) for n in _REPLICATED)
    return -(-total // N_CHIPS)


def _reduce_grads(g, local, early):
    rep_flat = jnp.concatenate([g[n].reshape(-1) for n in _REPLICATED])
    piece = _repl_piece_len(local)
    rep_flat = jnp.pad(rep_flat, (0, N_CHIPS * piece - rep_flat.shape[0]))

    def shard(n, j):
        full = g[n].reshape(_full_shape(local[n].shape, _SHARD_AXIS[n]))
        width = local[n].shape[_SHARD_AXIS[n]]
        return lax.slice_in_dim(full, j * width, (j + 1) * width, axis=_SHARD_AXIS[n])

    small = jnp.stack([_pack([shard(n, j) for n in _SMALL_SHARDED] + [rep_flat[j * piece:(j + 1) * piece]], F32)
                       for j in range(N_CHIPS)])
    early_names = [n for _, names, _, _ in early for n in names]
    late = [n for n in _BIG if n not in early_names]
    hps = _reduce_to_chips([g[n] for n in late] + [small], [BF16] * len(late) + [F32], tag="grads")
    rbs = list(_scatter_chips([hw for _, hw in hps], name="grads_scatter_chips"))
    early_halves, early_rbs = [], []
    for tag, _, halves_k, started in early:
        early_halves += halves_k
        early_rbs += _scatter_wait(started, rbs[0], name=f"{tag}_scatter_wait")
    halves = early_halves + [hf for hf, _ in hps]
    fs = [_add_chips(hf, rb, name=f"grads_add_chips{i}") for i, (hf, rb) in enumerate(zip(halves, early_rbs + rbs))]
    theirs = _join_halves(fs, name="grads_join_halves")
    red = dict(zip(early_names + late, zip(fs, theirs)))
    out = _unpack(_stitch(fs[-1], theirs[-1]).reshape(-1), [local[n].shape for n in _SMALL_SHARDED] + [(piece,)])
    red.update(zip(_SMALL_SHARDED, out[:-1]))
    rep_mine = _pack([out[-1]], F32)
    rep = _own_block(_allgather_chips([rep_mine], name="gather_replicated_grads")[0], rep_mine)
    rep = rep.reshape(N_CHIPS, -1)[:, :piece].reshape(-1)
    for n, val in zip(_REPLICATED, _unpack(rep, [local[n].shape for n in _REPLICATED])):
        red[n] = val
    return red


def _update(local, grads, m, v):
    grads, delta, new_m, new_v = dict(grads), {}, {}, {}
    for n in _BIG:
        shp = local[n].shape
        two = (shp[-2], shp[-1])
        res = _adamw(local[n].reshape(two), *grads[n], m[n].reshape(two), v[n].reshape(two), name=f"adamw_{n}")
        grads[n], delta[n], new_m[n], new_v[n] = [r.reshape(shp) for r in res]
    small = [n for n in _WEIGHTS if n not in _BIG]
    two = {n: (math.prod(local[n].shape[:-1]), local[n].shape[-1]) for n in small}
    res = _adamw_many([[src[n].reshape(two[n]) for n in small] for src in (local, grads, m, v)], name="adamw_small")
    for dst, rs in zip((delta, new_m, new_v), res):
        for n, val in zip(small, rs):
            dst[n] = val.reshape(local[n].shape)
    return grads, delta, new_m, new_v


def kernel(x, a_norm, a_w_in, a_w_out, b_norm, b_w_in, b_v_ln_g, b_v_ln_b, b_w_s, b_b_s, b_w_out, c_norm, c_w_in, c_conv_w, c_conv_b, c_ln_g, c_ln_b, c_w_out, d_norm, d_w_in, d_b_f, d_w_out, final_norm, loss_target, m_a_norm, m_a_w_in, m_a_w_out, m_b_norm, m_b_w_in, m_b_v_ln_g, m_b_v_ln_b, m_b_w_s, m_b_b_s, m_b_w_out, m_c_norm, m_c_w_in, m_c_conv_w, m_c_conv_b, m_c_ln_g, m_c_ln_b, m_c_w_out, m_d_norm, m_d_w_in, m_d_b_f, m_d_w_out, m_final_norm, v_a_norm, v_a_w_in, v_a_w_out, v_b_norm, v_b_w_in, v_b_v_ln_g, v_b_v_ln_b, v_b_w_s, v_b_b_s, v_b_w_out, v_c_norm, v_c_w_in, v_c_conv_w, v_c_conv_b, v_c_ln_g, v_c_ln_b, v_c_w_out, v_d_norm, v_d_w_in, v_d_b_f, v_d_w_out, v_final_norm):
    local = dict(zip(_WEIGHTS, (a_norm, a_w_in, a_w_out, b_norm, b_w_in, b_v_ln_g, b_v_ln_b, b_w_s, b_b_s, b_w_out, c_norm, c_w_in,
                                c_conv_w, c_conv_b, c_ln_g, c_ln_b, c_w_out, d_norm, d_w_in, d_b_f, d_w_out, final_norm)))
    m = dict(zip(_WEIGHTS, (m_a_norm, m_a_w_in, m_a_w_out, m_b_norm, m_b_w_in, m_b_v_ln_g, m_b_v_ln_b, m_b_w_s, m_b_b_s, m_b_w_out,
                            m_c_norm, m_c_w_in, m_c_conv_w, m_c_conv_b, m_c_ln_g, m_c_ln_b, m_c_w_out, m_d_norm, m_d_w_in, m_d_b_f,
                            m_d_w_out, m_final_norm)))
    v = dict(zip(_WEIGHTS, (v_a_norm, v_a_w_in, v_a_w_out, v_b_norm, v_b_w_in, v_b_v_ln_g, v_b_v_ln_b, v_b_w_s, v_b_b_s, v_b_w_out,
                            v_c_norm, v_c_w_in, v_c_conv_w, v_c_conv_b, v_c_ln_g, v_c_ln_b, v_c_w_out, v_d_norm, v_d_w_in, v_d_b_f,
                            v_d_w_out, v_final_norm)))
    loss_part, grad_x, g, early = _local_step(x, loss_target, *_gather_weights(local), _start_reduce)
    loss = lax.psum(loss_part, ("x", "y", "c"))
    grads = _reduce_grads(g, local, early)
    grads, delta, new_m, new_v = _update(local, grads, m, v)
    return (loss, grad_x, *[grads[n] for n in _WEIGHTS], *[delta[n] for n in _WEIGHTS],
            *[new_m[n] for n in _WEIGHTS], *[new_v[n] for n in _WEIGHTS])
```

```python
import math

import jax
import jax.numpy as jnp
from jax import lax
from jax.experimental import pallas as pl
from jax.experimental.pallas import tpu as pltpu

F32, BF16 = jnp.float32, jnp.bfloat16
MESH = pl.DeviceIdType.MESH

D_MODEL = 1024
HEADS = 16
HEAD_DIM = 64
BLK = 128
PAIRS = HEADS // 2
GM_W = 2048
GM_G = 16
CV_W = 2048
CV_K = 31
HALO = 32
EPS = 1e-6
N_CHIPS = 4
PACK_C = 1024
ADAM_LR, ADAM_B1, ADAM_B2, ADAM_EPS, ADAM_WD, ADAM_STEP = 0.001, 0.9, 0.999, 1e-08, 0.01, 10

_NT = (((1,), (1,)), ((), ()))
_TN = (((0,), (0,)), ((), ()))
_NN = (((1,), (0,)), ((), ()))


def _dot(a, b, dims=_NN):
    return lax.dot_general(a, b, dims, preferred_element_type=F32)


def _split3(x):
    hi = x.astype(BF16)
    r = x - hi.astype(F32)
    mid = r.astype(BF16)
    lo = (r - mid.astype(F32)).astype(BF16)
    return hi, mid, lo


def _dot3_left(m, x):
    hi, mid, lo = _split3(x)
    return _dot(m, hi) + _dot(m, mid) + _dot(m, lo)


def _sigmoid(x):
    return 1.0 / (1.0 + jnp.exp(-x))


def _silu(x):
    return x * _sigmoid(x)


def _dsilu(x):
    s = _sigmoid(x)
    return s * (1.0 + x * (1.0 - s))


_GELU_C = math.sqrt(2.0 / math.pi)
_GELU_A = 0.044715


def _gelu(x):
    return 0.5 * x * (1.0 + jnp.tanh(_GELU_C * (x + _GELU_A * x * x * x)))


def _dgelu(x):
    t = jnp.tanh(_GELU_C * (x + _GELU_A * x * x * x))
    return 0.5 * (1.0 + t) + 0.5 * x * (1.0 - t * t) * _GELU_C * (1.0 + 3.0 * _GELU_A * x * x)


def _log_sigmoid(x):
    return jnp.minimum(x, 0.0) - jnp.log(1.0 + jnp.exp(-jnp.abs(x)))


def _rms_fwd(x, g):
    r = lax.rsqrt(jnp.mean(x * x, axis=-1, keepdims=True) + EPS)
    return x * r * g


def _rms_bwd(dy, x, g):
    r = lax.rsqrt(jnp.mean(x * x, axis=-1, keepdims=True) + EPS)
    xh = x * r
    dxh = dy * g
    dx = r * (dxh - xh * jnp.mean(dxh * xh, axis=-1, keepdims=True))
    return dx, dy * xh


def _ln_stats(x):
    mu = jnp.mean(x, axis=-1, keepdims=True)
    xc = x - mu
    r = lax.rsqrt(jnp.mean(xc * xc, axis=-1, keepdims=True) + EPS)
    return xc * r, r


def _ln_bwd(dy, xh, r, g):
    dxh = dy * g
    return r * (dxh - jnp.mean(dxh, axis=-1, keepdims=True) - xh * jnp.mean(dxh * xh, axis=-1, keepdims=True))


def _colsum(x):
    return jnp.sum(x, axis=0, keepdims=True)


def _tile(n, want):
    for t in range(min(n, want), 7, -1):
        if n % t == 0 and t % 8 == 0:
            return t
    return n


MM_TILE = 1024


def _matmul(a, b, *, name, mode="nn", residual=None, out_shards=1, out_dtype=F32):
    (m, k) = a.shape
    b_shards = b.shape[0] if b.ndim == 3 else 1
    if mode == "nn":
        n = b.shape[-1] * b_shards
        tn, tk = _tile(n // max(b_shards, out_shards), MM_TILE), _tile(k, MM_TILE)
    else:
        n = b.shape[-2]
        tn, tk = _tile(n // out_shards, MM_TILE), _tile(k // b_shards, MM_TILE)
    tm = _tile(m, MM_TILE)
    nk = k // tk
    a_spec = pl.BlockSpec((tm, tk), lambda i, j, kk: (i, kk))
    if mode == "nn":
        dims = _NN
        if b_shards == 1:
            b_spec = pl.BlockSpec((tk, tn), lambda i, j, kk: (kk, j))
        else:
            per_b = n // b_shards // tn
            b_spec = pl.BlockSpec((None, tk, tn), lambda i, j, kk: (j // per_b, kk, j % per_b))
    else:
        dims = _NT
        if b_shards == 1:
            b_spec = pl.BlockSpec((tn, tk), lambda i, j, kk: (j, kk))
        else:
            per_b = k // b_shards // tk
            b_spec = pl.BlockSpec((None, tn, tk), lambda i, j, kk: (kk // per_b, j, kk % per_b))
    if out_shards == 1:
        o_spec = pl.BlockSpec((tm, tn), lambda i, j, kk: (i, j))
        o_shape = (m, n)
    else:
        per_o = n // out_shards // tn
        o_spec = pl.BlockSpec((None, tm, tn), lambda i, j, kk: (j // per_o, i, j % per_o))
        o_shape = (out_shards, m, n // out_shards)
    has_res = residual is not None

    def body(a_ref, b_ref, *rest):
        o_ref = rest[-1]
        kk = pl.program_id(2)
        part = _dot(a_ref[...].astype(BF16), b_ref[...].astype(BF16), dims)
        if has_res:
            @pl.when(kk == 0)
            def _():
                o_ref[...] = part + rest[0][...]
        else:
            @pl.when(kk == 0)
            def _():
                o_ref[...] = part.astype(out_dtype)

        if nk > 1:
            @pl.when(kk > 0)
            def _():
                o_ref[...] += part

    assert out_dtype == F32 or nk == 1
    return pl.pallas_call(
        body, name=name, grid=(m // tm, n // tn, nk),
        in_specs=[a_spec, b_spec] + ([o_spec] if has_res else []),
        out_specs=o_spec, out_shape=jax.ShapeDtypeStruct(o_shape, out_dtype),
        compiler_params=pltpu.CompilerParams(dimension_semantics=("parallel", "parallel", "arbitrary")),
    )(a, b, *([residual] if has_res else []))


def _matmul_sum(pairs, *, name):
    m, n = pairs[0][0].shape[0], pairs[0][1].shape[0]
    tm, tn = _tile(m, MM_TILE // 2), _tile(n, MM_TILE)
    n_pairs = len(pairs)

    def body(*refs):
        acc = None
        for p in range(n_pairs):
            part = _dot(refs[2 * p][...].astype(BF16), refs[2 * p + 1][...].astype(BF16), _NT)
            acc = part if acc is None else acc + part
        refs[-1][...] = acc

    in_specs = []
    for a, b in pairs:
        in_specs += [pl.BlockSpec((tm, a.shape[1]), lambda i, j: (i, 0)), pl.BlockSpec((tn, b.shape[1]), lambda i, j: (j, 0))]
    return pl.pallas_call(
        body, name=name, grid=(m // tm, n // tn), in_specs=in_specs, out_specs=pl.BlockSpec((tm, tn), lambda i, j: (i, j)),
        out_shape=jax.ShapeDtypeStruct((m, n), F32),
        compiler_params=pltpu.CompilerParams(dimension_semantics=("parallel", "parallel")),
    )(*[x for pair in pairs for x in pair])


def _rows(fn, *, name, steps, ins, outs, accs=(), scratch=()):
    ni, no, na = len(ins), len(outs), len(accs)

    def body(*refs):
        in_refs, out_refs = refs[:ni], refs[ni:ni + no]
        acc_refs, scr = refs[ni + no:ni + no + na], refs[ni + no + na:]
        i = pl.program_id(0)

        @pl.when(i == 0)
        def _():
            for r in acc_refs:
                r[...] = jnp.zeros(r.shape, r.dtype)

        fn(i, in_refs, out_refs, acc_refs, scr)

    def full(shape):
        nd = len(shape)
        return pl.BlockSpec(tuple(shape), lambda i: (0,) * nd)

    res = pl.pallas_call(
        body, name=name, grid=(steps,),
        in_specs=[pl.BlockSpec(bs, im) for _, bs, im in ins],
        out_specs=[pl.BlockSpec(bs, im) for _, _, bs, im in outs] + [full(s) for s, _ in accs],
        out_shape=[jax.ShapeDtypeStruct(s, d) for s, d, _, _ in outs] + [jax.ShapeDtypeStruct(s, d) for s, d in accs],
        scratch_shapes=list(scratch),
        compiler_params=pltpu.CompilerParams(dimension_semantics=("arbitrary",)),
    )(*[a for a, _, _ in ins])
    return res


def _rb(arr, bm, cb=0, width=None):
    w = arr.shape[1] if width is None else width
    return (arr, (bm, w), lambda i: (i, cb))


def _const(arr):
    nd = arr.ndim
    return (arr, tuple(arr.shape), lambda i: (0,) * nd)


def _ro(t, w, dtype, bm):
    return ((t, w), dtype, (bm, w), lambda i: (i, 0))


def _rot(t, w, dtype, bm):
    return ((w, t), dtype, (w, bm), lambda i: (0, i))


def _rmsnorm(x, g, *, name, bm=512):
    t, d = x.shape
    bm = _tile(t, bm)

    def fn(i, ins, outs, accs, scr):
        h = _rms_fwd(ins[0][...], ins[1][...])
        outs[0][...] = h.astype(BF16)
        outs[1][...] = h.T.astype(BF16)

    return _rows(fn, name=name, steps=t // bm, ins=[_rb(x, bm), _const(g)], outs=[_ro(t, d, BF16, bm), _rot(t, d, BF16, bm)])


def _rmsnorm_bwd(dh, x, g, dres, *, name, bm=512):
    t, d = x.shape
    bm = _tile(t, bm)

    def fn(i, ins, outs, accs, scr):
        dx, dgrow = _rms_bwd(ins[0][...], ins[1][...], ins[2][...])
        outs[0][...] = ins[3][...] + dx
        accs[0][...] += _colsum(dgrow)

    return _rows(fn, name=name, steps=t // bm, ins=[_rb(dh, bm), _rb(x, bm), _const(g), _rb(dres, bm)],
                 outs=[_ro(t, d, F32, bm)], accs=[((1, d), F32)])


def _gate(o, p, gcb, *, name, bm=512):
    t, w = o.shape
    bm = _tile(t, bm)

    def fn(i, ins, outs, accs, scr):
        y = ins[0][...] * _silu(ins[1][...])
        outs[0][...] = y.astype(BF16)
        outs[1][...] = y.T.astype(BF16)

    return _rows(fn, name=name, steps=t // bm, ins=[_rb(o, bm), _rb(p, bm, gcb, w)],
                 outs=[_ro(t, w, BF16, bm), _rot(t, w, BF16, bm)])


def _gate_bwd(dy, o, p, gcb, *, name, bm=512):
    t, w = o.shape
    bm = _tile(t, bm)

    def fn(i, ins, outs, accs, scr):
        dy_, o_, g_ = ins[0][...], ins[1][...], ins[2][...]
        outs[0][...] = dy_ * _silu(g_)
        outs[1][...] = dy_ * o_ * _dsilu(g_)

    return _rows(fn, name=name, steps=t // bm, ins=[_rb(dy, bm), _rb(o, bm), _rb(p, bm, gcb, w)],
                 outs=[_ro(t, w, F32, bm), _ro(t, w, F32, bm)])


def _loss_head(x, g, tgt, *, name, bm=512):
    t, d = x.shape
    bm = _tile(t, bm)

    def fn(i, ins, outs, accs, scr):
        x_, g_, tg = ins[0][...], ins[1][...], ins[2][...]
        err = _rms_fwd(x_, g_) - tg
        part = 0.5 * jnp.sum(jnp.sum(err * err, axis=-1, keepdims=True), axis=0, keepdims=True) / d
        dx, dgrow = _rms_bwd(err / d, x_, g_)
        outs[0][...] = dx
        accs[0][...] += _colsum(dgrow)
        accs[1][...] += jnp.broadcast_to(part, (1, BLK))

    return _rows(fn, name=name, steps=t // bm, ins=[_rb(x, bm), _const(g), _rb(tgt, bm)],
                 outs=[_ro(t, d, F32, bm)], accs=[((1, d), F32), ((1, BLK), F32)])


def _gmlp_mix_weights(ws_ref, g):
    row = lax.broadcasted_iota(jnp.int32, (BLK, BLK), 0)
    col = lax.broadcasted_iota(jnp.int32, (BLK, BLK), 1)
    tril = col <= row
    return jnp.where(tril, ws_ref[g], 0.0), tril


def _gmlp_fwd(p, ln_g, ln_b, w_s, bs_t, *, name):
    t = p.shape[0]

    def fn(i, ins, outs, accs, scr):
        p_ref, lg, lb, ws_ref, bst = ins
        vn = _ln_stats(_gelu(p_ref[:, GM_W:2 * GM_W]))[0] * lg[...] + lb[...]
        for g in range(GM_G):
            cs = slice(g * BLK, (g + 1) * BLK)
            wt, _ = _gmlp_mix_weights(ws_ref, g)
            s = _dot(wt.astype(BF16), vn[:, cs].astype(BF16)) + bst[:, g:g + 1]
            u = _gelu(p_ref[:, cs])
            gate = p_ref[:, 2 * GM_W + g * BLK:2 * GM_W + (g + 1) * BLK]
            y = u * s * _silu(gate)
            outs[0][:, cs] = y.astype(BF16)
            outs[1][cs, :] = y.T.astype(BF16)

    return _rows(fn, name=name, steps=t // BLK, ins=[_rb(p, BLK), _const(ln_g), _const(ln_b), _const(w_s), _const(bs_t)],
                 outs=[_ro(t, GM_W, BF16, BLK), _rot(t, GM_W, BF16, BLK)])


def _gmlp_bwd(dy, p, ln_g, ln_b, w_s, bs_t, *, name):
    t = p.shape[0]

    def fn(i, ins, outs, accs, scr):
        dy_ref, p_ref, lg, lb, ws_ref, bst = ins
        dp_ref = outs[0]
        dlg, dlb, dws, dbst = accs
        dvn_ref = scr[0]
        v_pre = p_ref[:, GM_W:2 * GM_W]
        xh, r = _ln_stats(_gelu(v_pre))
        vn = xh * lg[...] + lb[...]
        for g in range(GM_G):
            cs = slice(g * BLK, (g + 1) * BLK)
            gs = slice(2 * GM_W + g * BLK, 2 * GM_W + (g + 1) * BLK)
            wt, tril = _gmlp_mix_weights(ws_ref, g)
            vg = vn[:, cs].astype(BF16)
            s = _dot(wt.astype(BF16), vg) + bst[:, g:g + 1]
            u_pre, gate, dyg = p_ref[:, cs], p_ref[:, gs], dy_ref[:, cs]
            u = _gelu(u_pre)
            dos = dyg * _silu(gate)
            dp_ref[:, gs] = dyg * u * s * _dsilu(gate)
            dp_ref[:, cs] = dos * s * _dgelu(u_pre)
            ds = (dos * u).astype(BF16)
            dws[g] += jnp.where(tril, _dot(ds, vg, _NT), 0.0)
            dbst[:, g:g + 1] += jnp.sum(dos * u, axis=1, keepdims=True)
            dvn_ref[:, cs] = _dot(wt.astype(BF16), ds, _TN)
        dvn = dvn_ref[...]
        dlg[...] += _colsum(dvn * xh)
        dlb[...] += _colsum(dvn)
        dp_ref[:, GM_W:2 * GM_W] = _ln_bwd(dvn, xh, r, lg[...]) * _dgelu(v_pre)

    return _rows(fn, name=name, steps=t // BLK,
                 ins=[_rb(dy, BLK), _rb(p, BLK), _const(ln_g), _const(ln_b), _const(w_s), _const(bs_t)],
                 outs=[_ro(t, 3 * GM_W, F32, BLK)],
                 accs=[((1, GM_W), F32), ((1, GM_W), F32), ((GM_G, BLK, BLK), F32), ((BLK, GM_G), F32)],
                 scratch=[pltpu.VMEM((BLK, GM_W), F32)])


CV_BM = 128
CV_RC = 8
SUBLANES = 8
CV_FWD_OFFS = [HALO - (CV_K - 1) + k for k in range(CV_K)]
CV_BWD_OFFS = [CV_K - 1 - k for k in range(CV_K)]


def _conv_halo_prev(p, cb, bm):
    per = bm // HALO
    return (p, (HALO, CV_W), lambda i: (jnp.maximum(i * per - 1, 0), cb))


def _conv_scratch(bm):
    return [pltpu.VMEM((bm + HALO, CV_W), F32), pltpu.VMEM((SUBLANES - 1, bm + HALO - SUBLANES, CV_W), F32),
            pltpu.VMEM((bm, CV_W), F32)]


def _conv_shift_copies(ext_ref, sh_ref):
    rows = sh_ref.shape[1]
    for b in range(1, SUBLANES):
        sh_ref[b - 1] = ext_ref[pl.ds(b, rows), :]


def _conv_window(ext_ref, sh_ref, off, r0, rows):
    b = off % SUBLANES
    src = ext_ref if b == 0 else sh_ref.at[b - 1]
    return src[pl.ds(r0 + (off - b), rows), :]


def _conv_taps(ext_ref, sh_ref, cw_ref, y_ref, offs):
    bm = y_ref.shape[0]

    def chunk(ci, c):
        r0 = pl.multiple_of(ci * CV_RC, CV_RC)
        acc = jnp.zeros((CV_RC, CV_W), F32)
        for k in range(CV_K):
            acc = acc + cw_ref[pl.ds(k * SUBLANES, CV_RC), :] * _conv_window(ext_ref, sh_ref, offs[k], r0, CV_RC)
        y_ref[pl.ds(r0, CV_RC), :] = acc
        return c

    lax.fori_loop(0, bm // CV_RC, chunk, 0)


def _conv_dweights(dy1_ref, ext_ref, sh_ref, dcw_ref):
    bm = dy1_ref.shape[0]
    groups = 4
    for k in range(CV_K):
        def step(ci, acc, off=CV_FWD_OFFS[k]):
            prods = []
            for u in range(groups):
                r0 = pl.multiple_of((ci * groups + u) * CV_RC, CV_RC)
                prods.append(dy1_ref[pl.ds(r0, CV_RC), :] * _conv_window(ext_ref, sh_ref, off, r0, CV_RC))
            return acc + ((prods[0] + prods[1]) + (prods[2] + prods[3]))

        dcw_ref[k:k + 1, :] += _colsum(lax.fori_loop(0, bm // (CV_RC * groups), step, jnp.zeros((CV_RC, CV_W), F32)))


def _conv_fill(i, ext_ref, a_prev, b_prev, a, b, bm, seq):
    keep = jnp.where((i % (seq // bm)) == 0, 0.0, 1.0)
    ext_ref[pl.ds(0, HALO), :] = keep * (a_prev * _sigmoid(b_prev))
    ext_ref[pl.ds(HALO, bm), :] = a * _sigmoid(b)


def _conv_fwd(p, cw, cb, ln_g, ln_b, seq, *, name, bm=CV_BM):
    t = p.shape[0]

    def fn(i, ins, outs, accs, scr):
        a, b, gate, ap, bp = [r[...] for r in ins[:5]]
        cw_ref, cb_, lg, lb = ins[5], ins[6][...], ins[7][...], ins[8][...]
        ext, sh, y = scr
        _conv_fill(i, ext, ap, bp, a, b, bm, seq)
        _conv_shift_copies(ext, sh)
        _conv_taps(ext, sh, cw_ref, y, CV_FWD_OFFS)
        y2 = _ln_stats(y[...] + cb_)[0] * lg + lb
        out = _silu(y2) * _silu(gate)
        outs[0][...] = out.astype(BF16)
        outs[1][...] = out.T.astype(BF16)

    return _rows(fn, name=name, steps=t // bm,
                 ins=[_rb(p, bm, 0, CV_W), _rb(p, bm, 1, CV_W), _rb(p, bm, 2, CV_W),
                      _conv_halo_prev(p, 0, bm), _conv_halo_prev(p, 1, bm),
                      _const(cw), _const(cb), _const(ln_g), _const(ln_b)],
                 outs=[_ro(t, CV_W, BF16, bm), _rot(t, CV_W, BF16, bm)], scratch=_conv_scratch(bm))


def _conv_bwd_post(dy, p, cw, cb, ln_g, ln_b, seq, *, name, bm=CV_BM):
    t = p.shape[0]

    def fn(i, ins, outs, accs, scr):
        dy_, a, b, gate, ap, bp = [r[...] for r in ins[:6]]
        cw_ref, cb_, lg, lb = ins[6], ins[7][...], ins[8][...], ins[9][...]
        dlg, dlb, dcb, dcw = accs
        ext, sh, y = scr
        _conv_fill(i, ext, ap, bp, a, b, bm, seq)
        _conv_shift_copies(ext, sh)
        _conv_taps(ext, sh, cw_ref, y, CV_FWD_OFFS)
        xh, r = _ln_stats(y[...] + cb_)
        y2 = xh * lg + lb
        outs[1][...] = dy_ * _silu(y2) * _dsilu(gate)
        dy2 = dy_ * _silu(gate) * _dsilu(y2)
        dlg[...] += _colsum(dy2 * xh)
        dlb[...] += _colsum(dy2)
        dy1 = _ln_bwd(dy2, xh, r, lg)
        outs[0][...] = dy1
        dcb[...] += _colsum(dy1)
        _conv_dweights(outs[0], ext, sh, dcw)

    return _rows(fn, name=name, steps=t // bm,
                 ins=[_rb(dy, bm), _rb(p, bm, 0, CV_W), _rb(p, bm, 1, CV_W), _rb(p, bm, 2, CV_W),
                      _conv_halo_prev(p, 0, bm), _conv_halo_prev(p, 1, bm),
                      _const(cw), _const(cb), _const(ln_g), _const(ln_b)],
                 outs=[_ro(t, CV_W, F32, bm), _ro(t, CV_W, F32, bm)],
                 accs=[((1, CV_W), F32), ((1, CV_W), F32), ((1, CV_W), F32), ((CV_K, CV_W), F32)],
                 scratch=_conv_scratch(bm))


def _conv_bwd_pre(dy1, dgate, p, cw, seq, *, name, bm=CV_BM):
    t = p.shape[0]
    per = bm // HALO
    last_halo = t // HALO - 1

    def fn(i, ins, outs, accs, scr):
        d1, d1n, dg, a, b = [r[...] for r in ins[:5]]
        ext, sh, y = scr
        keep = jnp.where((i % (seq // bm)) == (seq // bm - 1), 0.0, 1.0)
        ext[pl.ds(0, bm), :] = d1
        ext[pl.ds(bm, HALO), :] = keep * d1n
        _conv_shift_copies(ext, sh)
        _conv_taps(ext, sh, ins[5], y, CV_BWD_OFFS)
        dy0 = y[...]
        sb = _sigmoid(b)
        outs[0][:, 0:CV_W] = dy0 * sb
        outs[0][:, CV_W:2 * CV_W] = dy0 * a * sb * (1.0 - sb)
        outs[0][:, 2 * CV_W:3 * CV_W] = dg

    return _rows(fn, name=name, steps=t // bm,
                 ins=[_rb(dy1, bm), (dy1, (HALO, CV_W), lambda i: (jnp.minimum((i + 1) * per, last_halo), 0)),
                      _rb(dgate, bm), _rb(p, bm, 0, CV_W), _rb(p, bm, 1, CV_W), _const(cw)],
                 outs=[_ro(t, 3 * CV_W, F32, bm)], scratch=_conv_scratch(bm))[0]


def _iotas():
    row = lax.broadcasted_iota(jnp.int32, (BLK, BLK), 0)
    col = lax.broadcasted_iota(jnp.int32, (BLK, BLK), 1)
    return row, col


def _heads(x, head0):
    if head0.shape != x.shape:
        head0 = lax.broadcasted_iota(jnp.int32, x.shape, 1) < HEAD_DIM
    return jnp.where(head0, x, 0.0).astype(BF16), jnp.where(head0, 0.0, x).astype(BF16)


def _pair_spec(seq, off):
    return pl.BlockSpec((seq, BLK), lambda b, hp: (b, off + hp))


def _stat_spec(seq):
    return pl.BlockSpec((None, None, seq, BLK), lambda b, hp: (b, hp, 0, 0))


_ATT_PARAMS = dict(compiler_params=pltpu.CompilerParams(dimension_semantics=("parallel", "parallel")))
_SCALE = 1.0 / math.sqrt(HEAD_DIM)


Q_BLOCK = 256
KEY_BLOCK = 256


def _stack_heads(x, head0, scale=None):
    if scale is not None:
        x = x * scale
    return jnp.concatenate(_heads(x, head0), axis=0)


def _pair_cols(x, head0, fill):
    a = jnp.max(jnp.where(head0, x, fill), axis=1, keepdims=True)
    b = jnp.max(jnp.where(head0, fill, x), axis=1, keepdims=True)
    return jnp.concatenate([a, b], axis=0)


def _causal_mask(t0, s0, tq, kw, inclusive):
    row = lax.broadcasted_iota(jnp.int32, (2 * tq, kw), 0) & (tq - 1)
    col = lax.broadcasted_iota(jnp.int32, (2 * tq, kw), 1)
    return (s0 + col) <= (t0 + row) if inclusive else (s0 + col) < (t0 + row)


def _sub(x, j):
    return x[:, j * BLK:(j + 1) * BLK]


def _tri_blocks(kw, relation):
    r = lax.broadcasted_iota(jnp.int32, (kw, kw), 0)
    c = lax.broadcasted_iota(jnp.int32, (kw, kw), 1)
    return (((r // BLK) == (c // BLK)) & relation(r, c)).astype(BF16)


def _block_cumsum(x, tri, ksub):
    hi = x.astype(BF16)
    lo = (x - hi.astype(F32)).astype(BF16)
    cs = _dot(jnp.concatenate([hi, lo], axis=0), tri)
    n = x.shape[0]
    cs = cs[:n] + cs[n:]
    return [_sub(cs, j) for j in range(ksub)], [jnp.sum(_sub(x, j), axis=1, keepdims=True) for j in range(ksub)]


def _sb_terms_z(z, mask):
    t = jnp.log(1.0 + jnp.exp(-jnp.abs(z)))
    lsz = jnp.minimum(z, 0.0) - t
    lr = lsz - z
    if mask is not None:
        lr = jnp.where(mask, lr, 0.0)
    return lsz, lr


def _sb_fwd(p, nb, seq, *, name):
    tq = min(Q_BLOCK, seq)
    nq = seq // tq
    kw = min(KEY_BLOCK, seq)
    ksub = kw // BLK

    def body(q_ref, k_ref, v_ref, o_ref, tot_ref):
        row, col = _iotas()
        colq = lax.broadcasted_iota(jnp.int32, (tq, BLK), 1)
        head0 = colq < HEAD_DIM
        upper = _tri_blocks(kw, lambda j, s: j > s)

        def qblock(qb, c):
            t0 = pl.multiple_of(qb * tq, tq)
            qs = _stack_heads(q_ref[pl.ds(t0, tq), :], head0, _SCALE)
            diag = (t0 + tq - 1) // kw

            def kblock(kb, carry, masked):
                acc, run = carry
                s0 = pl.multiple_of(kb * kw, kw)
                k = k_ref[pl.ds(s0, kw), :].astype(BF16)
                v0, v1 = _heads(v_ref[pl.ds(s0, kw), :], head0)
                mask = _causal_mask(t0, s0, tq, kw, False)[:tq] if masked else None
                zs = [_dot(qs[h * tq:(h + 1) * tq], k, _NT) for h in range(2)]
                terms = []
                for h in range(2):
                    lsz, lr = _sb_terms_z(zs[h], mask)
                    terms.append((lsz,) + _block_cumsum(lr, upper, ksub))
                runs = []
                for h, vh in enumerate((v0, v1)):
                    lsz, after, total = terms[h]
                    r = run[h]
                    ws = [None] * ksub
                    for j in reversed(range(ksub)):
                        w = jnp.exp(_sub(lsz, j) + after[j] + r)
                        if masked:
                            w = jnp.where(_sub(mask, j), w, 0.0)
                        ws[j] = w.astype(BF16)
                        r = r + total[j]
                    acc = acc + _dot(jnp.concatenate(ws, axis=1), vh)
                    runs.append(r)
                return acc, tuple(runs)

            zc = jnp.zeros((tq, 1), F32)
            carry = kblock(diag, (jnp.zeros((tq, BLK), F32), (zc, zc)), True)
            acc, run = lax.fori_loop(0, diag, lambda it, cr: kblock(diag - 1 - it, cr, False), carry)
            o_ref[pl.ds(t0, tq), :] = acc
            tot_ref[pl.ds(t0, tq), :] = jnp.where(head0, run[0], run[1])
            return c

        lax.fori_loop(0, nq, qblock, 0)

    return pl.pallas_call(
        body, name=name, grid=(nb, PAIRS),
        in_specs=[_pair_spec(seq, 0), _pair_spec(seq, PAIRS), _pair_spec(seq, 2 * PAIRS)],
        out_specs=[_pair_spec(seq, 0), _stat_spec(seq)],
        out_shape=[jax.ShapeDtypeStruct((nb * seq, D_MODEL), F32), jax.ShapeDtypeStruct((nb, PAIRS, seq, BLK), F32)],
        **_ATT_PARAMS,
    )(p, p, p)


def _sb_bwd(p, do, tot, nb, seq, *, name):
    tq = min(Q_BLOCK, seq)
    nq = seq // tq
    kw = min(KEY_BLOCK, seq)
    ksub = kw // BLK

    def body(q_ref, k_ref, v_ref, do_ref, tot_ref, dq_ref, dk_ref, dv_ref):
        row, col = _iotas()
        colq = lax.broadcasted_iota(jnp.int32, (tq, BLK), 1)
        head0 = colq < HEAD_DIM
        lower_incl = _tri_blocks(kw, lambda j, s: j <= s)
        lower_strict = _tri_blocks(kw, lambda s, j: s < j)
        dk_ref[...] = jnp.zeros(dk_ref.shape, F32)
        dv_ref[...] = jnp.zeros(dv_ref.shape, F32)

        def qblock(qb, c):
            t0 = pl.multiple_of(qb * tq, tq)
            qs = _stack_heads(q_ref[pl.ds(t0, tq), :], head0, _SCALE)
            dos = _stack_heads(do_ref[pl.ds(t0, tq), :], head0)
            tot = tot_ref[pl.ds(t0, tq), :]
            swapped = pltpu.roll(tot, HEAD_DIM, 1)
            tts = (jnp.where(head0, tot, swapped), jnp.where(head0, swapped, tot))
            diag = (t0 + tq - 1) // kw

            def kblock(kb, carry, masked):
                dq, pfs, efs = carry
                s0 = pl.multiple_of(kb * kw, kw)
                kf = k_ref[pl.ds(s0, kw), :]
                k = kf.astype(BF16)
                khs = _heads(kf, head0)
                v = v_ref[pl.ds(s0, kw), :].astype(BF16)
                mask = _causal_mask(t0, s0, tq, kw, False)[:tq] if masked else None
                zs = [_dot(qs[h * tq:(h + 1) * tq], k, _NT) for h in range(2)]
                dws = [_dot(dos[h * tq:(h + 1) * tq], v, _NT) for h in range(2)]
                first = []
                for h in range(2):
                    lsz, lr = _sb_terms_z(zs[h], None)
                    lrm = jnp.where(mask, lr, 0.0) if masked else lr
                    first.append((lsz, lr) + _block_cumsum(lrm, lower_incl, ksub))
                second, pfs_out = [], []
                for h in range(2):
                    lsz, lr, incl, total = first[h]
                    pf = pfs[h]
                    ws, ews = [], []
                    for j in range(ksub):
                        w = jnp.exp(_sub(lsz, j) + (tts[h] - pf - incl[j]))
                        if masked:
                            w = jnp.where(_sub(mask, j), w, 0.0)
                        pf = pf + total[j]
                        ws.append(w.astype(BF16))
                        ews.append(_sub(dws[h], j) * w)
                    pfs_out.append(pf)
                    second.append((ws, ews) + _block_cumsum(jnp.concatenate(ews, axis=1), lower_strict, ksub))
                dz_h, efs_out = [], []
                for h in range(2):
                    lsz, lr = first[h][:2]
                    ws, ews, before, etotal = second[h]
                    ef = efs[h]
                    dzs = []
                    for j in range(ksub):
                        dz = ews[j] * jnp.exp(_sub(lr, j)) - (ef + before[j]) * jnp.exp(_sub(lsz, j))
                        ef = ef + etotal[j]
                        if masked:
                            dz = jnp.where(_sub(mask, j), dz, 0.0)
                        dzs.append(dz.astype(BF16))
                    efs_out.append(ef)
                    dz_h.append(jnp.concatenate(dzs, axis=1))
                    dq = dq + _dot(dz_h[h], khs[h])
                w = jnp.concatenate([jnp.concatenate(second[h][0], axis=1) for h in range(2)], axis=0)
                dk_ref[pl.ds(s0, kw), :] += _dot(jnp.concatenate(dz_h, axis=0), qs, _TN)
                dv_ref[pl.ds(s0, kw), :] += _dot(w, dos, _TN)
                return dq, tuple(pfs_out), tuple(efs_out)

            zc = jnp.zeros((tq, 1), F32)
            carry = lax.fori_loop(0, diag, lambda kb, cr: kblock(kb, cr, False), (jnp.zeros((tq, BLK), F32), (zc, zc), (zc, zc)))
            dq_ref[pl.ds(t0, tq), :] = kblock(diag, carry, True)[0] * _SCALE
            return c

        lax.fori_loop(0, nq, qblock, 0)

    t = nb * seq
    return pl.pallas_call(
        body, name=name, grid=(nb, PAIRS),
        in_specs=[_pair_spec(seq, 0), _pair_spec(seq, PAIRS), _pair_spec(seq, 2 * PAIRS), _pair_spec(seq, 0), _stat_spec(seq)],
        out_specs=[_pair_spec(seq, 0)] * 3,
        out_shape=[jax.ShapeDtypeStruct((t, D_MODEL), F32)] * 3,
        **_ATT_PARAMS,
    )(p, p, p, do, tot)


def _fox_cum(f, bf, nb, seq, *, name):
    def body(f_ref, bf_ref, cc_ref, cr_ref):
        row, col = _iotas()
        lower = (col <= row).astype(BF16)
        carry = jnp.zeros((1, BLK), F32)
        for blk in range(seq // BLK):
            rs = slice(blk * BLK, (blk + 1) * BLK)
            lf = jnp.where(col < HEADS, _log_sigmoid(f_ref[rs, :] + bf_ref[...]), 0.0)
            cc = _dot3_left(lower, lf) + carry
            cc_ref[rs, :] = cc
            cr_ref[:, rs] = cc.T[0:HEADS, :]
            carry = carry + _colsum(lf)

    return pl.pallas_call(
        body, name=name, grid=(nb,),
        in_specs=[pl.BlockSpec((seq, BLK), lambda b: (b, 0)), pl.BlockSpec((1, BLK), lambda b: (0, 0))],
        out_specs=[pl.BlockSpec((seq, BLK), lambda b: (b, 0)), pl.BlockSpec((None, HEADS, seq), lambda b: (b, 0, 0))],
        out_shape=[jax.ShapeDtypeStruct((nb * seq, BLK), F32), jax.ShapeDtypeStruct((nb, HEADS, seq), F32)],
        compiler_params=pltpu.CompilerParams(dimension_semantics=("parallel",)),
    )(f, bf)


def _fox_cum_bwd(dcr, dcc, f, bf, nb, seq, *, name):
    def body(dcr_ref, dcc_ref, f_ref, bf_ref, df_ref, dbf_ref):
        row, col = _iotas()
        upper_incl = (col >= row).astype(BF16)

        @pl.when(pl.program_id(0) == 0)
        def _():
            dbf_ref[...] = jnp.zeros((1, BLK), F32)

        carry = jnp.zeros((1, BLK), F32)
        for blk in reversed(range(seq // BLK)):
            rs = slice(blk * BLK, (blk + 1) * BLK)
            dc = dcr_ref[:, rs].T + dcc_ref[rs, :]
            dlf = _dot3_left(upper_incl, dc) + carry
            carry = carry + _colsum(dc)
            fl = f_ref[rs, :] + bf_ref[...]
            df = jnp.where(col < HEADS, dlf * _sigmoid(-fl), 0.0)
            df_ref[rs, :] = df
            dbf_ref[...] += _colsum(df)

    return pl.pallas_call(
        body, name=name, grid=(nb,),
        in_specs=[pl.BlockSpec((None, BLK, seq), lambda b: (b, 0, 0)), pl.BlockSpec((seq, BLK), lambda b: (b, 0)),
                  pl.BlockSpec((seq, BLK), lambda b: (b, 0)), pl.BlockSpec((1, BLK), lambda b: (0, 0))],
        out_specs=[pl.BlockSpec((seq, BLK), lambda b: (b, 0)), pl.BlockSpec((1, BLK), lambda b: (0, 0))],
        out_shape=[jax.ShapeDtypeStruct((nb * seq, BLK), F32), jax.ShapeDtypeStruct((1, BLK), F32)],
        compiler_params=pltpu.CompilerParams(dimension_semantics=("arbitrary",)),
    )(dcr, dcc, f, bf)


def _fox_cum_cols(cc_ref, t0, tq, colq, hp):
    cc = cc_ref[pl.ds(t0, tq), :]
    c0 = jnp.sum(jnp.where(colq == 2 * hp, cc, 0.0), axis=1, keepdims=True)
    c1 = jnp.sum(jnp.where(colq == 2 * hp + 1, cc, 0.0), axis=1, keepdims=True)
    return c0, c1


def _fox_fwd(p, cc, cr, nb, seq, *, name):
    tq = min(Q_BLOCK, seq)
    nq = seq // tq
    kw = min(KEY_BLOCK, seq)
    ksub = kw // BLK

    def body(q_ref, k_ref, v_ref, cc_ref, cr_ref, o_ref, lse_ref):
        hp = pl.program_id(1)
        row, col = _iotas()
        colq = lax.broadcasted_iota(jnp.int32, (tq, BLK), 1)
        head0 = colq < HEAD_DIM

        def qblock(qb, c):
            t0 = pl.multiple_of(qb * tq, tq)
            qs = _stack_heads(q_ref[pl.ds(t0, tq), :], head0, _SCALE)
            c0, c1 = _fox_cum_cols(cc_ref, t0, tq, colq, hp)
            diag = (t0 + tq - 1) // kw

            def kblock(kb, carry, masked):
                accs, ms = carry
                s0 = pl.multiple_of(kb * kw, kw)
                k = k_ref[pl.ds(s0, kw), :].astype(BF16)
                vf = v_ref[pl.ds(s0, kw), :]
                own0 = lax.broadcasted_iota(jnp.int32, vf.shape, 1) < HEAD_DIM
                vs = (jnp.where(own0, vf, 1.0).astype(BF16), jnp.where(own0, 1.0, vf).astype(BF16))
                mask = _causal_mask(t0, s0, tq, kw, True)[:tq] if masked else None
                zs = [_dot(qs[h * tq:(h + 1) * tq], k, _NT) for h in range(2)]
                parts = []
                for h, ch in enumerate((c0, c1)):
                    s = zs[h] + (ch - cr_ref[h:h + 1, pl.ds(s0, kw)])
                    if masked:
                        s = jnp.where(mask, s, -jnp.inf)
                    m_new = jnp.maximum(ms[h], jnp.max(s, axis=1, keepdims=True))
                    parts.append((jnp.exp(s - m_new).astype(BF16), jnp.exp(ms[h] - m_new), m_new))
                return (tuple(accs[h] * parts[h][1] + _dot(parts[h][0], vs[h]) for h in range(2)),
                        tuple(parts[h][2] for h in range(2)))

            zeros, ninf = jnp.zeros((tq, BLK), F32), jnp.full((tq, 1), -jnp.inf, F32)
            carry = lax.fori_loop(0, diag, lambda kb, cr: kblock(kb, cr, False), ((zeros, zeros), (ninf, ninf)))
            (acc0, acc1), (m0, m1) = kblock(diag, carry, True)
            l = jnp.where(head0, pltpu.roll(acc0, HEAD_DIM, 1), pltpu.roll(acc1, HEAD_DIM, 1))
            o_ref[pl.ds(t0, tq), :] = jnp.where(head0, acc0, acc1) / l
            lse_ref[pl.ds(t0, tq), :] = jnp.where(head0, m0, m1) + jnp.log(l)
            return c

        lax.fori_loop(0, nq, qblock, 0)

    return pl.pallas_call(
        body, name=name, grid=(nb, PAIRS),
        in_specs=[_pair_spec(seq, 0), _pair_spec(seq, PAIRS), _pair_spec(seq, 2 * PAIRS),
                  pl.BlockSpec((seq, BLK), lambda b, hp: (b, 0)), pl.BlockSpec((None, None, SUBLANES, seq), lambda b, hp: (b, hp, 0, 0))],
        out_specs=[_pair_spec(seq, 0), _stat_spec(seq)],
        out_shape=[jax.ShapeDtypeStruct((nb * seq, D_MODEL), F32), jax.ShapeDtypeStruct((nb, PAIRS, seq, BLK), F32)],
        **_ATT_PARAMS,
    )(p, p, p, cc, cr)


def _fox_bwd(p, do, o, lse, cc, cr, nb, seq, *, name):
    tq = min(Q_BLOCK, seq)
    nq = seq // tq
    kw = min(KEY_BLOCK, seq)
    ksub = kw // BLK

    def body(q_ref, k_ref, v_ref, do_ref, o_ref, lse_ref, cc_ref, cr_ref, dq_ref, dk_ref, dv_ref, dcr_ref, dcc_ref):
        hp = pl.program_id(1)
        row, col = _iotas()
        colq = lax.broadcasted_iota(jnp.int32, (tq, BLK), 1)
        head0 = colq < HEAD_DIM
        dk_ref[...] = jnp.zeros(dk_ref.shape, F32)
        dv_ref[...] = jnp.zeros(dv_ref.shape, F32)
        dcr_ref[...] = jnp.zeros(dcr_ref.shape, F32)

        @pl.when(hp == 0)
        def _():
            dcc_ref[...] = jnp.zeros(dcc_ref.shape, F32)

        def qblock(qb, c):
            t0 = pl.multiple_of(qb * tq, tq)
            qs = _stack_heads(q_ref[pl.ds(t0, tq), :], head0, _SCALE)
            dof = do_ref[pl.ds(t0, tq), :]
            dos = _stack_heads(dof, head0)
            prod = dof * o_ref[pl.ds(t0, tq), :]
            dl = jnp.concatenate([jnp.sum(jnp.where(head0, prod, 0.0), axis=1, keepdims=True),
                                  jnp.sum(jnp.where(head0, 0.0, prod), axis=1, keepdims=True)], axis=0)
            lse = _pair_cols(lse_ref[pl.ds(t0, tq), :], head0, -jnp.inf)
            c0, c1 = _fox_cum_cols(cc_ref, t0, tq, colq, hp)
            diag = (t0 + tq - 1) // kw

            def kblock(kb, carry, masked):
                dq, rs = carry
                s0 = pl.multiple_of(kb * kw, kw)
                kf = k_ref[pl.ds(s0, kw), :]
                k = kf.astype(BF16)
                k0, k1 = _heads(kf, head0)
                v = v_ref[pl.ds(s0, kw), :].astype(BF16)
                mask = _causal_mask(t0, s0, tq, kw, True)[:tq] if masked else None
                zs = [_dot(qs[h * tq:(h + 1) * tq], k, _NT) for h in range(2)]
                dps = [_dot(dos[h * tq:(h + 1) * tq], v, _NT) for h in range(2)]
                prs, dss, rss = [], [], []
                for h, (ch, kh) in enumerate(((c0, k0), (c1, k1))):
                    rows = slice(h * tq, (h + 1) * tq)
                    pr = jnp.exp(zs[h] + (ch - cr_ref[h:h + 1, pl.ds(s0, kw)]) - lse[rows])
                    if masked:
                        pr = jnp.where(mask, pr, 0.0)
                    ds = pr * (dps[h] - dl[rows])
                    dcr_ref[h:h + 1, pl.ds(s0, kw)] -= _colsum(ds)
                    rss.append(rs[rows] + jnp.sum(ds, axis=1, keepdims=True))
                    prs.append(pr.astype(BF16))
                    dss.append(ds.astype(BF16))
                    dq = dq + _dot(dss[h], kh)
                dk_ref[pl.ds(s0, kw), :] += _dot(jnp.concatenate(dss, axis=0), qs, _TN)
                dv_ref[pl.ds(s0, kw), :] += _dot(jnp.concatenate(prs, axis=0), dos, _TN)
                return dq, jnp.concatenate(rss, axis=0)

            init = (jnp.zeros((tq, BLK), F32), jnp.zeros((2 * tq, 1), F32))
            carry = lax.fori_loop(0, diag, lambda kb, cr: kblock(kb, cr, False), init)
            dq, rs = kblock(diag, carry, True)
            dq_ref[pl.ds(t0, tq), :] = dq * _SCALE
            dcc_ref[pl.ds(t0, tq), :] += jnp.where(colq == 2 * hp, rs[:tq], 0.0) + jnp.where(colq == 2 * hp + 1, rs[tq:], 0.0)
            return c

        lax.fori_loop(0, nq, qblock, 0)

    t = nb * seq
    return pl.pallas_call(
        body, name=name, grid=(nb, PAIRS),
        in_specs=[_pair_spec(seq, 0), _pair_spec(seq, PAIRS), _pair_spec(seq, 2 * PAIRS), _pair_spec(seq, 0), _pair_spec(seq, 0),
                  _stat_spec(seq), pl.BlockSpec((seq, BLK), lambda b, hp: (b, 0)),
                  pl.BlockSpec((None, None, SUBLANES, seq), lambda b, hp: (b, hp, 0, 0))],
        out_specs=[_pair_spec(seq, 0)] * 3 + [pl.BlockSpec((None, None, SUBLANES, seq), lambda b, hp: (b, hp, 0, 0)),
                                              pl.BlockSpec((seq, BLK), lambda b, hp: (b, 0))],
        out_shape=[jax.ShapeDtypeStruct((t, D_MODEL), F32)] * 3 + [jax.ShapeDtypeStruct((nb, PAIRS, SUBLANES, seq), F32),
                                                                     jax.ShapeDtypeStruct((t, BLK), F32)],
        compiler_params=pltpu.CompilerParams(dimension_semantics=("parallel", "arbitrary")),
    )(p, p, p, do, o, lse, cc, cr)


def _row_shards(x):
    return x.reshape(N_CHIPS, x.shape[0] // N_CHIPS, x.shape[1])


def _local_step(x3, tgt3, w, later=None, start_reduce=None):
    nb, seq, d = x3.shape
    t = nb * seq
    x0, tgt = x3.reshape(t, d), tgt3.reshape(t, d)
    g = {}

    a_gain = w["a_norm"].reshape(1, d)
    h_a, ht_a = _rmsnorm(x0, a_gain, name="a_norm_fwd")
    p_a = _matmul(h_a, w["a_w_in"][:3], name="a_in_fwd", out_dtype=BF16)
    gate_a = _matmul(h_a, w["a_w_in"][3], name="a_in_gate_fwd")
    o_a, tot_a = _sb_fwd(p_a, nb, seq, name="a_attn_fwd")
    y_a, yt_a = _gate(o_a, gate_a, 0, name="a_gate_fwd")
    x1 = _matmul(y_a, w["a_w_out"], name="a_out_fwd", residual=x0)

    if later:
        w = {**w, **later[0](x1)}
    b_gain = w["b_norm"].reshape(1, d)
    b_lg, b_lb = w["b_v_ln_g"].reshape(1, GM_W), w["b_v_ln_b"].reshape(1, GM_W)
    b_ws, b_bst = w["b_w_s"].reshape(GM_G, BLK, BLK), w["b_b_s"].reshape(GM_G, BLK).T
    h_b, ht_b = _rmsnorm(x1, b_gain, name="b_norm_fwd")
    p_b = _matmul(h_b, w["b_w_in"], name="b_in_fwd")
    y_b, yt_b = _gmlp_fwd(p_b, b_lg, b_lb, b_ws, b_bst, name="b_mix_fwd")
    x2 = _matmul(y_b, w["b_w_out"], name="b_out_fwd", residual=x1)

    if later:
        w = {**w, **later[1](x2)}
    c_gain = w["c_norm"].reshape(1, d)
    c_cw = jnp.repeat(w["c_conv_w"].reshape(CV_K, CV_W), SUBLANES, axis=0)
    c_cb = w["c_conv_b"].reshape(1, CV_W)
    c_lg, c_lb = w["c_ln_g"].reshape(1, CV_W), w["c_ln_b"].reshape(1, CV_W)
    h_c, ht_c = _rmsnorm(x2, c_gain, name="c_norm_fwd")
    p_c = _matmul(h_c, w["c_w_in"], name="c_in_fwd")
    y_c, yt_c = _conv_fwd(p_c, c_cw, c_cb, c_lg, c_lb, seq, name="c_conv_fwd")
    x3_ = _matmul(y_c, w["c_w_out"], name="c_out_fwd", residual=x2)

    d_gain = w["d_norm"].reshape(1, d)
    d_win = w["d_w_in"].reshape(d, 4 * D_MODEL + HEADS)
    d_wmain = d_win[:, :4 * D_MODEL]
    d_wf = jnp.pad(d_win[:, 4 * D_MODEL:], ((0, 0), (0, BLK - HEADS)))
    d_bf = jnp.pad(w["d_b_f"].reshape(1, HEADS), ((0, 0), (0, BLK - HEADS)))
    h_d, ht_d = _rmsnorm(x3_, d_gain, name="d_norm_fwd")
    p_d = _matmul(h_d, d_wmain[:, :3 * D_MODEL], name="d_in_fwd", out_dtype=BF16)
    gate_d = _matmul(h_d, d_wmain[:, 3 * D_MODEL:], name="d_in_gate_fwd")
    f_d = _matmul(h_d, d_wf, name="d_inf_fwd")
    cc, cr = _fox_cum(f_d, d_bf, nb, seq, name="d_cum_fwd")
    cr = jnp.pad(cr.reshape(nb, PAIRS, 2, seq), ((0, 0), (0, 0), (0, SUBLANES - 2), (0, 0)))
    o_d, lse_d = _fox_fwd(p_d, cc, cr, nb, seq, name="d_attn_fwd")
    y_d, yt_d = _gate(o_d, gate_d, 0, name="d_gate_fwd")
    x4 = _matmul(y_d, w["d_w_out"], name="d_out_fwd", residual=x3_)

    f_gain = w["final_norm"].reshape(1, d)
    dx, g_fn, loss_row = _loss_head(x4, f_gain, tgt, name="loss_head")
    g["final_norm"] = g_fn

    g["d_w_out"] = _row_shards(_matmul(yt_d, dx, name="d_out_dw"))
    dy = _matmul(dx, w["d_w_out"], name="d_out_dy", mode="nt")
    do_d, dg_d = _gate_bwd(dy, o_d, gate_d, 0, name="d_gate_bwd")
    dq, dk, dv, dcr, dcc = _fox_bwd(p_d, do_d, o_d, lse_d, cc, cr, nb, seq, name="d_attn_bwd")
    dcr = jnp.pad(dcr[:, :, :2, :].reshape(nb, HEADS, seq), ((0, 0), (0, BLK - HEADS), (0, 0)))
    df, dbf = _fox_cum_bwd(dcr, dcc, f_d, d_bf, nb, seq, name="d_cum_bwd")
    g["d_b_f"] = dbf[:, :HEADS]
    parts = [dq, dk, dv, dg_d]
    dws = [_matmul(ht_d, pt, name=f"d_in_dw{n}") for n, pt in enumerate(parts)]
    dwf = _matmul(ht_d, df, name="d_inf_dw")
    g["d_w_in"] = jnp.concatenate(dws + [dwf[:, :HEADS]], axis=1).reshape(d, N_CHIPS, -1).transpose(1, 0, 2)
    dh = _matmul_sum([(df, d_wf)] + [(pt, d_wmain[:, n * D_MODEL:(n + 1) * D_MODEL]) for n, pt in enumerate(parts)],
                     name="d_in_dh")
    dx, g["d_norm"] = _rmsnorm_bwd(dh, x3_, d_gain, dx, name="d_norm_bwd")

    g["c_w_out"] = _row_shards(_matmul(yt_c, dx, name="c_out_dw"))
    dy = _matmul(dx, w["c_w_out"], name="c_out_dy", mode="nt")
    dy1, dgate, g["c_ln_g"], g["c_ln_b"], g["c_conv_b"], g["c_conv_w"] = _conv_bwd_post(
        dy, p_c, c_cw, c_cb, c_lg, c_lb, seq, name="c_conv_bwd_post")
    dp = _conv_bwd_pre(dy1, dgate, p_c, c_cw, seq, name="c_conv_bwd_pre")
    g["c_w_in"] = _matmul(ht_c, dp, name="c_in_dw", out_shards=N_CHIPS)
    dh = _matmul(dp, w["c_w_in"], name="c_in_dh", mode="nt")
    dx, g["c_norm"] = _rmsnorm_bwd(dh, x2, c_gain, dx, name="c_norm_bwd")

    early, b_wout, a_wout = [], w["b_w_out"], w["a_w_out"]
    if start_reduce is not None:
        begun, token = start_reduce({n: g[n] for n in ("d_w_in", "d_w_out", "c_w_in", "c_w_out")}, "grads_cd")
        early.append(begun)
        b_wout = b_wout + token[0, 0].astype(b_wout.dtype)
    g["b_w_out"] = _row_shards(_matmul(yt_b, dx, name="b_out_dw"))
    dy = _matmul(dx, b_wout, name="b_out_dy", mode="nt")
    dp, g["b_v_ln_g"], g["b_v_ln_b"], g["b_w_s"], dbst = _gmlp_bwd(dy, p_b, b_lg, b_lb, b_ws, b_bst, name="b_mix_bwd")
    g["b_b_s"] = dbst.T
    g["b_w_in"] = _matmul(ht_b, dp, name="b_in_dw", out_shards=N_CHIPS)
    dh = _matmul(dp, w["b_w_in"], name="b_in_dh", mode="nt")
    dx, g["b_norm"] = _rmsnorm_bwd(dh, x1, b_gain, dx, name="b_norm_bwd")

    if start_reduce is not None:
        begun, token = start_reduce({n: g[n] for n in ("b_w_in", "b_w_out")}, "grads_b")
        early.append(begun)
        a_wout = a_wout + token[0, 0].astype(a_wout.dtype)
    g["a_w_out"] = _row_shards(_matmul(yt_a, dx, name="a_out_dw"))
    dy = _matmul(dx, a_wout, name="a_out_dy", mode="nt")
    do_a, dg_a = _gate_bwd(dy, o_a, gate_a, 0, name="a_gate_bwd")
    dq, dk, dv = _sb_bwd(p_a, do_a, tot_a, nb, seq, name="a_attn_bwd")
    parts = [dq, dk, dv, dg_a]
    g["a_w_in"] = jnp.stack([_matmul(ht_a, pt, name=f"a_in_dw{n}") for n, pt in enumerate(parts)])
    dh = _matmul_sum([(pt, w["a_w_in"][n]) for n, pt in enumerate(parts)], name="a_in_dh")
    dx, g["a_norm"] = _rmsnorm_bwd(dh, x0, a_gain, dx, name="a_norm_bwd")

    return loss_row[0, 0], dx.reshape(nb, seq, d), g, early


_HBM = pl.BlockSpec(memory_space=pltpu.HBM)


def _place():
    return lax.axis_index("x"), lax.axis_index("y"), lax.axis_index("c")


def _other_chips(x, y):
    return [(1 - x, y), (x, 1 - y), (1 - x, 1 - y)]


def _allgather_chips(ss, *, name):
    n_ops = len(ss)

    def body(*refs):
        s_refs, o_refs, (send_sems, recv_sems) = refs[:n_ops], refs[n_ops:2 * n_ops], refs[2 * n_ops:]
        x, y, c = _place()
        me = 2 * x + y
        chips = _other_chips(x, y)

        def copy(i, kk, src, dst, to):
            return pltpu.make_async_remote_copy(src_ref=src, dst_ref=dst, send_sem=send_sems.at[6 * i + kk],
                                                recv_sem=recv_sems.at[6 * i + kk], device_id=to, device_id_type=MESH)

        def half(i, j, hc):
            h = s_refs[i].shape[0] // 2
            return o_refs[i].at[j, pl.ds(hc * h, h), :]

        first = [copy(i, kk, s_refs[i].at[pl.ds(c * (s_refs[i].shape[0] // 2), s_refs[i].shape[0] // 2), :], half(i, me, c),
                      (cx, cy, c)) for kk, (cx, cy) in enumerate(chips) for i in range(n_ops)]
        for cp in first:
            cp.start()
        passed = []
        for kk, (cx, cy) in enumerate(chips):
            for i in range(n_ops):
                blk = half(i, 2 * cx + cy, c)
                copy(i, kk, blk, blk, (cx, cy, c)).wait_recv()
                fwd = copy(i, 3 + kk, blk, blk, (x, y, 1 - c))
                fwd.start()
                passed.append(fwd)
        for kk, (cx, cy) in enumerate(chips):
            for i in range(n_ops):
                blk = half(i, 2 * cx + cy, 1 - c)
                copy(i, 3 + kk, blk, blk, (x, y, 1 - c)).wait_recv()
        for cp in first + passed:
            cp.wait_send()

    for s in ss:
        assert s.shape[0] % 32 == 0, s.shape
    return pl.pallas_call(
        body, name=name, in_specs=[_HBM] * n_ops, out_specs=[_HBM] * n_ops,
        out_shape=[jax.ShapeDtypeStruct((N_CHIPS,) + s.shape, s.dtype) for s in ss],
        scratch_shapes=[pltpu.SemaphoreType.DMA((6 * n_ops,)), pltpu.SemaphoreType.DMA((6 * n_ops,))],
    )(*ss)


_SEM = pl.BlockSpec(memory_space=pltpu.SEMAPHORE)
_ANY = pl.BlockSpec(memory_space=pl.ANY)
_DATAFLOW = pltpu.SideEffectType.DATAFLOW_SIDE_EFFECTING


def _chip_copies(s_refs, land_refs, send_sems, recv_sems):
    x, y, c = _place()
    me = 2 * x + y
    cps = []
    for i, (s_ref, land_ref) in enumerate(zip(s_refs, land_refs)):
        h = s_ref.shape[0] // 2
        for kk, (cx, cy) in enumerate(_other_chips(x, y)):
            cps.append(pltpu.make_async_remote_copy(
                src_ref=s_ref.at[pl.ds(c * h, h), :], dst_ref=land_ref.at[me, pl.ds(c * h, h), :], send_sem=send_sems.at[3 * i + kk],
                recv_sem=recv_sems.at[3 * i + kk], device_id=(cx, cy, c), device_id_type=MESH))
    return cps


def _gather_start(ss, after, *, name):
    n = len(ss)
    lands = [lax.empty((N_CHIPS,) + s.shape, s.dtype) for s in ss]

    def body(*refs):
        s_refs, land_refs = refs[:n], refs[n:2 * n]
        send_sems, recv_sems = refs[2 * n + 1], refs[2 * n + 2]
        token = refs[-1]
        for cp in _chip_copies(s_refs, land_refs, send_sems, recv_sems):
            cp.start()
        token[...] = jnp.zeros(token.shape, token.dtype)

    hbm = [pltpu.HBM(a.shape, a.dtype) for a in list(ss) + lands]
    res = pl.pallas_call(
        body, name=name,
        out_shape=(pltpu.SemaphoreType.DMA((3 * n,)), pltpu.SemaphoreType.DMA((3 * n,)), *hbm, jax.ShapeDtypeStruct((8, BLK), F32)),
        in_specs=[_HBM] * (2 * n) + [_ANY],
        out_specs=(_SEM, _SEM, *([_HBM] * (2 * n)), pl.BlockSpec(memory_space=pltpu.VMEM)),
        input_output_aliases={i: 2 + i for i in range(2 * n)},
        compiler_params=pltpu.CompilerParams(has_side_effects=_DATAFLOW),
    )(*[pltpu.with_memory_space_constraint(a, pltpu.HBM) for a in list(ss) + lands], after)
    return res[:-1], res[-1]


def _gather_wait(started, after, *, name):
    send_sems, recv_sems = started[0], started[1]
    n = (len(started) - 2) // 2

    def body(*refs):
        s_refs, land_refs = refs[:n], refs[n:2 * n]
        for cp in _chip_copies(s_refs, land_refs, refs[2 * n], refs[2 * n + 1]):
            cp.wait_send()
            cp.wait_recv()

    res = pl.pallas_call(
        body, name=name, out_shape=tuple(pltpu.HBM(a.shape, a.dtype) for a in started[2:]),
        in_specs=[_HBM] * (2 * n) + [_SEM, _SEM, _ANY], out_specs=tuple([_HBM] * (2 * n)),
        input_output_aliases={i: i for i in range(2 * n)},
        compiler_params=pltpu.CompilerParams(has_side_effects=_DATAFLOW),
    )(*started[2:], send_sems, recv_sems, after)
    return list(res[n:])


def _sibling_exchange(lands, *, name):
    n = len(lands)

    def body(*refs):
        o_refs, (send_sems, recv_sems) = refs[n:2 * n], refs[2 * n:]
        x, y, c = _place()
        cps = []
        for i, o_ref in enumerate(o_refs):
            h = o_ref.shape[1] // 2
            for kk, (cx, cy) in enumerate(_other_chips(x, y)):
                def half(hc):
                    return o_ref.at[2 * cx + cy, pl.ds(hc * h, h), :]
                sent = pltpu.make_async_remote_copy(src_ref=half(c), dst_ref=half(c), send_sem=send_sems.at[3 * i + kk],
                                                    recv_sem=recv_sems.at[3 * i + kk], device_id=(x, y, 1 - c), device_id_type=MESH)
                awaited = pltpu.make_async_remote_copy(src_ref=half(1 - c), dst_ref=half(1 - c), send_sem=send_sems.at[3 * i + kk],
                                                       recv_sem=recv_sems.at[3 * i + kk], device_id=(x, y, 1 - c),
                                                       device_id_type=MESH)
                cps.append((sent, awaited))
        for sent, _ in cps:
            sent.start()
        for sent, awaited in cps:
            awaited.wait_recv()
            sent.wait_send()

    return pl.pallas_call(
        body, name=name, in_specs=[_HBM] * n, out_specs=[_HBM] * n,
        out_shape=[jax.ShapeDtypeStruct(a.shape, a.dtype) for a in lands], scratch_shapes=_dma_sems(3 * n),
        input_output_aliases={i: i for i in range(n)},
    )(*lands)


def _own_block(gathered, s):
    me = 2 * lax.axis_index("x") + lax.axis_index("y")
    return lax.dynamic_update_slice(gathered, s[None], (me,) + (0,) * s.ndim)


def _dma_sems(n):
    return [pltpu.SemaphoreType.DMA((n,)), pltpu.SemaphoreType.DMA((n,))]


def _swap_halves(gps, *, name):
    n_ops = len(gps)

    def body(*refs):
        g_refs, o_refs, (send_sems, recv_sems) = refs[:n_ops], refs[n_ops:2 * n_ops], refs[2 * n_ops:]
        x, y, c = _place()
        cps = []
        for i, (g_ref, o_ref) in enumerate(zip(g_refs, o_refs)):
            h = g_ref.shape[1] // 2
            cps.append(pltpu.make_async_remote_copy(
                src_ref=g_ref.at[:, pl.ds((1 - c) * h, h), :], dst_ref=o_ref, send_sem=send_sems.at[i], recv_sem=recv_sems.at[i],
                device_id=(x, y, 1 - c), device_id_type=MESH))
        for cp in cps:
            cp.start()
        for cp in cps:
            cp.wait()

    return pl.pallas_call(
        body, name=name, in_specs=[_HBM] * n_ops, out_specs=[_HBM] * n_ops,
        out_shape=[jax.ShapeDtypeStruct((g.shape[0], g.shape[1] // 2, g.shape[2]), g.dtype) for g in gps],
        scratch_shapes=_dma_sems(n_ops),
    )(*gps)


def _scatter_chips(hps, *, name):
    n_ops = len(hps)

    def body(*refs):
        h_refs, o_refs, (send_sems, recv_sems) = refs[:n_ops], refs[n_ops:2 * n_ops], refs[2 * n_ops:]
        x, y, c = _place()
        cps = [pltpu.make_async_remote_copy(src_ref=h_ref.at[2 * cx + cy], dst_ref=o_ref.at[kk], send_sem=send_sems.at[3 * i + kk],
                                            recv_sem=recv_sems.at[3 * i + kk], device_id=(cx, cy, c), device_id_type=MESH)
               for i, (h_ref, o_ref) in enumerate(zip(h_refs, o_refs)) for kk, (cx, cy) in enumerate(_other_chips(x, y))]
        for cp in cps:
            cp.start()
        for cp in cps:
            cp.wait()

    return pl.pallas_call(
        body, name=name, in_specs=[_HBM] * n_ops, out_specs=[_HBM] * n_ops,
        out_shape=[jax.ShapeDtypeStruct((3,) + hp.shape[1:], hp.dtype) for hp in hps],
        scratch_shapes=_dma_sems(3 * n_ops),
    )(*hps)


def _join_halves(fs, *, name):
    n_ops = len(fs)

    def body(*refs):
        f_refs, o_refs, (send_sems, recv_sems) = refs[:n_ops], refs[n_ops:2 * n_ops], refs[2 * n_ops:]
        x, y, c = _place()
        cps = [pltpu.make_async_remote_copy(src_ref=f_ref, dst_ref=o_ref, send_sem=send_sems.at[i], recv_sem=recv_sems.at[i],
                                            device_id=(x, y, 1 - c), device_id_type=MESH)
               for i, (f_ref, o_ref) in enumerate(zip(f_refs, o_refs))]
        for cp in cps:
            cp.start()
        for cp in cps:
            cp.wait()

    return pl.pallas_call(
        body, name=name, in_specs=[_HBM] * n_ops, out_specs=[_HBM] * n_ops,
        out_shape=[jax.ShapeDtypeStruct(f.shape, f.dtype) for f in fs], scratch_shapes=_dma_sems(n_ops),
    )(*fs)


def _stitch(mine, theirs):
    south = lax.axis_index("c") == 0
    return jnp.concatenate([jnp.where(south, mine, theirs), jnp.where(south, theirs, mine)], axis=0)


def _add_halves(gp, ra, wire_dtype, *, name, bm=256):
    n, r, c_ = gp.shape
    h = r // 2
    bm = _tile(h, bm)
    per = h // bm
    c = lax.axis_index("c").astype(jnp.int32).reshape(1)

    def body(c_ref, g_ref, ra_ref, o_ref, ow_ref):
        s = g_ref[...] + ra_ref[...]
        o_ref[...] = s
        ow_ref[...] = s.astype(wire_dtype)

    mine = pl.BlockSpec((None, bm, c_), lambda j, i, cr: (j, i, 0))
    return pl.pallas_call(
        body, name=name,
        grid_spec=pltpu.PrefetchScalarGridSpec(
            num_scalar_prefetch=1, grid=(n, per),
            in_specs=[pl.BlockSpec((None, bm, c_), lambda j, i, cr: (j, cr[0] * per + i, 0)), mine],
            out_specs=[mine, mine]),
        out_shape=[jax.ShapeDtypeStruct((n, h, c_), F32), jax.ShapeDtypeStruct((n, h, c_), wire_dtype)],
        compiler_params=pltpu.CompilerParams(dimension_semantics=("parallel", "parallel")),
    )(c, gp, ra)


def _add_chips(hp, rb, *, name, bm=256):
    n, h, c_ = hp.shape
    bm = _tile(h, bm)
    me = (2 * lax.axis_index("x") + lax.axis_index("y")).astype(jnp.int32).reshape(1)

    def body(me_ref, h_ref, rb_ref, o_ref):
        o_ref[...] = ((h_ref[...] + rb_ref[0].astype(F32)) + rb_ref[1].astype(F32)) + rb_ref[2].astype(F32)

    return pl.pallas_call(
        body, name=name,
        grid_spec=pltpu.PrefetchScalarGridSpec(
            num_scalar_prefetch=1, grid=(h // bm,),
            in_specs=[pl.BlockSpec((None, bm, c_), lambda i, mr: (mr[0], i, 0)),
                      pl.BlockSpec((3, bm, c_), lambda i, mr: (0, i, 0))],
            out_specs=pl.BlockSpec((bm, c_), lambda i, mr: (i, 0))),
        out_shape=jax.ShapeDtypeStruct((h, c_), F32),
        compiler_params=pltpu.CompilerParams(dimension_semantics=("parallel",)),
    )(me, hp, rb)


def _scatter_copies(h_refs, land_refs, send_sems, recv_sems):
    x, y, c = _place()
    return [pltpu.make_async_remote_copy(src_ref=h_ref.at[2 * cx + cy], dst_ref=land_ref.at[kk], send_sem=send_sems.at[3 * i + kk],
                                         recv_sem=recv_sems.at[3 * i + kk], device_id=(cx, cy, c), device_id_type=MESH)
            for i, (h_ref, land_ref) in enumerate(zip(h_refs, land_refs)) for kk, (cx, cy) in enumerate(_other_chips(x, y))]


def _scatter_start(hps, after, *, name):
    n = len(hps)
    lands = [lax.empty((3,) + hp.shape[1:], hp.dtype) for hp in hps]

    def body(*refs):
        for cp in _scatter_copies(refs[:n], refs[n:2 * n], refs[2 * n + 1], refs[2 * n + 2]):
            cp.start()
        refs[-1][...] = jnp.zeros(refs[-1].shape, refs[-1].dtype)

    hbm = [pltpu.HBM(a.shape, a.dtype) for a in list(hps) + lands]
    res = pl.pallas_call(
        body, name=name,
        out_shape=(pltpu.SemaphoreType.DMA((3 * n,)), pltpu.SemaphoreType.DMA((3 * n,)), *hbm, jax.ShapeDtypeStruct((8, BLK), F32)),
        in_specs=[_HBM] * (2 * n) + [_ANY],
        out_specs=(_SEM, _SEM, *([_HBM] * (2 * n)), pl.BlockSpec(memory_space=pltpu.VMEM)),
        input_output_aliases={i: 2 + i for i in range(2 * n)},
        compiler_params=pltpu.CompilerParams(has_side_effects=_DATAFLOW),
    )(*[pltpu.with_memory_space_constraint(a, pltpu.HBM) for a in list(hps) + lands], after)
    return res[:-1], res[-1]


def _scatter_wait(started, after, *, name):
    n = (len(started) - 2) // 2

    def body(*refs):
        for cp in _scatter_copies(refs[:n], refs[n:2 * n], refs[2 * n], refs[2 * n + 1]):
            cp.wait_send()
            cp.wait_recv()

    res = pl.pallas_call(
        body, name=name, out_shape=tuple(pltpu.HBM(a.shape, a.dtype) for a in started[2:]),
        in_specs=[_HBM] * (2 * n) + [_SEM, _SEM, _ANY], out_specs=tuple([_HBM] * (2 * n)),
        input_output_aliases={i: i for i in range(2 * n)},
        compiler_params=pltpu.CompilerParams(has_side_effects=_DATAFLOW),
    )(*started[2:], started[0], started[1], after)
    return list(res[n:])


def _reduce_to_chips(gps, wire_dtypes, *, tag):
    ras = _swap_halves(gps, name=f"{tag}_swap_halves")
    return [_add_halves(gp, ra, wd, name=f"{tag}_add_halves{i}") for i, (gp, ra, wd) in enumerate(zip(gps, ras, wire_dtypes))]


def _start_reduce(early, tag):
    names = list(early)
    hps = _reduce_to_chips([early[n] for n in names], [BF16] * len(names), tag=tag)
    started, token = _scatter_start([hw for _, hw in hps], hps[-1][1], name=f"{tag}_scatter_start")
    return (tag, names, [hf for hf, _ in hps], started), token


def _adamw_math(w_ref, g_ref, m_ref, v_ref, d_ref, nm_ref, nv_ref):
    c1 = 1.0 - ADAM_B1 ** ADAM_STEP
    c2 = 1.0 - ADAM_B2 ** ADAM_STEP
    g_ = g_ref[...]
    m_ = ADAM_B1 * m_ref[...] + (1.0 - ADAM_B1) * g_
    v_ = ADAM_B2 * v_ref[...] + (1.0 - ADAM_B2) * (g_ * g_)
    d_ref[...] = -ADAM_LR * ((m_ / c1) / (jnp.sqrt(v_ / c2) + ADAM_EPS) + ADAM_WD * w_ref[...])
    nm_ref[...] = m_
    nv_ref[...] = v_


def _adamw_many(groups, *, name):
    n = len(groups[0])
    flat = [a for grp in groups for a in grp]

    def body(*refs):
        ins, outs = refs[:4 * n], refs[4 * n:]
        for i in range(n):
            _adamw_math(ins[i], ins[n + i], ins[2 * n + i], ins[3 * n + i], outs[i], outs[n + i], outs[2 * n + i])

    vmem = pl.BlockSpec(memory_space=pltpu.VMEM)
    res = pl.pallas_call(
        body, name=name, in_specs=[vmem] * (4 * n), out_specs=[vmem] * (3 * n),
        out_shape=[jax.ShapeDtypeStruct(a.shape, F32) for _ in range(3) for a in groups[0]],
    )(*flat)
    return res[:n], res[n:2 * n], res[2 * n:]


def _adamw(w, g_mine, g_theirs, m, v, *, name):
    r, c_ = w.shape
    h = r // 2
    bm = _tile(h, 256)
    per = h // bm
    c = lax.axis_index("c").astype(jnp.int32).reshape(1)

    def body(c_ref, w_ref, f_ref, t_ref, m_ref, v_ref, g_ref, d_ref, nm_ref, nv_ref):
        first_half = pl.program_id(0) < per
        mine = jnp.where(jnp.where(first_half, c_ref[0] == 0, c_ref[0] == 1), 1.0, 0.0)
        g_ref[...] = mine * f_ref[...] + (1.0 - mine) * t_ref[...]
        _adamw_math(w_ref, g_ref, m_ref, v_ref, d_ref, nm_ref, nv_ref)

    full = pl.BlockSpec((bm, c_), lambda i, cr: (i, 0))
    half = pl.BlockSpec((bm, c_), lambda i, cr: (i % per, 0))
    return pl.pallas_call(
        body, name=name,
        grid_spec=pltpu.PrefetchScalarGridSpec(num_scalar_prefetch=1, grid=(r // bm,), in_specs=[full, half, half, full, full],
                                               out_specs=[full] * 4),
        out_shape=[jax.ShapeDtypeStruct((r, c_), F32)] * 4,
        compiler_params=pltpu.CompilerParams(dimension_semantics=("parallel",)),
    )(c, w, g_mine, g_theirs, m, v)


_WEIGHTS = ["a_norm", "a_w_in", "a_w_out", "b_norm", "b_w_in", "b_v_ln_g", "b_v_ln_b", "b_w_s", "b_b_s", "b_w_out",
            "c_norm", "c_w_in", "c_conv_w", "c_conv_b", "c_ln_g", "c_ln_b", "c_w_out", "d_norm", "d_w_in", "d_b_f",
            "d_w_out", "final_norm"]
_SHARD_AXIS = {"a_norm": None, "a_w_in": 2, "a_w_out": 1, "b_norm": 1, "b_w_in": 2, "b_v_ln_g": 1, "b_v_ln_b": 1, "b_w_s": None,
               "b_b_s": None, "b_w_out": 1, "c_norm": 1, "c_w_in": 2, "c_conv_w": 2, "c_conv_b": 1, "c_ln_g": 1, "c_ln_b": 1,
               "c_w_out": 1, "d_norm": 1, "d_w_in": 2, "d_b_f": None, "d_w_out": 1, "final_norm": None}
_BIG = ["a_w_in", "a_w_out", "b_w_in", "b_w_out", "c_w_in", "c_w_out", "d_w_in", "d_w_out"]
_GATHER_GROUPS = (("a_w_in", "a_w_out"), ("b_w_in", "b_w_out"), ("c_w_in", "c_w_out", "d_w_in", "d_w_out"))
_SMALL_SHARDED = [n for n in _WEIGHTS if _SHARD_AXIS[n] is not None and n not in _BIG]
_REPLICATED = [n for n in _WEIGHTS if _SHARD_AXIS[n] is None]
_ROW_ALIGN = 32


def _pack(pieces, dtype, align=_ROW_ALIGN):
    flat = jnp.concatenate([p.reshape(-1).astype(dtype) for p in pieces])
    unit = align * PACK_C
    total = -(-flat.shape[0] // unit) * unit
    return jnp.pad(flat, (0, total - flat.shape[0])).reshape(total // PACK_C, PACK_C)


def _unpack(flat, shapes):
    out, off = [], 0
    for s in shapes:
        n = math.prod(s)
        out.append(flat[off:off + n].reshape(s))
        off += n
    return out


def _full_shape(local_shape, axis):
    s = list(local_shape)
    if axis is not None:
        s[axis] *= N_CHIPS
    return tuple(s)


def _gather_weights(local):
    def whole(n, gt):
        if _SHARD_AXIS[n] == 1:
            return gt.reshape(-1, gt.shape[-1])
        if n == "d_w_in":
            return gt.transpose(1, 0, 2).reshape(gt.shape[1], -1)
        return gt

    full = {n: local[n][0] if n != "final_norm" else local[n] for n in _REPLICATED}
    first = list(_GATHER_GROUPS[0])
    mine = [local[n][0].astype(BF16) for n in first] + [_pack([local[n] for n in _SMALL_SHARDED], F32)]
    got = [_own_block(gt, s) for gt, s in zip(_allgather_chips(mine, name="gather_weights"), mine)]
    full.update({n: whole(n, gt) for n, gt in zip(first, got)})
    small = got[-1].reshape(N_CHIPS, -1)
    shards = [_unpack(small[j], [local[n].shape[1:] for n in _SMALL_SHARDED]) for j in range(N_CHIPS)]
    for i, n in enumerate(_SMALL_SHARDED):
        full[n] = jnp.concatenate([shards[j][i] for j in range(N_CHIPS)], axis=_SHARD_AXIS[n] - 1)

    def begin(k, after):
        shards_k = [local[n][0].astype(BF16) for n in _GATHER_GROUPS[k]]
        started, token = _gather_start(shards_k, after, name=f"gather{k}_start")
        return shards_k, started, token

    pending = [begin(1, got[0])]
    full["a_norm"] = full["a_norm"] + pending[0][2][0, 0]

    def finish(k):
        def weights(after):
            shards_k, started, _ = pending[k - 1]
            lands = _gather_wait(started, after, name=f"gather{k}_wait")
            token = None
            if k + 1 < len(_GATHER_GROUPS):
                pending.append(begin(k + 1, lands[0]))
                token = pending[k][2]
            lands = _sibling_exchange(lands, name=f"gather{k}_exchange")
            out = {n: whole(n, _own_block(gt, s)) for n, gt, s in zip(_GATHER_GROUPS[k], lands, shards_k)}
            if token is not None:
                gain = _GATHER_GROUPS[k][0][0] + "_norm"
                out[gain] = full[gain] + token[0, 0]
            return out
        return weights

    return full, [finish(k) for k in range(1, len(_GATHER_GROUPS))]


def _repl_piece_len(local):
    total = sum(math.prod(local[n].shape) for n in _REPLICATED)
    return -(-total // N_CHIPS)


def _reduce_grads(g, local, early):
    rep_flat = jnp.concatenate([g[n].reshape(-1) for n in _REPLICATED])
    piece = _repl_piece_len(local)
    rep_flat = jnp.pad(rep_flat, (0, N_CHIPS * piece - rep_flat.shape[0]))

    def shard(n, j):
        full = g[n].reshape(_full_shape(local[n].shape, _SHARD_AXIS[n]))
        width = local[n].shape[_SHARD_AXIS[n]]
        return lax.slice_in_dim(full, j * width, (j + 1) * width, axis=_SHARD_AXIS[n])

    small = jnp.stack([_pack([shard(n, j) for n in _SMALL_SHARDED] + [rep_flat[j * piece:(j + 1) * piece]], F32)
                       for j in range(N_CHIPS)])
    early_names = [n for _, names, _, _ in early for n in names]
    late = [n for n in _BIG if n not in early_names]
    hps = _reduce_to_chips([g[n] for n in late] + [small], [BF16] * len(late) + [F32], tag="grads")
    rbs = list(_scatter_chips([hw for _, hw in hps], name="grads_scatter_chips"))
    early_halves, early_rbs = [], []
    for tag, _, halves_k, started in early:
        early_halves += halves_k
        early_rbs += _scatter_wait(started, rbs[0], name=f"{tag}_scatter_wait")
    halves = early_halves + [hf for hf, _ in hps]
    fs = [_add_chips(hf, rb, name=f"grads_add_chips{i}") for i, (hf, rb) in enumerate(zip(halves, early_rbs + rbs))]
    theirs = _join_halves(fs, name="grads_join_halves")
    red = dict(zip(early_names + late, zip(fs, theirs)))
    out = _unpack(_stitch(fs[-1], theirs[-1]).reshape(-1), [local[n].shape for n in _SMALL_SHARDED] + [(piece,)])
    red.update(zip(_SMALL_SHARDED, out[:-1]))
    rep_mine = _pack([out[-1]], F32)
    rep = _own_block(_allgather_chips([rep_mine], name="gather_replicated_grads")[0], rep_mine)
    rep = rep.reshape(N_CHIPS, -1)[:, :piece].reshape(-1)
    for n, val in zip(_REPLICATED, _unpack(rep, [local[n].shape for n in _REPLICATED])):
        red[n] = val
    return red


def _update(local, grads, m, v):
    grads, delta, new_m, new_v = dict(grads), {}, {}, {}
    for n in _BIG:
        shp = local[n].shape
        two = (shp[-2], shp[-1])
        res = _adamw(local[n].reshape(two), *grads[n], m[n].reshape(two), v[n].reshape(two), name=f"adamw_{n}")
        grads[n], delta[n], new_m[n], new_v[n] = [r.reshape(shp) for r in res]
    small = [n for n in _WEIGHTS if n not in _BIG]
    two = {n: (math.prod(local[n].shape[:-1]), local[n].shape[-1]) for n in small}
    res = _adamw_many([[src[n].reshape(two[n]) for n in small] for src in (local, grads, m, v)], name="adamw_small")
    for dst, rs in zip((delta, new_m, new_v), res):
        for n, val in zip(small, rs):
            dst[n] = val.reshape(local[n].shape)
    return grads, delta, new_m, new_v


def kernel(x, a_norm, a_w_in, a_w_out, b_norm, b_w_in, b_v_ln_g, b_v_ln_b, b_w_s, b_b_s, b_w_out, c_norm, c_w_in, c_conv_w, c_conv_b, c_ln_g, c_ln_b, c_w_out, d_norm, d_w_in, d_b_f, d_w_out, final_norm, loss_target, m_a_norm, m_a_w_in, m_a_w_out, m_b_norm, m_b_w_in, m_b_v_ln_g, m_b_v_ln_b, m_b_w_s, m_b_b_s, m_b_w_out, m_c_norm, m_c_w_in, m_c_conv_w, m_c_conv_b, m_c_ln_g, m_c_ln_b, m_c_w_out, m_d_norm, m_d_w_in, m_d_b_f, m_d_w_out, m_final_norm, v_a_norm, v_a_w_in, v_a_w_out, v_b_norm, v_b_w_in, v_b_v_ln_g, v_b_v_ln_b, v_b_w_s, v_b_b_s, v_b_w_out, v_c_norm, v_c_w_in, v_c_conv_w, v_c_conv_b, v_c_ln_g, v_c_ln_b, v_c_w_out, v_d_norm, v_d_w_in, v_d_b_f, v_d_w_out, v_final_norm):
    local = dict(zip(_WEIGHTS, (a_norm, a_w_in, a_w_out, b_norm, b_w_in, b_v_ln_g, b_v_ln_b, b_w_s, b_b_s, b_w_out, c_norm, c_w_in,
                                c_conv_w, c_conv_b, c_ln_g, c_ln_b, c_w_out, d_norm, d_w_in, d_b_f, d_w_out, final_norm)))
    m = dict(zip(_WEIGHTS, (m_a_norm, m_a_w_in, m_a_w_out, m_b_norm, m_b_w_in, m_b_v_ln_g, m_b_v_ln_b, m_b_w_s, m_b_b_s, m_b_w_out,
                            m_c_norm, m_c_w_in, m_c_conv_w, m_c_conv_b, m_c_ln_g, m_c_ln_b, m_c_w_out, m_d_norm, m_d_w_in, m_d_b_f,
                            m_d_w_out, m_final_norm)))
    v = dict(zip(_WEIGHTS, (v_a_norm, v_a_w_in, v_a_w_out, v_b_norm, v_b_w_in, v_b_v_ln_g, v_b_v_ln_b, v_b_w_s, v_b_b_s, v_b_w_out,
                            v_c_norm, v_c_w_in, v_c_conv_w, v_c_conv_b, v_c_ln_g, v_c_ln_b, v_c_w_out, v_d_norm, v_d_w_in, v_d_b_f,
                            v_d_w_out, v_final_norm)))
    loss_part, grad_x, g, early = _local_step(x, loss_target, *_gather_weights(local), _start_reduce)
    loss = lax.psum(loss_part, ("x", "y", "c"))
    grads = _reduce_grads(g, local, early)
    grads, delta, new_m, new_v = _update(local, grads, m, v)
    return (loss, grad_x, *[grads[n] for n in _WEIGHTS], *[delta[n] for n in _WEIGHTS],
            *[new_m[n] for n in _WEIGHTS], *[new_v[n] for n in _WEIGHTS])
```

```python
import math

import jax
import jax.numpy as jnp
from jax import lax
from jax.experimental import pallas as pl
from jax.experimental.pallas import tpu as pltpu

F32, BF16 = jnp.float32, jnp.bfloat16
MESH = pl.DeviceIdType.MESH

D_MODEL = 1024
HEADS = 16
HEAD_DIM = 64
BLK = 128
PAIRS = HEADS // 2
GM_W = 2048
GM_G = 16
CV_W = 2048
CV_K = 31
HALO = 32
EPS = 1e-6
N_CHIPS = 4
PACK_C = 1024
ADAM_LR, ADAM_B1, ADAM_B2, ADAM_EPS, ADAM_WD, ADAM_STEP = 0.001, 0.9, 0.999, 1e-08, 0.01, 10

_NT = (((1,), (1,)), ((), ()))
_TN = (((0,), (0,)), ((), ()))
_NN = (((1,), (0,)), ((), ()))


def _dot(a, b, dims=_NN):
    return lax.dot_general(a, b, dims, preferred_element_type=F32)


def _split3(x):
    hi = x.astype(BF16)
    r = x - hi.astype(F32)
    mid = r.astype(BF16)
    lo = (r - mid.astype(F32)).astype(BF16)
    return hi, mid, lo


def _dot3_left(m, x):
    hi, mid, lo = _split3(x)
    return _dot(m, hi) + _dot(m, mid) + _dot(m, lo)


def _sigmoid(x):
    return 1.0 / (1.0 + jnp.exp(-x))


def _silu(x):
    return x * _sigmoid(x)


def _dsilu(x):
    s = _sigmoid(x)
    return s * (1.0 + x * (1.0 - s))


_GELU_C = math.sqrt(2.0 / math.pi)
_GELU_A = 0.044715


def _gelu(x):
    return 0.5 * x * (1.0 + jnp.tanh(_GELU_C * (x + _GELU_A * x * x * x)))


def _dgelu(x):
    t = jnp.tanh(_GELU_C * (x + _GELU_A * x * x * x))
    return 0.5 * (1.0 + t) + 0.5 * x * (1.0 - t * t) * _GELU_C * (1.0 + 3.0 * _GELU_A * x * x)


def _log_sigmoid(x):
    return jnp.minimum(x, 0.0) - jnp.log(1.0 + jnp.exp(-jnp.abs(x)))


def _rms_fwd(x, g):
    r = lax.rsqrt(jnp.mean(x * x, axis=-1, keepdims=True) + EPS)
    return x * r * g


def _rms_bwd(dy, x, g):
    r = lax.rsqrt(jnp.mean(x * x, axis=-1, keepdims=True) + EPS)
    xh = x * r
    dxh = dy * g
    dx = r * (dxh - xh * jnp.mean(dxh * xh, axis=-1, keepdims=True))
    return dx, dy * xh


def _ln_stats(x):
    mu = jnp.mean(x, axis=-1, keepdims=True)
    xc = x - mu
    r = lax.rsqrt(jnp.mean(xc * xc, axis=-1, keepdims=True) + EPS)
    return xc * r, r


def _ln_bwd(dy, xh, r, g):
    dxh = dy * g
    return r * (dxh - jnp.mean(dxh, axis=-1, keepdims=True) - xh * jnp.mean(dxh * xh, axis=-1, keepdims=True))


def _colsum(x):
    return jnp.sum(x, axis=0, keepdims=True)


def _tile(n, want):
    for t in range(min(n, want), 7, -1):
        if n % t == 0 and t % 8 == 0:
            return t
    return n


MM_TILE = 1024


def _matmul(a, b, *, name, mode="nn", residual=None, out_shards=1, out_dtype=F32):
    (m, k) = a.shape
    b_shards = b.shape[0] if b.ndim == 3 else 1
    if mode == "nn":
        n = b.shape[-1] * b_shards
        tn, tk = _tile(n // max(b_shards, out_shards), MM_TILE), _tile(k, MM_TILE)
    else:
        n = b.shape[-2]
        tn, tk = _tile(n // out_shards, MM_TILE), _tile(k // b_shards, MM_TILE)
    tm = _tile(m, MM_TILE)
    nk = k // tk
    a_spec = pl.BlockSpec((tm, tk), lambda i, j, kk: (i, kk))
    if mode == "nn":
        dims = _NN
        if b_shards == 1:
            b_spec = pl.BlockSpec((tk, tn), lambda i, j, kk: (kk, j))
        else:
            per_b = n // b_shards // tn
            b_spec = pl.BlockSpec((None, tk, tn), lambda i, j, kk: (j // per_b, kk, j % per_b))
    else:
        dims = _NT
        if b_shards == 1:
            b_spec = pl.BlockSpec((tn, tk), lambda i, j, kk: (j, kk))
        else:
            per_b = k // b_shards // tk
            b_spec = pl.BlockSpec((None, tn, tk), lambda i, j, kk: (kk // per_b, j, kk % per_b))
    if out_shards == 1:
        o_spec = pl.BlockSpec((tm, tn), lambda i, j, kk: (i, j))
        o_shape = (m, n)
    else:
        per_o = n // out_shards // tn
        o_spec = pl.BlockSpec((None, tm, tn), lambda i, j, kk: (j // per_o, i, j % per_o))
        o_shape = (out_shards, m, n // out_shards)
    has_res = residual is not None

    def body(a_ref, b_ref, *rest):
        o_ref = rest[-1]
        kk = pl.program_id(2)
        part = _dot(a_ref[...].astype(BF16), b_ref[...].astype(BF16), dims)
        if has_res:
            @pl.when(kk == 0)
            def _():
                o_ref[...] = part + rest[0][...]
        else:
            @pl.when(kk == 0)
            def _():
                o_ref[...] = part.astype(out_dtype)

        if nk > 1:
            @pl.when(kk > 0)
            def _():
                o_ref[...] += part

    assert out_dtype == F32 or nk == 1
    return pl.pallas_call(
        body, name=name, grid=(m // tm, n // tn, nk),
        in_specs=[a_spec, b_spec] + ([o_spec] if has_res else []),
        out_specs=o_spec, out_shape=jax.ShapeDtypeStruct(o_shape, out_dtype),
        compiler_params=pltpu.CompilerParams(dimension_semantics=("parallel", "parallel", "arbitrary")),
    )(a, b, *([residual] if has_res else []))


def _matmul_parts(at, parts, *, name):
    m, k = at.shape
    n = parts[0].shape[1]
    tm, tk = _tile(m, MM_TILE // 2), _tile(k, MM_TILE // 2)
    n_parts = len(parts)

    def body(a_ref, *rest):
        o_ref = rest[-1]
        kk = pl.program_id(1)
        a = a_ref[...].astype(BF16)
        for p in range(n_parts):
            part = _dot(a, rest[p][...].astype(BF16))

            @pl.when(kk == 0)
            def _():
                o_ref[p] = part

            @pl.when(kk > 0)
            def _():
                o_ref[p] += part

    return pl.pallas_call(
        body, name=name, grid=(m // tm, k // tk),
        in_specs=[pl.BlockSpec((tm, tk), lambda i, kk: (i, kk))] + [pl.BlockSpec((tk, n), lambda i, kk: (kk, 0))] * n_parts,
        out_specs=pl.BlockSpec((n_parts, tm, n), lambda i, kk: (0, i, 0)),
        out_shape=jax.ShapeDtypeStruct((n_parts, m, n), F32),
        compiler_params=pltpu.CompilerParams(dimension_semantics=("parallel", "arbitrary")),
    )(at, *parts)


def _matmul_sum(pairs, *, name):
    m, n = pairs[0][0].shape[0], pairs[0][1].shape[0]
    tm, tn = _tile(m, MM_TILE // 2), _tile(n, MM_TILE)
    n_pairs = len(pairs)

    def body(*refs):
        acc = None
        for p in range(n_pairs):
            part = _dot(refs[2 * p][...].astype(BF16), refs[2 * p + 1][...].astype(BF16), _NT)
            acc = part if acc is None else acc + part
        refs[-1][...] = acc

    in_specs = []
    for a, b in pairs:
        in_specs += [pl.BlockSpec((tm, a.shape[1]), lambda i, j: (i, 0)), pl.BlockSpec((tn, b.shape[1]), lambda i, j: (j, 0))]
    return pl.pallas_call(
        body, name=name, grid=(m // tm, n // tn), in_specs=in_specs, out_specs=pl.BlockSpec((tm, tn), lambda i, j: (i, j)),
        out_shape=jax.ShapeDtypeStruct((m, n), F32),
        compiler_params=pltpu.CompilerParams(dimension_semantics=("parallel", "parallel")),
    )(*[x for pair in pairs for x in pair])


def _rows(fn, *, name, steps, ins, outs, accs=(), scratch=()):
    ni, no, na = len(ins), len(outs), len(accs)

    def body(*refs):
        in_refs, out_refs = refs[:ni], refs[ni:ni + no]
        acc_refs, scr = refs[ni + no:ni + no + na], refs[ni + no + na:]
        i = pl.program_id(0)

        @pl.when(i == 0)
        def _():
            for r in acc_refs:
                r[...] = jnp.zeros(r.shape, r.dtype)

        fn(i, in_refs, out_refs, acc_refs, scr)

    def full(shape):
        nd = len(shape)
        return pl.BlockSpec(tuple(shape), lambda i: (0,) * nd)

    res = pl.pallas_call(
        body, name=name, grid=(steps,),
        in_specs=[pl.BlockSpec(bs, im) for _, bs, im in ins],
        out_specs=[pl.BlockSpec(bs, im) for _, _, bs, im in outs] + [full(s) for s, _ in accs],
        out_shape=[jax.ShapeDtypeStruct(s, d) for s, d, _, _ in outs] + [jax.ShapeDtypeStruct(s, d) for s, d in accs],
        scratch_shapes=list(scratch),
        compiler_params=pltpu.CompilerParams(dimension_semantics=("arbitrary",)),
    )(*[a for a, _, _ in ins])
    return res


def _rb(arr, bm, cb=0, width=None):
    w = arr.shape[1] if width is None else width
    return (arr, (bm, w), lambda i: (i, cb))


def _const(arr):
    nd = arr.ndim
    return (arr, tuple(arr.shape), lambda i: (0,) * nd)


def _ro(t, w, dtype, bm):
    return ((t, w), dtype, (bm, w), lambda i: (i, 0))


def _rot(t, w, dtype, bm):
    return ((w, t), dtype, (w, bm), lambda i: (0, i))


def _rmsnorm(x, g, *, name, bm=512):
    t, d = x.shape
    bm = _tile(t, bm)

    def fn(i, ins, outs, accs, scr):
        h = _rms_fwd(ins[0][...], ins[1][...])
        outs[0][...] = h.astype(BF16)
        outs[1][...] = h.T.astype(BF16)

    return _rows(fn, name=name, steps=t // bm, ins=[_rb(x, bm), _const(g)], outs=[_ro(t, d, BF16, bm), _rot(t, d, BF16, bm)])


def _rmsnorm_bwd(dh, x, g, dres, *, name, bm=512):
    t, d = x.shape
    bm = _tile(t, bm)

    def fn(i, ins, outs, accs, scr):
        dx, dgrow = _rms_bwd(ins[0][...], ins[1][...], ins[2][...])
        outs[0][...] = ins[3][...] + dx
        accs[0][...] += _colsum(dgrow)

    return _rows(fn, name=name, steps=t // bm, ins=[_rb(dh, bm), _rb(x, bm), _const(g), _rb(dres, bm)],
                 outs=[_ro(t, d, F32, bm)], accs=[((1, d), F32)])


def _gate(o, p, gcb, *, name, bm=512):
    t, w = o.shape
    bm = _tile(t, bm)

    def fn(i, ins, outs, accs, scr):
        y = ins[0][...] * _silu(ins[1][...])
        outs[0][...] = y.astype(BF16)
        outs[1][...] = y.T.astype(BF16)

    return _rows(fn, name=name, steps=t // bm, ins=[_rb(o, bm), _rb(p, bm, gcb, w)],
                 outs=[_ro(t, w, BF16, bm), _rot(t, w, BF16, bm)])


def _gate_bwd(dy, o, p, gcb, *, name, bm=512):
    t, w = o.shape
    bm = _tile(t, bm)

    def fn(i, ins, outs, accs, scr):
        dy_, o_, g_ = ins[0][...], ins[1][...], ins[2][...]
        outs[0][...] = dy_ * _silu(g_)
        outs[1][...] = dy_ * o_ * _dsilu(g_)

    return _rows(fn, name=name, steps=t // bm, ins=[_rb(dy, bm), _rb(o, bm), _rb(p, bm, gcb, w)],
                 outs=[_ro(t, w, F32, bm), _ro(t, w, F32, bm)])


def _loss_head(x, g, tgt, *, name, bm=512):
    t, d = x.shape
    bm = _tile(t, bm)

    def fn(i, ins, outs, accs, scr):
        x_, g_, tg = ins[0][...], ins[1][...], ins[2][...]
        err = _rms_fwd(x_, g_) - tg
        part = 0.5 * jnp.sum(jnp.sum(err * err, axis=-1, keepdims=True), axis=0, keepdims=True) / d
        dx, dgrow = _rms_bwd(err / d, x_, g_)
        outs[0][...] = dx
        accs[0][...] += _colsum(dgrow)
        accs[1][...] += jnp.broadcast_to(part, (1, BLK))

    return _rows(fn, name=name, steps=t // bm, ins=[_rb(x, bm), _const(g), _rb(tgt, bm)],
                 outs=[_ro(t, d, F32, bm)], accs=[((1, d), F32), ((1, BLK), F32)])


def _gmlp_mix_weights(ws_ref, g):
    row = lax.broadcasted_iota(jnp.int32, (BLK, BLK), 0)
    col = lax.broadcasted_iota(jnp.int32, (BLK, BLK), 1)
    tril = col <= row
    return jnp.where(tril, ws_ref[g], 0.0), tril


def _gmlp_fwd(p, ln_g, ln_b, w_s, bs_t, *, name):
    t = p.shape[0]

    def fn(i, ins, outs, accs, scr):
        p_ref, lg, lb, ws_ref, bst = ins
        vn = _ln_stats(_gelu(p_ref[:, GM_W:2 * GM_W]))[0] * lg[...] + lb[...]
        for g in range(GM_G):
            cs = slice(g * BLK, (g + 1) * BLK)
            wt, _ = _gmlp_mix_weights(ws_ref, g)
            s = _dot(wt.astype(BF16), vn[:, cs].astype(BF16)) + bst[:, g:g + 1]
            u = _gelu(p_ref[:, cs])
            gate = p_ref[:, 2 * GM_W + g * BLK:2 * GM_W + (g + 1) * BLK]
            y = u * s * _silu(gate)
            outs[0][:, cs] = y.astype(BF16)
            outs[1][cs, :] = y.T.astype(BF16)

    return _rows(fn, name=name, steps=t // BLK, ins=[_rb(p, BLK), _const(ln_g), _const(ln_b), _const(w_s), _const(bs_t)],
                 outs=[_ro(t, GM_W, BF16, BLK), _rot(t, GM_W, BF16, BLK)])


def _gmlp_bwd(dy, p, ln_g, ln_b, w_s, bs_t, *, name):
    t = p.shape[0]

    def fn(i, ins, outs, accs, scr):
        dy_ref, p_ref, lg, lb, ws_ref, bst = ins
        dp_ref = outs[0]
        dlg, dlb, dws, dbst = accs
        dvn_ref = scr[0]
        v_pre = p_ref[:, GM_W:2 * GM_W]
        xh, r = _ln_stats(_gelu(v_pre))
        vn = xh * lg[...] + lb[...]
        for g in range(GM_G):
            cs = slice(g * BLK, (g + 1) * BLK)
            gs = slice(2 * GM_W + g * BLK, 2 * GM_W + (g + 1) * BLK)
            wt, tril = _gmlp_mix_weights(ws_ref, g)
            vg = vn[:, cs].astype(BF16)
            s = _dot(wt.astype(BF16), vg) + bst[:, g:g + 1]
            u_pre, gate, dyg = p_ref[:, cs], p_ref[:, gs], dy_ref[:, cs]
            u = _gelu(u_pre)
            dos = dyg * _silu(gate)
            dp_ref[:, gs] = dyg * u * s * _dsilu(gate)
            dp_ref[:, cs] = dos * s * _dgelu(u_pre)
            ds = (dos * u).astype(BF16)
            dws[g] += jnp.where(tril, _dot(ds, vg, _NT), 0.0)
            dbst[:, g:g + 1] += jnp.sum(dos * u, axis=1, keepdims=True)
            dvn_ref[:, cs] = _dot(wt.astype(BF16), ds, _TN)
        dvn = dvn_ref[...]
        dlg[...] += _colsum(dvn * xh)
        dlb[...] += _colsum(dvn)
        dp_ref[:, GM_W:2 * GM_W] = _ln_bwd(dvn, xh, r, lg[...]) * _dgelu(v_pre)

    return _rows(fn, name=name, steps=t // BLK,
                 ins=[_rb(dy, BLK), _rb(p, BLK), _const(ln_g), _const(ln_b), _const(w_s), _const(bs_t)],
                 outs=[_ro(t, 3 * GM_W, F32, BLK)],
                 accs=[((1, GM_W), F32), ((1, GM_W), F32), ((GM_G, BLK, BLK), F32), ((BLK, GM_G), F32)],
                 scratch=[pltpu.VMEM((BLK, GM_W), F32)])


CV_BM = 128
CV_RC = 8
SUBLANES = 8
CV_FWD_OFFS = [HALO - (CV_K - 1) + k for k in range(CV_K)]
CV_BWD_OFFS = [CV_K - 1 - k for k in range(CV_K)]


def _conv_halo_prev(p, cb, bm):
    per = bm // HALO
    return (p, (HALO, CV_W), lambda i: (jnp.maximum(i * per - 1, 0), cb))


def _conv_scratch(bm):
    return [pltpu.VMEM((bm + HALO, CV_W), F32), pltpu.VMEM((SUBLANES - 1, bm + HALO - SUBLANES, CV_W), F32),
            pltpu.VMEM((bm, CV_W), F32)]


def _conv_shift_copies(ext_ref, sh_ref):
    rows = sh_ref.shape[1]
    for b in range(1, SUBLANES):
        sh_ref[b - 1] = ext_ref[pl.ds(b, rows), :]


def _conv_window(ext_ref, sh_ref, off, r0, rows):
    b = off % SUBLANES
    src = ext_ref if b == 0 else sh_ref.at[b - 1]
    return src[pl.ds(r0 + (off - b), rows), :]


def _conv_taps(ext_ref, sh_ref, cw_ref, y_ref, offs):
    bm = y_ref.shape[0]

    def chunk(ci, c):
        r0 = pl.multiple_of(ci * CV_RC, CV_RC)
        acc = jnp.zeros((CV_RC, CV_W), F32)
        for k in range(CV_K):
            acc = acc + cw_ref[pl.ds(k * SUBLANES, CV_RC), :] * _conv_window(ext_ref, sh_ref, offs[k], r0, CV_RC)
        y_ref[pl.ds(r0, CV_RC), :] = acc
        return c

    lax.fori_loop(0, bm // CV_RC, chunk, 0)


def _conv_dweights(dy1_ref, ext_ref, sh_ref, dcw_ref):
    bm = dy1_ref.shape[0]
    groups = 4
    for k in range(CV_K):
        def step(ci, acc, off=CV_FWD_OFFS[k]):
            prods = []
            for u in range(groups):
                r0 = pl.multiple_of((ci * groups + u) * CV_RC, CV_RC)
                prods.append(dy1_ref[pl.ds(r0, CV_RC), :] * _conv_window(ext_ref, sh_ref, off, r0, CV_RC))
            return acc + ((prods[0] + prods[1]) + (prods[2] + prods[3]))

        dcw_ref[k:k + 1, :] += _colsum(lax.fori_loop(0, bm // (CV_RC * groups), step, jnp.zeros((CV_RC, CV_W), F32)))


def _conv_fill(i, ext_ref, a_prev, b_prev, a, b, bm, seq):
    keep = jnp.where((i % (seq // bm)) == 0, 0.0, 1.0)
    ext_ref[pl.ds(0, HALO), :] = keep * (a_prev * _sigmoid(b_prev))
    ext_ref[pl.ds(HALO, bm), :] = a * _sigmoid(b)


def _conv_fwd(p, cw, cb, ln_g, ln_b, seq, *, name, bm=CV_BM):
    t = p.shape[0]

    def fn(i, ins, outs, accs, scr):
        a, b, gate, ap, bp = [r[...] for r in ins[:5]]
        cw_ref, cb_, lg, lb = ins[5], ins[6][...], ins[7][...], ins[8][...]
        ext, sh, y = scr
        _conv_fill(i, ext, ap, bp, a, b, bm, seq)
        _conv_shift_copies(ext, sh)
        _conv_taps(ext, sh, cw_ref, y, CV_FWD_OFFS)
        y2 = _ln_stats(y[...] + cb_)[0] * lg + lb
        out = _silu(y2) * _silu(gate)
        outs[0][...] = out.astype(BF16)
        outs[1][...] = out.T.astype(BF16)

    return _rows(fn, name=name, steps=t // bm,
                 ins=[_rb(p, bm, 0, CV_W), _rb(p, bm, 1, CV_W), _rb(p, bm, 2, CV_W),
                      _conv_halo_prev(p, 0, bm), _conv_halo_prev(p, 1, bm),
                      _const(cw), _const(cb), _const(ln_g), _const(ln_b)],
                 outs=[_ro(t, CV_W, BF16, bm), _rot(t, CV_W, BF16, bm)], scratch=_conv_scratch(bm))


def _conv_bwd_post(dy, p, cw, cb, ln_g, ln_b, seq, *, name, bm=CV_BM):
    t = p.shape[0]

    def fn(i, ins, outs, accs, scr):
        dy_, a, b, gate, ap, bp = [r[...] for r in ins[:6]]
        cw_ref, cb_, lg, lb = ins[6], ins[7][...], ins[8][...], ins[9][...]
        dlg, dlb, dcb, dcw = accs
        ext, sh, y = scr
        _conv_fill(i, ext, ap, bp, a, b, bm, seq)
        _conv_shift_copies(ext, sh)
        _conv_taps(ext, sh, cw_ref, y, CV_FWD_OFFS)
        xh, r = _ln_stats(y[...] + cb_)
        y2 = xh * lg + lb
        outs[1][...] = dy_ * _silu(y2) * _dsilu(gate)
        dy2 = dy_ * _silu(gate) * _dsilu(y2)
        dlg[...] += _colsum(dy2 * xh)
        dlb[...] += _colsum(dy2)
        dy1 = _ln_bwd(dy2, xh, r, lg)
        outs[0][...] = dy1
        dcb[...] += _colsum(dy1)
        _conv_dweights(outs[0], ext, sh, dcw)

    return _rows(fn, name=name, steps=t // bm,
                 ins=[_rb(dy, bm), _rb(p, bm, 0, CV_W), _rb(p, bm, 1, CV_W), _rb(p, bm, 2, CV_W),
                      _conv_halo_prev(p, 0, bm), _conv_halo_prev(p, 1, bm),
                      _const(cw), _const(cb), _const(ln_g), _const(ln_b)],
                 outs=[_ro(t, CV_W, F32, bm), _ro(t, CV_W, F32, bm)],
                 accs=[((1, CV_W), F32), ((1, CV_W), F32), ((1, CV_W), F32), ((CV_K, CV_W), F32)],
                 scratch=_conv_scratch(bm))


def _conv_bwd_pre(dy1, dgate, p, cw, seq, *, name, bm=CV_BM):
    t = p.shape[0]
    per = bm // HALO
    last_halo = t // HALO - 1

    def fn(i, ins, outs, accs, scr):
        d1, d1n, dg, a, b = [r[...] for r in ins[:5]]
        ext, sh, y = scr
        keep = jnp.where((i % (seq // bm)) == (seq // bm - 1), 0.0, 1.0)
        ext[pl.ds(0, bm), :] = d1
        ext[pl.ds(bm, HALO), :] = keep * d1n
        _conv_shift_copies(ext, sh)
        _conv_taps(ext, sh, ins[5], y, CV_BWD_OFFS)
        dy0 = y[...]
        sb = _sigmoid(b)
        outs[0][:, 0:CV_W] = dy0 * sb
        outs[0][:, CV_W:2 * CV_W] = dy0 * a * sb * (1.0 - sb)
        outs[0][:, 2 * CV_W:3 * CV_W] = dg

    return _rows(fn, name=name, steps=t // bm,
                 ins=[_rb(dy1, bm), (dy1, (HALO, CV_W), lambda i: (jnp.minimum((i + 1) * per, last_halo), 0)),
                      _rb(dgate, bm), _rb(p, bm, 0, CV_W), _rb(p, bm, 1, CV_W), _const(cw)],
                 outs=[_ro(t, 3 * CV_W, F32, bm)], scratch=_conv_scratch(bm))[0]


def _iotas():
    row = lax.broadcasted_iota(jnp.int32, (BLK, BLK), 0)
    col = lax.broadcasted_iota(jnp.int32, (BLK, BLK), 1)
    return row, col


def _heads(x, head0):
    if head0.shape != x.shape:
        head0 = lax.broadcasted_iota(jnp.int32, x.shape, 1) < HEAD_DIM
    return jnp.where(head0, x, 0.0).astype(BF16), jnp.where(head0, 0.0, x).astype(BF16)


def _pair_spec(seq, off):
    return pl.BlockSpec((seq, BLK), lambda b, hp: (b, off + hp))


def _stat_spec(seq):
    return pl.BlockSpec((None, None, seq, BLK), lambda b, hp: (b, hp, 0, 0))


_ATT_PARAMS = dict(compiler_params=pltpu.CompilerParams(dimension_semantics=("parallel", "parallel")))
_SCALE = 1.0 / math.sqrt(HEAD_DIM)


Q_BLOCK = 256
KEY_BLOCK = 256


def _stack_heads(x, head0, scale=None):
    if scale is not None:
        x = x * scale
    return jnp.concatenate(_heads(x, head0), axis=0)


def _pair_cols(x, head0, fill):
    a = jnp.max(jnp.where(head0, x, fill), axis=1, keepdims=True)
    b = jnp.max(jnp.where(head0, fill, x), axis=1, keepdims=True)
    return jnp.concatenate([a, b], axis=0)


def _causal_mask(t0, s0, tq, kw, inclusive):
    row = lax.broadcasted_iota(jnp.int32, (2 * tq, kw), 0) & (tq - 1)
    col = lax.broadcasted_iota(jnp.int32, (2 * tq, kw), 1)
    return (s0 + col) <= (t0 + row) if inclusive else (s0 + col) < (t0 + row)


def _sub(x, j):
    return x[:, j * BLK:(j + 1) * BLK]


def _tri_blocks(kw, relation):
    r = lax.broadcasted_iota(jnp.int32, (kw, kw), 0)
    c = lax.broadcasted_iota(jnp.int32, (kw, kw), 1)
    return (((r // BLK) == (c // BLK)) & relation(r, c)).astype(BF16)


def _block_cumsum(x, tri, ksub):
    hi = x.astype(BF16)
    lo = (x - hi.astype(F32)).astype(BF16)
    cs = _dot(jnp.concatenate([hi, lo], axis=0), tri)
    n = x.shape[0]
    cs = cs[:n] + cs[n:]
    return [_sub(cs, j) for j in range(ksub)], [jnp.sum(_sub(x, j), axis=1, keepdims=True) for j in range(ksub)]


def _sb_terms_z(z, mask):
    t = jnp.log(1.0 + jnp.exp(-jnp.abs(z)))
    lsz = jnp.minimum(z, 0.0) - t
    lr = lsz - z
    if mask is not None:
        lr = jnp.where(mask, lr, 0.0)
    return lsz, lr


def _sb_fwd(p, nb, seq, *, name):
    tq = min(Q_BLOCK, seq)
    nq = seq // tq
    kw = min(KEY_BLOCK, seq)
    ksub = kw // BLK

    def body(q_ref, k_ref, v_ref, o_ref, tot_ref):
        row, col = _iotas()
        colq = lax.broadcasted_iota(jnp.int32, (tq, BLK), 1)
        head0 = colq < HEAD_DIM
        upper = _tri_blocks(kw, lambda j, s: j > s)

        def qblock(qb, c):
            t0 = pl.multiple_of(qb * tq, tq)
            qs = _stack_heads(q_ref[pl.ds(t0, tq), :], head0, _SCALE)
            diag = (t0 + tq - 1) // kw

            def kblock(kb, carry, masked):
                acc, run = carry
                s0 = pl.multiple_of(kb * kw, kw)
                k = k_ref[pl.ds(s0, kw), :].astype(BF16)
                v0, v1 = _heads(v_ref[pl.ds(s0, kw), :], head0)
                mask = _causal_mask(t0, s0, tq, kw, False)[:tq] if masked else None
                zs = [_dot(qs[h * tq:(h + 1) * tq], k, _NT) for h in range(2)]
                terms = []
                for h in range(2):
                    lsz, lr = _sb_terms_z(zs[h], mask)
                    terms.append((lsz,) + _block_cumsum(lr, upper, ksub))
                runs = []
                for h, vh in enumerate((v0, v1)):
                    lsz, after, total = terms[h]
                    r = run[h]
                    ws = [None] * ksub
                    for j in reversed(range(ksub)):
                        w = jnp.exp(_sub(lsz, j) + after[j] + r)
                        if masked:
                            w = jnp.where(_sub(mask, j), w, 0.0)
                        ws[j] = w.astype(BF16)
                        r = r + total[j]
                    acc = acc + _dot(jnp.concatenate(ws, axis=1), vh)
                    runs.append(r)
                return acc, tuple(runs)

            zc = jnp.zeros((tq, 1), F32)
            carry = kblock(diag, (jnp.zeros((tq, BLK), F32), (zc, zc)), True)
            acc, run = lax.fori_loop(0, diag, lambda it, cr: kblock(diag - 1 - it, cr, False), carry)
            o_ref[pl.ds(t0, tq), :] = acc
            tot_ref[pl.ds(t0, tq), :] = jnp.where(head0, run[0], run[1])
            return c

        lax.fori_loop(0, nq, qblock, 0)

    return pl.pallas_call(
        body, name=name, grid=(nb, PAIRS),
        in_specs=[_pair_spec(seq, 0), _pair_spec(seq, PAIRS), _pair_spec(seq, 2 * PAIRS)],
        out_specs=[_pair_spec(seq, 0), _stat_spec(seq)],
        out_shape=[jax.ShapeDtypeStruct((nb * seq, D_MODEL), F32), jax.ShapeDtypeStruct((nb, PAIRS, seq, BLK), F32)],
        **_ATT_PARAMS,
    )(p, p, p)


def _sb_bwd(p, do, tot, nb, seq, *, name):
    tq = min(Q_BLOCK, seq)
    nq = seq // tq
    kw = min(KEY_BLOCK, seq)
    ksub = kw // BLK

    def body(q_ref, k_ref, v_ref, do_ref, tot_ref, dq_ref, dk_ref, dv_ref):
        row, col = _iotas()
        colq = lax.broadcasted_iota(jnp.int32, (tq, BLK), 1)
        head0 = colq < HEAD_DIM
        lower_incl = _tri_blocks(kw, lambda j, s: j <= s)
        lower_strict = _tri_blocks(kw, lambda s, j: s < j)
        dk_ref[...] = jnp.zeros(dk_ref.shape, F32)
        dv_ref[...] = jnp.zeros(dv_ref.shape, F32)

        def qblock(qb, c):
            t0 = pl.multiple_of(qb * tq, tq)
            qs = _stack_heads(q_ref[pl.ds(t0, tq), :], head0, _SCALE)
            dos = _stack_heads(do_ref[pl.ds(t0, tq), :], head0)
            tot = tot_ref[pl.ds(t0, tq), :]
            swapped = pltpu.roll(tot, HEAD_DIM, 1)
            tts = (jnp.where(head0, tot, swapped), jnp.where(head0, swapped, tot))
            diag = (t0 + tq - 1) // kw

            def kblock(kb, carry, masked):
                dq, pfs, efs = carry
                s0 = pl.multiple_of(kb * kw, kw)
                kf = k_ref[pl.ds(s0, kw), :]
                k = kf.astype(BF16)
                khs = _heads(kf, head0)
                v = v_ref[pl.ds(s0, kw), :].astype(BF16)
                mask = _causal_mask(t0, s0, tq, kw, False)[:tq] if masked else None
                zs = [_dot(qs[h * tq:(h + 1) * tq], k, _NT) for h in range(2)]
                dws = [_dot(dos[h * tq:(h + 1) * tq], v, _NT) for h in range(2)]
                first = []
                for h in range(2):
                    lsz, lr = _sb_terms_z(zs[h], None)
                    lrm = jnp.where(mask, lr, 0.0) if masked else lr
                    first.append((lsz, lr) + _block_cumsum(lrm, lower_incl, ksub))
                second, pfs_out = [], []
                for h in range(2):
                    lsz, lr, incl, total = first[h]
                    pf = pfs[h]
                    ws, ews = [], []
                    for j in range(ksub):
                        w = jnp.exp(_sub(lsz, j) + (tts[h] - pf - incl[j]))
                        if masked:
                            w = jnp.where(_sub(mask, j), w, 0.0)
                        pf = pf + total[j]
                        ws.append(w.astype(BF16))
                        ews.append(_sub(dws[h], j) * w)
                    pfs_out.append(pf)
                    second.append((ws, ews) + _block_cumsum(jnp.concatenate(ews, axis=1), lower_strict, ksub))
                dz_h, efs_out = [], []
                for h in range(2):
                    lsz, lr = first[h][:2]
                    ws, ews, before, etotal = second[h]
                    ef = efs[h]
                    dzs = []
                    for j in range(ksub):
                        dz = ews[j] * jnp.exp(_sub(lr, j)) - (ef + before[j]) * jnp.exp(_sub(lsz, j))
                        ef = ef + etotal[j]
                        if masked:
                            dz = jnp.where(_sub(mask, j), dz, 0.0)
                        dzs.append(dz.astype(BF16))
                    efs_out.append(ef)
                    dz_h.append(jnp.concatenate(dzs, axis=1))
                    dq = dq + _dot(dz_h[h], khs[h])
                w = jnp.concatenate([jnp.concatenate(second[h][0], axis=1) for h in range(2)], axis=0)
                dk_ref[pl.ds(s0, kw), :] += _dot(jnp.concatenate(dz_h, axis=0), qs, _TN)
                dv_ref[pl.ds(s0, kw), :] += _dot(w, dos, _TN)
                return dq, tuple(pfs_out), tuple(efs_out)

            zc = jnp.zeros((tq, 1), F32)
            carry = lax.fori_loop(0, diag, lambda kb, cr: kblock(kb, cr, False), (jnp.zeros((tq, BLK), F32), (zc, zc), (zc, zc)))
            dq_ref[pl.ds(t0, tq), :] = kblock(diag, carry, True)[0] * _SCALE
            return c

        lax.fori_loop(0, nq, qblock, 0)

    t = nb * seq
    return pl.pallas_call(
        body, name=name, grid=(nb, PAIRS),
        in_specs=[_pair_spec(seq, 0), _pair_spec(seq, PAIRS), _pair_spec(seq, 2 * PAIRS), _pair_spec(seq, 0), _stat_spec(seq)],
        out_specs=[_pair_spec(seq, 0)] * 3,
        out_shape=[jax.ShapeDtypeStruct((t, D_MODEL), F32)] * 3,
        **_ATT_PARAMS,
    )(p, p, p, do, tot)


def _fox_cum(f, bf, nb, seq, *, name):
    def body(f_ref, bf_ref, cc_ref, cr_ref):
        row, col = _iotas()
        lower = (col <= row).astype(BF16)
        carry = jnp.zeros((1, BLK), F32)
        for blk in range(seq // BLK):
            rs = slice(blk * BLK, (blk + 1) * BLK)
            lf = jnp.where(col < HEADS, _log_sigmoid(f_ref[rs, :] + bf_ref[...]), 0.0)
            cc = _dot3_left(lower, lf) + carry
            cc_ref[rs, :] = cc
            cr_ref[:, rs] = cc.T[0:HEADS, :]
            carry = carry + _colsum(lf)

    return pl.pallas_call(
        body, name=name, grid=(nb,),
        in_specs=[pl.BlockSpec((seq, BLK), lambda b: (b, 0)), pl.BlockSpec((1, BLK), lambda b: (0, 0))],
        out_specs=[pl.BlockSpec((seq, BLK), lambda b: (b, 0)), pl.BlockSpec((None, HEADS, seq), lambda b: (b, 0, 0))],
        out_shape=[jax.ShapeDtypeStruct((nb * seq, BLK), F32), jax.ShapeDtypeStruct((nb, HEADS, seq), F32)],
        compiler_params=pltpu.CompilerParams(dimension_semantics=("parallel",)),
    )(f, bf)


def _fox_cum_bwd(dcr, dcc, f, bf, nb, seq, *, name):
    def body(dcr_ref, dcc_ref, f_ref, bf_ref, df_ref, dbf_ref):
        row, col = _iotas()
        upper_incl = (col >= row).astype(BF16)

        @pl.when(pl.program_id(0) == 0)
        def _():
            dbf_ref[...] = jnp.zeros((1, BLK), F32)

        carry = jnp.zeros((1, BLK), F32)
        for blk in reversed(range(seq // BLK)):
            rs = slice(blk * BLK, (blk + 1) * BLK)
            dc = dcr_ref[:, rs].T + dcc_ref[rs, :]
            dlf = _dot3_left(upper_incl, dc) + carry
            carry = carry + _colsum(dc)
            fl = f_ref[rs, :] + bf_ref[...]
            df = jnp.where(col < HEADS, dlf * _sigmoid(-fl), 0.0)
            df_ref[rs, :] = df
            dbf_ref[...] += _colsum(df)

    return pl.pallas_call(
        body, name=name, grid=(nb,),
        in_specs=[pl.BlockSpec((None, BLK, seq), lambda b: (b, 0, 0)), pl.BlockSpec((seq, BLK), lambda b: (b, 0)),
                  pl.BlockSpec((seq, BLK), lambda b: (b, 0)), pl.BlockSpec((1, BLK), lambda b: (0, 0))],
        out_specs=[pl.BlockSpec((seq, BLK), lambda b: (b, 0)), pl.BlockSpec((1, BLK), lambda b: (0, 0))],
        out_shape=[jax.ShapeDtypeStruct((nb * seq, BLK), F32), jax.ShapeDtypeStruct((1, BLK), F32)],
        compiler_params=pltpu.CompilerParams(dimension_semantics=("arbitrary",)),
    )(dcr, dcc, f, bf)


def _fox_cum_cols(cc_ref, t0, tq, colq, hp):
    cc = cc_ref[pl.ds(t0, tq), :]
    c0 = jnp.sum(jnp.where(colq == 2 * hp, cc, 0.0), axis=1, keepdims=True)
    c1 = jnp.sum(jnp.where(colq == 2 * hp + 1, cc, 0.0), axis=1, keepdims=True)
    return c0, c1


def _fox_fwd(p, cc, cr, nb, seq, *, name):
    tq = min(Q_BLOCK, seq)
    nq = seq // tq
    kw = min(KEY_BLOCK, seq)
    ksub = kw // BLK

    def body(q_ref, k_ref, v_ref, cc_ref, cr_ref, o_ref, lse_ref):
        hp = pl.program_id(1)
        row, col = _iotas()
        colq = lax.broadcasted_iota(jnp.int32, (tq, BLK), 1)
        head0 = colq < HEAD_DIM

        def qblock(qb, c):
            t0 = pl.multiple_of(qb * tq, tq)
            qs = _stack_heads(q_ref[pl.ds(t0, tq), :], head0, _SCALE)
            c0, c1 = _fox_cum_cols(cc_ref, t0, tq, colq, hp)
            diag = (t0 + tq - 1) // kw

            def kblock(kb, carry, masked):
                accs, ms = carry
                s0 = pl.multiple_of(kb * kw, kw)
                k = k_ref[pl.ds(s0, kw), :].astype(BF16)
                vf = v_ref[pl.ds(s0, kw), :]
                own0 = lax.broadcasted_iota(jnp.int32, vf.shape, 1) < HEAD_DIM
                vs = (jnp.where(own0, vf, 1.0).astype(BF16), jnp.where(own0, 1.0, vf).astype(BF16))
                mask = _causal_mask(t0, s0, tq, kw, True)[:tq] if masked else None
                zs = [_dot(qs[h * tq:(h + 1) * tq], k, _NT) for h in range(2)]
                parts = []
                for h, ch in enumerate((c0, c1)):
                    s = zs[h] + (ch - cr_ref[h:h + 1, pl.ds(s0, kw)])
                    if masked:
                        s = jnp.where(mask, s, -jnp.inf)
                    m_new = jnp.maximum(ms[h], jnp.max(s, axis=1, keepdims=True))
                    parts.append((jnp.exp(s - m_new).astype(BF16), jnp.exp(ms[h] - m_new), m_new))
                return (tuple(accs[h] * parts[h][1] + _dot(parts[h][0], vs[h]) for h in range(2)),
                        tuple(parts[h][2] for h in range(2)))

            zeros, ninf = jnp.zeros((tq, BLK), F32), jnp.full((tq, 1), -jnp.inf, F32)
            carry = lax.fori_loop(0, diag, lambda kb, cr: kblock(kb, cr, False), ((zeros, zeros), (ninf, ninf)))
            (acc0, acc1), (m0, m1) = kblock(diag, carry, True)
            l = jnp.where(head0, pltpu.roll(acc0, HEAD_DIM, 1), pltpu.roll(acc1, HEAD_DIM, 1))
            o_ref[pl.ds(t0, tq), :] = jnp.where(head0, acc0, acc1) / l
            lse_ref[pl.ds(t0, tq), :] = jnp.where(head0, m0, m1) + jnp.log(l)
            return c

        lax.fori_loop(0, nq, qblock, 0)

    return pl.pallas_call(
        body, name=name, grid=(nb, PAIRS),
        in_specs=[_pair_spec(seq, 0), _pair_spec(seq, PAIRS), _pair_spec(seq, 2 * PAIRS),
                  pl.BlockSpec((seq, BLK), lambda b, hp: (b, 0)), pl.BlockSpec((None, None, SUBLANES, seq), lambda b, hp: (b, hp, 0, 0))],
        out_specs=[_pair_spec(seq, 0), _stat_spec(seq)],
        out_shape=[jax.ShapeDtypeStruct((nb * seq, D_MODEL), F32), jax.ShapeDtypeStruct((nb, PAIRS, seq, BLK), F32)],
        **_ATT_PARAMS,
    )(p, p, p, cc, cr)


def _fox_bwd(p, do, o, lse, cc, cr, nb, seq, *, name):
    tq = min(Q_BLOCK, seq)
    nq = seq // tq
    kw = min(KEY_BLOCK, seq)
    ksub = kw // BLK

    def body(q_ref, k_ref, v_ref, do_ref, o_ref, lse_ref, cc_ref, cr_ref, dq_ref, dk_ref, dv_ref, dcr_ref, dcc_ref):
        hp = pl.program_id(1)
        row, col = _iotas()
        colq = lax.broadcasted_iota(jnp.int32, (tq, BLK), 1)
        head0 = colq < HEAD_DIM
        dk_ref[...] = jnp.zeros(dk_ref.shape, F32)
        dv_ref[...] = jnp.zeros(dv_ref.shape, F32)
        dcr_ref[...] = jnp.zeros(dcr_ref.shape, F32)

        @pl.when(hp == 0)
        def _():
            dcc_ref[...] = jnp.zeros(dcc_ref.shape, F32)

        def qblock(qb, c):
            t0 = pl.multiple_of(qb * tq, tq)
            qs = _stack_heads(q_ref[pl.ds(t0, tq), :], head0, _SCALE)
            dof = do_ref[pl.ds(t0, tq), :]
            dos = _stack_heads(dof, head0)
            prod = dof * o_ref[pl.ds(t0, tq), :]
            dl = jnp.concatenate([jnp.sum(jnp.where(head0, prod, 0.0), axis=1, keepdims=True),
                                  jnp.sum(jnp.where(head0, 0.0, prod), axis=1, keepdims=True)], axis=0)
            lse = _pair_cols(lse_ref[pl.ds(t0, tq), :], head0, -jnp.inf)
            c0, c1 = _fox_cum_cols(cc_ref, t0, tq, colq, hp)
            diag = (t0 + tq - 1) // kw

            def kblock(kb, carry, masked):
                dq, rs = carry
                s0 = pl.multiple_of(kb * kw, kw)
                kf = k_ref[pl.ds(s0, kw), :]
                k = kf.astype(BF16)
                k0, k1 = _heads(kf, head0)
                v = v_ref[pl.ds(s0, kw), :].astype(BF16)
                mask = _causal_mask(t0, s0, tq, kw, True)[:tq] if masked else None
                zs = [_dot(qs[h * tq:(h + 1) * tq], k, _NT) for h in range(2)]
                dps = [_dot(dos[h * tq:(h + 1) * tq], v, _NT) for h in range(2)]
                prs, dss, rss = [], [], []
                for h, (ch, kh) in enumerate(((c0, k0), (c1, k1))):
                    rows = slice(h * tq, (h + 1) * tq)
                    pr = jnp.exp(zs[h] + (ch - cr_ref[h:h + 1, pl.ds(s0, kw)]) - lse[rows])
                    if masked:
                        pr = jnp.where(mask, pr, 0.0)
                    ds = pr * (dps[h] - dl[rows])
                    dcr_ref[h:h + 1, pl.ds(s0, kw)] -= _colsum(ds)
                    rss.append(rs[rows] + jnp.sum(ds, axis=1, keepdims=True))
                    prs.append(pr.astype(BF16))
                    dss.append(ds.astype(BF16))
                    dq = dq + _dot(dss[h], kh)
                dk_ref[pl.ds(s0, kw), :] += _dot(jnp.concatenate(dss, axis=0), qs, _TN)
                dv_ref[pl.ds(s0, kw), :] += _dot(jnp.concatenate(prs, axis=0), dos, _TN)
                return dq, jnp.concatenate(rss, axis=0)

            init = (jnp.zeros((tq, BLK), F32), jnp.zeros((2 * tq, 1), F32))
            carry = lax.fori_loop(0, diag, lambda kb, cr: kblock(kb, cr, False), init)
            dq, rs = kblock(diag, carry, True)
            dq_ref[pl.ds(t0, tq), :] = dq * _SCALE
            dcc_ref[pl.ds(t0, tq), :] += jnp.where(colq == 2 * hp, rs[:tq], 0.0) + jnp.where(colq == 2 * hp + 1, rs[tq:], 0.0)
            return c

        lax.fori_loop(0, nq, qblock, 0)

    t = nb * seq
    return pl.pallas_call(
        body, name=name, grid=(nb, PAIRS),
        in_specs=[_pair_spec(seq, 0), _pair_spec(seq, PAIRS), _pair_spec(seq, 2 * PAIRS), _pair_spec(seq, 0), _pair_spec(seq, 0),
                  _stat_spec(seq), pl.BlockSpec((seq, BLK), lambda b, hp: (b, 0)),
                  pl.BlockSpec((None, None, SUBLANES, seq), lambda b, hp: (b, hp, 0, 0))],
        out_specs=[_pair_spec(seq, 0)] * 3 + [pl.BlockSpec((None, None, SUBLANES, seq), lambda b, hp: (b, hp, 0, 0)),
                                              pl.BlockSpec((seq, BLK), lambda b, hp: (b, 0))],
        out_shape=[jax.ShapeDtypeStruct((t, D_MODEL), F32)] * 3 + [jax.ShapeDtypeStruct((nb, PAIRS, SUBLANES, seq), F32),
                                                                     jax.ShapeDtypeStruct((t, BLK), F32)],
        compiler_params=pltpu.CompilerParams(dimension_semantics=("parallel", "arbitrary")),
    )(p, p, p, do, o, lse, cc, cr)


def _row_shards(x):
    return x.reshape(N_CHIPS, x.shape[0] // N_CHIPS, x.shape[1])


def _local_step(x3, tgt3, w, later=None, start_reduce=None):
    nb, seq, d = x3.shape
    t = nb * seq
    x0, tgt = x3.reshape(t, d), tgt3.reshape(t, d)
    g = {}

    a_gain = w["a_norm"].reshape(1, d)
    h_a, ht_a = _rmsnorm(x0, a_gain, name="a_norm_fwd")
    p_a = _matmul(h_a, w["a_w_in"][:3], name="a_in_fwd", out_dtype=BF16)
    gate_a = _matmul(h_a, w["a_w_in"][3], name="a_in_gate_fwd")
    o_a, tot_a = _sb_fwd(p_a, nb, seq, name="a_attn_fwd")
    y_a, yt_a = _gate(o_a, gate_a, 0, name="a_gate_fwd")
    x1 = _matmul(y_a, w["a_w_out"], name="a_out_fwd", residual=x0)

    if later:
        w = {**w, **later[0](x1)}
    b_gain = w["b_norm"].reshape(1, d)
    b_lg, b_lb = w["b_v_ln_g"].reshape(1, GM_W), w["b_v_ln_b"].reshape(1, GM_W)
    b_ws, b_bst = w["b_w_s"].reshape(GM_G, BLK, BLK), w["b_b_s"].reshape(GM_G, BLK).T
    h_b, ht_b = _rmsnorm(x1, b_gain, name="b_norm_fwd")
    p_b = _matmul(h_b, w["b_w_in"], name="b_in_fwd")
    y_b, yt_b = _gmlp_fwd(p_b, b_lg, b_lb, b_ws, b_bst, name="b_mix_fwd")
    x2 = _matmul(y_b, w["b_w_out"], name="b_out_fwd", residual=x1)

    if later:
        w = {**w, **later[1](x2)}
    c_gain = w["c_norm"].reshape(1, d)
    c_cw = jnp.repeat(w["c_conv_w"].reshape(CV_K, CV_W), SUBLANES, axis=0)
    c_cb = w["c_conv_b"].reshape(1, CV_W)
    c_lg, c_lb = w["c_ln_g"].reshape(1, CV_W), w["c_ln_b"].reshape(1, CV_W)
    h_c, ht_c = _rmsnorm(x2, c_gain, name="c_norm_fwd")
    p_c = _matmul(h_c, w["c_w_in"], name="c_in_fwd")
    y_c, yt_c = _conv_fwd(p_c, c_cw, c_cb, c_lg, c_lb, seq, name="c_conv_fwd")
    x3_ = _matmul(y_c, w["c_w_out"], name="c_out_fwd", residual=x2)

    d_gain = w["d_norm"].reshape(1, d)
    d_win = w["d_w_in"].reshape(d, 4 * D_MODEL + HEADS)
    d_wmain = d_win[:, :4 * D_MODEL]
    d_wf = jnp.pad(d_win[:, 4 * D_MODEL:], ((0, 0), (0, BLK - HEADS)))
    d_bf = jnp.pad(w["d_b_f"].reshape(1, HEADS), ((0, 0), (0, BLK - HEADS)))
    h_d, ht_d = _rmsnorm(x3_, d_gain, name="d_norm_fwd")
    p_d = _matmul(h_d, d_wmain[:, :3 * D_MODEL], name="d_in_fwd", out_dtype=BF16)
    gate_d = _matmul(h_d, d_wmain[:, 3 * D_MODEL:], name="d_in_gate_fwd")
    f_d = _matmul(h_d, d_wf, name="d_inf_fwd")
    cc, cr = _fox_cum(f_d, d_bf, nb, seq, name="d_cum_fwd")
    cr = jnp.pad(cr.reshape(nb, PAIRS, 2, seq), ((0, 0), (0, 0), (0, SUBLANES - 2), (0, 0)))
    o_d, lse_d = _fox_fwd(p_d, cc, cr, nb, seq, name="d_attn_fwd")
    y_d, yt_d = _gate(o_d, gate_d, 0, name="d_gate_fwd")
    x4 = _matmul(y_d, w["d_w_out"], name="d_out_fwd", residual=x3_)

    f_gain = w["final_norm"].reshape(1, d)
    dx, g_fn, loss_row = _loss_head(x4, f_gain, tgt, name="loss_head")
    g["final_norm"] = g_fn

    g["d_w_out"] = _row_shards(_matmul(yt_d, dx, name="d_out_dw"))
    dy = _matmul(dx, w["d_w_out"], name="d_out_dy", mode="nt")
    do_d, dg_d = _gate_bwd(dy, o_d, gate_d, 0, name="d_gate_bwd")
    dq, dk, dv, dcr, dcc = _fox_bwd(p_d, do_d, o_d, lse_d, cc, cr, nb, seq, name="d_attn_bwd")
    dcr = jnp.pad(dcr[:, :, :2, :].reshape(nb, HEADS, seq), ((0, 0), (0, BLK - HEADS), (0, 0)))
    df, dbf = _fox_cum_bwd(dcr, dcc, f_d, d_bf, nb, seq, name="d_cum_bwd")
    g["d_b_f"] = dbf[:, :HEADS]
    parts = [dq, dk, dv, dg_d]
    dws = _matmul_parts(ht_d, parts, name="d_in_dw")
    dwf = _matmul(ht_d, df, name="d_inf_dw")
    g["d_w_in"] = jnp.concatenate([dws[n] for n in range(4)] + [dwf[:, :HEADS]], axis=1).reshape(d, N_CHIPS, -1).transpose(1, 0, 2)
    dh = _matmul_sum([(df, d_wf)] + [(pt, d_wmain[:, n * D_MODEL:(n + 1) * D_MODEL]) for n, pt in enumerate(parts)],
                     name="d_in_dh")
    dx, g["d_norm"] = _rmsnorm_bwd(dh, x3_, d_gain, dx, name="d_norm_bwd")

    g["c_w_out"] = _row_shards(_matmul(yt_c, dx, name="c_out_dw"))
    dy = _matmul(dx, w["c_w_out"], name="c_out_dy", mode="nt")
    dy1, dgate, g["c_ln_g"], g["c_ln_b"], g["c_conv_b"], g["c_conv_w"] = _conv_bwd_post(
        dy, p_c, c_cw, c_cb, c_lg, c_lb, seq, name="c_conv_bwd_post")
    dp = _conv_bwd_pre(dy1, dgate, p_c, c_cw, seq, name="c_conv_bwd_pre")
    g["c_w_in"] = _matmul(ht_c, dp, name="c_in_dw", out_shards=N_CHIPS)
    dh = _matmul(dp, w["c_w_in"], name="c_in_dh", mode="nt")
    dx, g["c_norm"] = _rmsnorm_bwd(dh, x2, c_gain, dx, name="c_norm_bwd")

    early, b_wout, a_wout = [], w["b_w_out"], w["a_w_out"]
    if start_reduce is not None:
        begun, token = start_reduce({n: g[n] for n in ("d_w_in", "d_w_out", "c_w_in", "c_w_out")}, "grads_cd")
        early.append(begun)
        b_wout = b_wout + token[0, 0].astype(b_wout.dtype)
    g["b_w_out"] = _row_shards(_matmul(yt_b, dx, name="b_out_dw"))
    dy = _matmul(dx, b_wout, name="b_out_dy", mode="nt")
    dp, g["b_v_ln_g"], g["b_v_ln_b"], g["b_w_s"], dbst = _gmlp_bwd(dy, p_b, b_lg, b_lb, b_ws, b_bst, name="b_mix_bwd")
    g["b_b_s"] = dbst.T
    g["b_w_in"] = _matmul(ht_b, dp, name="b_in_dw", out_shards=N_CHIPS)
    dh = _matmul(dp, w["b_w_in"], name="b_in_dh", mode="nt")
    dx, g["b_norm"] = _rmsnorm_bwd(dh, x1, b_gain, dx, name="b_norm_bwd")

    if start_reduce is not None:
        begun, token = start_reduce({n: g[n] for n in ("b_w_in", "b_w_out")}, "grads_b")
        early.append(begun)
        a_wout = a_wout + token[0, 0].astype(a_wout.dtype)
    g["a_w_out"] = _row_shards(_matmul(yt_a, dx, name="a_out_dw"))
    dy = _matmul(dx, a_wout, name="a_out_dy", mode="nt")
    do_a, dg_a = _gate_bwd(dy, o_a, gate_a, 0, name="a_gate_bwd")
    dq, dk, dv = _sb_bwd(p_a, do_a, tot_a, nb, seq, name="a_attn_bwd")
    parts = [dq, dk, dv, dg_a]
    g["a_w_in"] = _matmul_parts(ht_a, parts, name="a_in_dw")
    dh = _matmul_sum([(pt, w["a_w_in"][n]) for n, pt in enumerate(parts)], name="a_in_dh")
    dx, g["a_norm"] = _rmsnorm_bwd(dh, x0, a_gain, dx, name="a_norm_bwd")

    return loss_row[0, 0], dx.reshape(nb, seq, d), g, early


_HBM = pl.BlockSpec(memory_space=pltpu.HBM)


def _place():
    return lax.axis_index("x"), lax.axis_index("y"), lax.axis_index("c")


def _other_chips(x, y):
    return [(1 - x, y), (x, 1 - y), (1 - x, 1 - y)]


def _allgather_chips(ss, *, name):
    n_ops = len(ss)

    def body(*refs):
        s_refs, o_refs, (send_sems, recv_sems) = refs[:n_ops], refs[n_ops:2 * n_ops], refs[2 * n_ops:]
        x, y, c = _place()
        me = 2 * x + y
        chips = _other_chips(x, y)

        def copy(i, kk, src, dst, to):
            return pltpu.make_async_remote_copy(src_ref=src, dst_ref=dst, send_sem=send_sems.at[6 * i + kk],
                                                recv_sem=recv_sems.at[6 * i + kk], device_id=to, device_id_type=MESH)

        def half(i, j, hc):
            h = s_refs[i].shape[0] // 2
            return o_refs[i].at[j, pl.ds(hc * h, h), :]

        first = [copy(i, kk, s_refs[i].at[pl.ds(c * (s_refs[i].shape[0] // 2), s_refs[i].shape[0] // 2), :], half(i, me, c),
                      (cx, cy, c)) for kk, (cx, cy) in enumerate(chips) for i in range(n_ops)]
        for cp in first:
            cp.start()
        passed = []
        for kk, (cx, cy) in enumerate(chips):
            for i in range(n_ops):
                blk = half(i, 2 * cx + cy, c)
                copy(i, kk, blk, blk, (cx, cy, c)).wait_recv()
                fwd = copy(i, 3 + kk, blk, blk, (x, y, 1 - c))
                fwd.start()
                passed.append(fwd)
        for kk, (cx, cy) in enumerate(chips):
            for i in range(n_ops):
                blk = half(i, 2 * cx + cy, 1 - c)
                copy(i, 3 + kk, blk, blk, (x, y, 1 - c)).wait_recv()
        for cp in first + passed:
            cp.wait_send()

    for s in ss:
        assert s.shape[0] % 32 == 0, s.shape
    return pl.pallas_call(
        body, name=name, in_specs=[_HBM] * n_ops, out_specs=[_HBM] * n_ops,
        out_shape=[jax.ShapeDtypeStruct((N_CHIPS,) + s.shape, s.dtype) for s in ss],
        scratch_shapes=[pltpu.SemaphoreType.DMA((6 * n_ops,)), pltpu.SemaphoreType.DMA((6 * n_ops,))],
    )(*ss)


_SEM = pl.BlockSpec(memory_space=pltpu.SEMAPHORE)
_ANY = pl.BlockSpec(memory_space=pl.ANY)
_DATAFLOW = pltpu.SideEffectType.DATAFLOW_SIDE_EFFECTING


def _chip_copies(s_refs, land_refs, send_sems, recv_sems):
    x, y, c = _place()
    me = 2 * x + y
    cps = []
    for i, (s_ref, land_ref) in enumerate(zip(s_refs, land_refs)):
        h = s_ref.shape[0] // 2
        for kk, (cx, cy) in enumerate(_other_chips(x, y)):
            cps.append(pltpu.make_async_remote_copy(
                src_ref=s_ref.at[pl.ds(c * h, h), :], dst_ref=land_ref.at[me, pl.ds(c * h, h), :], send_sem=send_sems.at[3 * i + kk],
                recv_sem=recv_sems.at[3 * i + kk], device_id=(cx, cy, c), device_id_type=MESH))
    return cps


def _gather_start(ss, after, *, name):
    n = len(ss)
    lands = [lax.empty((N_CHIPS,) + s.shape, s.dtype) for s in ss]

    def body(*refs):
        s_refs, land_refs = refs[:n], refs[n:2 * n]
        send_sems, recv_sems = refs[2 * n + 1], refs[2 * n + 2]
        token = refs[-1]
        for cp in _chip_copies(s_refs, land_refs, send_sems, recv_sems):
            cp.start()
        token[...] = jnp.zeros(token.shape, token.dtype)

    hbm = [pltpu.HBM(a.shape, a.dtype) for a in list(ss) + lands]
    res = pl.pallas_call(
        body, name=name,
        out_shape=(pltpu.SemaphoreType.DMA((3 * n,)), pltpu.SemaphoreType.DMA((3 * n,)), *hbm, jax.ShapeDtypeStruct((8, BLK), F32)),
        in_specs=[_HBM] * (2 * n) + [_ANY],
        out_specs=(_SEM, _SEM, *([_HBM] * (2 * n)), pl.BlockSpec(memory_space=pltpu.VMEM)),
        input_output_aliases={i: 2 + i for i in range(2 * n)},
        compiler_params=pltpu.CompilerParams(has_side_effects=_DATAFLOW),
    )(*[pltpu.with_memory_space_constraint(a, pltpu.HBM) for a in list(ss) + lands], after)
    return res[:-1], res[-1]


def _gather_wait(started, after, *, name):
    send_sems, recv_sems = started[0], started[1]
    n = (len(started) - 2) // 2

    def body(*refs):
        s_refs, land_refs = refs[:n], refs[n:2 * n]
        for cp in _chip_copies(s_refs, land_refs, refs[2 * n], refs[2 * n + 1]):
            cp.wait_send()
            cp.wait_recv()

    res = pl.pallas_call(
        body, name=name, out_shape=tuple(pltpu.HBM(a.shape, a.dtype) for a in started[2:]),
        in_specs=[_HBM] * (2 * n) + [_SEM, _SEM, _ANY], out_specs=tuple([_HBM] * (2 * n)),
        input_output_aliases={i: i for i in range(2 * n)},
        compiler_params=pltpu.CompilerParams(has_side_effects=_DATAFLOW),
    )(*started[2:], send_sems, recv_sems, after)
    return list(res[n:])


def _sibling_exchange(lands, *, name):
    n = len(lands)

    def body(*refs):
        o_refs, (send_sems, recv_sems) = refs[n:2 * n], refs[2 * n:]
        x, y, c = _place()
        cps = []
        for i, o_ref in enumerate(o_refs):
            h = o_ref.shape[1] // 2
            for kk, (cx, cy) in enumerate(_other_chips(x, y)):
                def half(hc):
                    return o_ref.at[2 * cx + cy, pl.ds(hc * h, h), :]
                sent = pltpu.make_async_remote_copy(src_ref=half(c), dst_ref=half(c), send_sem=send_sems.at[3 * i + kk],
                                                    recv_sem=recv_sems.at[3 * i + kk], device_id=(x, y, 1 - c), device_id_type=MESH)
                awaited = pltpu.make_async_remote_copy(src_ref=half(1 - c), dst_ref=half(1 - c), send_sem=send_sems.at[3 * i + kk],
                                                       recv_sem=recv_sems.at[3 * i + kk], device_id=(x, y, 1 - c),
                                                       device_id_type=MESH)
                cps.append((sent, awaited))
        for sent, _ in cps:
            sent.start()
        for sent, awaited in cps:
            awaited.wait_recv()
            sent.wait_send()

    return pl.pallas_call(
        body, name=name, in_specs=[_HBM] * n, out_specs=[_HBM] * n,
        out_shape=[jax.ShapeDtypeStruct(a.shape, a.dtype) for a in lands], scratch_shapes=_dma_sems(3 * n),
        input_output_aliases={i: i for i in range(n)},
    )(*lands)


def _own_block(gathered, s):
    me = 2 * lax.axis_index("x") + lax.axis_index("y")
    return lax.dynamic_update_slice(gathered, s[None], (me,) + (0,) * s.ndim)


def _dma_sems(n):
    return [pltpu.SemaphoreType.DMA((n,)), pltpu.SemaphoreType.DMA((n,))]


def _swap_halves(gps, *, name):
    n_ops = len(gps)

    def body(*refs):
        g_refs, o_refs, (send_sems, recv_sems) = refs[:n_ops], refs[n_ops:2 * n_ops], refs[2 * n_ops:]
        x, y, c = _place()
        cps = []
        for i, (g_ref, o_ref) in enumerate(zip(g_refs, o_refs)):
            h = g_ref.shape[1] // 2
            cps.append(pltpu.make_async_remote_copy(
                src_ref=g_ref.at[:, pl.ds((1 - c) * h, h), :], dst_ref=o_ref, send_sem=send_sems.at[i], recv_sem=recv_sems.at[i],
                device_id=(x, y, 1 - c), device_id_type=MESH))
        for cp in cps:
            cp.start()
        for cp in cps:
            cp.wait()

    return pl.pallas_call(
        body, name=name, in_specs=[_HBM] * n_ops, out_specs=[_HBM] * n_ops,
        out_shape=[jax.ShapeDtypeStruct((g.shape[0], g.shape[1] // 2, g.shape[2]), g.dtype) for g in gps],
        scratch_shapes=_dma_sems(n_ops),
    )(*gps)


def _scatter_chips(hps, *, name):
    n_ops = len(hps)

    def body(*refs):
        h_refs, o_refs, (send_sems, recv_sems) = refs[:n_ops], refs[n_ops:2 * n_ops], refs[2 * n_ops:]
        x, y, c = _place()
        cps = [pltpu.make_async_remote_copy(src_ref=h_ref.at[2 * cx + cy], dst_ref=o_ref.at[kk], send_sem=send_sems.at[3 * i + kk],
                                            recv_sem=recv_sems.at[3 * i + kk], device_id=(cx, cy, c), device_id_type=MESH)
               for i, (h_ref, o_ref) in enumerate(zip(h_refs, o_refs)) for kk, (cx, cy) in enumerate(_other_chips(x, y))]
        for cp in cps:
            cp.start()
        for cp in cps:
            cp.wait()

    return pl.pallas_call(
        body, name=name, in_specs=[_HBM] * n_ops, out_specs=[_HBM] * n_ops,
        out_shape=[jax.ShapeDtypeStruct((3,) + hp.shape[1:], hp.dtype) for hp in hps],
        scratch_shapes=_dma_sems(3 * n_ops),
    )(*hps)


def _join_halves(fs, *, name):
    n_ops = len(fs)

    def body(*refs):
        f_refs, o_refs, (send_sems, recv_sems) = refs[:n_ops], refs[n_ops:2 * n_ops], refs[2 * n_ops:]
        x, y, c = _place()
        cps = [pltpu.make_async_remote_copy(src_ref=f_ref, dst_ref=o_ref, send_sem=send_sems.at[i], recv_sem=recv_sems.at[i],
                                            device_id=(x, y, 1 - c), device_id_type=MESH)
               for i, (f_ref, o_ref) in enumerate(zip(f_refs, o_refs))]
        for cp in cps:
            cp.start()
        for cp in cps:
            cp.wait()

    return pl.pallas_call(
        body, name=name, in_specs=[_HBM] * n_ops, out_specs=[_HBM] * n_ops,
        out_shape=[jax.ShapeDtypeStruct(f.shape, f.dtype) for f in fs], scratch_shapes=_dma_sems(n_ops),
    )(*fs)


def _stitch(mine, theirs):
    south = lax.axis_index("c") == 0
    return jnp.concatenate([jnp.where(south, mine, theirs), jnp.where(south, theirs, mine)], axis=0)


def _add_halves(gp, ra, wire_dtype, *, name, bm=256):
    n, r, c_ = gp.shape
    h = r // 2
    bm = _tile(h, bm)
    per = h // bm
    c = lax.axis_index("c").astype(jnp.int32).reshape(1)

    def body(c_ref, g_ref, ra_ref, o_ref, ow_ref):
        s = g_ref[...] + ra_ref[...]
        o_ref[...] = s
        ow_ref[...] = s.astype(wire_dtype)

    mine = pl.BlockSpec((None, bm, c_), lambda j, i, cr: (j, i, 0))
    return pl.pallas_call(
        body, name=name,
        grid_spec=pltpu.PrefetchScalarGridSpec(
            num_scalar_prefetch=1, grid=(n, per),
            in_specs=[pl.BlockSpec((None, bm, c_), lambda j, i, cr: (j, cr[0] * per + i, 0)), mine],
            out_specs=[mine, mine]),
        out_shape=[jax.ShapeDtypeStruct((n, h, c_), F32), jax.ShapeDtypeStruct((n, h, c_), wire_dtype)],
        compiler_params=pltpu.CompilerParams(dimension_semantics=("parallel", "parallel")),
    )(c, gp, ra)


def _add_chips(hp, rb, *, name, bm=256):
    n, h, c_ = hp.shape
    bm = _tile(h, bm)
    me = (2 * lax.axis_index("x") + lax.axis_index("y")).astype(jnp.int32).reshape(1)

    def body(me_ref, h_ref, rb_ref, o_ref):
        o_ref[...] = ((h_ref[...] + rb_ref[0].astype(F32)) + rb_ref[1].astype(F32)) + rb_ref[2].astype(F32)

    return pl.pallas_call(
        body, name=name,
        grid_spec=pltpu.PrefetchScalarGridSpec(
            num_scalar_prefetch=1, grid=(h // bm,),
            in_specs=[pl.BlockSpec((None, bm, c_), lambda i, mr: (mr[0], i, 0)),
                      pl.BlockSpec((3, bm, c_), lambda i, mr: (0, i, 0))],
            out_specs=pl.BlockSpec((bm, c_), lambda i, mr: (i, 0))),
        out_shape=jax.ShapeDtypeStruct((h, c_), F32),
        compiler_params=pltpu.CompilerParams(dimension_semantics=("parallel",)),
    )(me, hp, rb)


def _scatter_copies(h_refs, land_refs, send_sems, recv_sems):
    x, y, c = _place()
    return [pltpu.make_async_remote_copy(src_ref=h_ref.at[2 * cx + cy], dst_ref=land_ref.at[kk], send_sem=send_sems.at[3 * i + kk],
                                         recv_sem=recv_sems.at[3 * i + kk], device_id=(cx, cy, c), device_id_type=MESH)
            for i, (h_ref, land_ref) in enumerate(zip(h_refs, land_refs)) for kk, (cx, cy) in enumerate(_other_chips(x, y))]


def _scatter_start(hps, after, *, name):
    n = len(hps)
    lands = [lax.empty((3,) + hp.shape[1:], hp.dtype) for hp in hps]

    def body(*refs):
        for cp in _scatter_copies(refs[:n], refs[n:2 * n], refs[2 * n + 1], refs[2 * n + 2]):
            cp.start()
        refs[-1][...] = jnp.zeros(refs[-1].shape, refs[-1].dtype)

    hbm = [pltpu.HBM(a.shape, a.dtype) for a in list(hps) + lands]
    res = pl.pallas_call(
        body, name=name,
        out_shape=(pltpu.SemaphoreType.DMA((3 * n,)), pltpu.SemaphoreType.DMA((3 * n,)), *hbm, jax.ShapeDtypeStruct((8, BLK), F32)),
        in_specs=[_HBM] * (2 * n) + [_ANY],
        out_specs=(_SEM, _SEM, *([_HBM] * (2 * n)), pl.BlockSpec(memory_space=pltpu.VMEM)),
        input_output_aliases={i: 2 + i for i in range(2 * n)},
        compiler_params=pltpu.CompilerParams(has_side_effects=_DATAFLOW),
    )(*[pltpu.with_memory_space_constraint(a, pltpu.HBM) for a in list(hps) + lands], after)
    return res[:-1], res[-1]


def _scatter_wait(started, after, *, name):
    n = (len(started) - 2) // 2

    def body(*refs):
        for cp in _scatter_copies(refs[:n], refs[n:2 * n], refs[2 * n], refs[2 * n + 1]):
            cp.wait_send()
            cp.wait_recv()

    res = pl.pallas_call(
        body, name=name, out_shape=tuple(pltpu.HBM(a.shape, a.dtype) for a in started[2:]),
        in_specs=[_HBM] * (2 * n) + [_SEM, _SEM, _ANY], out_specs=tuple([_HBM] * (2 * n)),
        input_output_aliases={i: i for i in range(2 * n)},
        compiler_params=pltpu.CompilerParams(has_side_effects=_DATAFLOW),
    )(*started[2:], started[0], started[1], after)
    return list(res[n:])


def _reduce_to_chips(gps, wire_dtypes, *, tag):
    ras = _swap_halves(gps, name=f"{tag}_swap_halves")
    return [_add_halves(gp, ra, wd, name=f"{tag}_add_halves{i}") for i, (gp, ra, wd) in enumerate(zip(gps, ras, wire_dtypes))]


def _start_reduce(early, tag):
    names = list(early)
    hps = _reduce_to_chips([early[n] for n in names], [BF16] * len(names), tag=tag)
    started, token = _scatter_start([hw for _, hw in hps], hps[-1][1], name=f"{tag}_scatter_start")
    return (tag, names, [hf for hf, _ in hps], started), token


def _adamw_math(w_ref, g_ref, m_ref, v_ref, d_ref, nm_ref, nv_ref):
    c1 = 1.0 - ADAM_B1 ** ADAM_STEP
    c2 = 1.0 - ADAM_B2 ** ADAM_STEP
    g_ = g_ref[...]
    m_ = ADAM_B1 * m_ref[...] + (1.0 - ADAM_B1) * g_
    v_ = ADAM_B2 * v_ref[...] + (1.0 - ADAM_B2) * (g_ * g_)
    d_ref[...] = -ADAM_LR * ((m_ / c1) / (jnp.sqrt(v_ / c2) + ADAM_EPS) + ADAM_WD * w_ref[...])
    nm_ref[...] = m_
    nv_ref[...] = v_


def _adamw_many(groups, *, name):
    n = len(groups[0])
    flat = [a for grp in groups for a in grp]

    def body(*refs):
        ins, outs = refs[:4 * n], refs[4 * n:]
        for i in range(n):
            _adamw_math(ins[i], ins[n + i], ins[2 * n + i], ins[3 * n + i], outs[i], outs[n + i], outs[2 * n + i])

    vmem = pl.BlockSpec(memory_space=pltpu.VMEM)
    res = pl.pallas_call(
        body, name=name, in_specs=[vmem] * (4 * n), out_specs=[vmem] * (3 * n),
        out_shape=[jax.ShapeDtypeStruct(a.shape, F32) for _ in range(3) for a in groups[0]],
    )(*flat)
    return res[:n], res[n:2 * n], res[2 * n:]


def _adamw(w, g_mine, g_theirs, m, v, *, name):
    r, c_ = w.shape
    h = r // 2
    bm = _tile(h, 256)
    per = h // bm
    c = lax.axis_index("c").astype(jnp.int32).reshape(1)

    def body(c_ref, w_ref, f_ref, t_ref, m_ref, v_ref, g_ref, d_ref, nm_ref, nv_ref):
        first_half = pl.program_id(0) < per
        mine = jnp.where(jnp.where(first_half, c_ref[0] == 0, c_ref[0] == 1), 1.0, 0.0)
        g_ref[...] = mine * f_ref[...] + (1.0 - mine) * t_ref[...]
        _adamw_math(w_ref, g_ref, m_ref, v_ref, d_ref, nm_ref, nv_ref)

    full = pl.BlockSpec((bm, c_), lambda i, cr: (i, 0))
    half = pl.BlockSpec((bm, c_), lambda i, cr: (i % per, 0))
    return pl.pallas_call(
        body, name=name,
        grid_spec=pltpu.PrefetchScalarGridSpec(num_scalar_prefetch=1, grid=(r // bm,), in_specs=[full, half, half, full, full],
                                               out_specs=[full] * 4),
        out_shape=[jax.ShapeDtypeStruct((r, c_), F32)] * 4,
        compiler_params=pltpu.CompilerParams(dimension_semantics=("parallel",)),
    )(c, w, g_mine, g_theirs, m, v)


_WEIGHTS = ["a_norm", "a_w_in", "a_w_out", "b_norm", "b_w_in", "b_v_ln_g", "b_v_ln_b", "b_w_s", "b_b_s", "b_w_out",
            "c_norm", "c_w_in", "c_conv_w", "c_conv_b", "c_ln_g", "c_ln_b", "c_w_out", "d_norm", "d_w_in", "d_b_f",
            "d_w_out", "final_norm"]
_SHARD_AXIS = {"a_norm": None, "a_w_in": 2, "a_w_out": 1, "b_norm": 1, "b_w_in": 2, "b_v_ln_g": 1, "b_v_ln_b": 1, "b_w_s": None,
               "b_b_s": None, "b_w_out": 1, "c_norm": 1, "c_w_in": 2, "c_conv_w": 2, "c_conv_b": 1, "c_ln_g": 1, "c_ln_b": 1,
               "c_w_out": 1, "d_norm": 1, "d_w_in": 2, "d_b_f": None, "d_w_out": 1, "final_norm": None}
_BIG = ["a_w_in", "a_w_out", "b_w_in", "b_w_out", "c_w_in", "c_w_out", "d_w_in", "d_w_out"]
_GATHER_GROUPS = (("a_w_in", "a_w_out"), ("b_w_in", "b_w_out"), ("c_w_in", "c_w_out", "d_w_in", "d_w_out"))
_SMALL_SHARDED = [n for n in _WEIGHTS if _SHARD_AXIS[n] is not None and n not in _BIG]
_REPLICATED = [n for n in _WEIGHTS if _SHARD_AXIS[n] is None]
_ROW_ALIGN = 32


def _pack(pieces, dtype, align=_ROW_ALIGN):
    flat = jnp.concatenate([p.reshape(-1).astype(dtype) for p in pieces])
    unit = align * PACK_C
    total = -(-flat.shape[0] // unit) * unit
    return jnp.pad(flat, (0, total - flat.shape[0])).reshape(total // PACK_C, PACK_C)


def _unpack(flat, shapes):
    out, off = [], 0
    for s in shapes:
        n = math.prod(s)
        out.append(flat[off:off + n].reshape(s))
        off += n
    return out


def _full_shape(local_shape, axis):
    s = list(local_shape)
    if axis is not None:
        s[axis] *= N_CHIPS
    return tuple(s)


def _gather_weights(local):
    def whole(n, gt):
        if _SHARD_AXIS[n] == 1:
            return gt.reshape(-1, gt.shape[-1])
        if n == "d_w_in":
            return gt.transpose(1, 0, 2).reshape(gt.shape[1], -1)
        return gt

    full = {n: local[n][0] if n != "final_norm" else local[n] for n in _REPLICATED}
    first = list(_GATHER_GROUPS[0])
    mine = [local[n][0].astype(BF16) for n in first] + [_pack([local[n] for n in _SMALL_SHARDED], F32)]
    got = [_own_block(gt, s) for gt, s in zip(_allgather_chips(mine, name="gather_weights"), mine)]
    full.update({n: whole(n, gt) for n, gt in zip(first, got)})
    small = got[-1].reshape(N_CHIPS, -1)
    shards = [_unpack(small[j], [local[n].shape[1:] for n in _SMALL_SHARDED]) for j in range(N_CHIPS)]
    for i, n in enumerate(_SMALL_SHARDED):
        full[n] = jnp.concatenate([shards[j][i] for j in range(N_CHIPS)], axis=_SHARD_AXIS[n] - 1)

    def begin(k, after):
        shards_k = [local[n][0].astype(BF16) for n in _GATHER_GROUPS[k]]
        started, token = _gather_start(shards_k, after, name=f"gather{k}_start")
        return shards_k, started, token

    pending = [begin(1, got[0])]
    full["a_norm"] = full["a_norm"] + pending[0][2][0, 0]

    def finish(k):
        def weights(after):
            shards_k, started, _ = pending[k - 1]
            lands = _gather_wait(started, after, name=f"gather{k}_wait")
            token = None
            if k + 1 < len(_GATHER_GROUPS):
                pending.append(begin(k + 1, lands[0]))
                token = pending[k][2]
            lands = _sibling_exchange(lands, name=f"gather{k}_exchange")
            out = {n: whole(n, _own_block(gt, s)) for n, gt, s in zip(_GATHER_GROUPS[k], lands, shards_k)}
            if token is not None:
                gain = _GATHER_GROUPS[k][0][0] + "_norm"
                out[gain] = full[gain] + token[0, 0]
            return out
        return weights

    return full, [finish(k) for k in range(1, len(_GATHER_GROUPS))]


def _repl_piece_len(local):
    total = sum(math.prod(local[n].shape) for n in _REPLICATED)
    return -(-total // N_CHIPS)


def _reduce_grads(g, local, early):
    rep_flat = jnp.concatenate([g[n].reshape(-1) for n in _REPLICATED])
    piece = _repl_piece_len(local)
    rep_flat = jnp.pad(rep_flat, (0, N_CHIPS * piece - rep_flat.shape[0]))

    def shard(n, j):
        full = g[n].reshape(_full_shape(local[n].shape, _SHARD_AXIS[n]))
        width = local[n].shape[_SHARD_AXIS[n]]
        return lax.slice_in_dim(full, j * width, (j + 1) * width, axis=_SHARD_AXIS[n])

    small = jnp.stack([_pack([shard(n, j) for n in _SMALL_SHARDED] + [rep_flat[j * piece:(j + 1) * piece]], F32)
                       for j in range(N_CHIPS)])
    early_names = [n for _, names, _, _ in early for n in names]
    late = [n for n in _BIG if n not in early_names]
    hps = _reduce_to_chips([g[n] for n in late] + [small], [BF16] * len(late) + [F32], tag="grads")
    rbs = list(_scatter_chips([hw for _, hw in hps], name="grads_scatter_chips"))
    early_halves, early_rbs = [], []
    for tag, _, halves_k, started in early:
        early_halves += halves_k
        early_rbs += _scatter_wait(started, rbs[0], name=f"{tag}_scatter_wait")
    halves = early_halves + [hf for hf, _ in hps]
    fs = [_add_chips(hf, rb, name=f"grads_add_chips{i}") for i, (hf, rb) in enumerate(zip(halves, early_rbs + rbs))]
    theirs = _join_halves(fs, name="grads_join_halves")
    red = dict(zip(early_names + late, zip(fs, theirs)))
    out = _unpack(_stitch(fs[-1], theirs[-1]).reshape(-1), [local[n].shape for n in _SMALL_SHARDED] + [(piece,)])
    red.update(zip(_SMALL_SHARDED, out[:-1]))
    rep_mine = _pack([out[-1]], F32)
    rep = _own_block(_allgather_chips([rep_mine], name="gather_replicated_grads")[0], rep_mine)
    rep = rep.reshape(N_CHIPS, -1)[:, :piece].reshape(-1)
    for n, val in zip(_REPLICATED, _unpack(rep, [local[n].shape for n in _REPLICATED])):
        red[n] = val
    return red


def _update(local, grads, m, v):
    grads, delta, new_m, new_v = dict(grads), {}, {}, {}
    for n in _BIG:
        shp = local[n].shape
        two = (shp[-2], shp[-1])
        res = _adamw(local[n].reshape(two), *grads[n], m[n].reshape(two), v[n].reshape(two), name=f"adamw_{n}")
        grads[n], delta[n], new_m[n], new_v[n] = [r.reshape(shp) for r in res]
    small = [n for n in _WEIGHTS if n not in _BIG]
    two = {n: (math.prod(local[n].shape[:-1]), local[n].shape[-1]) for n in small}
    res = _adamw_many([[src[n].reshape(two[n]) for n in small] for src in (local, grads, m, v)], name="adamw_small")
    for dst, rs in zip((delta, new_m, new_v), res):
        for n, val in zip(small, rs):
            dst[n] = val.reshape(local[n].shape)
    return grads, delta, new_m, new_v


def kernel(x, a_norm, a_w_in, a_w_out, b_norm, b_w_in, b_v_ln_g, b_v_ln_b, b_w_s, b_b_s, b_w_out, c_norm, c_w_in, c_conv_w, c_conv_b, c_ln_g, c_ln_b, c_w_out, d_norm, d_w_in, d_b_f, d_w_out, final_norm, loss_target, m_a_norm, m_a_w_in, m_a_w_out, m_b_norm, m_b_w_in, m_b_v_ln_g, m_b_v_ln_b, m_b_w_s, m_b_b_s, m_b_w_out, m_c_norm, m_c_w_in, m_c_conv_w, m_c_conv_b, m_c_ln_g, m_c_ln_b, m_c_w_out, m_d_norm, m_d_w_in, m_d_b_f, m_d_w_out, m_final_norm, v_a_norm, v_a_w_in, v_a_w_out, v_b_norm, v_b_w_in, v_b_v_ln_g, v_b_v_ln_b, v_b_w_s, v_b_b_s, v_b_w_out, v_c_norm, v_c_w_in, v_c_conv_w, v_c_conv_b, v_c_ln_g, v_c_ln_b, v_c_w_out, v_d_norm, v_d_w_in, v_d_b_f, v_d_w_out, v_final_norm):
    local = dict(zip(_WEIGHTS, (a_norm, a_w_in, a_w_out, b_norm, b_w_in, b_v_ln_g, b_v_ln_b, b_w_s, b_b_s, b_w_out, c_norm, c_w_in,
                                c_conv_w, c_conv_b, c_ln_g, c_ln_b, c_w_out, d_norm, d_w_in, d_b_f, d_w_out, final_norm)))
    m = dict(zip(_WEIGHTS, (m_a_norm, m_a_w_in, m_a_w_out, m_b_norm, m_b_w_in, m_b_v_ln_g, m_b_v_ln_b, m_b_w_s, m_b_b_s, m_b_w_out,
                            m_c_norm, m_c_w_in, m_c_conv_w, m_c_conv_b, m_c_ln_g, m_c_ln_b, m_c_w_out, m_d_norm, m_d_w_in, m_d_b_f,
                            m_d_w_out, m_final_norm)))
    v = dict(zip(_WEIGHTS, (v_a_norm, v_a_w_in, v_a_w_out, v_b_norm, v_b_w_in, v_b_v_ln_g, v_b_v_ln_b, v_b_w_s, v_b_b_s, v_b_w_out,
                            v_c_norm, v_c_w_in, v_c_conv_w, v_c_conv_b, v_c_ln_g, v_c_ln_b, v_c_w_out, v_d_norm, v_d_w_in, v_d_b_f,
                            v_d_w_out, v_final_norm)))
    loss_part, grad_x, g, early = _local_step(x, loss_target, *_gather_weights(local), _start_reduce)
    loss = lax.psum(loss_part, ("x", "y", "c"))
    grads = _reduce_grads(g, local, early)
    grads, delta, new_m, new_v = _update(local, grads, m, v)
    return (loss, grad_x, *[grads[n] for n in _WEIGHTS], *[delta[n] for n in _WEIGHTS],
            *[new_m[n] for n in _WEIGHTS], *[new_v[n] for n in _WEIGHTS])
```

```python
import math

import jax
import jax.numpy as jnp
from jax import lax
from jax.experimental import pallas as pl
from jax.experimental.pallas import tpu as pltpu

F32, BF16 = jnp.float32, jnp.bfloat16
MESH = pl.DeviceIdType.MESH

D_MODEL = 1024
HEADS = 16
HEAD_DIM = 64
BLK = 128
PAIRS = HEADS // 2
GM_W = 2048
GM_G = 16
CV_W = 2048
CV_K = 31
HALO = 32
EPS = 1e-6
N_CHIPS = 4
PACK_C = 1024
ADAM_LR, ADAM_B1, ADAM_B2, ADAM_EPS, ADAM_WD, ADAM_STEP = 0.001, 0.9, 0.999, 1e-08, 0.01, 10

_NT = (((1,), (1,)), ((), ()))
_TN = (((0,), (0,)), ((), ()))
_NN = (((1,), (0,)), ((), ()))


def _dot(a, b, dims=_NN):
    return lax.dot_general(a, b, dims, preferred_element_type=F32)


def _split3(x):
    hi = x.astype(BF16)
    r = x - hi.astype(F32)
    mid = r.astype(BF16)
    lo = (r - mid.astype(F32)).astype(BF16)
    return hi, mid, lo


def _dot3_left(m, x):
    hi, mid, lo = _split3(x)
    return _dot(m, hi) + _dot(m, mid) + _dot(m, lo)


def _sigmoid(x):
    return 1.0 / (1.0 + jnp.exp(-x))


def _silu(x):
    return x * _sigmoid(x)


def _dsilu(x):
    s = _sigmoid(x)
    return s * (1.0 + x * (1.0 - s))


_GELU_C = math.sqrt(2.0 / math.pi)
_GELU_A = 0.044715


def _gelu(x):
    return 0.5 * x * (1.0 + jnp.tanh(_GELU_C * (x + _GELU_A * x * x * x)))


def _dgelu(x):
    t = jnp.tanh(_GELU_C * (x + _GELU_A * x * x * x))
    return 0.5 * (1.0 + t) + 0.5 * x * (1.0 - t * t) * _GELU_C * (1.0 + 3.0 * _GELU_A * x * x)


def _log_sigmoid(x):
    return jnp.minimum(x, 0.0) - jnp.log(1.0 + jnp.exp(-jnp.abs(x)))


def _rms_fwd(x, g):
    r = lax.rsqrt(jnp.mean(x * x, axis=-1, keepdims=True) + EPS)
    return x * r * g


def _rms_bwd(dy, x, g):
    r = lax.rsqrt(jnp.mean(x * x, axis=-1, keepdims=True) + EPS)
    xh = x * r
    dxh = dy * g
    dx = r * (dxh - xh * jnp.mean(dxh * xh, axis=-1, keepdims=True))
    return dx, dy * xh


def _ln_stats(x):
    mu = jnp.mean(x, axis=-1, keepdims=True)
    xc = x - mu
    r = lax.rsqrt(jnp.mean(xc * xc, axis=-1, keepdims=True) + EPS)
    return xc * r, r


def _ln_bwd(dy, xh, r, g):
    dxh = dy * g
    return r * (dxh - jnp.mean(dxh, axis=-1, keepdims=True) - xh * jnp.mean(dxh * xh, axis=-1, keepdims=True))


def _colsum(x):
    return jnp.sum(x, axis=0, keepdims=True)


def _tile(n, want):
    for t in range(min(n, want), 7, -1):
        if n % t == 0 and t % 8 == 0:
            return t
    return n


MM_TILE = 1024


def _matmul(a, b, *, name, mode="nn", residual=None, out_shards=1, out_dtype=F32):
    (m, k) = a.shape
    b_shards = b.shape[0] if b.ndim == 3 else 1
    if mode == "nn":
        n = b.shape[-1] * b_shards
        tn, tk = _tile(n // max(b_shards, out_shards), MM_TILE), _tile(k, MM_TILE)
    else:
        n = b.shape[-2]
        tn, tk = _tile(n // out_shards, MM_TILE), _tile(k // b_shards, MM_TILE)
    tm = _tile(m, MM_TILE)
    nk = k // tk
    a_spec = pl.BlockSpec((tm, tk), lambda i, j, kk: (i, kk))
    if mode == "nn":
        dims = _NN
        if b_shards == 1:
            b_spec = pl.BlockSpec((tk, tn), lambda i, j, kk: (kk, j))
        else:
            per_b = n // b_shards // tn
            b_spec = pl.BlockSpec((None, tk, tn), lambda i, j, kk: (j // per_b, kk, j % per_b))
    else:
        dims = _NT
        if b_shards == 1:
            b_spec = pl.BlockSpec((tn, tk), lambda i, j, kk: (j, kk))
        else:
            per_b = k // b_shards // tk
            b_spec = pl.BlockSpec((None, tn, tk), lambda i, j, kk: (kk // per_b, j, kk % per_b))
    if out_shards == 1:
        o_spec = pl.BlockSpec((tm, tn), lambda i, j, kk: (i, j))
        o_shape = (m, n)
    else:
        per_o = n // out_shards // tn
        o_spec = pl.BlockSpec((None, tm, tn), lambda i, j, kk: (j // per_o, i, j % per_o))
        o_shape = (out_shards, m, n // out_shards)
    has_res = residual is not None

    def body(a_ref, b_ref, *rest):
        o_ref = rest[-1]
        kk = pl.program_id(2)
        part = _dot(a_ref[...].astype(BF16), b_ref[...].astype(BF16), dims)
        if has_res:
            @pl.when(kk == 0)
            def _():
                o_ref[...] = part + rest[0][...]
        else:
            @pl.when(kk == 0)
            def _():
                o_ref[...] = part.astype(out_dtype)

        if nk > 1:
            @pl.when(kk > 0)
            def _():
                o_ref[...] += part

    assert out_dtype == F32 or nk == 1
    return pl.pallas_call(
        body, name=name, grid=(m // tm, n // tn, nk),
        in_specs=[a_spec, b_spec] + ([o_spec] if has_res else []),
        out_specs=o_spec, out_shape=jax.ShapeDtypeStruct(o_shape, out_dtype),
        compiler_params=pltpu.CompilerParams(dimension_semantics=("parallel", "parallel", "arbitrary")),
    )(a, b, *([residual] if has_res else []))


def _matmul_parts(at, parts, *, name):
    m, k = at.shape
    n = parts[0].shape[1]
    tm, tk = _tile(m, MM_TILE // 2), _tile(k, MM_TILE // 2)
    n_parts = len(parts)

    def body(a_ref, *rest):
        o_ref = rest[-1]
        kk = pl.program_id(1)
        a = a_ref[...].astype(BF16)
        for p in range(n_parts):
            part = _dot(a, rest[p][...].astype(BF16))

            @pl.when(kk == 0)
            def _():
                o_ref[p] = part

            @pl.when(kk > 0)
            def _():
                o_ref[p] += part

    return pl.pallas_call(
        body, name=name, grid=(m // tm, k // tk),
        in_specs=[pl.BlockSpec((tm, tk), lambda i, kk: (i, kk))] + [pl.BlockSpec((tk, n), lambda i, kk: (kk, 0))] * n_parts,
        out_specs=pl.BlockSpec((n_parts, tm, n), lambda i, kk: (0, i, 0)),
        out_shape=jax.ShapeDtypeStruct((n_parts, m, n), F32),
        compiler_params=pltpu.CompilerParams(dimension_semantics=("parallel", "arbitrary")),
    )(at, *parts)


def _matmul_sum(pairs, *, name):
    m, n = pairs[0][0].shape[0], pairs[0][1].shape[0]
    tm, tn = _tile(m, MM_TILE // 2), _tile(n, MM_TILE)
    n_pairs = len(pairs)

    def body(*refs):
        acc = None
        for p in range(n_pairs):
            part = _dot(refs[2 * p][...].astype(BF16), refs[2 * p + 1][...].astype(BF16), _NT)
            acc = part if acc is None else acc + part
        refs[-1][...] = acc

    in_specs = []
    for a, b in pairs:
        in_specs += [pl.BlockSpec((tm, a.shape[1]), lambda i, j: (i, 0)), pl.BlockSpec((tn, b.shape[1]), lambda i, j: (j, 0))]
    return pl.pallas_call(
        body, name=name, grid=(m // tm, n // tn), in_specs=in_specs, out_specs=pl.BlockSpec((tm, tn), lambda i, j: (i, j)),
        out_shape=jax.ShapeDtypeStruct((m, n), F32),
        compiler_params=pltpu.CompilerParams(dimension_semantics=("parallel", "parallel")),
    )(*[x for pair in pairs for x in pair])


def _rows(fn, *, name, steps, ins, outs, accs=(), scratch=()):
    ni, no, na = len(ins), len(outs), len(accs)

    def body(*refs):
        in_refs, out_refs = refs[:ni], refs[ni:ni + no]
        acc_refs, scr = refs[ni + no:ni + no + na], refs[ni + no + na:]
        i = pl.program_id(0)

        @pl.when(i == 0)
        def _():
            for r in acc_refs:
                r[...] = jnp.zeros(r.shape, r.dtype)

        fn(i, in_refs, out_refs, acc_refs, scr)

    def full(shape):
        nd = len(shape)
        return pl.BlockSpec(tuple(shape), lambda i: (0,) * nd)

    res = pl.pallas_call(
        body, name=name, grid=(steps,),
        in_specs=[pl.BlockSpec(bs, im) for _, bs, im in ins],
        out_specs=[pl.BlockSpec(bs, im) for _, _, bs, im in outs] + [full(s) for s, _ in accs],
        out_shape=[jax.ShapeDtypeStruct(s, d) for s, d, _, _ in outs] + [jax.ShapeDtypeStruct(s, d) for s, d in accs],
        scratch_shapes=list(scratch),
        compiler_params=pltpu.CompilerParams(dimension_semantics=("arbitrary",)),
    )(*[a for a, _, _ in ins])
    return res


def _rb(arr, bm, cb=0, width=None):
    w = arr.shape[1] if width is None else width
    return (arr, (bm, w), lambda i: (i, cb))


def _const(arr):
    nd = arr.ndim
    return (arr, tuple(arr.shape), lambda i: (0,) * nd)


def _ro(t, w, dtype, bm):
    return ((t, w), dtype, (bm, w), lambda i: (i, 0))


def _rot(t, w, dtype, bm):
    return ((w, t), dtype, (w, bm), lambda i: (0, i))


def _rmsnorm(x, g, *, name, bm=512):
    t, d = x.shape
    bm = _tile(t, bm)

    def fn(i, ins, outs, accs, scr):
        h = _rms_fwd(ins[0][...], ins[1][...])
        outs[0][...] = h.astype(BF16)
        outs[1][...] = h.T.astype(BF16)

    return _rows(fn, name=name, steps=t // bm, ins=[_rb(x, bm), _const(g)], outs=[_ro(t, d, BF16, bm), _rot(t, d, BF16, bm)])


def _rmsnorm_bwd(dh, x, g, dres, *, name, bm=512):
    t, d = x.shape
    bm = _tile(t, bm)

    def fn(i, ins, outs, accs, scr):
        dx, dgrow = _rms_bwd(ins[0][...], ins[1][...], ins[2][...])
        outs[0][...] = ins[3][...] + dx
        accs[0][...] += _colsum(dgrow)

    return _rows(fn, name=name, steps=t // bm, ins=[_rb(dh, bm), _rb(x, bm), _const(g), _rb(dres, bm)],
                 outs=[_ro(t, d, F32, bm)], accs=[((1, d), F32)])


def _gate(o, p, gcb, *, name, bm=512):
    t, w = o.shape
    bm = _tile(t, bm)

    def fn(i, ins, outs, accs, scr):
        y = ins[0][...] * _silu(ins[1][...])
        outs[0][...] = y.astype(BF16)
        outs[1][...] = y.T.astype(BF16)

    return _rows(fn, name=name, steps=t // bm, ins=[_rb(o, bm), _rb(p, bm, gcb, w)],
                 outs=[_ro(t, w, BF16, bm), _rot(t, w, BF16, bm)])


def _gate_bwd(dy, o, p, gcb, *, name, bm=512):
    t, w = o.shape
    bm = _tile(t, bm)

    def fn(i, ins, outs, accs, scr):
        dy_, o_, g_ = ins[0][...], ins[1][...], ins[2][...]
        outs[0][...] = dy_ * _silu(g_)
        outs[1][...] = dy_ * o_ * _dsilu(g_)

    return _rows(fn, name=name, steps=t // bm, ins=[_rb(dy, bm), _rb(o, bm), _rb(p, bm, gcb, w)],
                 outs=[_ro(t, w, F32, bm), _ro(t, w, F32, bm)])


def _loss_head(x, g, tgt, *, name, bm=512):
    t, d = x.shape
    bm = _tile(t, bm)

    def fn(i, ins, outs, accs, scr):
        x_, g_, tg = ins[0][...], ins[1][...], ins[2][...]
        err = _rms_fwd(x_, g_) - tg
        part = 0.5 * jnp.sum(jnp.sum(err * err, axis=-1, keepdims=True), axis=0, keepdims=True) / d
        dx, dgrow = _rms_bwd(err / d, x_, g_)
        outs[0][...] = dx
        accs[0][...] += _colsum(dgrow)
        accs[1][...] += jnp.broadcast_to(part, (1, BLK))

    return _rows(fn, name=name, steps=t // bm, ins=[_rb(x, bm), _const(g), _rb(tgt, bm)],
                 outs=[_ro(t, d, F32, bm)], accs=[((1, d), F32), ((1, BLK), F32)])


def _gmlp_mix_weights(ws_ref, g):
    row = lax.broadcasted_iota(jnp.int32, (BLK, BLK), 0)
    col = lax.broadcasted_iota(jnp.int32, (BLK, BLK), 1)
    tril = col <= row
    return jnp.where(tril, ws_ref[g], 0.0), tril


def _gmlp_fwd(p, ln_g, ln_b, w_s, bs_t, *, name):
    t = p.shape[0]

    def fn(i, ins, outs, accs, scr):
        p_ref, lg, lb, ws_ref, bst = ins
        vn = _ln_stats(_gelu(p_ref[:, GM_W:2 * GM_W]))[0] * lg[...] + lb[...]
        for g in range(GM_G):
            cs = slice(g * BLK, (g + 1) * BLK)
            wt, _ = _gmlp_mix_weights(ws_ref, g)
            s = _dot(wt.astype(BF16), vn[:, cs].astype(BF16)) + bst[:, g:g + 1]
            u = _gelu(p_ref[:, cs])
            gate = p_ref[:, 2 * GM_W + g * BLK:2 * GM_W + (g + 1) * BLK]
            y = u * s * _silu(gate)
            outs[0][:, cs] = y.astype(BF16)
            outs[1][cs, :] = y.T.astype(BF16)

    return _rows(fn, name=name, steps=t // BLK, ins=[_rb(p, BLK), _const(ln_g), _const(ln_b), _const(w_s), _const(bs_t)],
                 outs=[_ro(t, GM_W, BF16, BLK), _rot(t, GM_W, BF16, BLK)])


def _gmlp_bwd(dy, p, ln_g, ln_b, w_s, bs_t, *, name):
    t = p.shape[0]

    def fn(i, ins, outs, accs, scr):
        dy_ref, p_ref, lg, lb, ws_ref, bst = ins
        dp_ref = outs[0]
        dlg, dlb, dws, dbst = accs
        dvn_ref = scr[0]
        v_pre = p_ref[:, GM_W:2 * GM_W]
        xh, r = _ln_stats(_gelu(v_pre))
        vn = xh * lg[...] + lb[...]
        for g in range(GM_G):
            cs = slice(g * BLK, (g + 1) * BLK)
            gs = slice(2 * GM_W + g * BLK, 2 * GM_W + (g + 1) * BLK)
            wt, tril = _gmlp_mix_weights(ws_ref, g)
            vg = vn[:, cs].astype(BF16)
            s = _dot(wt.astype(BF16), vg) + bst[:, g:g + 1]
            u_pre, gate, dyg = p_ref[:, cs], p_ref[:, gs], dy_ref[:, cs]
            u = _gelu(u_pre)
            dos = dyg * _silu(gate)
            dp_ref[:, gs] = dyg * u * s * _dsilu(gate)
            dp_ref[:, cs] = dos * s * _dgelu(u_pre)
            ds = (dos * u).astype(BF16)
            dws[g] += jnp.where(tril, _dot(ds, vg, _NT), 0.0)
            dbst[:, g:g + 1] += jnp.sum(dos * u, axis=1, keepdims=True)
            dvn_ref[:, cs] = _dot(wt.astype(BF16), ds, _TN)
        dvn = dvn_ref[...]
        dlg[...] += _colsum(dvn * xh)
        dlb[...] += _colsum(dvn)
        dp_ref[:, GM_W:2 * GM_W] = _ln_bwd(dvn, xh, r, lg[...]) * _dgelu(v_pre)

    return _rows(fn, name=name, steps=t // BLK,
                 ins=[_rb(dy, BLK), _rb(p, BLK), _const(ln_g), _const(ln_b), _const(w_s), _const(bs_t)],
                 outs=[_ro(t, 3 * GM_W, F32, BLK)],
                 accs=[((1, GM_W), F32), ((1, GM_W), F32), ((GM_G, BLK, BLK), F32), ((BLK, GM_G), F32)],
                 scratch=[pltpu.VMEM((BLK, GM_W), F32)])


CV_BM = 128
CV_RC = 8
SUBLANES = 8
CV_FWD_OFFS = [HALO - (CV_K - 1) + k for k in range(CV_K)]
CV_BWD_OFFS = [CV_K - 1 - k for k in range(CV_K)]


def _conv_halo_prev(p, cb, bm):
    per = bm // HALO
    return (p, (HALO, CV_W), lambda i: (jnp.maximum(i * per - 1, 0), cb))


def _conv_scratch(bm):
    return [pltpu.VMEM((bm + HALO, CV_W), F32), pltpu.VMEM((SUBLANES - 1, bm + HALO - SUBLANES, CV_W), F32),
            pltpu.VMEM((bm, CV_W), F32)]


def _conv_shift_copies(ext_ref, sh_ref):
    rows = sh_ref.shape[1]
    for b in range(1, SUBLANES):
        sh_ref[b - 1] = ext_ref[pl.ds(b, rows), :]


def _conv_window(ext_ref, sh_ref, off, r0, rows):
    b = off % SUBLANES
    src = ext_ref if b == 0 else sh_ref.at[b - 1]
    return src[pl.ds(r0 + (off - b), rows), :]


def _conv_taps(ext_ref, sh_ref, cw_ref, y_ref, offs):
    bm = y_ref.shape[0]

    def chunk(ci, c):
        r0 = pl.multiple_of(ci * CV_RC, CV_RC)
        acc = jnp.zeros((CV_RC, CV_W), F32)
        for k in range(CV_K):
            acc = acc + cw_ref[pl.ds(k * SUBLANES, CV_RC), :] * _conv_window(ext_ref, sh_ref, offs[k], r0, CV_RC)
        y_ref[pl.ds(r0, CV_RC), :] = acc
        return c

    lax.fori_loop(0, bm // CV_RC, chunk, 0)


def _conv_dweights(dy1_ref, ext_ref, sh_ref, dcw_ref):
    bm = dy1_ref.shape[0]
    groups = 4
    for k in range(CV_K):
        def step(ci, acc, off=CV_FWD_OFFS[k]):
            prods = []
            for u in range(groups):
                r0 = pl.multiple_of((ci * groups + u) * CV_RC, CV_RC)
                prods.append(dy1_ref[pl.ds(r0, CV_RC), :] * _conv_window(ext_ref, sh_ref, off, r0, CV_RC))
            return acc + ((prods[0] + prods[1]) + (prods[2] + prods[3]))

        dcw_ref[k:k + 1, :] += _colsum(lax.fori_loop(0, bm // (CV_RC * groups), step, jnp.zeros((CV_RC, CV_W), F32)))


def _conv_fill(i, ext_ref, a_prev, b_prev, a, b, bm, seq):
    keep = jnp.where((i % (seq // bm)) == 0, 0.0, 1.0)
    ext_ref[pl.ds(0, HALO), :] = keep * (a_prev * _sigmoid(b_prev))
    ext_ref[pl.ds(HALO, bm), :] = a * _sigmoid(b)


def _conv_fwd(p, cw, cb, ln_g, ln_b, seq, *, name, bm=CV_BM):
    t = p.shape[0]

    def fn(i, ins, outs, accs, scr):
        a, b, gate, ap, bp = [r[...] for r in ins[:5]]
        cw_ref, cb_, lg, lb = ins[5], ins[6][...], ins[7][...], ins[8][...]
        ext, sh, y = scr
        _conv_fill(i, ext, ap, bp, a, b, bm, seq)
        _conv_shift_copies(ext, sh)
        _conv_taps(ext, sh, cw_ref, y, CV_FWD_OFFS)
        y2 = _ln_stats(y[...] + cb_)[0] * lg + lb
        out = _silu(y2) * _silu(gate)
        outs[0][...] = out.astype(BF16)
        outs[1][...] = out.T.astype(BF16)

    return _rows(fn, name=name, steps=t // bm,
                 ins=[_rb(p, bm, 0, CV_W), _rb(p, bm, 1, CV_W), _rb(p, bm, 2, CV_W),
                      _conv_halo_prev(p, 0, bm), _conv_halo_prev(p, 1, bm),
                      _const(cw), _const(cb), _const(ln_g), _const(ln_b)],
                 outs=[_ro(t, CV_W, BF16, bm), _rot(t, CV_W, BF16, bm)], scratch=_conv_scratch(bm))


def _conv_bwd_post(dy, p, cw, cb, ln_g, ln_b, seq, *, name, bm=CV_BM):
    t = p.shape[0]

    def fn(i, ins, outs, accs, scr):
        dy_, a, b, gate, ap, bp = [r[...] for r in ins[:6]]
        cw_ref, cb_, lg, lb = ins[6], ins[7][...], ins[8][...], ins[9][...]
        dlg, dlb, dcb, dcw = accs
        ext, sh, y = scr
        _conv_fill(i, ext, ap, bp, a, b, bm, seq)
        _conv_shift_copies(ext, sh)
        _conv_taps(ext, sh, cw_ref, y, CV_FWD_OFFS)
        xh, r = _ln_stats(y[...] + cb_)
        y2 = xh * lg + lb
        outs[1][...] = dy_ * _silu(y2) * _dsilu(gate)
        dy2 = dy_ * _silu(gate) * _dsilu(y2)
        dlg[...] += _colsum(dy2 * xh)
        dlb[...] += _colsum(dy2)
        dy1 = _ln_bwd(dy2, xh, r, lg)
        outs[0][...] = dy1
        dcb[...] += _colsum(dy1)
        _conv_dweights(outs[0], ext, sh, dcw)

    return _rows(fn, name=name, steps=t // bm,
                 ins=[_rb(dy, bm), _rb(p, bm, 0, CV_W), _rb(p, bm, 1, CV_W), _rb(p, bm, 2, CV_W),
                      _conv_halo_prev(p, 0, bm), _conv_halo_prev(p, 1, bm),
                      _const(cw), _const(cb), _const(ln_g), _const(ln_b)],
                 outs=[_ro(t, CV_W, F32, bm), _ro(t, CV_W, F32, bm)],
                 accs=[((1, CV_W), F32), ((1, CV_W), F32), ((1, CV_W), F32), ((CV_K, CV_W), F32)],
                 scratch=_conv_scratch(bm))


def _conv_bwd_pre(dy1, dgate, p, cw, seq, *, name, bm=CV_BM):
    t = p.shape[0]
    per = bm // HALO
    last_halo = t // HALO - 1

    def fn(i, ins, outs, accs, scr):
        d1, d1n, dg, a, b = [r[...] for r in ins[:5]]
        ext, sh, y = scr
        keep = jnp.where((i % (seq // bm)) == (seq // bm - 1), 0.0, 1.0)
        ext[pl.ds(0, bm), :] = d1
        ext[pl.ds(bm, HALO), :] = keep * d1n
        _conv_shift_copies(ext, sh)
        _conv_taps(ext, sh, ins[5], y, CV_BWD_OFFS)
        dy0 = y[...]
        sb = _sigmoid(b)
        outs[0][:, 0:CV_W] = dy0 * sb
        outs[0][:, CV_W:2 * CV_W] = dy0 * a * sb * (1.0 - sb)
        outs[0][:, 2 * CV_W:3 * CV_W] = dg

    return _rows(fn, name=name, steps=t // bm,
                 ins=[_rb(dy1, bm), (dy1, (HALO, CV_W), lambda i: (jnp.minimum((i + 1) * per, last_halo), 0)),
                      _rb(dgate, bm), _rb(p, bm, 0, CV_W), _rb(p, bm, 1, CV_W), _const(cw)],
                 outs=[_ro(t, 3 * CV_W, F32, bm)], scratch=_conv_scratch(bm))[0]


def _iotas():
    row = lax.broadcasted_iota(jnp.int32, (BLK, BLK), 0)
    col = lax.broadcasted_iota(jnp.int32, (BLK, BLK), 1)
    return row, col


def _heads(x, head0):
    if head0.shape != x.shape:
        head0 = lax.broadcasted_iota(jnp.int32, x.shape, 1) < HEAD_DIM
    return jnp.where(head0, x, 0.0).astype(BF16), jnp.where(head0, 0.0, x).astype(BF16)


def _pair_spec(seq, off):
    return pl.BlockSpec((seq, BLK), lambda b, hp: (b, off + hp))


def _stat_spec(seq):
    return pl.BlockSpec((None, None, seq, BLK), lambda b, hp: (b, hp, 0, 0))


_ATT_PARAMS = dict(compiler_params=pltpu.CompilerParams(dimension_semantics=("parallel", "parallel")))
_SCALE = 1.0 / math.sqrt(HEAD_DIM)


Q_BLOCK = 256
KEY_BLOCK = 256
FOX_KEY_BLOCK = 512


def _stack_heads(x, head0, scale=None):
    if scale is not None:
        x = x * scale
    return jnp.concatenate(_heads(x, head0), axis=0)


def _pair_cols(x, head0, fill):
    a = jnp.max(jnp.where(head0, x, fill), axis=1, keepdims=True)
    b = jnp.max(jnp.where(head0, fill, x), axis=1, keepdims=True)
    return jnp.concatenate([a, b], axis=0)


def _causal_mask(t0, s0, tq, kw, inclusive):
    row = lax.broadcasted_iota(jnp.int32, (2 * tq, kw), 0) & (tq - 1)
    col = lax.broadcasted_iota(jnp.int32, (2 * tq, kw), 1)
    return (s0 + col) <= (t0 + row) if inclusive else (s0 + col) < (t0 + row)


def _sub(x, j):
    return x[:, j * BLK:(j + 1) * BLK]


def _tri_blocks(kw, relation):
    r = lax.broadcasted_iota(jnp.int32, (kw, kw), 0)
    c = lax.broadcasted_iota(jnp.int32, (kw, kw), 1)
    return (((r // BLK) == (c // BLK)) & relation(r, c)).astype(BF16)


def _block_cumsum(x, tri, ksub):
    hi = x.astype(BF16)
    lo = (x - hi.astype(F32)).astype(BF16)
    cs = _dot(jnp.concatenate([hi, lo], axis=0), tri)
    n = x.shape[0]
    cs = cs[:n] + cs[n:]
    return [_sub(cs, j) for j in range(ksub)], [jnp.sum(_sub(x, j), axis=1, keepdims=True) for j in range(ksub)]


def _sb_terms_z(z, mask):
    t = jnp.log(1.0 + jnp.exp(-jnp.abs(z)))
    lsz = jnp.minimum(z, 0.0) - t
    lr = lsz - z
    if mask is not None:
        lr = jnp.where(mask, lr, 0.0)
    return lsz, lr


def _sb_fwd(p, nb, seq, *, name):
    tq = min(Q_BLOCK, seq)
    nq = seq // tq
    kw = min(KEY_BLOCK, seq)
    ksub = kw // BLK

    def body(q_ref, k_ref, v_ref, o_ref, tot_ref):
        row, col = _iotas()
        colq = lax.broadcasted_iota(jnp.int32, (tq, BLK), 1)
        head0 = colq < HEAD_DIM
        upper = _tri_blocks(kw, lambda j, s: j > s)

        def qblock(qb, c):
            t0 = pl.multiple_of(qb * tq, tq)
            qs = _stack_heads(q_ref[pl.ds(t0, tq), :], head0, _SCALE)
            diag = (t0 + tq - 1) // kw

            def kblock(kb, carry, masked):
                acc, run = carry
                s0 = pl.multiple_of(kb * kw, kw)
                k = k_ref[pl.ds(s0, kw), :].astype(BF16)
                v0, v1 = _heads(v_ref[pl.ds(s0, kw), :], head0)
                mask = _causal_mask(t0, s0, tq, kw, False)[:tq] if masked else None
                zs = [_dot(qs[h * tq:(h + 1) * tq], k, _NT) for h in range(2)]
                terms = []
                for h in range(2):
                    lsz, lr = _sb_terms_z(zs[h], mask)
                    terms.append((lsz,) + _block_cumsum(lr, upper, ksub))
                runs = []
                for h, vh in enumerate((v0, v1)):
                    lsz, after, total = terms[h]
                    r = run[h]
                    ws = [None] * ksub
                    for j in reversed(range(ksub)):
                        w = jnp.exp(_sub(lsz, j) + after[j] + r)
                        if masked:
                            w = jnp.where(_sub(mask, j), w, 0.0)
                        ws[j] = w.astype(BF16)
                        r = r + total[j]
                    acc = acc + _dot(jnp.concatenate(ws, axis=1), vh)
                    runs.append(r)
                return acc, tuple(runs)

            zc = jnp.zeros((tq, 1), F32)
            carry = kblock(diag, (jnp.zeros((tq, BLK), F32), (zc, zc)), True)
            acc, run = lax.fori_loop(0, diag, lambda it, cr: kblock(diag - 1 - it, cr, False), carry)
            o_ref[pl.ds(t0, tq), :] = acc
            tot_ref[pl.ds(t0, tq), :] = jnp.where(head0, run[0], run[1])
            return c

        lax.fori_loop(0, nq, qblock, 0)

    return pl.pallas_call(
        body, name=name, grid=(nb, PAIRS),
        in_specs=[_pair_spec(seq, 0), _pair_spec(seq, PAIRS), _pair_spec(seq, 2 * PAIRS)],
        out_specs=[_pair_spec(seq, 0), _stat_spec(seq)],
        out_shape=[jax.ShapeDtypeStruct((nb * seq, D_MODEL), F32), jax.ShapeDtypeStruct((nb, PAIRS, seq, BLK), F32)],
        **_ATT_PARAMS,
    )(p, p, p)


def _sb_bwd(p, do, tot, nb, seq, *, name):
    tq = min(Q_BLOCK, seq)
    nq = seq // tq
    kw = min(KEY_BLOCK, seq)
    ksub = kw // BLK

    def body(q_ref, k_ref, v_ref, do_ref, tot_ref, dq_ref, dk_ref, dv_ref):
        row, col = _iotas()
        colq = lax.broadcasted_iota(jnp.int32, (tq, BLK), 1)
        head0 = colq < HEAD_DIM
        lower_incl = _tri_blocks(kw, lambda j, s: j <= s)
        lower_strict = _tri_blocks(kw, lambda s, j: s < j)
        dk_ref[...] = jnp.zeros(dk_ref.shape, F32)
        dv_ref[...] = jnp.zeros(dv_ref.shape, F32)

        def qblock(qb, c):
            t0 = pl.multiple_of(qb * tq, tq)
            qs = _stack_heads(q_ref[pl.ds(t0, tq), :], head0, _SCALE)
            dos = _stack_heads(do_ref[pl.ds(t0, tq), :], head0)
            tot = tot_ref[pl.ds(t0, tq), :]
            swapped = pltpu.roll(tot, HEAD_DIM, 1)
            tts = (jnp.where(head0, tot, swapped), jnp.where(head0, swapped, tot))
            diag = (t0 + tq - 1) // kw

            def kblock(kb, carry, masked):
                dq, pfs, efs = carry
                s0 = pl.multiple_of(kb * kw, kw)
                kf = k_ref[pl.ds(s0, kw), :]
                k = kf.astype(BF16)
                khs = _heads(kf, head0)
                v = v_ref[pl.ds(s0, kw), :].astype(BF16)
                mask = _causal_mask(t0, s0, tq, kw, False)[:tq] if masked else None
                zs = [_dot(qs[h * tq:(h + 1) * tq], k, _NT) for h in range(2)]
                dws = [_dot(dos[h * tq:(h + 1) * tq], v, _NT) for h in range(2)]
                first = []
                for h in range(2):
                    lsz, lr = _sb_terms_z(zs[h], None)
                    lrm = jnp.where(mask, lr, 0.0) if masked else lr
                    first.append((lsz, lr) + _block_cumsum(lrm, lower_incl, ksub))
                second, pfs_out = [], []
                for h in range(2):
                    lsz, lr, incl, total = first[h]
                    pf = pfs[h]
                    ws, ews = [], []
                    for j in range(ksub):
                        w = jnp.exp(_sub(lsz, j) + (tts[h] - pf - incl[j]))
                        if masked:
                            w = jnp.where(_sub(mask, j), w, 0.0)
                        pf = pf + total[j]
                        ws.append(w.astype(BF16))
                        ews.append(_sub(dws[h], j) * w)
                    pfs_out.append(pf)
                    second.append((ws, ews) + _block_cumsum(jnp.concatenate(ews, axis=1), lower_strict, ksub))
                dz_h, efs_out = [], []
                for h in range(2):
                    lsz, lr = first[h][:2]
                    ws, ews, before, etotal = second[h]
                    ef = efs[h]
                    dzs = []
                    for j in range(ksub):
                        dz = ews[j] * jnp.exp(_sub(lr, j)) - (ef + before[j]) * jnp.exp(_sub(lsz, j))
                        ef = ef + etotal[j]
                        if masked:
                            dz = jnp.where(_sub(mask, j), dz, 0.0)
                        dzs.append(dz.astype(BF16))
                    efs_out.append(ef)
                    dz_h.append(jnp.concatenate(dzs, axis=1))
                    dq = dq + _dot(dz_h[h], khs[h])
                w = jnp.concatenate([jnp.concatenate(second[h][0], axis=1) for h in range(2)], axis=0)
                dk_ref[pl.ds(s0, kw), :] += _dot(jnp.concatenate(dz_h, axis=0), qs, _TN)
                dv_ref[pl.ds(s0, kw), :] += _dot(w, dos, _TN)
                return dq, tuple(pfs_out), tuple(efs_out)

            zc = jnp.zeros((tq, 1), F32)
            carry = lax.fori_loop(0, diag, lambda kb, cr: kblock(kb, cr, False), (jnp.zeros((tq, BLK), F32), (zc, zc), (zc, zc)))
            dq_ref[pl.ds(t0, tq), :] = kblock(diag, carry, True)[0] * _SCALE
            return c

        lax.fori_loop(0, nq, qblock, 0)

    t = nb * seq
    return pl.pallas_call(
        body, name=name, grid=(nb, PAIRS),
        in_specs=[_pair_spec(seq, 0), _pair_spec(seq, PAIRS), _pair_spec(seq, 2 * PAIRS), _pair_spec(seq, 0), _stat_spec(seq)],
        out_specs=[_pair_spec(seq, 0)] * 3,
        out_shape=[jax.ShapeDtypeStruct((t, D_MODEL), F32)] * 3,
        **_ATT_PARAMS,
    )(p, p, p, do, tot)


def _fox_cum(f, bf, nb, seq, *, name):
    def body(f_ref, bf_ref, cc_ref, cr_ref):
        row, col = _iotas()
        lower = (col <= row).astype(BF16)
        carry = jnp.zeros((1, BLK), F32)
        for blk in range(seq // BLK):
            rs = slice(blk * BLK, (blk + 1) * BLK)
            lf = jnp.where(col < HEADS, _log_sigmoid(f_ref[rs, :] + bf_ref[...]), 0.0)
            cc = _dot3_left(lower, lf) + carry
            cc_ref[rs, :] = cc
            cr_ref[:, rs] = cc.T[0:HEADS, :]
            carry = carry + _colsum(lf)

    return pl.pallas_call(
        body, name=name, grid=(nb,),
        in_specs=[pl.BlockSpec((seq, BLK), lambda b: (b, 0)), pl.BlockSpec((1, BLK), lambda b: (0, 0))],
        out_specs=[pl.BlockSpec((seq, BLK), lambda b: (b, 0)), pl.BlockSpec((None, HEADS, seq), lambda b: (b, 0, 0))],
        out_shape=[jax.ShapeDtypeStruct((nb * seq, BLK), F32), jax.ShapeDtypeStruct((nb, HEADS, seq), F32)],
        compiler_params=pltpu.CompilerParams(dimension_semantics=("parallel",)),
    )(f, bf)


def _fox_cum_bwd(dcr, dcc, f, bf, nb, seq, *, name):
    def body(dcr_ref, dcc_ref, f_ref, bf_ref, df_ref, dbf_ref):
        row, col = _iotas()
        upper_incl = (col >= row).astype(BF16)

        @pl.when(pl.program_id(0) == 0)
        def _():
            dbf_ref[...] = jnp.zeros((1, BLK), F32)

        carry = jnp.zeros((1, BLK), F32)
        for blk in reversed(range(seq // BLK)):
            rs = slice(blk * BLK, (blk + 1) * BLK)
            dc = dcr_ref[:, rs].T + dcc_ref[rs, :]
            dlf = _dot3_left(upper_incl, dc) + carry
            carry = carry + _colsum(dc)
            fl = f_ref[rs, :] + bf_ref[...]
            df = jnp.where(col < HEADS, dlf * _sigmoid(-fl), 0.0)
            df_ref[rs, :] = df
            dbf_ref[...] += _colsum(df)

    return pl.pallas_call(
        body, name=name, grid=(nb,),
        in_specs=[pl.BlockSpec((None, BLK, seq), lambda b: (b, 0, 0)), pl.BlockSpec((seq, BLK), lambda b: (b, 0)),
                  pl.BlockSpec((seq, BLK), lambda b: (b, 0)), pl.BlockSpec((1, BLK), lambda b: (0, 0))],
        out_specs=[pl.BlockSpec((seq, BLK), lambda b: (b, 0)), pl.BlockSpec((1, BLK), lambda b: (0, 0))],
        out_shape=[jax.ShapeDtypeStruct((nb * seq, BLK), F32), jax.ShapeDtypeStruct((1, BLK), F32)],
        compiler_params=pltpu.CompilerParams(dimension_semantics=("arbitrary",)),
    )(dcr, dcc, f, bf)


def _fox_cum_cols(cc_ref, t0, tq, colq, hp):
    cc = cc_ref[pl.ds(t0, tq), :]
    c0 = jnp.sum(jnp.where(colq == 2 * hp, cc, 0.0), axis=1, keepdims=True)
    c1 = jnp.sum(jnp.where(colq == 2 * hp + 1, cc, 0.0), axis=1, keepdims=True)
    return c0, c1


def _fox_fwd(p, cc, cr, nb, seq, *, name):
    tq = min(Q_BLOCK, seq)
    nq = seq // tq
    kw = min(FOX_KEY_BLOCK, seq)
    ksub = kw // BLK

    def body(q_ref, k_ref, v_ref, cc_ref, cr_ref, o_ref, lse_ref):
        hp = pl.program_id(1)
        row, col = _iotas()
        colq = lax.broadcasted_iota(jnp.int32, (tq, BLK), 1)
        head0 = colq < HEAD_DIM

        def qblock(qb, c):
            t0 = pl.multiple_of(qb * tq, tq)
            qs = _stack_heads(q_ref[pl.ds(t0, tq), :], head0, _SCALE)
            c0, c1 = _fox_cum_cols(cc_ref, t0, tq, colq, hp)
            diag = (t0 + tq - 1) // kw

            def kblock(kb, carry, masked):
                accs, ms = carry
                s0 = pl.multiple_of(kb * kw, kw)
                k = k_ref[pl.ds(s0, kw), :].astype(BF16)
                vf = v_ref[pl.ds(s0, kw), :]
                own0 = lax.broadcasted_iota(jnp.int32, vf.shape, 1) < HEAD_DIM
                vs = (jnp.where(own0, vf, 1.0).astype(BF16), jnp.where(own0, 1.0, vf).astype(BF16))
                mask = _causal_mask(t0, s0, tq, kw, True)[:tq] if masked else None
                zs = [_dot(qs[h * tq:(h + 1) * tq], k, _NT) for h in range(2)]
                parts = []
                for h, ch in enumerate((c0, c1)):
                    s = zs[h] + (ch - cr_ref[h:h + 1, pl.ds(s0, kw)])
                    if masked:
                        s = jnp.where(mask, s, -jnp.inf)
                    m_new = jnp.maximum(ms[h], jnp.max(s, axis=1, keepdims=True))
                    parts.append((jnp.exp(s - m_new).astype(BF16), jnp.exp(ms[h] - m_new), m_new))
                return (tuple(accs[h] * parts[h][1] + _dot(parts[h][0], vs[h]) for h in range(2)),
                        tuple(parts[h][2] for h in range(2)))

            zeros, ninf = jnp.zeros((tq, BLK), F32), jnp.full((tq, 1), -jnp.inf, F32)
            carry = lax.fori_loop(0, diag, lambda kb, cr: kblock(kb, cr, False), ((zeros, zeros), (ninf, ninf)))
            (acc0, acc1), (m0, m1) = kblock(diag, carry, True)
            l = jnp.where(head0, pltpu.roll(acc0, HEAD_DIM, 1), pltpu.roll(acc1, HEAD_DIM, 1))
            o_ref[pl.ds(t0, tq), :] = jnp.where(head0, acc0, acc1) / l
            lse_ref[pl.ds(t0, tq), :] = jnp.where(head0, m0, m1) + jnp.log(l)
            return c

        lax.fori_loop(0, nq, qblock, 0)

    return pl.pallas_call(
        body, name=name, grid=(nb, PAIRS),
        in_specs=[_pair_spec(seq, 0), _pair_spec(seq, PAIRS), _pair_spec(seq, 2 * PAIRS),
                  pl.BlockSpec((seq, BLK), lambda b, hp: (b, 0)), pl.BlockSpec((None, None, SUBLANES, seq), lambda b, hp: (b, hp, 0, 0))],
        out_specs=[_pair_spec(seq, 0), _stat_spec(seq)],
        out_shape=[jax.ShapeDtypeStruct((nb * seq, D_MODEL), F32), jax.ShapeDtypeStruct((nb, PAIRS, seq, BLK), F32)],
        **_ATT_PARAMS,
    )(p, p, p, cc, cr)


def _fox_bwd(p, do, o, lse, cc, cr, nb, seq, *, name):
    tq = min(Q_BLOCK, seq)
    nq = seq // tq
    kw = min(FOX_KEY_BLOCK, seq)
    ksub = kw // BLK

    def body(q_ref, k_ref, v_ref, do_ref, o_ref, lse_ref, cc_ref, cr_ref, dq_ref, dk_ref, dv_ref, dcr_ref, dcc_ref):
        hp = pl.program_id(1)
        row, col = _iotas()
        colq = lax.broadcasted_iota(jnp.int32, (tq, BLK), 1)
        head0 = colq < HEAD_DIM
        dk_ref[...] = jnp.zeros(dk_ref.shape, F32)
        dv_ref[...] = jnp.zeros(dv_ref.shape, F32)
        dcr_ref[...] = jnp.zeros(dcr_ref.shape, F32)

        @pl.when(hp == 0)
        def _():
            dcc_ref[...] = jnp.zeros(dcc_ref.shape, F32)

        def qblock(qb, c):
            t0 = pl.multiple_of(qb * tq, tq)
            qs = _stack_heads(q_ref[pl.ds(t0, tq), :], head0, _SCALE)
            dof = do_ref[pl.ds(t0, tq), :]
            dos = _stack_heads(dof, head0)
            prod = dof * o_ref[pl.ds(t0, tq), :]
            dl = jnp.concatenate([jnp.sum(jnp.where(head0, prod, 0.0), axis=1, keepdims=True),
                                  jnp.sum(jnp.where(head0, 0.0, prod), axis=1, keepdims=True)], axis=0)
            lse = _pair_cols(lse_ref[pl.ds(t0, tq), :], head0, -jnp.inf)
            c0, c1 = _fox_cum_cols(cc_ref, t0, tq, colq, hp)
            diag = (t0 + tq - 1) // kw

            def kblock(kb, carry, masked):
                dq, rs = carry
                s0 = pl.multiple_of(kb * kw, kw)
                kf = k_ref[pl.ds(s0, kw), :]
                k = kf.astype(BF16)
                k0, k1 = _heads(kf, head0)
                v = v_ref[pl.ds(s0, kw), :].astype(BF16)
                mask = _causal_mask(t0, s0, tq, kw, True)[:tq] if masked else None
                zs = [_dot(qs[h * tq:(h + 1) * tq], k, _NT) for h in range(2)]
                dps = [_dot(dos[h * tq:(h + 1) * tq], v, _NT) for h in range(2)]
                prs, dss, rss = [], [], []
                for h, (ch, kh) in enumerate(((c0, k0), (c1, k1))):
                    rows = slice(h * tq, (h + 1) * tq)
                    pr = jnp.exp(zs[h] + (ch - cr_ref[h:h + 1, pl.ds(s0, kw)]) - lse[rows])
                    if masked:
                        pr = jnp.where(mask, pr, 0.0)
                    ds = pr * (dps[h] - dl[rows])
                    dcr_ref[h:h + 1, pl.ds(s0, kw)] -= _colsum(ds)
                    rss.append(rs[rows] + jnp.sum(ds, axis=1, keepdims=True))
                    prs.append(pr.astype(BF16))
                    dss.append(ds.astype(BF16))
                    dq = dq + _dot(dss[h], kh)
                dk_ref[pl.ds(s0, kw), :] += _dot(jnp.concatenate(dss, axis=0), qs, _TN)
                dv_ref[pl.ds(s0, kw), :] += _dot(jnp.concatenate(prs, axis=0), dos, _TN)
                return dq, jnp.concatenate(rss, axis=0)

            init = (jnp.zeros((tq, BLK), F32), jnp.zeros((2 * tq, 1), F32))
            carry = lax.fori_loop(0, diag, lambda kb, cr: kblock(kb, cr, False), init)
            dq, rs = kblock(diag, carry, True)
            dq_ref[pl.ds(t0, tq), :] = dq * _SCALE
            dcc_ref[pl.ds(t0, tq), :] += jnp.where(colq == 2 * hp, rs[:tq], 0.0) + jnp.where(colq == 2 * hp + 1, rs[tq:], 0.0)
            return c

        lax.fori_loop(0, nq, qblock, 0)

    t = nb * seq
    return pl.pallas_call(
        body, name=name, grid=(nb, PAIRS),
        in_specs=[_pair_spec(seq, 0), _pair_spec(seq, PAIRS), _pair_spec(seq, 2 * PAIRS), _pair_spec(seq, 0), _pair_spec(seq, 0),
                  _stat_spec(seq), pl.BlockSpec((seq, BLK), lambda b, hp: (b, 0)),
                  pl.BlockSpec((None, None, SUBLANES, seq), lambda b, hp: (b, hp, 0, 0))],
        out_specs=[_pair_spec(seq, 0)] * 3 + [pl.BlockSpec((None, None, SUBLANES, seq), lambda b, hp: (b, hp, 0, 0)),
                                              pl.BlockSpec((seq, BLK), lambda b, hp: (b, 0))],
        out_shape=[jax.ShapeDtypeStruct((t, D_MODEL), F32)] * 3 + [jax.ShapeDtypeStruct((nb, PAIRS, SUBLANES, seq), F32),
                                                                     jax.ShapeDtypeStruct((t, BLK), F32)],
        compiler_params=pltpu.CompilerParams(dimension_semantics=("parallel", "arbitrary")),
    )(p, p, p, do, o, lse, cc, cr)


def _row_shards(x):
    return x.reshape(N_CHIPS, x.shape[0] // N_CHIPS, x.shape[1])


def _local_step(x3, tgt3, w, later=None, start_reduce=None):
    nb, seq, d = x3.shape
    t = nb * seq
    x0, tgt = x3.reshape(t, d), tgt3.reshape(t, d)
    g = {}

    a_gain = w["a_norm"].reshape(1, d)
    h_a, ht_a = _rmsnorm(x0, a_gain, name="a_norm_fwd")
    p_a = _matmul(h_a, w["a_w_in"][:3], name="a_in_fwd", out_dtype=BF16)
    gate_a = _matmul(h_a, w["a_w_in"][3], name="a_in_gate_fwd")
    o_a, tot_a = _sb_fwd(p_a, nb, seq, name="a_attn_fwd")
    y_a, yt_a = _gate(o_a, gate_a, 0, name="a_gate_fwd")
    x1 = _matmul(y_a, w["a_w_out"], name="a_out_fwd", residual=x0)

    if later:
        w = {**w, **later[0](x1)}
    b_gain = w["b_norm"].reshape(1, d)
    b_lg, b_lb = w["b_v_ln_g"].reshape(1, GM_W), w["b_v_ln_b"].reshape(1, GM_W)
    b_ws, b_bst = w["b_w_s"].reshape(GM_G, BLK, BLK), w["b_b_s"].reshape(GM_G, BLK).T
    h_b, ht_b = _rmsnorm(x1, b_gain, name="b_norm_fwd")
    p_b = _matmul(h_b, w["b_w_in"], name="b_in_fwd")
    y_b, yt_b = _gmlp_fwd(p_b, b_lg, b_lb, b_ws, b_bst, name="b_mix_fwd")
    x2 = _matmul(y_b, w["b_w_out"], name="b_out_fwd", residual=x1)

    if later:
        w = {**w, **later[1](x2)}
    c_gain = w["c_norm"].reshape(1, d)
    c_cw = jnp.repeat(w["c_conv_w"].reshape(CV_K, CV_W), SUBLANES, axis=0)
    c_cb = w["c_conv_b"].reshape(1, CV_W)
    c_lg, c_lb = w["c_ln_g"].reshape(1, CV_W), w["c_ln_b"].reshape(1, CV_W)
    h_c, ht_c = _rmsnorm(x2, c_gain, name="c_norm_fwd")
    p_c = _matmul(h_c, w["c_w_in"], name="c_in_fwd")
    y_c, yt_c = _conv_fwd(p_c, c_cw, c_cb, c_lg, c_lb, seq, name="c_conv_fwd")
    x3_ = _matmul(y_c, w["c_w_out"], name="c_out_fwd", residual=x2)

    d_gain = w["d_norm"].reshape(1, d)
    d_win = w["d_w_in"].reshape(d, 4 * D_MODEL + HEADS)
    d_wmain = d_win[:, :4 * D_MODEL]
    d_wf = jnp.pad(d_win[:, 4 * D_MODEL:], ((0, 0), (0, BLK - HEADS)))
    d_bf = jnp.pad(w["d_b_f"].reshape(1, HEADS), ((0, 0), (0, BLK - HEADS)))
    h_d, ht_d = _rmsnorm(x3_, d_gain, name="d_norm_fwd")
    p_d = _matmul(h_d, d_wmain[:, :3 * D_MODEL], name="d_in_fwd", out_dtype=BF16)
    gate_d = _matmul(h_d, d_wmain[:, 3 * D_MODEL:], name="d_in_gate_fwd")
    f_d = _matmul(h_d, d_wf, name="d_inf_fwd")
    cc, cr = _fox_cum(f_d, d_bf, nb, seq, name="d_cum_fwd")
    cr = jnp.pad(cr.reshape(nb, PAIRS, 2, seq), ((0, 0), (0, 0), (0, SUBLANES - 2), (0, 0)))
    o_d, lse_d = _fox_fwd(p_d, cc, cr, nb, seq, name="d_attn_fwd")
    y_d, yt_d = _gate(o_d, gate_d, 0, name="d_gate_fwd")
    x4 = _matmul(y_d, w["d_w_out"], name="d_out_fwd", residual=x3_)

    f_gain = w["final_norm"].reshape(1, d)
    dx, g_fn, loss_row = _loss_head(x4, f_gain, tgt, name="loss_head")
    g["final_norm"] = g_fn

    g["d_w_out"] = _row_shards(_matmul(yt_d, dx, name="d_out_dw"))
    dy = _matmul(dx, w["d_w_out"], name="d_out_dy", mode="nt")
    do_d, dg_d = _gate_bwd(dy, o_d, gate_d, 0, name="d_gate_bwd")
    dq, dk, dv, dcr, dcc = _fox_bwd(p_d, do_d, o_d, lse_d, cc, cr, nb, seq, name="d_attn_bwd")
    dcr = jnp.pad(dcr[:, :, :2, :].reshape(nb, HEADS, seq), ((0, 0), (0, BLK - HEADS), (0, 0)))
    df, dbf = _fox_cum_bwd(dcr, dcc, f_d, d_bf, nb, seq, name="d_cum_bwd")
    g["d_b_f"] = dbf[:, :HEADS]
    parts = [dq, dk, dv, dg_d]
    dws = _matmul_parts(ht_d, parts, name="d_in_dw")
    dwf = _matmul(ht_d, df, name="d_inf_dw")
    g["d_w_in"] = jnp.concatenate([dws[n] for n in range(4)] + [dwf[:, :HEADS]], axis=1).reshape(d, N_CHIPS, -1).transpose(1, 0, 2)
    dh = _matmul_sum([(df, d_wf)] + [(pt, d_wmain[:, n * D_MODEL:(n + 1) * D_MODEL]) for n, pt in enumerate(parts)],
                     name="d_in_dh")
    dx, g["d_norm"] = _rmsnorm_bwd(dh, x3_, d_gain, dx, name="d_norm_bwd")

    g["c_w_out"] = _row_shards(_matmul(yt_c, dx, name="c_out_dw"))
    dy = _matmul(dx, w["c_w_out"], name="c_out_dy", mode="nt")
    dy1, dgate, g["c_ln_g"], g["c_ln_b"], g["c_conv_b"], g["c_conv_w"] = _conv_bwd_post(
        dy, p_c, c_cw, c_cb, c_lg, c_lb, seq, name="c_conv_bwd_post")
    dp = _conv_bwd_pre(dy1, dgate, p_c, c_cw, seq, name="c_conv_bwd_pre")
    g["c_w_in"] = _matmul(ht_c, dp, name="c_in_dw", out_shards=N_CHIPS)
    dh = _matmul(dp, w["c_w_in"], name="c_in_dh", mode="nt")
    dx, g["c_norm"] = _rmsnorm_bwd(dh, x2, c_gain, dx, name="c_norm_bwd")

    early, b_wout, a_wout = [], w["b_w_out"], w["a_w_out"]
    if start_reduce is not None:
        begun, token = start_reduce({n: g[n] for n in ("d_w_in", "d_w_out", "c_w_in", "c_w_out")}, "grads_cd")
        early.append(begun)
        b_wout = b_wout + token[0, 0].astype(b_wout.dtype)
    g["b_w_out"] = _row_shards(_matmul(yt_b, dx, name="b_out_dw"))
    dy = _matmul(dx, b_wout, name="b_out_dy", mode="nt")
    dp, g["b_v_ln_g"], g["b_v_ln_b"], g["b_w_s"], dbst = _gmlp_bwd(dy, p_b, b_lg, b_lb, b_ws, b_bst, name="b_mix_bwd")
    g["b_b_s"] = dbst.T
    g["b_w_in"] = _matmul(ht_b, dp, name="b_in_dw", out_shards=N_CHIPS)
    dh = _matmul(dp, w["b_w_in"], name="b_in_dh", mode="nt")
    dx, g["b_norm"] = _rmsnorm_bwd(dh, x1, b_gain, dx, name="b_norm_bwd")

    if start_reduce is not None:
        begun, token = start_reduce({n: g[n] for n in ("b_w_in", "b_w_out")}, "grads_b")
        early.append(begun)
        a_wout = a_wout + token[0, 0].astype(a_wout.dtype)
    g["a_w_out"] = _row_shards(_matmul(yt_a, dx, name="a_out_dw"))
    dy = _matmul(dx, a_wout, name="a_out_dy", mode="nt")
    do_a, dg_a = _gate_bwd(dy, o_a, gate_a, 0, name="a_gate_bwd")
    dq, dk, dv = _sb_bwd(p_a, do_a, tot_a, nb, seq, name="a_attn_bwd")
    parts = [dq, dk, dv, dg_a]
    g["a_w_in"] = _matmul_parts(ht_a, parts, name="a_in_dw")
    dh = _matmul_sum([(pt, w["a_w_in"][n]) for n, pt in enumerate(parts)], name="a_in_dh")
    dx, g["a_norm"] = _rmsnorm_bwd(dh, x0, a_gain, dx, name="a_norm_bwd")

    return loss_row[0, 0], dx.reshape(nb, seq, d), g, early


_HBM = pl.BlockSpec(memory_space=pltpu.HBM)


def _place():
    return lax.axis_index("x"), lax.axis_index("y"), lax.axis_index("c")


def _other_chips(x, y):
    return [(1 - x, y), (x, 1 - y), (1 - x, 1 - y)]


def _allgather_chips(ss, *, name):
    n_ops = len(ss)

    def body(*refs):
        s_refs, o_refs, (send_sems, recv_sems) = refs[:n_ops], refs[n_ops:2 * n_ops], refs[2 * n_ops:]
        x, y, c = _place()
        me = 2 * x + y
        chips = _other_chips(x, y)

        def copy(i, kk, src, dst, to):
            return pltpu.make_async_remote_copy(src_ref=src, dst_ref=dst, send_sem=send_sems.at[6 * i + kk],
                                                recv_sem=recv_sems.at[6 * i + kk], device_id=to, device_id_type=MESH)

        def half(i, j, hc):
            h = s_refs[i].shape[0] // 2
            return o_refs[i].at[j, pl.ds(hc * h, h), :]

        first = [copy(i, kk, s_refs[i].at[pl.ds(c * (s_refs[i].shape[0] // 2), s_refs[i].shape[0] // 2), :], half(i, me, c),
                      (cx, cy, c)) for kk, (cx, cy) in enumerate(chips) for i in range(n_ops)]
        for cp in first:
            cp.start()
        passed = []
        for kk, (cx, cy) in enumerate(chips):
            for i in range(n_ops):
                blk = half(i, 2 * cx + cy, c)
                copy(i, kk, blk, blk, (cx, cy, c)).wait_recv()
                fwd = copy(i, 3 + kk, blk, blk, (x, y, 1 - c))
                fwd.start()
                passed.append(fwd)
        for kk, (cx, cy) in enumerate(chips):
            for i in range(n_ops):
                blk = half(i, 2 * cx + cy, 1 - c)
                copy(i, 3 + kk, blk, blk, (x, y, 1 - c)).wait_recv()
        for cp in first + passed:
            cp.wait_send()

    for s in ss:
        assert s.shape[0] % 32 == 0, s.shape
    return pl.pallas_call(
        body, name=name, in_specs=[_HBM] * n_ops, out_specs=[_HBM] * n_ops,
        out_shape=[jax.ShapeDtypeStruct((N_CHIPS,) + s.shape, s.dtype) for s in ss],
        scratch_shapes=[pltpu.SemaphoreType.DMA((6 * n_ops,)), pltpu.SemaphoreType.DMA((6 * n_ops,))],
    )(*ss)


_SEM = pl.BlockSpec(memory_space=pltpu.SEMAPHORE)
_ANY = pl.BlockSpec(memory_space=pl.ANY)
_DATAFLOW = pltpu.SideEffectType.DATAFLOW_SIDE_EFFECTING


def _chip_copies(s_refs, land_refs, send_sems, recv_sems):
    x, y, c = _place()
    me = 2 * x + y
    cps = []
    for i, (s_ref, land_ref) in enumerate(zip(s_refs, land_refs)):
        h = s_ref.shape[0] // 2
        for kk, (cx, cy) in enumerate(_other_chips(x, y)):
            cps.append(pltpu.make_async_remote_copy(
                src_ref=s_ref.at[pl.ds(c * h, h), :], dst_ref=land_ref.at[me, pl.ds(c * h, h), :], send_sem=send_sems.at[3 * i + kk],
                recv_sem=recv_sems.at[3 * i + kk], device_id=(cx, cy, c), device_id_type=MESH))
    return cps


def _gather_start(ss, after, *, name):
    n = len(ss)
    lands = [lax.empty((N_CHIPS,) + s.shape, s.dtype) for s in ss]

    def body(*refs):
        s_refs, land_refs = refs[:n], refs[n:2 * n]
        send_sems, recv_sems = refs[2 * n + 1], refs[2 * n + 2]
        token = refs[-1]
        for cp in _chip_copies(s_refs, land_refs, send_sems, recv_sems):
            cp.start()
        token[...] = jnp.zeros(token.shape, token.dtype)

    hbm = [pltpu.HBM(a.shape, a.dtype) for a in list(ss) + lands]
    res = pl.pallas_call(
        body, name=name,
        out_shape=(pltpu.SemaphoreType.DMA((3 * n,)), pltpu.SemaphoreType.DMA((3 * n,)), *hbm, jax.ShapeDtypeStruct((8, BLK), F32)),
        in_specs=[_HBM] * (2 * n) + [_ANY],
        out_specs=(_SEM, _SEM, *([_HBM] * (2 * n)), pl.BlockSpec(memory_space=pltpu.VMEM)),
        input_output_aliases={i: 2 + i for i in range(2 * n)},
        compiler_params=pltpu.CompilerParams(has_side_effects=_DATAFLOW),
    )(*[pltpu.with_memory_space_constraint(a, pltpu.HBM) for a in list(ss) + lands], after)
    return res[:-1], res[-1]


def _gather_wait(started, after, *, name):
    send_sems, recv_sems = started[0], started[1]
    n = (len(started) - 2) // 2

    def body(*refs):
        s_refs, land_refs = refs[:n], refs[n:2 * n]
        for cp in _chip_copies(s_refs, land_refs, refs[2 * n], refs[2 * n + 1]):
            cp.wait_send()
            cp.wait_recv()

    res = pl.pallas_call(
        body, name=name, out_shape=tuple(pltpu.HBM(a.shape, a.dtype) for a in started[2:]),
        in_specs=[_HBM] * (2 * n) + [_SEM, _SEM, _ANY], out_specs=tuple([_HBM] * (2 * n)),
        input_output_aliases={i: i for i in range(2 * n)},
        compiler_params=pltpu.CompilerParams(has_side_effects=_DATAFLOW),
    )(*started[2:], send_sems, recv_sems, after)
    return list(res[n:])


def _sibling_exchange(lands, *, name):
    n = len(lands)

    def body(*refs):
        o_refs, (send_sems, recv_sems) = refs[n:2 * n], refs[2 * n:]
        x, y, c = _place()
        cps = []
        for i, o_ref in enumerate(o_refs):
            h = o_ref.shape[1] // 2
            for kk, (cx, cy) in enumerate(_other_chips(x, y)):
                def half(hc):
                    return o_ref.at[2 * cx + cy, pl.ds(hc * h, h), :]
                sent = pltpu.make_async_remote_copy(src_ref=half(c), dst_ref=half(c), send_sem=send_sems.at[3 * i + kk],
                                                    recv_sem=recv_sems.at[3 * i + kk], device_id=(x, y, 1 - c), device_id_type=MESH)
                awaited = pltpu.make_async_remote_copy(src_ref=half(1 - c), dst_ref=half(1 - c), send_sem=send_sems.at[3 * i + kk],
                                                       recv_sem=recv_sems.at[3 * i + kk], device_id=(x, y, 1 - c),
                                                       device_id_type=MESH)
                cps.append((sent, awaited))
        for sent, _ in cps:
            sent.start()
        for sent, awaited in cps:
            awaited.wait_recv()
            sent.wait_send()

    return pl.pallas_call(
        body, name=name, in_specs=[_HBM] * n, out_specs=[_HBM] * n,
        out_shape=[jax.ShapeDtypeStruct(a.shape, a.dtype) for a in lands], scratch_shapes=_dma_sems(3 * n),
        input_output_aliases={i: i for i in range(n)},
    )(*lands)


def _own_block(gathered, s):
    me = 2 * lax.axis_index("x") + lax.axis_index("y")
    return lax.dynamic_update_slice(gathered, s[None], (me,) + (0,) * s.ndim)


def _dma_sems(n):
    return [pltpu.SemaphoreType.DMA((n,)), pltpu.SemaphoreType.DMA((n,))]


def _swap_halves(gps, *, name):
    n_ops = len(gps)

    def body(*refs):
        g_refs, o_refs, (send_sems, recv_sems) = refs[:n_ops], refs[n_ops:2 * n_ops], refs[2 * n_ops:]
        x, y, c = _place()
        cps = []
        for i, (g_ref, o_ref) in enumerate(zip(g_refs, o_refs)):
            h = g_ref.shape[1] // 2
            cps.append(pltpu.make_async_remote_copy(
                src_ref=g_ref.at[:, pl.ds((1 - c) * h, h), :], dst_ref=o_ref, send_sem=send_sems.at[i], recv_sem=recv_sems.at[i],
                device_id=(x, y, 1 - c), device_id_type=MESH))
        for cp in cps:
            cp.start()
        for cp in cps:
            cp.wait()

    return pl.pallas_call(
        body, name=name, in_specs=[_HBM] * n_ops, out_specs=[_HBM] * n_ops,
        out_shape=[jax.ShapeDtypeStruct((g.shape[0], g.shape[1] // 2, g.shape[2]), g.dtype) for g in gps],
        scratch_shapes=_dma_sems(n_ops),
    )(*gps)


def _scatter_chips(hps, *, name):
    n_ops = len(hps)

    def body(*refs):
        h_refs, o_refs, (send_sems, recv_sems) = refs[:n_ops], refs[n_ops:2 * n_ops], refs[2 * n_ops:]
        x, y, c = _place()
        cps = [pltpu.make_async_remote_copy(src_ref=h_ref.at[2 * cx + cy], dst_ref=o_ref.at[kk], send_sem=send_sems.at[3 * i + kk],
                                            recv_sem=recv_sems.at[3 * i + kk], device_id=(cx, cy, c), device_id_type=MESH)
               for i, (h_ref, o_ref) in enumerate(zip(h_refs, o_refs)) for kk, (cx, cy) in enumerate(_other_chips(x, y))]
        for cp in cps:
            cp.start()
        for cp in cps:
            cp.wait()

    return pl.pallas_call(
        body, name=name, in_specs=[_HBM] * n_ops, out_specs=[_HBM] * n_ops,
        out_shape=[jax.ShapeDtypeStruct((3,) + hp.shape[1:], hp.dtype) for hp in hps],
        scratch_shapes=_dma_sems(3 * n_ops),
    )(*hps)


def _join_halves(fs, *, name):
    n_ops = len(fs)

    def body(*refs):
        f_refs, o_refs, (send_sems, recv_sems) = refs[:n_ops], refs[n_ops:2 * n_ops], refs[2 * n_ops:]
        x, y, c = _place()
        cps = [pltpu.make_async_remote_copy(src_ref=f_ref, dst_ref=o_ref, send_sem=send_sems.at[i], recv_sem=recv_sems.at[i],
                                            device_id=(x, y, 1 - c), device_id_type=MESH)
               for i, (f_ref, o_ref) in enumerate(zip(f_refs, o_refs))]
        for cp in cps:
            cp.start()
        for cp in cps:
            cp.wait()

    return pl.pallas_call(
        body, name=name, in_specs=[_HBM] * n_ops, out_specs=[_HBM] * n_ops,
        out_shape=[jax.ShapeDtypeStruct(f.shape, f.dtype) for f in fs], scratch_shapes=_dma_sems(n_ops),
    )(*fs)


def _stitch(mine, theirs):
    south = lax.axis_index("c") == 0
    return jnp.concatenate([jnp.where(south, mine, theirs), jnp.where(south, theirs, mine)], axis=0)


def _add_halves(gp, ra, wire_dtype, *, name, bm=256):
    n, r, c_ = gp.shape
    h = r // 2
    bm = _tile(h, bm)
    per = h // bm
    c = lax.axis_index("c").astype(jnp.int32).reshape(1)

    def body(c_ref, g_ref, ra_ref, o_ref, ow_ref):
        s = g_ref[...] + ra_ref[...]
        o_ref[...] = s
        ow_ref[...] = s.astype(wire_dtype)

    mine = pl.BlockSpec((None, bm, c_), lambda j, i, cr: (j, i, 0))
    return pl.pallas_call(
        body, name=name,
        grid_spec=pltpu.PrefetchScalarGridSpec(
            num_scalar_prefetch=1, grid=(n, per),
            in_specs=[pl.BlockSpec((None, bm, c_), lambda j, i, cr: (j, cr[0] * per + i, 0)), mine],
            out_specs=[mine, mine]),
        out_shape=[jax.ShapeDtypeStruct((n, h, c_), F32), jax.ShapeDtypeStruct((n, h, c_), wire_dtype)],
        compiler_params=pltpu.CompilerParams(dimension_semantics=("parallel", "parallel")),
    )(c, gp, ra)


def _add_chips(hp, rb, *, name, bm=256):
    n, h, c_ = hp.shape
    bm = _tile(h, bm)
    me = (2 * lax.axis_index("x") + lax.axis_index("y")).astype(jnp.int32).reshape(1)

    def body(me_ref, h_ref, rb_ref, o_ref):
        o_ref[...] = ((h_ref[...] + rb_ref[0].astype(F32)) + rb_ref[1].astype(F32)) + rb_ref[2].astype(F32)

    return pl.pallas_call(
        body, name=name,
        grid_spec=pltpu.PrefetchScalarGridSpec(
            num_scalar_prefetch=1, grid=(h // bm,),
            in_specs=[pl.BlockSpec((None, bm, c_), lambda i, mr: (mr[0], i, 0)),
                      pl.BlockSpec((3, bm, c_), lambda i, mr: (0, i, 0))],
            out_specs=pl.BlockSpec((bm, c_), lambda i, mr: (i, 0))),
        out_shape=jax.ShapeDtypeStruct((h, c_), F32),
        compiler_params=pltpu.CompilerParams(dimension_semantics=("parallel",)),
    )(me, hp, rb)


def _scatter_copies(h_refs, land_refs, send_sems, recv_sems):
    x, y, c = _place()
    return [pltpu.make_async_remote_copy(src_ref=h_ref.at[2 * cx + cy], dst_ref=land_ref.at[kk], send_sem=send_sems.at[3 * i + kk],
                                         recv_sem=recv_sems.at[3 * i + kk], device_id=(cx, cy, c), device_id_type=MESH)
            for i, (h_ref, land_ref) in enumerate(zip(h_refs, land_refs)) for kk, (cx, cy) in enumerate(_other_chips(x, y))]


def _scatter_start(hps, after, *, name):
    n = len(hps)
    lands = [lax.empty((3,) + hp.shape[1:], hp.dtype) for hp in hps]

    def body(*refs):
        for cp in _scatter_copies(refs[:n], refs[n:2 * n], refs[2 * n + 1], refs[2 * n + 2]):
            cp.start()
        refs[-1][...] = jnp.zeros(refs[-1].shape, refs[-1].dtype)

    hbm = [pltpu.HBM(a.shape, a.dtype) for a in list(hps) + lands]
    res = pl.pallas_call(
        body, name=name,
        out_shape=(pltpu.SemaphoreType.DMA((3 * n,)), pltpu.SemaphoreType.DMA((3 * n,)), *hbm, jax.ShapeDtypeStruct((8, BLK), F32)),
        in_specs=[_HBM] * (2 * n) + [_ANY],
        out_specs=(_SEM, _SEM, *([_HBM] * (2 * n)), pl.BlockSpec(memory_space=pltpu.VMEM)),
        input_output_aliases={i: 2 + i for i in range(2 * n)},
        compiler_params=pltpu.CompilerParams(has_side_effects=_DATAFLOW),
    )(*[pltpu.with_memory_space_constraint(a, pltpu.HBM) for a in list(hps) + lands], after)
    return res[:-1], res[-1]


def _scatter_wait(started, after, *, name):
    n = (len(started) - 2) // 2

    def body(*refs):
        for cp in _scatter_copies(refs[:n], refs[n:2 * n], refs[2 * n], refs[2 * n + 1]):
            cp.wait_send()
            cp.wait_recv()

    res = pl.pallas_call(
        body, name=name, out_shape=tuple(pltpu.HBM(a.shape, a.dtype) for a in started[2:]),
        in_specs=[_HBM] * (2 * n) + [_SEM, _SEM, _ANY], out_specs=tuple([_HBM] * (2 * n)),
        input_output_aliases={i: i for i in range(2 * n)},
        compiler_params=pltpu.CompilerParams(has_side_effects=_DATAFLOW),
    )(*started[2:], started[0], started[1], after)
    return list(res[n:])


def _reduce_to_chips(gps, wire_dtypes, *, tag):
    ras = _swap_halves(gps, name=f"{tag}_swap_halves")
    return [_add_halves(gp, ra, wd, name=f"{tag}_add_halves{i}") for i, (gp, ra, wd) in enumerate(zip(gps, ras, wire_dtypes))]


def _start_reduce(early, tag):
    names = list(early)
    hps = _reduce_to_chips([early[n] for n in names], [BF16] * len(names), tag=tag)
    started, token = _scatter_start([hw for _, hw in hps], hps[-1][1], name=f"{tag}_scatter_start")
    return (tag, names, [hf for hf, _ in hps], started), token


def _adamw_math(w_ref, g_ref, m_ref, v_ref, d_ref, nm_ref, nv_ref):
    c1 = 1.0 - ADAM_B1 ** ADAM_STEP
    c2 = 1.0 - ADAM_B2 ** ADAM_STEP
    g_ = g_ref[...]
    m_ = ADAM_B1 * m_ref[...] + (1.0 - ADAM_B1) * g_
    v_ = ADAM_B2 * v_ref[...] + (1.0 - ADAM_B2) * (g_ * g_)
    d_ref[...] = -ADAM_LR * ((m_ / c1) / (jnp.sqrt(v_ / c2) + ADAM_EPS) + ADAM_WD * w_ref[...])
    nm_ref[...] = m_
    nv_ref[...] = v_


def _adamw_many(groups, *, name):
    n = len(groups[0])
    flat = [a for grp in groups for a in grp]

    def body(*refs):
        ins, outs = refs[:4 * n], refs[4 * n:]
        for i in range(n):
            _adamw_math(ins[i], ins[n + i], ins[2 * n + i], ins[3 * n + i], outs[i], outs[n + i], outs[2 * n + i])

    vmem = pl.BlockSpec(memory_space=pltpu.VMEM)
    res = pl.pallas_call(
        body, name=name, in_specs=[vmem] * (4 * n), out_specs=[vmem] * (3 * n),
        out_shape=[jax.ShapeDtypeStruct(a.shape, F32) for _ in range(3) for a in groups[0]],
    )(*flat)
    return res[:n], res[n:2 * n], res[2 * n:]


def _adamw(w, g_mine, g_theirs, m, v, *, name):
    r, c_ = w.shape
    h = r // 2
    bm = _tile(h, 256)
    per = h // bm
    c = lax.axis_index("c").astype(jnp.int32).reshape(1)

    def body(c_ref, w_ref, f_ref, t_ref, m_ref, v_ref, g_ref, d_ref, nm_ref, nv_ref):
        first_half = pl.program_id(0) < per
        mine = jnp.where(jnp.where(first_half, c_ref[0] == 0, c_ref[0] == 1), 1.0, 0.0)
        g_ref[...] = mine * f_ref[...] + (1.0 - mine) * t_ref[...]
        _adamw_math(w_ref, g_ref, m_ref, v_ref, d_ref, nm_ref, nv_ref)

    full = pl.BlockSpec((bm, c_), lambda i, cr: (i, 0))
    half = pl.BlockSpec((bm, c_), lambda i, cr: (i % per, 0))
    return pl.pallas_call(
        body, name=name,
        grid_spec=pltpu.PrefetchScalarGridSpec(num_scalar_prefetch=1, grid=(r // bm,), in_specs=[full, half, half, full, full],
                                               out_specs=[full] * 4),
        out_shape=[jax.ShapeDtypeStruct((r, c_), F32)] * 4,
        compiler_params=pltpu.CompilerParams(dimension_semantics=("parallel",)),
    )(c, w, g_mine, g_theirs, m, v)


_WEIGHTS = ["a_norm", "a_w_in", "a_w_out", "b_norm", "b_w_in", "b_v_ln_g", "b_v_ln_b", "b_w_s", "b_b_s", "b_w_out",
            "c_norm", "c_w_in", "c_conv_w", "c_conv_b", "c_ln_g", "c_ln_b", "c_w_out", "d_norm", "d_w_in", "d_b_f",
            "d_w_out", "final_norm"]
_SHARD_AXIS = {"a_norm": None, "a_w_in": 2, "a_w_out": 1, "b_norm": 1, "b_w_in": 2, "b_v_ln_g": 1, "b_v_ln_b": 1, "b_w_s": None,
               "b_b_s": None, "b_w_out": 1, "c_norm": 1, "c_w_in": 2, "c_conv_w": 2, "c_conv_b": 1, "c_ln_g": 1, "c_ln_b": 1,
               "c_w_out": 1, "d_norm": 1, "d_w_in": 2, "d_b_f": None, "d_w_out": 1, "final_norm": None}
_BIG = ["a_w_in", "a_w_out", "b_w_in", "b_w_out", "c_w_in", "c_w_out", "d_w_in", "d_w_out"]
_GATHER_GROUPS = (("a_w_in", "a_w_out"), ("b_w_in", "b_w_out"), ("c_w_in", "c_w_out", "d_w_in", "d_w_out"))
_SMALL_SHARDED = [n for n in _WEIGHTS if _SHARD_AXIS[n] is not None and n not in _BIG]
_REPLICATED = [n for n in _WEIGHTS if _SHARD_AXIS[n] is None]
_ROW_ALIGN = 32


def _pack(pieces, dtype, align=_ROW_ALIGN):
    flat = jnp.concatenate([p.reshape(-1).astype(dtype) for p in pieces])
    unit = align * PACK_C
    total = -(-flat.shape[0] // unit) * unit
    return jnp.pad(flat, (0, total - flat.shape[0])).reshape(total // PACK_C, PACK_C)


def _unpack(flat, shapes):
    out, off = [], 0
    for s in shapes:
        n = math.prod(s)
        out.append(flat[off:off + n].reshape(s))
        off += n
    return out


def _full_shape(local_shape, axis):
    s = list(local_shape)
    if axis is not None:
        s[axis] *= N_CHIPS
    return tuple(s)


def _gather_weights(local):
    def whole(n, gt):
        if _SHARD_AXIS[n] == 1:
            return gt.reshape(-1, gt.shape[-1])
        if n == "d_w_in":
            return gt.transpose(1, 0, 2).reshape(gt.shape[1], -1)
        return gt

    full = {n: local[n][0] if n != "final_norm" else local[n] for n in _REPLICATED}
    first = list(_GATHER_GROUPS[0])
    mine = [local[n][0].astype(BF16) for n in first] + [_pack([local[n] for n in _SMALL_SHARDED], F32)]
    got = [_own_block(gt, s) for gt, s in zip(_allgather_chips(mine, name="gather_weights"), mine)]
    full.update({n: whole(n, gt) for n, gt in zip(first, got)})
    small = got[-1].reshape(N_CHIPS, -1)
    shards = [_unpack(small[j], [local[n].shape[1:] for n in _SMALL_SHARDED]) for j in range(N_CHIPS)]
    for i, n in enumerate(_SMALL_SHARDED):
        full[n] = jnp.concatenate([shards[j][i] for j in range(N_CHIPS)], axis=_SHARD_AXIS[n] - 1)

    def begin(k, after):
        shards_k = [local[n][0].astype(BF16) for n in _GATHER_GROUPS[k]]
        started, token = _gather_start(shards_k, after, name=f"gather{k}_start")
        return shards_k, started, token

    pending = [begin(1, got[0])]
    full["a_norm"] = full["a_norm"] + pending[0][2][0, 0]

    def finish(k):
        def weights(after):
            shards_k, started, _ = pending[k - 1]
            lands = _gather_wait(started, after, name=f"gather{k}_wait")
            token = None
            if k + 1 < len(_GATHER_GROUPS):
                pending.append(begin(k + 1, lands[0]))
                token = pending[k][2]
            lands = _sibling_exchange(lands, name=f"gather{k}_exchange")
            out = {n: whole(n, _own_block(gt, s)) for n, gt, s in zip(_GATHER_GROUPS[k], lands, shards_k)}
            if token is not None:
                gain = _GATHER_GROUPS[k][0][0] + "_norm"
                out[gain] = full[gain] + token[0, 0]
            return out
        return weights

    return full, [finish(k) for k in range(1, len(_GATHER_GROUPS))]


def _repl_piece_len(local):
    total = sum(math.prod(local[n].shape) for n in _REPLICATED)
    return -(-total // N_CHIPS)


def _reduce_grads(g, local, early):
    rep_flat = jnp.concatenate([g[n].reshape(-1) for n in _REPLICATED])
    piece = _repl_piece_len(local)
    rep_flat = jnp.pad(rep_flat, (0, N_CHIPS * piece - rep_flat.shape[0]))

    def shard(n, j):
        full = g[n].reshape(_full_shape(local[n].shape, _SHARD_AXIS[n]))
        width = local[n].shape[_SHARD_AXIS[n]]
        return lax.slice_in_dim(full, j * width, (j + 1) * width, axis=_SHARD_AXIS[n])

    small = jnp.stack([_pack([shard(n, j) for n in _SMALL_SHARDED] + [rep_flat[j * piece:(j + 1) * piece]], F32)
                       for j in range(N_CHIPS)])
    early_names = [n for _, names, _, _ in early for n in names]
    late = [n for n in _BIG if n not in early_names]
    hps = _reduce_to_chips([g[n] for n in late] + [small], [BF16] * len(late) + [F32], tag="grads")
    rbs = list(_scatter_chips([hw for _, hw in hps], name="grads_scatter_chips"))
    early_halves, early_rbs = [], []
    for tag, _, halves_k, started in early:
        early_halves += halves_k
        early_rbs += _scatter_wait(started, rbs[0], name=f"{tag}_scatter_wait")
    halves = early_halves + [hf for hf, _ in hps]
    fs = [_add_chips(hf, rb, name=f"grads_add_chips{i}") for i, (hf, rb) in enumerate(zip(halves, early_rbs + rbs))]
    theirs = _join_halves(fs, name="grads_join_halves")
    red = dict(zip(early_names + late, zip(fs, theirs)))
    out = _unpack(_stitch(fs[-1], theirs[-1]).reshape(-1), [local[n].shape for n in _SMALL_SHARDED] + [(piece,)])
    red.update(zip(_SMALL_SHARDED, out[:-1]))
    rep_mine = _pack([out[-1]], F32)
    rep = _own_block(_allgather_chips([rep_mine], name="gather_replicated_grads")[0], rep_mine)
    rep = rep.reshape(N_CHIPS, -1)[:, :piece].reshape(-1)
    for n, val in zip(_REPLICATED, _unpack(rep, [local[n].shape for n in _REPLICATED])):
        red[n] = val
    return red


def _update(local, grads, m, v):
    grads, delta, new_m, new_v = dict(grads), {}, {}, {}
    for n in _BIG:
        shp = local[n].shape
        two = (shp[-2], shp[-1])
        res = _adamw(local[n].reshape(two), *grads[n], m[n].reshape(two), v[n].reshape(two), name=f"adamw_{n}")
        grads[n], delta[n], new_m[n], new_v[n] = [r.reshape(shp) for r in res]
    small = [n for n in _WEIGHTS if n not in _BIG]
    two = {n: (math.prod(local[n].shape[:-1]), local[n].shape[-1]) for n in small}
    res = _adamw_many([[src[n].reshape(two[n]) for n in small] for src in (local, grads, m, v)], name="adamw_small")
    for dst, rs in zip((delta, new_m, new_v), res):
        for n, val in zip(small, rs):
            dst[n] = val.reshape(local[n].shape)
    return grads, delta, new_m, new_v


def kernel(x, a_norm, a_w_in, a_w_out, b_norm, b_w_in, b_v_ln_g, b_v_ln_b, b_w_s, b_b_s, b_w_out, c_norm, c_w_in, c_conv_w, c_conv_b, c_ln_g, c_ln_b, c_w_out, d_norm, d_w_in, d_b_f, d_w_out, final_norm, loss_target, m_a_norm, m_a_w_in, m_a_w_out, m_b_norm, m_b_w_in, m_b_v_ln_g, m_b_v_ln_b, m_b_w_s, m_b_b_s, m_b_w_out, m_c_norm, m_c_w_in, m_c_conv_w, m_c_conv_b, m_c_ln_g, m_c_ln_b, m_c_w_out, m_d_norm, m_d_w_in, m_d_b_f, m_d_w_out, m_final_norm, v_a_norm, v_a_w_in, v_a_w_out, v_b_norm, v_b_w_in, v_b_v_ln_g, v_b_v_ln_b, v_b_w_s, v_b_b_s, v_b_w_out, v_c_norm, v_c_w_in, v_c_conv_w, v_c_conv_b, v_c_ln_g, v_c_ln_b, v_c_w_out, v_d_norm, v_d_w_in, v_d_b_f, v_d_w_out, v_final_norm):
    local = dict(zip(_WEIGHTS, (a_norm, a_w_in, a_w_out, b_norm, b_w_in, b_v_ln_g, b_v_ln_b, b_w_s, b_b_s, b_w_out, c_norm, c_w_in,
                                c_conv_w, c_conv_b, c_ln_g, c_ln_b, c_w_out, d_norm, d_w_in, d_b_f, d_w_out, final_norm)))
    m = dict(zip(_WEIGHTS, (m_a_norm, m_a_w_in, m_a_w_out, m_b_norm, m_b_w_in, m_b_v_ln_g, m_b_v_ln_b, m_b_w_s, m_b_b_s, m_b_w_out,
                            m_c_norm, m_c_w_in, m_c_conv_w, m_c_conv_b, m_c_ln_g, m_c_ln_b, m_c_w_out, m_d_norm, m_d_w_in, m_d_b_f,
                            m_d_w_out, m_final_norm)))
    v = dict(zip(_WEIGHTS, (v_a_norm, v_a_w_in, v_a_w_out, v_b_norm, v_b_w_in, v_b_v_ln_g, v_b_v_ln_b, v_b_w_s, v_b_b_s, v_b_w_out,
                            v_c_norm, v_c_w_in, v_c_conv_w, v_c_conv_b, v_c_ln_g, v_c_ln_b, v_c_w_out, v_d_norm, v_d_w_in, v_d_b_f,
                            v_d_w_out, v_final_norm)))
    loss_part, grad_x, g, early = _local_step(x, loss_target, *_gather_weights(local), _start_reduce)
    loss = lax.psum(loss_part, ("x", "y", "c"))
    grads = _reduce_grads(g, local, early)
    grads, delta, new_m, new_v = _update(local, grads, m, v)
    return (loss, grad_x, *[grads[n] for n in _WEIGHTS], *[delta[n] for n in _WEIGHTS],
            *[new_m[n] for n in _WEIGHTS], *[new_v[n] for n in _WEIGHTS])
```

```python
import math

import jax
import jax.numpy as jnp
from jax import lax
from jax.experimental import pallas as pl
from jax.experimental.pallas import tpu as pltpu

F32, BF16 = jnp.float32, jnp.bfloat16
MESH = pl.DeviceIdType.MESH

D_MODEL = 1024
HEADS = 16
HEAD_DIM = 64
BLK = 128
PAIRS = HEADS // 2
GM_W = 2048
GM_G = 16
CV_W = 2048
CV_K = 31
HALO = 32
EPS = 1e-6
N_CHIPS = 4
PACK_C = 1024
ADAM_LR, ADAM_B1, ADAM_B2, ADAM_EPS, ADAM_WD, ADAM_STEP = 0.001, 0.9, 0.999, 1e-08, 0.01, 10

_NT = (((1,), (1,)), ((), ()))
_TN = (((0,), (0,)), ((), ()))
_NN = (((1,), (0,)), ((), ()))


def _dot(a, b, dims=_NN):
    return lax.dot_general(a, b, dims, preferred_element_type=F32)


def _split3(x):
    hi = x.astype(BF16)
    r = x - hi.astype(F32)
    mid = r.astype(BF16)
    lo = (r - mid.astype(F32)).astype(BF16)
    return hi, mid, lo


def _dot3_left(m, x):
    hi, mid, lo = _split3(x)
    return _dot(m, hi) + _dot(m, mid) + _dot(m, lo)


def _sigmoid(x):
    return 1.0 / (1.0 + jnp.exp(-x))


def _silu(x):
    return x * _sigmoid(x)


def _dsilu(x):
    s = _sigmoid(x)
    return s * (1.0 + x * (1.0 - s))


_GELU_C = math.sqrt(2.0 / math.pi)
_GELU_A = 0.044715


def _gelu(x):
    return 0.5 * x * (1.0 + jnp.tanh(_GELU_C * (x + _GELU_A * x * x * x)))


def _dgelu(x):
    t = jnp.tanh(_GELU_C * (x + _GELU_A * x * x * x))
    return 0.5 * (1.0 + t) + 0.5 * x * (1.0 - t * t) * _GELU_C * (1.0 + 3.0 * _GELU_A * x * x)


def _log_sigmoid(x):
    return jnp.minimum(x, 0.0) - jnp.log(1.0 + jnp.exp(-jnp.abs(x)))


def _rms_fwd(x, g):
    r = lax.rsqrt(jnp.mean(x * x, axis=-1, keepdims=True) + EPS)
    return x * r * g


def _rms_bwd(dy, x, g):
    r = lax.rsqrt(jnp.mean(x * x, axis=-1, keepdims=True) + EPS)
    xh = x * r
    dxh = dy * g
    dx = r * (dxh - xh * jnp.mean(dxh * xh, axis=-1, keepdims=True))
    return dx, dy * xh


def _ln_stats(x):
    mu = jnp.mean(x, axis=-1, keepdims=True)
    xc = x - mu
    r = lax.rsqrt(jnp.mean(xc * xc, axis=-1, keepdims=True) + EPS)
    return xc * r, r


def _ln_bwd(dy, xh, r, g):
    dxh = dy * g
    return r * (dxh - jnp.mean(dxh, axis=-1, keepdims=True) - xh * jnp.mean(dxh * xh, axis=-1, keepdims=True))


def _colsum(x):
    return jnp.sum(x, axis=0, keepdims=True)


def _tile(n, want):
    for t in range(min(n, want), 7, -1):
        if n % t == 0 and t % 8 == 0:
            return t
    return n


MM_TILE = 1024


def _matmul(a, b, *, name, mode="nn", residual=None, out_shards=1, out_dtype=F32):
    (m, k) = a.shape
    b_shards = b.shape[0] if b.ndim == 3 else 1
    if mode == "nn":
        n = b.shape[-1] * b_shards
        tn, tk = _tile(n // max(b_shards, out_shards), MM_TILE), _tile(k, MM_TILE)
    else:
        n = b.shape[-2]
        tn, tk = _tile(n // out_shards, MM_TILE), _tile(k // b_shards, MM_TILE)
    tm = _tile(m, MM_TILE)
    nk = k // tk
    a_spec = pl.BlockSpec((tm, tk), lambda i, j, kk: (i, kk))
    if mode == "nn":
        dims = _NN
        if b_shards == 1:
            b_spec = pl.BlockSpec((tk, tn), lambda i, j, kk: (kk, j))
        else:
            per_b = n // b_shards // tn
            b_spec = pl.BlockSpec((None, tk, tn), lambda i, j, kk: (j // per_b, kk, j % per_b))
    else:
        dims = _NT
        if b_shards == 1:
            b_spec = pl.BlockSpec((tn, tk), lambda i, j, kk: (j, kk))
        else:
            per_b = k // b_shards // tk
            b_spec = pl.BlockSpec((None, tn, tk), lambda i, j, kk: (kk // per_b, j, kk % per_b))
    if out_shards == 1:
        o_spec = pl.BlockSpec((tm, tn), lambda i, j, kk: (i, j))
        o_shape = (m, n)
    else:
        per_o = n // out_shards // tn
        o_spec = pl.BlockSpec((None, tm, tn), lambda i, j, kk: (j // per_o, i, j % per_o))
        o_shape = (out_shards, m, n // out_shards)
    has_res = residual is not None

    def body(a_ref, b_ref, *rest):
        o_ref = rest[-1]
        kk = pl.program_id(2)
        part = _dot(a_ref[...].astype(BF16), b_ref[...].astype(BF16), dims)
        if has_res:
            @pl.when(kk == 0)
            def _():
                o_ref[...] = part + rest[0][...]
        else:
            @pl.when(kk == 0)
            def _():
                o_ref[...] = part.astype(out_dtype)

        if nk > 1:
            @pl.when(kk > 0)
            def _():
                o_ref[...] += part

    assert out_dtype == F32 or nk == 1
    return pl.pallas_call(
        body, name=name, grid=(m // tm, n // tn, nk),
        in_specs=[a_spec, b_spec] + ([o_spec] if has_res else []),
        out_specs=o_spec, out_shape=jax.ShapeDtypeStruct(o_shape, out_dtype),
        compiler_params=pltpu.CompilerParams(dimension_semantics=("parallel", "parallel", "arbitrary")),
    )(a, b, *([residual] if has_res else []))


def _matmul_parts(at, parts, *, name):
    m, k = at.shape
    n = parts[0].shape[1]
    tm, tk = _tile(m, MM_TILE // 2), _tile(k, MM_TILE // 2)
    n_parts = len(parts)

    def body(a_ref, *rest):
        o_ref = rest[-1]
        kk = pl.program_id(1)
        a = a_ref[...].astype(BF16)
        for p in range(n_parts):
            part = _dot(a, rest[p][...].astype(BF16))

            @pl.when(kk == 0)
            def _():
                o_ref[p] = part

            @pl.when(kk > 0)
            def _():
                o_ref[p] += part

    return pl.pallas_call(
        body, name=name, grid=(m // tm, k // tk),
        in_specs=[pl.BlockSpec((tm, tk), lambda i, kk: (i, kk))] + [pl.BlockSpec((tk, n), lambda i, kk: (kk, 0))] * n_parts,
        out_specs=pl.BlockSpec((n_parts, tm, n), lambda i, kk: (0, i, 0)),
        out_shape=jax.ShapeDtypeStruct((n_parts, m, n), F32),
        compiler_params=pltpu.CompilerParams(dimension_semantics=("parallel", "arbitrary")),
    )(at, *parts)


def _matmul_sum(pairs, *, name):
    m, n = pairs[0][0].shape[0], pairs[0][1].shape[0]
    tm, tn = _tile(m, MM_TILE // 2), _tile(n, MM_TILE)
    n_pairs = len(pairs)

    def body(*refs):
        acc = None
        for p in range(n_pairs):
            part = _dot(refs[2 * p][...].astype(BF16), refs[2 * p + 1][...].astype(BF16), _NT)
            acc = part if acc is None else acc + part
        refs[-1][...] = acc

    in_specs = []
    for a, b in pairs:
        in_specs += [pl.BlockSpec((tm, a.shape[1]), lambda i, j: (i, 0)), pl.BlockSpec((tn, b.shape[1]), lambda i, j: (j, 0))]
    return pl.pallas_call(
        body, name=name, grid=(m // tm, n // tn), in_specs=in_specs, out_specs=pl.BlockSpec((tm, tn), lambda i, j: (i, j)),
        out_shape=jax.ShapeDtypeStruct((m, n), F32),
        compiler_params=pltpu.CompilerParams(dimension_semantics=("parallel", "parallel")),
    )(*[x for pair in pairs for x in pair])


def _rows(fn, *, name, steps, ins, outs, accs=(), scratch=()):
    ni, no, na = len(ins), len(outs), len(accs)

    def body(*refs):
        in_refs, out_refs = refs[:ni], refs[ni:ni + no]
        acc_refs, scr = refs[ni + no:ni + no + na], refs[ni + no + na:]
        i = pl.program_id(0)

        @pl.when(i == 0)
        def _():
            for r in acc_refs:
                r[...] = jnp.zeros(r.shape, r.dtype)

        fn(i, in_refs, out_refs, acc_refs, scr)

    def full(shape):
        nd = len(shape)
        return pl.BlockSpec(tuple(shape), lambda i: (0,) * nd)

    res = pl.pallas_call(
        body, name=name, grid=(steps,),
        in_specs=[pl.BlockSpec(bs, im) for _, bs, im in ins],
        out_specs=[pl.BlockSpec(bs, im) for _, _, bs, im in outs] + [full(s) for s, _ in accs],
        out_shape=[jax.ShapeDtypeStruct(s, d) for s, d, _, _ in outs] + [jax.ShapeDtypeStruct(s, d) for s, d in accs],
        scratch_shapes=list(scratch),
        compiler_params=pltpu.CompilerParams(dimension_semantics=("arbitrary",)),
    )(*[a for a, _, _ in ins])
    return res


def _rb(arr, bm, cb=0, width=None):
    w = arr.shape[1] if width is None else width
    return (arr, (bm, w), lambda i: (i, cb))


def _const(arr):
    nd = arr.ndim
    return (arr, tuple(arr.shape), lambda i: (0,) * nd)


def _ro(t, w, dtype, bm):
    return ((t, w), dtype, (bm, w), lambda i: (i, 0))


def _rot(t, w, dtype, bm):
    return ((w, t), dtype, (w, bm), lambda i: (0, i))


def _rmsnorm(x, g, *, name, bm=512):
    t, d = x.shape
    bm = _tile(t, bm)

    def fn(i, ins, outs, accs, scr):
        h = _rms_fwd(ins[0][...], ins[1][...])
        outs[0][...] = h.astype(BF16)
        outs[1][...] = h.T.astype(BF16)

    return _rows(fn, name=name, steps=t // bm, ins=[_rb(x, bm), _const(g)], outs=[_ro(t, d, BF16, bm), _rot(t, d, BF16, bm)])


def _rmsnorm_bwd(dh, x, g, dres, *, name, bm=512):
    t, d = x.shape
    bm = _tile(t, bm)

    def fn(i, ins, outs, accs, scr):
        dx, dgrow = _rms_bwd(ins[0][...], ins[1][...], ins[2][...])
        outs[0][...] = ins[3][...] + dx
        accs[0][...] += _colsum(dgrow)

    return _rows(fn, name=name, steps=t // bm, ins=[_rb(dh, bm), _rb(x, bm), _const(g), _rb(dres, bm)],
                 outs=[_ro(t, d, F32, bm)], accs=[((1, d), F32)])


def _gate(o, p, gcb, *, name, bm=512):
    t, w = o.shape
    bm = _tile(t, bm)

    def fn(i, ins, outs, accs, scr):
        y = ins[0][...] * _silu(ins[1][...])
        outs[0][...] = y.astype(BF16)
        outs[1][...] = y.T.astype(BF16)

    return _rows(fn, name=name, steps=t // bm, ins=[_rb(o, bm), _rb(p, bm, gcb, w)],
                 outs=[_ro(t, w, BF16, bm), _rot(t, w, BF16, bm)])


def _gate_bwd(dy, o, p, gcb, *, name, bm=512):
    t, w = o.shape
    bm = _tile(t, bm)

    def fn(i, ins, outs, accs, scr):
        dy_, o_, g_ = ins[0][...], ins[1][...], ins[2][...]
        outs[0][...] = dy_ * _silu(g_)
        outs[1][...] = dy_ * o_ * _dsilu(g_)

    return _rows(fn, name=name, steps=t // bm, ins=[_rb(dy, bm), _rb(o, bm), _rb(p, bm, gcb, w)],
                 outs=[_ro(t, w, F32, bm), _ro(t, w, F32, bm)])


def _loss_head(x, g, tgt, *, name, bm=512):
    t, d = x.shape
    bm = _tile(t, bm)

    def fn(i, ins, outs, accs, scr):
        x_, g_, tg = ins[0][...], ins[1][...], ins[2][...]
        err = _rms_fwd(x_, g_) - tg
        part = 0.5 * jnp.sum(jnp.sum(err * err, axis=-1, keepdims=True), axis=0, keepdims=True) / d
        dx, dgrow = _rms_bwd(err / d, x_, g_)
        outs[0][...] = dx
        accs[0][...] += _colsum(dgrow)
        accs[1][...] += jnp.broadcast_to(part, (1, BLK))

    return _rows(fn, name=name, steps=t // bm, ins=[_rb(x, bm), _const(g), _rb(tgt, bm)],
                 outs=[_ro(t, d, F32, bm)], accs=[((1, d), F32), ((1, BLK), F32)])


def _gmlp_mix_weights(ws_ref, g):
    row = lax.broadcasted_iota(jnp.int32, (BLK, BLK), 0)
    col = lax.broadcasted_iota(jnp.int32, (BLK, BLK), 1)
    tril = col <= row
    return jnp.where(tril, ws_ref[g], 0.0), tril


def _gmlp_fwd(p, ln_g, ln_b, w_s, bs_t, *, name):
    t = p.shape[0]

    def fn(i, ins, outs, accs, scr):
        p_ref, lg, lb, ws_ref, bst = ins
        vn = _ln_stats(_gelu(p_ref[:, GM_W:2 * GM_W]))[0] * lg[...] + lb[...]
        for g in range(GM_G):
            cs = slice(g * BLK, (g + 1) * BLK)
            wt, _ = _gmlp_mix_weights(ws_ref, g)
            s = _dot(wt.astype(BF16), vn[:, cs].astype(BF16)) + bst[:, g:g + 1]
            u = _gelu(p_ref[:, cs])
            gate = p_ref[:, 2 * GM_W + g * BLK:2 * GM_W + (g + 1) * BLK]
            y = u * s * _silu(gate)
            outs[0][:, cs] = y.astype(BF16)
            outs[1][cs, :] = y.T.astype(BF16)

    return _rows(fn, name=name, steps=t // BLK, ins=[_rb(p, BLK), _const(ln_g), _const(ln_b), _const(w_s), _const(bs_t)],
                 outs=[_ro(t, GM_W, BF16, BLK), _rot(t, GM_W, BF16, BLK)])


def _gmlp_bwd(dy, p, ln_g, ln_b, w_s, bs_t, *, name):
    t = p.shape[0]

    def fn(i, ins, outs, accs, scr):
        dy_ref, p_ref, lg, lb, ws_ref, bst = ins
        dp_ref = outs[0]
        dlg, dlb, dws, dbst = accs
        dvn_ref = scr[0]
        v_pre = p_ref[:, GM_W:2 * GM_W]
        xh, r = _ln_stats(_gelu(v_pre))
        vn = xh * lg[...] + lb[...]
        for g in range(GM_G):
            cs = slice(g * BLK, (g + 1) * BLK)
            gs = slice(2 * GM_W + g * BLK, 2 * GM_W + (g + 1) * BLK)
            wt, tril = _gmlp_mix_weights(ws_ref, g)
            vg = vn[:, cs].astype(BF16)
            s = _dot(wt.astype(BF16), vg) + bst[:, g:g + 1]
            u_pre, gate, dyg = p_ref[:, cs], p_ref[:, gs], dy_ref[:, cs]
            u = _gelu(u_pre)
            dos = dyg * _silu(gate)
            dp_ref[:, gs] = dyg * u * s * _dsilu(gate)
            dp_ref[:, cs] = dos * s * _dgelu(u_pre)
            ds = (dos * u).astype(BF16)
            dws[g] += jnp.where(tril, _dot(ds, vg, _NT), 0.0)
            dbst[:, g:g + 1] += jnp.sum(dos * u, axis=1, keepdims=True)
            dvn_ref[:, cs] = _dot(wt.astype(BF16), ds, _TN)
        dvn = dvn_ref[...]
        dlg[...] += _colsum(dvn * xh)
        dlb[...] += _colsum(dvn)
        dp_ref[:, GM_W:2 * GM_W] = _ln_bwd(dvn, xh, r, lg[...]) * _dgelu(v_pre)

    return _rows(fn, name=name, steps=t // BLK,
                 ins=[_rb(dy, BLK), _rb(p, BLK), _const(ln_g), _const(ln_b), _const(w_s), _const(bs_t)],
                 outs=[_ro(t, 3 * GM_W, F32, BLK)],
                 accs=[((1, GM_W), F32), ((1, GM_W), F32), ((GM_G, BLK, BLK), F32), ((BLK, GM_G), F32)],
                 scratch=[pltpu.VMEM((BLK, GM_W), F32)])


CV_BM = 128
CV_RC = 8
SUBLANES = 8
CV_FWD_OFFS = [HALO - (CV_K - 1) + k for k in range(CV_K)]
CV_BWD_OFFS = [CV_K - 1 - k for k in range(CV_K)]


def _conv_halo_prev(p, cb, bm):
    per = bm // HALO
    return (p, (HALO, CV_W), lambda i: (jnp.maximum(i * per - 1, 0), cb))


def _conv_scratch(bm):
    return [pltpu.VMEM((bm + HALO, CV_W), F32), pltpu.VMEM((SUBLANES - 1, bm + HALO - SUBLANES, CV_W), F32),
            pltpu.VMEM((bm, CV_W), F32)]


def _conv_shift_copies(ext_ref, sh_ref):
    rows = sh_ref.shape[1]
    for b in range(1, SUBLANES):
        sh_ref[b - 1] = ext_ref[pl.ds(b, rows), :]


def _conv_window(ext_ref, sh_ref, off, r0, rows):
    b = off % SUBLANES
    src = ext_ref if b == 0 else sh_ref.at[b - 1]
    return src[pl.ds(r0 + (off - b), rows), :]


def _conv_taps(ext_ref, sh_ref, cw_ref, y_ref, offs):
    bm = y_ref.shape[0]

    def chunk(ci, c):
        r0 = pl.multiple_of(ci * CV_RC, CV_RC)
        acc = jnp.zeros((CV_RC, CV_W), F32)
        for k in range(CV_K):
            acc = acc + cw_ref[pl.ds(k * SUBLANES, CV_RC), :] * _conv_window(ext_ref, sh_ref, offs[k], r0, CV_RC)
        y_ref[pl.ds(r0, CV_RC), :] = acc
        return c

    lax.fori_loop(0, bm // CV_RC, chunk, 0)


def _conv_dweights(dy1_ref, ext_ref, sh_ref, dcw_ref):
    bm = dy1_ref.shape[0]
    groups = 4
    for k in range(CV_K):
        def step(ci, acc, off=CV_FWD_OFFS[k]):
            prods = []
            for u in range(groups):
                r0 = pl.multiple_of((ci * groups + u) * CV_RC, CV_RC)
                prods.append(dy1_ref[pl.ds(r0, CV_RC), :] * _conv_window(ext_ref, sh_ref, off, r0, CV_RC))
            return acc + ((prods[0] + prods[1]) + (prods[2] + prods[3]))

        dcw_ref[k:k + 1, :] += _colsum(lax.fori_loop(0, bm // (CV_RC * groups), step, jnp.zeros((CV_RC, CV_W), F32)))


def _conv_fill(i, ext_ref, a_prev, b_prev, a, b, bm, seq):
    keep = jnp.where((i % (seq // bm)) == 0, 0.0, 1.0)
    ext_ref[pl.ds(0, HALO), :] = keep * (a_prev * _sigmoid(b_prev))
    ext_ref[pl.ds(HALO, bm), :] = a * _sigmoid(b)


def _conv_fwd(p, cw, cb, ln_g, ln_b, seq, *, name, bm=CV_BM):
    t = p.shape[0]

    def fn(i, ins, outs, accs, scr):
        a, b, gate, ap, bp = [r[...] for r in ins[:5]]
        cw_ref, cb_, lg, lb = ins[5], ins[6][...], ins[7][...], ins[8][...]
        ext, sh, y = scr
        _conv_fill(i, ext, ap, bp, a, b, bm, seq)
        _conv_shift_copies(ext, sh)
        _conv_taps(ext, sh, cw_ref, y, CV_FWD_OFFS)
        y2 = _ln_stats(y[...] + cb_)[0] * lg + lb
        out = _silu(y2) * _silu(gate)
        outs[0][...] = out.astype(BF16)
        outs[1][...] = out.T.astype(BF16)

    return _rows(fn, name=name, steps=t // bm,
                 ins=[_rb(p, bm, 0, CV_W), _rb(p, bm, 1, CV_W), _rb(p, bm, 2, CV_W),
                      _conv_halo_prev(p, 0, bm), _conv_halo_prev(p, 1, bm),
                      _const(cw), _const(cb), _const(ln_g), _const(ln_b)],
                 outs=[_ro(t, CV_W, BF16, bm), _rot(t, CV_W, BF16, bm)], scratch=_conv_scratch(bm))


def _conv_bwd_post(dy, p, cw, cb, ln_g, ln_b, seq, *, name, bm=CV_BM):
    t = p.shape[0]

    def fn(i, ins, outs, accs, scr):
        dy_, a, b, gate, ap, bp = [r[...] for r in ins[:6]]
        cw_ref, cb_, lg, lb = ins[6], ins[7][...], ins[8][...], ins[9][...]
        dlg, dlb, dcb, dcw = accs
        ext, sh, y = scr
        _conv_fill(i, ext, ap, bp, a, b, bm, seq)
        _conv_shift_copies(ext, sh)
        _conv_taps(ext, sh, cw_ref, y, CV_FWD_OFFS)
        xh, r = _ln_stats(y[...] + cb_)
        y2 = xh * lg + lb
        outs[1][...] = dy_ * _silu(y2) * _dsilu(gate)
        dy2 = dy_ * _silu(gate) * _dsilu(y2)
        dlg[...] += _colsum(dy2 * xh)
        dlb[...] += _colsum(dy2)
        dy1 = _ln_bwd(dy2, xh, r, lg)
        outs[0][...] = dy1
        dcb[...] += _colsum(dy1)
        _conv_dweights(outs[0], ext, sh, dcw)

    return _rows(fn, name=name, steps=t // bm,
                 ins=[_rb(dy, bm), _rb(p, bm, 0, CV_W), _rb(p, bm, 1, CV_W), _rb(p, bm, 2, CV_W),
                      _conv_halo_prev(p, 0, bm), _conv_halo_prev(p, 1, bm),
                      _const(cw), _const(cb), _const(ln_g), _const(ln_b)],
                 outs=[_ro(t, CV_W, F32, bm), _ro(t, CV_W, F32, bm)],
                 accs=[((1, CV_W), F32), ((1, CV_W), F32), ((1, CV_W), F32), ((CV_K, CV_W), F32)],
                 scratch=_conv_scratch(bm))


def _conv_bwd_pre(dy1, dgate, p, cw, seq, *, name, bm=CV_BM):
    t = p.shape[0]
    per = bm // HALO
    last_halo = t // HALO - 1

    def fn(i, ins, outs, accs, scr):
        d1, d1n, dg, a, b = [r[...] for r in ins[:5]]
        ext, sh, y = scr
        keep = jnp.where((i % (seq // bm)) == (seq // bm - 1), 0.0, 1.0)
        ext[pl.ds(0, bm), :] = d1
        ext[pl.ds(bm, HALO), :] = keep * d1n
        _conv_shift_copies(ext, sh)
        _conv_taps(ext, sh, ins[5], y, CV_BWD_OFFS)
        dy0 = y[...]
        sb = _sigmoid(b)
        outs[0][:, 0:CV_W] = dy0 * sb
        outs[0][:, CV_W:2 * CV_W] = dy0 * a * sb * (1.0 - sb)
        outs[0][:, 2 * CV_W:3 * CV_W] = dg

    return _rows(fn, name=name, steps=t // bm,
                 ins=[_rb(dy1, bm), (dy1, (HALO, CV_W), lambda i: (jnp.minimum((i + 1) * per, last_halo), 0)),
                      _rb(dgate, bm), _rb(p, bm, 0, CV_W), _rb(p, bm, 1, CV_W), _const(cw)],
                 outs=[_ro(t, 3 * CV_W, F32, bm)], scratch=_conv_scratch(bm))[0]


def _iotas():
    row = lax.broadcasted_iota(jnp.int32, (BLK, BLK), 0)
    col = lax.broadcasted_iota(jnp.int32, (BLK, BLK), 1)
    return row, col


def _heads(x, head0):
    if head0.shape != x.shape:
        head0 = lax.broadcasted_iota(jnp.int32, x.shape, 1) < HEAD_DIM
    return jnp.where(head0, x, 0.0).astype(BF16), jnp.where(head0, 0.0, x).astype(BF16)


def _pair_spec(seq, off):
    return pl.BlockSpec((seq, BLK), lambda b, hp: (b, off + hp))


def _stat_spec(seq):
    return pl.BlockSpec((None, None, seq, BLK), lambda b, hp: (b, hp, 0, 0))


_ATT_PARAMS = dict(compiler_params=pltpu.CompilerParams(dimension_semantics=("parallel", "parallel")))
_SCALE = 1.0 / math.sqrt(HEAD_DIM)


Q_BLOCK = 256
KEY_BLOCK = 256
FOX_BLOCK = 512


def _stack_heads(x, head0, scale=None):
    if scale is not None:
        x = x * scale
    return jnp.concatenate(_heads(x, head0), axis=0)


def _pair_cols(x, head0, fill):
    a = jnp.max(jnp.where(head0, x, fill), axis=1, keepdims=True)
    b = jnp.max(jnp.where(head0, fill, x), axis=1, keepdims=True)
    return jnp.concatenate([a, b], axis=0)


def _causal_mask(t0, s0, tq, kw, inclusive):
    row = lax.broadcasted_iota(jnp.int32, (2 * tq, kw), 0) & (tq - 1)
    col = lax.broadcasted_iota(jnp.int32, (2 * tq, kw), 1)
    return (s0 + col) <= (t0 + row) if inclusive else (s0 + col) < (t0 + row)


def _sub(x, j):
    return x[:, j * BLK:(j + 1) * BLK]


def _tri_blocks(kw, relation):
    r = lax.broadcasted_iota(jnp.int32, (kw, kw), 0)
    c = lax.broadcasted_iota(jnp.int32, (kw, kw), 1)
    return (((r // BLK) == (c // BLK)) & relation(r, c)).astype(BF16)


def _block_cumsum(x, tri, ksub):
    hi = x.astype(BF16)
    lo = (x - hi.astype(F32)).astype(BF16)
    cs = _dot(jnp.concatenate([hi, lo], axis=0), tri)
    n = x.shape[0]
    cs = cs[:n] + cs[n:]
    return [_sub(cs, j) for j in range(ksub)], [jnp.sum(_sub(x, j), axis=1, keepdims=True) for j in range(ksub)]


def _sb_terms_z(z, mask):
    t = jnp.log(1.0 + jnp.exp(-jnp.abs(z)))
    lsz = jnp.minimum(z, 0.0) - t
    lr = lsz - z
    if mask is not None:
        lr = jnp.where(mask, lr, 0.0)
    return lsz, lr


def _sb_fwd(p, nb, seq, *, name):
    tq = min(Q_BLOCK, seq)
    nq = seq // tq
    kw = min(KEY_BLOCK, seq)
    ksub = kw // BLK

    def body(q_ref, k_ref, v_ref, o_ref, tot_ref):
        row, col = _iotas()
        colq = lax.broadcasted_iota(jnp.int32, (tq, BLK), 1)
        head0 = colq < HEAD_DIM
        upper = _tri_blocks(kw, lambda j, s: j > s)

        def qblock(qb, c):
            t0 = pl.multiple_of(qb * tq, tq)
            qs = _stack_heads(q_ref[pl.ds(t0, tq), :], head0, _SCALE)
            diag = (t0 + tq - 1) // kw

            def kblock(kb, carry, masked):
                acc, run = carry
                s0 = pl.multiple_of(kb * kw, kw)
                k = k_ref[pl.ds(s0, kw), :].astype(BF16)
                v0, v1 = _heads(v_ref[pl.ds(s0, kw), :], head0)
                mask = _causal_mask(t0, s0, tq, kw, False)[:tq] if masked else None
                zs = [_dot(qs[h * tq:(h + 1) * tq], k, _NT) for h in range(2)]
                terms = []
                for h in range(2):
                    lsz, lr = _sb_terms_z(zs[h], mask)
                    terms.append((lsz,) + _block_cumsum(lr, upper, ksub))
                runs = []
                for h, vh in enumerate((v0, v1)):
                    lsz, after, total = terms[h]
                    r = run[h]
                    ws = [None] * ksub
                    for j in reversed(range(ksub)):
                        w = jnp.exp(_sub(lsz, j) + after[j] + r)
                        if masked:
                            w = jnp.where(_sub(mask, j), w, 0.0)
                        ws[j] = w.astype(BF16)
                        r = r + total[j]
                    acc = acc + _dot(jnp.concatenate(ws, axis=1), vh)
                    runs.append(r)
                return acc, tuple(runs)

            zc = jnp.zeros((tq, 1), F32)
            carry = kblock(diag, (jnp.zeros((tq, BLK), F32), (zc, zc)), True)
            acc, run = lax.fori_loop(0, diag, lambda it, cr: kblock(diag - 1 - it, cr, False), carry)
            o_ref[pl.ds(t0, tq), :] = acc
            tot_ref[pl.ds(t0, tq), :] = jnp.where(head0, run[0], run[1])
            return c

        lax.fori_loop(0, nq, qblock, 0)

    return pl.pallas_call(
        body, name=name, grid=(nb, PAIRS),
        in_specs=[_pair_spec(seq, 0), _pair_spec(seq, PAIRS), _pair_spec(seq, 2 * PAIRS)],
        out_specs=[_pair_spec(seq, 0), _stat_spec(seq)],
        out_shape=[jax.ShapeDtypeStruct((nb * seq, D_MODEL), F32), jax.ShapeDtypeStruct((nb, PAIRS, seq, BLK), F32)],
        **_ATT_PARAMS,
    )(p, p, p)


def _sb_bwd(p, do, tot, nb, seq, *, name):
    tq = min(Q_BLOCK, seq)
    nq = seq // tq
    kw = min(KEY_BLOCK, seq)
    ksub = kw // BLK

    def body(q_ref, k_ref, v_ref, do_ref, tot_ref, dq_ref, dk_ref, dv_ref):
        row, col = _iotas()
        colq = lax.broadcasted_iota(jnp.int32, (tq, BLK), 1)
        head0 = colq < HEAD_DIM
        lower_incl = _tri_blocks(kw, lambda j, s: j <= s)
        lower_strict = _tri_blocks(kw, lambda s, j: s < j)
        dk_ref[...] = jnp.zeros(dk_ref.shape, F32)
        dv_ref[...] = jnp.zeros(dv_ref.shape, F32)

        def qblock(qb, c):
            t0 = pl.multiple_of(qb * tq, tq)
            qs = _stack_heads(q_ref[pl.ds(t0, tq), :], head0, _SCALE)
            dos = _stack_heads(do_ref[pl.ds(t0, tq), :], head0)
            tot = tot_ref[pl.ds(t0, tq), :]
            swapped = pltpu.roll(tot, HEAD_DIM, 1)
            tts = (jnp.where(head0, tot, swapped), jnp.where(head0, swapped, tot))
            diag = (t0 + tq - 1) // kw

            def kblock(kb, carry, masked):
                dq, pfs, efs = carry
                s0 = pl.multiple_of(kb * kw, kw)
                kf = k_ref[pl.ds(s0, kw), :]
                k = kf.astype(BF16)
                khs = _heads(kf, head0)
                v = v_ref[pl.ds(s0, kw), :].astype(BF16)
                mask = _causal_mask(t0, s0, tq, kw, False)[:tq] if masked else None
                zs = [_dot(qs[h * tq:(h + 1) * tq], k, _NT) for h in range(2)]
                dws = [_dot(dos[h * tq:(h + 1) * tq], v, _NT) for h in range(2)]
                first = []
                for h in range(2):
                    lsz, lr = _sb_terms_z(zs[h], None)
                    lrm = jnp.where(mask, lr, 0.0) if masked else lr
                    first.append((lsz, lr) + _block_cumsum(lrm, lower_incl, ksub))
                second, pfs_out = [], []
                for h in range(2):
                    lsz, lr, incl, total = first[h]
                    pf = pfs[h]
                    ws, ews = [], []
                    for j in range(ksub):
                        w = jnp.exp(_sub(lsz, j) + (tts[h] - pf - incl[j]))
                        if masked:
                            w = jnp.where(_sub(mask, j), w, 0.0)
                        pf = pf + total[j]
                        ws.append(w.astype(BF16))
                        ews.append(_sub(dws[h], j) * w)
                    pfs_out.append(pf)
                    second.append((ws, ews) + _block_cumsum(jnp.concatenate(ews, axis=1), lower_strict, ksub))
                dz_h, efs_out = [], []
                for h in range(2):
                    lsz, lr = first[h][:2]
                    ws, ews, before, etotal = second[h]
                    ef = efs[h]
                    dzs = []
                    for j in range(ksub):
                        dz = ews[j] * jnp.exp(_sub(lr, j)) - (ef + before[j]) * jnp.exp(_sub(lsz, j))
                        ef = ef + etotal[j]
                        if masked:
                            dz = jnp.where(_sub(mask, j), dz, 0.0)
                        dzs.append(dz.astype(BF16))
                    efs_out.append(ef)
                    dz_h.append(jnp.concatenate(dzs, axis=1))
                    dq = dq + _dot(dz_h[h], khs[h])
                w = jnp.concatenate([jnp.concatenate(second[h][0], axis=1) for h in range(2)], axis=0)
                dk_ref[pl.ds(s0, kw), :] += _dot(jnp.concatenate(dz_h, axis=0), qs, _TN)
                dv_ref[pl.ds(s0, kw), :] += _dot(w, dos, _TN)
                return dq, tuple(pfs_out), tuple(efs_out)

            zc = jnp.zeros((tq, 1), F32)
            carry = lax.fori_loop(0, diag, lambda kb, cr: kblock(kb, cr, False), (jnp.zeros((tq, BLK), F32), (zc, zc), (zc, zc)))
            dq_ref[pl.ds(t0, tq), :] = kblock(diag, carry, True)[0] * _SCALE
            return c

        lax.fori_loop(0, nq, qblock, 0)

    t = nb * seq
    return pl.pallas_call(
        body, name=name, grid=(nb, PAIRS),
        in_specs=[_pair_spec(seq, 0), _pair_spec(seq, PAIRS), _pair_spec(seq, 2 * PAIRS), _pair_spec(seq, 0), _stat_spec(seq)],
        out_specs=[_pair_spec(seq, 0)] * 3,
        out_shape=[jax.ShapeDtypeStruct((t, D_MODEL), F32)] * 3,
        **_ATT_PARAMS,
    )(p, p, p, do, tot)


def _fox_cum(f, bf, nb, seq, *, name):
    def body(f_ref, bf_ref, cc_ref, cr_ref):
        row, col = _iotas()
        lower = (col <= row).astype(BF16)
        carry = jnp.zeros((1, BLK), F32)
        for blk in range(seq // BLK):
            rs = slice(blk * BLK, (blk + 1) * BLK)
            lf = jnp.where(col < HEADS, _log_sigmoid(f_ref[rs, :] + bf_ref[...]), 0.0)
            cc = _dot3_left(lower, lf) + carry
            cc_ref[rs, :] = cc
            cr_ref[:, rs] = cc.T[0:HEADS, :]
            carry = carry + _colsum(lf)

    return pl.pallas_call(
        body, name=name, grid=(nb,),
        in_specs=[pl.BlockSpec((seq, BLK), lambda b: (b, 0)), pl.BlockSpec((1, BLK), lambda b: (0, 0))],
        out_specs=[pl.BlockSpec((seq, BLK), lambda b: (b, 0)), pl.BlockSpec((None, HEADS, seq), lambda b: (b, 0, 0))],
        out_shape=[jax.ShapeDtypeStruct((nb * seq, BLK), F32), jax.ShapeDtypeStruct((nb, HEADS, seq), F32)],
        compiler_params=pltpu.CompilerParams(dimension_semantics=("parallel",)),
    )(f, bf)


def _fox_cum_bwd(dcr, dcc, f, bf, nb, seq, *, name):
    def body(dcr_ref, dcc_ref, f_ref, bf_ref, df_ref, dbf_ref):
        row, col = _iotas()
        upper_incl = (col >= row).astype(BF16)

        @pl.when(pl.program_id(0) == 0)
        def _():
            dbf_ref[...] = jnp.zeros((1, BLK), F32)

        carry = jnp.zeros((1, BLK), F32)
        for blk in reversed(range(seq // BLK)):
            rs = slice(blk * BLK, (blk + 1) * BLK)
            dc = dcr_ref[:, rs].T + dcc_ref[rs, :]
            dlf = _dot3_left(upper_incl, dc) + carry
            carry = carry + _colsum(dc)
            fl = f_ref[rs, :] + bf_ref[...]
            df = jnp.where(col < HEADS, dlf * _sigmoid(-fl), 0.0)
            df_ref[rs, :] = df
            dbf_ref[...] += _colsum(df)

    return pl.pallas_call(
        body, name=name, grid=(nb,),
        in_specs=[pl.BlockSpec((None, BLK, seq), lambda b: (b, 0, 0)), pl.BlockSpec((seq, BLK), lambda b: (b, 0)),
                  pl.BlockSpec((seq, BLK), lambda b: (b, 0)), pl.BlockSpec((1, BLK), lambda b: (0, 0))],
        out_specs=[pl.BlockSpec((seq, BLK), lambda b: (b, 0)), pl.BlockSpec((1, BLK), lambda b: (0, 0))],
        out_shape=[jax.ShapeDtypeStruct((nb * seq, BLK), F32), jax.ShapeDtypeStruct((1, BLK), F32)],
        compiler_params=pltpu.CompilerParams(dimension_semantics=("arbitrary",)),
    )(dcr, dcc, f, bf)


def _fox_cum_cols(cc_ref, t0, tq, colq, hp):
    cc = cc_ref[pl.ds(t0, tq), :]
    c0 = jnp.sum(jnp.where(colq == 2 * hp, cc, 0.0), axis=1, keepdims=True)
    c1 = jnp.sum(jnp.where(colq == 2 * hp + 1, cc, 0.0), axis=1, keepdims=True)
    return c0, c1


def _fox_fwd(p, cc, cr, nb, seq, *, name):
    tq = min(FOX_BLOCK, seq)
    nq = seq // tq
    kw = min(FOX_BLOCK, seq)
    ksub = kw // BLK

    def body(q_ref, k_ref, v_ref, cc_ref, cr_ref, o_ref, lse_ref):
        hp = pl.program_id(1)
        row, col = _iotas()
        colq = lax.broadcasted_iota(jnp.int32, (tq, BLK), 1)
        head0 = colq < HEAD_DIM

        def qblock(qb, c):
            t0 = pl.multiple_of(qb * tq, tq)
            qs = _stack_heads(q_ref[pl.ds(t0, tq), :], head0, _SCALE)
            c0, c1 = _fox_cum_cols(cc_ref, t0, tq, colq, hp)
            diag = (t0 + tq - 1) // kw

            def kblock(kb, carry, masked):
                accs, ms = carry
                s0 = pl.multiple_of(kb * kw, kw)
                k = k_ref[pl.ds(s0, kw), :].astype(BF16)
                vf = v_ref[pl.ds(s0, kw), :]
                own0 = lax.broadcasted_iota(jnp.int32, vf.shape, 1) < HEAD_DIM
                vs = (jnp.where(own0, vf, 1.0).astype(BF16), jnp.where(own0, 1.0, vf).astype(BF16))
                mask = _causal_mask(t0, s0, tq, kw, True)[:tq] if masked else None
                zs = [_dot(qs[h * tq:(h + 1) * tq], k, _NT) for h in range(2)]
                parts = []
                for h, ch in enumerate((c0, c1)):
                    s = zs[h] + (ch - cr_ref[h:h + 1, pl.ds(s0, kw)])
                    if masked:
                        s = jnp.where(mask, s, -jnp.inf)
                    m_new = jnp.maximum(ms[h], jnp.max(s, axis=1, keepdims=True))
                    parts.append((jnp.exp(s - m_new).astype(BF16), jnp.exp(ms[h] - m_new), m_new))
                return (tuple(accs[h] * parts[h][1] + _dot(parts[h][0], vs[h]) for h in range(2)),
                        tuple(parts[h][2] for h in range(2)))

            zeros, ninf = jnp.zeros((tq, BLK), F32), jnp.full((tq, 1), -jnp.inf, F32)
            carry = lax.fori_loop(0, diag, lambda kb, cr: kblock(kb, cr, False), ((zeros, zeros), (ninf, ninf)))
            (acc0, acc1), (m0, m1) = kblock(diag, carry, True)
            l = jnp.where(head0, pltpu.roll(acc0, HEAD_DIM, 1), pltpu.roll(acc1, HEAD_DIM, 1))
            o_ref[pl.ds(t0, tq), :] = jnp.where(head0, acc0, acc1) / l
            lse_ref[pl.ds(t0, tq), :] = jnp.where(head0, m0, m1) + jnp.log(l)
            return c

        lax.fori_loop(0, nq, qblock, 0)

    return pl.pallas_call(
        body, name=name, grid=(nb, PAIRS),
        in_specs=[_pair_spec(seq, 0), _pair_spec(seq, PAIRS), _pair_spec(seq, 2 * PAIRS),
                  pl.BlockSpec((seq, BLK), lambda b, hp: (b, 0)), pl.BlockSpec((None, None, SUBLANES, seq), lambda b, hp: (b, hp, 0, 0))],
        out_specs=[_pair_spec(seq, 0), _stat_spec(seq)],
        out_shape=[jax.ShapeDtypeStruct((nb * seq, D_MODEL), F32), jax.ShapeDtypeStruct((nb, PAIRS, seq, BLK), F32)],
        **_ATT_PARAMS,
    )(p, p, p, cc, cr)


def _fox_bwd(p, do, o, lse, cc, cr, nb, seq, *, name):
    tq = min(FOX_BLOCK, seq)
    nq = seq // tq
    kw = min(FOX_BLOCK, seq)
    ksub = kw // BLK

    def body(q_ref, k_ref, v_ref, do_ref, o_ref, lse_ref, cc_ref, cr_ref, dq_ref, dk_ref, dv_ref, dcr_ref, dcc_ref):
        hp = pl.program_id(1)
        row, col = _iotas()
        colq = lax.broadcasted_iota(jnp.int32, (tq, BLK), 1)
        head0 = colq < HEAD_DIM
        dk_ref[...] = jnp.zeros(dk_ref.shape, F32)
        dv_ref[...] = jnp.zeros(dv_ref.shape, F32)
        dcr_ref[...] = jnp.zeros(dcr_ref.shape, F32)

        @pl.when(hp == 0)
        def _():
            dcc_ref[...] = jnp.zeros(dcc_ref.shape, F32)

        def qblock(qb, c):
            t0 = pl.multiple_of(qb * tq, tq)
            qs = _stack_heads(q_ref[pl.ds(t0, tq), :], head0, _SCALE)
            dof = do_ref[pl.ds(t0, tq), :]
            dos = _stack_heads(dof, head0)
            prod = dof * o_ref[pl.ds(t0, tq), :]
            dl = jnp.concatenate([jnp.sum(jnp.where(head0, prod, 0.0), axis=1, keepdims=True),
                                  jnp.sum(jnp.where(head0, 0.0, prod), axis=1, keepdims=True)], axis=0)
            lse = _pair_cols(lse_ref[pl.ds(t0, tq), :], head0, -jnp.inf)
            c0, c1 = _fox_cum_cols(cc_ref, t0, tq, colq, hp)
            diag = (t0 + tq - 1) // kw

            def kblock(kb, carry, masked):
                dq, rs = carry
                s0 = pl.multiple_of(kb * kw, kw)
                kf = k_ref[pl.ds(s0, kw), :]
                k = kf.astype(BF16)
                k0, k1 = _heads(kf, head0)
                v = v_ref[pl.ds(s0, kw), :].astype(BF16)
                mask = _causal_mask(t0, s0, tq, kw, True)[:tq] if masked else None
                zs = [_dot(qs[h * tq:(h + 1) * tq], k, _NT) for h in range(2)]
                dps = [_dot(dos[h * tq:(h + 1) * tq], v, _NT) for h in range(2)]
                prs, dss, rss = [], [], []
                for h, (ch, kh) in enumerate(((c0, k0), (c1, k1))):
                    rows = slice(h * tq, (h + 1) * tq)
                    pr = jnp.exp(zs[h] + (ch - cr_ref[h:h + 1, pl.ds(s0, kw)]) - lse[rows])
                    if masked:
                        pr = jnp.where(mask, pr, 0.0)
                    ds = pr * (dps[h] - dl[rows])
                    dcr_ref[h:h + 1, pl.ds(s0, kw)] -= _colsum(ds)
                    rss.append(rs[rows] + jnp.sum(ds, axis=1, keepdims=True))
                    prs.append(pr.astype(BF16))
                    dss.append(ds.astype(BF16))
                    dq = dq + _dot(dss[h], kh)
                dk_ref[pl.ds(s0, kw), :] += _dot(jnp.concatenate(dss, axis=0), qs, _TN)
                dv_ref[pl.ds(s0, kw), :] += _dot(jnp.concatenate(prs, axis=0), dos, _TN)
                return dq, jnp.concatenate(rss, axis=0)

            init = (jnp.zeros((tq, BLK), F32), jnp.zeros((2 * tq, 1), F32))
            carry = lax.fori_loop(0, diag, lambda kb, cr: kblock(kb, cr, False), init)
            dq, rs = kblock(diag, carry, True)
            dq_ref[pl.ds(t0, tq), :] = dq * _SCALE
            dcc_ref[pl.ds(t0, tq), :] += jnp.where(colq == 2 * hp, rs[:tq], 0.0) + jnp.where(colq == 2 * hp + 1, rs[tq:], 0.0)
            return c

        lax.fori_loop(0, nq, qblock, 0)

    t = nb * seq
    return pl.pallas_call(
        body, name=name, grid=(nb, PAIRS),
        in_specs=[_pair_spec(seq, 0), _pair_spec(seq, PAIRS), _pair_spec(seq, 2 * PAIRS), _pair_spec(seq, 0), _pair_spec(seq, 0),
                  _stat_spec(seq), pl.BlockSpec((seq, BLK), lambda b, hp: (b, 0)),
                  pl.BlockSpec((None, None, SUBLANES, seq), lambda b, hp: (b, hp, 0, 0))],
        out_specs=[_pair_spec(seq, 0)] * 3 + [pl.BlockSpec((None, None, SUBLANES, seq), lambda b, hp: (b, hp, 0, 0)),
                                              pl.BlockSpec((seq, BLK), lambda b, hp: (b, 0))],
        out_shape=[jax.ShapeDtypeStruct((t, D_MODEL), F32)] * 3 + [jax.ShapeDtypeStruct((nb, PAIRS, SUBLANES, seq), F32),
                                                                     jax.ShapeDtypeStruct((t, BLK), F32)],
        compiler_params=pltpu.CompilerParams(dimension_semantics=("parallel", "arbitrary")),
    )(p, p, p, do, o, lse, cc, cr)


def _row_shards(x):
    return x.reshape(N_CHIPS, x.shape[0] // N_CHIPS, x.shape[1])


def _local_step(x3, tgt3, w, later=None, start_reduce=None):
    nb, seq, d = x3.shape
    t = nb * seq
    x0, tgt = x3.reshape(t, d), tgt3.reshape(t, d)
    g = {}

    a_gain = w["a_norm"].reshape(1, d)
    h_a, ht_a = _rmsnorm(x0, a_gain, name="a_norm_fwd")
    p_a = _matmul(h_a, w["a_w_in"][:3], name="a_in_fwd", out_dtype=BF16)
    gate_a = _matmul(h_a, w["a_w_in"][3], name="a_in_gate_fwd")
    o_a, tot_a = _sb_fwd(p_a, nb, seq, name="a_attn_fwd")
    y_a, yt_a = _gate(o_a, gate_a, 0, name="a_gate_fwd")
    x1 = _matmul(y_a, w["a_w_out"], name="a_out_fwd", residual=x0)

    if later:
        w = {**w, **later[0](x1)}
    b_gain = w["b_norm"].reshape(1, d)
    b_lg, b_lb = w["b_v_ln_g"].reshape(1, GM_W), w["b_v_ln_b"].reshape(1, GM_W)
    b_ws, b_bst = w["b_w_s"].reshape(GM_G, BLK, BLK), w["b_b_s"].reshape(GM_G, BLK).T
    h_b, ht_b = _rmsnorm(x1, b_gain, name="b_norm_fwd")
    p_b = _matmul(h_b, w["b_w_in"], name="b_in_fwd")
    y_b, yt_b = _gmlp_fwd(p_b, b_lg, b_lb, b_ws, b_bst, name="b_mix_fwd")
    x2 = _matmul(y_b, w["b_w_out"], name="b_out_fwd", residual=x1)

    if later:
        w = {**w, **later[1](x2)}
    c_gain = w["c_norm"].reshape(1, d)
    c_cw = jnp.repeat(w["c_conv_w"].reshape(CV_K, CV_W), SUBLANES, axis=0)
    c_cb = w["c_conv_b"].reshape(1, CV_W)
    c_lg, c_lb = w["c_ln_g"].reshape(1, CV_W), w["c_ln_b"].reshape(1, CV_W)
    h_c, ht_c = _rmsnorm(x2, c_gain, name="c_norm_fwd")
    p_c = _matmul(h_c, w["c_w_in"], name="c_in_fwd")
    y_c, yt_c = _conv_fwd(p_c, c_cw, c_cb, c_lg, c_lb, seq, name="c_conv_fwd")
    x3_ = _matmul(y_c, w["c_w_out"], name="c_out_fwd", residual=x2)

    d_gain = w["d_norm"].reshape(1, d)
    d_win = w["d_w_in"].reshape(d, 4 * D_MODEL + HEADS)
    d_wmain = d_win[:, :4 * D_MODEL]
    d_wf = jnp.pad(d_win[:, 4 * D_MODEL:], ((0, 0), (0, BLK - HEADS)))
    d_bf = jnp.pad(w["d_b_f"].reshape(1, HEADS), ((0, 0), (0, BLK - HEADS)))
    h_d, ht_d = _rmsnorm(x3_, d_gain, name="d_norm_fwd")
    p_d = _matmul(h_d, d_wmain[:, :3 * D_MODEL], name="d_in_fwd", out_dtype=BF16)
    gate_d = _matmul(h_d, d_wmain[:, 3 * D_MODEL:], name="d_in_gate_fwd")
    f_d = _matmul(h_d, d_wf, name="d_inf_fwd")
    cc, cr = _fox_cum(f_d, d_bf, nb, seq, name="d_cum_fwd")
    cr = jnp.pad(cr.reshape(nb, PAIRS, 2, seq), ((0, 0), (0, 0), (0, SUBLANES - 2), (0, 0)))
    o_d, lse_d = _fox_fwd(p_d, cc, cr, nb, seq, name="d_attn_fwd")
    y_d, yt_d = _gate(o_d, gate_d, 0, name="d_gate_fwd")
    x4 = _matmul(y_d, w["d_w_out"], name="d_out_fwd", residual=x3_)

    f_gain = w["final_norm"].reshape(1, d)
    dx, g_fn, loss_row = _loss_head(x4, f_gain, tgt, name="loss_head")
    g["final_norm"] = g_fn

    g["d_w_out"] = _row_shards(_matmul(yt_d, dx, name="d_out_dw"))
    dy = _matmul(dx, w["d_w_out"], name="d_out_dy", mode="nt")
    do_d, dg_d = _gate_bwd(dy, o_d, gate_d, 0, name="d_gate_bwd")
    dq, dk, dv, dcr, dcc = _fox_bwd(p_d, do_d, o_d, lse_d, cc, cr, nb, seq, name="d_attn_bwd")
    dcr = jnp.pad(dcr[:, :, :2, :].reshape(nb, HEADS, seq), ((0, 0), (0, BLK - HEADS), (0, 0)))
    df, dbf = _fox_cum_bwd(dcr, dcc, f_d, d_bf, nb, seq, name="d_cum_bwd")
    g["d_b_f"] = dbf[:, :HEADS]
    parts = [dq, dk, dv, dg_d]
    dws = _matmul_parts(ht_d, parts, name="d_in_dw")
    dwf = _matmul(ht_d, df, name="d_inf_dw")
    g["d_w_in"] = jnp.concatenate([dws[n] for n in range(4)] + [dwf[:, :HEADS]], axis=1).reshape(d, N_CHIPS, -1).transpose(1, 0, 2)
    dh = _matmul_sum([(df, d_wf)] + [(pt, d_wmain[:, n * D_MODEL:(n + 1) * D_MODEL]) for n, pt in enumerate(parts)],
                     name="d_in_dh")
    dx, g["d_norm"] = _rmsnorm_bwd(dh, x3_, d_gain, dx, name="d_norm_bwd")

    g["c_w_out"] = _row_shards(_matmul(yt_c, dx, name="c_out_dw"))
    dy = _matmul(dx, w["c_w_out"], name="c_out_dy", mode="nt")
    dy1, dgate, g["c_ln_g"], g["c_ln_b"], g["c_conv_b"], g["c_conv_w"] = _conv_bwd_post(
        dy, p_c, c_cw, c_cb, c_lg, c_lb, seq, name="c_conv_bwd_post")
    dp = _conv_bwd_pre(dy1, dgate, p_c, c_cw, seq, name="c_conv_bwd_pre")
    g["c_w_in"] = _matmul(ht_c, dp, name="c_in_dw", out_shards=N_CHIPS)
    dh = _matmul(dp, w["c_w_in"], name="c_in_dh", mode="nt")
    dx, g["c_norm"] = _rmsnorm_bwd(dh, x2, c_gain, dx, name="c_norm_bwd")

    early, b_wout, a_wout = [], w["b_w_out"], w["a_w_out"]
    if start_reduce is not None:
        begun, token = start_reduce({n: g[n] for n in ("d_w_in", "d_w_out", "c_w_in", "c_w_out")}, "grads_cd")
        early.append(begun)
        b_wout = b_wout + token[0, 0].astype(b_wout.dtype)
    g["b_w_out"] = _row_shards(_matmul(yt_b, dx, name="b_out_dw"))
    dy = _matmul(dx, b_wout, name="b_out_dy", mode="nt")
    dp, g["b_v_ln_g"], g["b_v_ln_b"], g["b_w_s"], dbst = _gmlp_bwd(dy, p_b, b_lg, b_lb, b_ws, b_bst, name="b_mix_bwd")
    g["b_b_s"] = dbst.T
    g["b_w_in"] = _matmul(ht_b, dp, name="b_in_dw", out_shards=N_CHIPS)
    dh = _matmul(dp, w["b_w_in"], name="b_in_dh", mode="nt")
    dx, g["b_norm"] = _rmsnorm_bwd(dh, x1, b_gain, dx, name="b_norm_bwd")

    if start_reduce is not None:
        begun, token = start_reduce({n: g[n] for n in ("b_w_in", "b_w_out")}, "grads_b")
        early.append(begun)
        a_wout = a_wout + token[0, 0].astype(a_wout.dtype)
    g["a_w_out"] = _row_shards(_matmul(yt_a, dx, name="a_out_dw"))
    dy = _matmul(dx, a_wout, name="a_out_dy", mode="nt")
    do_a, dg_a = _gate_bwd(dy, o_a, gate_a, 0, name="a_gate_bwd")
    dq, dk, dv = _sb_bwd(p_a, do_a, tot_a, nb, seq, name="a_attn_bwd")
    parts = [dq, dk, dv, dg_a]
    g["a_w_in"] = _matmul_parts(ht_a, parts, name="a_in_dw")
    dh = _matmul_sum([(pt, w["a_w_in"][n]) for n, pt in enumerate(parts)], name="a_in_dh")
    dx, g["a_norm"] = _rmsnorm_bwd(dh, x0, a_gain, dx, name="a_norm_bwd")

    return loss_row[0, 0], dx.reshape(nb, seq, d), g, early


_HBM = pl.BlockSpec(memory_space=pltpu.HBM)


def _place():
    return lax.axis_index("x"), lax.axis_index("y"), lax.axis_index("c")


def _other_chips(x, y):
    return [(1 - x, y), (x, 1 - y), (1 - x, 1 - y)]


def _allgather_chips(ss, *, name):
    n_ops = len(ss)

    def body(*refs):
        s_refs, o_refs, (send_sems, recv_sems) = refs[:n_ops], refs[n_ops:2 * n_ops], refs[2 * n_ops:]
        x, y, c = _place()
        me = 2 * x + y
        chips = _other_chips(x, y)

        def copy(i, kk, src, dst, to):
            return pltpu.make_async_remote_copy(src_ref=src, dst_ref=dst, send_sem=send_sems.at[6 * i + kk],
                                                recv_sem=recv_sems.at[6 * i + kk], device_id=to, device_id_type=MESH)

        def half(i, j, hc):
            h = s_refs[i].shape[0] // 2
            return o_refs[i].at[j, pl.ds(hc * h, h), :]

        first = [copy(i, kk, s_refs[i].at[pl.ds(c * (s_refs[i].shape[0] // 2), s_refs[i].shape[0] // 2), :], half(i, me, c),
                      (cx, cy, c)) for kk, (cx, cy) in enumerate(chips) for i in range(n_ops)]
        for cp in first:
            cp.start()
        passed = []
        for kk, (cx, cy) in enumerate(chips):
            for i in range(n_ops):
                blk = half(i, 2 * cx + cy, c)
                copy(i, kk, blk, blk, (cx, cy, c)).wait_recv()
                fwd = copy(i, 3 + kk, blk, blk, (x, y, 1 - c))
                fwd.start()
                passed.append(fwd)
        for kk, (cx, cy) in enumerate(chips):
            for i in range(n_ops):
                blk = half(i, 2 * cx + cy, 1 - c)
                copy(i, 3 + kk, blk, blk, (x, y, 1 - c)).wait_recv()
        for cp in first + passed:
            cp.wait_send()

    for s in ss:
        assert s.shape[0] % 32 == 0, s.shape
    return pl.pallas_call(
        body, name=name, in_specs=[_HBM] * n_ops, out_specs=[_HBM] * n_ops,
        out_shape=[jax.ShapeDtypeStruct((N_CHIPS,) + s.shape, s.dtype) for s in ss],
        scratch_shapes=[pltpu.SemaphoreType.DMA((6 * n_ops,)), pltpu.SemaphoreType.DMA((6 * n_ops,))],
    )(*ss)


_SEM = pl.BlockSpec(memory_space=pltpu.SEMAPHORE)
_ANY = pl.BlockSpec(memory_space=pl.ANY)
_DATAFLOW = pltpu.SideEffectType.DATAFLOW_SIDE_EFFECTING


def _chip_copies(s_refs, land_refs, send_sems, recv_sems):
    x, y, c = _place()
    me = 2 * x + y
    cps = []
    for i, (s_ref, land_ref) in enumerate(zip(s_refs, land_refs)):
        h = s_ref.shape[0] // 2
        for kk, (cx, cy) in enumerate(_other_chips(x, y)):
            cps.append(pltpu.make_async_remote_copy(
                src_ref=s_ref.at[pl.ds(c * h, h), :], dst_ref=land_ref.at[me, pl.ds(c * h, h), :], send_sem=send_sems.at[3 * i + kk],
                recv_sem=recv_sems.at[3 * i + kk], device_id=(cx, cy, c), device_id_type=MESH))
    return cps


def _gather_start(ss, after, *, name):
    n = len(ss)
    lands = [lax.empty((N_CHIPS,) + s.shape, s.dtype) for s in ss]

    def body(*refs):
        s_refs, land_refs = refs[:n], refs[n:2 * n]
        send_sems, recv_sems = refs[2 * n + 1], refs[2 * n + 2]
        token = refs[-1]
        for cp in _chip_copies(s_refs, land_refs, send_sems, recv_sems):
            cp.start()
        token[...] = jnp.zeros(token.shape, token.dtype)

    hbm = [pltpu.HBM(a.shape, a.dtype) for a in list(ss) + lands]
    res = pl.pallas_call(
        body, name=name,
        out_shape=(pltpu.SemaphoreType.DMA((3 * n,)), pltpu.SemaphoreType.DMA((3 * n,)), *hbm, jax.ShapeDtypeStruct((8, BLK), F32)),
        in_specs=[_HBM] * (2 * n) + [_ANY],
        out_specs=(_SEM, _SEM, *([_HBM] * (2 * n)), pl.BlockSpec(memory_space=pltpu.VMEM)),
        input_output_aliases={i: 2 + i for i in range(2 * n)},
        compiler_params=pltpu.CompilerParams(has_side_effects=_DATAFLOW),
    )(*[pltpu.with_memory_space_constraint(a, pltpu.HBM) for a in list(ss) + lands], after)
    return res[:-1], res[-1]


def _gather_wait(started, after, *, name):
    send_sems, recv_sems = started[0], started[1]
    n = (len(started) - 2) // 2

    def body(*refs):
        s_refs, land_refs = refs[:n], refs[n:2 * n]
        for cp in _chip_copies(s_refs, land_refs, refs[2 * n], refs[2 * n + 1]):
            cp.wait_send()
            cp.wait_recv()

    res = pl.pallas_call(
        body, name=name, out_shape=tuple(pltpu.HBM(a.shape, a.dtype) for a in started[2:]),
        in_specs=[_HBM] * (2 * n) + [_SEM, _SEM, _ANY], out_specs=tuple([_HBM] * (2 * n)),
        input_output_aliases={i: i for i in range(2 * n)},
        compiler_params=pltpu.CompilerParams(has_side_effects=_DATAFLOW),
    )(*started[2:], send_sems, recv_sems, after)
    return list(res[n:])


def _sibling_exchange(lands, *, name):
    n = len(lands)

    def body(*refs):
        o_refs, (send_sems, recv_sems) = refs[n:2 * n], refs[2 * n:]
        x, y, c = _place()
        cps = []
        for i, o_ref in enumerate(o_refs):
            h = o_ref.shape[1] // 2
            for kk, (cx, cy) in enumerate(_other_chips(x, y)):
                def half(hc):
                    return o_ref.at[2 * cx + cy, pl.ds(hc * h, h), :]
                sent = pltpu.make_async_remote_copy(src_ref=half(c), dst_ref=half(c), send_sem=send_sems.at[3 * i + kk],
                                                    recv_sem=recv_sems.at[3 * i + kk], device_id=(x, y, 1 - c), device_id_type=MESH)
                awaited = pltpu.make_async_remote_copy(src_ref=half(1 - c), dst_ref=half(1 - c), send_sem=send_sems.at[3 * i + kk],
                                                       recv_sem=recv_sems.at[3 * i + kk], device_id=(x, y, 1 - c),
                                                       device_id_type=MESH)
                cps.append((sent, awaited))
        for sent, _ in cps:
            sent.start()
        for sent, awaited in cps:
            awaited.wait_recv()
            sent.wait_send()

    return pl.pallas_call(
        body, name=name, in_specs=[_HBM] * n, out_specs=[_HBM] * n,
        out_shape=[jax.ShapeDtypeStruct(a.shape, a.dtype) for a in lands], scratch_shapes=_dma_sems(3 * n),
        input_output_aliases={i: i for i in range(n)},
    )(*lands)


def _own_block(gathered, s):
    me = 2 * lax.axis_index("x") + lax.axis_index("y")
    return lax.dynamic_update_slice(gathered, s[None], (me,) + (0,) * s.ndim)


def _dma_sems(n):
    return [pltpu.SemaphoreType.DMA((n,)), pltpu.SemaphoreType.DMA((n,))]


def _swap_halves(gps, *, name):
    n_ops = len(gps)

    def body(*refs):
        g_refs, o_refs, (send_sems, recv_sems) = refs[:n_ops], refs[n_ops:2 * n_ops], refs[2 * n_ops:]
        x, y, c = _place()
        cps = []
        for i, (g_ref, o_ref) in enumerate(zip(g_refs, o_refs)):
            h = g_ref.shape[1] // 2
            cps.append(pltpu.make_async_remote_copy(
                src_ref=g_ref.at[:, pl.ds((1 - c) * h, h), :], dst_ref=o_ref, send_sem=send_sems.at[i], recv_sem=recv_sems.at[i],
                device_id=(x, y, 1 - c), device_id_type=MESH))
        for cp in cps:
            cp.start()
        for cp in cps:
            cp.wait()

    return pl.pallas_call(
        body, name=name, in_specs=[_HBM] * n_ops, out_specs=[_HBM] * n_ops,
        out_shape=[jax.ShapeDtypeStruct((g.shape[0], g.shape[1] // 2, g.shape[2]), g.dtype) for g in gps],
        scratch_shapes=_dma_sems(n_ops),
    )(*gps)


def _scatter_chips(hps, *, name):
    n_ops = len(hps)

    def body(*refs):
        h_refs, o_refs, (send_sems, recv_sems) = refs[:n_ops], refs[n_ops:2 * n_ops], refs[2 * n_ops:]
        x, y, c = _place()
        cps = [pltpu.make_async_remote_copy(src_ref=h_ref.at[2 * cx + cy], dst_ref=o_ref.at[kk], send_sem=send_sems.at[3 * i + kk],
                                            recv_sem=recv_sems.at[3 * i + kk], device_id=(cx, cy, c), device_id_type=MESH)
               for i, (h_ref, o_ref) in enumerate(zip(h_refs, o_refs)) for kk, (cx, cy) in enumerate(_other_chips(x, y))]
        for cp in cps:
            cp.start()
        for cp in cps:
            cp.wait()

    return pl.pallas_call(
        body, name=name, in_specs=[_HBM] * n_ops, out_specs=[_HBM] * n_ops,
        out_shape=[jax.ShapeDtypeStruct((3,) + hp.shape[1:], hp.dtype) for hp in hps],
        scratch_shapes=_dma_sems(3 * n_ops),
    )(*hps)


def _join_halves(fs, *, name):
    n_ops = len(fs)

    def body(*refs):
        f_refs, o_refs, (send_sems, recv_sems) = refs[:n_ops], refs[n_ops:2 * n_ops], refs[2 * n_ops:]
        x, y, c = _place()
        cps = [pltpu.make_async_remote_copy(src_ref=f_ref, dst_ref=o_ref, send_sem=send_sems.at[i], recv_sem=recv_sems.at[i],
                                            device_id=(x, y, 1 - c), device_id_type=MESH)
               for i, (f_ref, o_ref) in enumerate(zip(f_refs, o_refs))]
        for cp in cps:
            cp.start()
        for cp in cps:
            cp.wait()

    return pl.pallas_call(
        body, name=name, in_specs=[_HBM] * n_ops, out_specs=[_HBM] * n_ops,
        out_shape=[jax.ShapeDtypeStruct(f.shape, f.dtype) for f in fs], scratch_shapes=_dma_sems(n_ops),
    )(*fs)


def _stitch(mine, theirs):
    south = lax.axis_index("c") == 0
    return jnp.concatenate([jnp.where(south, mine, theirs), jnp.where(south, theirs, mine)], axis=0)


def _add_halves(gp, ra, wire_dtype, *, name, bm=256):
    n, r, c_ = gp.shape
    h = r // 2
    bm = _tile(h, bm)
    per = h // bm
    c = lax.axis_index("c").astype(jnp.int32).reshape(1)

    def body(c_ref, g_ref, ra_ref, o_ref, ow_ref):
        s = g_ref[...] + ra_ref[...]
        o_ref[...] = s
        ow_ref[...] = s.astype(wire_dtype)

    mine = pl.BlockSpec((None, bm, c_), lambda j, i, cr: (j, i, 0))
    return pl.pallas_call(
        body, name=name,
        grid_spec=pltpu.PrefetchScalarGridSpec(
            num_scalar_prefetch=1, grid=(n, per),
            in_specs=[pl.BlockSpec((None, bm, c_), lambda j, i, cr: (j, cr[0] * per + i, 0)), mine],
            out_specs=[mine, mine]),
        out_shape=[jax.ShapeDtypeStruct((n, h, c_), F32), jax.ShapeDtypeStruct((n, h, c_), wire_dtype)],
        compiler_params=pltpu.CompilerParams(dimension_semantics=("parallel", "parallel")),
    )(c, gp, ra)


def _add_chips(hp, rb, *, name, bm=256):
    n, h, c_ = hp.shape
    bm = _tile(h, bm)
    me = (2 * lax.axis_index("x") + lax.axis_index("y")).astype(jnp.int32).reshape(1)

    def body(me_ref, h_ref, rb_ref, o_ref):
        o_ref[...] = ((h_ref[...] + rb_ref[0].astype(F32)) + rb_ref[1].astype(F32)) + rb_ref[2].astype(F32)

    return pl.pallas_call(
        body, name=name,
        grid_spec=pltpu.PrefetchScalarGridSpec(
            num_scalar_prefetch=1, grid=(h // bm,),
            in_specs=[pl.BlockSpec((None, bm, c_), lambda i, mr: (mr[0], i, 0)),
                      pl.BlockSpec((3, bm, c_), lambda i, mr: (0, i, 0))],
            out_specs=pl.BlockSpec((bm, c_), lambda i, mr: (i, 0))),
        out_shape=jax.ShapeDtypeStruct((h, c_), F32),
        compiler_params=pltpu.CompilerParams(dimension_semantics=("parallel",)),
    )(me, hp, rb)


def _scatter_copies(h_refs, land_refs, send_sems, recv_sems):
    x, y, c = _place()
    return [pltpu.make_async_remote_copy(src_ref=h_ref.at[2 * cx + cy], dst_ref=land_ref.at[kk], send_sem=send_sems.at[3 * i + kk],
                                         recv_sem=recv_sems.at[3 * i + kk], device_id=(cx, cy, c), device_id_type=MESH)
            for i, (h_ref, land_ref) in enumerate(zip(h_refs, land_refs)) for kk, (cx, cy) in enumerate(_other_chips(x, y))]


def _scatter_start(hps, after, *, name):
    n = len(hps)
    lands = [lax.empty((3,) + hp.shape[1:], hp.dtype) for hp in hps]

    def body(*refs):
        for cp in _scatter_copies(refs[:n], refs[n:2 * n], refs[2 * n + 1], refs[2 * n + 2]):
            cp.start()
        refs[-1][...] = jnp.zeros(refs[-1].shape, refs[-1].dtype)

    hbm = [pltpu.HBM(a.shape, a.dtype) for a in list(hps) + lands]
    res = pl.pallas_call(
        body, name=name,
        out_shape=(pltpu.SemaphoreType.DMA((3 * n,)), pltpu.SemaphoreType.DMA((3 * n,)), *hbm, jax.ShapeDtypeStruct((8, BLK), F32)),
        in_specs=[_HBM] * (2 * n) + [_ANY],
        out_specs=(_SEM, _SEM, *([_HBM] * (2 * n)), pl.BlockSpec(memory_space=pltpu.VMEM)),
        input_output_aliases={i: 2 + i for i in range(2 * n)},
        compiler_params=pltpu.CompilerParams(has_side_effects=_DATAFLOW),
    )(*[pltpu.with_memory_space_constraint(a, pltpu.HBM) for a in list(hps) + lands], after)
    return res[:-1], res[-1]


def _scatter_wait(started, after, *, name):
    n = (len(started) - 2) // 2

    def body(*refs):
        for cp in _scatter_copies(refs[:n], refs[n:2 * n], refs[2 * n], refs[2 * n + 1]):
            cp.wait_send()
            cp.wait_recv()

    res = pl.pallas_call(
        body, name=name, out_shape=tuple(pltpu.HBM(a.shape, a.dtype) for a in started[2:]),
        in_specs=[_HBM] * (2 * n) + [_SEM, _SEM, _ANY], out_specs=tuple([_HBM] * (2 * n)),
        input_output_aliases={i: i for i in range(2 * n)},
        compiler_params=pltpu.CompilerParams(has_side_effects=_DATAFLOW),
    )(*started[2:], started[0], started[1], after)
    return list(res[n:])


def _reduce_to_chips(gps, wire_dtypes, *, tag):
    ras = _swap_halves(gps, name=f"{tag}_swap_halves")
    return [_add_halves(gp, ra, wd, name=f"{tag}_add_halves{i}") for i, (gp, ra, wd) in enumerate(zip(gps, ras, wire_dtypes))]


def _start_reduce(early, tag):
    names = list(early)
    hps = _reduce_to_chips([early[n] for n in names], [BF16] * len(names), tag=tag)
    started, token = _scatter_start([hw for _, hw in hps], hps[-1][1], name=f"{tag}_scatter_start")
    return (tag, names, [hf for hf, _ in hps], started), token


def _adamw_math(w_ref, g_ref, m_ref, v_ref, d_ref, nm_ref, nv_ref):
    c1 = 1.0 - ADAM_B1 ** ADAM_STEP
    c2 = 1.0 - ADAM_B2 ** ADAM_STEP
    g_ = g_ref[...]
    m_ = ADAM_B1 * m_ref[...] + (1.0 - ADAM_B1) * g_
    v_ = ADAM_B2 * v_ref[...] + (1.0 - ADAM_B2) * (g_ * g_)
    d_ref[...] = -ADAM_LR * ((m_ / c1) / (jnp.sqrt(v_ / c2) + ADAM_EPS) + ADAM_WD * w_ref[...])
    nm_ref[...] = m_
    nv_ref[...] = v_


def _adamw_many(groups, *, name):
    n = len(groups[0])
    flat = [a for grp in groups for a in grp]

    def body(*refs):
        ins, outs = refs[:4 * n], refs[4 * n:]
        for i in range(n):
            _adamw_math(ins[i], ins[n + i], ins[2 * n + i], ins[3 * n + i], outs[i], outs[n + i], outs[2 * n + i])

    vmem = pl.BlockSpec(memory_space=pltpu.VMEM)
    res = pl.pallas_call(
        body, name=name, in_specs=[vmem] * (4 * n), out_specs=[vmem] * (3 * n),
        out_shape=[jax.ShapeDtypeStruct(a.shape, F32) for _ in range(3) for a in groups[0]],
    )(*flat)
    return res[:n], res[n:2 * n], res[2 * n:]


def _adamw(w, g_mine, g_theirs, m, v, *, name):
    r, c_ = w.shape
    h = r // 2
    bm = _tile(h, 256)
    per = h // bm
    c = lax.axis_index("c").astype(jnp.int32).reshape(1)

    def body(c_ref, w_ref, f_ref, t_ref, m_ref, v_ref, g_ref, d_ref, nm_ref, nv_ref):
        first_half = pl.program_id(0) < per
        mine = jnp.where(jnp.where(first_half, c_ref[0] == 0, c_ref[0] == 1), 1.0, 0.0)
        g_ref[...] = mine * f_ref[...] + (1.0 - mine) * t_ref[...]
        _adamw_math(w_ref, g_ref, m_ref, v_ref, d_ref, nm_ref, nv_ref)

    full = pl.BlockSpec((bm, c_), lambda i, cr: (i, 0))
    half = pl.BlockSpec((bm, c_), lambda i, cr: (i % per, 0))
    return pl.pallas_call(
        body, name=name,
        grid_spec=pltpu.PrefetchScalarGridSpec(num_scalar_prefetch=1, grid=(r // bm,), in_specs=[full, half, half, full, full],
                                               out_specs=[full] * 4),
        out_shape=[jax.ShapeDtypeStruct((r, c_), F32)] * 4,
        compiler_params=pltpu.CompilerParams(dimension_semantics=("parallel",)),
    )(c, w, g_mine, g_theirs, m, v)


_WEIGHTS = ["a_norm", "a_w_in", "a_w_out", "b_norm", "b_w_in", "b_v_ln_g", "b_v_ln_b", "b_w_s", "b_b_s", "b_w_out",
            "c_norm", "c_w_in", "c_conv_w", "c_conv_b", "c_ln_g", "c_ln_b", "c_w_out", "d_norm", "d_w_in", "d_b_f",
            "d_w_out", "final_norm"]
_SHARD_AXIS = {"a_norm": None, "a_w_in": 2, "a_w_out": 1, "b_norm": 1, "b_w_in": 2, "b_v_ln_g": 1, "b_v_ln_b": 1, "b_w_s": None,
               "b_b_s": None, "b_w_out": 1, "c_norm": 1, "c_w_in": 2, "c_conv_w": 2, "c_conv_b": 1, "c_ln_g": 1, "c_ln_b": 1,
               "c_w_out": 1, "d_norm": 1, "d_w_in": 2, "d_b_f": None, "d_w_out": 1, "final_norm": None}
_BIG = ["a_w_in", "a_w_out", "b_w_in", "b_w_out", "c_w_in", "c_w_out", "d_w_in", "d_w_out"]
_GATHER_GROUPS = (("a_w_in", "a_w_out"), ("b_w_in", "b_w_out"), ("c_w_in", "c_w_out", "d_w_in", "d_w_out"))
_SMALL_SHARDED = [n for n in _WEIGHTS if _SHARD_AXIS[n] is not None and n not in _BIG]
_REPLICATED = [n for n in _WEIGHTS if _SHARD_AXIS[n] is None]
_ROW_ALIGN = 32


def _pack(pieces, dtype, align=_ROW_ALIGN):
    flat = jnp.concatenate([p.reshape(-1).astype(dtype) for p in pieces])
    unit = align * PACK_C
    total = -(-flat.shape[0] // unit) * unit
    return jnp.pad(flat, (0, total - flat.shape[0])).reshape(total // PACK_C, PACK_C)


def _unpack(flat, shapes):
    out, off = [], 0
    for s in shapes:
        n = math.prod(s)
        out.append(flat[off:off + n].reshape(s))
        off += n
    return out


def _full_shape(local_shape, axis):
    s = list(local_shape)
    if axis is not None:
        s[axis] *= N_CHIPS
    return tuple(s)


def _gather_weights(local):
    def whole(n, gt):
        if _SHARD_AXIS[n] == 1:
            return gt.reshape(-1, gt.shape[-1])
        if n == "d_w_in":
            return gt.transpose(1, 0, 2).reshape(gt.shape[1], -1)
        return gt

    full = {n: local[n][0] if n != "final_norm" else local[n] for n in _REPLICATED}
    first = list(_GATHER_GROUPS[0])
    mine = [local[n][0].astype(BF16) for n in first] + [_pack([local[n] for n in _SMALL_SHARDED], F32)]
    got = [_own_block(gt, s) for gt, s in zip(_allgather_chips(mine, name="gather_weights"), mine)]
    full.update({n: whole(n, gt) for n, gt in zip(first, got)})
    small = got[-1].reshape(N_CHIPS, -1)
    shards = [_unpack(small[j], [local[n].shape[1:] for n in _SMALL_SHARDED]) for j in range(N_CHIPS)]
    for i, n in enumerate(_SMALL_SHARDED):
        full[n] = jnp.concatenate([shards[j][i] for j in range(N_CHIPS)], axis=_SHARD_AXIS[n] - 1)

    def begin(k, after):
        shards_k = [local[n][0].astype(BF16) for n in _GATHER_GROUPS[k]]
        started, token = _gather_start(shards_k, after, name=f"gather{k}_start")
        return shards_k, started, token

    pending = [begin(1, got[0])]
    full["a_norm"] = full["a_norm"] + pending[0][2][0, 0]

    def finish(k):
        def weights(after):
            shards_k, started, _ = pending[k - 1]
            lands = _gather_wait(started, after, name=f"gather{k}_wait")
            token = None
            if k + 1 < len(_GATHER_GROUPS):
                pending.append(begin(k + 1, lands[0]))
                token = pending[k][2]
            lands = _sibling_exchange(lands, name=f"gather{k}_exchange")
            out = {n: whole(n, _own_block(gt, s)) for n, gt, s in zip(_GATHER_GROUPS[k], lands, shards_k)}
            if token is not None:
                gain = _GATHER_GROUPS[k][0][0] + "_norm"
                out[gain] = full[gain] + token[0, 0]
            return out
        return weights

    return full, [finish(k) for k in range(1, len(_GATHER_GROUPS))]


def _repl_piece_len(local):
    total = sum(math.prod(local[n].shape) for n in _REPLICATED)
    return -(-total // N_CHIPS)


def _reduce_grads(g, local, early):
    rep_flat = jnp.concatenate([g[n].reshape(-1) for n in _REPLICATED])
    piece = _repl_piece_len(local)
    rep_flat = jnp.pad(rep_flat, (0, N_CHIPS * piece - rep_flat.shape[0]))

    def shard(n, j):
        full = g[n].reshape(_full_shape(local[n].shape, _SHARD_AXIS[n]))
        width = local[n].shape[_SHARD_AXIS[n]]
        return lax.slice_in_dim(full, j * width, (j + 1) * width, axis=_SHARD_AXIS[n])

    small = jnp.stack([_pack([shard(n, j) for n in _SMALL_SHARDED] + [rep_flat[j * piece:(j + 1) * piece]], F32)
                       for j in range(N_CHIPS)])
    early_names = [n for _, names, _, _ in early for n in names]
    late = [n for n in _BIG if n not in early_names]
    hps = _reduce_to_chips([g[n] for n in late] + [small], [BF16] * len(late) + [F32], tag="grads")
    rbs = list(_scatter_chips([hw for _, hw in hps], name="grads_scatter_chips"))
    early_halves, early_rbs = [], []
    for tag, _, halves_k, started in early:
        early_halves += halves_k
        early_rbs += _scatter_wait(started, rbs[0], name=f"{tag}_scatter_wait")
    halves = early_halves + [hf for hf, _ in hps]
    fs = [_add_chips(hf, rb, name=f"grads_add_chips{i}") for i, (hf, rb) in enumerate(zip(halves, early_rbs + rbs))]
    theirs = _join_halves(fs, name="grads_join_halves")
    red = dict(zip(early_names + late, zip(fs, theirs)))
    out = _unpack(_stitch(fs[-1], theirs[-1]).reshape(-1), [local[n].shape for n in _SMALL_SHARDED] + [(piece,)])
    red.update(zip(_SMALL_SHARDED, out[:-1]))
    rep_mine = _pack([out[-1]], F32)
    rep = _own_block(_allgather_chips([rep_mine], name="gather_replicated_grads")[0], rep_mine)
    rep = rep.reshape(N_CHIPS, -1)[:, :piece].reshape(-1)
    for n, val in zip(_REPLICATED, _unpack(rep, [local[n].shape for n in _REPLICATED])):
        red[n] = val
    return red


def _update(local, grads, m, v):
    grads, delta, new_m, new_v = dict(grads), {}, {}, {}
    for n in _BIG:
        shp = local[n].shape
        two = (shp[-2], shp[-1])
        res = _adamw(local[n].reshape(two), *grads[n], m[n].reshape(two), v[n].reshape(two), name=f"adamw_{n}")
        grads[n], delta[n], new_m[n], new_v[n] = [r.reshape(shp) for r in res]
    small = [n for n in _WEIGHTS if n not in _BIG]
    two = {n: (math.prod(local[n].shape[:-1]), local[n].shape[-1]) for n in small}
    res = _adamw_many([[src[n].reshape(two[n]) for n in small] for src in (local, grads, m, v)], name="adamw_small")
    for dst, rs in zip((delta, new_m, new_v), res):
        for n, val in zip(small, rs):
            dst[n] = val.reshape(local[n].shape)
    return grads, delta, new_m, new_v


def kernel(x, a_norm, a_w_in, a_w_out, b_norm, b_w_in, b_v_ln_g, b_v_ln_b, b_w_s, b_b_s, b_w_out, c_norm, c_w_in, c_conv_w, c_conv_b, c_ln_g, c_ln_b, c_w_out, d_norm, d_w_in, d_b_f, d_w_out, final_norm, loss_target, m_a_norm, m_a_w_in, m_a_w_out, m_b_norm, m_b_w_in, m_b_v_ln_g, m_b_v_ln_b, m_b_w_s, m_b_b_s, m_b_w_out, m_c_norm, m_c_w_in, m_c_conv_w, m_c_conv_b, m_c_ln_g, m_c_ln_b, m_c_w_out, m_d_norm, m_d_w_in, m_d_b_f, m_d_w_out, m_final_norm, v_a_norm, v_a_w_in, v_a_w_out, v_b_norm, v_b_w_in, v_b_v_ln_g, v_b_v_ln_b, v_b_w_s, v_b_b_s, v_b_w_out, v_c_norm, v_c_w_in, v_c_conv_w, v_c_conv_b, v_c_ln_g, v_c_ln_b, v_c_w_out, v_d_norm, v_d_w_in, v_d_b_f, v_d_w_out, v_final_norm):
    local = dict(zip(_WEIGHTS, (a_norm, a_w_in, a_w_out, b_norm, b_w_in, b_v_ln_g, b_v_ln_b, b_w_s, b_b_s, b_w_out, c_norm, c_w_in,
                                c_conv_w, c_conv_b, c_ln_g, c_ln_b, c_w_out, d_norm, d_w_in, d_b_f, d_w_out, final_norm)))
    m = dict(zip(_WEIGHTS, (m_a_norm, m_a_w_in, m_a_w_out, m_b_norm, m_b_w_in, m_b_v_ln_g, m_b_v_ln_b, m_b_w_s, m_b_b_s, m_b_w_out,
                            m_c_norm, m_c_w_in, m_c_conv_w, m_c_conv_b, m_c_ln_g, m_c_ln_b, m_c_w_out, m_d_norm, m_d_w_in, m_d_b_f,
                            m_d_w_out, m_final_norm)))
    v = dict(zip(_WEIGHTS, (v_a_norm, v_a_w_in, v_a_w_out, v_b_norm, v_b_w_in, v_b_v_ln_g, v_b_v_ln_b, v_b_w_s, v_b_b_s, v_b_w_out,
                            v_c_norm, v_c_w_in, v_c_conv_w, v_c_conv_b, v_c_ln_g, v_c_ln_b, v_c_w_out, v_d_norm, v_d_w_in, v_d_b_f,
                            v_d_w_out, v_final_norm)))
    loss_part, grad_x, g, early = _local_step(x, loss_target, *_gather_weights(local), _start_reduce)
    loss = lax.psum(loss_part, ("x", "y", "c"))
    grads = _reduce_grads(g, local, early)
    grads, delta, new_m, new_v = _update(local, grads, m, v)
    return (loss, grad_x, *[grads[n] for n in _WEIGHTS], *[delta[n] for n in _WEIGHTS],
            *[new_m[n] for n in _WEIGHTS], *[new_v[n] for n in _WEIGHTS])
```

```python
import math

import jax
import jax.numpy as jnp
from jax import lax
from jax.experimental import pallas as pl
from jax.experimental.pallas import tpu as pltpu

F32, BF16 = jnp.float32, jnp.bfloat16
MESH = pl.DeviceIdType.MESH

D_MODEL = 1024
HEADS = 16
HEAD_DIM = 64
BLK = 128
PAIRS = HEADS // 2
GM_W = 2048
GM_G = 16
CV_W = 2048
CV_K = 31
HALO = 32
EPS = 1e-6
N_CHIPS = 4
PACK_C = 1024
ADAM_LR, ADAM_B1, ADAM_B2, ADAM_EPS, ADAM_WD, ADAM_STEP = 0.001, 0.9, 0.999, 1e-08, 0.01, 10

_NT = (((1,), (1,)), ((), ()))
_TN = (((0,), (0,)), ((), ()))
_NN = (((1,), (0,)), ((), ()))


def _dot(a, b, dims=_NN):
    return lax.dot_general(a, b, dims, preferred_element_type=F32)


def _split3(x):
    hi = x.astype(BF16)
    r = x - hi.astype(F32)
    mid = r.astype(BF16)
    lo = (r - mid.astype(F32)).astype(BF16)
    return hi, mid, lo


def _dot3_left(m, x):
    hi, mid, lo = _split3(x)
    return _dot(m, hi) + _dot(m, mid) + _dot(m, lo)


def _sigmoid(x):
    return 1.0 / (1.0 + jnp.exp(-x))


def _silu(x):
    return x * _sigmoid(x)


def _dsilu(x):
    s = _sigmoid(x)
    return s * (1.0 + x * (1.0 - s))


_GELU_C = math.sqrt(2.0 / math.pi)
_GELU_A = 0.044715


def _gelu(x):
    return 0.5 * x * (1.0 + jnp.tanh(_GELU_C * (x + _GELU_A * x * x * x)))


def _dgelu(x):
    t = jnp.tanh(_GELU_C * (x + _GELU_A * x * x * x))
    return 0.5 * (1.0 + t) + 0.5 * x * (1.0 - t * t) * _GELU_C * (1.0 + 3.0 * _GELU_A * x * x)


def _log_sigmoid(x):
    return jnp.minimum(x, 0.0) - jnp.log(1.0 + jnp.exp(-jnp.abs(x)))


def _rms_fwd(x, g):
    r = lax.rsqrt(jnp.mean(x * x, axis=-1, keepdims=True) + EPS)
    return x * r * g


def _rms_bwd(dy, x, g):
    r = lax.rsqrt(jnp.mean(x * x, axis=-1, keepdims=True) + EPS)
    xh = x * r
    dxh = dy * g
    dx = r * (dxh - xh * jnp.mean(dxh * xh, axis=-1, keepdims=True))
    return dx, dy * xh


def _ln_stats(x):
    mu = jnp.mean(x, axis=-1, keepdims=True)
    xc = x - mu
    r = lax.rsqrt(jnp.mean(xc * xc, axis=-1, keepdims=True) + EPS)
    return xc * r, r


def _ln_bwd(dy, xh, r, g):
    dxh = dy * g
    return r * (dxh - jnp.mean(dxh, axis=-1, keepdims=True) - xh * jnp.mean(dxh * xh, axis=-1, keepdims=True))


def _colsum(x):
    return jnp.sum(x, axis=0, keepdims=True)


def _tile(n, want):
    for t in range(min(n, want), 7, -1):
        if n % t == 0 and t % 8 == 0:
            return t
    return n


MM_TILE = 1024


def _matmul(a, b, *, name, mode="nn", residual=None, out_shards=1, out_dtype=F32):
    (m, k) = a.shape
    b_shards = b.shape[0] if b.ndim == 3 else 1
    if mode == "nn":
        n = b.shape[-1] * b_shards
        tn, tk = _tile(n // max(b_shards, out_shards), MM_TILE), _tile(k, MM_TILE)
    else:
        n = b.shape[-2]
        tn, tk = _tile(n // out_shards, MM_TILE), _tile(k // b_shards, MM_TILE)
    tm = _tile(m, MM_TILE)
    nk = k // tk
    a_spec = pl.BlockSpec((tm, tk), lambda i, j, kk: (i, kk))
    if mode == "nn":
        dims = _NN
        if b_shards == 1:
            b_spec = pl.BlockSpec((tk, tn), lambda i, j, kk: (kk, j))
        else:
            per_b = n // b_shards // tn
            b_spec = pl.BlockSpec((None, tk, tn), lambda i, j, kk: (j // per_b, kk, j % per_b))
    else:
        dims = _NT
        if b_shards == 1:
            b_spec = pl.BlockSpec((tn, tk), lambda i, j, kk: (j, kk))
        else:
            per_b = k // b_shards // tk
            b_spec = pl.BlockSpec((None, tn, tk), lambda i, j, kk: (kk // per_b, j, kk % per_b))
    if out_shards == 1:
        o_spec = pl.BlockSpec((tm, tn), lambda i, j, kk: (i, j))
        o_shape = (m, n)
    else:
        per_o = n // out_shards // tn
        o_spec = pl.BlockSpec((None, tm, tn), lambda i, j, kk: (j // per_o, i, j % per_o))
        o_shape = (out_shards, m, n // out_shards)
    has_res = residual is not None

    def body(a_ref, b_ref, *rest):
        o_ref = rest[-1]
        kk = pl.program_id(2)
        part = _dot(a_ref[...].astype(BF16), b_ref[...].astype(BF16), dims)
        if has_res:
            @pl.when(kk == 0)
            def _():
                o_ref[...] = part + rest[0][...]
        else:
            @pl.when(kk == 0)
            def _():
                o_ref[...] = part.astype(out_dtype)

        if nk > 1:
            @pl.when(kk > 0)
            def _():
                o_ref[...] += part

    assert out_dtype == F32 or nk == 1
    return pl.pallas_call(
        body, name=name, grid=(m // tm, n // tn, nk),
        in_specs=[a_spec, b_spec] + ([o_spec] if has_res else []),
        out_specs=o_spec, out_shape=jax.ShapeDtypeStruct(o_shape, out_dtype),
        compiler_params=pltpu.CompilerParams(dimension_semantics=("parallel", "parallel", "arbitrary")),
    )(a, b, *([residual] if has_res else []))


def _matmul_parts(at, parts, *, name):
    m, k = at.shape
    n = parts[0].shape[1]
    tm, tk = _tile(m, MM_TILE // 2), _tile(k, MM_TILE // 2)
    n_parts = len(parts)

    def body(a_ref, *rest):
        o_ref = rest[-1]
        kk = pl.program_id(1)
        a = a_ref[...].astype(BF16)
        for p in range(n_parts):
            part = _dot(a, rest[p][...].astype(BF16))

            @pl.when(kk == 0)
            def _():
                o_ref[p] = part

            @pl.when(kk > 0)
            def _():
                o_ref[p] += part

    return pl.pallas_call(
        body, name=name, grid=(m // tm, k // tk),
        in_specs=[pl.BlockSpec((tm, tk), lambda i, kk: (i, kk))] + [pl.BlockSpec((tk, n), lambda i, kk: (kk, 0))] * n_parts,
        out_specs=pl.BlockSpec((n_parts, tm, n), lambda i, kk: (0, i, 0)),
        out_shape=jax.ShapeDtypeStruct((n_parts, m, n), F32),
        compiler_params=pltpu.CompilerParams(dimension_semantics=("parallel", "arbitrary")),
    )(at, *parts)


def _matmul_sum(pairs, *, name):
    m, n = pairs[0][0].shape[0], pairs[0][1].shape[0]
    tm, tn = _tile(m, MM_TILE // 2), _tile(n, MM_TILE)
    n_pairs = len(pairs)

    def body(*refs):
        acc = None
        for p in range(n_pairs):
            part = _dot(refs[2 * p][...].astype(BF16), refs[2 * p + 1][...].astype(BF16), _NT)
            acc = part if acc is None else acc + part
        refs[-1][...] = acc

    in_specs = []
    for a, b in pairs:
        in_specs += [pl.BlockSpec((tm, a.shape[1]), lambda i, j: (i, 0)), pl.BlockSpec((tn, b.shape[1]), lambda i, j: (j, 0))]
    return pl.pallas_call(
        body, name=name, grid=(m // tm, n // tn), in_specs=in_specs, out_specs=pl.BlockSpec((tm, tn), lambda i, j: (i, j)),
        out_shape=jax.ShapeDtypeStruct((m, n), F32),
        compiler_params=pltpu.CompilerParams(dimension_semantics=("parallel", "parallel")),
    )(*[x for pair in pairs for x in pair])


def _rows(fn, *, name, steps, ins, outs, accs=(), scratch=()):
    ni, no, na = len(ins), len(outs), len(accs)

    def body(*refs):
        in_refs, out_refs = refs[:ni], refs[ni:ni + no]
        acc_refs, scr = refs[ni + no:ni + no + na], refs[ni + no + na:]
        i = pl.program_id(0)

        @pl.when(i == 0)
        def _():
            for r in acc_refs:
                r[...] = jnp.zeros(r.shape, r.dtype)

        fn(i, in_refs, out_refs, acc_refs, scr)

    def full(shape):
        nd = len(shape)
        return pl.BlockSpec(tuple(shape), lambda i: (0,) * nd)

    res = pl.pallas_call(
        body, name=name, grid=(steps,),
        in_specs=[pl.BlockSpec(bs, im) for _, bs, im in ins],
        out_specs=[pl.BlockSpec(bs, im) for _, _, bs, im in outs] + [full(s) for s, _ in accs],
        out_shape=[jax.ShapeDtypeStruct(s, d) for s, d, _, _ in outs] + [jax.ShapeDtypeStruct(s, d) for s, d in accs],
        scratch_shapes=list(scratch),
        compiler_params=pltpu.CompilerParams(dimension_semantics=("arbitrary",)),
    )(*[a for a, _, _ in ins])
    return res


def _rb(arr, bm, cb=0, width=None):
    w = arr.shape[1] if width is None else width
    return (arr, (bm, w), lambda i: (i, cb))


def _const(arr):
    nd = arr.ndim
    return (arr, tuple(arr.shape), lambda i: (0,) * nd)


def _ro(t, w, dtype, bm):
    return ((t, w), dtype, (bm, w), lambda i: (i, 0))


def _rot(t, w, dtype, bm):
    return ((w, t), dtype, (w, bm), lambda i: (0, i))


def _rmsnorm(x, g, *, name, bm=512):
    t, d = x.shape
    bm = _tile(t, bm)

    def fn(i, ins, outs, accs, scr):
        h = _rms_fwd(ins[0][...], ins[1][...])
        outs[0][...] = h.astype(BF16)
        outs[1][...] = h.T.astype(BF16)

    return _rows(fn, name=name, steps=t // bm, ins=[_rb(x, bm), _const(g)], outs=[_ro(t, d, BF16, bm), _rot(t, d, BF16, bm)])


def _rmsnorm_bwd(dh, x, g, dres, *, name, bm=512):
    t, d = x.shape
    bm = _tile(t, bm)

    def fn(i, ins, outs, accs, scr):
        dx, dgrow = _rms_bwd(ins[0][...], ins[1][...], ins[2][...])
        outs[0][...] = ins[3][...] + dx
        accs[0][...] += _colsum(dgrow)

    return _rows(fn, name=name, steps=t // bm, ins=[_rb(dh, bm), _rb(x, bm), _const(g), _rb(dres, bm)],
                 outs=[_ro(t, d, F32, bm)], accs=[((1, d), F32)])


def _gate(o, p, gcb, *, name, bm=512):
    t, w = o.shape
    bm = _tile(t, bm)

    def fn(i, ins, outs, accs, scr):
        y = ins[0][...] * _silu(ins[1][...])
        outs[0][...] = y.astype(BF16)
        outs[1][...] = y.T.astype(BF16)

    return _rows(fn, name=name, steps=t // bm, ins=[_rb(o, bm), _rb(p, bm, gcb, w)],
                 outs=[_ro(t, w, BF16, bm), _rot(t, w, BF16, bm)])


def _gate_bwd(dy, o, p, gcb, *, name, bm=512):
    t, w = o.shape
    bm = _tile(t, bm)

    def fn(i, ins, outs, accs, scr):
        dy_, o_, g_ = ins[0][...], ins[1][...], ins[2][...]
        outs[0][...] = dy_ * _silu(g_)
        outs[1][...] = dy_ * o_ * _dsilu(g_)

    return _rows(fn, name=name, steps=t // bm, ins=[_rb(dy, bm), _rb(o, bm), _rb(p, bm, gcb, w)],
                 outs=[_ro(t, w, F32, bm), _ro(t, w, F32, bm)])


def _loss_head(x, g, tgt, *, name, bm=512):
    t, d = x.shape
    bm = _tile(t, bm)

    def fn(i, ins, outs, accs, scr):
        x_, g_, tg = ins[0][...], ins[1][...], ins[2][...]
        err = _rms_fwd(x_, g_) - tg
        part = 0.5 * jnp.sum(jnp.sum(err * err, axis=-1, keepdims=True), axis=0, keepdims=True) / d
        dx, dgrow = _rms_bwd(err / d, x_, g_)
        outs[0][...] = dx
        accs[0][...] += _colsum(dgrow)
        accs[1][...] += jnp.broadcast_to(part, (1, BLK))

    return _rows(fn, name=name, steps=t // bm, ins=[_rb(x, bm), _const(g), _rb(tgt, bm)],
                 outs=[_ro(t, d, F32, bm)], accs=[((1, d), F32), ((1, BLK), F32)])


def _gmlp_mix_weights(ws_ref, g):
    row = lax.broadcasted_iota(jnp.int32, (BLK, BLK), 0)
    col = lax.broadcasted_iota(jnp.int32, (BLK, BLK), 1)
    tril = col <= row
    return jnp.where(tril, ws_ref[g], 0.0), tril


def _gmlp_fwd(p, ln_g, ln_b, w_s, bs_t, *, name):
    t = p.shape[0]

    def fn(i, ins, outs, accs, scr):
        p_ref, lg, lb, ws_ref, bst = ins
        vn = _ln_stats(_gelu(p_ref[:, GM_W:2 * GM_W]))[0] * lg[...] + lb[...]
        for g in range(GM_G):
            cs = slice(g * BLK, (g + 1) * BLK)
            wt, _ = _gmlp_mix_weights(ws_ref, g)
            s = _dot(wt.astype(BF16), vn[:, cs].astype(BF16)) + bst[:, g:g + 1]
            u = _gelu(p_ref[:, cs])
            gate = p_ref[:, 2 * GM_W + g * BLK:2 * GM_W + (g + 1) * BLK]
            y = u * s * _silu(gate)
            outs[0][:, cs] = y.astype(BF16)
            outs[1][cs, :] = y.T.astype(BF16)

    return _rows(fn, name=name, steps=t // BLK, ins=[_rb(p, BLK), _const(ln_g), _const(ln_b), _const(w_s), _const(bs_t)],
                 outs=[_ro(t, GM_W, BF16, BLK), _rot(t, GM_W, BF16, BLK)])


def _gmlp_bwd(dy, p, ln_g, ln_b, w_s, bs_t, *, name):
    t = p.shape[0]

    def fn(i, ins, outs, accs, scr):
        dy_ref, p_ref, lg, lb, ws_ref, bst = ins
        dp_ref = outs[0]
        dlg, dlb, dws, dbst = accs
        dvn_ref = scr[0]
        v_pre = p_ref[:, GM_W:2 * GM_W]
        xh, r = _ln_stats(_gelu(v_pre))
        vn = xh * lg[...] + lb[...]
        for g in range(GM_G):
            cs = slice(g * BLK, (g + 1) * BLK)
            gs = slice(2 * GM_W + g * BLK, 2 * GM_W + (g + 1) * BLK)
            wt, tril = _gmlp_mix_weights(ws_ref, g)
            vg = vn[:, cs].astype(BF16)
            s = _dot(wt.astype(BF16), vg) + bst[:, g:g + 1]
            u_pre, gate, dyg = p_ref[:, cs], p_ref[:, gs], dy_ref[:, cs]
            u = _gelu(u_pre)
            dos = dyg * _silu(gate)
            dp_ref[:, gs] = dyg * u * s * _dsilu(gate)
            dp_ref[:, cs] = dos * s * _dgelu(u_pre)
            ds = (dos * u).astype(BF16)
            dws[g] += jnp.where(tril, _dot(ds, vg, _NT), 0.0)
            dbst[:, g:g + 1] += jnp.sum(dos * u, axis=1, keepdims=True)
            dvn_ref[:, cs] = _dot(wt.astype(BF16), ds, _TN)
        dvn = dvn_ref[...]
        dlg[...] += _colsum(dvn * xh)
        dlb[...] += _colsum(dvn)
        dp_ref[:, GM_W:2 * GM_W] = _ln_bwd(dvn, xh, r, lg[...]) * _dgelu(v_pre)

    return _rows(fn, name=name, steps=t // BLK,
                 ins=[_rb(dy, BLK), _rb(p, BLK), _const(ln_g), _const(ln_b), _const(w_s), _const(bs_t)],
                 outs=[_ro(t, 3 * GM_W, F32, BLK)],
                 accs=[((1, GM_W), F32), ((1, GM_W), F32), ((GM_G, BLK, BLK), F32), ((BLK, GM_G), F32)],
                 scratch=[pltpu.VMEM((BLK, GM_W), F32)])


CV_BM = 128
CV_RC = 8
CV_ROWS = 16
SUBLANES = 8
CV_FWD_OFFS = [HALO - (CV_K - 1) + k for k in range(CV_K)]
CV_BWD_OFFS = [CV_K - 1 - k for k in range(CV_K)]


def _conv_halo_prev(p, cb, bm):
    per = bm // HALO
    return (p, (HALO, CV_W), lambda i: (jnp.maximum(i * per - 1, 0), cb))


def _conv_scratch(bm):
    return [pltpu.VMEM((bm + HALO, CV_W), F32), pltpu.VMEM((SUBLANES - 1, bm + HALO - SUBLANES, CV_W), F32),
            pltpu.VMEM((bm, CV_W), F32)]


def _conv_shift_copies(ext_ref, sh_ref):
    rows = sh_ref.shape[1]
    for b in range(1, SUBLANES):
        sh_ref[b - 1] = ext_ref[pl.ds(b, rows), :]


def _conv_window(ext_ref, sh_ref, off, r0, rows):
    b = off % SUBLANES
    src = ext_ref if b == 0 else sh_ref.at[b - 1]
    return src[pl.ds(r0 + (off - b), rows), :]


def _conv_taps(ext_ref, sh_ref, cw_ref, y_ref, offs):
    bm = y_ref.shape[0]

    def chunk(ci, c):
        r0 = pl.multiple_of(ci * CV_RC, CV_RC)
        acc = jnp.zeros((CV_RC, CV_W), F32)
        for k in range(CV_K):
            acc = acc + cw_ref[pl.ds(k * SUBLANES, CV_RC), :] * _conv_window(ext_ref, sh_ref, offs[k], r0, CV_RC)
        y_ref[pl.ds(r0, CV_RC), :] = acc
        return c

    lax.fori_loop(0, bm // CV_RC, chunk, 0)


def _conv_dweights(dy1_ref, ext_ref, sh_ref, dcw_ref):
    bm = dy1_ref.shape[0]
    groups = 4
    for k in range(CV_K):
        def step(ci, acc, off=CV_FWD_OFFS[k]):
            prods = []
            for u in range(groups):
                r0 = pl.multiple_of((ci * groups + u) * CV_RC, CV_RC)
                prods.append(dy1_ref[pl.ds(r0, CV_RC), :] * _conv_window(ext_ref, sh_ref, off, r0, CV_RC))
            return acc + ((prods[0] + prods[1]) + (prods[2] + prods[3]))

        dcw_ref[k:k + 1, :] += _colsum(lax.fori_loop(0, bm // (CV_RC * groups), step, jnp.zeros((CV_RC, CV_W), F32)))


def _conv_fill(i, ext_ref, a_prev, b_prev, a, b, bm, seq):
    keep = jnp.where((i % (seq // bm)) == 0, 0.0, 1.0)
    ext_ref[pl.ds(0, HALO), :] = keep * (a_prev * _sigmoid(b_prev))
    ext_ref[pl.ds(HALO, bm), :] = a * _sigmoid(b)


def _conv_fwd(p, cw, cb, ln_g, ln_b, seq, *, name, bm=CV_BM):
    t = p.shape[0]

    def fn(i, ins, outs, accs, scr):
        cw_ref, cb_, lg, lb = ins[5], ins[6][...], ins[7][...], ins[8][...]
        ext, sh, y = scr
        _conv_fill(i, ext, ins[3][...], ins[4][...], ins[0][...], ins[1][...], bm, seq)
        _conv_shift_copies(ext, sh)
        _conv_taps(ext, sh, cw_ref, y, CV_FWD_OFFS)

        def rows(ci, c):
            rs = pl.ds(pl.multiple_of(ci * CV_ROWS, CV_ROWS), CV_ROWS)
            y2 = _ln_stats(y[rs, :] + cb_)[0] * lg + lb
            outs[0][rs, :] = (_silu(y2) * _silu(ins[2][rs, :])).astype(BF16)
            return c

        lax.fori_loop(0, bm // CV_ROWS, rows, 0)
        outs[1][...] = outs[0][...].astype(F32).T.astype(BF16)

    return _rows(fn, name=name, steps=t // bm,
                 ins=[_rb(p, bm, 0, CV_W), _rb(p, bm, 1, CV_W), _rb(p, bm, 2, CV_W),
                      _conv_halo_prev(p, 0, bm), _conv_halo_prev(p, 1, bm),
                      _const(cw), _const(cb), _const(ln_g), _const(ln_b)],
                 outs=[_ro(t, CV_W, BF16, bm), _rot(t, CV_W, BF16, bm)], scratch=_conv_scratch(bm))


def _conv_bwd_post(dy, p, cw, cb, ln_g, ln_b, seq, *, name, bm=CV_BM):
    t = p.shape[0]

    def fn(i, ins, outs, accs, scr):
        cw_ref, cb_, lg, lb = ins[6], ins[7][...], ins[8][...], ins[9][...]
        dlg, dlb, dcb, dcw = accs
        ext, sh, y = scr
        _conv_fill(i, ext, ins[4][...], ins[5][...], ins[1][...], ins[2][...], bm, seq)
        _conv_shift_copies(ext, sh)
        _conv_taps(ext, sh, cw_ref, y, CV_FWD_OFFS)

        def rows(ci, c):
            rs = pl.ds(pl.multiple_of(ci * CV_ROWS, CV_ROWS), CV_ROWS)
            xh, r = _ln_stats(y[rs, :] + cb_)
            y2 = xh * lg + lb
            dy_, gate = ins[0][rs, :], ins[3][rs, :]
            outs[1][rs, :] = dy_ * _silu(y2) * _dsilu(gate)
            dy2 = dy_ * _silu(gate) * _dsilu(y2)
            dlg[...] += _colsum(dy2 * xh)
            dlb[...] += _colsum(dy2)
            dy1 = _ln_bwd(dy2, xh, r, lg)
            outs[0][rs, :] = dy1
            dcb[...] += _colsum(dy1)
            return c

        lax.fori_loop(0, bm // CV_ROWS, rows, 0)
        _conv_dweights(outs[0], ext, sh, dcw)

    return _rows(fn, name=name, steps=t // bm,
                 ins=[_rb(dy, bm), _rb(p, bm, 0, CV_W), _rb(p, bm, 1, CV_W), _rb(p, bm, 2, CV_W),
                      _conv_halo_prev(p, 0, bm), _conv_halo_prev(p, 1, bm),
                      _const(cw), _const(cb), _const(ln_g), _const(ln_b)],
                 outs=[_ro(t, CV_W, F32, bm), _ro(t, CV_W, F32, bm)],
                 accs=[((1, CV_W), F32), ((1, CV_W), F32), ((1, CV_W), F32), ((CV_K, CV_W), F32)],
                 scratch=_conv_scratch(bm))


def _conv_bwd_pre(dy1, dgate, p, cw, seq, *, name, bm=CV_BM):
    t = p.shape[0]
    per = bm // HALO
    last_halo = t // HALO - 1

    def fn(i, ins, outs, accs, scr):
        d1, d1n, dg, a, b = [r[...] for r in ins[:5]]
        ext, sh, y = scr
        keep = jnp.where((i % (seq // bm)) == (seq // bm - 1), 0.0, 1.0)
        ext[pl.ds(0, bm), :] = d1
        ext[pl.ds(bm, HALO), :] = keep * d1n
        _conv_shift_copies(ext, sh)
        _conv_taps(ext, sh, ins[5], y, CV_BWD_OFFS)
        dy0 = y[...]
        sb = _sigmoid(b)
        outs[0][:, 0:CV_W] = dy0 * sb
        outs[0][:, CV_W:2 * CV_W] = dy0 * a * sb * (1.0 - sb)
        outs[0][:, 2 * CV_W:3 * CV_W] = dg

    return _rows(fn, name=name, steps=t // bm,
                 ins=[_rb(dy1, bm), (dy1, (HALO, CV_W), lambda i: (jnp.minimum((i + 1) * per, last_halo), 0)),
                      _rb(dgate, bm), _rb(p, bm, 0, CV_W), _rb(p, bm, 1, CV_W), _const(cw)],
                 outs=[_ro(t, 3 * CV_W, F32, bm)], scratch=_conv_scratch(bm))[0]


def _iotas():
    row = lax.broadcasted_iota(jnp.int32, (BLK, BLK), 0)
    col = lax.broadcasted_iota(jnp.int32, (BLK, BLK), 1)
    return row, col


def _heads(x, head0):
    if head0.shape != x.shape:
        head0 = lax.broadcasted_iota(jnp.int32, x.shape, 1) < HEAD_DIM
    return jnp.where(head0, x, 0.0).astype(BF16), jnp.where(head0, 0.0, x).astype(BF16)


def _pair_spec(seq, off):
    return pl.BlockSpec((seq, BLK), lambda b, hp: (b, off + hp))


def _stat_spec(seq):
    return pl.BlockSpec((None, None, seq, BLK), lambda b, hp: (b, hp, 0, 0))


_ATT_PARAMS = dict(compiler_params=pltpu.CompilerParams(dimension_semantics=("parallel", "parallel")))
_SCALE = 1.0 / math.sqrt(HEAD_DIM)


Q_BLOCK = 256
KEY_BLOCK = 256
FOX_BLOCK = 512


def _stack_heads(x, head0, scale=None):
    if scale is not None:
        x = x * scale
    return jnp.concatenate(_heads(x, head0), axis=0)


def _pair_cols(x, head0, fill):
    a = jnp.max(jnp.where(head0, x, fill), axis=1, keepdims=True)
    b = jnp.max(jnp.where(head0, fill, x), axis=1, keepdims=True)
    return jnp.concatenate([a, b], axis=0)


def _causal_mask(t0, s0, tq, kw, inclusive):
    row = lax.broadcasted_iota(jnp.int32, (2 * tq, kw), 0) & (tq - 1)
    col = lax.broadcasted_iota(jnp.int32, (2 * tq, kw), 1)
    return (s0 + col) <= (t0 + row) if inclusive else (s0 + col) < (t0 + row)


def _sub(x, j):
    return x[:, j * BLK:(j + 1) * BLK]


def _tri_blocks(kw, relation):
    r = lax.broadcasted_iota(jnp.int32, (kw, kw), 0)
    c = lax.broadcasted_iota(jnp.int32, (kw, kw), 1)
    return (((r // BLK) == (c // BLK)) & relation(r, c)).astype(BF16)


def _block_cumsum(x, tri, ksub):
    hi = x.astype(BF16)
    lo = (x - hi.astype(F32)).astype(BF16)
    cs = _dot(jnp.concatenate([hi, lo], axis=0), tri)
    n = x.shape[0]
    cs = cs[:n] + cs[n:]
    return [_sub(cs, j) for j in range(ksub)], [jnp.sum(_sub(x, j), axis=1, keepdims=True) for j in range(ksub)]


def _sb_terms_z(z, mask):
    t = jnp.log(1.0 + jnp.exp(-jnp.abs(z)))
    lsz = jnp.minimum(z, 0.0) - t
    lr = lsz - z
    if mask is not None:
        lr = jnp.where(mask, lr, 0.0)
    return lsz, lr


def _sb_fwd(p, nb, seq, *, name):
    tq = min(Q_BLOCK, seq)
    nq = seq // tq
    kw = min(KEY_BLOCK, seq)
    ksub = kw // BLK

    def body(q_ref, k_ref, v_ref, o_ref, tot_ref):
        row, col = _iotas()
        colq = lax.broadcasted_iota(jnp.int32, (tq, BLK), 1)
        head0 = colq < HEAD_DIM
        upper = _tri_blocks(kw, lambda j, s: j > s)

        def qblock(qb, c):
            t0 = pl.multiple_of(qb * tq, tq)
            qs = _stack_heads(q_ref[pl.ds(t0, tq), :], head0, _SCALE)
            diag = (t0 + tq - 1) // kw

            def kblock(kb, carry, masked):
                acc, run = carry
                s0 = pl.multiple_of(kb * kw, kw)
                k = k_ref[pl.ds(s0, kw), :].astype(BF16)
                v0, v1 = _heads(v_ref[pl.ds(s0, kw), :], head0)
                mask = _causal_mask(t0, s0, tq, kw, False)[:tq] if masked else None
                zs = [_dot(qs[h * tq:(h + 1) * tq], k, _NT) for h in range(2)]
                terms = []
                for h in range(2):
                    lsz, lr = _sb_terms_z(zs[h], mask)
                    terms.append((lsz,) + _block_cumsum(lr, upper, ksub))
                runs = []
                for h, vh in enumerate((v0, v1)):
                    lsz, after, total = terms[h]
                    r = run[h]
                    ws = [None] * ksub
                    for j in reversed(range(ksub)):
                        w = jnp.exp(_sub(lsz, j) + after[j] + r)
                        if masked:
                            w = jnp.where(_sub(mask, j), w, 0.0)
                        ws[j] = w.astype(BF16)
                        r = r + total[j]
                    acc = acc + _dot(jnp.concatenate(ws, axis=1), vh)
                    runs.append(r)
                return acc, tuple(runs)

            zc = jnp.zeros((tq, 1), F32)
            carry = kblock(diag, (jnp.zeros((tq, BLK), F32), (zc, zc)), True)
            acc, run = lax.fori_loop(0, diag, lambda it, cr: kblock(diag - 1 - it, cr, False), carry)
            o_ref[pl.ds(t0, tq), :] = acc
            tot_ref[pl.ds(t0, tq), :] = jnp.where(head0, run[0], run[1])
            return c

        lax.fori_loop(0, nq, qblock, 0)

    return pl.pallas_call(
        body, name=name, grid=(nb, PAIRS),
        in_specs=[_pair_spec(seq, 0), _pair_spec(seq, PAIRS), _pair_spec(seq, 2 * PAIRS)],
        out_specs=[_pair_spec(seq, 0), _stat_spec(seq)],
        out_shape=[jax.ShapeDtypeStruct((nb * seq, D_MODEL), F32), jax.ShapeDtypeStruct((nb, PAIRS, seq, BLK), F32)],
        **_ATT_PARAMS,
    )(p, p, p)


def _sb_bwd(p, do, tot, nb, seq, *, name):
    tq = min(Q_BLOCK, seq)
    nq = seq // tq
    kw = min(KEY_BLOCK, seq)
    ksub = kw // BLK

    def body(q_ref, k_ref, v_ref, do_ref, tot_ref, dq_ref, dk_ref, dv_ref):
        row, col = _iotas()
        colq = lax.broadcasted_iota(jnp.int32, (tq, BLK), 1)
        head0 = colq < HEAD_DIM
        lower_incl = _tri_blocks(kw, lambda j, s: j <= s)
        lower_strict = _tri_blocks(kw, lambda s, j: s < j)
        dk_ref[...] = jnp.zeros(dk_ref.shape, F32)
        dv_ref[...] = jnp.zeros(dv_ref.shape, F32)

        def qblock(qb, c):
            t0 = pl.multiple_of(qb * tq, tq)
            qs = _stack_heads(q_ref[pl.ds(t0, tq), :], head0, _SCALE)
            dos = _stack_heads(do_ref[pl.ds(t0, tq), :], head0)
            tot = tot_ref[pl.ds(t0, tq), :]
            swapped = pltpu.roll(tot, HEAD_DIM, 1)
            tts = (jnp.where(head0, tot, swapped), jnp.where(head0, swapped, tot))
            diag = (t0 + tq - 1) // kw

            def kblock(kb, carry, masked):
                dq, pfs, efs = carry
                s0 = pl.multiple_of(kb * kw, kw)
                kf = k_ref[pl.ds(s0, kw), :]
                k = kf.astype(BF16)
                khs = _heads(kf, head0)
                v = v_ref[pl.ds(s0, kw), :].astype(BF16)
                mask = _causal_mask(t0, s0, tq, kw, False)[:tq] if masked else None
                zs = [_dot(qs[h * tq:(h + 1) * tq], k, _NT) for h in range(2)]
                dws = [_dot(dos[h * tq:(h + 1) * tq], v, _NT) for h in range(2)]
                first = []
                for h in range(2):
                    lsz, lr = _sb_terms_z(zs[h], None)
                    lrm = jnp.where(mask, lr, 0.0) if masked else lr
                    first.append((lsz, lr) + _block_cumsum(lrm, lower_incl, ksub))
                second, pfs_out = [], []
                for h in range(2):
                    lsz, lr, incl, total = first[h]
                    pf = pfs[h]
                    ws, ews = [], []
                    for j in range(ksub):
                        w = jnp.exp(_sub(lsz, j) + (tts[h] - pf - incl[j]))
                        if masked:
                            w = jnp.where(_sub(mask, j), w, 0.0)
                        pf = pf + total[j]
                        ws.append(w.astype(BF16))
                        ews.append(_sub(dws[h], j) * w)
                    pfs_out.append(pf)
                    second.append((ws, ews) + _block_cumsum(jnp.concatenate(ews, axis=1), lower_strict, ksub))
                dz_h, efs_out = [], []
                for h in range(2):
                    lsz, lr = first[h][:2]
                    ws, ews, before, etotal = second[h]
                    ef = efs[h]
                    dzs = []
                    for j in range(ksub):
                        dz = ews[j] * jnp.exp(_sub(lr, j)) - (ef + before[j]) * jnp.exp(_sub(lsz, j))
                        ef = ef + etotal[j]
                        if masked:
                            dz = jnp.where(_sub(mask, j), dz, 0.0)
                        dzs.append(dz.astype(BF16))
                    efs_out.append(ef)
                    dz_h.append(jnp.concatenate(dzs, axis=1))
                    dq = dq + _dot(dz_h[h], khs[h])
                w = jnp.concatenate([jnp.concatenate(second[h][0], axis=1) for h in range(2)], axis=0)
                dk_ref[pl.ds(s0, kw), :] += _dot(jnp.concatenate(dz_h, axis=0), qs, _TN)
                dv_ref[pl.ds(s0, kw), :] += _dot(w, dos, _TN)
                return dq, tuple(pfs_out), tuple(efs_out)

            zc = jnp.zeros((tq, 1), F32)
            carry = lax.fori_loop(0, diag, lambda kb, cr: kblock(kb, cr, False), (jnp.zeros((tq, BLK), F32), (zc, zc), (zc, zc)))
            dq_ref[pl.ds(t0, tq), :] = kblock(diag, carry, True)[0] * _SCALE
            return c

        lax.fori_loop(0, nq, qblock, 0)

    t = nb * seq
    return pl.pallas_call(
        body, name=name, grid=(nb, PAIRS),
        in_specs=[_pair_spec(seq, 0), _pair_spec(seq, PAIRS), _pair_spec(seq, 2 * PAIRS), _pair_spec(seq, 0), _stat_spec(seq)],
        out_specs=[_pair_spec(seq, 0)] * 3,
        out_shape=[jax.ShapeDtypeStruct((t, D_MODEL), F32)] * 3,
        **_ATT_PARAMS,
    )(p, p, p, do, tot)


def _fox_cum(f, bf, nb, seq, *, name):
    def body(f_ref, bf_ref, cc_ref, cr_ref):
        row, col = _iotas()
        lower = (col <= row).astype(BF16)
        carry = jnp.zeros((1, BLK), F32)
        for blk in range(seq // BLK):
            rs = slice(blk * BLK, (blk + 1) * BLK)
            lf = jnp.where(col < HEADS, _log_sigmoid(f_ref[rs, :] + bf_ref[...]), 0.0)
            cc = _dot3_left(lower, lf) + carry
            cc_ref[rs, :] = cc
            cr_ref[:, rs] = cc.T[0:HEADS, :]
            carry = carry + _colsum(lf)

    return pl.pallas_call(
        body, name=name, grid=(nb,),
        in_specs=[pl.BlockSpec((seq, BLK), lambda b: (b, 0)), pl.BlockSpec((1, BLK), lambda b: (0, 0))],
        out_specs=[pl.BlockSpec((seq, BLK), lambda b: (b, 0)), pl.BlockSpec((None, HEADS, seq), lambda b: (b, 0, 0))],
        out_shape=[jax.ShapeDtypeStruct((nb * seq, BLK), F32), jax.ShapeDtypeStruct((nb, HEADS, seq), F32)],
        compiler_params=pltpu.CompilerParams(dimension_semantics=("parallel",)),
    )(f, bf)


def _fox_cum_bwd(dcr, dcc, f, bf, nb, seq, *, name):
    def body(dcr_ref, dcc_ref, f_ref, bf_ref, df_ref, dbf_ref):
        row, col = _iotas()
        upper_incl = (col >= row).astype(BF16)

        @pl.when(pl.program_id(0) == 0)
        def _():
            dbf_ref[...] = jnp.zeros((1, BLK), F32)

        carry = jnp.zeros((1, BLK), F32)
        for blk in reversed(range(seq // BLK)):
            rs = slice(blk * BLK, (blk + 1) * BLK)
            dc = dcr_ref[:, rs].T + dcc_ref[rs, :]
            dlf = _dot3_left(upper_incl, dc) + carry
            carry = carry + _colsum(dc)
            fl = f_ref[rs, :] + bf_ref[...]
            df = jnp.where(col < HEADS, dlf * _sigmoid(-fl), 0.0)
            df_ref[rs, :] = df
            dbf_ref[...] += _colsum(df)

    return pl.pallas_call(
        body, name=name, grid=(nb,),
        in_specs=[pl.BlockSpec((None, BLK, seq), lambda b: (b, 0, 0)), pl.BlockSpec((seq, BLK), lambda b: (b, 0)),
                  pl.BlockSpec((seq, BLK), lambda b: (b, 0)), pl.BlockSpec((1, BLK), lambda b: (0, 0))],
        out_specs=[pl.BlockSpec((seq, BLK), lambda b: (b, 0)), pl.BlockSpec((1, BLK), lambda b: (0, 0))],
        out_shape=[jax.ShapeDtypeStruct((nb * seq, BLK), F32), jax.ShapeDtypeStruct((1, BLK), F32)],
        compiler_params=pltpu.CompilerParams(dimension_semantics=("arbitrary",)),
    )(dcr, dcc, f, bf)


def _fox_cum_cols(cc_ref, t0, tq, colq, hp):
    cc = cc_ref[pl.ds(t0, tq), :]
    c0 = jnp.sum(jnp.where(colq == 2 * hp, cc, 0.0), axis=1, keepdims=True)
    c1 = jnp.sum(jnp.where(colq == 2 * hp + 1, cc, 0.0), axis=1, keepdims=True)
    return c0, c1


def _fox_fwd(p, cc, cr, nb, seq, *, name):
    tq = min(FOX_BLOCK, seq)
    nq = seq // tq
    kw = min(FOX_BLOCK, seq)
    ksub = kw // BLK

    def body(q_ref, k_ref, v_ref, cc_ref, cr_ref, o_ref, lse_ref):
        hp = pl.program_id(1)
        row, col = _iotas()
        colq = lax.broadcasted_iota(jnp.int32, (tq, BLK), 1)
        head0 = colq < HEAD_DIM

        def qblock(qb, c):
            t0 = pl.multiple_of(qb * tq, tq)
            qs = _stack_heads(q_ref[pl.ds(t0, tq), :], head0, _SCALE)
            c0, c1 = _fox_cum_cols(cc_ref, t0, tq, colq, hp)
            diag = (t0 + tq - 1) // kw

            def kblock(kb, carry, masked):
                accs, ms = carry
                s0 = pl.multiple_of(kb * kw, kw)
                k = k_ref[pl.ds(s0, kw), :].astype(BF16)
                vf = v_ref[pl.ds(s0, kw), :]
                own0 = lax.broadcasted_iota(jnp.int32, vf.shape, 1) < HEAD_DIM
                vs = (jnp.where(own0, vf, 1.0).astype(BF16), jnp.where(own0, 1.0, vf).astype(BF16))
                mask = _causal_mask(t0, s0, tq, kw, True)[:tq] if masked else None
                zs = [_dot(qs[h * tq:(h + 1) * tq], k, _NT) for h in range(2)]
                parts = []
                for h, ch in enumerate((c0, c1)):
                    s = zs[h] + (ch - cr_ref[h:h + 1, pl.ds(s0, kw)])
                    if masked:
                        s = jnp.where(mask, s, -jnp.inf)
                    m_new = jnp.maximum(ms[h], jnp.max(s, axis=1, keepdims=True))
                    parts.append((jnp.exp(s - m_new).astype(BF16), jnp.exp(ms[h] - m_new), m_new))
                return (tuple(accs[h] * parts[h][1] + _dot(parts[h][0], vs[h]) for h in range(2)),
                        tuple(parts[h][2] for h in range(2)))

            zeros, ninf = jnp.zeros((tq, BLK), F32), jnp.full((tq, 1), -jnp.inf, F32)
            carry = lax.fori_loop(0, diag, lambda kb, cr: kblock(kb, cr, False), ((zeros, zeros), (ninf, ninf)))
            (acc0, acc1), (m0, m1) = kblock(diag, carry, True)
            l = jnp.where(head0, pltpu.roll(acc0, HEAD_DIM, 1), pltpu.roll(acc1, HEAD_DIM, 1))
            o_ref[pl.ds(t0, tq), :] = jnp.where(head0, acc0, acc1) / l
            lse_ref[pl.ds(t0, tq), :] = jnp.where(head0, m0, m1) + jnp.log(l)
            return c

        lax.fori_loop(0, nq, qblock, 0)

    return pl.pallas_call(
        body, name=name, grid=(nb, PAIRS),
        in_specs=[_pair_spec(seq, 0), _pair_spec(seq, PAIRS), _pair_spec(seq, 2 * PAIRS),
                  pl.BlockSpec((seq, BLK), lambda b, hp: (b, 0)), pl.BlockSpec((None, None, SUBLANES, seq), lambda b, hp: (b, hp, 0, 0))],
        out_specs=[_pair_spec(seq, 0), _stat_spec(seq)],
        out_shape=[jax.ShapeDtypeStruct((nb * seq, D_MODEL), F32), jax.ShapeDtypeStruct((nb, PAIRS, seq, BLK), F32)],
        **_ATT_PARAMS,
    )(p, p, p, cc, cr)


def _fox_bwd(p, do, o, lse, cc, cr, nb, seq, *, name):
    tq = min(FOX_BLOCK, seq)
    nq = seq // tq
    kw = min(FOX_BLOCK, seq)
    ksub = kw // BLK

    def body(q_ref, k_ref, v_ref, do_ref, o_ref, lse_ref, cc_ref, cr_ref, dq_ref, dk_ref, dv_ref, dcr_ref, dcc_ref):
        hp = pl.program_id(1)
        row, col = _iotas()
        colq = lax.broadcasted_iota(jnp.int32, (tq, BLK), 1)
        head0 = colq < HEAD_DIM
        dk_ref[...] = jnp.zeros(dk_ref.shape, F32)
        dv_ref[...] = jnp.zeros(dv_ref.shape, F32)
        dcr_ref[...] = jnp.zeros(dcr_ref.shape, F32)

        @pl.when(hp == 0)
        def _():
            dcc_ref[...] = jnp.zeros(dcc_ref.shape, F32)

        def qblock(qb, c):
            t0 = pl.multiple_of(qb * tq, tq)
            qs = _stack_heads(q_ref[pl.ds(t0, tq), :], head0, _SCALE)
            dof = do_ref[pl.ds(t0, tq), :]
            dos = _stack_heads(dof, head0)
            prod = dof * o_ref[pl.ds(t0, tq), :]
            dl = jnp.concatenate([jnp.sum(jnp.where(head0, prod, 0.0), axis=1, keepdims=True),
                                  jnp.sum(jnp.where(head0, 0.0, prod), axis=1, keepdims=True)], axis=0)
            lse = _pair_cols(lse_ref[pl.ds(t0, tq), :], head0, -jnp.inf)
            c0, c1 = _fox_cum_cols(cc_ref, t0, tq, colq, hp)
            diag = (t0 + tq - 1) // kw

            def kblock(kb, carry, masked):
                dq, rs = carry
                s0 = pl.multiple_of(kb * kw, kw)
                kf = k_ref[pl.ds(s0, kw), :]
                k = kf.astype(BF16)
                k0, k1 = _heads(kf, head0)
                v = v_ref[pl.ds(s0, kw), :].astype(BF16)
                mask = _causal_mask(t0, s0, tq, kw, True)[:tq] if masked else None
                zs = [_dot(qs[h * tq:(h + 1) * tq], k, _NT) for h in range(2)]
                dps = [_dot(dos[h * tq:(h + 1) * tq], v, _NT) for h in range(2)]
                prs, dss, rss = [], [], []
                for h, (ch, kh) in enumerate(((c0, k0), (c1, k1))):
                    rows = slice(h * tq, (h + 1) * tq)
                    pr = jnp.exp(zs[h] + (ch - cr_ref[h:h + 1, pl.ds(s0, kw)]) - lse[rows])
                    if masked:
                        pr = jnp.where(mask, pr, 0.0)
                    ds = pr * (dps[h] - dl[rows])
                    dcr_ref[h:h + 1, pl.ds(s0, kw)] -= _colsum(ds)
                    rss.append(rs[rows] + jnp.sum(ds, axis=1, keepdims=True))
                    prs.append(pr.astype(BF16))
                    dss.append(ds.astype(BF16))
                    dq = dq + _dot(dss[h], kh)
                dk_ref[pl.ds(s0, kw), :] += _dot(jnp.concatenate(dss, axis=0), qs, _TN)
                dv_ref[pl.ds(s0, kw), :] += _dot(jnp.concatenate(prs, axis=0), dos, _TN)
                return dq, jnp.concatenate(rss, axis=0)

            init = (jnp.zeros((tq, BLK), F32), jnp.zeros((2 * tq, 1), F32))
            carry = lax.fori_loop(0, diag, lambda kb, cr: kblock(kb, cr, False), init)
            dq, rs = kblock(diag, carry, True)
            dq_ref[pl.ds(t0, tq), :] = dq * _SCALE
            dcc_ref[pl.ds(t0, tq), :] += jnp.where(colq == 2 * hp, rs[:tq], 0.0) + jnp.where(colq == 2 * hp + 1, rs[tq:], 0.0)
            return c

        lax.fori_loop(0, nq, qblock, 0)

    t = nb * seq
    return pl.pallas_call(
        body, name=name, grid=(nb, PAIRS),
        in_specs=[_pair_spec(seq, 0), _pair_spec(seq, PAIRS), _pair_spec(seq, 2 * PAIRS), _pair_spec(seq, 0), _pair_spec(seq, 0),
                  _stat_spec(seq), pl.BlockSpec((seq, BLK), lambda b, hp: (b, 0)),
                  pl.BlockSpec((None, None, SUBLANES, seq), lambda b, hp: (b, hp, 0, 0))],
        out_specs=[_pair_spec(seq, 0)] * 3 + [pl.BlockSpec((None, None, SUBLANES, seq), lambda b, hp: (b, hp, 0, 0)),
                                              pl.BlockSpec((seq, BLK), lambda b, hp: (b, 0))],
        out_shape=[jax.ShapeDtypeStruct((t, D_MODEL), F32)] * 3 + [jax.ShapeDtypeStruct((nb, PAIRS, SUBLANES, seq), F32),
                                                                     jax.ShapeDtypeStruct((t, BLK), F32)],
        compiler_params=pltpu.CompilerParams(dimension_semantics=("parallel", "arbitrary")),
    )(p, p, p, do, o, lse, cc, cr)


def _row_shards(x):
    return x.reshape(N_CHIPS, x.shape[0] // N_CHIPS, x.shape[1])


def _local_step(x3, tgt3, w, later=None, start_reduce=None):
    nb, seq, d = x3.shape
    t = nb * seq
    x0, tgt = x3.reshape(t, d), tgt3.reshape(t, d)
    g = {}

    a_gain = w["a_norm"].reshape(1, d)
    h_a, ht_a = _rmsnorm(x0, a_gain, name="a_norm_fwd")
    p_a = _matmul(h_a, w["a_w_in"][:3], name="a_in_fwd", out_dtype=BF16)
    gate_a = _matmul(h_a, w["a_w_in"][3], name="a_in_gate_fwd")
    o_a, tot_a = _sb_fwd(p_a, nb, seq, name="a_attn_fwd")
    y_a, yt_a = _gate(o_a, gate_a, 0, name="a_gate_fwd")
    x1 = _matmul(y_a, w["a_w_out"], name="a_out_fwd", residual=x0)

    if later:
        w = {**w, **later[0](x1)}
    b_gain = w["b_norm"].reshape(1, d)
    b_lg, b_lb = w["b_v_ln_g"].reshape(1, GM_W), w["b_v_ln_b"].reshape(1, GM_W)
    b_ws, b_bst = w["b_w_s"].reshape(GM_G, BLK, BLK), w["b_b_s"].reshape(GM_G, BLK).T
    h_b, ht_b = _rmsnorm(x1, b_gain, name="b_norm_fwd")
    p_b = _matmul(h_b, w["b_w_in"], name="b_in_fwd")
    y_b, yt_b = _gmlp_fwd(p_b, b_lg, b_lb, b_ws, b_bst, name="b_mix_fwd")
    x2 = _matmul(y_b, w["b_w_out"], name="b_out_fwd", residual=x1)

    if later:
        w = {**w, **later[1](x2)}
    c_gain = w["c_norm"].reshape(1, d)
    c_cw = jnp.repeat(w["c_conv_w"].reshape(CV_K, CV_W), SUBLANES, axis=0)
    c_cb = w["c_conv_b"].reshape(1, CV_W)
    c_lg, c_lb = w["c_ln_g"].reshape(1, CV_W), w["c_ln_b"].reshape(1, CV_W)
    h_c, ht_c = _rmsnorm(x2, c_gain, name="c_norm_fwd")
    p_c = _matmul(h_c, w["c_w_in"], name="c_in_fwd")
    y_c, yt_c = _conv_fwd(p_c, c_cw, c_cb, c_lg, c_lb, seq, name="c_conv_fwd")
    x3_ = _matmul(y_c, w["c_w_out"], name="c_out_fwd", residual=x2)

    d_gain = w["d_norm"].reshape(1, d)
    d_win = w["d_w_in"].reshape(d, 4 * D_MODEL + HEADS)
    d_wmain = d_win[:, :4 * D_MODEL]
    d_wf = jnp.pad(d_win[:, 4 * D_MODEL:], ((0, 0), (0, BLK - HEADS)))
    d_bf = jnp.pad(w["d_b_f"].reshape(1, HEADS), ((0, 0), (0, BLK - HEADS)))
    h_d, ht_d = _rmsnorm(x3_, d_gain, name="d_norm_fwd")
    p_d = _matmul(h_d, d_wmain[:, :3 * D_MODEL], name="d_in_fwd", out_dtype=BF16)
    gate_d = _matmul(h_d, d_wmain[:, 3 * D_MODEL:], name="d_in_gate_fwd")
    f_d = _matmul(h_d, d_wf, name="d_inf_fwd")
    cc, cr = _fox_cum(f_d, d_bf, nb, seq, name="d_cum_fwd")
    cr = jnp.pad(cr.reshape(nb, PAIRS, 2, seq), ((0, 0), (0, 0), (0, SUBLANES - 2), (0, 0)))
    o_d, lse_d = _fox_fwd(p_d, cc, cr, nb, seq, name="d_attn_fwd")
    y_d, yt_d = _gate(o_d, gate_d, 0, name="d_gate_fwd")
    x4 = _matmul(y_d, w["d_w_out"], name="d_out_fwd", residual=x3_)

    f_gain = w["final_norm"].reshape(1, d)
    dx, g_fn, loss_row = _loss_head(x4, f_gain, tgt, name="loss_head")
    g["final_norm"] = g_fn

    g["d_w_out"] = _row_shards(_matmul(yt_d, dx, name="d_out_dw"))
    dy = _matmul(dx, w["d_w_out"], name="d_out_dy", mode="nt")
    do_d, dg_d = _gate_bwd(dy, o_d, gate_d, 0, name="d_gate_bwd")
    dq, dk, dv, dcr, dcc = _fox_bwd(p_d, do_d, o_d, lse_d, cc, cr, nb, seq, name="d_attn_bwd")
    dcr = jnp.pad(dcr[:, :, :2, :].reshape(nb, HEADS, seq), ((0, 0), (0, BLK - HEADS), (0, 0)))
    df, dbf = _fox_cum_bwd(dcr, dcc, f_d, d_bf, nb, seq, name="d_cum_bwd")
    g["d_b_f"] = dbf[:, :HEADS]
    parts = [dq, dk, dv, dg_d]
    dws = _matmul_parts(ht_d, parts, name="d_in_dw")
    dwf = _matmul(ht_d, df, name="d_inf_dw")
    g["d_w_in"] = jnp.concatenate([dws[n] for n in range(4)] + [dwf[:, :HEADS]], axis=1).reshape(d, N_CHIPS, -1).transpose(1, 0, 2)
    dh = _matmul_sum([(df, d_wf)] + [(pt, d_wmain[:, n * D_MODEL:(n + 1) * D_MODEL]) for n, pt in enumerate(parts)],
                     name="d_in_dh")
    dx, g["d_norm"] = _rmsnorm_bwd(dh, x3_, d_gain, dx, name="d_norm_bwd")

    g["c_w_out"] = _row_shards(_matmul(yt_c, dx, name="c_out_dw"))
    dy = _matmul(dx, w["c_w_out"], name="c_out_dy", mode="nt")
    dy1, dgate, g["c_ln_g"], g["c_ln_b"], g["c_conv_b"], g["c_conv_w"] = _conv_bwd_post(
        dy, p_c, c_cw, c_cb, c_lg, c_lb, seq, name="c_conv_bwd_post")
    dp = _conv_bwd_pre(dy1, dgate, p_c, c_cw, seq, name="c_conv_bwd_pre")
    g["c_w_in"] = _matmul(ht_c, dp, name="c_in_dw", out_shards=N_CHIPS)
    dh = _matmul(dp, w["c_w_in"], name="c_in_dh", mode="nt")
    dx, g["c_norm"] = _rmsnorm_bwd(dh, x2, c_gain, dx, name="c_norm_bwd")

    early, b_wout, a_wout = [], w["b_w_out"], w["a_w_out"]
    if start_reduce is not None:
        begun, token = start_reduce({n: g[n] for n in ("d_w_in", "d_w_out", "c_w_in", "c_w_out")}, "grads_cd")
        early.append(begun)
        b_wout = b_wout + token[0, 0].astype(b_wout.dtype)
    g["b_w_out"] = _row_shards(_matmul(yt_b, dx, name="b_out_dw"))
    dy = _matmul(dx, b_wout, name="b_out_dy", mode="nt")
    dp, g["b_v_ln_g"], g["b_v_ln_b"], g["b_w_s"], dbst = _gmlp_bwd(dy, p_b, b_lg, b_lb, b_ws, b_bst, name="b_mix_bwd")
    g["b_b_s"] = dbst.T
    g["b_w_in"] = _matmul(ht_b, dp, name="b_in_dw", out_shards=N_CHIPS)
    dh = _matmul(dp, w["b_w_in"], name="b_in_dh", mode="nt")
    dx, g["b_norm"] = _rmsnorm_bwd(dh, x1, b_gain, dx, name="b_norm_bwd")

    if start_reduce is not None:
        begun, token = start_reduce({n: g[n] for n in ("b_w_in", "b_w_out")}, "grads_b")
        early.append(begun)
        a_wout = a_wout + token[0, 0].astype(a_wout.dtype)
    g["a_w_out"] = _row_shards(_matmul(yt_a, dx, name="a_out_dw"))
    dy = _matmul(dx, a_wout, name="a_out_dy", mode="nt")
    do_a, dg_a = _gate_bwd(dy, o_a, gate_a, 0, name="a_gate_bwd")
    dq, dk, dv = _sb_bwd(p_a, do_a, tot_a, nb, seq, name="a_attn_bwd")
    parts = [dq, dk, dv, dg_a]
    g["a_w_in"] = _matmul_parts(ht_a, parts, name="a_in_dw")
    dh = _matmul_sum([(pt, w["a_w_in"][n]) for n, pt in enumerate(parts)], name="a_in_dh")
    dx, g["a_norm"] = _rmsnorm_bwd(dh, x0, a_gain, dx, name="a_norm_bwd")

    return loss_row[0, 0], dx.reshape(nb, seq, d), g, early


_HBM = pl.BlockSpec(memory_space=pltpu.HBM)


def _place():
    return lax.axis_index("x"), lax.axis_index("y"), lax.axis_index("c")


def _other_chips(x, y):
    return [(1 - x, y), (x, 1 - y), (1 - x, 1 - y)]


def _allgather_chips(ss, *, name):
    n_ops = len(ss)

    def body(*refs):
        s_refs, o_refs, (send_sems, recv_sems) = refs[:n_ops], refs[n_ops:2 * n_ops], refs[2 * n_ops:]
        x, y, c = _place()
        me = 2 * x + y
        chips = _other_chips(x, y)

        def copy(i, kk, src, dst, to):
            return pltpu.make_async_remote_copy(src_ref=src, dst_ref=dst, send_sem=send_sems.at[6 * i + kk],
                                                recv_sem=recv_sems.at[6 * i + kk], device_id=to, device_id_type=MESH)

        def half(i, j, hc):
            h = s_refs[i].shape[0] // 2
            return o_refs[i].at[j, pl.ds(hc * h, h), :]

        first = [copy(i, kk, s_refs[i].at[pl.ds(c * (s_refs[i].shape[0] // 2), s_refs[i].shape[0] // 2), :], half(i, me, c),
                      (cx, cy, c)) for kk, (cx, cy) in enumerate(chips) for i in range(n_ops)]
        for cp in first:
            cp.start()
        passed = []
        for kk, (cx, cy) in enumerate(chips):
            for i in range(n_ops):
                blk = half(i, 2 * cx + cy, c)
                copy(i, kk, blk, blk, (cx, cy, c)).wait_recv()
                fwd = copy(i, 3 + kk, blk, blk, (x, y, 1 - c))
                fwd.start()
                passed.append(fwd)
        for kk, (cx, cy) in enumerate(chips):
            for i in range(n_ops):
                blk = half(i, 2 * cx + cy, 1 - c)
                copy(i, 3 + kk, blk, blk, (x, y, 1 - c)).wait_recv()
        for cp in first + passed:
            cp.wait_send()

    for s in ss:
        assert s.shape[0] % 32 == 0, s.shape
    return pl.pallas_call(
        body, name=name, in_specs=[_HBM] * n_ops, out_specs=[_HBM] * n_ops,
        out_shape=[jax.ShapeDtypeStruct((N_CHIPS,) + s.shape, s.dtype) for s in ss],
        scratch_shapes=[pltpu.SemaphoreType.DMA((6 * n_ops,)), pltpu.SemaphoreType.DMA((6 * n_ops,))],
    )(*ss)


_SEM = pl.BlockSpec(memory_space=pltpu.SEMAPHORE)
_ANY = pl.BlockSpec(memory_space=pl.ANY)
_DATAFLOW = pltpu.SideEffectType.DATAFLOW_SIDE_EFFECTING


def _chip_copies(s_refs, land_refs, send_sems, recv_sems):
    x, y, c = _place()
    me = 2 * x + y
    cps = []
    for i, (s_ref, land_ref) in enumerate(zip(s_refs, land_refs)):
        h = s_ref.shape[0] // 2
        for kk, (cx, cy) in enumerate(_other_chips(x, y)):
            cps.append(pltpu.make_async_remote_copy(
                src_ref=s_ref.at[pl.ds(c * h, h), :], dst_ref=land_ref.at[me, pl.ds(c * h, h), :], send_sem=send_sems.at[3 * i + kk],
                recv_sem=recv_sems.at[3 * i + kk], device_id=(cx, cy, c), device_id_type=MESH))
    return cps


def _gather_start(ss, after, *, name):
    n = len(ss)
    lands = [lax.empty((N_CHIPS,) + s.shape, s.dtype) for s in ss]

    def body(*refs):
        s_refs, land_refs = refs[:n], refs[n:2 * n]
        send_sems, recv_sems = refs[2 * n + 1], refs[2 * n + 2]
        token = refs[-1]
        for cp in _chip_copies(s_refs, land_refs, send_sems, recv_sems):
            cp.start()
        token[...] = jnp.zeros(token.shape, token.dtype)

    hbm = [pltpu.HBM(a.shape, a.dtype) for a in list(ss) + lands]
    res = pl.pallas_call(
        body, name=name,
        out_shape=(pltpu.SemaphoreType.DMA((3 * n,)), pltpu.SemaphoreType.DMA((3 * n,)), *hbm, jax.ShapeDtypeStruct((8, BLK), F32)),
        in_specs=[_HBM] * (2 * n) + [_ANY],
        out_specs=(_SEM, _SEM, *([_HBM] * (2 * n)), pl.BlockSpec(memory_space=pltpu.VMEM)),
        input_output_aliases={i: 2 + i for i in range(2 * n)},
        compiler_params=pltpu.CompilerParams(has_side_effects=_DATAFLOW),
    )(*[pltpu.with_memory_space_constraint(a, pltpu.HBM) for a in list(ss) + lands], after)
    return res[:-1], res[-1]


def _gather_wait(started, after, *, name):
    send_sems, recv_sems = started[0], started[1]
    n = (len(started) - 2) // 2

    def body(*refs):
        s_refs, land_refs = refs[:n], refs[n:2 * n]
        for cp in _chip_copies(s_refs, land_refs, refs[2 * n], refs[2 * n + 1]):
            cp.wait_send()
            cp.wait_recv()

    res = pl.pallas_call(
        body, name=name, out_shape=tuple(pltpu.HBM(a.shape, a.dtype) for a in started[2:]),
        in_specs=[_HBM] * (2 * n) + [_SEM, _SEM, _ANY], out_specs=tuple([_HBM] * (2 * n)),
        input_output_aliases={i: i for i in range(2 * n)},
        compiler_params=pltpu.CompilerParams(has_side_effects=_DATAFLOW),
    )(*started[2:], send_sems, recv_sems, after)
    return list(res[n:])


def _sibling_exchange(lands, *, name):
    n = len(lands)

    def body(*refs):
        o_refs, (send_sems, recv_sems) = refs[n:2 * n], refs[2 * n:]
        x, y, c = _place()
        cps = []
        for i, o_ref in enumerate(o_refs):
            h = o_ref.shape[1] // 2
            for kk, (cx, cy) in enumerate(_other_chips(x, y)):
                def half(hc):
                    return o_ref.at[2 * cx + cy, pl.ds(hc * h, h), :]
                sent = pltpu.make_async_remote_copy(src_ref=half(c), dst_ref=half(c), send_sem=send_sems.at[3 * i + kk],
                                                    recv_sem=recv_sems.at[3 * i + kk], device_id=(x, y, 1 - c), device_id_type=MESH)
                awaited = pltpu.make_async_remote_copy(src_ref=half(1 - c), dst_ref=half(1 - c), send_sem=send_sems.at[3 * i + kk],
                                                       recv_sem=recv_sems.at[3 * i + kk], device_id=(x, y, 1 - c),
                                                       device_id_type=MESH)
                cps.append((sent, awaited))
        for sent, _ in cps:
            sent.start()
        for sent, awaited in cps:
            awaited.wait_recv()
            sent.wait_send()

    return pl.pallas_call(
        body, name=name, in_specs=[_HBM] * n, out_specs=[_HBM] * n,
        out_shape=[jax.ShapeDtypeStruct(a.shape, a.dtype) for a in lands], scratch_shapes=_dma_sems(3 * n),
        input_output_aliases={i: i for i in range(n)},
    )(*lands)


def _own_block(gathered, s):
    me = 2 * lax.axis_index("x") + lax.axis_index("y")
    return lax.dynamic_update_slice(gathered, s[None], (me,) + (0,) * s.ndim)


def _dma_sems(n):
    return [pltpu.SemaphoreType.DMA((n,)), pltpu.SemaphoreType.DMA((n,))]


def _swap_halves(gps, *, name):
    n_ops = len(gps)

    def body(*refs):
        g_refs, o_refs, (send_sems, recv_sems) = refs[:n_ops], refs[n_ops:2 * n_ops], refs[2 * n_ops:]
        x, y, c = _place()
        cps = []
        for i, (g_ref, o_ref) in enumerate(zip(g_refs, o_refs)):
            h = g_ref.shape[1] // 2
            cps.append(pltpu.make_async_remote_copy(
                src_ref=g_ref.at[:, pl.ds((1 - c) * h, h), :], dst_ref=o_ref, send_sem=send_sems.at[i], recv_sem=recv_sems.at[i],
                device_id=(x, y, 1 - c), device_id_type=MESH))
        for cp in cps:
            cp.start()
        for cp in cps:
            cp.wait()

    return pl.pallas_call(
        body, name=name, in_specs=[_HBM] * n_ops, out_specs=[_HBM] * n_ops,
        out_shape=[jax.ShapeDtypeStruct((g.shape[0], g.shape[1] // 2, g.shape[2]), g.dtype) for g in gps],
        scratch_shapes=_dma_sems(n_ops),
    )(*gps)


def _scatter_chips(hps, *, name):
    n_ops = len(hps)

    def body(*refs):
        h_refs, o_refs, (send_sems, recv_sems) = refs[:n_ops], refs[n_ops:2 * n_ops], refs[2 * n_ops:]
        x, y, c = _place()
        cps = [pltpu.make_async_remote_copy(src_ref=h_ref.at[2 * cx + cy], dst_ref=o_ref.at[kk], send_sem=send_sems.at[3 * i + kk],
                                            recv_sem=recv_sems.at[3 * i + kk], device_id=(cx, cy, c), device_id_type=MESH)
               for i, (h_ref, o_ref) in enumerate(zip(h_refs, o_refs)) for kk, (cx, cy) in enumerate(_other_chips(x, y))]
        for cp in cps:
            cp.start()
        for cp in cps:
            cp.wait()

    return pl.pallas_call(
        body, name=name, in_specs=[_HBM] * n_ops, out_specs=[_HBM] * n_ops,
        out_shape=[jax.ShapeDtypeStruct((3,) + hp.shape[1:], hp.dtype) for hp in hps],
        scratch_shapes=_dma_sems(3 * n_ops),
    )(*hps)


def _join_halves(fs, *, name):
    n_ops = len(fs)

    def body(*refs):
        f_refs, o_refs, (send_sems, recv_sems) = refs[:n_ops], refs[n_ops:2 * n_ops], refs[2 * n_ops:]
        x, y, c = _place()
        cps = [pltpu.make_async_remote_copy(src_ref=f_ref, dst_ref=o_ref, send_sem=send_sems.at[i], recv_sem=recv_sems.at[i],
                                            device_id=(x, y, 1 - c), device_id_type=MESH)
               for i, (f_ref, o_ref) in enumerate(zip(f_refs, o_refs))]
        for cp in cps:
            cp.start()
        for cp in cps:
            cp.wait()

    return pl.pallas_call(
        body, name=name, in_specs=[_HBM] * n_ops, out_specs=[_HBM] * n_ops,
        out_shape=[jax.ShapeDtypeStruct(f.shape, f.dtype) for f in fs], scratch_shapes=_dma_sems(n_ops),
    )(*fs)


def _stitch(mine, theirs):
    south = lax.axis_index("c") == 0
    return jnp.concatenate([jnp.where(south, mine, theirs), jnp.where(south, theirs, mine)], axis=0)


def _add_halves(gp, ra, wire_dtype, *, name, bm=256):
    n, r, c_ = gp.shape
    h = r // 2
    bm = _tile(h, bm)
    per = h // bm
    c = lax.axis_index("c").astype(jnp.int32).reshape(1)

    def body(c_ref, g_ref, ra_ref, o_ref, ow_ref):
        s = g_ref[...] + ra_ref[...]
        o_ref[...] = s
        ow_ref[...] = s.astype(wire_dtype)

    mine = pl.BlockSpec((None, bm, c_), lambda j, i, cr: (j, i, 0))
    return pl.pallas_call(
        body, name=name,
        grid_spec=pltpu.PrefetchScalarGridSpec(
            num_scalar_prefetch=1, grid=(n, per),
            in_specs=[pl.BlockSpec((None, bm, c_), lambda j, i, cr: (j, cr[0] * per + i, 0)), mine],
            out_specs=[mine, mine]),
        out_shape=[jax.ShapeDtypeStruct((n, h, c_), F32), jax.ShapeDtypeStruct((n, h, c_), wire_dtype)],
        compiler_params=pltpu.CompilerParams(dimension_semantics=("parallel", "parallel")),
    )(c, gp, ra)


def _add_chips(hp, rb, *, name, bm=256):
    n, h, c_ = hp.shape
    bm = _tile(h, bm)
    me = (2 * lax.axis_index("x") + lax.axis_index("y")).astype(jnp.int32).reshape(1)

    def body(me_ref, h_ref, rb_ref, o_ref):
        o_ref[...] = ((h_ref[...] + rb_ref[0].astype(F32)) + rb_ref[1].astype(F32)) + rb_ref[2].astype(F32)

    return pl.pallas_call(
        body, name=name,
        grid_spec=pltpu.PrefetchScalarGridSpec(
            num_scalar_prefetch=1, grid=(h // bm,),
            in_specs=[pl.BlockSpec((None, bm, c_), lambda i, mr: (mr[0], i, 0)),
                      pl.BlockSpec((3, bm, c_), lambda i, mr: (0, i, 0))],
            out_specs=pl.BlockSpec((bm, c_), lambda i, mr: (i, 0))),
        out_shape=jax.ShapeDtypeStruct((h, c_), F32),
        compiler_params=pltpu.CompilerParams(dimension_semantics=("parallel",)),
    )(me, hp, rb)


def _scatter_copies(h_refs, land_refs, send_sems, recv_sems):
    x, y, c = _place()
    return [pltpu.make_async_remote_copy(src_ref=h_ref.at[2 * cx + cy], dst_ref=land_ref.at[kk], send_sem=send_sems.at[3 * i + kk],
                                         recv_sem=recv_sems.at[3 * i + kk], device_id=(cx, cy, c), device_id_type=MESH)
            for i, (h_ref, land_ref) in enumerate(zip(h_refs, land_refs)) for kk, (cx, cy) in enumerate(_other_chips(x, y))]


def _scatter_start(hps, after, *, name):
    n = len(hps)
    lands = [lax.empty((3,) + hp.shape[1:], hp.dtype) for hp in hps]

    def body(*refs):
        for cp in _scatter_copies(refs[:n], refs[n:2 * n], refs[2 * n + 1], refs[2 * n + 2]):
            cp.start()
        refs[-1][...] = jnp.zeros(refs[-1].shape, refs[-1].dtype)

    hbm = [pltpu.HBM(a.shape, a.dtype) for a in list(hps) + lands]
    res = pl.pallas_call(
        body, name=name,
        out_shape=(pltpu.SemaphoreType.DMA((3 * n,)), pltpu.SemaphoreType.DMA((3 * n,)), *hbm, jax.ShapeDtypeStruct((8, BLK), F32)),
        in_specs=[_HBM] * (2 * n) + [_ANY],
        out_specs=(_SEM, _SEM, *([_HBM] * (2 * n)), pl.BlockSpec(memory_space=pltpu.VMEM)),
        input_output_aliases={i: 2 + i for i in range(2 * n)},
        compiler_params=pltpu.CompilerParams(has_side_effects=_DATAFLOW),
    )(*[pltpu.with_memory_space_constraint(a, pltpu.HBM) for a in list(hps) + lands], after)
    return res[:-1], res[-1]


def _scatter_wait(started, after, *, name):
    n = (len(started) - 2) // 2

    def body(*refs):
        for cp in _scatter_copies(refs[:n], refs[n:2 * n], refs[2 * n], refs[2 * n + 1]):
            cp.wait_send()
            cp.wait_recv()

    res = pl.pallas_call(
        body, name=name, out_shape=tuple(pltpu.HBM(a.shape, a.dtype) for a in started[2:]),
        in_specs=[_HBM] * (2 * n) + [_SEM, _SEM, _ANY], out_specs=tuple([_HBM] * (2 * n)),
        input_output_aliases={i: i for i in range(2 * n)},
        compiler_params=pltpu.CompilerParams(has_side_effects=_DATAFLOW),
    )(*started[2:], started[0], started[1], after)
    return list(res[n:])


def _reduce_to_chips(gps, wire_dtypes, *, tag):
    ras = _swap_halves(gps, name=f"{tag}_swap_halves")
    return [_add_halves(gp, ra, wd, name=f"{tag}_add_halves{i}") for i, (gp, ra, wd) in enumerate(zip(gps, ras, wire_dtypes))]


def _start_reduce(early, tag):
    names = list(early)
    hps = _reduce_to_chips([early[n] for n in names], [BF16] * len(names), tag=tag)
    started, token = _scatter_start([hw for _, hw in hps], hps[-1][1], name=f"{tag}_scatter_start")
    return (tag, names, [hf for hf, _ in hps], started), token


def _adamw_math(w_ref, g_ref, m_ref, v_ref, d_ref, nm_ref, nv_ref):
    c1 = 1.0 - ADAM_B1 ** ADAM_STEP
    c2 = 1.0 - ADAM_B2 ** ADAM_STEP
    g_ = g_ref[...]
    m_ = ADAM_B1 * m_ref[...] + (1.0 - ADAM_B1) * g_
    v_ = ADAM_B2 * v_ref[...] + (1.0 - ADAM_B2) * (g_ * g_)
    d_ref[...] = -ADAM_LR * ((m_ / c1) / (jnp.sqrt(v_ / c2) + ADAM_EPS) + ADAM_WD * w_ref[...])
    nm_ref[...] = m_
    nv_ref[...] = v_


def _adamw_many(groups, *, name):
    n = len(groups[0])
    flat = [a for grp in groups for a in grp]

    def body(*refs):
        ins, outs = refs[:4 * n], refs[4 * n:]
        for i in range(n):
            _adamw_math(ins[i], ins[n + i], ins[2 * n + i], ins[3 * n + i], outs[i], outs[n + i], outs[2 * n + i])

    vmem = pl.BlockSpec(memory_space=pltpu.VMEM)
    res = pl.pallas_call(
        body, name=name, in_specs=[vmem] * (4 * n), out_specs=[vmem] * (3 * n),
        out_shape=[jax.ShapeDtypeStruct(a.shape, F32) for _ in range(3) for a in groups[0]],
    )(*flat)
    return res[:n], res[n:2 * n], res[2 * n:]


def _adamw(w, g_mine, g_theirs, m, v, *, name):
    r, c_ = w.shape
    h = r // 2
    bm = _tile(h, 256)
    per = h // bm
    c = lax.axis_index("c").astype(jnp.int32).reshape(1)

    def body(c_ref, w_ref, f_ref, t_ref, m_ref, v_ref, g_ref, d_ref, nm_ref, nv_ref):
        first_half = pl.program_id(0) < per
        mine = jnp.where(jnp.where(first_half, c_ref[0] == 0, c_ref[0] == 1), 1.0, 0.0)
        g_ref[...] = mine * f_ref[...] + (1.0 - mine) * t_ref[...]
        _adamw_math(w_ref, g_ref, m_ref, v_ref, d_ref, nm_ref, nv_ref)

    full = pl.BlockSpec((bm, c_), lambda i, cr: (i, 0))
    half = pl.BlockSpec((bm, c_), lambda i, cr: (i % per, 0))
    return pl.pallas_call(
        body, name=name,
        grid_spec=pltpu.PrefetchScalarGridSpec(num_scalar_prefetch=1, grid=(r // bm,), in_specs=[full, half, half, full, full],
                                               out_specs=[full] * 4),
        out_shape=[jax.ShapeDtypeStruct((r, c_), F32)] * 4,
        compiler_params=pltpu.CompilerParams(dimension_semantics=("parallel",)),
    )(c, w, g_mine, g_theirs, m, v)


_WEIGHTS = ["a_norm", "a_w_in", "a_w_out", "b_norm", "b_w_in", "b_v_ln_g", "b_v_ln_b", "b_w_s", "b_b_s", "b_w_out",
            "c_norm", "c_w_in", "c_conv_w", "c_conv_b", "c_ln_g", "c_ln_b", "c_w_out", "d_norm", "d_w_in", "d_b_f",
            "d_w_out", "final_norm"]
_SHARD_AXIS = {"a_norm": None, "a_w_in": 2, "a_w_out": 1, "b_norm": 1, "b_w_in": 2, "b_v_ln_g": 1, "b_v_ln_b": 1, "b_w_s": None,
               "b_b_s": None, "b_w_out": 1, "c_norm": 1, "c_w_in": 2, "c_conv_w": 2, "c_conv_b": 1, "c_ln_g": 1, "c_ln_b": 1,
               "c_w_out": 1, "d_norm": 1, "d_w_in": 2, "d_b_f": None, "d_w_out": 1, "final_norm": None}
_BIG = ["a_w_in", "a_w_out", "b_w_in", "b_w_out", "c_w_in", "c_w_out", "d_w_in", "d_w_out"]
_GATHER_GROUPS = (("a_w_in", "a_w_out"), ("b_w_in", "b_w_out"), ("c_w_in", "c_w_out", "d_w_in", "d_w_out"))
_SMALL_SHARDED = [n for n in _WEIGHTS if _SHARD_AXIS[n] is not None and n not in _BIG]
_REPLICATED = [n for n in _WEIGHTS if _SHARD_AXIS[n] is None]
_ROW_ALIGN = 32


def _pack(pieces, dtype, align=_ROW_ALIGN):
    flat = jnp.concatenate([p.reshape(-1).astype(dtype) for p in pieces])
    unit = align * PACK_C
    total = -(-flat.shape[0] // unit) * unit
    return jnp.pad(flat, (0, total - flat.shape[0])).reshape(total // PACK_C, PACK_C)


def _unpack(flat, shapes):
    out, off = [], 0
    for s in shapes:
        n = math.prod(s)
        out.append(flat[off:off + n].reshape(s))
        off += n
    return out


def _full_shape(local_shape, axis):
    s = list(local_shape)
    if axis is not None:
        s[axis] *= N_CHIPS
    return tuple(s)


def _gather_weights(local):
    def whole(n, gt):
        if _SHARD_AXIS[n] == 1:
            return gt.reshape(-1, gt.shape[-1])
        if n == "d_w_in":
            return gt.transpose(1, 0, 2).reshape(gt.shape[1], -1)
        return gt

    full = {n: local[n][0] if n != "final_norm" else local[n] for n in _REPLICATED}
    first = list(_GATHER_GROUPS[0])
    mine = [local[n][0].astype(BF16) for n in first] + [_pack([local[n] for n in _SMALL_SHARDED], F32)]
    got = [_own_block(gt, s) for gt, s in zip(_allgather_chips(mine, name="gather_weights"), mine)]
    full.update({n: whole(n, gt) for n, gt in zip(first, got)})
    small = got[-1].reshape(N_CHIPS, -1)
    shards = [_unpack(small[j], [local[n].shape[1:] for n in _SMALL_SHARDED]) for j in range(N_CHIPS)]
    for i, n in enumerate(_SMALL_SHARDED):
        full[n] = jnp.concatenate([shards[j][i] for j in range(N_CHIPS)], axis=_SHARD_AXIS[n] - 1)

    def begin(k, after):
        shards_k = [local[n][0].astype(BF16) for n in _GATHER_GROUPS[k]]
        started, token = _gather_start(shards_k, after, name=f"gather{k}_start")
        return shards_k, started, token

    pending = [begin(1, got[0])]
    full["a_norm"] = full["a_norm"] + pending[0][2][0, 0]

    def finish(k):
        def weights(after):
            shards_k, started, _ = pending[k - 1]
            lands = _gather_wait(started, after, name=f"gather{k}_wait")
            token = None
            if k + 1 < len(_GATHER_GROUPS):
                pending.append(begin(k + 1, lands[0]))
                token = pending[k][2]
            lands = _sibling_exchange(lands, name=f"gather{k}_exchange")
            out = {n: whole(n, _own_block(gt, s)) for n, gt, s in zip(_GATHER_GROUPS[k], lands, shards_k)}
            if token is not None:
                gain = _GATHER_GROUPS[k][0][0] + "_norm"
                out[gain] = full[gain] + token[0, 0]
            return out
        return weights

    return full, [finish(k) for k in range(1, len(_GATHER_GROUPS))]


def _repl_piece_len(local):
    total = sum(math.prod(local[n].shape) for n in _REPLICATED)
    return -(-total // N_CHIPS)


def _reduce_grads(g, local, early):
    rep_flat = jnp.concatenate([g[n].reshape(-1) for n in _REPLICATED])
    piece = _repl_piece_len(local)
    rep_flat = jnp.pad(rep_flat, (0, N_CHIPS * piece - rep_flat.shape[0]))

    def shard(n, j):
        full = g[n].reshape(_full_shape(local[n].shape, _SHARD_AXIS[n]))
        width = local[n].shape[_SHARD_AXIS[n]]
        return lax.slice_in_dim(full, j * width, (j + 1) * width, axis=_SHARD_AXIS[n])

    small = jnp.stack([_pack([shard(n, j) for n in _SMALL_SHARDED] + [rep_flat[j * piece:(j + 1) * piece]], F32)
                       for j in range(N_CHIPS)])
    early_names = [n for _, names, _, _ in early for n in names]
    late = [n for n in _BIG if n not in early_names]
    hps = _reduce_to_chips([g[n] for n in late] + [small], [BF16] * len(late) + [F32], tag="grads")
    rbs = list(_scatter_chips([hw for _, hw in hps], name="grads_scatter_chips"))
    early_halves, early_rbs = [], []
    for tag, _, halves_k, started in early:
        early_halves += halves_k
        early_rbs += _scatter_wait(started, rbs[0], name=f"{tag}_scatter_wait")
    halves = early_halves + [hf for hf, _ in hps]
    fs = [_add_chips(hf, rb, name=f"grads_add_chips{i}") for i, (hf, rb) in enumerate(zip(halves, early_rbs + rbs))]
    theirs = _join_halves(fs, name="grads_join_halves")
    red = dict(zip(early_names + late, zip(fs, theirs)))
    out = _unpack(_stitch(fs[-1], theirs[-1]).reshape(-1), [local[n].shape for n in _SMALL_SHARDED] + [(piece,)])
    red.update(zip(_SMALL_SHARDED, out[:-1]))
    rep_mine = _pack([out[-1]], F32)
    rep = _own_block(_allgather_chips([rep_mine], name="gather_replicated_grads")[0], rep_mine)
    rep = rep.reshape(N_CHIPS, -1)[:, :piece].reshape(-1)
    for n, val in zip(_REPLICATED, _unpack(rep, [local[n].shape for n in _REPLICATED])):
        red[n] = val
    return red


def _update(local, grads, m, v):
    grads, delta, new_m, new_v = dict(grads), {}, {}, {}
    for n in _BIG:
        shp = local[n].shape
        two = (shp[-2], shp[-1])
        res = _adamw(local[n].reshape(two), *grads[n], m[n].reshape(two), v[n].reshape(two), name=f"adamw_{n}")
        grads[n], delta[n], new_m[n], new_v[n] = [r.reshape(shp) for r in res]
    small = [n for n in _WEIGHTS if n not in _BIG]
    two = {n: (math.prod(local[n].shape[:-1]), local[n].shape[-1]) for n in small}
    res = _adamw_many([[src[n].reshape(two[n]) for n in small] for src in (local, grads, m, v)], name="adamw_small")
    for dst, rs in zip((delta, new_m, new_v), res):
        for n, val in zip(small, rs):
            dst[n] = val.reshape(local[n].shape)
    return grads, delta, new_m, new_v


def kernel(x, a_norm, a_w_in, a_w_out, b_norm, b_w_in, b_v_ln_g, b_v_ln_b, b_w_s, b_b_s, b_w_out, c_norm, c_w_in, c_conv_w, c_conv_b, c_ln_g, c_ln_b, c_w_out, d_norm, d_w_in, d_b_f, d_w_out, final_norm, loss_target, m_a_norm, m_a_w_in, m_a_w_out, m_b_norm, m_b_w_in, m_b_v_ln_g, m_b_v_ln_b, m_b_w_s, m_b_b_s, m_b_w_out, m_c_norm, m_c_w_in, m_c_conv_w, m_c_conv_b, m_c_ln_g, m_c_ln_b, m_c_w_out, m_d_norm, m_d_w_in, m_d_b_f, m_d_w_out, m_final_norm, v_a_norm, v_a_w_in, v_a_w_out, v_b_norm, v_b_w_in, v_b_v_ln_g, v_b_v_ln_b, v_b_w_s, v_b_b_s, v_b_w_out, v_c_norm, v_c_w_in, v_c_conv_w, v_c_conv_b, v_c_ln_g, v_c_ln_b, v_c_w_out, v_d_norm, v_d_w_in, v_d_b_f, v_d_w_out, v_final_norm):
    local = dict(zip(_WEIGHTS, (a_norm, a_w_in, a_w_out, b_norm, b_w_in, b_v_ln_g, b_v_ln_b, b_w_s, b_b_s, b_w_out, c_norm, c_w_in,
                                c_conv_w, c_conv_b, c_ln_g, c_ln_b, c_w_out, d_norm, d_w_in, d_b_f, d_w_out, final_norm)))
    m = dict(zip(_WEIGHTS, (m_a_norm, m_a_w_in, m_a_w_out, m_b_norm, m_b_w_in, m_b_v_ln_g, m_b_v_ln_b, m_b_w_s, m_b_b_s, m_b_w_out,
                            m_c_norm, m_c_w_in, m_c_conv_w, m_c_conv_b, m_c_ln_g, m_c_ln_b, m_c_w_out, m_d_norm, m_d_w_in, m_d_b_f,
                            m_d_w_out, m_final_norm)))
    v = dict(zip(_WEIGHTS, (v_a_norm, v_a_w_in, v_a_w_out, v_b_norm, v_b_w_in, v_b_v_ln_g, v_b_v_ln_b, v_b_w_s, v_b_b_s, v_b_w_out,
                            v_c_norm, v_c_w_in, v_c_conv_w, v_c_conv_b, v_c_ln_g, v_c_ln_b, v_c_w_out, v_d_norm, v_d_w_in, v_d_b_f,
                            v_d_w_out, v_final_norm)))
    loss_part, grad_x, g, early = _local_step(x, loss_target, *_gather_weights(local), _start_reduce)
    loss = lax.psum(loss_part, ("x", "y", "c"))
    grads = _reduce_grads(g, local, early)
    grads, delta, new_m, new_v = _update(local, grads, m, v)
    return (loss, grad_x, *[grads[n] for n in _WEIGHTS], *[delta[n] for n in _WEIGHTS],
            *[new_m[n] for n in _WEIGHTS], *[new_v[n] for n in _WEIGHTS])
```
